```python
import jax, jax.numpy as jnp
from jax import lax
import numpy as np

D_MODEL = 1024
BATCH = 8
SEQ = 2048
DEPTH = 4

N_MIXERS = 2
N_A_LAYERS = (DEPTH + 1) // 2
N_B_LAYERS = DEPTH // 2

DN_HEADS = 8
DN_HEAD_DIM = 128
DN_KEY = DN_HEADS * DN_HEAD_DIM
DN_VAL = DN_HEADS * DN_HEAD_DIM
DN_QKV = 2 * DN_KEY + DN_VAL
DN_IN = DN_QKV + DN_VAL + 2 * DN_HEADS
DN_CONV = 4
DN_CHUNK = 64

CF_CH = D_MODEL
CF_KERNEL = 31

FF_DIM = 4 * D_MODEL

ALPHA = (2.0 * DEPTH) ** 0.25
BETA_INIT = (8.0 * DEPTH) ** -0.25
N_MOD = 6
LN_EPS = 1e-5
RMS_EPS = 1e-6
L2_EPS = 1e-6

kernel_name = "hybrid_gdn_conformer_deepnorm_adaln"


def layer_norm(x, g, b):
    xf = x.astype(jnp.float32)
    mu = jnp.mean(xf, axis=-1, keepdims=True)
    var = jnp.mean(jnp.square(xf - mu), axis=-1, keepdims=True)
    return ((xf - mu) * lax.rsqrt(var + LN_EPS) * g + b).astype(x.dtype)


def l2_normalize(x):
    xf = x.astype(jnp.float32)
    return xf * lax.rsqrt(jnp.sum(xf * xf, axis=-1, keepdims=True) + L2_EPS)


def causal_depthwise_conv(x, w):
    k, ch = w.shape
    return lax.conv_general_dilated(
        x, w.astype(x.dtype)[:, None, :], window_strides=(1,), padding=((k - 1, 0),),
        dimension_numbers=('NWC', 'WIO', 'NWC'), feature_group_count=ch)


def gated_delta_rule(q, k, v, g, beta):
    b, s, h, dk = q.shape
    dv = v.shape[-1]
    c = DN_CHUNK
    n = s // c
    f32 = jnp.float32

    def chunks(t):
        t = t.astype(f32).reshape((b, n, c, h) + t.shape[3:])
        return jnp.moveaxis(t, 3, 1)

    q, k, v, g, beta = chunks(q), chunks(k), chunks(v), chunks(g), chunks(beta)
    q = q * (dk ** -0.5)
    gam = jnp.cumsum(g, axis=-1)
    causal = jnp.tril(jnp.ones((c, c), dtype=bool))
    strict = jnp.tril(jnp.ones((c, c), dtype=bool), -1)
    diff = gam[..., :, None] - gam[..., None, :]
    decay = jnp.exp(jnp.where(causal, diff, -jnp.inf))

    kb = k * beta[..., None]
    a_kk = jnp.where(strict, jnp.einsum('bhncd,bhnsd->bhncs', kb, k) * decay, 0.0)
    eye = jnp.eye(c, dtype=f32)
    t_inv = lax.linalg.triangular_solve(eye + a_kk, jnp.broadcast_to(eye, a_kk.shape),
                                        left_side=True, lower=True)
    u = jnp.matmul(t_inv, v * beta[..., None])
    w = jnp.matmul(t_inv, kb * jnp.exp(gam)[..., None])
    a_qk = jnp.einsum('bhncd,bhnsd->bhncs', q, k) * decay
    q_dec = q * jnp.exp(gam)[..., None]
    k_dec = k * jnp.exp(gam[..., -1:] - gam)[..., None]
    g_last = jnp.exp(gam[..., -1])

    def step(state, xs):
        u_i, w_i, a_i, qd_i, kd_i, gl_i = xs
        v_new = u_i - jnp.einsum('bhck,bhkv->bhcv', w_i, state)
        o_i = (jnp.einsum('bhck,bhkv->bhcv', qd_i, state)
               + jnp.einsum('bhcs,bhsv->bhcv', a_i, v_new))
        state = state * gl_i[..., None, None] + jnp.einsum('bhck,bhcv->bhkv', kd_i, v_new)
        return state, o_i

    xs = tuple(jnp.moveaxis(t, 2, 0) for t in (u, w, a_qk, q_dec, k_dec, g_last))
    s0 = jnp.zeros((b, h, dk, dv), f32)
    _, o = lax.scan(step, s0, xs)
    return jnp.transpose(o, (1, 0, 3, 2, 4)).reshape(b, s, h, dv)


def deltanet_mixer(h, w_in, conv_w, a_log, dt_bias, norm_w, w_out):
    b, s, _ = h.shape
    proj = h @ w_in
    qkv, z, bt, at = jnp.split(proj, [DN_QKV, DN_QKV + DN_VAL, DN_QKV + DN_VAL + DN_HEADS], axis=-1)
    qkv = jax.nn.silu(causal_depthwise_conv(qkv, conv_w))
    q, k, v = jnp.split(qkv, [DN_KEY, 2 * DN_KEY], axis=-1)
    q = l2_normalize(q.reshape(b, s, DN_HEADS, DN_HEAD_DIM))
    k = l2_normalize(k.reshape(b, s, DN_HEADS, DN_HEAD_DIM))
    v = v.reshape(b, s, DN_HEADS, DN_HEAD_DIM)
    beta = jax.nn.sigmoid(bt.astype(jnp.float32))
    g = -jnp.exp(a_log.astype(jnp.float32)) * jax.nn.softplus(at.astype(jnp.float32) + dt_bias)
    o = gated_delta_rule(q, k, v, g, beta)
    o = o * lax.rsqrt(jnp.mean(o * o, axis=-1, keepdims=True) + RMS_EPS) * norm_w
    o = o * jax.nn.silu(z.reshape(b, s, DN_HEADS, DN_HEAD_DIM).astype(jnp.float32))
    return o.reshape(b, s, DN_VAL).astype(h.dtype) @ w_out


def conformer_conv_mixer(h, w_in, dw_w, dw_b, ln_g, ln_b, w_out):
    val, gate = jnp.split(h @ w_in, 2, axis=-1)
    u = val * jax.nn.sigmoid(gate)
    u = causal_depthwise_conv(u, dw_w) + dw_b
    u = jax.nn.silu(layer_norm(u, ln_g, ln_b))
    return u @ w_out


def sq_relu_mlp(h, w1, w2):
    return jnp.square(jax.nn.relu(h @ w1)) @ w2


def _fwd_setup_inputs(seed: int = 0) -> dict:
    key = jax.random.key(seed)
    ks = jax.random.split(key, 24)
    nrm = jax.random.normal
    f32 = jnp.float32
    x = nrm(ks[0], (BATCH, SEQ, D_MODEL), f32)
    c = nrm(ks[1], (BATCH, D_MODEL), f32)
    ada_w = nrm(ks[2], (DEPTH, D_MODEL, N_MOD * D_MODEL), f32) * (0.1 * D_MODEL ** -0.5)
    ada_b = nrm(ks[3], (DEPTH, N_MOD * D_MODEL), f32) * 0.01
    ln_g = 1.0 + 0.01 * nrm(ks[4], (DEPTH, 2, D_MODEL), f32)
    ln_b = 0.01 * nrm(ks[5], (DEPTH, 2, D_MODEL), f32)
    dn_w_in = nrm(ks[6], (N_A_LAYERS, D_MODEL, DN_IN), f32) * D_MODEL ** -0.5
    dn_conv_w = nrm(ks[7], (N_A_LAYERS, DN_CONV, DN_QKV), f32) * DN_CONV ** -0.5
    dn_a_log = jnp.log(jax.random.uniform(ks[8], (N_A_LAYERS, DN_HEADS), f32, 1.0, 16.0))
    dt = jnp.exp(jax.random.uniform(ks[9], (N_A_LAYERS, DN_HEADS), f32, float(np.log(1e-3)), float(np.log(1e-1))))
    dn_dt_bias = dt + jnp.log(-jnp.expm1(-dt))
    dn_norm_w = 1.0 + 0.01 * nrm(ks[10], (N_A_LAYERS, DN_HEAD_DIM), f32)
    dn_w_out = nrm(ks[11], (N_A_LAYERS, DN_VAL, D_MODEL), f32) * (BETA_INIT * DN_VAL ** -0.5)
    cf_w_in = nrm(ks[12], (N_B_LAYERS, D_MODEL, 2 * CF_CH), f32) * D_MODEL ** -0.5
    cf_dw_w = nrm(ks[13], (N_B_LAYERS, CF_KERNEL, CF_CH), f32) * CF_KERNEL ** -0.5
    cf_dw_b = 0.01 * nrm(ks[14], (N_B_LAYERS, CF_CH), f32)
    cf_ln_g = 1.0 + 0.01 * nrm(ks[15], (N_B_LAYERS, CF_CH), f32)
    cf_ln_b = 0.01 * nrm(ks[16], (N_B_LAYERS, CF_CH), f32)
    cf_w_out = nrm(ks[17], (N_B_LAYERS, CF_CH, D_MODEL), f32) * (BETA_INIT * CF_CH ** -0.5)
    ff_w1 = nrm(ks[18], (DEPTH, D_MODEL, FF_DIM), f32) * D_MODEL ** -0.5
    ff_w2 = nrm(ks[19], (DEPTH, FF_DIM, D_MODEL), f32) * (BETA_INIT * FF_DIM ** -0.5)
    return {"x": x, "c": c, "ada_w": ada_w, "ada_b": ada_b, "ln_g": ln_g, "ln_b": ln_b,
            "dn_w_in": dn_w_in, "dn_conv_w": dn_conv_w, "dn_a_log": dn_a_log,
            "dn_dt_bias": dn_dt_bias, "dn_norm_w": dn_norm_w, "dn_w_out": dn_w_out,
            "cf_w_in": cf_w_in, "cf_dw_w": cf_dw_w, "cf_dw_b": cf_dw_b, "cf_ln_g": cf_ln_g,
            "cf_ln_b": cf_ln_b, "cf_w_out": cf_w_out, "ff_w1": ff_w1, "ff_w2": ff_w2}


def _fwd_reference(x, c, ada_w, ada_b, ln_g, ln_b, dn_w_in, dn_conv_w, dn_a_log, dn_dt_bias,
              dn_norm_w, dn_w_out, cf_w_in, cf_dw_w, cf_dw_b, cf_ln_g, cf_ln_b, cf_w_out,
              ff_w1, ff_w2):
    cond = jax.nn.silu(c)
    for i in range(DEPTH):
        mod = cond @ ada_w[i] + ada_b[i]
        sh1, sc1, gt1, sh2, sc2, gt2 = [m[:, None, :] for m in jnp.split(mod, N_MOD, axis=-1)]
        h = x * (1.0 + sc1) + sh1
        j = i // N_MIXERS
        if i % N_MIXERS == 0:
            y = deltanet_mixer(h, dn_w_in[j], dn_conv_w[j], dn_a_log[j], dn_dt_bias[j],
                               dn_norm_w[j], dn_w_out[j])
        else:
            y = conformer_conv_mixer(h, cf_w_in[j], cf_dw_w[j], cf_dw_b[j], cf_ln_g[j],
                                     cf_ln_b[j], cf_w_out[j])
        x = layer_norm(ALPHA * x + (1.0 + gt1) * y, ln_g[i, 0], ln_b[i, 0])
        h = x * (1.0 + sc2) + sh2
        x = layer_norm(ALPHA * x + (1.0 + gt2) * sq_relu_mlp(h, ff_w1[i], ff_w2[i]), ln_g[i, 1], ln_b[i, 1])
    return x


import jax as _jax
import jax.numpy as _jnp

TWIN_FORMAT = 'train_step'
FWD_PARAMS = ['x', 'c', 'ada_w', 'ada_b', 'ln_g', 'ln_b', 'dn_w_in', 'dn_conv_w', 'dn_a_log', 'dn_dt_bias', 'dn_norm_w', 'dn_w_out', 'cf_w_in', 'cf_dw_w', 'cf_dw_b', 'cf_ln_g', 'cf_ln_b', 'cf_w_out', 'ff_w1', 'ff_w2']
TWIN_WEIGHTS = ['ada_w', 'ada_b', 'ln_g', 'ln_b', 'dn_w_in', 'dn_conv_w', 'dn_a_log', 'dn_dt_bias', 'dn_norm_w', 'dn_w_out', 'cf_w_in', 'cf_dw_w', 'cf_dw_b', 'cf_ln_g', 'cf_ln_b', 'cf_w_out', 'ff_w1', 'ff_w2']
TWIN_DIFF_INPUT = 'x'
TWIN_INPUTS = ['x', 'c', 'ada_w', 'ada_b', 'ln_g', 'ln_b', 'dn_w_in', 'dn_conv_w', 'dn_a_log', 'dn_dt_bias', 'dn_norm_w', 'dn_w_out', 'cf_w_in', 'cf_dw_w', 'cf_dw_b', 'cf_ln_g', 'cf_ln_b', 'cf_w_out', 'ff_w1', 'ff_w2', 'loss_target', 'm_ada_w', 'm_ada_b', 'm_ln_g', 'm_ln_b', 'm_dn_w_in', 'm_dn_conv_w', 'm_dn_a_log', 'm_dn_dt_bias', 'm_dn_norm_w', 'm_dn_w_out', 'm_cf_w_in', 'm_cf_dw_w', 'm_cf_dw_b', 'm_cf_ln_g', 'm_cf_ln_b', 'm_cf_w_out', 'm_ff_w1', 'm_ff_w2', 'v_ada_w', 'v_ada_b', 'v_ln_g', 'v_ln_b', 'v_dn_w_in', 'v_dn_conv_w', 'v_dn_a_log', 'v_dn_dt_bias', 'v_dn_norm_w', 'v_dn_w_out', 'v_cf_w_in', 'v_cf_dw_w', 'v_cf_dw_b', 'v_cf_ln_g', 'v_cf_ln_b', 'v_cf_w_out', 'v_ff_w1', 'v_ff_w2']
TWIN_OUTPUTS = ['loss', 'grad_x', 'grad_ada_w', 'grad_ada_b', 'grad_ln_g', 'grad_ln_b', 'grad_dn_w_in', 'grad_dn_conv_w', 'grad_dn_a_log', 'grad_dn_dt_bias', 'grad_dn_norm_w', 'grad_dn_w_out', 'grad_cf_w_in', 'grad_cf_dw_w', 'grad_cf_dw_b', 'grad_cf_ln_g', 'grad_cf_ln_b', 'grad_cf_w_out', 'grad_ff_w1', 'grad_ff_w2', 'delta_ada_w', 'delta_ada_b', 'delta_ln_g', 'delta_ln_b', 'delta_dn_w_in', 'delta_dn_conv_w', 'delta_dn_a_log', 'delta_dn_dt_bias', 'delta_dn_norm_w', 'delta_dn_w_out', 'delta_cf_w_in', 'delta_cf_dw_w', 'delta_cf_dw_b', 'delta_cf_ln_g', 'delta_cf_ln_b', 'delta_cf_w_out', 'delta_ff_w1', 'delta_ff_w2', 'new_m_ada_w', 'new_m_ada_b', 'new_m_ln_g', 'new_m_ln_b', 'new_m_dn_w_in', 'new_m_dn_conv_w', 'new_m_dn_a_log', 'new_m_dn_dt_bias', 'new_m_dn_norm_w', 'new_m_dn_w_out', 'new_m_cf_w_in', 'new_m_cf_dw_w', 'new_m_cf_dw_b', 'new_m_cf_ln_g', 'new_m_cf_ln_b', 'new_m_cf_w_out', 'new_m_ff_w1', 'new_m_ff_w2', 'new_v_ada_w', 'new_v_ada_b', 'new_v_ln_g', 'new_v_ln_b', 'new_v_dn_w_in', 'new_v_dn_conv_w', 'new_v_dn_a_log', 'new_v_dn_dt_bias', 'new_v_dn_norm_w', 'new_v_dn_w_out', 'new_v_cf_w_in', 'new_v_cf_dw_w', 'new_v_cf_dw_b', 'new_v_cf_ln_g', 'new_v_cf_ln_b', 'new_v_cf_w_out', 'new_v_ff_w1', 'new_v_ff_w2']
TWIN_LEAF_KINDS = {'loss': 'loss', 'grad_x': 'grad_x', 'grad_ada_w': 'grad_w', 'grad_ada_b': 'grad_w', 'grad_ln_g': 'grad_w', 'grad_ln_b': 'grad_w', 'grad_dn_w_in': 'grad_w', 'grad_dn_conv_w': 'grad_w', 'grad_dn_a_log': 'grad_w', 'grad_dn_dt_bias': 'grad_w', 'grad_dn_norm_w': 'grad_w', 'grad_dn_w_out': 'grad_w', 'grad_cf_w_in': 'grad_w', 'grad_cf_dw_w': 'grad_w', 'grad_cf_dw_b': 'grad_w', 'grad_cf_ln_g': 'grad_w', 'grad_cf_ln_b': 'grad_w', 'grad_cf_w_out': 'grad_w', 'grad_ff_w1': 'grad_w', 'grad_ff_w2': 'grad_w', 'delta_ada_w': 'delta_w', 'delta_ada_b': 'delta_w', 'delta_ln_g': 'delta_w', 'delta_ln_b': 'delta_w', 'delta_dn_w_in': 'delta_w', 'delta_dn_conv_w': 'delta_w', 'delta_dn_a_log': 'delta_w', 'delta_dn_dt_bias': 'delta_w', 'delta_dn_norm_w': 'delta_w', 'delta_dn_w_out': 'delta_w', 'delta_cf_w_in': 'delta_w', 'delta_cf_dw_w': 'delta_w', 'delta_cf_dw_b': 'delta_w', 'delta_cf_ln_g': 'delta_w', 'delta_cf_ln_b': 'delta_w', 'delta_cf_w_out': 'delta_w', 'delta_ff_w1': 'delta_w', 'delta_ff_w2': 'delta_w', 'new_m_ada_w': 'new_m', 'new_m_ada_b': 'new_m', 'new_m_ln_g': 'new_m', 'new_m_ln_b': 'new_m', 'new_m_dn_w_in': 'new_m', 'new_m_dn_conv_w': 'new_m', 'new_m_dn_a_log': 'new_m', 'new_m_dn_dt_bias': 'new_m', 'new_m_dn_norm_w': 'new_m', 'new_m_dn_w_out': 'new_m', 'new_m_cf_w_in': 'new_m', 'new_m_cf_dw_w': 'new_m', 'new_m_cf_dw_b': 'new_m', 'new_m_cf_ln_g': 'new_m', 'new_m_cf_ln_b': 'new_m', 'new_m_cf_w_out': 'new_m', 'new_m_ff_w1': 'new_m', 'new_m_ff_w2': 'new_m', 'new_v_ada_w': 'new_v', 'new_v_ada_b': 'new_v', 'new_v_ln_g': 'new_v', 'new_v_ln_b': 'new_v', 'new_v_dn_w_in': 'new_v', 'new_v_dn_conv_w': 'new_v', 'new_v_dn_a_log': 'new_v', 'new_v_dn_dt_bias': 'new_v', 'new_v_dn_norm_w': 'new_v', 'new_v_dn_w_out': 'new_v', 'new_v_cf_w_in': 'new_v', 'new_v_cf_dw_w': 'new_v', 'new_v_cf_dw_b': 'new_v', 'new_v_cf_ln_g': 'new_v', 'new_v_cf_ln_b': 'new_v', 'new_v_cf_w_out': 'new_v', 'new_v_ff_w1': 'new_v', 'new_v_ff_w2': 'new_v'}


def _forward(args):
    return _fwd_reference(*[args[k] for k in FWD_PARAMS])


def _output_shape():
    out = _jax.eval_shape(lambda: _forward(_fwd_setup_inputs(0)))
    return out.shape, out.dtype

N_MICROBATCH = 1
ADAM_LR = 0.001
ADAM_B1 = 0.9
ADAM_B2 = 0.999
ADAM_EPS = 1e-08
ADAM_WD = 0.01
ADAM_STEP = 10
PER_EXAMPLE_BATCH_AXIS = {'x': 0, 'c': 0, 'loss_target': 0}
SHARED_INPUTS = []
_WEIGHT_DTYPES = {'ada_w': _jnp.float32, 'ada_b': _jnp.float32, 'ln_g': _jnp.float32, 'ln_b': _jnp.float32, 'dn_w_in': _jnp.float32, 'dn_conv_w': _jnp.float32, 'dn_a_log': _jnp.float32, 'dn_dt_bias': _jnp.float32, 'dn_norm_w': _jnp.float32, 'dn_w_out': _jnp.float32, 'cf_w_in': _jnp.float32, 'cf_dw_w': _jnp.float32, 'cf_dw_b': _jnp.float32, 'cf_ln_g': _jnp.float32, 'cf_ln_b': _jnp.float32, 'cf_w_out': _jnp.float32, 'ff_w1': _jnp.float32, 'ff_w2': _jnp.float32}
MOMENT_SCALE = {'ada_w': 2.220311e-02, 'ada_b': 3.805942e-02, 'ln_g': 5.677228e+00, 'ln_b': 1.363643e+00, 'dn_w_in': 1.636471e-02, 'dn_conv_w': 1.524953e-02, 'dn_a_log': 6.362662e-02, 'dn_dt_bias': 6.049272e-02, 'dn_norm_w': 5.593577e-02, 'dn_w_out': 4.617113e-02, 'cf_w_in': 1.529935e-02, 'cf_dw_w': 1.994526e-02, 'cf_dw_b': 4.327698e-02, 'cf_ln_g': 2.621655e-02, 'cf_ln_b': 2.574794e-02, 'cf_w_out': 4.820760e-02, 'ff_w1': 2.211726e-02, 'ff_w2': 9.696644e-02}


def _to_microbatches(a, axis):
    t = _jnp.moveaxis(a, axis, 0)
    t = t.reshape((N_MICROBATCH, t.shape[0] // N_MICROBATCH) + t.shape[1:])
    return _jnp.moveaxis(t, 1, axis + 1)


def setup_inputs(seed: int = 0) -> dict:
    inp = _fwd_setup_inputs(seed)
    key = _jax.random.fold_in(_jax.random.key(seed), 7919)
    shape, _ = _output_shape()
    out = dict(inp)
    out["loss_target"] = _jax.random.normal(_jax.random.fold_in(key, 0), shape, _jnp.float32)
    for i, name in enumerate(TWIN_WEIGHTS):
        w = inp[name].astype(_jnp.float32)
        if MOMENT_SCALE is None:
            s = _jnp.sqrt(_jnp.mean(_jnp.square(w)) + 1e-30)
        else:
            s = MOMENT_SCALE[name]
        km, kv = _jax.random.split(_jax.random.fold_in(key, i + 1))
        out[name] = w
        out["m_" + name] = s * _jax.random.normal(km, w.shape, _jnp.float32)
        out["v_" + name] = (s * s) * _jax.random.uniform(kv, w.shape, _jnp.float32, 0.5, 1.5)
    if N_MICROBATCH > 1:
        for name, axis in PER_EXAMPLE_BATCH_AXIS.items():
            out[name] = _to_microbatches(out[name], axis)
    return {'x': out['x'], 'c': out['c'], 'ada_w': out['ada_w'], 'ada_b': out['ada_b'], 'ln_g': out['ln_g'], 'ln_b': out['ln_b'], 'dn_w_in': out['dn_w_in'], 'dn_conv_w': out['dn_conv_w'], 'dn_a_log': out['dn_a_log'], 'dn_dt_bias': out['dn_dt_bias'], 'dn_norm_w': out['dn_norm_w'], 'dn_w_out': out['dn_w_out'], 'cf_w_in': out['cf_w_in'], 'cf_dw_w': out['cf_dw_w'], 'cf_dw_b': out['cf_dw_b'], 'cf_ln_g': out['cf_ln_g'], 'cf_ln_b': out['cf_ln_b'], 'cf_w_out': out['cf_w_out'], 'ff_w1': out['ff_w1'], 'ff_w2': out['ff_w2'], 'loss_target': out['loss_target'], 'm_ada_w': out['m_ada_w'], 'm_ada_b': out['m_ada_b'], 'm_ln_g': out['m_ln_g'], 'm_ln_b': out['m_ln_b'], 'm_dn_w_in': out['m_dn_w_in'], 'm_dn_conv_w': out['m_dn_conv_w'], 'm_dn_a_log': out['m_dn_a_log'], 'm_dn_dt_bias': out['m_dn_dt_bias'], 'm_dn_norm_w': out['m_dn_norm_w'], 'm_dn_w_out': out['m_dn_w_out'], 'm_cf_w_in': out['m_cf_w_in'], 'm_cf_dw_w': out['m_cf_dw_w'], 'm_cf_dw_b': out['m_cf_dw_b'], 'm_cf_ln_g': out['m_cf_ln_g'], 'm_cf_ln_b': out['m_cf_ln_b'], 'm_cf_w_out': out['m_cf_w_out'], 'm_ff_w1': out['m_ff_w1'], 'm_ff_w2': out['m_ff_w2'], 'v_ada_w': out['v_ada_w'], 'v_ada_b': out['v_ada_b'], 'v_ln_g': out['v_ln_g'], 'v_ln_b': out['v_ln_b'], 'v_dn_w_in': out['v_dn_w_in'], 'v_dn_conv_w': out['v_dn_conv_w'], 'v_dn_a_log': out['v_dn_a_log'], 'v_dn_dt_bias': out['v_dn_dt_bias'], 'v_dn_norm_w': out['v_dn_norm_w'], 'v_dn_w_out': out['v_dn_w_out'], 'v_cf_w_in': out['v_cf_w_in'], 'v_cf_dw_w': out['v_cf_dw_w'], 'v_cf_dw_b': out['v_cf_dw_b'], 'v_cf_ln_g': out['v_cf_ln_g'], 'v_cf_ln_b': out['v_cf_ln_b'], 'v_cf_w_out': out['v_cf_w_out'], 'v_ff_w1': out['v_ff_w1'], 'v_ff_w2': out['v_ff_w2']}


def _loss(weights, diff, rest, loss_target):
    with _jax.named_scope("forward"):
        args = {**rest, TWIN_DIFF_INPUT: diff, **{k: w.astype(_WEIGHT_DTYPES[k]) for k, w in weights.items()}}
        y = _forward(args)
    with _jax.named_scope("loss_head"):
        err = _jnp.square(y.astype(_jnp.float32) - loss_target)
        return 0.5 * _jnp.sum(_jnp.mean(err, axis=-1)) if err.ndim else 0.5 * err


def _adamw(w, g, m, v):
    m = ADAM_B1 * m + (1.0 - ADAM_B1) * g
    v = ADAM_B2 * v + (1.0 - ADAM_B2) * _jnp.square(g)
    m_hat = m / (1.0 - ADAM_B1 ** ADAM_STEP)
    v_hat = v / (1.0 - ADAM_B2 ** ADAM_STEP)
    delta = -ADAM_LR * (m_hat / (_jnp.sqrt(v_hat) + ADAM_EPS) + ADAM_WD * w)
    return delta, m, v


def reference(x, c, ada_w, ada_b, ln_g, ln_b, dn_w_in, dn_conv_w, dn_a_log, dn_dt_bias, dn_norm_w, dn_w_out, cf_w_in, cf_dw_w, cf_dw_b, cf_ln_g, cf_ln_b, cf_w_out, ff_w1, ff_w2, loss_target, m_ada_w, m_ada_b, m_ln_g, m_ln_b, m_dn_w_in, m_dn_conv_w, m_dn_a_log, m_dn_dt_bias, m_dn_norm_w, m_dn_w_out, m_cf_w_in, m_cf_dw_w, m_cf_dw_b, m_cf_ln_g, m_cf_ln_b, m_cf_w_out, m_ff_w1, m_ff_w2, v_ada_w, v_ada_b, v_ln_g, v_ln_b, v_dn_w_in, v_dn_conv_w, v_dn_a_log, v_dn_dt_bias, v_dn_norm_w, v_dn_w_out, v_cf_w_in, v_cf_dw_w, v_cf_dw_b, v_cf_ln_g, v_cf_ln_b, v_cf_w_out, v_ff_w1, v_ff_w2):
    given = dict(x=x, c=c, ada_w=ada_w, ada_b=ada_b, ln_g=ln_g, ln_b=ln_b, dn_w_in=dn_w_in, dn_conv_w=dn_conv_w, dn_a_log=dn_a_log, dn_dt_bias=dn_dt_bias, dn_norm_w=dn_norm_w, dn_w_out=dn_w_out, cf_w_in=cf_w_in, cf_dw_w=cf_dw_w, cf_dw_b=cf_dw_b, cf_ln_g=cf_ln_g, cf_ln_b=cf_ln_b, cf_w_out=cf_w_out, ff_w1=ff_w1, ff_w2=ff_w2, loss_target=loss_target, m_ada_w=m_ada_w, m_ada_b=m_ada_b, m_ln_g=m_ln_g, m_ln_b=m_ln_b, m_dn_w_in=m_dn_w_in, m_dn_conv_w=m_dn_conv_w, m_dn_a_log=m_dn_a_log, m_dn_dt_bias=m_dn_dt_bias, m_dn_norm_w=m_dn_norm_w, m_dn_w_out=m_dn_w_out, m_cf_w_in=m_cf_w_in, m_cf_dw_w=m_cf_dw_w, m_cf_dw_b=m_cf_dw_b, m_cf_ln_g=m_cf_ln_g, m_cf_ln_b=m_cf_ln_b, m_cf_w_out=m_cf_w_out, m_ff_w1=m_ff_w1, m_ff_w2=m_ff_w2, v_ada_w=v_ada_w, v_ada_b=v_ada_b, v_ln_g=v_ln_g, v_ln_b=v_ln_b, v_dn_w_in=v_dn_w_in, v_dn_conv_w=v_dn_conv_w, v_dn_a_log=v_dn_a_log, v_dn_dt_bias=v_dn_dt_bias, v_dn_norm_w=v_dn_norm_w, v_dn_w_out=v_dn_w_out, v_cf_w_in=v_cf_w_in, v_cf_dw_w=v_cf_dw_w, v_cf_dw_b=v_cf_dw_b, v_cf_ln_g=v_cf_ln_g, v_cf_ln_b=v_cf_ln_b, v_cf_w_out=v_cf_w_out, v_ff_w1=v_ff_w1, v_ff_w2=v_ff_w2)
    weights = {n: given[n] for n in TWIN_WEIGHTS}
    shared = {n: given[n] for n in SHARED_INPUTS}
    per_example = {n: given[n] for n in ['x', 'c']}
    grad_fn = _jax.value_and_grad(_loss, argnums=(0, 1))

    def one_microbatch(ex, loss_target):
        ex = dict(ex)
        diff = ex.pop(TWIN_DIFF_INPUT)
        return grad_fn(weights, diff, {**shared, **ex}, loss_target)

    if N_MICROBATCH == 1:
        loss, (grad_w, grad_x) = one_microbatch(per_example, given["loss_target"])
    else:
        def body(carry, xs):
            loss_sum, grad_sum = carry
            l_k, (gw_k, gx_k) = one_microbatch(xs[0], xs[1])
            with _jax.named_scope("update"):
                return (loss_sum + l_k, _jax.tree.map(_jnp.add, grad_sum, gw_k)), gx_k

        init = (_jnp.zeros((), _jnp.float32), _jax.tree.map(_jnp.zeros_like, weights))
        (loss, grad_w), grad_x = _jax.lax.scan(body, init, (per_example, given["loss_target"]))
    with _jax.named_scope("update"):
        delta_w, new_m, new_v = {}, {}, {}
        for n in TWIN_WEIGHTS:
            delta_w[n], new_m[n], new_v[n] = _adamw(weights[n], grad_w[n], given["m_" + n], given["v_" + n])
    return (loss, grad_x, *[grad_w[n] for n in TWIN_WEIGHTS], *[delta_w[n] for n in TWIN_WEIGHTS],
            *[new_m[n] for n in TWIN_WEIGHTS], *[new_v[n] for n in TWIN_WEIGHTS])
```

```python
import functools

import jax
import jax.numpy as jnp
from jax import lax
from jax.experimental import pallas as pl
from jax.experimental.pallas import tpu as pltpu

F32 = jnp.float32
MXU_DTYPE = jnp.bfloat16
N_DEV = 8
LANES = 128
HEAD_DIM = 128
CHUNK = 64
DN_CONV = 4
N_MOD = 6
LN_EPS = 1e-5
RMS_EPS = 1e-6
L2_EPS = 1e-6
ADAM_LR = 0.001
ADAM_B1 = 0.9
ADAM_B2 = 0.999
ADAM_EPS = 1e-08
ADAM_WD = 0.01
ADAM_STEP = 10

HI = lax.Precision.HIGHEST
NN = ((1,), (0,))
NT = ((1,), (1,))
TN = ((0,), (0,))

ROW_TILE = 256
CONV_TILE = 256


def _dot(a, b, dims=NN):
    return lax.dot_general(a, b, (dims, ((), ())), precision=HI, preferred_element_type=F32)


def _cparams(n):
    return pltpu.CompilerParams(dimension_semantics=("arbitrary",) * n)


def _call(body, *, grid, ins, outs, name, scratch=()):
    res = pl.pallas_call(
        body,
        grid=grid,
        in_specs=[pl.BlockSpec(b, m) for _, b, m in ins],
        out_specs=[pl.BlockSpec(b, m) for _, _, b, m in outs],
        out_shape=[jax.ShapeDtypeStruct(s, d) for s, d, _, _ in outs],
        scratch_shapes=list(scratch),
        name=name,
        compiler_params=_cparams(len(grid)),
    )(*[a for a, _, _ in ins])
    return res


def _tile(n, pref, unit=LANES):
    if n <= pref:
        return n
    t = (pref // unit) * unit
    while t > unit and n % t:
        t -= unit
    assert n % t == 0, (n, pref)
    return t


def _rowmap(fn, rows, consts, row_outs, acc_outs, name):
    rows = [r if isinstance(r, tuple) else (r, r.shape[1], 0) for r in rows]
    s = rows[0][0].shape[0]
    tm = min(ROW_TILE, s)
    nr, nc, no, na = len(rows), len(consts), len(row_outs), len(acc_outs)

    def body(*refs):
        rin, cin = refs[:nr], refs[nr:nr + nc]
        rout, aout = refs[nr + nc:nr + nc + no], refs[nr + nc + no:]
        ro, ao = fn(*[r[...] for r in rin], *[c[...] for c in cin])
        for ref, val in zip(rout, ro):
            ref[...] = val.astype(ref.dtype)
        if na:
            first = pl.program_id(0) == 0

            @pl.when(first)
            def _():
                for ref, val in zip(aout, ao):
                    ref[...] = val

            @pl.when(jnp.logical_not(first))
            def _():
                for ref, val in zip(aout, ao):
                    ref[...] += val

    ins = [(a, (tm, w), functools.partial(lambda i, cb: (i, cb), cb=cb)) for a, w, cb in rows]
    ins += [(c, c.shape, lambda i: (0, 0)) for c in consts]
    outs = [((s, w), d, (tm, w), lambda i: (i, 0)) for w, d in row_outs]
    outs += [(shp, F32, shp, lambda i: (0, 0)) for shp in acc_outs]
    res = _call(body, grid=(s // tm,), ins=ins, outs=outs, name=name)
    return res[:no], res[no:]


def _ln(z, g, b):
    mu = jnp.mean(z, -1, keepdims=True)
    var = jnp.mean(jnp.square(z - mu), -1, keepdims=True)
    return (z - mu) * lax.rsqrt(var + LN_EPS) * g + b


def _combine(alpha, x, y, gt, g, b, sc, sh):
    xn = _ln(alpha * x + (1.0 + gt) * y, g, b)
    return xn, xn * (1.0 + sc) + sh


def _modulate_fwd(x, sc, sh):
    def fn(x, sc, sh):
        return ((x * (1.0 + sc) + sh),), ()

    (h,), _ = _rowmap(fn, [x], [sc, sh], [(x.shape[1], MXU_DTYPE)], [], "modulate_fwd")
    return h


def _modulate_bwd(x, dx, dh, sc, sh):
    d = x.shape[1]

    def fn(x, dx, dh, sc, sh):
        _, vjp = jax.vjp(lambda x, sc, sh: x * (1.0 + sc) + sh, x, sc, sh)
        gx, gsc, gsh = vjp(dh)
        return (dx + gx,), (gsc, gsh)

    (gx,), (gsc, gsh) = _rowmap(fn, [x, dx, dh], [sc, sh], [(d, F32)], [(1, d), (1, d)], "modulate_bwd")
    return gx, gsc, gsh


def _combine_fwd(alpha, x, y, gt, g, b, sc, sh):
    d = x.shape[1]

    def fn(x, y, gt, g, b, sc, sh):
        return _combine(alpha, x, y, gt, g, b, sc, sh), ()

    (xn, h), _ = _rowmap(fn, [x, y], [gt, g, b, sc, sh], [(d, F32), (d, MXU_DTYPE)], [], "combine_fwd")
    return xn, h


def _combine_bwd(alpha, x, y, dxn, dh, gt, g, b, sc, sh):
    d = x.shape[1]

    def fn(x, y, dxn, dh, gt, g, b, sc, sh):
        _, vjp = jax.vjp(functools.partial(_combine, alpha), x, y, gt, g, b, sc, sh)
        gx, gy, ggt, gg, gb, gsc, gsh = vjp((dxn, dh))
        return (gx, gy), (ggt, gg, gb, gsc, gsh)

    (gx, gy), accs = _rowmap(fn, [x, y, dxn, dh], [gt, g, b, sc, sh], [(d, F32), (d, MXU_DTYPE)],
                             [(1, d)] * 5, "combine_bwd")
    return gx, gy, accs


def _last_fwd_bwd(alpha, x, y, tgt, gt, g, b):
    d = x.shape[1]

    def fn(x, y, tgt, gt, g, b):
        xn, vjp = jax.vjp(lambda x, y, gt, g, b: _ln(alpha * x + (1.0 + gt) * y, g, b), x, y, gt, g, b)
        err = xn - tgt
        gx, gy, ggt, gg, gb = vjp(err * (1.0 / d))
        rows = jnp.sum(jnp.square(err), axis=-1, keepdims=True)
        loss = (0.5 / d) * jnp.sum(rows, axis=0, keepdims=True) * jnp.ones((1, LANES), F32)
        return (gx, gy), (loss, ggt, gg, gb)

    (gx, gy), accs = _rowmap(fn, [x, y, tgt], [gt, g, b], [(d, F32), (d, MXU_DTYPE)],
                             [(1, LANES), (1, d), (1, d), (1, d)], "last_fwd_bwd")
    return gx, gy, accs


def _mm_call(a, a_blk, a_map, b, b_blk, b_map, outs, dims, grid, name, relu2=False):
    nk = grid[2]
    n_out = len(outs)

    def body(a_ref, b_ref, *rest):
        out_refs = rest[:n_out]

        def finish(val):
            if relu2:
                out_refs[0][...] = val.astype(out_refs[0].dtype)
                out_refs[1][...] = jnp.square(jnp.maximum(val, 0.0)).astype(out_refs[1].dtype)
            else:
                out_refs[0][...] = val.astype(out_refs[0].dtype)

        p = lax.dot_general(a_ref[...], b_ref[...], (dims, ((), ())), preferred_element_type=F32)
        if nk == 1:
            finish(p)
        else:
            acc = rest[n_out]
            k = pl.program_id(2)

            @pl.when(k == 0)
            def _():
                acc[...] = p

            @pl.when(k > 0)
            def _():
                acc[...] += p

            @pl.when(k == nk - 1)
            def _():
                finish(acc[...])

    out_blk = tuple(x for x in outs[0][2] if x is not None)
    scratch = [pltpu.VMEM(out_blk, F32)] if nk > 1 else []
    return _call(body, grid=grid, ins=[(a, a_blk, a_map), (b, b_blk, b_map)], outs=outs, name=name, scratch=scratch)


def _mm_nn(a, b, out_dtype, name, b_layer=None, relu2=False):
    m, kdim = a.shape
    tm = _tile(m, 512, 8)
    if b_layer is None:
        n = b.shape[1]
        tn, tk = _tile(n, 512), _tile(kdim, 1024)
        b_blk, b_map = (tk, tn), lambda i, j, k: (k, j)
    elif b_layer[1] == "cols":
        g, _, _, ng = b.shape
        n = g * ng
        tn, tk = _tile(ng, 512), _tile(kdim, 1024)
        npg = ng // tn
        b_blk = (None, None, tk, tn)
        b_map = functools.partial(lambda i, j, k, l, npg: (j // npg, l, k, j % npg), l=b_layer[0], npg=npg)
    else:
        g, _, kg, n = b.shape
        tn, tk = _tile(n, 512), _tile(kg, 1024)
        kpg = kg // tk
        b_blk = (None, None, tk, tn)
        b_map = functools.partial(lambda i, j, k, l, kpg: (k // kpg, l, k % kpg, j), l=b_layer[0], kpg=kpg)
    grid = (m // tm, n // tn, kdim // tk)
    outs = [((m, n), F32 if relu2 else out_dtype, (tm, tn), lambda i, j, k: (i, j))]
    if relu2:
        outs.append(((m, n), out_dtype, (tm, tn), lambda i, j, k: (i, j)))
    res = _mm_call(a, (tm, tk), lambda i, j, k: (i, k), b, b_blk, b_map, outs, NN, grid, name, relu2)
    return res if relu2 else res[0]


def _mm_nt(a, b, out_dtype, name, b_layer=None):
    m, n = a.shape
    tm = _tile(m, 512, 8)
    if b_layer is None:
        kout = b.shape[0]
        to, tc = _tile(kout, 512), _tile(n, 1024)
        b_blk, b_map = (to, tc), lambda i, j, k: (j, k)
    elif b_layer[1] == "cols":
        g, _, kout, ng = b.shape
        to, tc = _tile(kout, 512), _tile(ng, 1024)
        cpg = ng // tc
        b_blk = (None, None, to, tc)
        b_map = functools.partial(lambda i, j, k, l, cpg: (k // cpg, l, j, k % cpg), l=b_layer[0], cpg=cpg)
    else:
        g, _, kg, _ = b.shape
        kout = g * kg
        to, tc = _tile(kg, 512), _tile(n, 1024)
        opg = kg // to
        b_blk = (None, None, to, tc)
        b_map = functools.partial(lambda i, j, k, l, opg: (j // opg, l, j % opg, k), l=b_layer[0], opg=opg)
    grid = (m // tm, kout // to, n // tc)
    outs = [((m, kout), out_dtype, (tm, to), lambda i, j, k: (i, j))]
    return _mm_call(a, (tm, tc), lambda i, j, k: (i, k), b, b_blk, b_map, outs, NT, grid, name)[0]


def _mm_tn(a, b, out_dtype, name, split=None):
    m, kdim = a.shape
    n = b.shape[1]
    tc = _tile(m, 2048, 8)
    if split is None:
        tk, tn = _tile(kdim, 512), _tile(n, 512)
        out = ((kdim, n), out_dtype, (tk, tn), lambda i, j, k: (i, j))
    elif split == "cols":
        ng = n // N_DEV
        tk, tn = _tile(kdim, 512), _tile(ng, 512)
        npg = ng // tn
        out = ((N_DEV, kdim, ng), out_dtype, (None, tk, tn),
               functools.partial(lambda i, j, k, npg: (j // npg, i, j % npg), npg=npg))
    else:
        kg = kdim // N_DEV
        tk, tn = _tile(kg, 512), _tile(n, 512)
        kpg = kg // tk
        out = ((N_DEV, kg, n), out_dtype, (None, tk, tn),
               functools.partial(lambda i, j, k, kpg: (i // kpg, i % kpg, j), kpg=kpg))
    grid = (kdim // tk, n // tn, m // tc)
    return _mm_call(a, (tc, tk), lambda i, j, k: (k, i), b, (tc, tn), lambda i, j, k: (k, j), [out], TN, grid, name)[0]


def _shifted(xa, off, rows):
    if off % 8 == 0:
        return xa[off:off + rows]
    return pltpu.roll(xa, xa.shape[0] - off, 0)[:rows]


def _conv_pad(taps):
    return -(-(taps - 1) // 8) * 8


def _conv_tile(xp_ref, w, i, rows, taps):
    pad = _conv_pad(taps)
    r0 = pl.multiple_of(i * rows, rows)
    xa = xp_ref[pl.ds(r0, rows + pad), :]
    views = [_shifted(xa, pad - (taps - 1) + j, rows) for j in range(taps)]
    acc = w[0:1, :] * views[0]
    for j in range(1, taps):
        acc = acc + w[j:j + 1, :] * views[j]
    return r0, acc, views


def _conv_back_tile(yp_ref, w, i, rows, taps):
    pad = _conv_pad(taps)
    r0 = pl.multiple_of(i * rows, rows)
    ya = yp_ref[pl.ds(r0, rows + pad), :]
    acc = w[taps - 1:taps, :] * ya[:rows]
    for j in range(taps - 1):
        acc = acc + w[j:j + 1, :] * _shifted(ya, taps - 1 - j, rows)
    return r0, acc


def _tap_sums(dy, views, taps):
    row = lax.broadcasted_iota(jnp.int32, (taps, LANES), 0)
    acc = jnp.zeros((taps, LANES), F32)
    for j in range(taps):
        acc = acc + jnp.where(row == j, jnp.sum(dy * views[j], axis=0, keepdims=True), 0.0)
    return acc


def _silu_l2(xc, l2):
    a = jax.nn.silu(xc)
    if l2:
        a = a * lax.rsqrt(jnp.sum(a * a, axis=-1, keepdims=True) + L2_EPS)
    return a


def _dn_conv_fwd(proj, conv_w, c0, nblk, l2, name):
    s = proj.shape[0]
    pad = _conv_pad(DN_CONV)
    rows = min(CONV_TILE, s)

    def body(x_ref, w_ref, o_ref, xp):
        xp[0:pad, :] = jnp.zeros((pad, LANES), F32)
        xp[pad:, :] = x_ref[...]
        w = w_ref[...]

        def tile(i, c):
            r0, acc, _ = _conv_tile(xp, w, i, rows, DN_CONV)
            o_ref[pl.ds(r0, rows), :] = _silu_l2(acc, l2)
            return c

        lax.fori_loop(0, s // rows, tile, 0)

    return _call(body, grid=(nblk,),
                 ins=[(proj, (s, LANES), lambda c: (0, c0 + c)), (conv_w, (DN_CONV, LANES), lambda c: (0, c0 + c))],
                 outs=[((nblk, s, LANES), F32, (None, s, LANES), lambda c: (c, 0, 0))],
                 name=name, scratch=[pltpu.VMEM((s + pad, LANES), F32)])[0]


def _dn_conv_bwd(proj, conv_w, da, c0, nblk, l2, name):
    s = proj.shape[0]
    pad = _conv_pad(DN_CONV)
    rows = min(CONV_TILE, s)

    def body(x_ref, w_ref, da_ref, dx_ref, dw_ref, xp, yp):
        xp[0:pad, :] = jnp.zeros((pad, LANES), F32)
        xp[pad:, :] = x_ref[...]
        yp[s:, :] = jnp.zeros((pad, LANES), F32)
        w = w_ref[...]

        def tile(i, dw):
            r0, acc, views = _conv_tile(xp, w, i, rows, DN_CONV)
            _, vjp = jax.vjp(functools.partial(_silu_l2, l2=l2), acc)
            (dxc,) = vjp(da_ref[pl.ds(r0, rows), :])
            yp[pl.ds(r0, rows), :] = dxc
            return dw + _tap_sums(dxc, views, DN_CONV)

        dw_ref[...] = lax.fori_loop(0, s // rows, tile, jnp.zeros((DN_CONV, LANES), F32))

        def tile2(i, c):
            r0, acc = _conv_back_tile(yp, w, i, rows, DN_CONV)
            dx_ref[pl.ds(r0, rows), :] = acc.astype(dx_ref.dtype)
            return c

        lax.fori_loop(0, s // rows, tile2, 0)

    return _call(body, grid=(nblk,),
                 ins=[(proj, (s, LANES), lambda c: (0, c0 + c)), (conv_w, (DN_CONV, LANES), lambda c: (0, c0 + c)),
                      (da, (None, s, LANES), lambda c: (c, 0, 0))],
                 outs=[((s, nblk * LANES), MXU_DTYPE, (s, LANES), lambda c: (0, c)),
                       ((DN_CONV, nblk * LANES), F32, (DN_CONV, LANES), lambda c: (0, c))],
                 name=name, scratch=[pltpu.VMEM((s + pad, LANES), F32), pltpu.VMEM((s + pad, LANES), F32)])


def _cf_conv_fwd(vg, dw_w, dw_b):
    s, c2 = vg.shape
    ch = c2 // 2
    nblk = ch // LANES
    taps = dw_w.shape[0]
    pad = _conv_pad(taps)
    rows = min(CONV_TILE, s)

    def body(v_ref, g_ref, w_ref, b_ref, o_ref, xp):
        xp[0:pad, :] = jnp.zeros((pad, LANES), F32)
        xp[pad:, :] = v_ref[...] * jax.nn.sigmoid(g_ref[...])
        w = w_ref[...]
        bias = b_ref[...]

        def tile(i, c):
            r0, acc, _ = _conv_tile(xp, w, i, rows, taps)
            o_ref[pl.ds(r0, rows), :] = acc + bias
            return c

        lax.fori_loop(0, s // rows, tile, 0)

    return _call(body, grid=(nblk,),
                 ins=[(vg, (s, LANES), lambda c: (0, c)), (vg, (s, LANES), lambda c: (0, nblk + c)),
                      (dw_w, (taps, LANES), lambda c: (0, c)), (dw_b, (1, LANES), lambda c: (0, c))],
                 outs=[((s, ch), F32, (s, LANES), lambda c: (0, c))],
                 name="cf_conv_fwd", scratch=[pltpu.VMEM((s + pad, LANES), F32)])[0]


def _cf_conv_bwd(vg, dw_w, du):
    s, c2 = vg.shape
    ch = c2 // 2
    nblk = ch // LANES
    taps = dw_w.shape[0]
    pad = _conv_pad(taps)
    rows = min(CONV_TILE, s)

    def body(v_ref, g_ref, w_ref, du_ref, dv_ref, dg_ref, dw_ref, db_ref, xp, yp):
        sig = jax.nn.sigmoid(g_ref[...])
        xp[0:pad, :] = jnp.zeros((pad, LANES), F32)
        xp[pad:, :] = v_ref[...] * sig
        yp[0:s, :] = du_ref[...]
        yp[s:, :] = jnp.zeros((pad, LANES), F32)
        w = w_ref[...]
        db_ref[...] = jnp.sum(du_ref[...], axis=0, keepdims=True)

        def tile(i, dw):
            r0, _, views = _conv_tile(xp, w, i, rows, taps)
            return dw + _tap_sums(du_ref[pl.ds(r0, rows), :], views, taps)

        dw_ref[...] = lax.fori_loop(0, s // rows, tile, jnp.zeros((taps, LANES), F32))

        def tile2(i, c):
            r0, du0 = _conv_back_tile(yp, w, i, rows, taps)
            val = v_ref[pl.ds(r0, rows), :]
            sg = jax.nn.sigmoid(g_ref[pl.ds(r0, rows), :])
            dv_ref[pl.ds(r0, rows), :] = (du0 * sg).astype(dv_ref.dtype)
            dg_ref[pl.ds(r0, rows), :] = (du0 * val * sg * (1.0 - sg)).astype(dg_ref.dtype)
            return c

        lax.fori_loop(0, s // rows, tile2, 0)

    return _call(body, grid=(nblk,),
                 ins=[(vg, (s, LANES), lambda c: (0, c)), (vg, (s, LANES), lambda c: (0, nblk + c)),
                      (dw_w, (taps, LANES), lambda c: (0, c)), (du, (s, LANES), lambda c: (0, c))],
                 outs=[((s, ch), MXU_DTYPE, (s, LANES), lambda c: (0, c)),
                       ((s, ch), MXU_DTYPE, (s, LANES), lambda c: (0, c)),
                       ((taps, ch), F32, (taps, LANES), lambda c: (0, c)),
                       ((1, ch), F32, (1, LANES), lambda c: (0, c))],
                 name="cf_conv_bwd", scratch=[pltpu.VMEM((s + pad, LANES), F32), pltpu.VMEM((s + pad, LANES), F32)])


def _masks():
    r = lax.broadcasted_iota(jnp.int32, (CHUNK, CHUNK), 0)
    c = lax.broadcasted_iota(jnp.int32, (CHUNK, CHUNK), 1)
    return r >= c, r > c, r <= c


def _tri_inv(a):
    r = lax.broadcasted_iota(jnp.int32, (CHUNK, CHUNK), 0)
    c = lax.broadcasted_iota(jnp.int32, (CHUNK, CHUNK), 1)
    p = jnp.where(r == c, 1.0, 0.0).astype(F32) - a
    q = _dot(a, a)
    n = 2
    while True:
        p = p + _dot(p, q)
        n *= 2
        if n >= CHUNK:
            return p
        q = _dot(q, q)


def _chunk_decay(g):
    causal, _, upper = _masks()
    gb = jnp.broadcast_to(g, (CHUNK, CHUNK))
    gam_r = _dot(jnp.where(causal, 1.0, 0.0).astype(F32), gb)
    gam_s = _dot(jnp.ones((CHUNK, CHUNK), F32), jnp.where(upper, gb, 0.0))
    dm = jnp.where(causal, jnp.exp(jnp.where(causal, gam_r - gam_s, 0.0)), 0.0)
    return gam_r[:, 0:1], dm


def _chunk_prep(q, k, v, beta, g):
    _, strict, _ = _masks()
    gam, dm = _chunk_decay(g)
    eg = jnp.exp(gam)
    kb = k * beta
    a = jnp.where(strict, _dot(kb, k, NT) * dm, 0.0)
    t = _tri_inv(a)
    u = _dot(t, v * beta)
    w = _dot(t, kb * eg)
    aqk = _dot(q * (HEAD_DIM ** -0.5), k, NT) * dm
    return u, w, aqk, t, gam


def _chunk_prep_bwd(q, k, v, beta, g, t, du, dw, daqk, dqd, dkd, dgl):
    causal, strict, _ = _masks()
    scale = HEAD_DIM ** -0.5
    gam, dm = _chunk_decay(g)
    eg = jnp.exp(gam)
    gam_last = gam[CHUNK - 1:CHUNK, :]
    rr = jnp.exp(gam_last - gam)
    kb = k * beta
    qs = q * scale
    kk = _dot(kb, k, NT)
    a = jnp.where(strict, kk * dm, 0.0)
    aqk = _dot(qs, k, NT) * dm
    vb = v * beta
    kbe = kb * eg

    dt = _dot(du, vb, NT) + _dot(dw, kbe, NT)
    dvb = _dot(t, du, TN)
    dkbe = _dot(t, dw, TN)
    da = jnp.where(strict, -_dot(_dot(t, dt, TN), t, NT), 0.0)
    dkk = da * dm
    dqk = daqk * dm
    ddiff = da * a + daqk * aqk
    dkb = _dot(dkk, k) + dkbe * eg
    dk = _dot(dkk, kb, TN) + _dot(dqk, qs, TN) + dkb * beta + dkd * rr
    dq = (_dot(dqk, k) + dqd * eg) * scale
    dbeta = jnp.sum(dkb * k, axis=-1, keepdims=True) + jnp.sum(dvb * v, axis=-1, keepdims=True)
    dv = dvb * beta
    deg = jnp.sum(dkbe * kb, axis=-1, keepdims=True) + jnp.sum(dqd * qs, axis=-1, keepdims=True)
    drr = jnp.sum(dkd * k, axis=-1, keepdims=True)
    colsum = _dot(ddiff, jnp.ones((CHUNK, LANES), F32), TN)[:, 0:1]
    dgam = deg * eg - drr * rr + jnp.sum(ddiff, axis=-1, keepdims=True) - colsum
    dgam_last = jnp.sum(drr * rr, axis=0, keepdims=True) + dgl[0:1, :] * jnp.exp(gam_last)
    row = lax.broadcasted_iota(jnp.int32, (CHUNK, 1), 0)
    dgam = dgam + jnp.where(row == CHUNK - 1, dgam_last, 0.0)
    dg = _dot(jnp.where(causal, 1.0, 0.0).astype(F32), jnp.broadcast_to(dgam, (CHUNK, LANES)), TN)[:, 0:1]
    return dq, dk, dv, dbeta, dg


def _prep_group(s):
    nch = s // CHUNK
    return 4 if nch % 4 == 0 else 1


def _dn_prep(q, k, v, beta, g):
    h, s, _ = q.shape
    cb = _prep_group(s)
    rb = cb * CHUNK

    def body(q_ref, k_ref, v_ref, b_ref, g_ref, u_ref, w_ref, a_ref, t_ref, gam_ref):
        for i in range(cb):
            sl = slice(i * CHUNK, (i + 1) * CHUNK)
            u, w, aqk, t, gam = _chunk_prep(q_ref[sl, :], k_ref[sl, :], v_ref[sl, :], b_ref[sl, :], g_ref[sl, :])
            u_ref[sl, :] = u
            w_ref[sl, :] = w
            a_ref[sl, :] = aqk
            t_ref[sl, :] = t
            gam_ref[sl, :] = gam

    big = lambda x: (x, (None, rb, HEAD_DIM), lambda hh, n: (hh, n, 0))
    col = lambda x: (x, (None, rb, 1), lambda hh, n: (hh, n, 0))
    o_big = ((h, s, HEAD_DIM), F32, (None, rb, HEAD_DIM), lambda hh, n: (hh, n, 0))
    o_sq = ((h, s, CHUNK), F32, (None, rb, CHUNK), lambda hh, n: (hh, n, 0))
    o_col = ((h, s, 1), F32, (None, rb, 1), lambda hh, n: (hh, n, 0))
    return _call(body, grid=(h, s // rb), ins=[big(q), big(k), big(v), col(beta), col(g)],
                 outs=[o_big, o_big, o_sq, o_sq, o_col], name="dn_prep")


def _dn_prep_bwd(q, k, v, beta, g, t, du, dw, daqk, dqd, dkd, dgl):
    h, s, _ = q.shape
    cb = _prep_group(s)
    rb = cb * CHUNK

    def body(q_ref, k_ref, v_ref, b_ref, g_ref, t_ref, du_ref, dw_ref, da_ref, dqd_ref, dkd_ref, dgl_ref,
             dq_ref, dk_ref, dv_ref, db_ref, dg_ref):
        for i in range(cb):
            sl = slice(i * CHUNK, (i + 1) * CHUNK)
            dq, dk, dv, dbeta, dg = _chunk_prep_bwd(
                q_ref[sl, :], k_ref[sl, :], v_ref[sl, :], b_ref[sl, :], g_ref[sl, :], t_ref[sl, :],
                du_ref[sl, :], dw_ref[sl, :], da_ref[sl, :], dqd_ref[sl, :], dkd_ref[sl, :], dgl_ref[sl, :])
            dq_ref[sl, :] = dq
            dk_ref[sl, :] = dk
            dv_ref[sl, :] = dv
            db_ref[sl, :] = dbeta
            dg_ref[sl, :] = dg

    big = lambda x: (x, (None, rb, HEAD_DIM), lambda hh, n: (hh, n, 0))
    sq = lambda x: (x, (None, rb, CHUNK), lambda hh, n: (hh, n, 0))
    col = lambda x: (x, (None, rb, 1), lambda hh, n: (hh, n, 0))
    o_big = ((h, s, HEAD_DIM), F32, (None, rb, HEAD_DIM), lambda hh, n: (hh, n, 0))
    o_col = ((h, s, 1), F32, (None, rb, 1), lambda hh, n: (hh, n, 0))
    return _call(body, grid=(h, s // rb),
                 ins=[big(q), big(k), big(v), col(beta), col(g), sq(t), big(du), big(dw), sq(daqk), big(dqd), big(dkd),
                      col(dgl)],
                 outs=[o_big, o_big, o_big, o_col, o_col], name="dn_prep_bwd")


def _chunk_scaled(q, k, gam):
    gam_last = gam[CHUNK - 1:CHUNK, :]
    q_dec = q * (HEAD_DIM ** -0.5) * jnp.exp(gam)
    k_dec = k * jnp.exp(gam_last - gam)
    return q_dec, k_dec, jnp.exp(gam_last)


def _dn_scan(q, k, u, w, aqk, gam):
    h, s, _ = q.shape
    nch = s // CHUNK

    def body(q_ref, k_ref, u_ref, w_ref, a_ref, gam_ref, o_ref, st_ref, state):
        @pl.when(pl.program_id(0) == 0)
        def _():
            state[...] = jnp.zeros_like(state)

        for hh in range(h):
            s0 = state[hh]
            st_ref[hh] = s0
            q_dec, k_dec, gl = _chunk_scaled(q_ref[hh], k_ref[hh], gam_ref[hh])
            v_new = u_ref[hh] - _dot(w_ref[hh], s0)
            o_ref[:, hh * HEAD_DIM:(hh + 1) * HEAD_DIM] = _dot(q_dec, s0) + _dot(a_ref[hh], v_new)
            state[hh] = s0 * gl + _dot(k_dec, v_new, TN)

    big = lambda x: (x, (h, CHUNK, HEAD_DIM), lambda n: (0, n, 0))
    return _call(body, grid=(nch,),
                 ins=[big(q), big(k), big(u), big(w), (aqk, (h, CHUNK, CHUNK), lambda n: (0, n, 0)),
                      (gam, (h, CHUNK, 1), lambda n: (0, n, 0))],
                 outs=[((s, h * HEAD_DIM), F32, (CHUNK, h * HEAD_DIM), lambda n: (n, 0)),
                       ((nch, h, HEAD_DIM, HEAD_DIM), F32, (None, h, HEAD_DIM, HEAD_DIM), lambda n: (n, 0, 0, 0))],
                 name="dn_scan", scratch=[pltpu.VMEM((h, HEAD_DIM, HEAD_DIM), F32)])


def _dn_scan_bwd(q, k, u, w, aqk, gam, states, do):
    h, s, _ = q.shape
    nch = s // CHUNK

    def body(q_ref, k_ref, u_ref, w_ref, a_ref, gam_ref, st_ref, do_ref,
             du_ref, dw_ref, da_ref, dqd_ref, dkd_ref, dgl_ref, dstate):
        @pl.when(pl.program_id(0) == 0)
        def _():
            dstate[...] = jnp.zeros_like(dstate)

        for hh in range(h):
            s0 = st_ref[hh]
            ds = dstate[hh]
            doh = do_ref[:, hh * HEAD_DIM:(hh + 1) * HEAD_DIM]
            wv = w_ref[hh]
            av = a_ref[hh]
            q_dec, k_dec, gl = _chunk_scaled(q_ref[hh], k_ref[hh], gam_ref[hh])
            v_new = u_ref[hh] - _dot(wv, s0)
            dv_new = _dot(av, doh, TN) + _dot(k_dec, ds)
            du_ref[hh] = dv_new
            dw_ref[hh] = -_dot(dv_new, s0, NT)
            da_ref[hh] = _dot(doh, v_new, NT)
            dqd_ref[hh] = _dot(doh, s0, NT)
            dkd_ref[hh] = _dot(v_new, ds, NT)
            tot = jnp.sum(jnp.sum(s0 * ds, axis=-1, keepdims=True), axis=0, keepdims=True)
            dgl_ref[hh] = jnp.broadcast_to(tot, (CHUNK, 1))
            dstate[hh] = ds * gl + _dot(q_dec, doh, TN) - _dot(wv, dv_new, TN)

    rev = lambda n: (0, nch - 1 - n, 0)
    big = lambda x: (x, (h, CHUNK, HEAD_DIM), rev)
    o_big = ((h, s, HEAD_DIM), F32, (h, CHUNK, HEAD_DIM), rev)
    return _call(body, grid=(nch,),
                 ins=[big(q), big(k), big(u), big(w), (aqk, (h, CHUNK, CHUNK), rev), (gam, (h, CHUNK, 1), rev),
                      (states, (None, h, HEAD_DIM, HEAD_DIM), lambda n: (nch - 1 - n, 0, 0, 0)),
                      (do, (CHUNK, h * HEAD_DIM), lambda n: (nch - 1 - n, 0))],
                 outs=[o_big, o_big, ((h, s, CHUNK), F32, (h, CHUNK, CHUNK), rev), o_big, o_big,
                       ((h, s, 1), F32, (h, CHUNK, 1), rev)],
                 name="dn_scan_bwd", scratch=[pltpu.VMEM((h, HEAD_DIM, HEAD_DIM), F32)])


def _gates(x, a_log, dt_b, h):
    lane = lax.broadcasted_iota(jnp.int32, x.shape, 1)
    return jnp.where(lane < h, jax.nn.sigmoid(x), -jnp.exp(a_log) * jax.nn.softplus(x + dt_b))


def _head_out(oh, zh, nw):
    on = oh * lax.rsqrt(jnp.mean(oh * oh, axis=-1, keepdims=True) + RMS_EPS) * nw
    return on * jax.nn.silu(zh)


def _to_heads(x, h):
    return jnp.transpose(x[:, :h])[:, :, None]


def _pad_lanes(x, lo):
    return jnp.zeros((1, LANES), F32).at[0, lo:lo + x.shape[0]].set(x)


def _deltanet_fwd(hin, w_in, conv_w, a_log, dt_bias, norm_w, w_out):
    h = a_log.shape[0]
    hw = h * HEAD_DIM
    proj = _mm_nn(hin, w_in, F32, "dn_proj")
    q = _dn_conv_fwd(proj, conv_w, 0, h, True, "dn_conv_q")
    k = _dn_conv_fwd(proj, conv_w, h, h, True, "dn_conv_k")
    v = _dn_conv_fwd(proj, conv_w, 2 * h, h, False, "dn_conv_v")
    alp, dtp = _pad_lanes(a_log, h), _pad_lanes(dt_bias, h)

    def gates_fn(x, al, db):
        return (_gates(x, al, db, h),), ()

    (bg,), _ = _rowmap(gates_fn, [(proj, LANES, 4 * h)], [alp, dtp], [(LANES, F32)], [], "dn_gates")
    beta, g = _to_heads(bg, h), _to_heads(bg[:, h:], h)
    u, w, aqk, t, gam = _dn_prep(q, k, v, beta, g)
    o, states = _dn_scan(q, k, u, w, aqk, gam)
    nw = norm_w[None, :]

    def out_fn(o, z, nw):
        parts = [_head_out(o[:, i * HEAD_DIM:(i + 1) * HEAD_DIM], z[:, i * HEAD_DIM:(i + 1) * HEAD_DIM], nw)
                 for i in range(h)]
        return (jnp.concatenate(parts, axis=-1),), ()

    (og,), _ = _rowmap(out_fn, [o, (proj, hw, 3)], [nw], [(hw, MXU_DTYPE)], [], "dn_out")
    y = _mm_nn(og, w_out, F32, "dn_y")
    return y, (hin, proj, q, k, v, beta, g, u, w, aqk, t, gam, states, o, og, alp, dtp, nw)


def _deltanet_bwd(res, dy, w_in, conv_w, w_out):
    hin, proj, q, k, v, beta, g, u, w, aqk, t, gam, states, o, og, alp, dtp, nw = res
    h = q.shape[0]
    hw = h * HEAD_DIM
    s = hin.shape[0]
    d_w_out = _mm_tn(og, dy, MXU_DTYPE, "dn_dwout", split="rows")
    dog = _mm_nt(dy, w_out, F32, "dn_dog")

    def out_bwd(o, z, dog, nw):
        dos, dzs = [], []
        dn = jnp.zeros((1, HEAD_DIM), F32)
        for i in range(h):
            sl = slice(i * HEAD_DIM, (i + 1) * HEAD_DIM)
            _, vjp = jax.vjp(_head_out, o[:, sl], z[:, sl], nw)
            a, b, c = vjp(dog[:, sl])
            dos.append(a)
            dzs.append(b)
            dn = dn + c
        return (jnp.concatenate(dos, axis=-1), jnp.concatenate(dzs, axis=-1)), (dn,)

    (do, dz), (d_norm_w,) = _rowmap(out_bwd, [o, (proj, hw, 3), dog], [nw], [(hw, F32), (hw, MXU_DTYPE)],
                                    [(1, HEAD_DIM)], "dn_out_bwd")
    du, dw, daqk, dqd, dkd, dgl = _dn_scan_bwd(q, k, u, w, aqk, gam, states, do)
    dq, dk, dv, dbeta, dg = _dn_prep_bwd(q, k, v, beta, g, t, du, dw, daqk, dqd, dkd, dgl)
    dpq, dwq = _dn_conv_bwd(proj, conv_w, dq, 0, h, True, "dn_conv_q_bwd")
    dpk, dwk = _dn_conv_bwd(proj, conv_w, dk, h, h, True, "dn_conv_k_bwd")
    dpv, dwv = _dn_conv_bwd(proj, conv_w, dv, 2 * h, h, False, "dn_conv_v_bwd")
    dbg = jnp.concatenate([jnp.transpose(dbeta[:, :, 0]), jnp.transpose(dg[:, :, 0]),
                           jnp.zeros((s, LANES - 2 * h), F32)], axis=1)

    def gates_bwd(x, dbg, al, db):
        _, vjp = jax.vjp(functools.partial(_gates, h=h), x, al, db)
        gx, gal, gdb = vjp(dbg)
        return (gx,), (gal, gdb)

    (dba,), (d_alp, d_dtp) = _rowmap(gates_bwd, [(proj, LANES, 4 * h), dbg], [alp, dtp], [(LANES, MXU_DTYPE)],
                                     [(1, LANES), (1, LANES)], "dn_gates_bwd")
    dproj = jnp.concatenate([dpq, dpk, dpv, dz, dba], axis=1)
    d_w_in = _mm_tn(hin, dproj, MXU_DTYPE, "dn_dwin")
    dh = _mm_nt(dproj, w_in, F32, "dn_dh")
    d_conv_w = jnp.concatenate([dwq, dwk, dwv], axis=1)
    return dh, dict(w_in=d_w_in, w_out=d_w_out, conv_w=d_conv_w, a_log=d_alp[0, h:2 * h], dt_bias=d_dtp[0, h:2 * h],
                    norm_w=d_norm_w[0])


def _ln_silu(u, g, b):
    return jax.nn.silu(_ln(u, g, b))


def _conformer_fwd(hin, w_in, dw_w, dw_b, ln_g, ln_b, w_out):
    vg = _mm_nn(hin, w_in, F32, "cf_vg")
    u1 = _cf_conv_fwd(vg, dw_w, dw_b)
    ch = u1.shape[1]

    def fn(u, g, b):
        return (_ln_silu(u, g, b),), ()

    (u2,), _ = _rowmap(fn, [u1], [ln_g, ln_b], [(ch, MXU_DTYPE)], [], "cf_ln")
    y = _mm_nn(u2, w_out, F32, "cf_y")
    return y, (hin, vg, u1, u2)


def _conformer_bwd(res, dy, w_in, dw_w, ln_g, ln_b, w_out):
    hin, vg, u1, u2 = res
    ch = u1.shape[1]
    d_w_out = _mm_tn(u2, dy, MXU_DTYPE, "cf_dwout", split="rows")
    du2 = _mm_nt(dy, w_out, F32, "cf_du2")

    def fn(u, du2, g, b):
        _, vjp = jax.vjp(_ln_silu, u, g, b)
        gu, gg, gb = vjp(du2)
        return (gu,), (gg, gb)

    (du1,), (d_ln_g, d_ln_b) = _rowmap(fn, [u1, du2], [ln_g, ln_b], [(ch, F32)], [(1, ch), (1, ch)], "cf_ln_bwd")
    dval, dgate, d_dw_w, d_dw_b = _cf_conv_bwd(vg, dw_w, du1)
    dvg = jnp.concatenate([dval, dgate], axis=1)
    d_w_in = _mm_tn(hin, dvg, MXU_DTYPE, "cf_dwin", split="cols")
    dh = _mm_nt(dvg, w_in, F32, "cf_dh")
    return dh, dict(w_in=d_w_in, w_out=d_w_out, dw_w=d_dw_w, dw_b=d_dw_b[0], ln_g=d_ln_g[0], ln_b=d_ln_b[0])


def _mlp_fwd(hin, w1g, w2g, layer):
    a, r = _mm_nn(hin, w1g, MXU_DTYPE, "ff_a", b_layer=(layer, "cols"), relu2=True)
    m = _mm_nn(r, w2g, F32, "ff_m", b_layer=(layer, "rows"))
    return m, (hin, a, r)


def _mlp_bwd(res, dm, w1g, w2g, layer):
    hin, a, r = res
    f = a.shape[1]
    d_w2 = _mm_tn(r, dm, MXU_DTYPE, "ff_dw2", split="rows")
    dr = _mm_nt(dm, w2g, F32, "ff_dr", b_layer=(layer, "rows"))

    def fn(a, dr):
        return ((dr * 2.0 * jnp.maximum(a, 0.0)),), ()

    (da,), _ = _rowmap(fn, [a, dr], [], [(f, MXU_DTYPE)], [], "ff_da")
    d_w1 = _mm_tn(hin, da, MXU_DTYPE, "ff_dw1", split="cols")
    dh = _mm_nt(da, w1g, F32, "ff_dh", b_layer=(layer, "cols"))
    return dh, d_w1, d_w2


def _ada_fwd(c_all, ada_w):
    depth, d, nl = ada_w.shape
    tn = _tile(nl, 256)

    def body(c_ref, w_ref, o_ref, cond_ref):
        cond = jax.nn.silu(c_ref[...]).astype(MXU_DTYPE)
        cond_ref[...] = cond
        o_ref[...] = lax.dot_general(cond, w_ref[...].astype(MXU_DTYPE), (NN, ((), ())), preferred_element_type=F32)

    return _call(body, grid=(depth, nl // tn),
                 ins=[(c_all, c_all.shape, lambda l, j: (0, 0)), (ada_w, (None, d, tn), lambda l, j: (l, 0, j))],
                 outs=[((depth, N_DEV, nl), F32, (None, N_DEV, tn), lambda l, j: (l, 0, j)),
                       (c_all.shape, MXU_DTYPE, c_all.shape, lambda l, j: (0, 0))],
                 name="ada_fwd")


def _ada_bwd(cond_all, dmod_cols):
    depth, _, nl = dmod_cols.shape
    d = cond_all.shape[1]
    tn = _tile(nl, 256)

    def body(c_ref, g_ref, o_ref):
        o_ref[...] = lax.dot_general(c_ref[...], g_ref[...].astype(MXU_DTYPE), (TN, ((), ())),
                                     preferred_element_type=F32)

    return _call(body, grid=(depth, nl // tn),
                 ins=[(cond_all, cond_all.shape, lambda l, j: (0, 0)), (dmod_cols, (None, N_DEV, tn), lambda l, j: (l, 0, j))],
                 outs=[((depth, d, nl), F32, (None, d, tn), lambda l, j: (l, 0, j))], name="ada_bwd")[0]


def _exchange(arrs, scatter, name):
    nt = len(arrs)
    out_shape = [jax.ShapeDtypeStruct(a.shape if scatter else (N_DEV,) + a.shape, a.dtype) for a in arrs]

    def body(*refs):
        ins, outs = refs[:nt], refs[nt:2 * nt]
        send, recv, loc = refs[2 * nt:]
        x, y, c = lax.axis_index("x"), lax.axis_index("y"), lax.axis_index("c")
        me = 4 * x + 2 * y + c
        copies = []
        for t in range(nt):
            own = pltpu.make_async_copy(ins[t].at[me] if scatter else ins[t], outs[t].at[me], loc.at[t])
            own.start()
            copies.append(own)
            for k in range(1, N_DEV):
                px = 1 - x if k & 4 else x
                py = 1 - y if k & 2 else y
                pc = 1 - c if k & 1 else c
                peer = 4 * px + 2 * py + pc
                cp = pltpu.make_async_remote_copy(
                    src_ref=ins[t].at[peer] if scatter else ins[t], dst_ref=outs[t].at[me],
                    send_sem=send.at[t, k - 1], recv_sem=recv.at[t, k - 1],
                    device_id=(px, py, pc), device_id_type=pl.DeviceIdType.MESH)
                cp.start()
                copies.append(cp)
        for cp in copies:
            cp.wait()

    any_spec = pl.BlockSpec(memory_space=pl.ANY)
    return pl.pallas_call(
        body, out_shape=out_shape, in_specs=[any_spec] * nt, out_specs=[any_spec] * nt,
        scratch_shapes=[pltpu.SemaphoreType.DMA((nt, N_DEV - 1)), pltpu.SemaphoreType.DMA((nt, N_DEV - 1)),
                        pltpu.SemaphoreType.DMA((nt,))],
        name=name)(*arrs)


def _adamw(parts, w, m, v, name):
    p, nl, r, c = parts.shape
    tr = _tile(r, 256, 8)

    def body(p_ref, w_ref, m_ref, v_ref, g_out, d_out, m_out, v_out):
        g = p_ref[0].astype(F32)
        for i in range(1, p):
            g = g + p_ref[i].astype(F32)
        m2 = ADAM_B1 * m_ref[...] + (1.0 - ADAM_B1) * g
        v2 = ADAM_B2 * v_ref[...] + (1.0 - ADAM_B2) * jnp.square(g)
        m_hat = m2 / (1.0 - ADAM_B1 ** ADAM_STEP)
        v_hat = v2 / (1.0 - ADAM_B2 ** ADAM_STEP)
        g_out[...] = g
        d_out[...] = -ADAM_LR * (m_hat / (jnp.sqrt(v_hat) + ADAM_EPS) + ADAM_WD * w_ref[...])
        m_out[...] = m2
        v_out[...] = v2

    blk = (None, tr, c)
    imap = lambda l, i: (l, i, 0)
    out = ((nl, r, c), F32, blk, imap)
    return _call(body, grid=(nl, r // tr),
                 ins=[(parts, (p, None, tr, c), lambda l, i: (0, l, i, 0)), (w, blk, imap), (m, blk, imap), (v, blk, imap)],
                 outs=[out] * 4, name=name)


def _rows(x):
    return x.reshape(-1, LANES)


def _pad_rows(x, mult=8):
    r = x.shape[0]
    extra = (-r) % mult
    return jnp.pad(x, ((0, extra), (0, 0))) if extra else x


def _shard_cols(x, me, groups):
    lead = x.shape[:-1]
    xr = x.reshape(lead + (N_DEV, groups * LANES))
    xs = lax.dynamic_index_in_dim(xr, me, axis=len(lead), keepdims=False)
    return xs.reshape(N_DEV, -1, LANES)


def kernel(x, c, ada_w, ada_b, ln_g, ln_b, dn_w_in, dn_conv_w, dn_a_log, dn_dt_bias, dn_norm_w, dn_w_out, cf_w_in, cf_dw_w, cf_dw_b, cf_ln_g, cf_ln_b, cf_w_out, ff_w1, ff_w2, loss_target, m_ada_w, m_ada_b, m_ln_g, m_ln_b, m_dn_w_in, m_dn_conv_w, m_dn_a_log, m_dn_dt_bias, m_dn_norm_w, m_dn_w_out, m_cf_w_in, m_cf_dw_w, m_cf_dw_b, m_cf_ln_g, m_cf_ln_b, m_cf_w_out, m_ff_w1, m_ff_w2, v_ada_w, v_ada_b, v_ln_g, v_ln_b, v_dn_w_in, v_dn_conv_w, v_dn_a_log, v_dn_dt_bias, v_dn_norm_w, v_dn_w_out, v_cf_w_in, v_cf_dw_w, v_cf_dw_b, v_cf_ln_g, v_cf_ln_b, v_cf_w_out, v_ff_w1, v_ff_w2):
    depth, d, _ = ada_w.shape
    n_a, n_b = dn_w_in.shape[0], cf_w_in.shape[0]
    heads = dn_a_log.shape[1]
    hw = heads * HEAD_DIM
    taps = cf_dw_w.shape[1]
    s = x.shape[1]
    alpha = (2.0 * depth) ** 0.25
    me = 4 * lax.axis_index("x") + 2 * lax.axis_index("y") + lax.axis_index("c")
    xs, tgt = x[0], loss_target[0]

    small_local = [_rows(ln_g), _rows(ln_b), _rows(dn_conv_w), _rows(cf_dw_w), _rows(cf_dw_b), _rows(cf_ln_g),
                   _rows(cf_ln_b), _rows(c)]
    sizes = [a.shape[0] for a in small_local]
    packed = _pad_rows(jnp.concatenate(small_local, axis=0))
    big_local = [w.astype(MXU_DTYPE) for w in (dn_w_in, dn_w_out, cf_w_in, cf_w_out, ff_w1, ff_w2)]
    gathered = _exchange([packed] + big_local, False, "comm_gather_params")
    small_all = gathered[0]
    g_dn_w_in, g_dn_w_out, g_cf_w_in, g_cf_w_out, g_ff_w1, g_ff_w2 = gathered[1:]
    offs = [0]
    for z in sizes:
        offs.append(offs[-1] + z)

    def small(i):
        return small_all[:, offs[i]:offs[i + 1], :]

    def unshard(piece, lead, groups):
        t = piece.reshape((N_DEV,) + lead + (groups * LANES,))
        t = jnp.moveaxis(t, 0, len(lead))
        return t.reshape(lead + (N_DEV * groups * LANES,))

    ln_g_f = unshard(small(0), (depth, 2), 1)
    ln_b_f = unshard(small(1), (depth, 2), 1)
    conv_w_f = unshard(small(2), (n_a, DN_CONV), 3 * heads // N_DEV)
    dw_w_f = unshard(small(3), (n_b, taps), 1)
    dw_b_f = unshard(small(4), (n_b,), 1)
    cf_ln_g_f = unshard(small(5), (n_b,), 1)
    cf_ln_b_f = unshard(small(6), (n_b,), 1)
    c_all = small(7).reshape(N_DEV, d)

    dn_in_cols = dn_w_in.shape[2]
    w_dn_in = jnp.moveaxis(g_dn_w_in, 0, 2).reshape(n_a, d, N_DEV * dn_in_cols)
    w_dn_in = jnp.pad(w_dn_in, ((0, 0), (0, 0), (0, 4 * hw + LANES - N_DEV * dn_in_cols)))
    w_dn_out = jnp.moveaxis(g_dn_w_out, 0, 1).reshape(n_a, hw, d)
    w_cf_in = jnp.moveaxis(g_cf_w_in, 0, 2).reshape(n_b, d, 2 * d)
    w_cf_out = jnp.moveaxis(g_cf_w_out, 0, 1).reshape(n_b, d, d)

    mod_part, cond_all = _ada_fwd(c_all, ada_w)
    (mod_all,) = _exchange([mod_part], False, "comm_gather_mod")
    mod_mine = lax.dynamic_index_in_dim(mod_all, me, axis=2, keepdims=False)
    mod_mine = jnp.moveaxis(mod_mine, 0, 1).reshape(depth, N_MOD * d)

    def add_bias(a, b):
        return (a + b,), ()

    (mod,), _ = _rowmap(add_bias, [mod_mine, ada_b], [], [(N_MOD * d, F32)], [], "ada_bias")

    def mod_row(i, j):
        return mod[i:i + 1, j * d:(j + 1) * d]

    def ln_row(arr, i, j):
        return arr[i, j][None, :]

    subs = []
    h_cur = _modulate_fwd(xs, mod_row(0, 1), mod_row(0, 0))
    x_cur = xs
    last = None
    for i in range(depth):
        j = i // 2
        if i % 2 == 0:
            y, res = _deltanet_fwd(h_cur, w_dn_in[j], conv_w_f[j], dn_a_log[j], dn_dt_bias[j], dn_norm_w[j], w_dn_out[j])
        else:
            y, res = _conformer_fwd(h_cur, w_cf_in[j], dw_w_f[j], dw_b_f[j][None, :], cf_ln_g_f[j][None, :],
                                    cf_ln_b_f[j][None, :], w_cf_out[j])
        p1 = (mod_row(i, 2), ln_row(ln_g_f, i, 0), ln_row(ln_b_f, i, 0), mod_row(i, 4), mod_row(i, 3))
        x_mid, h_mid = _combine_fwd(alpha, x_cur, y, *p1)
        subs.append((x_cur, y, p1, res))
        m_out, res2 = _mlp_fwd(h_mid, g_ff_w1, g_ff_w2, i)
        if i + 1 < depth:
            p2 = (mod_row(i, 5), ln_row(ln_g_f, i, 1), ln_row(ln_b_f, i, 1), mod_row(i + 1, 1), mod_row(i + 1, 0))
            x_next, h_next = _combine_fwd(alpha, x_mid, m_out, *p2)
            subs.append((x_mid, m_out, p2, res2))
            x_cur, h_cur = x_next, h_next
        else:
            p2 = (mod_row(i, 5), ln_row(ln_g_f, i, 1), ln_row(ln_b_f, i, 1))
            last = (x_mid, m_out, p2, res2)

    x_in, y_in, p_last, res_last = last
    dx, dy, (loss_acc, g_gt, g_g, g_b) = _last_fwd_bwd(alpha, x_in, y_in, tgt, *p_last)
    loss = lax.psum(loss_acc[0, 0], ("x", "y", "c"))

    d_mod = [[None] * N_MOD for _ in range(depth)]
    d_ln_g = [[None, None] for _ in range(depth)]
    d_ln_b = [[None, None] for _ in range(depth)]
    d_mod[depth - 1][5], d_ln_g[depth - 1][1], d_ln_b[depth - 1][1] = g_gt, g_g, g_b
    gw = dict(dn=[None] * n_a, cf=[None] * n_b, ff1=[None] * depth, ff2=[None] * depth)

    dh, gw["ff1"][depth - 1], gw["ff2"][depth - 1] = _mlp_bwd(res_last, dy, g_ff_w1, g_ff_w2, depth - 1)
    for idx in range(len(subs) - 1, -1, -1):
        x_in, y_in, prm, res = subs[idx]
        i, second = idx // 2, idx % 2
        dx, dy, (g_gt, g_g, g_b, g_sc, g_sh) = _combine_bwd(alpha, x_in, y_in, dx, dh, *prm)
        d_mod[i][5 if second else 2], d_ln_g[i][second], d_ln_b[i][second] = g_gt, g_g, g_b
        nxt_i, nxt_base = (i + 1, 0) if second else (i, 3)
        d_mod[nxt_i][nxt_base + 1], d_mod[nxt_i][nxt_base] = g_sc, g_sh
        if second:
            dh, gw["ff1"][i], gw["ff2"][i] = _mlp_bwd(res, dy, g_ff_w1, g_ff_w2, i)
        elif i % 2 == 0:
            j = i // 2
            dh, gw["dn"][j] = _deltanet_bwd(res, dy, w_dn_in[j], conv_w_f[j], w_dn_out[j])
        else:
            j = i // 2
            dh, gw["cf"][j] = _conformer_bwd(res, dy, w_cf_in[j], dw_w_f[j], cf_ln_g_f[j][None, :],
                                             cf_ln_b_f[j][None, :], w_cf_out[j])
    grad_x, g_sc, g_sh = _modulate_bwd(xs, dx, dh, mod_row(0, 1), mod_row(0, 0))
    d_mod[0][1], d_mod[0][0] = g_sc, g_sh
    d_mod_full = jnp.concatenate([jnp.concatenate(r, axis=1) for r in d_mod], axis=0)

    def dest_blocks_cols(gfull, cols):
        return jnp.moveaxis(gfull[:, :N_DEV * cols].reshape(gfull.shape[0], N_DEV, cols), 1, 0)

    p_dn_w_in = jnp.stack([dest_blocks_cols(gw["dn"][j]["w_in"], dn_in_cols) for j in range(n_a)], axis=1)
    p_dn_w_out = jnp.stack([gw["dn"][j]["w_out"] for j in range(n_a)], axis=1)
    p_cf_w_in = jnp.stack([gw["cf"][j]["w_in"] for j in range(n_b)], axis=1)
    p_cf_w_out = jnp.stack([gw["cf"][j]["w_out"] for j in range(n_b)], axis=1)
    p_ff_w1 = jnp.stack(gw["ff1"], axis=1)
    p_ff_w2 = jnp.stack(gw["ff2"], axis=1)
    r_dn_w_in, r_dn_w_out, r_cf_w_in, r_cf_w_out, r_ff_w1, r_ff_w2 = _exchange(
        [p_dn_w_in, p_dn_w_out, p_cf_w_in, p_cf_w_out, p_ff_w1, p_ff_w2], True, "comm_scatter_grads")

    def stack_rows(lst):
        return jnp.stack(lst, axis=0)

    gs_ln_g = jnp.stack([jnp.concatenate(r, axis=0) for r in d_ln_g], axis=0)
    gs_ln_b = jnp.stack([jnp.concatenate(r, axis=0) for r in d_ln_b], axis=0)
    gs_conv_w = stack_rows([gw["dn"][j]["conv_w"] for j in range(n_a)])
    gs_dw_w = stack_rows([gw["cf"][j]["dw_w"] for j in range(n_b)])
    gs_dw_b = stack_rows([gw["cf"][j]["dw_b"] for j in range(n_b)])
    gs_cf_ln_g = stack_rows([gw["cf"][j]["ln_g"] for j in range(n_b)])
    gs_cf_ln_b = stack_rows([gw["cf"][j]["ln_b"] for j in range(n_b)])
    gs_a_log = stack_rows([_pad_lanes(gw["dn"][j]["a_log"], 0)[0] for j in range(n_a)])
    gs_dt_bias = stack_rows([_pad_lanes(gw["dn"][j]["dt_bias"], 0)[0] for j in range(n_a)])
    gs_norm_w = stack_rows([gw["dn"][j]["norm_w"] for j in range(n_a)])
    small_grads = [gs_ln_g, gs_ln_b, gs_conv_w, gs_dw_w, gs_dw_b, gs_cf_ln_g, gs_cf_ln_b, gs_a_log, gs_dt_bias,
                   gs_norm_w, d_mod_full]
    sg_rows = [_rows(a) for a in small_grads]
    sg_sizes = [a.shape[0] for a in sg_rows]
    (sg_all,) = _exchange([_pad_rows(jnp.concatenate(sg_rows, axis=0))], False, "comm_gather_small_grads")
    sg_offs = [0]
    for z in sg_sizes:
        sg_offs.append(sg_offs[-1] + z)

    def sg(i, shape):
        return sg_all[:, sg_offs[i]:sg_offs[i + 1], :].reshape((N_DEV,) + shape)

    dmod_all = sg(10, (depth, N_MOD * d))
    nl = ada_w.shape[2]
    dmod_cols = lax.dynamic_slice_in_dim(dmod_all, me * nl, nl, axis=2)
    g_ada_w = _ada_bwd(cond_all, jnp.moveaxis(dmod_cols, 0, 1))

    outs = {}

    def run_adamw(key, parts, w, m, v):
        shp = w.shape
        as3 = lambda t: t.reshape((-1,) + shp[-2:]) if t.ndim >= 3 else t.reshape((1,) + shp)
        parts3 = parts.reshape((parts.shape[0],) + as3(w).shape)
        res = _adamw(parts3, as3(w), as3(m), as3(v), "adamw_" + key)
        outs[key] = tuple(r.reshape(shp) for r in res)

    run_adamw("ada_w", g_ada_w[None], ada_w, m_ada_w, v_ada_w)
    run_adamw("dn_w_in", r_dn_w_in, dn_w_in, m_dn_w_in, v_dn_w_in)
    run_adamw("dn_w_out", r_dn_w_out, dn_w_out, m_dn_w_out, v_dn_w_out)
    run_adamw("cf_w_in", r_cf_w_in, cf_w_in, m_cf_w_in, v_cf_w_in)
    run_adamw("cf_w_out", r_cf_w_out, cf_w_out, m_cf_w_out, v_cf_w_out)
    run_adamw("ff_w1", r_ff_w1, ff_w1, m_ff_w1, v_ff_w1)
    run_adamw("ff_w2", r_ff_w2, ff_w2, m_ff_w2, v_ff_w2)

    cgroups = 3 * heads // N_DEV
    shard_parts = [
        _shard_cols(sg(0, (depth, 2, d)), me, 1), _shard_cols(sg(1, (depth, 2, d)), me, 1),
        _shard_cols(sg(2, (n_a, DN_CONV, 3 * hw)), me, cgroups), _shard_cols(sg(3, (n_b, taps, d)), me, 1),
        _shard_cols(sg(4, (n_b, d)), me, 1), _shard_cols(sg(5, (n_b, d)), me, 1), _shard_cols(sg(6, (n_b, d)), me, 1),
    ]
    repl_parts = [sg(7, (n_a, LANES)), sg(8, (n_a, LANES)), sg(9, (n_a, HEAD_DIM)),
                  sg(10, (depth, N_MOD * d)).reshape(N_DEV, -1, LANES)]
    small_parts = shard_parts + repl_parts
    sp_sizes = [a.shape[1] for a in small_parts]
    parts_packed = jnp.concatenate(small_parts, axis=1)
    extra = (-parts_packed.shape[1]) % 8
    parts_packed = jnp.pad(parts_packed, ((0, 0), (0, extra), (0, 0)))

    def pad_heads(t):
        return jnp.pad(t, ((0, 0), (0, LANES - heads)))

    def pack_state(ln_g_, ln_b_, conv_w_, dw_w_, dw_b_, cln_g_, cln_b_, a_log_, dt_b_, norm_w_, ada_b_):
        rows = [_rows(ln_g_), _rows(ln_b_), _rows(conv_w_), _rows(dw_w_), _rows(dw_b_), _rows(cln_g_), _rows(cln_b_),
                pad_heads(a_log_), pad_heads(dt_b_), norm_w_, _rows(ada_b_)]
        return _pad_rows(jnp.concatenate(rows, axis=0))

    w_s = pack_state(ln_g, ln_b, dn_conv_w, cf_dw_w, cf_dw_b, cf_ln_g, cf_ln_b, dn_a_log, dn_dt_bias, dn_norm_w, ada_b)
    m_s = pack_state(m_ln_g, m_ln_b, m_dn_conv_w, m_cf_dw_w, m_cf_dw_b, m_cf_ln_g, m_cf_ln_b, m_dn_a_log,
                     m_dn_dt_bias, m_dn_norm_w, m_ada_b)
    v_s = pack_state(v_ln_g, v_ln_b, v_dn_conv_w, v_cf_dw_w, v_cf_dw_b, v_cf_ln_g, v_cf_ln_b, v_dn_a_log,
                     v_dn_dt_bias, v_dn_norm_w, v_ada_b)
    res_s = _adamw(parts_packed[:, None], w_s[None], m_s[None], v_s[None], "adamw_small")
    sp_offs = [0]
    for z in sp_sizes:
        sp_offs.append(sp_offs[-1] + z)
    small_keys = ["ln_g", "ln_b", "dn_conv_w", "cf_dw_w", "cf_dw_b", "cf_ln_g", "cf_ln_b", "dn_a_log", "dn_dt_bias",
                  "dn_norm_w", "ada_b"]
    small_shapes = [ln_g.shape, ln_b.shape, dn_conv_w.shape, cf_dw_w.shape, cf_dw_b.shape, cf_ln_g.shape,
                    cf_ln_b.shape, dn_a_log.shape, dn_dt_bias.shape, dn_norm_w.shape, ada_b.shape]
    for n, (key, shp) in enumerate(zip(small_keys, small_shapes)):
        vals = []
        for r in res_s:
            piece = r[0, sp_offs[n]:sp_offs[n + 1], :]
            if key in ("dn_a_log", "dn_dt_bias"):
                piece = piece[:, :heads]
            vals.append(piece.reshape(shp))
        outs[key] = tuple(vals)

    order = ["ada_w", "ada_b", "ln_g", "ln_b", "dn_w_in", "dn_conv_w", "dn_a_log", "dn_dt_bias", "dn_norm_w",
             "dn_w_out", "cf_w_in", "cf_dw_w", "cf_dw_b", "cf_ln_g", "cf_ln_b", "cf_w_out", "ff_w1", "ff_w2"]
    result = [loss, grad_x[None]]
    for part in range(4):
        result += [outs[k][part] for k in order]
    return tuple(result)
```

```python
import functools

import jax
import jax.numpy as jnp
from jax import lax
from jax.experimental import pallas as pl
from jax.experimental.pallas import tpu as pltpu

F32 = jnp.float32
MXU_DTYPE = jnp.bfloat16
N_DEV = 8
LANES = 128
HEAD_DIM = 128
CHUNK = 64
DN_CONV = 4
N_MOD = 6
LN_EPS = 1e-5
RMS_EPS = 1e-6
L2_EPS = 1e-6
ADAM_LR = 0.001
ADAM_B1 = 0.9
ADAM_B2 = 0.999
ADAM_EPS = 1e-08
ADAM_WD = 0.01
ADAM_STEP = 10

HI = lax.Precision.HIGHEST
NN = ((1,), (0,))
NT = ((1,), (1,))
TN = ((0,), (0,))

ROW_TILE = 256
CONV_TILE = 256


def _dot(a, b, dims=NN):
    return lax.dot_general(a, b, (dims, ((), ())), precision=HI, preferred_element_type=F32)


def _mdot(a, b, dims=NN):
    return lax.dot_general(a.astype(MXU_DTYPE), b.astype(MXU_DTYPE), (dims, ((), ())), preferred_element_type=F32)


def _cparams(n):
    return pltpu.CompilerParams(dimension_semantics=("arbitrary",) * n)


def _call(body, *, grid, ins, outs, name, scratch=()):
    res = pl.pallas_call(
        body,
        grid=grid,
        in_specs=[pl.BlockSpec(b, m) for _, b, m in ins],
        out_specs=[pl.BlockSpec(b, m) for _, _, b, m in outs],
        out_shape=[jax.ShapeDtypeStruct(s, d) for s, d, _, _ in outs],
        scratch_shapes=list(scratch),
        name=name,
        compiler_params=_cparams(len(grid)),
    )(*[a for a, _, _ in ins])
    return res


def _tile(n, pref, unit=LANES):
    if n <= pref:
        return n
    t = (pref // unit) * unit
    while t > unit and n % t:
        t -= unit
    assert n % t == 0, (n, pref)
    return t


def _rowmap(fn, rows, consts, row_outs, acc_outs, name):
    rows = [r if isinstance(r, tuple) else (r, r.shape[1], 0) for r in rows]
    s = rows[0][0].shape[0]
    tm = min(ROW_TILE, s)
    nr, nc, no, na = len(rows), len(consts), len(row_outs), len(acc_outs)

    def body(*refs):
        rin, cin = refs[:nr], refs[nr:nr + nc]
        rout, aout = refs[nr + nc:nr + nc + no], refs[nr + nc + no:]
        ro, ao = fn(*[r[...] for r in rin], *[c[...] for c in cin])
        for ref, val in zip(rout, ro):
            ref[...] = val.astype(ref.dtype)
        if na:
            first = pl.program_id(0) == 0

            @pl.when(first)
            def _():
                for ref, val in zip(aout, ao):
                    ref[...] = val

            @pl.when(jnp.logical_not(first))
            def _():
                for ref, val in zip(aout, ao):
                    ref[...] += val

    ins = [(a, (tm, w), functools.partial(lambda i, cb: (i, cb), cb=cb)) for a, w, cb in rows]
    ins += [(c, c.shape, lambda i: (0, 0)) for c in consts]
    outs = [((s, w), d, (tm, w), lambda i: (i, 0)) for w, d in row_outs]
    outs += [(shp, F32, shp, lambda i: (0, 0)) for shp in acc_outs]
    res = _call(body, grid=(s // tm,), ins=ins, outs=outs, name=name)
    return res[:no], res[no:]


def _ln(z, g, b):
    mu = jnp.mean(z, -1, keepdims=True)
    var = jnp.mean(jnp.square(z - mu), -1, keepdims=True)
    return (z - mu) * lax.rsqrt(var + LN_EPS) * g + b


def _combine(alpha, x, y, gt, g, b, sc, sh):
    xn = _ln(alpha * x + (1.0 + gt) * y, g, b)
    return xn, xn * (1.0 + sc) + sh


def _modulate_fwd(x, sc, sh):
    def fn(x, sc, sh):
        return ((x * (1.0 + sc) + sh),), ()

    (h,), _ = _rowmap(fn, [x], [sc, sh], [(x.shape[1], MXU_DTYPE)], [], "modulate_fwd")
    return h


def _modulate_bwd(x, dx, dh, sc, sh):
    d = x.shape[1]

    def fn(x, dx, dh, sc, sh):
        _, vjp = jax.vjp(lambda x, sc, sh: x * (1.0 + sc) + sh, x, sc, sh)
        gx, gsc, gsh = vjp(dh)
        return (dx + gx,), (gsc, gsh)

    (gx,), (gsc, gsh) = _rowmap(fn, [x, dx, dh], [sc, sh], [(d, F32)], [(1, d), (1, d)], "modulate_bwd")
    return gx, gsc, gsh


def _combine_fwd(alpha, x, y, gt, g, b, sc, sh):
    d = x.shape[1]

    def fn(x, y, gt, g, b, sc, sh):
        return _combine(alpha, x, y, gt, g, b, sc, sh), ()

    (xn, h), _ = _rowmap(fn, [x, y], [gt, g, b, sc, sh], [(d, F32), (d, MXU_DTYPE)], [], "combine_fwd")
    return xn, h


def _combine_bwd(alpha, x, y, dxn, dh, gt, g, b, sc, sh):
    d = x.shape[1]

    def fn(x, y, dxn, dh, gt, g, b, sc, sh):
        _, vjp = jax.vjp(functools.partial(_combine, alpha), x, y, gt, g, b, sc, sh)
        gx, gy, ggt, gg, gb, gsc, gsh = vjp((dxn, dh))
        return (gx, gy), (ggt, gg, gb, gsc, gsh)

    (gx, gy), accs = _rowmap(fn, [x, y, dxn, dh], [gt, g, b, sc, sh], [(d, F32), (d, MXU_DTYPE)],
                             [(1, d)] * 5, "combine_bwd")
    return gx, gy, accs


def _last_fwd_bwd(alpha, x, y, tgt, gt, g, b):
    d = x.shape[1]

    def fn(x, y, tgt, gt, g, b):
        xn, vjp = jax.vjp(lambda x, y, gt, g, b: _ln(alpha * x + (1.0 + gt) * y, g, b), x, y, gt, g, b)
        err = xn - tgt
        gx, gy, ggt, gg, gb = vjp(err * (1.0 / d))
        rows = jnp.sum(jnp.square(err), axis=-1, keepdims=True)
        loss = (0.5 / d) * jnp.sum(rows, axis=0, keepdims=True) * jnp.ones((1, LANES), F32)
        return (gx, gy), (loss, ggt, gg, gb)

    (gx, gy), accs = _rowmap(fn, [x, y, tgt], [gt, g, b], [(d, F32), (d, MXU_DTYPE)],
                             [(1, LANES), (1, d), (1, d), (1, d)], "last_fwd_bwd")
    return gx, gy, accs


def _mm_call(a, a_blk, a_map, b, b_blk, b_map, outs, dims, grid, name, relu2=False):
    nk = grid[2]
    n_out = len(outs)

    def body(a_ref, b_ref, *rest):
        out_refs = rest[:n_out]

        def finish(val):
            if relu2:
                out_refs[0][...] = val.astype(out_refs[0].dtype)
                out_refs[1][...] = jnp.square(jnp.maximum(val, 0.0)).astype(out_refs[1].dtype)
            else:
                out_refs[0][...] = val.astype(out_refs[0].dtype)

        p = lax.dot_general(a_ref[...], b_ref[...], (dims, ((), ())), preferred_element_type=F32)
        if nk == 1:
            finish(p)
        else:
            acc = rest[n_out]
            k = pl.program_id(2)

            @pl.when(k == 0)
            def _():
                acc[...] = p

            @pl.when(k > 0)
            def _():
                acc[...] += p

            @pl.when(k == nk - 1)
            def _():
                finish(acc[...])

    out_blk = tuple(x for x in outs[0][2] if x is not None)
    scratch = [pltpu.VMEM(out_blk, F32)] if nk > 1 else []
    return _call(body, grid=grid, ins=[(a, a_blk, a_map), (b, b_blk, b_map)], outs=outs, name=name, scratch=scratch)


def _mm_nn(a, b, out_dtype, name, b_layer=None, relu2=False):
    m, kdim = a.shape
    tm = _tile(m, 512, 8)
    if b_layer is None:
        n = b.shape[1]
        tn, tk = _tile(n, 512), _tile(kdim, 1024)
        b_blk, b_map = (tk, tn), lambda i, j, k: (k, j)
    elif b_layer[1] == "cols":
        g, _, _, ng = b.shape
        n = g * ng
        tn, tk = _tile(ng, 512), _tile(kdim, 1024)
        npg = ng // tn
        b_blk = (None, None, tk, tn)
        b_map = functools.partial(lambda i, j, k, l, npg: (j // npg, l, k, j % npg), l=b_layer[0], npg=npg)
    else:
        g, _, kg, n = b.shape
        tn, tk = _tile(n, 512), _tile(kg, 1024)
        kpg = kg // tk
        b_blk = (None, None, tk, tn)
        b_map = functools.partial(lambda i, j, k, l, kpg: (k // kpg, l, k % kpg, j), l=b_layer[0], kpg=kpg)
    grid = (m // tm, n // tn, kdim // tk)
    outs = [((m, n), F32 if relu2 else out_dtype, (tm, tn), lambda i, j, k: (i, j))]
    if relu2:
        outs.append(((m, n), out_dtype, (tm, tn), lambda i, j, k: (i, j)))
    res = _mm_call(a, (tm, tk), lambda i, j, k: (i, k), b, b_blk, b_map, outs, NN, grid, name, relu2)
    return res if relu2 else res[0]


def _mm_nt(a, b, out_dtype, name, b_layer=None):
    m, n = a.shape
    tm = _tile(m, 512, 8)
    if b_layer is None:
        kout = b.shape[0]
        to, tc = _tile(kout, 512), _tile(n, 1024)
        b_blk, b_map = (to, tc), lambda i, j, k: (j, k)
    elif b_layer[1] == "cols":
        g, _, kout, ng = b.shape
        to, tc = _tile(kout, 512), _tile(ng, 1024)
        cpg = ng // tc
        b_blk = (None, None, to, tc)
        b_map = functools.partial(lambda i, j, k, l, cpg: (k // cpg, l, j, k % cpg), l=b_layer[0], cpg=cpg)
    else:
        g, _, kg, _ = b.shape
        kout = g * kg
        to, tc = _tile(kg, 512), _tile(n, 1024)
        opg = kg // to
        b_blk = (None, None, to, tc)
        b_map = functools.partial(lambda i, j, k, l, opg: (j // opg, l, j % opg, k), l=b_layer[0], opg=opg)
    grid = (m // tm, kout // to, n // tc)
    outs = [((m, kout), out_dtype, (tm, to), lambda i, j, k: (i, j))]
    return _mm_call(a, (tm, tc), lambda i, j, k: (i, k), b, b_blk, b_map, outs, NT, grid, name)[0]


def _mm_tn(a, b, out_dtype, name, split=None):
    m, kdim = a.shape
    n = b.shape[1]
    tc = _tile(m, 2048, 8)
    if split is None:
        tk, tn = _tile(kdim, 512), _tile(n, 512)
        out = ((kdim, n), out_dtype, (tk, tn), lambda i, j, k: (i, j))
    elif split == "cols":
        ng = n // N_DEV
        tk, tn = _tile(kdim, 512), _tile(ng, 512)
        npg = ng // tn
        out = ((N_DEV, kdim, ng), out_dtype, (None, tk, tn),
               functools.partial(lambda i, j, k, npg: (j // npg, i, j % npg), npg=npg))
    else:
        kg = kdim // N_DEV
        tk, tn = _tile(kg, 512), _tile(n, 512)
        kpg = kg // tk
        out = ((N_DEV, kg, n), out_dtype, (None, tk, tn),
               functools.partial(lambda i, j, k, kpg: (i // kpg, i % kpg, j), kpg=kpg))
    grid = (kdim // tk, n // tn, m // tc)
    return _mm_call(a, (tc, tk), lambda i, j, k: (k, i), b, (tc, tn), lambda i, j, k: (k, j), [out], TN, grid, name)[0]


def _shifted(xa, off, rows):
    if off % 8 == 0:
        return xa[off:off + rows]
    return pltpu.roll(xa, xa.shape[0] - off, 0)[:rows]


def _conv_pad(taps):
    return -(-(taps - 1) // 8) * 8


def _conv_tile(xp_ref, w, i, rows, taps):
    pad = _conv_pad(taps)
    r0 = pl.multiple_of(i * rows, rows)
    xa = xp_ref[pl.ds(r0, rows + pad), :]
    views = [_shifted(xa, pad - (taps - 1) + j, rows) for j in range(taps)]
    acc = w[0:1, :] * views[0]
    for j in range(1, taps):
        acc = acc + w[j:j + 1, :] * views[j]
    return r0, acc, views


def _conv_back_tile(yp_ref, w, i, rows, taps):
    pad = _conv_pad(taps)
    r0 = pl.multiple_of(i * rows, rows)
    ya = yp_ref[pl.ds(r0, rows + pad), :]
    acc = w[taps - 1:taps, :] * ya[:rows]
    for j in range(taps - 1):
        acc = acc + w[j:j + 1, :] * _shifted(ya, taps - 1 - j, rows)
    return r0, acc


def _tap_sums(dy, views, taps):
    row = lax.broadcasted_iota(jnp.int32, (taps, LANES), 0)
    acc = jnp.zeros((taps, LANES), F32)
    for j in range(taps):
        acc = acc + jnp.where(row == j, jnp.sum(dy * views[j], axis=0, keepdims=True), 0.0)
    return acc


def _silu_l2(xc, l2):
    a = jax.nn.silu(xc)
    if l2:
        a = a * lax.rsqrt(jnp.sum(a * a, axis=-1, keepdims=True) + L2_EPS)
    return a


def _dn_conv_fwd(proj, conv_w, c0, nblk, l2, name):
    s = proj.shape[0]
    pad = _conv_pad(DN_CONV)
    rows = min(CONV_TILE, s)

    def body(x_ref, w_ref, o_ref, xp):
        xp[0:pad, :] = jnp.zeros((pad, LANES), F32)
        xp[pad:, :] = x_ref[...]
        w = w_ref[...]

        def tile(i, c):
            r0, acc, _ = _conv_tile(xp, w, i, rows, DN_CONV)
            o_ref[pl.ds(r0, rows), :] = _silu_l2(acc, l2)
            return c

        lax.fori_loop(0, s // rows, tile, 0)

    return _call(body, grid=(nblk,),
                 ins=[(proj, (s, LANES), lambda c: (0, c0 + c)), (conv_w, (DN_CONV, LANES), lambda c: (0, c0 + c))],
                 outs=[((nblk, s, LANES), F32, (None, s, LANES), lambda c: (c, 0, 0))],
                 name=name, scratch=[pltpu.VMEM((s + pad, LANES), F32)])[0]


def _dn_conv_bwd(proj, conv_w, da, c0, nblk, l2, name):
    s = proj.shape[0]
    pad = _conv_pad(DN_CONV)
    rows = min(CONV_TILE, s)

    def body(x_ref, w_ref, da_ref, dx_ref, dw_ref, xp, yp):
        xp[0:pad, :] = jnp.zeros((pad, LANES), F32)
        xp[pad:, :] = x_ref[...]
        yp[s:, :] = jnp.zeros((pad, LANES), F32)
        w = w_ref[...]

        def tile(i, dw):
            r0, acc, views = _conv_tile(xp, w, i, rows, DN_CONV)
            _, vjp = jax.vjp(functools.partial(_silu_l2, l2=l2), acc)
            (dxc,) = vjp(da_ref[pl.ds(r0, rows), :])
            yp[pl.ds(r0, rows), :] = dxc
            return dw + _tap_sums(dxc, views, DN_CONV)

        dw_ref[...] = lax.fori_loop(0, s // rows, tile, jnp.zeros((DN_CONV, LANES), F32))

        def tile2(i, c):
            r0, acc = _conv_back_tile(yp, w, i, rows, DN_CONV)
            dx_ref[pl.ds(r0, rows), :] = acc.astype(dx_ref.dtype)
            return c

        lax.fori_loop(0, s // rows, tile2, 0)

    return _call(body, grid=(nblk,),
                 ins=[(proj, (s, LANES), lambda c: (0, c0 + c)), (conv_w, (DN_CONV, LANES), lambda c: (0, c0 + c)),
                      (da, (None, s, LANES), lambda c: (c, 0, 0))],
                 outs=[((s, nblk * LANES), MXU_DTYPE, (s, LANES), lambda c: (0, c)),
                       ((DN_CONV, nblk * LANES), F32, (DN_CONV, LANES), lambda c: (0, c))],
                 name=name, scratch=[pltpu.VMEM((s + pad, LANES), F32), pltpu.VMEM((s + pad, LANES), F32)])


def _cf_conv_fwd(vg, dw_w, dw_b):
    s, c2 = vg.shape
    ch = c2 // 2
    nblk = ch // LANES
    taps = dw_w.shape[0]
    pad = _conv_pad(taps)
    rows = min(CONV_TILE, s)

    def body(v_ref, g_ref, w_ref, b_ref, o_ref, xp):
        xp[0:pad, :] = jnp.zeros((pad, LANES), F32)
        xp[pad:, :] = v_ref[...] * jax.nn.sigmoid(g_ref[...])
        w = w_ref[...]
        bias = b_ref[...]

        def tile(i, c):
            r0, acc, _ = _conv_tile(xp, w, i, rows, taps)
            o_ref[pl.ds(r0, rows), :] = acc + bias
            return c

        lax.fori_loop(0, s // rows, tile, 0)

    return _call(body, grid=(nblk,),
                 ins=[(vg, (s, LANES), lambda c: (0, c)), (vg, (s, LANES), lambda c: (0, nblk + c)),
                      (dw_w, (taps, LANES), lambda c: (0, c)), (dw_b, (1, LANES), lambda c: (0, c))],
                 outs=[((s, ch), F32, (s, LANES), lambda c: (0, c))],
                 name="cf_conv_fwd", scratch=[pltpu.VMEM((s + pad, LANES), F32)])[0]


def _cf_conv_bwd(vg, dw_w, du):
    s, c2 = vg.shape
    ch = c2 // 2
    nblk = ch // LANES
    taps = dw_w.shape[0]
    pad = _conv_pad(taps)
    rows = min(CONV_TILE, s)

    def body(v_ref, g_ref, w_ref, du_ref, dv_ref, dg_ref, dw_ref, db_ref, xp, yp):
        sig = jax.nn.sigmoid(g_ref[...])
        xp[0:pad, :] = jnp.zeros((pad, LANES), F32)
        xp[pad:, :] = v_ref[...] * sig
        yp[0:s, :] = du_ref[...]
        yp[s:, :] = jnp.zeros((pad, LANES), F32)
        w = w_ref[...]
        db_ref[...] = jnp.sum(du_ref[...], axis=0, keepdims=True)

        def tile(i, dw):
            r0, _, views = _conv_tile(xp, w, i, rows, taps)
            return dw + _tap_sums(du_ref[pl.ds(r0, rows), :], views, taps)

        dw_ref[...] = lax.fori_loop(0, s // rows, tile, jnp.zeros((taps, LANES), F32))

        def tile2(i, c):
            r0, du0 = _conv_back_tile(yp, w, i, rows, taps)
            val = v_ref[pl.ds(r0, rows), :]
            sg = jax.nn.sigmoid(g_ref[pl.ds(r0, rows), :])
            dv_ref[pl.ds(r0, rows), :] = (du0 * sg).astype(dv_ref.dtype)
            dg_ref[pl.ds(r0, rows), :] = (du0 * val * sg * (1.0 - sg)).astype(dg_ref.dtype)
            return c

        lax.fori_loop(0, s // rows, tile2, 0)

    return _call(body, grid=(nblk,),
                 ins=[(vg, (s, LANES), lambda c: (0, c)), (vg, (s, LANES), lambda c: (0, nblk + c)),
                      (dw_w, (taps, LANES), lambda c: (0, c)), (du, (s, LANES), lambda c: (0, c))],
                 outs=[((s, ch), MXU_DTYPE, (s, LANES), lambda c: (0, c)),
                       ((s, ch), MXU_DTYPE, (s, LANES), lambda c: (0, c)),
                       ((taps, ch), F32, (taps, LANES), lambda c: (0, c)),
                       ((1, ch), F32, (1, LANES), lambda c: (0, c))],
                 name="cf_conv_bwd", scratch=[pltpu.VMEM((s + pad, LANES), F32), pltpu.VMEM((s + pad, LANES), F32)])


def _masks():
    r = lax.broadcasted_iota(jnp.int32, (CHUNK, CHUNK), 0)
    c = lax.broadcasted_iota(jnp.int32, (CHUNK, CHUNK), 1)
    return r >= c, r > c, r <= c


def _tri_inv(a):
    r = lax.broadcasted_iota(jnp.int32, (CHUNK, CHUNK), 0)
    c = lax.broadcasted_iota(jnp.int32, (CHUNK, CHUNK), 1)
    p = jnp.where(r == c, 1.0, 0.0).astype(F32) - a
    q = _dot(a, a)
    n = 2
    while True:
        p = p + _dot(p, q)
        n *= 2
        if n >= CHUNK:
            return p
        q = _dot(q, q)


def _chunk_decay(g):
    causal, _, upper = _masks()
    gb = jnp.broadcast_to(g, (CHUNK, CHUNK))
    gam_r = _dot(jnp.where(causal, 1.0, 0.0).astype(F32), gb)
    gam_s = _dot(jnp.ones((CHUNK, CHUNK), F32), jnp.where(upper, gb, 0.0))
    dm = jnp.where(causal, jnp.exp(jnp.where(causal, gam_r - gam_s, 0.0)), 0.0)
    return gam_r[:, 0:1], dm


def _chunk_prep(q, k, v, beta, g):
    _, strict, _ = _masks()
    gam, dm = _chunk_decay(g)
    eg = jnp.exp(gam)
    kb = k * beta
    a = jnp.where(strict, _mdot(kb, k, NT) * dm, 0.0)
    t = _tri_inv(a)
    u = _mdot(t, v * beta)
    w = _mdot(t, kb * eg)
    aqk = _mdot(q * (HEAD_DIM ** -0.5), k, NT) * dm
    return u, w, aqk, t, gam


def _chunk_prep_bwd(q, k, v, beta, g, t, du, dw, daqk, dqd, dkd, dgl):
    causal, strict, _ = _masks()
    scale = HEAD_DIM ** -0.5
    gam, dm = _chunk_decay(g)
    eg = jnp.exp(gam)
    gam_last = gam[CHUNK - 1:CHUNK, :]
    rr = jnp.exp(gam_last - gam)
    kb = k * beta
    qs = q * scale
    kk = _mdot(kb, k, NT)
    a = jnp.where(strict, kk * dm, 0.0)
    aqk = _mdot(qs, k, NT) * dm
    vb = v * beta
    kbe = kb * eg

    dt = _mdot(du, vb, NT) + _mdot(dw, kbe, NT)
    dvb = _mdot(t, du, TN)
    dkbe = _mdot(t, dw, TN)
    da = jnp.where(strict, -_dot(_dot(t, dt, TN), t, NT), 0.0)
    dkk = da * dm
    dqk = daqk * dm
    ddiff = da * a + daqk * aqk
    dkb = _mdot(dkk, k) + dkbe * eg
    dk = _mdot(dkk, kb, TN) + _mdot(dqk, qs, TN) + dkb * beta + dkd * rr
    dq = (_mdot(dqk, k) + dqd * eg) * scale
    dbeta = jnp.sum(dkb * k, axis=-1, keepdims=True) + jnp.sum(dvb * v, axis=-1, keepdims=True)
    dv = dvb * beta
    deg = jnp.sum(dkbe * kb, axis=-1, keepdims=True) + jnp.sum(dqd * qs, axis=-1, keepdims=True)
    drr = jnp.sum(dkd * k, axis=-1, keepdims=True)
    colsum = _dot(ddiff, jnp.ones((CHUNK, LANES), F32), TN)[:, 0:1]
    dgam = deg * eg - drr * rr + jnp.sum(ddiff, axis=-1, keepdims=True) - colsum
    dgam_last = jnp.sum(drr * rr, axis=0, keepdims=True) + dgl[0:1, :] * jnp.exp(gam_last)
    row = lax.broadcasted_iota(jnp.int32, (CHUNK, 1), 0)
    dgam = dgam + jnp.where(row == CHUNK - 1, dgam_last, 0.0)
    dg = _dot(jnp.where(causal, 1.0, 0.0).astype(F32), jnp.broadcast_to(dgam, (CHUNK, LANES)), TN)[:, 0:1]
    return dq, dk, dv, dbeta, dg


def _prep_group(s):
    nch = s // CHUNK
    return 4 if nch % 4 == 0 else 1


def _dn_prep(q, k, v, beta, g):
    h, s, _ = q.shape
    cb = _prep_group(s)
    rb = cb * CHUNK

    def body(q_ref, k_ref, v_ref, b_ref, g_ref, u_ref, w_ref, a_ref, t_ref, gam_ref):
        for i in range(cb):
            sl = slice(i * CHUNK, (i + 1) * CHUNK)
            u, w, aqk, t, gam = _chunk_prep(q_ref[sl, :], k_ref[sl, :], v_ref[sl, :], b_ref[sl, :], g_ref[sl, :])
            u_ref[sl, :] = u
            w_ref[sl, :] = w
            a_ref[sl, :] = aqk
            t_ref[sl, :] = t
            gam_ref[sl, :] = gam

    big = lambda x: (x, (None, rb, HEAD_DIM), lambda hh, n: (hh, n, 0))
    col = lambda x: (x, (None, rb, 1), lambda hh, n: (hh, n, 0))
    o_big = ((h, s, HEAD_DIM), F32, (None, rb, HEAD_DIM), lambda hh, n: (hh, n, 0))
    o_sq = ((h, s, CHUNK), F32, (None, rb, CHUNK), lambda hh, n: (hh, n, 0))
    o_col = ((h, s, 1), F32, (None, rb, 1), lambda hh, n: (hh, n, 0))
    return _call(body, grid=(h, s // rb), ins=[big(q), big(k), big(v), col(beta), col(g)],
                 outs=[o_big, o_big, o_sq, o_sq, o_col], name="dn_prep")


def _dn_prep_bwd(q, k, v, beta, g, t, du, dw, daqk, dqd, dkd, dgl):
    h, s, _ = q.shape
    cb = _prep_group(s)
    rb = cb * CHUNK

    def body(q_ref, k_ref, v_ref, b_ref, g_ref, t_ref, du_ref, dw_ref, da_ref, dqd_ref, dkd_ref, dgl_ref,
             dq_ref, dk_ref, dv_ref, db_ref, dg_ref):
        for i in range(cb):
            sl = slice(i * CHUNK, (i + 1) * CHUNK)
            dq, dk, dv, dbeta, dg = _chunk_prep_bwd(
                q_ref[sl, :], k_ref[sl, :], v_ref[sl, :], b_ref[sl, :], g_ref[sl, :], t_ref[sl, :],
                du_ref[sl, :], dw_ref[sl, :], da_ref[sl, :], dqd_ref[sl, :], dkd_ref[sl, :], dgl_ref[sl, :])
            dq_ref[sl, :] = dq
            dk_ref[sl, :] = dk
            dv_ref[sl, :] = dv
            db_ref[sl, :] = dbeta
            dg_ref[sl, :] = dg

    big = lambda x: (x, (None, rb, HEAD_DIM), lambda hh, n: (hh, n, 0))
    sq = lambda x: (x, (None, rb, CHUNK), lambda hh, n: (hh, n, 0))
    col = lambda x: (x, (None, rb, 1), lambda hh, n: (hh, n, 0))
    o_big = ((h, s, HEAD_DIM), F32, (None, rb, HEAD_DIM), lambda hh, n: (hh, n, 0))
    o_col = ((h, s, 1), F32, (None, rb, 1), lambda hh, n: (hh, n, 0))
    return _call(body, grid=(h, s // rb),
                 ins=[big(q), big(k), big(v), col(beta), col(g), sq(t), big(du), big(dw), sq(daqk), big(dqd), big(dkd),
                      col(dgl)],
                 outs=[o_big, o_big, o_big, o_col, o_col], name="dn_prep_bwd")


def _chunk_scaled(q, k, gam):
    gam_last = gam[CHUNK - 1:CHUNK, :]
    q_dec = q * (HEAD_DIM ** -0.5) * jnp.exp(gam)
    k_dec = k * jnp.exp(gam_last - gam)
    return q_dec, k_dec, jnp.exp(gam_last)


def _dn_scan(q, k, u, w, aqk, gam):
    h, s, _ = q.shape
    nch = s // CHUNK

    def body(q_ref, k_ref, u_ref, w_ref, a_ref, gam_ref, o_ref, st_ref, state):
        @pl.when(pl.program_id(0) == 0)
        def _():
            state[...] = jnp.zeros_like(state)

        for hh in range(h):
            s0 = state[hh]
            st_ref[hh] = s0
            q_dec, k_dec, gl = _chunk_scaled(q_ref[hh], k_ref[hh], gam_ref[hh])
            v_new = u_ref[hh] - _mdot(w_ref[hh], s0)
            o_ref[:, hh * HEAD_DIM:(hh + 1) * HEAD_DIM] = _mdot(q_dec, s0) + _mdot(a_ref[hh], v_new)
            state[hh] = s0 * gl + _mdot(k_dec, v_new, TN)

    big = lambda x: (x, (h, CHUNK, HEAD_DIM), lambda n: (0, n, 0))
    return _call(body, grid=(nch,),
                 ins=[big(q), big(k), big(u), big(w), (aqk, (h, CHUNK, CHUNK), lambda n: (0, n, 0)),
                      (gam, (h, CHUNK, 1), lambda n: (0, n, 0))],
                 outs=[((s, h * HEAD_DIM), F32, (CHUNK, h * HEAD_DIM), lambda n: (n, 0)),
                       ((nch, h, HEAD_DIM, HEAD_DIM), F32, (None, h, HEAD_DIM, HEAD_DIM), lambda n: (n, 0, 0, 0))],
                 name="dn_scan", scratch=[pltpu.VMEM((h, HEAD_DIM, HEAD_DIM), F32)])


def _dn_scan_bwd(q, k, u, w, aqk, gam, states, do):
    h, s, _ = q.shape
    nch = s // CHUNK

    def body(q_ref, k_ref, u_ref, w_ref, a_ref, gam_ref, st_ref, do_ref,
             du_ref, dw_ref, da_ref, dqd_ref, dkd_ref, dgl_ref, dstate):
        @pl.when(pl.program_id(0) == 0)
        def _():
            dstate[...] = jnp.zeros_like(dstate)

        for hh in range(h):
            s0 = st_ref[hh]
            ds = dstate[hh]
            doh = do_ref[:, hh * HEAD_DIM:(hh + 1) * HEAD_DIM]
            wv = w_ref[hh]
            av = a_ref[hh]
            q_dec, k_dec, gl = _chunk_scaled(q_ref[hh], k_ref[hh], gam_ref[hh])
            v_new = u_ref[hh] - _mdot(wv, s0)
            dv_new = _mdot(av, doh, TN) + _mdot(k_dec, ds)
            du_ref[hh] = dv_new
            dw_ref[hh] = -_mdot(dv_new, s0, NT)
            da_ref[hh] = _mdot(doh, v_new, NT)
            dqd_ref[hh] = _mdot(doh, s0, NT)
            dkd_ref[hh] = _mdot(v_new, ds, NT)
            tot = jnp.sum(jnp.sum(s0 * ds, axis=-1, keepdims=True), axis=0, keepdims=True)
            dgl_ref[hh] = jnp.broadcast_to(tot, (CHUNK, 1))
            dstate[hh] = ds * gl + _mdot(q_dec, doh, TN) - _mdot(wv, dv_new, TN)

    rev = lambda n: (0, nch - 1 - n, 0)
    big = lambda x: (x, (h, CHUNK, HEAD_DIM), rev)
    o_big = ((h, s, HEAD_DIM), F32, (h, CHUNK, HEAD_DIM), rev)
    return _call(body, grid=(nch,),
                 ins=[big(q), big(k), big(u), big(w), (aqk, (h, CHUNK, CHUNK), rev), (gam, (h, CHUNK, 1), rev),
                      (states, (None, h, HEAD_DIM, HEAD_DIM), lambda n: (nch - 1 - n, 0, 0, 0)),
                      (do, (CHUNK, h * HEAD_DIM), lambda n: (nch - 1 - n, 0))],
                 outs=[o_big, o_big, ((h, s, CHUNK), F32, (h, CHUNK, CHUNK), rev), o_big, o_big,
                       ((h, s, 1), F32, (h, CHUNK, 1), rev)],
                 name="dn_scan_bwd", scratch=[pltpu.VMEM((h, HEAD_DIM, HEAD_DIM), F32)])


def _gates(x, a_log, dt_b, h):
    lane = lax.broadcasted_iota(jnp.int32, x.shape, 1)
    return jnp.where(lane < h, jax.nn.sigmoid(x), -jnp.exp(a_log) * jax.nn.softplus(x + dt_b))


def _head_out(oh, zh, nw):
    on = oh * lax.rsqrt(jnp.mean(oh * oh, axis=-1, keepdims=True) + RMS_EPS) * nw
    return on * jax.nn.silu(zh)


def _to_heads(x, h):
    return jnp.transpose(x[:, :h])[:, :, None]


def _pad_lanes(x, lo):
    return jnp.zeros((1, LANES), F32).at[0, lo:lo + x.shape[0]].set(x)


def _deltanet_fwd(hin, w_in, conv_w, a_log, dt_bias, norm_w, w_out):
    h = a_log.shape[0]
    hw = h * HEAD_DIM
    proj = _mm_nn(hin, w_in, F32, "dn_proj")
    q = _dn_conv_fwd(proj, conv_w, 0, h, True, "dn_conv_q")
    k = _dn_conv_fwd(proj, conv_w, h, h, True, "dn_conv_k")
    v = _dn_conv_fwd(proj, conv_w, 2 * h, h, False, "dn_conv_v")
    alp, dtp = _pad_lanes(a_log, h), _pad_lanes(dt_bias, h)

    def gates_fn(x, al, db):
        return (_gates(x, al, db, h),), ()

    (bg,), _ = _rowmap(gates_fn, [(proj, LANES, 4 * h)], [alp, dtp], [(LANES, F32)], [], "dn_gates")
    beta, g = _to_heads(bg, h), _to_heads(bg[:, h:], h)
    u, w, aqk, t, gam = _dn_prep(q, k, v, beta, g)
    o, states = _dn_scan(q, k, u, w, aqk, gam)
    nw = norm_w[None, :]

    def out_fn(o, z, nw):
        parts = [_head_out(o[:, i * HEAD_DIM:(i + 1) * HEAD_DIM], z[:, i * HEAD_DIM:(i + 1) * HEAD_DIM], nw)
                 for i in range(h)]
        return (jnp.concatenate(parts, axis=-1),), ()

    (og,), _ = _rowmap(out_fn, [o, (proj, hw, 3)], [nw], [(hw, MXU_DTYPE)], [], "dn_out")
    y = _mm_nn(og, w_out, F32, "dn_y")
    return y, (hin, proj, q, k, v, beta, g, u, w, aqk, t, gam, states, o, og, alp, dtp, nw)


def _deltanet_bwd(res, dy, w_in, conv_w, w_out):
    hin, proj, q, k, v, beta, g, u, w, aqk, t, gam, states, o, og, alp, dtp, nw = res
    h = q.shape[0]
    hw = h * HEAD_DIM
    s = hin.shape[0]
    d_w_out = _mm_tn(og, dy, MXU_DTYPE, "dn_dwout", split="rows")
    dog = _mm_nt(dy, w_out, F32, "dn_dog")

    def out_bwd(o, z, dog, nw):
        dos, dzs = [], []
        dn = jnp.zeros((1, HEAD_DIM), F32)
        for i in range(h):
            sl = slice(i * HEAD_DIM, (i + 1) * HEAD_DIM)
            _, vjp = jax.vjp(_head_out, o[:, sl], z[:, sl], nw)
            a, b, c = vjp(dog[:, sl])
            dos.append(a)
            dzs.append(b)
            dn = dn + c
        return (jnp.concatenate(dos, axis=-1), jnp.concatenate(dzs, axis=-1)), (dn,)

    (do, dz), (d_norm_w,) = _rowmap(out_bwd, [o, (proj, hw, 3), dog], [nw], [(hw, F32), (hw, MXU_DTYPE)],
                                    [(1, HEAD_DIM)], "dn_out_bwd")
    du, dw, daqk, dqd, dkd, dgl = _dn_scan_bwd(q, k, u, w, aqk, gam, states, do)
    dq, dk, dv, dbeta, dg = _dn_prep_bwd(q, k, v, beta, g, t, du, dw, daqk, dqd, dkd, dgl)
    dpq, dwq = _dn_conv_bwd(proj, conv_w, dq, 0, h, True, "dn_conv_q_bwd")
    dpk, dwk = _dn_conv_bwd(proj, conv_w, dk, h, h, True, "dn_conv_k_bwd")
    dpv, dwv = _dn_conv_bwd(proj, conv_w, dv, 2 * h, h, False, "dn_conv_v_bwd")
    dbg = jnp.concatenate([jnp.transpose(dbeta[:, :, 0]), jnp.transpose(dg[:, :, 0]),
                           jnp.zeros((s, LANES - 2 * h), F32)], axis=1)

    def gates_bwd(x, dbg, al, db):
        _, vjp = jax.vjp(functools.partial(_gates, h=h), x, al, db)
        gx, gal, gdb = vjp(dbg)
        return (gx,), (gal, gdb)

    (dba,), (d_alp, d_dtp) = _rowmap(gates_bwd, [(proj, LANES, 4 * h), dbg], [alp, dtp], [(LANES, MXU_DTYPE)],
                                     [(1, LANES), (1, LANES)], "dn_gates_bwd")
    dproj = jnp.concatenate([dpq, dpk, dpv, dz, dba], axis=1)
    d_w_in = _mm_tn(hin, dproj, MXU_DTYPE, "dn_dwin")
    dh = _mm_nt(dproj, w_in, F32, "dn_dh")
    d_conv_w = jnp.concatenate([dwq, dwk, dwv], axis=1)
    return dh, dict(w_in=d_w_in, w_out=d_w_out, conv_w=d_conv_w, a_log=d_alp[0, h:2 * h], dt_bias=d_dtp[0, h:2 * h],
                    norm_w=d_norm_w[0])


def _ln_silu(u, g, b):
    return jax.nn.silu(_ln(u, g, b))


def _conformer_fwd(hin, w_in, dw_w, dw_b, ln_g, ln_b, w_out):
    vg = _mm_nn(hin, w_in, F32, "cf_vg")
    u1 = _cf_conv_fwd(vg, dw_w, dw_b)
    ch = u1.shape[1]

    def fn(u, g, b):
        return (_ln_silu(u, g, b),), ()

    (u2,), _ = _rowmap(fn, [u1], [ln_g, ln_b], [(ch, MXU_DTYPE)], [], "cf_ln")
    y = _mm_nn(u2, w_out, F32, "cf_y")
    return y, (hin, vg, u1, u2)


def _conformer_bwd(res, dy, w_in, dw_w, ln_g, ln_b, w_out):
    hin, vg, u1, u2 = res
    ch = u1.shape[1]
    d_w_out = _mm_tn(u2, dy, MXU_DTYPE, "cf_dwout", split="rows")
    du2 = _mm_nt(dy, w_out, F32, "cf_du2")

    def fn(u, du2, g, b):
        _, vjp = jax.vjp(_ln_silu, u, g, b)
        gu, gg, gb = vjp(du2)
        return (gu,), (gg, gb)

    (du1,), (d_ln_g, d_ln_b) = _rowmap(fn, [u1, du2], [ln_g, ln_b], [(ch, F32)], [(1, ch), (1, ch)], "cf_ln_bwd")
    dval, dgate, d_dw_w, d_dw_b = _cf_conv_bwd(vg, dw_w, du1)
    dvg = jnp.concatenate([dval, dgate], axis=1)
    d_w_in = _mm_tn(hin, dvg, MXU_DTYPE, "cf_dwin", split="cols")
    dh = _mm_nt(dvg, w_in, F32, "cf_dh")
    return dh, dict(w_in=d_w_in, w_out=d_w_out, dw_w=d_dw_w, dw_b=d_dw_b[0], ln_g=d_ln_g[0], ln_b=d_ln_b[0])


def _mlp_fwd(hin, w1g, w2g, layer):
    a, r = _mm_nn(hin, w1g, MXU_DTYPE, "ff_a", b_layer=(layer, "cols"), relu2=True)
    m = _mm_nn(r, w2g, F32, "ff_m", b_layer=(layer, "rows"))
    return m, (hin, a, r)


def _mlp_bwd(res, dm, w1g, w2g, layer):
    hin, a, r = res
    f = a.shape[1]
    d_w2 = _mm_tn(r, dm, MXU_DTYPE, "ff_dw2", split="rows")
    dr = _mm_nt(dm, w2g, F32, "ff_dr", b_layer=(layer, "rows"))

    def fn(a, dr):
        return ((dr * 2.0 * jnp.maximum(a, 0.0)),), ()

    (da,), _ = _rowmap(fn, [a, dr], [], [(f, MXU_DTYPE)], [], "ff_da")
    d_w1 = _mm_tn(hin, da, MXU_DTYPE, "ff_dw1", split="cols")
    dh = _mm_nt(da, w1g, F32, "ff_dh", b_layer=(layer, "cols"))
    return dh, d_w1, d_w2


def _ada_fwd(c_all, ada_w):
    depth, d, nl = ada_w.shape
    tn = _tile(nl, 256)

    def body(c_ref, w_ref, o_ref, cond_ref):
        cond = jax.nn.silu(c_ref[...]).astype(MXU_DTYPE)
        cond_ref[...] = cond
        o_ref[...] = lax.dot_general(cond, w_ref[...].astype(MXU_DTYPE), (NN, ((), ())), preferred_element_type=F32)

    return _call(body, grid=(depth, nl // tn),
                 ins=[(c_all, c_all.shape, lambda l, j: (0, 0)), (ada_w, (None, d, tn), lambda l, j: (l, 0, j))],
                 outs=[((depth, N_DEV, nl), F32, (None, N_DEV, tn), lambda l, j: (l, 0, j)),
                       (c_all.shape, MXU_DTYPE, c_all.shape, lambda l, j: (0, 0))],
                 name="ada_fwd")


def _ada_bwd(cond_all, dmod_cols):
    depth, _, nl = dmod_cols.shape
    d = cond_all.shape[1]
    tn = _tile(nl, 256)

    def body(c_ref, g_ref, o_ref):
        o_ref[...] = lax.dot_general(c_ref[...], g_ref[...].astype(MXU_DTYPE), (TN, ((), ())),
                                     preferred_element_type=F32)

    return _call(body, grid=(depth, nl // tn),
                 ins=[(cond_all, cond_all.shape, lambda l, j: (0, 0)), (dmod_cols, (None, N_DEV, tn), lambda l, j: (l, 0, j))],
                 outs=[((depth, d, nl), F32, (None, d, tn), lambda l, j: (l, 0, j))], name="ada_bwd")[0]


def _exchange(arrs, scatter, name):
    nt = len(arrs)
    out_shape = [jax.ShapeDtypeStruct(a.shape if scatter else (N_DEV,) + a.shape, a.dtype) for a in arrs]

    def body(*refs):
        ins, outs = refs[:nt], refs[nt:2 * nt]
        send, recv, loc = refs[2 * nt:]
        x, y, c = lax.axis_index("x"), lax.axis_index("y"), lax.axis_index("c")
        me = 4 * x + 2 * y + c
        copies = []
        for t in range(nt):
            own = pltpu.make_async_copy(ins[t].at[me] if scatter else ins[t], outs[t].at[me], loc.at[t])
            own.start()
            copies.append(own)
            for k in range(1, N_DEV):
                px = 1 - x if k & 4 else x
                py = 1 - y if k & 2 else y
                pc = 1 - c if k & 1 else c
                peer = 4 * px + 2 * py + pc
                cp = pltpu.make_async_remote_copy(
                    src_ref=ins[t].at[peer] if scatter else ins[t], dst_ref=outs[t].at[me],
                    send_sem=send.at[t, k - 1], recv_sem=recv.at[t, k - 1],
                    device_id=(px, py, pc), device_id_type=pl.DeviceIdType.MESH)
                cp.start()
                copies.append(cp)
        for cp in copies:
            cp.wait()

    any_spec = pl.BlockSpec(memory_space=pl.ANY)
    return pl.pallas_call(
        body, out_shape=out_shape, in_specs=[any_spec] * nt, out_specs=[any_spec] * nt,
        scratch_shapes=[pltpu.SemaphoreType.DMA((nt, N_DEV - 1)), pltpu.SemaphoreType.DMA((nt, N_DEV - 1)),
                        pltpu.SemaphoreType.DMA((nt,))],
        name=name)(*arrs)


def _adamw(parts, w, m, v, name):
    p, nl, r, c = parts.shape
    tr = _tile(r, 256, 8)

    def body(p_ref, w_ref, m_ref, v_ref, g_out, d_out, m_out, v_out):
        g = p_ref[0].astype(F32)
        for i in range(1, p):
            g = g + p_ref[i].astype(F32)
        m2 = ADAM_B1 * m_ref[...] + (1.0 - ADAM_B1) * g
        v2 = ADAM_B2 * v_ref[...] + (1.0 - ADAM_B2) * jnp.square(g)
        m_hat = m2 / (1.0 - ADAM_B1 ** ADAM_STEP)
        v_hat = v2 / (1.0 - ADAM_B2 ** ADAM_STEP)
        g_out[...] = g
        d_out[...] = -ADAM_LR * (m_hat / (jnp.sqrt(v_hat) + ADAM_EPS) + ADAM_WD * w_ref[...])
        m_out[...] = m2
        v_out[...] = v2

    blk = (None, tr, c)
    imap = lambda l, i: (l, i, 0)
    out = ((nl, r, c), F32, blk, imap)
    return _call(body, grid=(nl, r // tr),
                 ins=[(parts, (p, None, tr, c), lambda l, i: (0, l, i, 0)), (w, blk, imap), (m, blk, imap), (v, blk, imap)],
                 outs=[out] * 4, name=name)


def _rows(x):
    return x.reshape(-1, LANES)


def _pad_rows(x, mult=8):
    r = x.shape[0]
    extra = (-r) % mult
    return jnp.pad(x, ((0, extra), (0, 0))) if extra else x


def _shard_cols(x, me, groups):
    lead = x.shape[:-1]
    xr = x.reshape(lead + (N_DEV, groups * LANES))
    xs = lax.dynamic_index_in_dim(xr, me, axis=len(lead), keepdims=False)
    return xs.reshape(N_DEV, -1, LANES)


def kernel(x, c, ada_w, ada_b, ln_g, ln_b, dn_w_in, dn_conv_w, dn_a_log, dn_dt_bias, dn_norm_w, dn_w_out, cf_w_in, cf_dw_w, cf_dw_b, cf_ln_g, cf_ln_b, cf_w_out, ff_w1, ff_w2, loss_target, m_ada_w, m_ada_b, m_ln_g, m_ln_b, m_dn_w_in, m_dn_conv_w, m_dn_a_log, m_dn_dt_bias, m_dn_norm_w, m_dn_w_out, m_cf_w_in, m_cf_dw_w, m_cf_dw_b, m_cf_ln_g, m_cf_ln_b, m_cf_w_out, m_ff_w1, m_ff_w2, v_ada_w, v_ada_b, v_ln_g, v_ln_b, v_dn_w_in, v_dn_conv_w, v_dn_a_log, v_dn_dt_bias, v_dn_norm_w, v_dn_w_out, v_cf_w_in, v_cf_dw_w, v_cf_dw_b, v_cf_ln_g, v_cf_ln_b, v_cf_w_out, v_ff_w1, v_ff_w2):
    depth, d, _ = ada_w.shape
    n_a, n_b = dn_w_in.shape[0], cf_w_in.shape[0]
    heads = dn_a_log.shape[1]
    hw = heads * HEAD_DIM
    taps = cf_dw_w.shape[1]
    s = x.shape[1]
    alpha = (2.0 * depth) ** 0.25
    me = 4 * lax.axis_index("x") + 2 * lax.axis_index("y") + lax.axis_index("c")
    xs, tgt = x[0], loss_target[0]

    small_local = [_rows(ln_g), _rows(ln_b), _rows(dn_conv_w), _rows(cf_dw_w), _rows(cf_dw_b), _rows(cf_ln_g),
                   _rows(cf_ln_b), _rows(c)]
    sizes = [a.shape[0] for a in small_local]
    packed = _pad_rows(jnp.concatenate(small_local, axis=0))
    big_local = [w.astype(MXU_DTYPE) for w in (dn_w_in, dn_w_out, cf_w_in, cf_w_out, ff_w1, ff_w2)]
    gathered = _exchange([packed] + big_local, False, "comm_gather_params")
    small_all = gathered[0]
    g_dn_w_in, g_dn_w_out, g_cf_w_in, g_cf_w_out, g_ff_w1, g_ff_w2 = gathered[1:]
    offs = [0]
    for z in sizes:
        offs.append(offs[-1] + z)

    def small(i):
        return small_all[:, offs[i]:offs[i + 1], :]

    def unshard(piece, lead, groups):
        t = piece.reshape((N_DEV,) + lead + (groups * LANES,))
        t = jnp.moveaxis(t, 0, len(lead))
        return t.reshape(lead + (N_DEV * groups * LANES,))

    ln_g_f = unshard(small(0), (depth, 2), 1)
    ln_b_f = unshard(small(1), (depth, 2), 1)
    conv_w_f = unshard(small(2), (n_a, DN_CONV), 3 * heads // N_DEV)
    dw_w_f = unshard(small(3), (n_b, taps), 1)
    dw_b_f = unshard(small(4), (n_b,), 1)
    cf_ln_g_f = unshard(small(5), (n_b,), 1)
    cf_ln_b_f = unshard(small(6), (n_b,), 1)
    c_all = small(7).reshape(N_DEV, d)

    dn_in_cols = dn_w_in.shape[2]
    w_dn_in = jnp.moveaxis(g_dn_w_in, 0, 2).reshape(n_a, d, N_DEV * dn_in_cols)
    w_dn_in = jnp.pad(w_dn_in, ((0, 0), (0, 0), (0, 4 * hw + LANES - N_DEV * dn_in_cols)))
    w_dn_out = jnp.moveaxis(g_dn_w_out, 0, 1).reshape(n_a, hw, d)
    w_cf_in = jnp.moveaxis(g_cf_w_in, 0, 2).reshape(n_b, d, 2 * d)
    w_cf_out = jnp.moveaxis(g_cf_w_out, 0, 1).reshape(n_b, d, d)

    mod_part, cond_all = _ada_fwd(c_all, ada_w)
    (mod_all,) = _exchange([mod_part], False, "comm_gather_mod")
    mod_mine = lax.dynamic_index_in_dim(mod_all, me, axis=2, keepdims=False)
    mod_mine = jnp.moveaxis(mod_mine, 0, 1).reshape(depth, N_MOD * d)

    def add_bias(a, b):
        return (a + b,), ()

    (mod,), _ = _rowmap(add_bias, [mod_mine, ada_b], [], [(N_MOD * d, F32)], [], "ada_bias")

    def mod_row(i, j):
        return mod[i:i + 1, j * d:(j + 1) * d]

    def ln_row(arr, i, j):
        return arr[i, j][None, :]

    subs = []
    h_cur = _modulate_fwd(xs, mod_row(0, 1), mod_row(0, 0))
    x_cur = xs
    last = None
    for i in range(depth):
        j = i // 2
        if i % 2 == 0:
            y, res = _deltanet_fwd(h_cur, w_dn_in[j], conv_w_f[j], dn_a_log[j], dn_dt_bias[j], dn_norm_w[j], w_dn_out[j])
        else:
            y, res = _conformer_fwd(h_cur, w_cf_in[j], dw_w_f[j], dw_b_f[j][None, :], cf_ln_g_f[j][None, :],
                                    cf_ln_b_f[j][None, :], w_cf_out[j])
        p1 = (mod_row(i, 2), ln_row(ln_g_f, i, 0), ln_row(ln_b_f, i, 0), mod_row(i, 4), mod_row(i, 3))
        x_mid, h_mid = _combine_fwd(alpha, x_cur, y, *p1)
        subs.append((x_cur, y, p1, res))
        m_out, res2 = _mlp_fwd(h_mid, g_ff_w1, g_ff_w2, i)
        if i + 1 < depth:
            p2 = (mod_row(i, 5), ln_row(ln_g_f, i, 1), ln_row(ln_b_f, i, 1), mod_row(i + 1, 1), mod_row(i + 1, 0))
            x_next, h_next = _combine_fwd(alpha, x_mid, m_out, *p2)
            subs.append((x_mid, m_out, p2, res2))
            x_cur, h_cur = x_next, h_next
        else:
            p2 = (mod_row(i, 5), ln_row(ln_g_f, i, 1), ln_row(ln_b_f, i, 1))
            last = (x_mid, m_out, p2, res2)

    x_in, y_in, p_last, res_last = last
    dx, dy, (loss_acc, g_gt, g_g, g_b) = _last_fwd_bwd(alpha, x_in, y_in, tgt, *p_last)
    loss = lax.psum(loss_acc[0, 0], ("x", "y", "c"))

    d_mod = [[None] * N_MOD for _ in range(depth)]
    d_ln_g = [[None, None] for _ in range(depth)]
    d_ln_b = [[None, None] for _ in range(depth)]
    d_mod[depth - 1][5], d_ln_g[depth - 1][1], d_ln_b[depth - 1][1] = g_gt, g_g, g_b
    gw = dict(dn=[None] * n_a, cf=[None] * n_b, ff1=[None] * depth, ff2=[None] * depth)

    dh, gw["ff1"][depth - 1], gw["ff2"][depth - 1] = _mlp_bwd(res_last, dy, g_ff_w1, g_ff_w2, depth - 1)
    for idx in range(len(subs) - 1, -1, -1):
        x_in, y_in, prm, res = subs[idx]
        i, second = idx // 2, idx % 2
        dx, dy, (g_gt, g_g, g_b, g_sc, g_sh) = _combine_bwd(alpha, x_in, y_in, dx, dh, *prm)
        d_mod[i][5 if second else 2], d_ln_g[i][second], d_ln_b[i][second] = g_gt, g_g, g_b
        nxt_i, nxt_base = (i + 1, 0) if second else (i, 3)
        d_mod[nxt_i][nxt_base + 1], d_mod[nxt_i][nxt_base] = g_sc, g_sh
        if second:
            dh, gw["ff1"][i], gw["ff2"][i] = _mlp_bwd(res, dy, g_ff_w1, g_ff_w2, i)
        elif i % 2 == 0:
            j = i // 2
            dh, gw["dn"][j] = _deltanet_bwd(res, dy, w_dn_in[j], conv_w_f[j], w_dn_out[j])
        else:
            j = i // 2
            dh, gw["cf"][j] = _conformer_bwd(res, dy, w_cf_in[j], dw_w_f[j], cf_ln_g_f[j][None, :],
                                             cf_ln_b_f[j][None, :], w_cf_out[j])
    grad_x, g_sc, g_sh = _modulate_bwd(xs, dx, dh, mod_row(0, 1), mod_row(0, 0))
    d_mod[0][1], d_mod[0][0] = g_sc, g_sh
    d_mod_full = jnp.concatenate([jnp.concatenate(r, axis=1) for r in d_mod], axis=0)

    def dest_blocks_cols(gfull, cols):
        return jnp.moveaxis(gfull[:, :N_DEV * cols].reshape(gfull.shape[0], N_DEV, cols), 1, 0)

    p_dn_w_in = jnp.stack([dest_blocks_cols(gw["dn"][j]["w_in"], dn_in_cols) for j in range(n_a)], axis=1)
    p_dn_w_out = jnp.stack([gw["dn"][j]["w_out"] for j in range(n_a)], axis=1)
    p_cf_w_in = jnp.stack([gw["cf"][j]["w_in"] for j in range(n_b)], axis=1)
    p_cf_w_out = jnp.stack([gw["cf"][j]["w_out"] for j in range(n_b)], axis=1)
    p_ff_w1 = jnp.stack(gw["ff1"], axis=1)
    p_ff_w2 = jnp.stack(gw["ff2"], axis=1)
    r_dn_w_in, r_dn_w_out, r_cf_w_in, r_cf_w_out, r_ff_w1, r_ff_w2 = _exchange(
        [p_dn_w_in, p_dn_w_out, p_cf_w_in, p_cf_w_out, p_ff_w1, p_ff_w2], True, "comm_scatter_grads")

    def stack_rows(lst):
        return jnp.stack(lst, axis=0)

    gs_ln_g = jnp.stack([jnp.concatenate(r, axis=0) for r in d_ln_g], axis=0)
    gs_ln_b = jnp.stack([jnp.concatenate(r, axis=0) for r in d_ln_b], axis=0)
    gs_conv_w = stack_rows([gw["dn"][j]["conv_w"] for j in range(n_a)])
    gs_dw_w = stack_rows([gw["cf"][j]["dw_w"] for j in range(n_b)])
    gs_dw_b = stack_rows([gw["cf"][j]["dw_b"] for j in range(n_b)])
    gs_cf_ln_g = stack_rows([gw["cf"][j]["ln_g"] for j in range(n_b)])
    gs_cf_ln_b = stack_rows([gw["cf"][j]["ln_b"] for j in range(n_b)])
    gs_a_log = stack_rows([_pad_lanes(gw["dn"][j]["a_log"], 0)[0] for j in range(n_a)])
    gs_dt_bias = stack_rows([_pad_lanes(gw["dn"][j]["dt_bias"], 0)[0] for j in range(n_a)])
    gs_norm_w = stack_rows([gw["dn"][j]["norm_w"] for j in range(n_a)])
    small_grads = [gs_ln_g, gs_ln_b, gs_conv_w, gs_dw_w, gs_dw_b, gs_cf_ln_g, gs_cf_ln_b, gs_a_log, gs_dt_bias,
                   gs_norm_w, d_mod_full]
    sg_rows = [_rows(a) for a in small_grads]
    sg_sizes = [a.shape[0] for a in sg_rows]
    (sg_all,) = _exchange([_pad_rows(jnp.concatenate(sg_rows, axis=0))], False, "comm_gather_small_grads")
    sg_offs = [0]
    for z in sg_sizes:
        sg_offs.append(sg_offs[-1] + z)

    def sg(i, shape):
        return sg_all[:, sg_offs[i]:sg_offs[i + 1], :].reshape((N_DEV,) + shape)

    dmod_all = sg(10, (depth, N_MOD * d))
    nl = ada_w.shape[2]
    dmod_cols = lax.dynamic_slice_in_dim(dmod_all, me * nl, nl, axis=2)
    g_ada_w = _ada_bwd(cond_all, jnp.moveaxis(dmod_cols, 0, 1))

    outs = {}

    def run_adamw(key, parts, w, m, v):
        shp = w.shape
        as3 = lambda t: t.reshape((-1,) + shp[-2:]) if t.ndim >= 3 else t.reshape((1,) + shp)
        parts3 = parts.reshape((parts.shape[0],) + as3(w).shape)
        res = _adamw(parts3, as3(w), as3(m), as3(v), "adamw_" + key)
        outs[key] = tuple(r.reshape(shp) for r in res)

    run_adamw("ada_w", g_ada_w[None], ada_w, m_ada_w, v_ada_w)
    run_adamw("dn_w_in", r_dn_w_in, dn_w_in, m_dn_w_in, v_dn_w_in)
    run_adamw("dn_w_out", r_dn_w_out, dn_w_out, m_dn_w_out, v_dn_w_out)
    run_adamw("cf_w_in", r_cf_w_in, cf_w_in, m_cf_w_in, v_cf_w_in)
    run_adamw("cf_w_out", r_cf_w_out, cf_w_out, m_cf_w_out, v_cf_w_out)
    run_adamw("ff_w1", r_ff_w1, ff_w1, m_ff_w1, v_ff_w1)
    run_adamw("ff_w2", r_ff_w2, ff_w2, m_ff_w2, v_ff_w2)

    cgroups = 3 * heads // N_DEV
    shard_parts = [
        _shard_cols(sg(0, (depth, 2, d)), me, 1), _shard_cols(sg(1, (depth, 2, d)), me, 1),
        _shard_cols(sg(2, (n_a, DN_CONV, 3 * hw)), me, cgroups), _shard_cols(sg(3, (n_b, taps, d)), me, 1),
        _shard_cols(sg(4, (n_b, d)), me, 1), _shard_cols(sg(5, (n_b, d)), me, 1), _shard_cols(sg(6, (n_b, d)), me, 1),
    ]
    repl_parts = [sg(7, (n_a, LANES)), sg(8, (n_a, LANES)), sg(9, (n_a, HEAD_DIM)),
                  sg(10, (depth, N_MOD * d)).reshape(N_DEV, -1, LANES)]
    small_parts = shard_parts + repl_parts
    sp_sizes = [a.shape[1] for a in small_parts]
    parts_packed = jnp.concatenate(small_parts, axis=1)
    extra = (-parts_packed.shape[1]) % 8
    parts_packed = jnp.pad(parts_packed, ((0, 0), (0, extra), (0, 0)))

    def pad_heads(t):
        return jnp.pad(t, ((0, 0), (0, LANES - heads)))

    def pack_state(ln_g_, ln_b_, conv_w_, dw_w_, dw_b_, cln_g_, cln_b_, a_log_, dt_b_, norm_w_, ada_b_):
        rows = [_rows(ln_g_), _rows(ln_b_), _rows(conv_w_), _rows(dw_w_), _rows(dw_b_), _rows(cln_g_), _rows(cln_b_),
                pad_heads(a_log_), pad_heads(dt_b_), norm_w_, _rows(ada_b_)]
        return _pad_rows(jnp.concatenate(rows, axis=0))

    w_s = pack_state(ln_g, ln_b, dn_conv_w, cf_dw_w, cf_dw_b, cf_ln_g, cf_ln_b, dn_a_log, dn_dt_bias, dn_norm_w, ada_b)
    m_s = pack_state(m_ln_g, m_ln_b, m_dn_conv_w, m_cf_dw_w, m_cf_dw_b, m_cf_ln_g, m_cf_ln_b, m_dn_a_log,
                     m_dn_dt_bias, m_dn_norm_w, m_ada_b)
    v_s = pack_state(v_ln_g, v_ln_b, v_dn_conv_w, v_cf_dw_w, v_cf_dw_b, v_cf_ln_g, v_cf_ln_b, v_dn_a_log,
                     v_dn_dt_bias, v_dn_norm_w, v_ada_b)
    res_s = _adamw(parts_packed[:, None], w_s[None], m_s[None], v_s[None], "adamw_small")
    sp_offs = [0]
    for z in sp_sizes:
        sp_offs.append(sp_offs[-1] + z)
    small_keys = ["ln_g", "ln_b", "dn_conv_w", "cf_dw_w", "cf_dw_b", "cf_ln_g", "cf_ln_b", "dn_a_log", "dn_dt_bias",
                  "dn_norm_w", "ada_b"]
    small_shapes = [ln_g.shape, ln_b.shape, dn_conv_w.shape, cf_dw_w.shape, cf_dw_b.shape, cf_ln_g.shape,
                    cf_ln_b.shape, dn_a_log.shape, dn_dt_bias.shape, dn_norm_w.shape, ada_b.shape]
    for n, (key, shp) in enumerate(zip(small_keys, small_shapes)):
        vals = []
        for r in res_s:
            piece = r[0, sp_offs[n]:sp_offs[n + 1], :]
            if key in ("dn_a_log", "dn_dt_bias"):
                piece = piece[:, :heads]
            vals.append(piece.reshape(shp))
        outs[key] = tuple(vals)

    order = ["ada_w", "ada_b", "ln_g", "ln_b", "dn_w_in", "dn_conv_w", "dn_a_log", "dn_dt_bias", "dn_norm_w",
             "dn_w_out", "cf_w_in", "cf_dw_w", "cf_dw_b", "cf_ln_g", "cf_ln_b", "cf_w_out", "ff_w1", "ff_w2"]
    result = [loss, grad_x[None]]
    for part in range(4):
        result += [outs[k][part] for k in order]
    return tuple(result)
```

```python
import functools

import jax
import jax.numpy as jnp
from jax import lax
from jax.experimental import pallas as pl
from jax.experimental.pallas import tpu as pltpu

F32 = jnp.float32
MXU_DTYPE = jnp.bfloat16
N_DEV = 8
LANES = 128
HEAD_DIM = 128
CHUNK = 64
DN_CONV = 4
N_MOD = 6
LN_EPS = 1e-5
RMS_EPS = 1e-6
L2_EPS = 1e-6
ADAM_LR = 0.001
ADAM_B1 = 0.9
ADAM_B2 = 0.999
ADAM_EPS = 1e-08
ADAM_WD = 0.01
ADAM_STEP = 10

HI = lax.Precision.HIGHEST
NN = ((1,), (0,))
NT = ((1,), (1,))
TN = ((0,), (0,))

ROW_TILE = 256
CONV_TILE = 256


def _mdot(a, b, dims=NN):
    return lax.dot_general(a.astype(MXU_DTYPE), b.astype(MXU_DTYPE), (dims, ((), ())), preferred_element_type=F32)


def _split3(x):
    hi = x.astype(MXU_DTYPE)
    r1 = x - hi.astype(F32)
    mid = r1.astype(MXU_DTYPE)
    lo = (r1 - mid.astype(F32)).astype(MXU_DTYPE)
    return hi, mid, lo


def _dot01(a, b, dims=NN, mask_first=True):
    d = lambda p, q: lax.dot_general(p, q, (dims, ((), ())), preferred_element_type=F32)
    if mask_first:
        m = a.astype(MXU_DTYPE)
        return sum(d(m, p) for p in _split3(b))
    m = b.astype(MXU_DTYPE)
    return sum(d(p, m) for p in _split3(a))


def _dot3(a, b, dims=NN):
    ah, am, _ = _split3(a)
    bh, bm, _ = _split3(b)
    d = lambda p, q: lax.dot_general(p, q, (dims, ((), ())), preferred_element_type=F32)
    return d(ah, bh) + (d(ah, bm) + d(am, bh))


def _cparams(n):
    return pltpu.CompilerParams(dimension_semantics=("arbitrary",) * n)


def _call(body, *, grid, ins, outs, name, scratch=()):
    res = pl.pallas_call(
        body,
        grid=grid,
        in_specs=[pl.BlockSpec(b, m) for _, b, m in ins],
        out_specs=[pl.BlockSpec(b, m) for _, _, b, m in outs],
        out_shape=[jax.ShapeDtypeStruct(s, d) for s, d, _, _ in outs],
        scratch_shapes=list(scratch),
        name=name,
        compiler_params=_cparams(len(grid)),
    )(*[a for a, _, _ in ins])
    return res


def _tile(n, pref, unit=LANES):
    if n <= pref:
        return n
    t = (pref // unit) * unit
    while t > unit and n % t:
        t -= unit
    assert n % t == 0, (n, pref)
    return t


def _rowmap(fn, rows, consts, row_outs, acc_outs, name):
    rows = [r if isinstance(r, tuple) else (r, r.shape[1], 0) for r in rows]
    s = rows[0][0].shape[0]
    tm = min(ROW_TILE, s)
    nr, nc, no, na = len(rows), len(consts), len(row_outs), len(acc_outs)

    def body(*refs):
        rin, cin = refs[:nr], refs[nr:nr + nc]
        rout, aout = refs[nr + nc:nr + nc + no], refs[nr + nc + no:]
        ro, ao = fn(*[r[...] for r in rin], *[c[...] for c in cin])
        for ref, val in zip(rout, ro):
            ref[...] = val.astype(ref.dtype)
        if na:
            first = pl.program_id(0) == 0

            @pl.when(first)
            def _():
                for ref, val in zip(aout, ao):
                    ref[...] = val

            @pl.when(jnp.logical_not(first))
            def _():
                for ref, val in zip(aout, ao):
                    ref[...] += val

    ins = [(a, (tm, w), functools.partial(lambda i, cb: (i, cb), cb=cb)) for a, w, cb in rows]
    ins += [(c, c.shape, lambda i: (0, 0)) for c in consts]
    outs = [((s, w), d, (tm, w), lambda i: (i, 0)) for w, d in row_outs]
    outs += [(shp, F32, shp, lambda i: (0, 0)) for shp in acc_outs]
    res = _call(body, grid=(s // tm,), ins=ins, outs=outs, name=name)
    return res[:no], res[no:]


def _ln(z, g, b):
    mu = jnp.mean(z, -1, keepdims=True)
    var = jnp.mean(jnp.square(z - mu), -1, keepdims=True)
    return (z - mu) * lax.rsqrt(var + LN_EPS) * g + b


def _combine(alpha, x, y, gt, g, b, sc, sh):
    xn = _ln(alpha * x + (1.0 + gt) * y, g, b)
    return xn, xn * (1.0 + sc) + sh


def _modulate_fwd(x, sc, sh):
    def fn(x, sc, sh):
        return ((x * (1.0 + sc) + sh),), ()

    (h,), _ = _rowmap(fn, [x], [sc, sh], [(x.shape[1], MXU_DTYPE)], [], "modulate_fwd")
    return h


def _modulate_bwd(x, dx, dh, sc, sh):
    d = x.shape[1]

    def fn(x, dx, dh, sc, sh):
        _, vjp = jax.vjp(lambda x, sc, sh: x * (1.0 + sc) + sh, x, sc, sh)
        gx, gsc, gsh = vjp(dh)
        return (dx + gx,), (gsc, gsh)

    (gx,), (gsc, gsh) = _rowmap(fn, [x, dx, dh], [sc, sh], [(d, F32)], [(1, d), (1, d)], "modulate_bwd")
    return gx, gsc, gsh


def _combine_fwd(alpha, x, y, gt, g, b, sc, sh):
    d = x.shape[1]

    def fn(x, y, gt, g, b, sc, sh):
        return _combine(alpha, x, y, gt, g, b, sc, sh), ()

    (xn, h), _ = _rowmap(fn, [x, y], [gt, g, b, sc, sh], [(d, F32), (d, MXU_DTYPE)], [], "combine_fwd")
    return xn, h


def _combine_bwd(alpha, x, y, dxn, dh, gt, g, b, sc, sh):
    d = x.shape[1]

    def fn(x, y, dxn, dh, gt, g, b, sc, sh):
        _, vjp = jax.vjp(functools.partial(_combine, alpha), x, y, gt, g, b, sc, sh)
        gx, gy, ggt, gg, gb, gsc, gsh = vjp((dxn, dh))
        return (gx, gy), (ggt, gg, gb, gsc, gsh)

    (gx, gy), accs = _rowmap(fn, [x, y, dxn, dh], [gt, g, b, sc, sh], [(d, F32), (d, MXU_DTYPE)],
                             [(1, d)] * 5, "combine_bwd")
    return gx, gy, accs


def _last_fwd_bwd(alpha, x, y, tgt, gt, g, b):
    d = x.shape[1]

    def fn(x, y, tgt, gt, g, b):
        xn, vjp = jax.vjp(lambda x, y, gt, g, b: _ln(alpha * x + (1.0 + gt) * y, g, b), x, y, gt, g, b)
        err = xn - tgt
        gx, gy, ggt, gg, gb = vjp(err * (1.0 / d))
        rows = jnp.sum(jnp.square(err), axis=-1, keepdims=True)
        loss = (0.5 / d) * jnp.sum(rows, axis=0, keepdims=True) * jnp.ones((1, LANES), F32)
        return (gx, gy), (loss, ggt, gg, gb)

    (gx, gy), accs = _rowmap(fn, [x, y, tgt], [gt, g, b], [(d, F32), (d, MXU_DTYPE)],
                             [(1, LANES), (1, d), (1, d), (1, d)], "last_fwd_bwd")
    return gx, gy, accs


def _mm_call(a, a_blk, a_map, b, b_blk, b_map, outs, dims, grid, name, relu2=False):
    nk = grid[2]
    n_out = len(outs)

    def body(a_ref, b_ref, *rest):
        out_refs = rest[:n_out]

        def finish(val):
            if relu2:
                out_refs[0][...] = val.astype(out_refs[0].dtype)
                out_refs[1][...] = jnp.square(jnp.maximum(val, 0.0)).astype(out_refs[1].dtype)
            else:
                out_refs[0][...] = val.astype(out_refs[0].dtype)

        p = lax.dot_general(a_ref[...], b_ref[...], (dims, ((), ())), preferred_element_type=F32)
        if nk == 1:
            finish(p)
        else:
            acc = rest[n_out]
            k = pl.program_id(2)

            @pl.when(k == 0)
            def _():
                acc[...] = p

            @pl.when(k > 0)
            def _():
                acc[...] += p

            @pl.when(k == nk - 1)
            def _():
                finish(acc[...])

    out_blk = tuple(x for x in outs[0][2] if x is not None)
    scratch = [pltpu.VMEM(out_blk, F32)] if nk > 1 else []
    return _call(body, grid=grid, ins=[(a, a_blk, a_map), (b, b_blk, b_map)], outs=outs, name=name, scratch=scratch)


def _mm_nn(a, b, out_dtype, name, relu2=False):
    m, kdim = a.shape
    tm = _tile(m, 512, 8)
    tk = _tile(kdim, 1024)
    if b.ndim == 2:
        n = b.shape[1]
        tn = _tile(n, 512)
        b_blk, b_map = (tk, tn), lambda i, j, k: (k, j)
    else:
        g, _, ng = b.shape
        n = g * ng
        tn = _tile(ng, 512)
        b_blk = (None, tk, tn)
        b_map = functools.partial(lambda i, j, k, npg: (j // npg, k, j % npg), npg=ng // tn)
    grid = (m // tm, n // tn, kdim // tk)
    outs = [((m, n), F32 if relu2 else out_dtype, (tm, tn), lambda i, j, k: (i, j))]
    if relu2:
        outs.append(((m, n), out_dtype, (tm, tn), lambda i, j, k: (i, j)))
    res = _mm_call(a, (tm, tk), lambda i, j, k: (i, k), b, b_blk, b_map, outs, NN, grid, name, relu2)
    return res if relu2 else res[0]


def _mm_nt(a, b, out_dtype, name):
    m, n = a.shape
    tm = _tile(m, 512, 8)
    if b.ndim == 2:
        kout = b.shape[0]
        to, tc = _tile(kout, 512), _tile(n, 1024)
        b_blk, b_map = (to, tc), lambda i, j, k: (j, k)
    else:
        _, kout, ng = b.shape
        to, tc = _tile(kout, 512), _tile(ng, 1024)
        b_blk = (None, to, tc)
        b_map = functools.partial(lambda i, j, k, cpg: (k // cpg, j, k % cpg), cpg=ng // tc)
    grid = (m // tm, kout // to, n // tc)
    outs = [((m, kout), out_dtype, (tm, to), lambda i, j, k: (i, j))]
    return _mm_call(a, (tm, tc), lambda i, j, k: (i, k), b, b_blk, b_map, outs, NT, grid, name)[0]


def _mm_tn(a, b, out_dtype, name, split_cols=False):
    m, kdim = a.shape
    n = b.shape[1]
    tc = _tile(m, 2048, 8)
    tk = _tile(kdim, 512)
    if not split_cols:
        tn = _tile(n, 512)
        out = ((kdim, n), out_dtype, (tk, tn), lambda i, j, k: (i, j))
    else:
        ng = n // N_DEV
        tn = _tile(ng, 512)
        out = ((N_DEV, kdim, ng), out_dtype, (None, tk, tn),
               functools.partial(lambda i, j, k, npg: (j // npg, i, j % npg), npg=ng // tn))
    grid = (kdim // tk, n // tn, m // tc)
    return _mm_call(a, (tc, tk), lambda i, j, k: (k, i), b, (tc, tn), lambda i, j, k: (k, j), [out], TN, grid, name)[0]


def _shifted(xa, off, rows):
    if off % 8 == 0:
        return xa[off:off + rows]
    return pltpu.roll(xa, xa.shape[0] - off, 0)[:rows]


def _conv_pad(taps):
    return -(-(taps - 1) // 8) * 8


def _conv_tile(xp_ref, w, i, rows, taps):
    pad = _conv_pad(taps)
    r0 = pl.multiple_of(i * rows, rows)
    xa = xp_ref[pl.ds(r0, rows + pad), :]
    views = [_shifted(xa, pad - (taps - 1) + j, rows) for j in range(taps)]
    acc = w[0:1, :] * views[0]
    for j in range(1, taps):
        acc = acc + w[j:j + 1, :] * views[j]
    return r0, acc, views


def _conv_back_tile(yp_ref, w, i, rows, taps):
    pad = _conv_pad(taps)
    r0 = pl.multiple_of(i * rows, rows)
    ya = yp_ref[pl.ds(r0, rows + pad), :]
    acc = w[taps - 1:taps, :] * ya[:rows]
    for j in range(taps - 1):
        acc = acc + w[j:j + 1, :] * _shifted(ya, taps - 1 - j, rows)
    return r0, acc


def _tap_sums(dy, views, taps):
    row = lax.broadcasted_iota(jnp.int32, (taps, LANES), 0)
    acc = jnp.zeros((taps, LANES), F32)
    for j in range(taps):
        acc = acc + jnp.where(row == j, jnp.sum(dy * views[j], axis=0, keepdims=True), 0.0)
    return acc


def _silu_l2(xc, l2):
    a = jax.nn.silu(xc)
    if l2:
        a = a * lax.rsqrt(jnp.sum(a * a, axis=-1, keepdims=True) + L2_EPS)
    return a


def _dn_conv_fwd(proj, conv_w, c0, nblk, l2, name):
    s = proj.shape[0]
    pad = _conv_pad(DN_CONV)
    rows = min(CONV_TILE, s)

    def body(x_ref, w_ref, o_ref, xp):
        xp[0:pad, :] = jnp.zeros((pad, LANES), F32)
        xp[pad:, :] = x_ref[...]
        w = w_ref[...]

        def tile(i, c):
            r0, acc, _ = _conv_tile(xp, w, i, rows, DN_CONV)
            o_ref[pl.ds(r0, rows), :] = _silu_l2(acc, l2)
            return c

        lax.fori_loop(0, s // rows, tile, 0)

    return _call(body, grid=(nblk,),
                 ins=[(proj, (s, LANES), lambda c: (0, c0 + c)), (conv_w, (DN_CONV, LANES), lambda c: (0, c0 + c))],
                 outs=[((nblk, s, LANES), F32, (None, s, LANES), lambda c: (c, 0, 0))],
                 name=name, scratch=[pltpu.VMEM((s + pad, LANES), F32)])[0]


def _dn_conv_bwd(proj, conv_w, da, c0, nblk, l2, name):
    s = proj.shape[0]
    pad = _conv_pad(DN_CONV)
    rows = min(CONV_TILE, s)

    def body(x_ref, w_ref, da_ref, dx_ref, dw_ref, xp, yp):
        xp[0:pad, :] = jnp.zeros((pad, LANES), F32)
        xp[pad:, :] = x_ref[...]
        yp[s:, :] = jnp.zeros((pad, LANES), F32)
        w = w_ref[...]

        def tile(i, dw):
            r0, acc, views = _conv_tile(xp, w, i, rows, DN_CONV)
            _, vjp = jax.vjp(functools.partial(_silu_l2, l2=l2), acc)
            (dxc,) = vjp(da_ref[pl.ds(r0, rows), :])
            yp[pl.ds(r0, rows), :] = dxc
            return dw + _tap_sums(dxc, views, DN_CONV)

        dw_ref[...] = lax.fori_loop(0, s // rows, tile, jnp.zeros((DN_CONV, LANES), F32))

        def tile2(i, c):
            r0, acc = _conv_back_tile(yp, w, i, rows, DN_CONV)
            dx_ref[pl.ds(r0, rows), :] = acc.astype(dx_ref.dtype)
            return c

        lax.fori_loop(0, s // rows, tile2, 0)

    return _call(body, grid=(nblk,),
                 ins=[(proj, (s, LANES), lambda c: (0, c0 + c)), (conv_w, (DN_CONV, LANES), lambda c: (0, c0 + c)),
                      (da, (None, s, LANES), lambda c: (c, 0, 0))],
                 outs=[((s, nblk * LANES), MXU_DTYPE, (s, LANES), lambda c: (0, c)),
                       ((DN_CONV, nblk * LANES), F32, (DN_CONV, LANES), lambda c: (0, c))],
                 name=name, scratch=[pltpu.VMEM((s + pad, LANES), F32), pltpu.VMEM((s + pad, LANES), F32)])


def _cf_conv_fwd(vg, dw_w, dw_b):
    s, c2 = vg.shape
    ch = c2 // 2
    nblk = ch // LANES
    taps = dw_w.shape[0]
    pad = _conv_pad(taps)
    rows = min(CONV_TILE, s)

    def body(v_ref, g_ref, w_ref, b_ref, o_ref, xp):
        xp[0:pad, :] = jnp.zeros((pad, LANES), F32)
        xp[pad:, :] = v_ref[...] * jax.nn.sigmoid(g_ref[...])
        w = w_ref[...]
        bias = b_ref[...]

        def tile(i, c):
            r0, acc, _ = _conv_tile(xp, w, i, rows, taps)
            o_ref[pl.ds(r0, rows), :] = acc + bias
            return c

        lax.fori_loop(0, s // rows, tile, 0)

    return _call(body, grid=(nblk,),
                 ins=[(vg, (s, LANES), lambda c: (0, c)), (vg, (s, LANES), lambda c: (0, nblk + c)),
                      (dw_w, (taps, LANES), lambda c: (0, c)), (dw_b, (1, LANES), lambda c: (0, c))],
                 outs=[((s, ch), F32, (s, LANES), lambda c: (0, c))],
                 name="cf_conv_fwd", scratch=[pltpu.VMEM((s + pad, LANES), F32)])[0]


def _cf_conv_bwd(vg, dw_w, du):
    s, c2 = vg.shape
    ch = c2 // 2
    nblk = ch // LANES
    taps = dw_w.shape[0]
    pad = _conv_pad(taps)
    rows = min(CONV_TILE, s)

    def body(v_ref, g_ref, w_ref, du_ref, dv_ref, dg_ref, dw_ref, db_ref, xp, yp):
        sig = jax.nn.sigmoid(g_ref[...])
        xp[0:pad, :] = jnp.zeros((pad, LANES), F32)
        xp[pad:, :] = v_ref[...] * sig
        yp[0:s, :] = du_ref[...]
        yp[s:, :] = jnp.zeros((pad, LANES), F32)
        w = w_ref[...]
        db_ref[...] = jnp.sum(du_ref[...], axis=0, keepdims=True)

        def tile(i, dw):
            r0, _, views = _conv_tile(xp, w, i, rows, taps)
            return dw + _tap_sums(du_ref[pl.ds(r0, rows), :], views, taps)

        dw_ref[...] = lax.fori_loop(0, s // rows, tile, jnp.zeros((taps, LANES), F32))

        def tile2(i, c):
            r0, du0 = _conv_back_tile(yp, w, i, rows, taps)
            val = v_ref[pl.ds(r0, rows), :]
            sg = jax.nn.sigmoid(g_ref[pl.ds(r0, rows), :])
            dv_ref[pl.ds(r0, rows), :] = (du0 * sg).astype(dv_ref.dtype)
            dg_ref[pl.ds(r0, rows), :] = (du0 * val * sg * (1.0 - sg)).astype(dg_ref.dtype)
            return c

        lax.fori_loop(0, s // rows, tile2, 0)

    return _call(body, grid=(nblk,),
                 ins=[(vg, (s, LANES), lambda c: (0, c)), (vg, (s, LANES), lambda c: (0, nblk + c)),
                      (dw_w, (taps, LANES), lambda c: (0, c)), (du, (s, LANES), lambda c: (0, c))],
                 outs=[((s, ch), MXU_DTYPE, (s, LANES), lambda c: (0, c)),
                       ((s, ch), MXU_DTYPE, (s, LANES), lambda c: (0, c)),
                       ((taps, ch), F32, (taps, LANES), lambda c: (0, c)),
                       ((1, ch), F32, (1, LANES), lambda c: (0, c))],
                 name="cf_conv_bwd", scratch=[pltpu.VMEM((s + pad, LANES), F32), pltpu.VMEM((s + pad, LANES), F32)])


def _masks():
    r = lax.broadcasted_iota(jnp.int32, (CHUNK, CHUNK), 0)
    c = lax.broadcasted_iota(jnp.int32, (CHUNK, CHUNK), 1)
    return r >= c, r > c, r <= c


def _tri_inv(a):
    r = lax.broadcasted_iota(jnp.int32, (CHUNK, CHUNK), 0)
    c = lax.broadcasted_iota(jnp.int32, (CHUNK, CHUNK), 1)
    t = jnp.where(r == c, 1.0, 0.0).astype(F32)
    b = 1
    while b < CHUNK:
        q = jnp.where((r // (2 * b) == c // (2 * b)) & (r % (2 * b) >= b) & (c % (2 * b) < b), a, 0.0)
        t = t - _dot3(_dot3(t, q), t)
        b *= 2
    return t


def _chunk_decay(g):
    causal, _, upper = _masks()
    gb = jnp.broadcast_to(g, (CHUNK, CHUNK))
    gam_r = _dot01(jnp.where(causal, 1.0, 0.0), gb)
    gam_s = _dot01(jnp.ones((CHUNK, CHUNK), F32), jnp.where(upper, gb, 0.0))
    dm = jnp.where(causal, jnp.exp(jnp.where(causal, gam_r - gam_s, 0.0)), 0.0)
    return gam_r[:, 0:1], dm


def _chunk_prep(q, k, v, beta, g):
    _, strict, _ = _masks()
    gam, dm = _chunk_decay(g)
    eg = jnp.exp(gam)
    kb = k * beta
    a = jnp.where(strict, _mdot(kb, k, NT) * dm, 0.0)
    t = _tri_inv(a)
    u = _mdot(t, v * beta)
    w = _mdot(t, kb * eg)
    aqk = _mdot(q * (HEAD_DIM ** -0.5), k, NT) * dm
    return u, w, aqk, t, gam


def _chunk_prep_bwd(q, k, v, beta, g, t, du, dw, daqk, dqd, dkd, dgl):
    causal, strict, _ = _masks()
    scale = HEAD_DIM ** -0.5
    gam, dm = _chunk_decay(g)
    eg = jnp.exp(gam)
    gam_last = gam[CHUNK - 1:CHUNK, :]
    rr = jnp.exp(gam_last - gam)
    kb = k * beta
    qs = q * scale
    kk = _mdot(kb, k, NT)
    a = jnp.where(strict, kk * dm, 0.0)
    aqk = _mdot(qs, k, NT) * dm
    vb = v * beta
    kbe = kb * eg

    dt = _mdot(du, vb, NT) + _mdot(dw, kbe, NT)
    dvb = _mdot(t, du, TN)
    dkbe = _mdot(t, dw, TN)
    da = jnp.where(strict, -_dot3(_dot3(t, dt, TN), t, NT), 0.0)
    dkk = da * dm
    dqk = daqk * dm
    ddiff = da * a + daqk * aqk
    dkb = _mdot(dkk, k) + dkbe * eg
    dk = _mdot(dkk, kb, TN) + _mdot(dqk, qs, TN) + dkb * beta + dkd * rr
    dq = (_mdot(dqk, k) + dqd * eg) * scale
    dbeta = jnp.sum(dkb * k, axis=-1, keepdims=True) + jnp.sum(dvb * v, axis=-1, keepdims=True)
    dv = dvb * beta
    deg = jnp.sum(dkbe * kb, axis=-1, keepdims=True) + jnp.sum(dqd * qs, axis=-1, keepdims=True)
    drr = jnp.sum(dkd * k, axis=-1, keepdims=True)
    colsum = _dot01(ddiff, jnp.ones((CHUNK, LANES), F32), TN, mask_first=False)[:, 0:1]
    dgam = deg * eg - drr * rr + jnp.sum(ddiff, axis=-1, keepdims=True) - colsum
    dgam_last = jnp.sum(drr * rr, axis=0, keepdims=True) + dgl[0:1, :] * jnp.exp(gam_last)
    row = lax.broadcasted_iota(jnp.int32, (CHUNK, 1), 0)
    dgam = dgam + jnp.where(row == CHUNK - 1, dgam_last, 0.0)
    dg = _dot01(jnp.where(causal, 1.0, 0.0), jnp.broadcast_to(dgam, (CHUNK, LANES)), TN)[:, 0:1]
    return dq, dk, dv, dbeta, dg


def _prep_group(s):
    nch = s // CHUNK
    return 4 if nch % 4 == 0 else 1


def _dn_prep(q, k, v, beta, g):
    h, s, _ = q.shape
    cb = _prep_group(s)
    rb = cb * CHUNK

    def body(q_ref, k_ref, v_ref, b_ref, g_ref, u_ref, w_ref, a_ref, t_ref, gam_ref):
        for i in range(cb):
            sl = slice(i * CHUNK, (i + 1) * CHUNK)
            u, w, aqk, t, gam = _chunk_prep(q_ref[sl, :], k_ref[sl, :], v_ref[sl, :], b_ref[sl, :], g_ref[sl, :])
            u_ref[sl, :] = u
            w_ref[sl, :] = w
            a_ref[sl, :] = aqk
            t_ref[sl, :] = t
            gam_ref[sl, :] = gam

    big = lambda x: (x, (None, rb, HEAD_DIM), lambda hh, n: (hh, n, 0))
    col = lambda x: (x, (None, rb, 1), lambda hh, n: (hh, n, 0))
    o_big = ((h, s, HEAD_DIM), F32, (None, rb, HEAD_DIM), lambda hh, n: (hh, n, 0))
    o_sq = ((h, s, CHUNK), F32, (None, rb, CHUNK), lambda hh, n: (hh, n, 0))
    o_col = ((h, s, 1), F32, (None, rb, 1), lambda hh, n: (hh, n, 0))
    return _call(body, grid=(h, s // rb), ins=[big(q), big(k), big(v), col(beta), col(g)],
                 outs=[o_big, o_big, o_sq, o_sq, o_col], name="dn_prep")


def _dn_prep_bwd(q, k, v, beta, g, t, du, dw, daqk, dqd, dkd, dgl):
    h, s, _ = q.shape
    cb = _prep_group(s)
    rb = cb * CHUNK

    def body(q_ref, k_ref, v_ref, b_ref, g_ref, t_ref, du_ref, dw_ref, da_ref, dqd_ref, dkd_ref, dgl_ref,
             dq_ref, dk_ref, dv_ref, db_ref, dg_ref):
        for i in range(cb):
            sl = slice(i * CHUNK, (i + 1) * CHUNK)
            dq, dk, dv, dbeta, dg = _chunk_prep_bwd(
                q_ref[sl, :], k_ref[sl, :], v_ref[sl, :], b_ref[sl, :], g_ref[sl, :], t_ref[sl, :],
                du_ref[sl, :], dw_ref[sl, :], da_ref[sl, :], dqd_ref[sl, :], dkd_ref[sl, :], dgl_ref[sl, :])
            dq_ref[sl, :] = dq
            dk_ref[sl, :] = dk
            dv_ref[sl, :] = dv
            db_ref[sl, :] = dbeta
            dg_ref[sl, :] = dg

    big = lambda x: (x, (None, rb, HEAD_DIM), lambda hh, n: (hh, n, 0))
    sq = lambda x: (x, (None, rb, CHUNK), lambda hh, n: (hh, n, 0))
    col = lambda x: (x, (None, rb, 1), lambda hh, n: (hh, n, 0))
    o_big = ((h, s, HEAD_DIM), F32, (None, rb, HEAD_DIM), lambda hh, n: (hh, n, 0))
    o_col = ((h, s, 1), F32, (None, rb, 1), lambda hh, n: (hh, n, 0))
    return _call(body, grid=(h, s // rb),
                 ins=[big(q), big(k), big(v), col(beta), col(g), sq(t), big(du), big(dw), sq(daqk), big(dqd), big(dkd),
                      col(dgl)],
                 outs=[o_big, o_big, o_big, o_col, o_col], name="dn_prep_bwd")


def _chunk_scaled(q, k, gam):
    gam_last = gam[CHUNK - 1:CHUNK, :]
    q_dec = q * (HEAD_DIM ** -0.5) * jnp.exp(gam)
    k_dec = k * jnp.exp(gam_last - gam)
    return q_dec, k_dec, jnp.exp(gam_last)


def _dn_scan(q, k, u, w, aqk, gam):
    h, s, _ = q.shape
    nch = s // CHUNK

    def body(q_ref, k_ref, u_ref, w_ref, a_ref, gam_ref, o_ref, st_ref, state):
        @pl.when(pl.program_id(0) == 0)
        def _():
            state[...] = jnp.zeros_like(state)

        for hh in range(h):
            s0 = state[hh]
            st_ref[hh] = s0
            q_dec, k_dec, gl = _chunk_scaled(q_ref[hh], k_ref[hh], gam_ref[hh])
            v_new = u_ref[hh] - _mdot(w_ref[hh], s0)
            o_ref[:, hh * HEAD_DIM:(hh + 1) * HEAD_DIM] = _mdot(q_dec, s0) + _mdot(a_ref[hh], v_new)
            state[hh] = s0 * gl + _mdot(k_dec, v_new, TN)

    big = lambda x: (x, (h, CHUNK, HEAD_DIM), lambda n: (0, n, 0))
    return _call(body, grid=(nch,),
                 ins=[big(q), big(k), big(u), big(w), (aqk, (h, CHUNK, CHUNK), lambda n: (0, n, 0)),
                      (gam, (h, CHUNK, 1), lambda n: (0, n, 0))],
                 outs=[((s, h * HEAD_DIM), F32, (CHUNK, h * HEAD_DIM), lambda n: (n, 0)),
                       ((nch, h, HEAD_DIM, HEAD_DIM), F32, (None, h, HEAD_DIM, HEAD_DIM), lambda n: (n, 0, 0, 0))],
                 name="dn_scan", scratch=[pltpu.VMEM((h, HEAD_DIM, HEAD_DIM), F32)])


def _dn_scan_bwd(q, k, u, w, aqk, gam, states, do):
    h, s, _ = q.shape
    nch = s // CHUNK

    def body(q_ref, k_ref, u_ref, w_ref, a_ref, gam_ref, st_ref, do_ref,
             du_ref, dw_ref, da_ref, dqd_ref, dkd_ref, dgl_ref, dstate):
        @pl.when(pl.program_id(0) == 0)
        def _():
            dstate[...] = jnp.zeros_like(dstate)

        for hh in range(h):
            s0 = st_ref[hh]
            ds = dstate[hh]
            doh = do_ref[:, hh * HEAD_DIM:(hh + 1) * HEAD_DIM]
            wv = w_ref[hh]
            av = a_ref[hh]
            q_dec, k_dec, gl = _chunk_scaled(q_ref[hh], k_ref[hh], gam_ref[hh])
            v_new = u_ref[hh] - _mdot(wv, s0)
            dv_new = _mdot(av, doh, TN) + _mdot(k_dec, ds)
            du_ref[hh] = dv_new
            dw_ref[hh] = -_mdot(dv_new, s0, NT)
            da_ref[hh] = _mdot(doh, v_new, NT)
            dqd_ref[hh] = _mdot(doh, s0, NT)
            dkd_ref[hh] = _mdot(v_new, ds, NT)
            tot = jnp.sum(jnp.sum(s0 * ds, axis=-1, keepdims=True), axis=0, keepdims=True)
            dgl_ref[hh] = jnp.broadcast_to(tot, (CHUNK, 1))
            dstate[hh] = ds * gl + _mdot(q_dec, doh, TN) - _mdot(wv, dv_new, TN)

    rev = lambda n: (0, nch - 1 - n, 0)
    big = lambda x: (x, (h, CHUNK, HEAD_DIM), rev)
    o_big = ((h, s, HEAD_DIM), F32, (h, CHUNK, HEAD_DIM), rev)
    return _call(body, grid=(nch,),
                 ins=[big(q), big(k), big(u), big(w), (aqk, (h, CHUNK, CHUNK), rev), (gam, (h, CHUNK, 1), rev),
                      (states, (None, h, HEAD_DIM, HEAD_DIM), lambda n: (nch - 1 - n, 0, 0, 0)),
                      (do, (CHUNK, h * HEAD_DIM), lambda n: (nch - 1 - n, 0))],
                 outs=[o_big, o_big, ((h, s, CHUNK), F32, (h, CHUNK, CHUNK), rev), o_big, o_big,
                       ((h, s, 1), F32, (h, CHUNK, 1), rev)],
                 name="dn_scan_bwd", scratch=[pltpu.VMEM((h, HEAD_DIM, HEAD_DIM), F32)])


def _gates(x, a_log, dt_b, h):
    lane = lax.broadcasted_iota(jnp.int32, x.shape, 1)
    return jnp.where(lane < h, jax.nn.sigmoid(x), -jnp.exp(a_log) * jax.nn.softplus(x + dt_b))


def _head_out(oh, zh, nw):
    on = oh * lax.rsqrt(jnp.mean(oh * oh, axis=-1, keepdims=True) + RMS_EPS) * nw
    return on * jax.nn.silu(zh)


def _to_heads(x, h):
    return jnp.transpose(x[:, :h])[:, :, None]


def _pad_lanes(x, lo):
    return jnp.zeros((1, LANES), F32).at[0, lo:lo + x.shape[0]].set(x)


def _deltanet_fwd(hin, get_w_in, conv_w, a_log, dt_bias, norm_w, get_w_out):
    h = a_log.shape[0]
    hw = h * HEAD_DIM
    w_in = get_w_in(hin)
    proj = _mm_nn(hin, w_in, F32, "dn_proj")
    q = _dn_conv_fwd(proj, conv_w, 0, h, True, "dn_conv_q")
    k = _dn_conv_fwd(proj, conv_w, h, h, True, "dn_conv_k")
    v = _dn_conv_fwd(proj, conv_w, 2 * h, h, False, "dn_conv_v")
    alp, dtp = _pad_lanes(a_log, h), _pad_lanes(dt_bias, h)

    def gates_fn(x, al, db):
        return (_gates(x, al, db, h),), ()

    (bg,), _ = _rowmap(gates_fn, [(proj, LANES, 4 * h)], [alp, dtp], [(LANES, F32)], [], "dn_gates")
    beta, g = _to_heads(bg, h), _to_heads(bg[:, h:], h)
    u, w, aqk, t, gam = _dn_prep(q, k, v, beta, g)
    o, states = _dn_scan(q, k, u, w, aqk, gam)
    nw = norm_w[None, :]

    def out_fn(o, z, nw):
        parts = [_head_out(o[:, i * HEAD_DIM:(i + 1) * HEAD_DIM], z[:, i * HEAD_DIM:(i + 1) * HEAD_DIM], nw)
                 for i in range(h)]
        return (jnp.concatenate(parts, axis=-1),), ()

    (og,), _ = _rowmap(out_fn, [o, (proj, hw, 3)], [nw], [(hw, MXU_DTYPE)], [], "dn_out")
    w_out = get_w_out(og)
    y = _mm_nn(og, w_out, F32, "dn_y")
    return y, (hin, proj, q, k, v, beta, g, u, w, aqk, t, gam, states, o, og, alp, dtp, nw, w_in, w_out)


def _deltanet_bwd(res, dy, conv_w):
    hin, proj, q, k, v, beta, g, u, w, aqk, t, gam, states, o, og, alp, dtp, nw, w_in, w_out = res
    h = q.shape[0]
    hw = h * HEAD_DIM
    s = hin.shape[0]
    d_w_out = _mm_tn(og, dy, MXU_DTYPE, "dn_dwout")
    dog = _mm_nt(dy, w_out, F32, "dn_dog")

    def out_bwd(o, z, dog, nw):
        dos, dzs = [], []
        dn = jnp.zeros((1, HEAD_DIM), F32)
        for i in range(h):
            sl = slice(i * HEAD_DIM, (i + 1) * HEAD_DIM)
            _, vjp = jax.vjp(_head_out, o[:, sl], z[:, sl], nw)
            a, b, c = vjp(dog[:, sl])
            dos.append(a)
            dzs.append(b)
            dn = dn + c
        return (jnp.concatenate(dos, axis=-1), jnp.concatenate(dzs, axis=-1)), (dn,)

    (do, dz), (d_norm_w,) = _rowmap(out_bwd, [o, (proj, hw, 3), dog], [nw], [(hw, F32), (hw, MXU_DTYPE)],
                                    [(1, HEAD_DIM)], "dn_out_bwd")
    du, dw, daqk, dqd, dkd, dgl = _dn_scan_bwd(q, k, u, w, aqk, gam, states, do)
    dq, dk, dv, dbeta, dg = _dn_prep_bwd(q, k, v, beta, g, t, du, dw, daqk, dqd, dkd, dgl)
    dpq, dwq = _dn_conv_bwd(proj, conv_w, dq, 0, h, True, "dn_conv_q_bwd")
    dpk, dwk = _dn_conv_bwd(proj, conv_w, dk, h, h, True, "dn_conv_k_bwd")
    dpv, dwv = _dn_conv_bwd(proj, conv_w, dv, 2 * h, h, False, "dn_conv_v_bwd")
    dbg = jnp.concatenate([jnp.transpose(dbeta[:, :, 0]), jnp.transpose(dg[:, :, 0]),
                           jnp.zeros((s, LANES - 2 * h), F32)], axis=1)

    def gates_bwd(x, dbg, al, db):
        _, vjp = jax.vjp(functools.partial(_gates, h=h), x, al, db)
        gx, gal, gdb = vjp(dbg)
        return (gx,), (gal, gdb)

    (dba,), (d_alp, d_dtp) = _rowmap(gates_bwd, [(proj, LANES, 4 * h), dbg], [alp, dtp], [(LANES, MXU_DTYPE)],
                                     [(1, LANES), (1, LANES)], "dn_gates_bwd")
    dproj = jnp.concatenate([dpq, dpk, dpv, dz, dba], axis=1)
    d_w_in = _mm_tn(hin, dproj, MXU_DTYPE, "dn_dwin")
    dh = _mm_nt(dproj, w_in, F32, "dn_dh")
    d_conv_w = jnp.concatenate([dwq, dwk, dwv], axis=1)
    return dh, dict(w_in=d_w_in, w_out=d_w_out, conv_w=d_conv_w, a_log=d_alp[0, h:2 * h], dt_bias=d_dtp[0, h:2 * h],
                    norm_w=d_norm_w[0])


def _ln_silu(u, g, b):
    return jax.nn.silu(_ln(u, g, b))


def _conformer_fwd(hin, get_w_in, dw_w, dw_b, ln_g, ln_b, get_w_out):
    w_in = get_w_in(hin)
    vg = _mm_nn(hin, w_in, F32, "cf_vg")
    u1 = _cf_conv_fwd(vg, dw_w, dw_b)
    ch = u1.shape[1]

    def fn(u, g, b):
        return (_ln_silu(u, g, b),), ()

    (u2,), _ = _rowmap(fn, [u1], [ln_g, ln_b], [(ch, MXU_DTYPE)], [], "cf_ln")
    w_out = get_w_out(u2)
    y = _mm_nn(u2, w_out, F32, "cf_y")
    return y, (hin, vg, u1, u2, w_in, w_out)


def _conformer_bwd(res, dy, dw_w, ln_g, ln_b):
    hin, vg, u1, u2, w_in, w_out = res
    ch = u1.shape[1]
    d_w_out = _mm_tn(u2, dy, MXU_DTYPE, "cf_dwout")
    du2 = _mm_nt(dy, w_out, F32, "cf_du2")

    def fn(u, du2, g, b):
        _, vjp = jax.vjp(_ln_silu, u, g, b)
        gu, gg, gb = vjp(du2)
        return (gu,), (gg, gb)

    (du1,), (d_ln_g, d_ln_b) = _rowmap(fn, [u1, du2], [ln_g, ln_b], [(ch, F32)], [(1, ch), (1, ch)], "cf_ln_bwd")
    dval, dgate, d_dw_w, d_dw_b = _cf_conv_bwd(vg, dw_w, du1)
    dvg = jnp.concatenate([dval, dgate], axis=1)
    d_w_in = _mm_tn(hin, dvg, MXU_DTYPE, "cf_dwin", split_cols=True)
    dh = _mm_nt(dvg, w_in, F32, "cf_dh")
    return dh, dict(w_in=d_w_in, w_out=d_w_out, dw_w=d_dw_w, dw_b=d_dw_b[0], ln_g=d_ln_g[0], ln_b=d_ln_b[0])


def _mlp_fwd(hin, get_w1, get_w2):
    w1 = get_w1(hin)
    a, r = _mm_nn(hin, w1, MXU_DTYPE, "ff_a", relu2=True)
    w2 = get_w2(r)
    m = _mm_nn(r, w2, F32, "ff_m")
    return m, (hin, a, r, w1, w2)


def _mlp_bwd(res, dm):
    hin, a, r, w1, w2 = res
    f = a.shape[1]
    d_w2 = _mm_tn(r, dm, MXU_DTYPE, "ff_dw2")
    dr = _mm_nt(dm, w2, F32, "ff_dr")

    def fn(a, dr):
        return ((dr * 2.0 * jnp.maximum(a, 0.0)),), ()

    (da,), _ = _rowmap(fn, [a, dr], [], [(f, MXU_DTYPE)], [], "ff_da")
    d_w1 = _mm_tn(hin, da, MXU_DTYPE, "ff_dw1", split_cols=True)
    dh = _mm_nt(da, w1, F32, "ff_dh")
    return dh, d_w1, d_w2


def _ada_fwd(c_all, ada_w):
    depth, d, nl = ada_w.shape
    tn = _tile(nl, 256)

    def body(c_ref, w_ref, o_ref, cond_ref):
        cond = jax.nn.silu(c_ref[...]).astype(MXU_DTYPE)
        cond_ref[...] = cond
        o_ref[...] = lax.dot_general(cond, w_ref[...].astype(MXU_DTYPE), (NN, ((), ())), preferred_element_type=F32)

    return _call(body, grid=(depth, nl // tn),
                 ins=[(c_all, c_all.shape, lambda l, j: (0, 0)), (ada_w, (None, d, tn), lambda l, j: (l, 0, j))],
                 outs=[((depth, N_DEV, nl), F32, (None, N_DEV, tn), lambda l, j: (l, 0, j)),
                       (c_all.shape, MXU_DTYPE, c_all.shape, lambda l, j: (0, 0))],
                 name="ada_fwd")


def _ada_bwd(cond_all, dmod_cols):
    depth, _, nl = dmod_cols.shape
    d = cond_all.shape[1]
    tn = _tile(nl, 256)

    def body(c_ref, g_ref, o_ref):
        o_ref[...] = lax.dot_general(c_ref[...], g_ref[...].astype(MXU_DTYPE), (TN, ((), ())),
                                     preferred_element_type=F32)

    return _call(body, grid=(depth, nl // tn),
                 ins=[(cond_all, cond_all.shape, lambda l, j: (0, 0)), (dmod_cols, (None, N_DEV, tn), lambda l, j: (l, 0, j))],
                 outs=[((depth, d, nl), F32, (None, d, tn), lambda l, j: (l, 0, j))], name="ada_bwd")[0]


def _peers():
    x, y, c = lax.axis_index("x"), lax.axis_index("y"), lax.axis_index("c")
    peers = []
    for k in range(1, N_DEV):
        px = 1 - x if k & 4 else x
        py = 1 - y if k & 2 else y
        pc = 1 - c if k & 1 else c
        peers.append(((px, py, pc), 4 * px + 2 * py + pc))
    return 4 * x + 2 * y + c, peers


_HBM = pl.BlockSpec(memory_space=pltpu.HBM)
_SEM = pl.BlockSpec(memory_space=pltpu.SEMAPHORE)
_ANY = pl.BlockSpec(memory_space=pl.ANY)
_EFFECT = pltpu.SideEffectType.DATAFLOW_SIDE_EFFECTING


def _xfer_start(srcs, lands, scatter, after, name):
    nt = len(srcs)

    def body(*refs):
        src, land = refs[:nt], refs[nt:2 * nt]
        sems = refs[2 * nt + 1:4 * nt + 1]
        token = refs[-1]
        me, peers = _peers()
        for t in range(nt):
            for k, (pid, plin) in enumerate(peers):
                pltpu.make_async_remote_copy(
                    src_ref=src[t].at[plin] if scatter else src[t], dst_ref=land[t].at[me],
                    send_sem=sems[2 * t].at[k], recv_sem=sems[2 * t + 1].at[k],
                    device_id=pid, device_id_type=pl.DeviceIdType.MESH).start()
        token[...] = jnp.zeros_like(token)

    out_shape = [pltpu.SemaphoreType.DMA((N_DEV - 1,)) for _ in range(2 * nt)]
    out_shape += [pltpu.HBM(a.shape, a.dtype) for a in list(srcs) + list(lands)]
    out_shape += [jax.ShapeDtypeStruct((8, LANES), F32)]
    res = pl.pallas_call(
        body, name=name, out_shape=out_shape,
        in_specs=[_HBM] * (2 * nt) + [_ANY],
        out_specs=[_SEM] * (2 * nt) + [_HBM] * (2 * nt) + [pl.BlockSpec(memory_space=pltpu.VMEM)],
        input_output_aliases={i: 2 * nt + i for i in range(2 * nt)},
        compiler_params=pltpu.CompilerParams(has_side_effects=_EFFECT),
    )(*[pltpu.with_memory_space_constraint(a, pltpu.HBM) for a in list(srcs) + list(lands)], after)
    sems, thru = res[:2 * nt], res[2 * nt:4 * nt]
    return [(sems[2 * t], sems[2 * t + 1], thru[t], thru[nt + t]) for t in range(nt)], res[-1]


def _xfer_wait(handle, scatter, after, name):
    send, recv, src, land = handle

    def body(src_ref, land_ref, send_sem, recv_sem, after_ref, src_dead, land_out):
        _, peers = _peers()
        for k, (pid, plin) in enumerate(peers):
            cp = pltpu.make_async_remote_copy(
                src_ref=src_ref.at[plin] if scatter else src_ref, dst_ref=land_ref.at[plin],
                send_sem=send_sem.at[k], recv_sem=recv_sem.at[k],
                device_id=pid, device_id_type=pl.DeviceIdType.MESH)
            cp.wait_send()
            cp.wait_recv()

    return pl.pallas_call(
        body, name=name, out_shape=(pltpu.HBM(src.shape, src.dtype), pltpu.HBM(land.shape, land.dtype)),
        in_specs=(_HBM, _HBM, _SEM, _SEM, _ANY), out_specs=(_HBM, _HBM), input_output_aliases={0: 0, 1: 1},
        compiler_params=pltpu.CompilerParams(has_side_effects=_EFFECT),
    )(src, land, send, recv, after)[1]


def _landing(x, me):
    return lax.dynamic_update_slice(lax.empty((N_DEV,) + x.shape, x.dtype), x[None], (me,) + (0,) * x.ndim)


def _landing_scatter(p, me):
    own = lax.dynamic_index_in_dim(p, me, axis=0, keepdims=True)
    return lax.dynamic_update_slice(lax.empty(p.shape, p.dtype), own, (me,) + (0,) * (p.ndim - 1))


def _exchange(arrs, scatter, name):
    nt = len(arrs)
    out_shape = [jax.ShapeDtypeStruct(a.shape if scatter else (N_DEV,) + a.shape, a.dtype) for a in arrs]

    def body(*refs):
        ins, outs = refs[:nt], refs[nt:2 * nt]
        send, recv, loc = refs[2 * nt:]
        me, peers = _peers()
        copies = []
        for t in range(nt):
            own = pltpu.make_async_copy(ins[t].at[me] if scatter else ins[t], outs[t].at[me], loc.at[t])
            own.start()
            copies.append(own)
            for k, (pid, plin) in enumerate(peers):
                cp = pltpu.make_async_remote_copy(
                    src_ref=ins[t].at[plin] if scatter else ins[t], dst_ref=outs[t].at[me],
                    send_sem=send.at[t, k], recv_sem=recv.at[t, k],
                    device_id=pid, device_id_type=pl.DeviceIdType.MESH)
                cp.start()
                copies.append(cp)
        for cp in copies:
            cp.wait()

    any_spec = pl.BlockSpec(memory_space=pl.ANY)
    return pl.pallas_call(
        body, out_shape=out_shape, in_specs=[any_spec] * nt, out_specs=[any_spec] * nt,
        scratch_shapes=[pltpu.SemaphoreType.DMA((nt, N_DEV - 1)), pltpu.SemaphoreType.DMA((nt, N_DEV - 1)),
                        pltpu.SemaphoreType.DMA((nt,))],
        name=name)(*arrs)


def _adamw_body(n_parts):
    def body(p_ref, w_ref, m_ref, v_ref, *rest):
        g_out, d_out, m_out, v_out = rest[-4:]
        g = p_ref[0].astype(F32)
        for i in range(1, n_parts):
            g = g + p_ref[i].astype(F32)
        m2 = ADAM_B1 * m_ref[...] + (1.0 - ADAM_B1) * g
        v2 = ADAM_B2 * v_ref[...] + (1.0 - ADAM_B2) * jnp.square(g)
        m_hat = m2 / (1.0 - ADAM_B1 ** ADAM_STEP)
        v_hat = v2 / (1.0 - ADAM_B2 ** ADAM_STEP)
        g_out[...] = g
        d_out[...] = -ADAM_LR * (m_hat / (jnp.sqrt(v_hat) + ADAM_EPS) + ADAM_WD * w_ref[...])
        m_out[...] = m2
        v_out[...] = v2

    return body


def _adamw_layer(parts, w, m, v, layer, prev, name):
    p, r, c = parts.shape
    tr = _tile(r, 256, 8)
    blk = pl.BlockSpec((None, tr, c), lambda i: (layer, i, 0))
    in_specs = [pl.BlockSpec((p, tr, c), lambda i: (0, i, 0)), blk, blk, blk]
    args = [parts, w, m, v]
    aliases = {}
    if prev is not None:
        in_specs += [_ANY] * 4
        args += list(prev)
        aliases = {4 + i: i for i in range(4)}
    return pl.pallas_call(
        _adamw_body(p), grid=(r // tr,), in_specs=in_specs, out_specs=[blk] * 4,
        out_shape=[jax.ShapeDtypeStruct(w.shape, F32)] * 4, input_output_aliases=aliases, name=name,
        compiler_params=_cparams(1))(*args)


def _adamw(parts, w, m, v, name):
    p, nl, r, c = parts.shape
    tr = _tile(r, 256, 8)
    body = _adamw_body(p)

    blk = (None, tr, c)
    imap = lambda l, i: (l, i, 0)
    out = ((nl, r, c), F32, blk, imap)
    return _call(body, grid=(nl, r // tr),
                 ins=[(parts, (p, None, tr, c), lambda l, i: (0, l, i, 0)), (w, blk, imap), (m, blk, imap), (v, blk, imap)],
                 outs=[out] * 4, name=name)


def _rows(x):
    return x.reshape(-1, LANES)


def _pad_rows(x, mult=8):
    r = x.shape[0]
    extra = (-r) % mult
    return jnp.pad(x, ((0, extra), (0, 0))) if extra else x


def _shard_cols(x, me, groups):
    lead = x.shape[:-1]
    xr = x.reshape(lead + (N_DEV, groups * LANES))
    xs = lax.dynamic_index_in_dim(xr, me, axis=len(lead), keepdims=False)
    return xs.reshape(N_DEV, -1, LANES)


def kernel(x, c, ada_w, ada_b, ln_g, ln_b, dn_w_in, dn_conv_w, dn_a_log, dn_dt_bias, dn_norm_w, dn_w_out, cf_w_in, cf_dw_w, cf_dw_b, cf_ln_g, cf_ln_b, cf_w_out, ff_w1, ff_w2, loss_target, m_ada_w, m_ada_b, m_ln_g, m_ln_b, m_dn_w_in, m_dn_conv_w, m_dn_a_log, m_dn_dt_bias, m_dn_norm_w, m_dn_w_out, m_cf_w_in, m_cf_dw_w, m_cf_dw_b, m_cf_ln_g, m_cf_ln_b, m_cf_w_out, m_ff_w1, m_ff_w2, v_ada_w, v_ada_b, v_ln_g, v_ln_b, v_dn_w_in, v_dn_conv_w, v_dn_a_log, v_dn_dt_bias, v_dn_norm_w, v_dn_w_out, v_cf_w_in, v_cf_dw_w, v_cf_dw_b, v_cf_ln_g, v_cf_ln_b, v_cf_w_out, v_ff_w1, v_ff_w2):
    depth, d, _ = ada_w.shape
    n_a, n_b = dn_w_in.shape[0], cf_w_in.shape[0]
    heads = dn_a_log.shape[1]
    hw = heads * HEAD_DIM
    taps = cf_dw_w.shape[1]
    s = x.shape[1]
    alpha = (2.0 * depth) ** 0.25
    me = 4 * lax.axis_index("x") + 2 * lax.axis_index("y") + lax.axis_index("c")
    xs, tgt = x[0], loss_target[0]

    small_local = [_rows(ln_g), _rows(ln_b), _rows(dn_conv_w), _rows(cf_dw_w), _rows(cf_dw_b), _rows(cf_ln_g),
                   _rows(cf_ln_b), _rows(c)]
    sizes = [a.shape[0] for a in small_local]
    packed = _pad_rows(jnp.concatenate(small_local, axis=0))
    (small_all,) = _exchange([packed], False, "comm_gather_params")
    offs = [0]
    for z in sizes:
        offs.append(offs[-1] + z)

    def small(i):
        return small_all[:, offs[i]:offs[i + 1], :]

    def unshard(piece, lead, groups):
        t = piece.reshape((N_DEV,) + lead + (groups * LANES,))
        t = jnp.moveaxis(t, 0, len(lead))
        return t.reshape(lead + (N_DEV * groups * LANES,))

    ln_g_f = unshard(small(0), (depth, 2), 1)
    ln_b_f = unshard(small(1), (depth, 2), 1)
    conv_w_f = unshard(small(2), (n_a, DN_CONV), 3 * heads // N_DEV)
    dw_w_f = unshard(small(3), (n_b, taps), 1)
    dw_b_f = unshard(small(4), (n_b,), 1)
    cf_ln_g_f = unshard(small(5), (n_b,), 1)
    cf_ln_b_f = unshard(small(6), (n_b,), 1)
    c_all = small(7).reshape(N_DEV, d)

    mod_part, cond_all = _ada_fwd(c_all, ada_w)
    (mod_all,) = _exchange([mod_part], False, "comm_gather_mod")
    mod_mine = lax.dynamic_index_in_dim(mod_all, me, axis=2, keepdims=False)
    mod_mine = jnp.moveaxis(mod_mine, 0, 1).reshape(depth, N_MOD * d)

    dn_in_cols = dn_w_in.shape[2]
    keys, shards = [], []
    for i in range(depth):
        j = i // 2
        mixer = [("dn_in", dn_w_in), ("dn_out", dn_w_out)] if i % 2 == 0 else [("cf_in", cf_w_in), ("cf_out", cf_w_out)]
        for nm, wt in mixer:
            keys.append((nm, j))
            shards.append(wt[j].astype(MXU_DTYPE))
        keys += [("ff1", i), ("ff2", i)]
        shards += [ff_w1[i].astype(MXU_DTYPE), ff_w2[i].astype(MXU_DTYPE)]
    handles, token = _xfer_start(shards, [_landing(a, me) for a in shards], False, mod_all, "gather_weights_start")
    handles = dict(zip(keys, handles))
    weights = {}

    def gathered(key, after):
        if key not in weights:
            weights[key] = _xfer_wait(handles[key], False, after, "gather_wait_%s_%d" % key)
        return weights[key]

    def get_dn_in(j):
        def get(after):
            g = gathered(("dn_in", j), after)
            w = jnp.moveaxis(g, 0, 1).reshape(d, N_DEV * dn_in_cols)
            return jnp.pad(w, ((0, 0), (0, 4 * hw + LANES - N_DEV * dn_in_cols)))
        return get

    def get_rows(key):
        return lambda after: gathered(key, after).reshape((-1, d))

    def get_cols(key):
        return lambda after: gathered(key, after)

    def add_bias(a, b):
        return (a + b,), ()

    (mod,), _ = _rowmap(add_bias, [mod_mine + token[0, 0], ada_b], [], [(N_MOD * d, F32)], [], "ada_bias")

    def mod_row(i, j):
        return mod[i:i + 1, j * d:(j + 1) * d]

    def ln_row(arr, i, j):
        return arr[i, j][None, :]

    subs = []
    h_cur = _modulate_fwd(xs, mod_row(0, 1), mod_row(0, 0))
    x_cur = xs
    last = None
    for i in range(depth):
        j = i // 2
        if i % 2 == 0:
            y, res = _deltanet_fwd(h_cur, get_dn_in(j), conv_w_f[j], dn_a_log[j], dn_dt_bias[j], dn_norm_w[j],
                                   get_rows(("dn_out", j)))
        else:
            y, res = _conformer_fwd(h_cur, get_cols(("cf_in", j)), dw_w_f[j], dw_b_f[j][None, :], cf_ln_g_f[j][None, :],
                                    cf_ln_b_f[j][None, :], get_rows(("cf_out", j)))
        p1 = (mod_row(i, 2), ln_row(ln_g_f, i, 0), ln_row(ln_b_f, i, 0), mod_row(i, 4), mod_row(i, 3))
        x_mid, h_mid = _combine_fwd(alpha, x_cur, y, *p1)
        subs.append((x_cur, y, p1, res))
        m_out, res2 = _mlp_fwd(h_mid, get_cols(("ff1", i)), get_rows(("ff2", i)))
        if i + 1 < depth:
            p2 = (mod_row(i, 5), ln_row(ln_g_f, i, 1), ln_row(ln_b_f, i, 1), mod_row(i + 1, 1), mod_row(i + 1, 0))
            x_next, h_next = _combine_fwd(alpha, x_mid, m_out, *p2)
            subs.append((x_mid, m_out, p2, res2))
            x_cur, h_cur = x_next, h_next
        else:
            p2 = (mod_row(i, 5), ln_row(ln_g_f, i, 1), ln_row(ln_b_f, i, 1))
            last = (x_mid, m_out, p2, res2)

    x_in, y_in, p_last, res_last = last
    dx, dy, (loss_acc, g_gt, g_g, g_b) = _last_fwd_bwd(alpha, x_in, y_in, tgt, *p_last)
    loss = lax.psum(loss_acc[0, 0], ("x", "y", "c"))

    d_mod = [[None] * N_MOD for _ in range(depth)]
    d_ln_g = [[None, None] for _ in range(depth)]
    d_ln_b = [[None, None] for _ in range(depth)]
    d_mod[depth - 1][5], d_ln_g[depth - 1][1], d_ln_b[depth - 1][1] = g_gt, g_g, g_b
    gw = dict(dn=[None] * n_a, cf=[None] * n_b)

    sent = {}

    def send_grads(named, tag):
        parts = [p for _, p in named]
        hs, tok = _xfer_start(parts, [_landing_scatter(p, me) for p in parts], True, parts[0], "scatter_start_" + tag)
        for (key, _), hnd in zip(named, hs):
            sent[key] = hnd
        return tok[0, 0]

    def by_rows(g):
        return g.reshape((N_DEV, g.shape[0] // N_DEV, g.shape[1]))

    def send_mlp(i, d_w1, d_w2):
        return send_grads([(("ff1", i), d_w1), (("ff2", i), by_rows(d_w2))], "ff_%d" % i)

    dh, d_w1, d_w2 = _mlp_bwd(res_last, dy)
    pin = send_mlp(depth - 1, d_w1, d_w2)
    for idx in range(len(subs) - 1, -1, -1):
        x_in, y_in, prm, res = subs[idx]
        i, second = idx // 2, idx % 2
        prm = (prm[0] + pin,) + tuple(prm[1:])
        dx, dy, (g_gt, g_g, g_b, g_sc, g_sh) = _combine_bwd(alpha, x_in, y_in, dx, dh, *prm)
        d_mod[i][5 if second else 2], d_ln_g[i][second], d_ln_b[i][second] = g_gt, g_g, g_b
        nxt_i, nxt_base = (i + 1, 0) if second else (i, 3)
        d_mod[nxt_i][nxt_base + 1], d_mod[nxt_i][nxt_base] = g_sc, g_sh
        j = i // 2
        if second:
            dh, d_w1, d_w2 = _mlp_bwd(res, dy)
            pin = send_mlp(i, d_w1, d_w2)
        elif i % 2 == 0:
            dh, gw["dn"][j] = _deltanet_bwd(res, dy, conv_w_f[j])
            d_in = gw["dn"][j]["w_in"][:, :N_DEV * dn_in_cols].reshape(d, N_DEV, dn_in_cols)
            pin = send_grads([(("dn_in", j), jnp.moveaxis(d_in, 1, 0)), (("dn_out", j), by_rows(gw["dn"][j]["w_out"]))],
                             "dn_%d" % j)
        else:
            dh, gw["cf"][j] = _conformer_bwd(res, dy, dw_w_f[j], cf_ln_g_f[j][None, :], cf_ln_b_f[j][None, :])
            pin = send_grads([(("cf_in", j), gw["cf"][j]["w_in"]), (("cf_out", j), by_rows(gw["cf"][j]["w_out"]))],
                             "cf_%d" % j)
    grad_x, g_sc, g_sh = _modulate_bwd(xs, dx, dh, mod_row(0, 1) + pin, mod_row(0, 0))
    d_mod[0][1], d_mod[0][0] = g_sc, g_sh
    d_mod_full = jnp.concatenate([jnp.concatenate(r, axis=1) for r in d_mod], axis=0)

    stacked = {"dn_w_in": ("dn_in", dn_w_in, m_dn_w_in, v_dn_w_in), "dn_w_out": ("dn_out", dn_w_out, m_dn_w_out, v_dn_w_out),
               "cf_w_in": ("cf_in", cf_w_in, m_cf_w_in, v_cf_w_in), "cf_w_out": ("cf_out", cf_w_out, m_cf_w_out, v_cf_w_out),
               "ff_w1": ("ff1", ff_w1, m_ff_w1, v_ff_w1), "ff_w2": ("ff2", ff_w2, m_ff_w2, v_ff_w2)}
    chains = {key: None for key in stacked}

    def update_layer(i):
        mixer = ["dn_w_in", "dn_w_out"] if i % 2 == 0 else ["cf_w_in", "cf_w_out"]
        for key, idx in [("ff_w1", i), ("ff_w2", i)] + [(k, i // 2) for k in mixer]:
            short, w, m, v = stacked[key]
            parts = _xfer_wait(sent[(short, idx)], True, grad_x, "scatter_wait_%s_%d" % (short, idx))
            chains[key] = _adamw_layer(parts, w, m, v, idx, chains[key], "adamw_%s_%d" % (key, idx))

    for i in range(depth - 1, 0, -1):
        update_layer(i)

    def stack_rows(lst):
        return jnp.stack(lst, axis=0)

    gs_ln_g = jnp.stack([jnp.concatenate(r, axis=0) for r in d_ln_g], axis=0)
    gs_ln_b = jnp.stack([jnp.concatenate(r, axis=0) for r in d_ln_b], axis=0)
    gs_conv_w = stack_rows([gw["dn"][j]["conv_w"] for j in range(n_a)])
    gs_dw_w = stack_rows([gw["cf"][j]["dw_w"] for j in range(n_b)])
    gs_dw_b = stack_rows([gw["cf"][j]["dw_b"] for j in range(n_b)])
    gs_cf_ln_g = stack_rows([gw["cf"][j]["ln_g"] for j in range(n_b)])
    gs_cf_ln_b = stack_rows([gw["cf"][j]["ln_b"] for j in range(n_b)])
    gs_a_log = stack_rows([_pad_lanes(gw["dn"][j]["a_log"], 0)[0] for j in range(n_a)])
    gs_dt_bias = stack_rows([_pad_lanes(gw["dn"][j]["dt_bias"], 0)[0] for j in range(n_a)])
    gs_norm_w = stack_rows([gw["dn"][j]["norm_w"] for j in range(n_a)])
    small_grads = [gs_ln_g, gs_ln_b, gs_conv_w, gs_dw_w, gs_dw_b, gs_cf_ln_g, gs_cf_ln_b, gs_a_log, gs_dt_bias,
                   gs_norm_w, d_mod_full]
    sg_rows = [_rows(a) for a in small_grads]
    sg_sizes = [a.shape[0] for a in sg_rows]
    (sg_all,) = _exchange([_pad_rows(jnp.concatenate(sg_rows, axis=0))], False, "comm_gather_small_grads")
    sg_offs = [0]
    for z in sg_sizes:
        sg_offs.append(sg_offs[-1] + z)

    def sg(i, shape):
        return sg_all[:, sg_offs[i]:sg_offs[i + 1], :].reshape((N_DEV,) + shape)

    dmod_all = sg(10, (depth, N_MOD * d))
    nl = ada_w.shape[2]
    dmod_cols = lax.dynamic_slice_in_dim(dmod_all, me * nl, nl, axis=2)
    g_ada_w = _ada_bwd(cond_all, jnp.moveaxis(dmod_cols, 0, 1))

    outs = {}

    def run_adamw(key, parts, w, m, v):
        shp = w.shape
        as3 = lambda t: t.reshape((-1,) + shp[-2:]) if t.ndim >= 3 else t.reshape((1,) + shp)
        parts3 = parts.reshape((parts.shape[0],) + as3(w).shape)
        res = _adamw(parts3, as3(w), as3(m), as3(v), "adamw_" + key)
        outs[key] = tuple(r.reshape(shp) for r in res)

    run_adamw("ada_w", g_ada_w[None], ada_w, m_ada_w, v_ada_w)

    cgroups = 3 * heads // N_DEV
    shard_parts = [
        _shard_cols(sg(0, (depth, 2, d)), me, 1), _shard_cols(sg(1, (depth, 2, d)), me, 1),
        _shard_cols(sg(2, (n_a, DN_CONV, 3 * hw)), me, cgroups), _shard_cols(sg(3, (n_b, taps, d)), me, 1),
        _shard_cols(sg(4, (n_b, d)), me, 1), _shard_cols(sg(5, (n_b, d)), me, 1), _shard_cols(sg(6, (n_b, d)), me, 1),
    ]
    repl_parts = [sg(7, (n_a, LANES)), sg(8, (n_a, LANES)), sg(9, (n_a, HEAD_DIM)),
                  sg(10, (depth, N_MOD * d)).reshape(N_DEV, -1, LANES)]
    small_parts = shard_parts + repl_parts
    sp_sizes = [a.shape[1] for a in small_parts]
    parts_packed = jnp.concatenate(small_parts, axis=1)
    extra = (-parts_packed.shape[1]) % 8
    parts_packed = jnp.pad(parts_packed, ((0, 0), (0, extra), (0, 0)))

    def pad_heads(t):
        return jnp.pad(t, ((0, 0), (0, LANES - heads)))

    def pack_state(ln_g_, ln_b_, conv_w_, dw_w_, dw_b_, cln_g_, cln_b_, a_log_, dt_b_, norm_w_, ada_b_):
        rows = [_rows(ln_g_), _rows(ln_b_), _rows(conv_w_), _rows(dw_w_), _rows(dw_b_), _rows(cln_g_), _rows(cln_b_),
                pad_heads(a_log_), pad_heads(dt_b_), norm_w_, _rows(ada_b_)]
        return _pad_rows(jnp.concatenate(rows, axis=0))

    w_s = pack_state(ln_g, ln_b, dn_conv_w, cf_dw_w, cf_dw_b, cf_ln_g, cf_ln_b, dn_a_log, dn_dt_bias, dn_norm_w, ada_b)
    m_s = pack_state(m_ln_g, m_ln_b, m_dn_conv_w, m_cf_dw_w, m_cf_dw_b, m_cf_ln_g, m_cf_ln_b, m_dn_a_log,
                     m_dn_dt_bias, m_dn_norm_w, m_ada_b)
    v_s = pack_state(v_ln_g, v_ln_b, v_dn_conv_w, v_cf_dw_w, v_cf_dw_b, v_cf_ln_g, v_cf_ln_b, v_dn_a_log,
                     v_dn_dt_bias, v_dn_norm_w, v_ada_b)
    res_s = _adamw(parts_packed[:, None], w_s[None], m_s[None], v_s[None], "adamw_small")
    sp_offs = [0]
    for z in sp_sizes:
        sp_offs.append(sp_offs[-1] + z)
    small_keys = ["ln_g", "ln_b", "dn_conv_w", "cf_dw_w", "cf_dw_b", "cf_ln_g", "cf_ln_b", "dn_a_log", "dn_dt_bias",
                  "dn_norm_w", "ada_b"]
    small_shapes = [ln_g.shape, ln_b.shape, dn_conv_w.shape, cf_dw_w.shape, cf_dw_b.shape, cf_ln_g.shape,
                    cf_ln_b.shape, dn_a_log.shape, dn_dt_bias.shape, dn_norm_w.shape, ada_b.shape]
    for n, (key, shp) in enumerate(zip(small_keys, small_shapes)):
        vals = []
        for r in res_s:
            piece = r[0, sp_offs[n]:sp_offs[n + 1], :]
            if key in ("dn_a_log", "dn_dt_bias"):
                piece = piece[:, :heads]
            vals.append(piece.reshape(shp))
        outs[key] = tuple(vals)

    update_layer(0)
    for key in stacked:
        outs[key] = tuple(chains[key])

    order = ["ada_w", "ada_b", "ln_g", "ln_b", "dn_w_in", "dn_conv_w", "dn_a_log", "dn_dt_bias", "dn_norm_w",
             "dn_w_out", "cf_w_in", "cf_dw_w", "cf_dw_b", "cf_ln_g", "cf_ln_b", "cf_w_out", "ff_w1", "ff_w2"]
    result = [loss, grad_x[None]]
    for part in range(4):
        result += [outs[k][part] for k in order]
    return tuple(result)
```

```python
import functools

import jax
import jax.numpy as jnp
from jax import lax
from jax.experimental import pallas as pl
from jax.experimental.pallas import tpu as pltpu

F32 = jnp.float32
MXU_DTYPE = jnp.bfloat16
N_DEV = 8
LANES = 128
HEAD_DIM = 128
CHUNK = 64
DN_CONV = 4
N_MOD = 6
LN_EPS = 1e-5
RMS_EPS = 1e-6
L2_EPS = 1e-6
ADAM_LR = 0.001
ADAM_B1 = 0.9
ADAM_B2 = 0.999
ADAM_EPS = 1e-08
ADAM_WD = 0.01
ADAM_STEP = 10

HI = lax.Precision.HIGHEST
NN = ((1,), (0,))
NT = ((1,), (1,))
TN = ((0,), (0,))

ROW_TILE = 256
CONV_TILE = 256


def _mdot(a, b, dims=NN):
    return lax.dot_general(a.astype(MXU_DTYPE), b.astype(MXU_DTYPE), (dims, ((), ())), preferred_element_type=F32)


def _split3(x):
    hi = x.astype(MXU_DTYPE)
    r1 = x - hi.astype(F32)
    mid = r1.astype(MXU_DTYPE)
    lo = (r1 - mid.astype(F32)).astype(MXU_DTYPE)
    return hi, mid, lo


def _dot01(a, b, dims=NN, mask_first=True):
    d = lambda p, q: lax.dot_general(p, q, (dims, ((), ())), preferred_element_type=F32)
    if mask_first:
        m = a.astype(MXU_DTYPE)
        return sum(d(m, p) for p in _split3(b))
    m = b.astype(MXU_DTYPE)
    return sum(d(p, m) for p in _split3(a))


def _dot3(a, b, dims=NN):
    ah, am, _ = _split3(a)
    bh, bm, _ = _split3(b)
    d = lambda p, q: lax.dot_general(p, q, (dims, ((), ())), preferred_element_type=F32)
    return d(ah, bh) + (d(ah, bm) + d(am, bh))


def _cparams(n):
    return pltpu.CompilerParams(dimension_semantics=("arbitrary",) * n)


def _call(body, *, grid, ins, outs, name, scratch=()):
    res = pl.pallas_call(
        body,
        grid=grid,
        in_specs=[pl.BlockSpec(b, m) for _, b, m in ins],
        out_specs=[pl.BlockSpec(b, m) for _, _, b, m in outs],
        out_shape=[jax.ShapeDtypeStruct(s, d) for s, d, _, _ in outs],
        scratch_shapes=list(scratch),
        name=name,
        compiler_params=_cparams(len(grid)),
    )(*[a for a, _, _ in ins])
    return res


def _tile(n, pref, unit=LANES):
    if n <= pref:
        return n
    t = (pref // unit) * unit
    while t > unit and n % t:
        t -= unit
    assert n % t == 0, (n, pref)
    return t


def _rowmap(fn, rows, consts, row_outs, acc_outs, name):
    rows = [r if isinstance(r, tuple) else (r, r.shape[1], 0) for r in rows]
    s = rows[0][0].shape[0]
    tm = min(ROW_TILE, s)
    nr, nc, no, na = len(rows), len(consts), len(row_outs), len(acc_outs)

    def body(*refs):
        rin, cin = refs[:nr], refs[nr:nr + nc]
        rout, aout = refs[nr + nc:nr + nc + no], refs[nr + nc + no:]
        ro, ao = fn(*[r[...] for r in rin], *[c[...] for c in cin])
        for ref, val in zip(rout, ro):
            ref[...] = val.astype(ref.dtype)
        if na:
            first = pl.program_id(0) == 0

            @pl.when(first)
            def _():
                for ref, val in zip(aout, ao):
                    ref[...] = val

            @pl.when(jnp.logical_not(first))
            def _():
                for ref, val in zip(aout, ao):
                    ref[...] += val

    ins = [(a, (tm, w), functools.partial(lambda i, cb: (i, cb), cb=cb)) for a, w, cb in rows]
    ins += [(c, c.shape, lambda i: (0, 0)) for c in consts]
    outs = [((s, w), d, (tm, w), lambda i: (i, 0)) for w, d in row_outs]
    outs += [(shp, F32, shp, lambda i: (0, 0)) for shp in acc_outs]
    res = _call(body, grid=(s // tm,), ins=ins, outs=outs, name=name)
    return res[:no], res[no:]


def _ln(z, g, b):
    mu = jnp.mean(z, -1, keepdims=True)
    var = jnp.mean(jnp.square(z - mu), -1, keepdims=True)
    return (z - mu) * lax.rsqrt(var + LN_EPS) * g + b


def _combine(alpha, x, y, gt, g, b, sc, sh):
    xn = _ln(alpha * x + (1.0 + gt) * y, g, b)
    return xn, xn * (1.0 + sc) + sh


def _modulate_fwd(x, sc, sh):
    def fn(x, sc, sh):
        return ((x * (1.0 + sc) + sh),), ()

    (h,), _ = _rowmap(fn, [x], [sc, sh], [(x.shape[1], MXU_DTYPE)], [], "modulate_fwd")
    return h


def _modulate_bwd(x, dx, dh, sc, sh):
    d = x.shape[1]

    def fn(x, dx, dh, sc, sh):
        _, vjp = jax.vjp(lambda x, sc, sh: x * (1.0 + sc) + sh, x, sc, sh)
        gx, gsc, gsh = vjp(dh)
        return (dx + gx,), (gsc, gsh)

    (gx,), (gsc, gsh) = _rowmap(fn, [x, dx, dh], [sc, sh], [(d, F32)], [(1, d), (1, d)], "modulate_bwd")
    return gx, gsc, gsh


def _combine_fwd(alpha, x, y, gt, g, b, sc, sh):
    d = x.shape[1]

    def fn(x, y, gt, g, b, sc, sh):
        return _combine(alpha, x, y, gt, g, b, sc, sh), ()

    (xn, h), _ = _rowmap(fn, [x, y], [gt, g, b, sc, sh], [(d, F32), (d, MXU_DTYPE)], [], "combine_fwd")
    return xn, h


def _combine_bwd(alpha, x, y, dxn, dh, gt, g, b, sc, sh):
    d = x.shape[1]

    def fn(x, y, dxn, dh, gt, g, b, sc, sh):
        _, vjp = jax.vjp(functools.partial(_combine, alpha), x, y, gt, g, b, sc, sh)
        gx, gy, ggt, gg, gb, gsc, gsh = vjp((dxn, dh))
        return (gx, gy), (ggt, gg, gb, gsc, gsh)

    (gx, gy), accs = _rowmap(fn, [x, y, dxn, dh], [gt, g, b, sc, sh], [(d, F32), (d, MXU_DTYPE)],
                             [(1, d)] * 5, "combine_bwd")
    return gx, gy, accs


def _last_fwd_bwd(alpha, x, y, tgt, gt, g, b):
    d = x.shape[1]

    def fn(x, y, tgt, gt, g, b):
        xn, vjp = jax.vjp(lambda x, y, gt, g, b: _ln(alpha * x + (1.0 + gt) * y, g, b), x, y, gt, g, b)
        err = xn - tgt
        gx, gy, ggt, gg, gb = vjp(err * (1.0 / d))
        rows = jnp.sum(jnp.square(err), axis=-1, keepdims=True)
        loss = (0.5 / d) * jnp.sum(rows, axis=0, keepdims=True) * jnp.ones((1, LANES), F32)
        return (gx, gy), (loss, ggt, gg, gb)

    (gx, gy), accs = _rowmap(fn, [x, y, tgt], [gt, g, b], [(d, F32), (d, MXU_DTYPE)],
                             [(1, LANES), (1, d), (1, d), (1, d)], "last_fwd_bwd")
    return gx, gy, accs


def _mm_call(a, a_blk, a_map, b, b_blk, b_map, outs, dims, grid, name, relu2=False):
    nk = grid[2]
    n_out = len(outs)

    def body(a_ref, b_ref, *rest):
        out_refs = rest[:n_out]

        def finish(val):
            if relu2:
                out_refs[0][...] = val.astype(out_refs[0].dtype)
                out_refs[1][...] = jnp.square(jnp.maximum(val, 0.0)).astype(out_refs[1].dtype)
            else:
                out_refs[0][...] = val.astype(out_refs[0].dtype)

        p = lax.dot_general(a_ref[...], b_ref[...], (dims, ((), ())), preferred_element_type=F32)
        if nk == 1:
            finish(p)
        else:
            acc = rest[n_out]
            k = pl.program_id(2)

            @pl.when(k == 0)
            def _():
                acc[...] = p

            @pl.when(k > 0)
            def _():
                acc[...] += p

            @pl.when(k == nk - 1)
            def _():
                finish(acc[...])

    out_blk = tuple(x for x in outs[0][2] if x is not None)
    scratch = [pltpu.VMEM(out_blk, F32)] if nk > 1 else []
    return _call(body, grid=grid, ins=[(a, a_blk, a_map), (b, b_blk, b_map)], outs=outs, name=name, scratch=scratch)


def _mm_nn(a, b, out_dtype, name, relu2=False):
    m, kdim = a.shape
    tm = _tile(m, 512, 8)
    tk = _tile(kdim, 1024)
    if b.ndim == 2:
        n = b.shape[1]
        tn = _tile(n, 512)
        b_blk, b_map = (tk, tn), lambda i, j, k: (k, j)
    else:
        g, _, ng = b.shape
        n = g * ng
        tn = _tile(ng, 512)
        b_blk = (None, tk, tn)
        b_map = functools.partial(lambda i, j, k, npg: (j // npg, k, j % npg), npg=ng // tn)
    grid = (m // tm, n // tn, kdim // tk)
    outs = [((m, n), F32 if relu2 else out_dtype, (tm, tn), lambda i, j, k: (i, j))]
    if relu2:
        outs.append(((m, n), out_dtype, (tm, tn), lambda i, j, k: (i, j)))
    res = _mm_call(a, (tm, tk), lambda i, j, k: (i, k), b, b_blk, b_map, outs, NN, grid, name, relu2)
    return res if relu2 else res[0]


def _mm_nt(a, b, out_dtype, name):
    m, n = a.shape
    tm = _tile(m, 512, 8)
    if b.ndim == 2:
        kout = b.shape[0]
        to, tc = _tile(kout, 512), _tile(n, 1024)
        b_blk, b_map = (to, tc), lambda i, j, k: (j, k)
    else:
        _, kout, ng = b.shape
        to, tc = _tile(kout, 512), _tile(ng, 1024)
        b_blk = (None, to, tc)
        b_map = functools.partial(lambda i, j, k, cpg: (k // cpg, j, k % cpg), cpg=ng // tc)
    grid = (m // tm, kout // to, n // tc)
    outs = [((m, kout), out_dtype, (tm, to), lambda i, j, k: (i, j))]
    return _mm_call(a, (tm, tc), lambda i, j, k: (i, k), b, b_blk, b_map, outs, NT, grid, name)[0]


def _mm_tn(a, b, out_dtype, name, split_cols=False):
    m, kdim = a.shape
    n = b.shape[1]
    tc = _tile(m, 2048, 8)
    tk = _tile(kdim, 512)
    if not split_cols:
        tn = _tile(n, 512)
        out = ((kdim, n), out_dtype, (tk, tn), lambda i, j, k: (i, j))
    else:
        ng = n // N_DEV
        tn = _tile(ng, 512)
        out = ((N_DEV, kdim, ng), out_dtype, (None, tk, tn),
               functools.partial(lambda i, j, k, npg: (j // npg, i, j % npg), npg=ng // tn))
    grid = (kdim // tk, n // tn, m // tc)
    return _mm_call(a, (tc, tk), lambda i, j, k: (k, i), b, (tc, tn), lambda i, j, k: (k, j), [out], TN, grid, name)[0]


def _shifted(xa, off, rows):
    if off % 8 == 0:
        return xa[off:off + rows]
    return pltpu.roll(xa, xa.shape[0] - off, 0)[:rows]


def _conv_pad(taps):
    return -(-(taps - 1) // 8) * 8


def _conv_tile(xp_ref, w, i, rows, taps):
    pad = _conv_pad(taps)
    r0 = pl.multiple_of(i * rows, rows)
    xa = xp_ref[pl.ds(r0, rows + pad), :]
    views = [_shifted(xa, pad - (taps - 1) + j, rows) for j in range(taps)]
    acc = w[0:1, :] * views[0]
    for j in range(1, taps):
        acc = acc + w[j:j + 1, :] * views[j]
    return r0, acc, views


def _conv_back_tile(yp_ref, w, i, rows, taps):
    pad = _conv_pad(taps)
    r0 = pl.multiple_of(i * rows, rows)
    ya = yp_ref[pl.ds(r0, rows + pad), :]
    acc = w[taps - 1:taps, :] * ya[:rows]
    for j in range(taps - 1):
        acc = acc + w[j:j + 1, :] * _shifted(ya, taps - 1 - j, rows)
    return r0, acc


def _tap_sums(dy, views, taps):
    row = lax.broadcasted_iota(jnp.int32, (taps, LANES), 0)
    acc = jnp.zeros((taps, LANES), F32)
    for j in range(taps):
        acc = acc + jnp.where(row == j, jnp.sum(dy * views[j], axis=0, keepdims=True), 0.0)
    return acc


def _silu_l2(xc, l2):
    a = jax.nn.silu(xc)
    if l2:
        a = a * lax.rsqrt(jnp.sum(a * a, axis=-1, keepdims=True) + L2_EPS)
    return a


def _dn_conv_fwd(proj, conv_w, c0, nblk, l2, name):
    s = proj.shape[0]
    pad = _conv_pad(DN_CONV)
    rows = min(CONV_TILE, s)

    def body(x_ref, w_ref, o_ref, xp):
        xp[0:pad, :] = jnp.zeros((pad, LANES), F32)
        xp[pad:, :] = x_ref[...]
        w = w_ref[...]

        def tile(i, c):
            r0, acc, _ = _conv_tile(xp, w, i, rows, DN_CONV)
            o_ref[pl.ds(r0, rows), :] = _silu_l2(acc, l2)
            return c

        lax.fori_loop(0, s // rows, tile, 0)

    return _call(body, grid=(nblk,),
                 ins=[(proj, (s, LANES), lambda c: (0, c0 + c)), (conv_w, (DN_CONV, LANES), lambda c: (0, c0 + c))],
                 outs=[((nblk, s, LANES), F32, (None, s, LANES), lambda c: (c, 0, 0))],
                 name=name, scratch=[pltpu.VMEM((s + pad, LANES), F32)])[0]


def _dn_conv_bwd(proj, conv_w, da, c0, nblk, l2, name):
    s = proj.shape[0]
    pad = _conv_pad(DN_CONV)
    rows = min(CONV_TILE, s)

    def body(x_ref, w_ref, da_ref, dx_ref, dw_ref, xp, yp):
        xp[0:pad, :] = jnp.zeros((pad, LANES), F32)
        xp[pad:, :] = x_ref[...]
        yp[s:, :] = jnp.zeros((pad, LANES), F32)
        w = w_ref[...]

        def tile(i, dw):
            r0, acc, views = _conv_tile(xp, w, i, rows, DN_CONV)
            _, vjp = jax.vjp(functools.partial(_silu_l2, l2=l2), acc)
            (dxc,) = vjp(da_ref[pl.ds(r0, rows), :])
            yp[pl.ds(r0, rows), :] = dxc
            return dw + _tap_sums(dxc, views, DN_CONV)

        dw_ref[...] = lax.fori_loop(0, s // rows, tile, jnp.zeros((DN_CONV, LANES), F32))

        def tile2(i, c):
            r0, acc = _conv_back_tile(yp, w, i, rows, DN_CONV)
            dx_ref[pl.ds(r0, rows), :] = acc.astype(dx_ref.dtype)
            return c

        lax.fori_loop(0, s // rows, tile2, 0)

    return _call(body, grid=(nblk,),
                 ins=[(proj, (s, LANES), lambda c: (0, c0 + c)), (conv_w, (DN_CONV, LANES), lambda c: (0, c0 + c)),
                      (da, (None, s, LANES), lambda c: (c, 0, 0))],
                 outs=[((s, nblk * LANES), MXU_DTYPE, (s, LANES), lambda c: (0, c)),
                       ((DN_CONV, nblk * LANES), F32, (DN_CONV, LANES), lambda c: (0, c))],
                 name=name, scratch=[pltpu.VMEM((s + pad, LANES), F32), pltpu.VMEM((s + pad, LANES), F32)])


def _cf_conv_fwd(vg, dw_w, dw_b):
    s, c2 = vg.shape
    ch = c2 // 2
    nblk = ch // LANES
    taps = dw_w.shape[0]
    pad = _conv_pad(taps)
    rows = min(CONV_TILE, s)

    def body(v_ref, g_ref, w_ref, b_ref, o_ref, xp):
        xp[0:pad, :] = jnp.zeros((pad, LANES), F32)
        xp[pad:, :] = v_ref[...] * jax.nn.sigmoid(g_ref[...])
        w = w_ref[...]
        bias = b_ref[...]

        def tile(i, c):
            r0, acc, _ = _conv_tile(xp, w, i, rows, taps)
            o_ref[pl.ds(r0, rows), :] = acc + bias
            return c

        lax.fori_loop(0, s // rows, tile, 0)

    return _call(body, grid=(nblk,),
                 ins=[(vg, (s, LANES), lambda c: (0, c)), (vg, (s, LANES), lambda c: (0, nblk + c)),
                      (dw_w, (taps, LANES), lambda c: (0, c)), (dw_b, (1, LANES), lambda c: (0, c))],
                 outs=[((s, ch), F32, (s, LANES), lambda c: (0, c))],
                 name="cf_conv_fwd", scratch=[pltpu.VMEM((s + pad, LANES), F32)])[0]


def _cf_conv_bwd(vg, dw_w, du):
    s, c2 = vg.shape
    ch = c2 // 2
    nblk = ch // LANES
    taps = dw_w.shape[0]
    pad = _conv_pad(taps)
    rows = min(CONV_TILE, s)

    def body(v_ref, g_ref, w_ref, du_ref, dv_ref, dg_ref, dw_ref, db_ref, xp, yp):
        sig = jax.nn.sigmoid(g_ref[...])
        xp[0:pad, :] = jnp.zeros((pad, LANES), F32)
        xp[pad:, :] = v_ref[...] * sig
        yp[0:s, :] = du_ref[...]
        yp[s:, :] = jnp.zeros((pad, LANES), F32)
        w = w_ref[...]
        db_ref[...] = jnp.sum(du_ref[...], axis=0, keepdims=True)

        def tile(i, dw):
            r0, _, views = _conv_tile(xp, w, i, rows, taps)
            return dw + _tap_sums(du_ref[pl.ds(r0, rows), :], views, taps)

        dw_ref[...] = lax.fori_loop(0, s // rows, tile, jnp.zeros((taps, LANES), F32))

        def tile2(i, c):
            r0, du0 = _conv_back_tile(yp, w, i, rows, taps)
            val = v_ref[pl.ds(r0, rows), :]
            sg = jax.nn.sigmoid(g_ref[pl.ds(r0, rows), :])
            dv_ref[pl.ds(r0, rows), :] = (du0 * sg).astype(dv_ref.dtype)
            dg_ref[pl.ds(r0, rows), :] = (du0 * val * sg * (1.0 - sg)).astype(dg_ref.dtype)
            return c

        lax.fori_loop(0, s // rows, tile2, 0)

    return _call(body, grid=(nblk,),
                 ins=[(vg, (s, LANES), lambda c: (0, c)), (vg, (s, LANES), lambda c: (0, nblk + c)),
                      (dw_w, (taps, LANES), lambda c: (0, c)), (du, (s, LANES), lambda c: (0, c))],
                 outs=[((s, ch), MXU_DTYPE, (s, LANES), lambda c: (0, c)),
                       ((s, ch), MXU_DTYPE, (s, LANES), lambda c: (0, c)),
                       ((taps, ch), F32, (taps, LANES), lambda c: (0, c)),
                       ((1, ch), F32, (1, LANES), lambda c: (0, c))],
                 name="cf_conv_bwd", scratch=[pltpu.VMEM((s + pad, LANES), F32), pltpu.VMEM((s + pad, LANES), F32)])


def _masks():
    r = lax.broadcasted_iota(jnp.int32, (CHUNK, CHUNK), 0)
    c = lax.broadcasted_iota(jnp.int32, (CHUNK, CHUNK), 1)
    return r >= c, r > c, r <= c


def _chunk_decay(g):
    causal, _, upper = _masks()
    gb = jnp.broadcast_to(g, (CHUNK, CHUNK))
    gam_r = _dot01(jnp.where(causal, 1.0, 0.0), gb)
    gam_s = _dot01(jnp.ones((CHUNK, CHUNK), F32), jnp.where(upper, gb, 0.0))
    dm = jnp.where(causal, jnp.exp(jnp.where(causal, gam_r - gam_s, 0.0)), 0.0)
    return gam_r[:, 0:1], dm


def _chunk_scores(q, k, beta, dm):
    _, strict, _ = _masks()
    both = _mdot(jnp.concatenate([k * beta, q * (HEAD_DIM ** -0.5)], axis=0), k, NT)
    return jnp.where(strict, both[:CHUNK] * dm, 0.0), both[CHUNK:] * dm


def _lockstep(gens):
    results = [None] * len(gens)
    alive = list(range(len(gens)))
    while alive:
        for i in list(alive):
            try:
                next(gens[i])
            except StopIteration as stop:
                results[i] = stop.value
                alive.remove(i)
    return results


def _chunk_prep_bwd(q, k, v, beta, gam, t, du, dw, daqk, dqd, dkd, dgl):
    causal, strict, _ = _masks()
    r = lax.broadcasted_iota(jnp.int32, (CHUNK, CHUNK), 0)
    c = lax.broadcasted_iota(jnp.int32, (CHUNK, CHUNK), 1)
    scale = HEAD_DIM ** -0.5
    eg = jnp.exp(gam)
    gam_last = gam[CHUNK - 1:CHUNK, :]
    rr = jnp.exp(gam_last - gam)
    kb = k * beta
    qs = q * scale
    vb = v * beta
    kbe = kb * eg
    gam_b = jnp.broadcast_to(gam, (CHUNK, CHUNK))
    gam_s = _dot01(jnp.ones((CHUNK, CHUNK), F32), jnp.where(r == c, gam_b, 0.0))
    both = _mdot(jnp.concatenate([kb, qs], axis=0), k, NT)
    duw = jnp.concatenate([du, dw], axis=1)
    dt = _mdot(duw, jnp.concatenate([vb, kbe], axis=1), NT)
    dvk = _mdot(t, duw, TN)
    yield
    dm = jnp.where(causal, jnp.exp(jnp.where(causal, gam_b - gam_s, 0.0)), 0.0)
    a = jnp.where(strict, both[:CHUNK] * dm, 0.0)
    aqk = both[CHUNK:] * dm
    dvb, dkbe = dvk[:, :HEAD_DIM], dvk[:, HEAD_DIM:]
    x = _dot3(t, dt, TN)
    yield
    da = jnp.where(strict, -_dot3(x, t, NT), 0.0)
    yield
    dkk = da * dm
    dqk = daqk * dm
    ddiff = da * a + daqk * aqk
    dboth = jnp.concatenate([dkk, dqk], axis=0)
    dkq = _mdot(dboth, k)
    dk_mm = _mdot(dboth, jnp.concatenate([kb, qs], axis=0), TN)
    colsum = _dot01(ddiff, jnp.ones((CHUNK, LANES), F32), TN, mask_first=False)[:, 0:1]
    yield
    dkb = dkq[:CHUNK] + dkbe * eg
    dk = dk_mm + dkb * beta + dkd * rr
    dq = (dkq[CHUNK:] + dqd * eg) * scale
    dbeta = jnp.sum(dkb * k, axis=-1, keepdims=True) + jnp.sum(dvb * v, axis=-1, keepdims=True)
    dv = dvb * beta
    deg = jnp.sum(dkbe * kb, axis=-1, keepdims=True) + jnp.sum(dqd * qs, axis=-1, keepdims=True)
    drr = jnp.sum(dkd * k, axis=-1, keepdims=True)
    dgam = deg * eg - drr * rr + jnp.sum(ddiff, axis=-1, keepdims=True) - colsum
    dgam_last = jnp.sum(drr * rr, axis=0, keepdims=True) + dgl[0:1, :] * jnp.exp(gam_last)
    row = lax.broadcasted_iota(jnp.int32, (CHUNK, 1), 0)
    dgam = dgam + jnp.where(row == CHUNK - 1, dgam_last, 0.0)
    dg = _dot01(jnp.where(causal, 1.0, 0.0), jnp.broadcast_to(dgam, (CHUNK, LANES)), TN)[:, 0:1]
    return dq, dk, dv, dbeta, dg


def _prep_group(s):
    nch = s // CHUNK
    return 8 if nch % 8 == 0 else (4 if nch % 4 == 0 else 1)


def _tri_solve_lanes(a_l):
    n = a_l.shape[1]
    group = 8

    def body(a_ref, t_ref):
        t_ref[...] = jnp.zeros_like(t_ref)
        col = lax.broadcasted_iota(jnp.int32, (CHUNK, n), 0)

        def row(r, carry):
            r0 = pl.multiple_of(r * CHUNK, CHUNK)

            def inner(sg, acc):
                a8 = a_ref[pl.ds(r0 + pl.multiple_of(sg * group, group), group), :]
                for j in range(group):
                    t0 = pl.multiple_of((sg * group + j) * CHUNK, CHUNK)
                    acc = acc + a8[j:j + 1, :] * t_ref[pl.ds(t0, CHUNK), :]
                return acc

            acc = lax.fori_loop(0, (r + group - 1) // group, inner, jnp.zeros((CHUNK, n), F32))
            t_ref[pl.ds(r0, CHUNK), :] = jnp.where(col == r, 1.0, 0.0) - acc
            return carry

        lax.fori_loop(0, CHUNK, row, 0)

    return pl.pallas_call(body, out_shape=jax.ShapeDtypeStruct(a_l.shape, F32), name="dn_tri_solve")(a_l)


def _dn_prep(q, k, v, beta, g):
    h, s, _ = q.shape
    cb = _prep_group(s)
    rb = cb * CHUNK
    big = lambda x: (x, (None, rb, HEAD_DIM), lambda hh, n: (hh, n, 0))
    sq = lambda x: (x, (None, rb, CHUNK), lambda hh, n: (hh, n, 0))
    col = lambda x: (x, (None, rb, 1), lambda hh, n: (hh, n, 0))
    o_big = ((h, s, HEAD_DIM), F32, (None, rb, HEAD_DIM), lambda hh, n: (hh, n, 0))
    o_sq = ((h, s, CHUNK), F32, (None, rb, CHUNK), lambda hh, n: (hh, n, 0))
    o_col = ((h, s, 1), F32, (None, rb, 1), lambda hh, n: (hh, n, 0))

    def scores(q_ref, k_ref, b_ref, g_ref, a_ref, aqk_ref, gam_ref):
        for i in range(cb):
            sl = slice(i * CHUNK, (i + 1) * CHUNK)
            gam, dm = _chunk_decay(g_ref[sl, :])
            a_ref[sl, :], aqk_ref[sl, :] = _chunk_scores(q_ref[sl, :], k_ref[sl, :], b_ref[sl, :], dm)
            gam_ref[sl, :] = gam

    a, aqk, gam = _call(scores, grid=(h, s // rb), ins=[big(q), big(k), col(beta), col(g)],
                        outs=[o_sq, o_sq, o_col], name="dn_scores")
    n_prob = h * (s // CHUNK)
    t_l = _tri_solve_lanes(jnp.transpose(a.reshape(n_prob, CHUNK * CHUNK)))
    t = jnp.transpose(t_l).reshape(h, s, CHUNK)

    def wy(k_ref, v_ref, b_ref, gam_ref, t_ref, u_ref, w_ref):
        for i in range(cb):
            sl = slice(i * CHUNK, (i + 1) * CHUNK)
            kb = k_ref[sl, :] * b_ref[sl, :]
            rhs = jnp.concatenate([v_ref[sl, :] * b_ref[sl, :], kb * jnp.exp(gam_ref[sl, :])], axis=1)
            uw = _mdot(t_ref[sl, :], rhs)
            u_ref[sl, :] = uw[:, :HEAD_DIM]
            w_ref[sl, :] = uw[:, HEAD_DIM:]

    u, w = _call(wy, grid=(h, s // rb), ins=[big(k), big(v), col(beta), col(gam), sq(t)], outs=[o_big, o_big],
                 name="dn_wy")
    return u, w, aqk, t, gam


def _dn_prep_bwd(q, k, v, beta, gam, t, du, dw, daqk, dqd, dkd, dgl):
    h, s, _ = q.shape
    cb = _prep_group(s)
    rb = cb * CHUNK

    def body(q_ref, k_ref, v_ref, b_ref, g_ref, t_ref, du_ref, dw_ref, da_ref, dqd_ref, dkd_ref, dgl_ref,
             dq_ref, dk_ref, dv_ref, db_ref, dg_ref):
        slices = [slice(i * CHUNK, (i + 1) * CHUNK) for i in range(cb)]
        results = _lockstep([_chunk_prep_bwd(
            q_ref[sl, :], k_ref[sl, :], v_ref[sl, :], b_ref[sl, :], g_ref[sl, :], t_ref[sl, :],
            du_ref[sl, :], dw_ref[sl, :], da_ref[sl, :], dqd_ref[sl, :], dkd_ref[sl, :], dgl_ref[sl, :])
            for sl in slices])
        for sl, (dq, dk, dv, dbeta, dg) in zip(slices, results):
            dq_ref[sl, :] = dq
            dk_ref[sl, :] = dk
            dv_ref[sl, :] = dv
            db_ref[sl, :] = dbeta
            dg_ref[sl, :] = dg

    big = lambda x: (x, (None, rb, HEAD_DIM), lambda hh, n: (hh, n, 0))
    sq = lambda x: (x, (None, rb, CHUNK), lambda hh, n: (hh, n, 0))
    col = lambda x: (x, (None, rb, 1), lambda hh, n: (hh, n, 0))
    o_big = ((h, s, HEAD_DIM), F32, (None, rb, HEAD_DIM), lambda hh, n: (hh, n, 0))
    o_col = ((h, s, 1), F32, (None, rb, 1), lambda hh, n: (hh, n, 0))
    return _call(body, grid=(h, s // rb),
                 ins=[big(q), big(k), big(v), col(beta), col(gam), sq(t), big(du), big(dw), sq(daqk), big(dqd), big(dkd),
                      col(dgl)],
                 outs=[o_big, o_big, o_big, o_col, o_col], name="dn_prep_bwd")


def _chunk_scaled(q, k, gam):
    gam_last = gam[CHUNK - 1:CHUNK, :]
    q_dec = q * (HEAD_DIM ** -0.5) * jnp.exp(gam)
    k_dec = k * jnp.exp(gam_last - gam)
    return q_dec, k_dec, jnp.exp(gam_last)


def _dn_scan(q, k, u, w, aqk, gam):
    h, s, _ = q.shape
    nch = s // CHUNK

    def body(q_ref, k_ref, u_ref, w_ref, a_ref, gam_ref, o_ref, st_ref, state):
        @pl.when(pl.program_id(0) == 0)
        def _():
            state[...] = jnp.zeros_like(state)

        def head(hh):
            s0 = state[hh]
            st_ref[hh] = s0
            q_dec, k_dec, gl = _chunk_scaled(q_ref[hh], k_ref[hh], gam_ref[hh])
            both = _mdot(jnp.concatenate([w_ref[hh], q_dec], axis=0), s0)
            yield
            v_new = u_ref[hh] - both[:CHUNK]
            o_ref[:, hh * HEAD_DIM:(hh + 1) * HEAD_DIM] = both[CHUNK:] + _mdot(a_ref[hh], v_new)
            state[hh] = s0 * gl + _mdot(k_dec, v_new, TN)

        _lockstep([head(hh) for hh in range(h)])

    big = lambda x: (x, (h, CHUNK, HEAD_DIM), lambda n: (0, n, 0))
    return _call(body, grid=(nch,),
                 ins=[big(q), big(k), big(u), big(w), (aqk, (h, CHUNK, CHUNK), lambda n: (0, n, 0)),
                      (gam, (h, CHUNK, 1), lambda n: (0, n, 0))],
                 outs=[((s, h * HEAD_DIM), F32, (CHUNK, h * HEAD_DIM), lambda n: (n, 0)),
                       ((nch, h, HEAD_DIM, HEAD_DIM), F32, (None, h, HEAD_DIM, HEAD_DIM), lambda n: (n, 0, 0, 0))],
                 name="dn_scan", scratch=[pltpu.VMEM((h, HEAD_DIM, HEAD_DIM), F32)])


def _dn_scan_bwd(q, k, u, w, aqk, gam, states, do):
    h, s, _ = q.shape
    nch = s // CHUNK

    def body(q_ref, k_ref, u_ref, w_ref, a_ref, gam_ref, st_ref, do_ref,
             du_ref, dw_ref, da_ref, dqd_ref, dkd_ref, dgl_ref, dstate):
        @pl.when(pl.program_id(0) == 0)
        def _():
            dstate[...] = jnp.zeros_like(dstate)

        def head(hh):
            s0 = st_ref[hh]
            ds = dstate[hh]
            doh = do_ref[:, hh * HEAD_DIM:(hh + 1) * HEAD_DIM]
            wv = w_ref[hh]
            q_dec, k_dec, gl = _chunk_scaled(q_ref[hh], k_ref[hh], gam_ref[hh])
            ws = _mdot(wv, s0)
            dv_new = _mdot(a_ref[hh], doh, TN) + _mdot(k_dec, ds)
            dqd_ref[hh] = _mdot(doh, s0, NT)
            qdo = _mdot(q_dec, doh, TN)
            tot = jnp.sum(jnp.sum(s0 * ds, axis=-1, keepdims=True), axis=0, keepdims=True)
            dgl_ref[hh] = jnp.broadcast_to(tot, (CHUNK, 1))
            yield
            v_new = u_ref[hh] - ws
            du_ref[hh] = dv_new
            dw_ref[hh] = -_mdot(dv_new, s0, NT)
            da_ref[hh] = _mdot(doh, v_new, NT)
            dkd_ref[hh] = _mdot(v_new, ds, NT)
            dstate[hh] = ds * gl + qdo - _mdot(wv, dv_new, TN)

        _lockstep([head(hh) for hh in range(h)])

    rev = lambda n: (0, nch - 1 - n, 0)
    big = lambda x: (x, (h, CHUNK, HEAD_DIM), rev)
    o_big = ((h, s, HEAD_DIM), F32, (h, CHUNK, HEAD_DIM), rev)
    return _call(body, grid=(nch,),
                 ins=[big(q), big(k), big(u), big(w), (aqk, (h, CHUNK, CHUNK), rev), (gam, (h, CHUNK, 1), rev),
                      (states, (None, h, HEAD_DIM, HEAD_DIM), lambda n: (nch - 1 - n, 0, 0, 0)),
                      (do, (CHUNK, h * HEAD_DIM), lambda n: (nch - 1 - n, 0))],
                 outs=[o_big, o_big, ((h, s, CHUNK), F32, (h, CHUNK, CHUNK), rev), o_big, o_big,
                       ((h, s, 1), F32, (h, CHUNK, 1), rev)],
                 name="dn_scan_bwd", scratch=[pltpu.VMEM((h, HEAD_DIM, HEAD_DIM), F32)])


def _gates(x, a_log, dt_b, h):
    lane = lax.broadcasted_iota(jnp.int32, x.shape, 1)
    return jnp.where(lane < h, jax.nn.sigmoid(x), -jnp.exp(a_log) * jax.nn.softplus(x + dt_b))


def _head_out(oh, zh, nw):
    on = oh * lax.rsqrt(jnp.mean(oh * oh, axis=-1, keepdims=True) + RMS_EPS) * nw
    return on * jax.nn.silu(zh)


def _to_heads(x, h):
    return jnp.transpose(x[:, :h])[:, :, None]


def _pad_lanes(x, lo):
    return jnp.zeros((1, LANES), F32).at[0, lo:lo + x.shape[0]].set(x)


def _deltanet_fwd(hin, get_w_in, conv_w, a_log, dt_bias, norm_w, get_w_out):
    h = a_log.shape[0]
    hw = h * HEAD_DIM
    w_in = get_w_in(hin)
    proj = _mm_nn(hin, w_in, F32, "dn_proj")
    q = _dn_conv_fwd(proj, conv_w, 0, h, True, "dn_conv_q")
    k = _dn_conv_fwd(proj, conv_w, h, h, True, "dn_conv_k")
    v = _dn_conv_fwd(proj, conv_w, 2 * h, h, False, "dn_conv_v")
    alp, dtp = _pad_lanes(a_log, h), _pad_lanes(dt_bias, h)

    def gates_fn(x, al, db):
        return (_gates(x, al, db, h),), ()

    (bg,), _ = _rowmap(gates_fn, [(proj, LANES, 4 * h)], [alp, dtp], [(LANES, F32)], [], "dn_gates")
    beta, g = _to_heads(bg, h), _to_heads(bg[:, h:], h)
    u, w, aqk, t, gam = _dn_prep(q, k, v, beta, g)
    o, states = _dn_scan(q, k, u, w, aqk, gam)
    nw = norm_w[None, :]

    def out_fn(o, z, nw):
        parts = [_head_out(o[:, i * HEAD_DIM:(i + 1) * HEAD_DIM], z[:, i * HEAD_DIM:(i + 1) * HEAD_DIM], nw)
                 for i in range(h)]
        return (jnp.concatenate(parts, axis=-1),), ()

    (og,), _ = _rowmap(out_fn, [o, (proj, hw, 3)], [nw], [(hw, MXU_DTYPE)], [], "dn_out")
    w_out = get_w_out(og)
    y = _mm_nn(og, w_out, F32, "dn_y")
    return y, (hin, proj, q, k, v, beta, g, u, w, aqk, t, gam, states, o, og, alp, dtp, nw, w_in, w_out)


def _deltanet_bwd(res, dy, conv_w):
    hin, proj, q, k, v, beta, g, u, w, aqk, t, gam, states, o, og, alp, dtp, nw, w_in, w_out = res
    h = q.shape[0]
    hw = h * HEAD_DIM
    s = hin.shape[0]
    d_w_out = _mm_tn(og, dy, MXU_DTYPE, "dn_dwout")
    dog = _mm_nt(dy, w_out, F32, "dn_dog")

    def out_bwd(o, z, dog, nw):
        dos, dzs = [], []
        dn = jnp.zeros((1, HEAD_DIM), F32)
        for i in range(h):
            sl = slice(i * HEAD_DIM, (i + 1) * HEAD_DIM)
            _, vjp = jax.vjp(_head_out, o[:, sl], z[:, sl], nw)
            a, b, c = vjp(dog[:, sl])
            dos.append(a)
            dzs.append(b)
            dn = dn + c
        return (jnp.concatenate(dos, axis=-1), jnp.concatenate(dzs, axis=-1)), (dn,)

    (do, dz), (d_norm_w,) = _rowmap(out_bwd, [o, (proj, hw, 3), dog], [nw], [(hw, F32), (hw, MXU_DTYPE)],
                                    [(1, HEAD_DIM)], "dn_out_bwd")
    du, dw, daqk, dqd, dkd, dgl = _dn_scan_bwd(q, k, u, w, aqk, gam, states, do)
    dq, dk, dv, dbeta, dg = _dn_prep_bwd(q, k, v, beta, gam, t, du, dw, daqk, dqd, dkd, dgl)
    dpq, dwq = _dn_conv_bwd(proj, conv_w, dq, 0, h, True, "dn_conv_q_bwd")
    dpk, dwk = _dn_conv_bwd(proj, conv_w, dk, h, h, True, "dn_conv_k_bwd")
    dpv, dwv = _dn_conv_bwd(proj, conv_w, dv, 2 * h, h, False, "dn_conv_v_bwd")
    dbg = jnp.concatenate([jnp.transpose(dbeta[:, :, 0]), jnp.transpose(dg[:, :, 0]),
                           jnp.zeros((s, LANES - 2 * h), F32)], axis=1)

    def gates_bwd(x, dbg, al, db):
        _, vjp = jax.vjp(functools.partial(_gates, h=h), x, al, db)
        gx, gal, gdb = vjp(dbg)
        return (gx,), (gal, gdb)

    (dba,), (d_alp, d_dtp) = _rowmap(gates_bwd, [(proj, LANES, 4 * h), dbg], [alp, dtp], [(LANES, MXU_DTYPE)],
                                     [(1, LANES), (1, LANES)], "dn_gates_bwd")
    dproj = jnp.concatenate([dpq, dpk, dpv, dz, dba], axis=1)
    d_w_in = _mm_tn(hin, dproj, MXU_DTYPE, "dn_dwin")
    dh = _mm_nt(dproj, w_in, F32, "dn_dh")
    d_conv_w = jnp.concatenate([dwq, dwk, dwv], axis=1)
    return dh, dict(w_in=d_w_in, w_out=d_w_out, conv_w=d_conv_w, a_log=d_alp[0, h:2 * h], dt_bias=d_dtp[0, h:2 * h],
                    norm_w=d_norm_w[0])


def _ln_silu(u, g, b):
    return jax.nn.silu(_ln(u, g, b))


def _conformer_fwd(hin, get_w_in, dw_w, dw_b, ln_g, ln_b, get_w_out):
    w_in = get_w_in(hin)
    vg = _mm_nn(hin, w_in, F32, "cf_vg")
    u1 = _cf_conv_fwd(vg, dw_w, dw_b)
    ch = u1.shape[1]

    def fn(u, g, b):
        return (_ln_silu(u, g, b),), ()

    (u2,), _ = _rowmap(fn, [u1], [ln_g, ln_b], [(ch, MXU_DTYPE)], [], "cf_ln")
    w_out = get_w_out(u2)
    y = _mm_nn(u2, w_out, F32, "cf_y")
    return y, (hin, vg, u1, u2, w_in, w_out)


def _conformer_bwd(res, dy, dw_w, ln_g, ln_b):
    hin, vg, u1, u2, w_in, w_out = res
    ch = u1.shape[1]
    d_w_out = _mm_tn(u2, dy, MXU_DTYPE, "cf_dwout")
    du2 = _mm_nt(dy, w_out, F32, "cf_du2")

    def fn(u, du2, g, b):
        _, vjp = jax.vjp(_ln_silu, u, g, b)
        gu, gg, gb = vjp(du2)
        return (gu,), (gg, gb)

    (du1,), (d_ln_g, d_ln_b) = _rowmap(fn, [u1, du2], [ln_g, ln_b], [(ch, F32)], [(1, ch), (1, ch)], "cf_ln_bwd")
    dval, dgate, d_dw_w, d_dw_b = _cf_conv_bwd(vg, dw_w, du1)
    dvg = jnp.concatenate([dval, dgate], axis=1)
    d_w_in = _mm_tn(hin, dvg, MXU_DTYPE, "cf_dwin", split_cols=True)
    dh = _mm_nt(dvg, w_in, F32, "cf_dh")
    return dh, dict(w_in=d_w_in, w_out=d_w_out, dw_w=d_dw_w, dw_b=d_dw_b[0], ln_g=d_ln_g[0], ln_b=d_ln_b[0])


def _mlp_fwd(hin, get_w1, get_w2):
    w1 = get_w1(hin)
    a, r = _mm_nn(hin, w1, MXU_DTYPE, "ff_a", relu2=True)
    w2 = get_w2(r)
    m = _mm_nn(r, w2, F32, "ff_m")
    return m, (hin, a, r, w1, w2)


def _mlp_bwd(res, dm):
    hin, a, r, w1, w2 = res
    f = a.shape[1]
    d_w2 = _mm_tn(r, dm, MXU_DTYPE, "ff_dw2")
    dr = _mm_nt(dm, w2, F32, "ff_dr")

    def fn(a, dr):
        return ((dr * 2.0 * jnp.maximum(a, 0.0)),), ()

    (da,), _ = _rowmap(fn, [a, dr], [], [(f, MXU_DTYPE)], [], "ff_da")
    d_w1 = _mm_tn(hin, da, MXU_DTYPE, "ff_dw1", split_cols=True)
    dh = _mm_nt(da, w1, F32, "ff_dh")
    return dh, d_w1, d_w2


def _ada_fwd(c_all, ada_w):
    depth, d, nl = ada_w.shape
    tn = _tile(nl, 256)

    def body(c_ref, w_ref, o_ref, cond_ref):
        cond = jax.nn.silu(c_ref[...]).astype(MXU_DTYPE)
        cond_ref[...] = cond
        o_ref[...] = lax.dot_general(cond, w_ref[...].astype(MXU_DTYPE), (NN, ((), ())), preferred_element_type=F32)

    return _call(body, grid=(depth, nl // tn),
                 ins=[(c_all, c_all.shape, lambda l, j: (0, 0)), (ada_w, (None, d, tn), lambda l, j: (l, 0, j))],
                 outs=[((depth, N_DEV, nl), F32, (None, N_DEV, tn), lambda l, j: (l, 0, j)),
                       (c_all.shape, MXU_DTYPE, c_all.shape, lambda l, j: (0, 0))],
                 name="ada_fwd")


def _ada_bwd(cond_all, dmod_cols):
    depth, _, nl = dmod_cols.shape
    d = cond_all.shape[1]
    tn = _tile(nl, 256)

    def body(c_ref, g_ref, o_ref):
        o_ref[...] = lax.dot_general(c_ref[...], g_ref[...].astype(MXU_DTYPE), (TN, ((), ())),
                                     preferred_element_type=F32)

    return _call(body, grid=(depth, nl // tn),
                 ins=[(cond_all, cond_all.shape, lambda l, j: (0, 0)), (dmod_cols, (None, N_DEV, tn), lambda l, j: (l, 0, j))],
                 outs=[((depth, d, nl), F32, (None, d, tn), lambda l, j: (l, 0, j))], name="ada_bwd")[0]


def _peers():
    x, y, c = lax.axis_index("x"), lax.axis_index("y"), lax.axis_index("c")
    peers = []
    for k in range(1, N_DEV):
        px = 1 - x if k & 4 else x
        py = 1 - y if k & 2 else y
        pc = 1 - c if k & 1 else c
        peers.append(((px, py, pc), 4 * px + 2 * py + pc))
    return 4 * x + 2 * y + c, peers


_HBM = pl.BlockSpec(memory_space=pltpu.HBM)
_SEM = pl.BlockSpec(memory_space=pltpu.SEMAPHORE)
_ANY = pl.BlockSpec(memory_space=pl.ANY)
_EFFECT = pltpu.SideEffectType.DATAFLOW_SIDE_EFFECTING


def _xfer_start(srcs, lands, scatter, after, name):
    nt = len(srcs)

    def body(*refs):
        src, land = refs[:nt], refs[nt:2 * nt]
        sems = refs[2 * nt + 1:4 * nt + 1]
        token = refs[-1]
        me, peers = _peers()
        for t in range(nt):
            for k, (pid, plin) in enumerate(peers):
                pltpu.make_async_remote_copy(
                    src_ref=src[t].at[plin] if scatter else src[t], dst_ref=land[t].at[me],
                    send_sem=sems[2 * t].at[k], recv_sem=sems[2 * t + 1].at[k],
                    device_id=pid, device_id_type=pl.DeviceIdType.MESH).start()
        token[...] = jnp.zeros_like(token)

    out_shape = [pltpu.SemaphoreType.DMA((N_DEV - 1,)) for _ in range(2 * nt)]
    out_shape += [pltpu.HBM(a.shape, a.dtype) for a in list(srcs) + list(lands)]
    out_shape += [jax.ShapeDtypeStruct((8, LANES), F32)]
    res = pl.pallas_call(
        body, name=name, out_shape=out_shape,
        in_specs=[_HBM] * (2 * nt) + [_ANY],
        out_specs=[_SEM] * (2 * nt) + [_HBM] * (2 * nt) + [pl.BlockSpec(memory_space=pltpu.VMEM)],
        input_output_aliases={i: 2 * nt + i for i in range(2 * nt)},
        compiler_params=pltpu.CompilerParams(has_side_effects=_EFFECT),
    )(*[pltpu.with_memory_space_constraint(a, pltpu.HBM) for a in list(srcs) + list(lands)], after)
    sems, thru = res[:2 * nt], res[2 * nt:4 * nt]
    return [(sems[2 * t], sems[2 * t + 1], thru[t], thru[nt + t]) for t in range(nt)], res[-1]


def _xfer_wait(handle, scatter, after, name):
    send, recv, src, land = handle

    def body(src_ref, land_ref, send_sem, recv_sem, after_ref, src_dead, land_out):
        _, peers = _peers()
        for k, (pid, plin) in enumerate(peers):
            cp = pltpu.make_async_remote_copy(
                src_ref=src_ref.at[plin] if scatter else src_ref, dst_ref=land_ref.at[plin],
                send_sem=send_sem.at[k], recv_sem=recv_sem.at[k],
                device_id=pid, device_id_type=pl.DeviceIdType.MESH)
            cp.wait_send()
            cp.wait_recv()

    return pl.pallas_call(
        body, name=name, out_shape=(pltpu.HBM(src.shape, src.dtype), pltpu.HBM(land.shape, land.dtype)),
        in_specs=(_HBM, _HBM, _SEM, _SEM, _ANY), out_specs=(_HBM, _HBM), input_output_aliases={0: 0, 1: 1},
        compiler_params=pltpu.CompilerParams(has_side_effects=_EFFECT),
    )(src, land, send, recv, after)[1]


def _landing(x, me):
    return lax.dynamic_update_slice(lax.empty((N_DEV,) + x.shape, x.dtype), x[None], (me,) + (0,) * x.ndim)


def _landing_scatter(p, me):
    own = lax.dynamic_index_in_dim(p, me, axis=0, keepdims=True)
    return lax.dynamic_update_slice(lax.empty(p.shape, p.dtype), own, (me,) + (0,) * (p.ndim - 1))


def _exchange(arrs, scatter, name):
    nt = len(arrs)
    out_shape = [jax.ShapeDtypeStruct(a.shape if scatter else (N_DEV,) + a.shape, a.dtype) for a in arrs]

    def body(*refs):
        ins, outs = refs[:nt], refs[nt:2 * nt]
        send, recv, loc = refs[2 * nt:]
        me, peers = _peers()
        copies = []
        for t in range(nt):
            own = pltpu.make_async_copy(ins[t].at[me] if scatter else ins[t], outs[t].at[me], loc.at[t])
            own.start()
            copies.append(own)
            for k, (pid, plin) in enumerate(peers):
                cp = pltpu.make_async_remote_copy(
                    src_ref=ins[t].at[plin] if scatter else ins[t], dst_ref=outs[t].at[me],
                    send_sem=send.at[t, k], recv_sem=recv.at[t, k],
                    device_id=pid, device_id_type=pl.DeviceIdType.MESH)
                cp.start()
                copies.append(cp)
        for cp in copies:
            cp.wait()

    any_spec = pl.BlockSpec(memory_space=pl.ANY)
    return pl.pallas_call(
        body, out_shape=out_shape, in_specs=[any_spec] * nt, out_specs=[any_spec] * nt,
        scratch_shapes=[pltpu.SemaphoreType.DMA((nt, N_DEV - 1)), pltpu.SemaphoreType.DMA((nt, N_DEV - 1)),
                        pltpu.SemaphoreType.DMA((nt,))],
        name=name)(*arrs)


def _adamw_body(n_parts):
    def body(p_ref, w_ref, m_ref, v_ref, *rest):
        g_out, d_out, m_out, v_out = rest[-4:]
        g = p_ref[0].astype(F32)
        for i in range(1, n_parts):
            g = g + p_ref[i].astype(F32)
        m2 = ADAM_B1 * m_ref[...] + (1.0 - ADAM_B1) * g
        v2 = ADAM_B2 * v_ref[...] + (1.0 - ADAM_B2) * jnp.square(g)
        m_hat = m2 / (1.0 - ADAM_B1 ** ADAM_STEP)
        v_hat = v2 / (1.0 - ADAM_B2 ** ADAM_STEP)
        g_out[...] = g
        d_out[...] = -ADAM_LR * (m_hat / (jnp.sqrt(v_hat) + ADAM_EPS) + ADAM_WD * w_ref[...])
        m_out[...] = m2
        v_out[...] = v2

    return body


def _adamw_layer(parts, w, m, v, layer, prev, name):
    p, r, c = parts.shape
    tr = _tile(r, 256, 8)
    blk = pl.BlockSpec((None, tr, c), lambda i: (layer, i, 0))
    in_specs = [pl.BlockSpec((p, tr, c), lambda i: (0, i, 0)), blk, blk, blk]
    args = [parts, w, m, v]
    aliases = {}
    if prev is not None:
        in_specs += [_ANY] * 4
        args += list(prev)
        aliases = {4 + i: i for i in range(4)}
    return pl.pallas_call(
        _adamw_body(p), grid=(r // tr,), in_specs=in_specs, out_specs=[blk] * 4,
        out_shape=[jax.ShapeDtypeStruct(w.shape, F32)] * 4, input_output_aliases=aliases, name=name,
        compiler_params=_cparams(1))(*args)


def _adamw(parts, w, m, v, name):
    p, nl, r, c = parts.shape
    tr = _tile(r, 256, 8)
    body = _adamw_body(p)

    blk = (None, tr, c)
    imap = lambda l, i: (l, i, 0)
    out = ((nl, r, c), F32, blk, imap)
    return _call(body, grid=(nl, r // tr),
                 ins=[(parts, (p, None, tr, c), lambda l, i: (0, l, i, 0)), (w, blk, imap), (m, blk, imap), (v, blk, imap)],
                 outs=[out] * 4, name=name)


def _rows(x):
    return x.reshape(-1, LANES)


def _pad_rows(x, mult=8):
    r = x.shape[0]
    extra = (-r) % mult
    return jnp.pad(x, ((0, extra), (0, 0))) if extra else x


def _shard_cols(x, me, groups):
    lead = x.shape[:-1]
    xr = x.reshape(lead + (N_DEV, groups * LANES))
    xs = lax.dynamic_index_in_dim(xr, me, axis=len(lead), keepdims=False)
    return xs.reshape(N_DEV, -1, LANES)


def kernel(x, c, ada_w, ada_b, ln_g, ln_b, dn_w_in, dn_conv_w, dn_a_log, dn_dt_bias, dn_norm_w, dn_w_out, cf_w_in, cf_dw_w, cf_dw_b, cf_ln_g, cf_ln_b, cf_w_out, ff_w1, ff_w2, loss_target, m_ada_w, m_ada_b, m_ln_g, m_ln_b, m_dn_w_in, m_dn_conv_w, m_dn_a_log, m_dn_dt_bias, m_dn_norm_w, m_dn_w_out, m_cf_w_in, m_cf_dw_w, m_cf_dw_b, m_cf_ln_g, m_cf_ln_b, m_cf_w_out, m_ff_w1, m_ff_w2, v_ada_w, v_ada_b, v_ln_g, v_ln_b, v_dn_w_in, v_dn_conv_w, v_dn_a_log, v_dn_dt_bias, v_dn_norm_w, v_dn_w_out, v_cf_w_in, v_cf_dw_w, v_cf_dw_b, v_cf_ln_g, v_cf_ln_b, v_cf_w_out, v_ff_w1, v_ff_w2):
    depth, d, _ = ada_w.shape
    n_a, n_b = dn_w_in.shape[0], cf_w_in.shape[0]
    heads = dn_a_log.shape[1]
    hw = heads * HEAD_DIM
    taps = cf_dw_w.shape[1]
    s = x.shape[1]
    alpha = (2.0 * depth) ** 0.25
    me = 4 * lax.axis_index("x") + 2 * lax.axis_index("y") + lax.axis_index("c")
    xs, tgt = x[0], loss_target[0]

    small_local = [_rows(ln_g), _rows(ln_b), _rows(dn_conv_w), _rows(cf_dw_w), _rows(cf_dw_b), _rows(cf_ln_g),
                   _rows(cf_ln_b), _rows(c)]
    sizes = [a.shape[0] for a in small_local]
    packed = _pad_rows(jnp.concatenate(small_local, axis=0))
    (small_all,) = _exchange([packed], False, "comm_gather_params")
    offs = [0]
    for z in sizes:
        offs.append(offs[-1] + z)

    def small(i):
        return small_all[:, offs[i]:offs[i + 1], :]

    def unshard(piece, lead, groups):
        t = piece.reshape((N_DEV,) + lead + (groups * LANES,))
        t = jnp.moveaxis(t, 0, len(lead))
        return t.reshape(lead + (N_DEV * groups * LANES,))

    ln_g_f = unshard(small(0), (depth, 2), 1)
    ln_b_f = unshard(small(1), (depth, 2), 1)
    conv_w_f = unshard(small(2), (n_a, DN_CONV), 3 * heads // N_DEV)
    dw_w_f = unshard(small(3), (n_b, taps), 1)
    dw_b_f = unshard(small(4), (n_b,), 1)
    cf_ln_g_f = unshard(small(5), (n_b,), 1)
    cf_ln_b_f = unshard(small(6), (n_b,), 1)
    c_all = small(7).reshape(N_DEV, d)

    mod_part, cond_all = _ada_fwd(c_all, ada_w)
    (mod_all,) = _exchange([mod_part], False, "comm_gather_mod")
    mod_mine = lax.dynamic_index_in_dim(mod_all, me, axis=2, keepdims=False)
    mod_mine = jnp.moveaxis(mod_mine, 0, 1).reshape(depth, N_MOD * d)

    dn_in_cols = dn_w_in.shape[2]
    keys, shards = [], []
    for i in range(depth):
        j = i // 2
        mixer = [("dn_in", dn_w_in), ("dn_out", dn_w_out)] if i % 2 == 0 else [("cf_in", cf_w_in), ("cf_out", cf_w_out)]
        for nm, wt in mixer:
            keys.append((nm, j))
            shards.append(wt[j].astype(MXU_DTYPE))
        keys += [("ff1", i), ("ff2", i)]
        shards += [ff_w1[i].astype(MXU_DTYPE), ff_w2[i].astype(MXU_DTYPE)]
    handles, token = _xfer_start(shards, [_landing(a, me) for a in shards], False, mod_all, "gather_weights_start")
    handles = dict(zip(keys, handles))
    weights = {}

    def gathered(key, after):
        if key not in weights:
            weights[key] = _xfer_wait(handles[key], False, after, "gather_wait_%s_%d" % key)
        return weights[key]

    def get_dn_in(j):
        def get(after):
            g = gathered(("dn_in", j), after)
            w = jnp.moveaxis(g, 0, 1).reshape(d, N_DEV * dn_in_cols)
            return jnp.pad(w, ((0, 0), (0, 4 * hw + LANES - N_DEV * dn_in_cols)))
        return get

    def get_rows(key):
        return lambda after: gathered(key, after).reshape((-1, d))

    def get_cols(key):
        return lambda after: gathered(key, after)

    def add_bias(a, b):
        return (a + b,), ()

    (mod,), _ = _rowmap(add_bias, [mod_mine + token[0, 0], ada_b], [], [(N_MOD * d, F32)], [], "ada_bias")

    def mod_row(i, j):
        return mod[i:i + 1, j * d:(j + 1) * d]

    def ln_row(arr, i, j):
        return arr[i, j][None, :]

    subs = []
    h_cur = _modulate_fwd(xs, mod_row(0, 1), mod_row(0, 0))
    x_cur = xs
    last = None
    for i in range(depth):
        j = i // 2
        if i % 2 == 0:
            y, res = _deltanet_fwd(h_cur, get_dn_in(j), conv_w_f[j], dn_a_log[j], dn_dt_bias[j], dn_norm_w[j],
                                   get_rows(("dn_out", j)))
        else:
            y, res = _conformer_fwd(h_cur, get_cols(("cf_in", j)), dw_w_f[j], dw_b_f[j][None, :], cf_ln_g_f[j][None, :],
                                    cf_ln_b_f[j][None, :], get_rows(("cf_out", j)))
        p1 = (mod_row(i, 2), ln_row(ln_g_f, i, 0), ln_row(ln_b_f, i, 0), mod_row(i, 4), mod_row(i, 3))
        x_mid, h_mid = _combine_fwd(alpha, x_cur, y, *p1)
        subs.append((x_cur, y, p1, res))
        m_out, res2 = _mlp_fwd(h_mid, get_cols(("ff1", i)), get_rows(("ff2", i)))
        if i + 1 < depth:
            p2 = (mod_row(i, 5), ln_row(ln_g_f, i, 1), ln_row(ln_b_f, i, 1), mod_row(i + 1, 1), mod_row(i + 1, 0))
            x_next, h_next = _combine_fwd(alpha, x_mid, m_out, *p2)
            subs.append((x_mid, m_out, p2, res2))
            x_cur, h_cur = x_next, h_next
        else:
            p2 = (mod_row(i, 5), ln_row(ln_g_f, i, 1), ln_row(ln_b_f, i, 1))
            last = (x_mid, m_out, p2, res2)

    x_in, y_in, p_last, res_last = last
    dx, dy, (loss_acc, g_gt, g_g, g_b) = _last_fwd_bwd(alpha, x_in, y_in, tgt, *p_last)
    loss = lax.psum(loss_acc[0, 0], ("x", "y", "c"))

    d_mod = [[None] * N_MOD for _ in range(depth)]
    d_ln_g = [[None, None] for _ in range(depth)]
    d_ln_b = [[None, None] for _ in range(depth)]
    d_mod[depth - 1][5], d_ln_g[depth - 1][1], d_ln_b[depth - 1][1] = g_gt, g_g, g_b
    gw = dict(dn=[None] * n_a, cf=[None] * n_b)

    sent = {}

    def send_grads(named, tag):
        parts = [p for _, p in named]
        hs, tok = _xfer_start(parts, [_landing_scatter(p, me) for p in parts], True, parts[0], "scatter_start_" + tag)
        for (key, _), hnd in zip(named, hs):
            sent[key] = hnd
        return tok[0, 0]

    def by_rows(g):
        return g.reshape((N_DEV, g.shape[0] // N_DEV, g.shape[1]))

    def send_mlp(i, d_w1, d_w2):
        return send_grads([(("ff1", i), d_w1), (("ff2", i), by_rows(d_w2))], "ff_%d" % i)

    dh, d_w1, d_w2 = _mlp_bwd(res_last, dy)
    pin = send_mlp(depth - 1, d_w1, d_w2)
    for idx in range(len(subs) - 1, -1, -1):
        x_in, y_in, prm, res = subs[idx]
        i, second = idx // 2, idx % 2
        prm = (prm[0] + pin,) + tuple(prm[1:])
        dx, dy, (g_gt, g_g, g_b, g_sc, g_sh) = _combine_bwd(alpha, x_in, y_in, dx, dh, *prm)
        d_mod[i][5 if second else 2], d_ln_g[i][second], d_ln_b[i][second] = g_gt, g_g, g_b
        nxt_i, nxt_base = (i + 1, 0) if second else (i, 3)
        d_mod[nxt_i][nxt_base + 1], d_mod[nxt_i][nxt_base] = g_sc, g_sh
        j = i // 2
        if second:
            dh, d_w1, d_w2 = _mlp_bwd(res, dy)
            pin = send_mlp(i, d_w1, d_w2)
        elif i % 2 == 0:
            dh, gw["dn"][j] = _deltanet_bwd(res, dy, conv_w_f[j])
            d_in = gw["dn"][j]["w_in"][:, :N_DEV * dn_in_cols].reshape(d, N_DEV, dn_in_cols)
            pin = send_grads([(("dn_in", j), jnp.moveaxis(d_in, 1, 0)), (("dn_out", j), by_rows(gw["dn"][j]["w_out"]))],
                             "dn_%d" % j)
        else:
            dh, gw["cf"][j] = _conformer_bwd(res, dy, dw_w_f[j], cf_ln_g_f[j][None, :], cf_ln_b_f[j][None, :])
            pin = send_grads([(("cf_in", j), gw["cf"][j]["w_in"]), (("cf_out", j), by_rows(gw["cf"][j]["w_out"]))],
                             "cf_%d" % j)
    grad_x, g_sc, g_sh = _modulate_bwd(xs, dx, dh, mod_row(0, 1) + pin, mod_row(0, 0))
    d_mod[0][1], d_mod[0][0] = g_sc, g_sh
    d_mod_full = jnp.concatenate([jnp.concatenate(r, axis=1) for r in d_mod], axis=0)

    stacked = {"dn_w_in": ("dn_in", dn_w_in, m_dn_w_in, v_dn_w_in), "dn_w_out": ("dn_out", dn_w_out, m_dn_w_out, v_dn_w_out),
               "cf_w_in": ("cf_in", cf_w_in, m_cf_w_in, v_cf_w_in), "cf_w_out": ("cf_out", cf_w_out, m_cf_w_out, v_cf_w_out),
               "ff_w1": ("ff1", ff_w1, m_ff_w1, v_ff_w1), "ff_w2": ("ff2", ff_w2, m_ff_w2, v_ff_w2)}
    chains = {key: None for key in stacked}

    def update_layer(i):
        mixer = ["dn_w_in", "dn_w_out"] if i % 2 == 0 else ["cf_w_in", "cf_w_out"]
        for key, idx in [("ff_w1", i), ("ff_w2", i)] + [(k, i // 2) for k in mixer]:
            short, w, m, v = stacked[key]
            parts = _xfer_wait(sent[(short, idx)], True, grad_x, "scatter_wait_%s_%d" % (short, idx))
            chains[key] = _adamw_layer(parts, w, m, v, idx, chains[key], "adamw_%s_%d" % (key, idx))

    for i in range(depth - 1, 0, -1):
        update_layer(i)

    def stack_rows(lst):
        return jnp.stack(lst, axis=0)

    gs_ln_g = jnp.stack([jnp.concatenate(r, axis=0) for r in d_ln_g], axis=0)
    gs_ln_b = jnp.stack([jnp.concatenate(r, axis=0) for r in d_ln_b], axis=0)
    gs_conv_w = stack_rows([gw["dn"][j]["conv_w"] for j in range(n_a)])
    gs_dw_w = stack_rows([gw["cf"][j]["dw_w"] for j in range(n_b)])
    gs_dw_b = stack_rows([gw["cf"][j]["dw_b"] for j in range(n_b)])
    gs_cf_ln_g = stack_rows([gw["cf"][j]["ln_g"] for j in range(n_b)])
    gs_cf_ln_b = stack_rows([gw["cf"][j]["ln_b"] for j in range(n_b)])
    gs_a_log = stack_rows([_pad_lanes(gw["dn"][j]["a_log"], 0)[0] for j in range(n_a)])
    gs_dt_bias = stack_rows([_pad_lanes(gw["dn"][j]["dt_bias"], 0)[0] for j in range(n_a)])
    gs_norm_w = stack_rows([gw["dn"][j]["norm_w"] for j in range(n_a)])
    small_grads = [gs_ln_g, gs_ln_b, gs_conv_w, gs_dw_w, gs_dw_b, gs_cf_ln_g, gs_cf_ln_b, gs_a_log, gs_dt_bias,
                   gs_norm_w, d_mod_full]
    sg_rows = [_rows(a) for a in small_grads]
    sg_sizes = [a.shape[0] for a in sg_rows]
    (sg_all,) = _exchange([_pad_rows(jnp.concatenate(sg_rows, axis=0))], False, "comm_gather_small_grads")
    sg_offs = [0]
    for z in sg_sizes:
        sg_offs.append(sg_offs[-1] + z)

    def sg(i, shape):
        return sg_all[:, sg_offs[i]:sg_offs[i + 1], :].reshape((N_DEV,) + shape)

    dmod_all = sg(10, (depth, N_MOD * d))
    nl = ada_w.shape[2]
    dmod_cols = lax.dynamic_slice_in_dim(dmod_all, me * nl, nl, axis=2)
    g_ada_w = _ada_bwd(cond_all, jnp.moveaxis(dmod_cols, 0, 1))

    outs = {}

    def run_adamw(key, parts, w, m, v):
        shp = w.shape
        as3 = lambda t: t.reshape((-1,) + shp[-2:]) if t.ndim >= 3 else t.reshape((1,) + shp)
        parts3 = parts.reshape((parts.shape[0],) + as3(w).shape)
        res = _adamw(parts3, as3(w), as3(m), as3(v), "adamw_" + key)
        outs[key] = tuple(r.reshape(shp) for r in res)

    run_adamw("ada_w", g_ada_w[None], ada_w, m_ada_w, v_ada_w)

    cgroups = 3 * heads // N_DEV
    shard_parts = [
        _shard_cols(sg(0, (depth, 2, d)), me, 1), _shard_cols(sg(1, (depth, 2, d)), me, 1),
        _shard_cols(sg(2, (n_a, DN_CONV, 3 * hw)), me, cgroups), _shard_cols(sg(3, (n_b, taps, d)), me, 1),
        _shard_cols(sg(4, (n_b, d)), me, 1), _shard_cols(sg(5, (n_b, d)), me, 1), _shard_cols(sg(6, (n_b, d)), me, 1),
    ]
    repl_parts = [sg(7, (n_a, LANES)), sg(8, (n_a, LANES)), sg(9, (n_a, HEAD_DIM)),
                  sg(10, (depth, N_MOD * d)).reshape(N_DEV, -1, LANES)]
    small_parts = shard_parts + repl_parts
    sp_sizes = [a.shape[1] for a in small_parts]
    parts_packed = jnp.concatenate(small_parts, axis=1)
    extra = (-parts_packed.shape[1]) % 8
    parts_packed = jnp.pad(parts_packed, ((0, 0), (0, extra), (0, 0)))

    def pad_heads(t):
        return jnp.pad(t, ((0, 0), (0, LANES - heads)))

    def pack_state(ln_g_, ln_b_, conv_w_, dw_w_, dw_b_, cln_g_, cln_b_, a_log_, dt_b_, norm_w_, ada_b_):
        rows = [_rows(ln_g_), _rows(ln_b_), _rows(conv_w_), _rows(dw_w_), _rows(dw_b_), _rows(cln_g_), _rows(cln_b_),
                pad_heads(a_log_), pad_heads(dt_b_), norm_w_, _rows(ada_b_)]
        return _pad_rows(jnp.concatenate(rows, axis=0))

    w_s = pack_state(ln_g, ln_b, dn_conv_w, cf_dw_w, cf_dw_b, cf_ln_g, cf_ln_b, dn_a_log, dn_dt_bias, dn_norm_w, ada_b)
    m_s = pack_state(m_ln_g, m_ln_b, m_dn_conv_w, m_cf_dw_w, m_cf_dw_b, m_cf_ln_g, m_cf_ln_b, m_dn_a_log,
                     m_dn_dt_bias, m_dn_norm_w, m_ada_b)
    v_s = pack_state(v_ln_g, v_ln_b, v_dn_conv_w, v_cf_dw_w, v_cf_dw_b, v_cf_ln_g, v_cf_ln_b, v_dn_a_log,
                     v_dn_dt_bias, v_dn_norm_w, v_ada_b)
    res_s = _adamw(parts_packed[:, None], w_s[None], m_s[None], v_s[None], "adamw_small")
    sp_offs = [0]
    for z in sp_sizes:
        sp_offs.append(sp_offs[-1] + z)
    small_keys = ["ln_g", "ln_b", "dn_conv_w", "cf_dw_w", "cf_dw_b", "cf_ln_g", "cf_ln_b", "dn_a_log", "dn_dt_bias",
                  "dn_norm_w", "ada_b"]
    small_shapes = [ln_g.shape, ln_b.shape, dn_conv_w.shape, cf_dw_w.shape, cf_dw_b.shape, cf_ln_g.shape,
                    cf_ln_b.shape, dn_a_log.shape, dn_dt_bias.shape, dn_norm_w.shape, ada_b.shape]
    for n, (key, shp) in enumerate(zip(small_keys, small_shapes)):
        vals = []
        for r in res_s:
            piece = r[0, sp_offs[n]:sp_offs[n + 1], :]
            if key in ("dn_a_log", "dn_dt_bias"):
                piece = piece[:, :heads]
            vals.append(piece.reshape(shp))
        outs[key] = tuple(vals)

    update_layer(0)
    for key in stacked:
        outs[key] = tuple(chains[key])

    order = ["ada_w", "ada_b", "ln_g", "ln_b", "dn_w_in", "dn_conv_w", "dn_a_log", "dn_dt_bias", "dn_norm_w",
             "dn_w_out", "cf_w_in", "cf_dw_w", "cf_dw_b", "cf_ln_g", "cf_ln_b", "cf_w_out", "ff_w1", "ff_w2"]
    result = [loss, grad_x[None]]
    for part in range(4):
        result += [outs[k][part] for k in order]
    return tuple(result)
```

```python
import functools

import jax
import jax.numpy as jnp
from jax import lax
from jax.experimental import pallas as pl
from jax.experimental.pallas import tpu as pltpu

F32 = jnp.float32
MXU_DTYPE = jnp.bfloat16
N_DEV = 8
LANES = 128
HEAD_DIM = 128
CHUNK = 64
DN_CONV = 4
N_MOD = 6
LN_EPS = 1e-5
RMS_EPS = 1e-6
L2_EPS = 1e-6
ADAM_LR = 0.001
ADAM_B1 = 0.9
ADAM_B2 = 0.999
ADAM_EPS = 1e-08
ADAM_WD = 0.01
ADAM_STEP = 10

HI = lax.Precision.HIGHEST
NN = ((1,), (0,))
NT = ((1,), (1,))
TN = ((0,), (0,))

ROW_TILE = 256
CONV_TILE = 256


def _mdot(a, b, dims=NN):
    return lax.dot_general(a.astype(MXU_DTYPE), b.astype(MXU_DTYPE), (dims, ((), ())), preferred_element_type=F32)


def _split3(x):
    hi = x.astype(MXU_DTYPE)
    r1 = x - hi.astype(F32)
    mid = r1.astype(MXU_DTYPE)
    lo = (r1 - mid.astype(F32)).astype(MXU_DTYPE)
    return hi, mid, lo


def _dot01(a, b, dims=NN, mask_first=True):
    d = lambda p, q: lax.dot_general(p, q, (dims, ((), ())), preferred_element_type=F32)
    if mask_first:
        m = a.astype(MXU_DTYPE)
        return sum(d(m, p) for p in _split3(b))
    m = b.astype(MXU_DTYPE)
    return sum(d(p, m) for p in _split3(a))


def _dot3(a, b, dims=NN):
    ah, am, _ = _split3(a)
    bh, bm, _ = _split3(b)
    d = lambda p, q: lax.dot_general(p, q, (dims, ((), ())), preferred_element_type=F32)
    return d(ah, bh) + (d(ah, bm) + d(am, bh))


def _cparams(n):
    return pltpu.CompilerParams(dimension_semantics=("arbitrary",) * n)


def _call(body, *, grid, ins, outs, name, scratch=()):
    res = pl.pallas_call(
        body,
        grid=grid,
        in_specs=[pl.BlockSpec(b, m) for _, b, m in ins],
        out_specs=[pl.BlockSpec(b, m) for _, _, b, m in outs],
        out_shape=[jax.ShapeDtypeStruct(s, d) for s, d, _, _ in outs],
        scratch_shapes=list(scratch),
        name=name,
        compiler_params=_cparams(len(grid)),
    )(*[a for a, _, _ in ins])
    return res


def _tile(n, pref, unit=LANES):
    if n <= pref:
        return n
    t = (pref // unit) * unit
    while t > unit and n % t:
        t -= unit
    assert n % t == 0, (n, pref)
    return t


def _rowmap(fn, rows, consts, row_outs, acc_outs, name):
    rows = [r if isinstance(r, tuple) else (r, r.shape[1], 0) for r in rows]
    s = rows[0][0].shape[0]
    tm = min(ROW_TILE, s)
    nr, nc, no, na = len(rows), len(consts), len(row_outs), len(acc_outs)

    def body(*refs):
        rin, cin = refs[:nr], refs[nr:nr + nc]
        rout, aout = refs[nr + nc:nr + nc + no], refs[nr + nc + no:]
        ro, ao = fn(*[r[...] for r in rin], *[c[...] for c in cin])
        for ref, val in zip(rout, ro):
            ref[...] = val.astype(ref.dtype)
        if na:
            first = pl.program_id(0) == 0

            @pl.when(first)
            def _():
                for ref, val in zip(aout, ao):
                    ref[...] = val

            @pl.when(jnp.logical_not(first))
            def _():
                for ref, val in zip(aout, ao):
                    ref[...] += val

    ins = [(a, (tm, w), functools.partial(lambda i, cb: (i, cb), cb=cb)) for a, w, cb in rows]
    ins += [(c, c.shape, lambda i: (0, 0)) for c in consts]
    outs = [((s, w), d, (tm, w), lambda i: (i, 0)) for w, d in row_outs]
    outs += [(shp, F32, shp, lambda i: (0, 0)) for shp in acc_outs]
    res = _call(body, grid=(s // tm,), ins=ins, outs=outs, name=name)
    return res[:no], res[no:]


def _ln(z, g, b):
    mu = jnp.mean(z, -1, keepdims=True)
    var = jnp.mean(jnp.square(z - mu), -1, keepdims=True)
    return (z - mu) * lax.rsqrt(var + LN_EPS) * g + b


def _combine(alpha, x, y, gt, g, b, sc, sh):
    xn = _ln(alpha * x + (1.0 + gt) * y, g, b)
    return xn, xn * (1.0 + sc) + sh


def _modulate_fwd(x, sc, sh):
    def fn(x, sc, sh):
        return ((x * (1.0 + sc) + sh),), ()

    (h,), _ = _rowmap(fn, [x], [sc, sh], [(x.shape[1], MXU_DTYPE)], [], "modulate_fwd")
    return h


def _modulate_bwd(x, dx, dh, sc, sh):
    d = x.shape[1]

    def fn(x, dx, dh, sc, sh):
        _, vjp = jax.vjp(lambda x, sc, sh: x * (1.0 + sc) + sh, x, sc, sh)
        gx, gsc, gsh = vjp(dh)
        return (dx + gx,), (gsc, gsh)

    (gx,), (gsc, gsh) = _rowmap(fn, [x, dx, dh], [sc, sh], [(d, F32)], [(1, d), (1, d)], "modulate_bwd")
    return gx, gsc, gsh


def _combine_fwd(alpha, x, y, gt, g, b, sc, sh):
    d = x.shape[1]

    def fn(x, y, gt, g, b, sc, sh):
        return _combine(alpha, x, y, gt, g, b, sc, sh), ()

    (xn, h), _ = _rowmap(fn, [x, y], [gt, g, b, sc, sh], [(d, F32), (d, MXU_DTYPE)], [], "combine_fwd")
    return xn, h


def _combine_bwd(alpha, x, y, dxn, dh, gt, g, b, sc, sh):
    d = x.shape[1]

    def fn(x, y, dxn, dh, gt, g, b, sc, sh):
        _, vjp = jax.vjp(functools.partial(_combine, alpha), x, y, gt, g, b, sc, sh)
        gx, gy, ggt, gg, gb, gsc, gsh = vjp((dxn, dh))
        return (gx, gy), (ggt, gg, gb, gsc, gsh)

    (gx, gy), accs = _rowmap(fn, [x, y, dxn, dh], [gt, g, b, sc, sh], [(d, F32), (d, MXU_DTYPE)],
                             [(1, d)] * 5, "combine_bwd")
    return gx, gy, accs


def _last_fwd_bwd(alpha, x, y, tgt, gt, g, b):
    d = x.shape[1]

    def fn(x, y, tgt, gt, g, b):
        xn, vjp = jax.vjp(lambda x, y, gt, g, b: _ln(alpha * x + (1.0 + gt) * y, g, b), x, y, gt, g, b)
        err = xn - tgt
        gx, gy, ggt, gg, gb = vjp(err * (1.0 / d))
        rows = jnp.sum(jnp.square(err), axis=-1, keepdims=True)
        loss = (0.5 / d) * jnp.sum(rows, axis=0, keepdims=True) * jnp.ones((1, LANES), F32)
        return (gx, gy), (loss, ggt, gg, gb)

    (gx, gy), accs = _rowmap(fn, [x, y, tgt], [gt, g, b], [(d, F32), (d, MXU_DTYPE)],
                             [(1, LANES), (1, d), (1, d), (1, d)], "last_fwd_bwd")
    return gx, gy, accs


MM_VMEM_BUDGET = 40 * 2 ** 20


def _fit(options, cost):
    for o in options:
        if 2 * cost(o) <= MM_VMEM_BUDGET:
            return o
    return options[-1]


def _row_tiles(m):
    return [t for t in (2048, 1024, 512, 256) if t <= m and m % t == 0] or [m]


def _mm_call(a, a_blk, a_map, b, b_blk, b_map, outs, dims, grid, name, epi=None, extra=None, split=None):
    nk = grid[2]
    n_out = len(outs)
    n_in = 3 if extra is not None else 2

    def body(*refs):
        a_ref, b_ref = refs[0], refs[1]
        rest = refs[n_in:]
        out_refs = rest[:n_out]

        def finish(val):
            if epi == "relu2":
                out_refs[0][...] = val.astype(out_refs[0].dtype)
                out_refs[1][...] = jnp.square(jnp.maximum(val, 0.0)).astype(out_refs[1].dtype)
            elif epi == "relu2_bwd":
                out_refs[0][...] = (val * 2.0 * jnp.maximum(refs[2][...], 0.0)).astype(out_refs[0].dtype)
            elif split is not None:
                for g in range(split[0]):
                    out_refs[0][g] = val[:, g * split[1]:(g + 1) * split[1]].astype(out_refs[0].dtype)
            else:
                out_refs[0][...] = val.astype(out_refs[0].dtype)

        p = lax.dot_general(a_ref[...], b_ref[...], (dims, ((), ())), preferred_element_type=F32)
        if nk == 1:
            finish(p)
        else:
            acc = rest[n_out]
            k = pl.program_id(2)

            @pl.when(k == 0)
            def _():
                acc[...] = p

            @pl.when(k > 0)
            def _():
                acc[...] += p

            @pl.when(k == nk - 1)
            def _():
                finish(acc[...])

    if nk > 1:
        out_blk = tuple(x for x in outs[0][2] if x is not None)
        if split is not None:
            out_blk = (out_blk[1], split[0] * split[1])
        scratch = [pltpu.VMEM(out_blk, F32)]
    else:
        scratch = []
    ins = [(a, a_blk, a_map), (b, b_blk, b_map)] + ([extra] if extra is not None else [])
    return _call(body, grid=grid, ins=ins, outs=outs, name=name, scratch=scratch)


def _isz(dt):
    return jnp.dtype(dt).itemsize


def _mm_nn(a, b, out_dtype, name, relu2=False):
    m, kdim = a.shape
    if b.ndim == 2:
        n = b.shape[1]
        tn = _tile(n, 512)
        b_blk, b_map = (kdim, tn), lambda i, j, k: (0, j)
    else:
        g, _, ng = b.shape
        n = g * ng
        tn = _tile(ng, 512)
        b_blk = (None, kdim, tn)
        b_map = functools.partial(lambda i, j, k, npg: (j // npg, 0, j % npg), npg=ng // tn)
    out_bytes = (4 + _isz(out_dtype)) if relu2 else _isz(out_dtype)
    tm = _fit(_row_tiles(m), lambda t: t * kdim * _isz(a.dtype) + kdim * tn * _isz(b.dtype) + t * tn * out_bytes)
    grid = (m // tm, n // tn, 1)
    outs = [((m, n), F32 if relu2 else out_dtype, (tm, tn), lambda i, j, k: (i, j))]
    if relu2:
        outs.append(((m, n), out_dtype, (tm, tn), lambda i, j, k: (i, j)))
    res = _mm_call(a, (tm, kdim), lambda i, j, k: (i, 0), b, b_blk, b_map, outs, NN, grid, name,
                   epi="relu2" if relu2 else None)
    return res if relu2 else res[0]


def _mm_nt(a, b, out_dtype, name, relu2_of=None):
    m, n = a.shape
    extra_bytes = 4 if relu2_of is not None else 0
    if b.ndim == 2:
        kout = b.shape[0]
        to, tc, nk = _tile(kout, 512), n, 1
        b_blk, b_map = (to, tc), lambda i, j, k: (j, 0)
        acc_bytes = 0
    else:
        nk, kout, tc = b.shape
        to = _tile(kout, 1024)
        b_blk, b_map = (None, to, tc), lambda i, j, k: (k, j, 0)
        acc_bytes = 2
    tm = _fit(_row_tiles(m), lambda t: t * tc * _isz(a.dtype) + to * tc * _isz(b.dtype)
              + t * to * (_isz(out_dtype) + extra_bytes + acc_bytes))
    grid = (m // tm, kout // to, nk)
    outs = [((m, kout), out_dtype, (tm, to), lambda i, j, k: (i, j))]
    extra = (relu2_of, (tm, to), lambda i, j, k: (i, j)) if relu2_of is not None else None
    return _mm_call(a, (tm, tc), lambda i, j, k: (i, k), b, b_blk, b_map, outs, NT, grid, name,
                    epi="relu2_bwd" if relu2_of is not None else None, extra=extra)[0]


def _mm_tn(a, b, out_dtype, name, split_cols=False):
    m, kdim = a.shape
    n = b.shape[1]
    tk = _tile(kdim, 512)
    tn = _tile(n, 1024)
    if not split_cols:
        out, split = ((kdim, n), out_dtype, (tk, tn), lambda i, j, k: (i, j)), None
    else:
        ng = n // N_DEV
        if tn % ng:
            tn = _tile(ng, 512)
        if tn >= ng:
            gb = tn // ng
            out = ((N_DEV, kdim, ng), out_dtype, (gb, tk, ng), lambda i, j, k: (j, i, 0))
            split = (gb, ng)
        else:
            out = ((N_DEV, kdim, ng), out_dtype, (None, tk, tn),
                   functools.partial(lambda i, j, k, npg: (j // npg, i, j % npg), npg=ng // tn))
            split = None
    grid = (kdim // tk, n // tn, 1)
    return _mm_call(a, (m, tk), lambda i, j, k: (0, i), b, (m, tn), lambda i, j, k: (0, j), [out], TN, grid, name,
                    split=split)[0]


def _shifted(xa, off, rows):
    if off % 8 == 0:
        return xa[off:off + rows]
    return pltpu.roll(xa, xa.shape[0] - off, 0)[:rows]


def _conv_pad(taps):
    return -(-(taps - 1) // 8) * 8


def _conv_tile(xp_ref, w, i, rows, taps):
    pad = _conv_pad(taps)
    r0 = pl.multiple_of(i * rows, rows)
    xa = xp_ref[pl.ds(r0, rows + pad), :]
    views = [_shifted(xa, pad - (taps - 1) + j, rows) for j in range(taps)]
    acc = w[0:1, :] * views[0]
    for j in range(1, taps):
        acc = acc + w[j:j + 1, :] * views[j]
    return r0, acc, views


def _conv_back_tile(yp_ref, w, i, rows, taps):
    pad = _conv_pad(taps)
    r0 = pl.multiple_of(i * rows, rows)
    ya = yp_ref[pl.ds(r0, rows + pad), :]
    acc = w[taps - 1:taps, :] * ya[:rows]
    for j in range(taps - 1):
        acc = acc + w[j:j + 1, :] * _shifted(ya, taps - 1 - j, rows)
    return r0, acc


def _tap_sums(dy, views, taps):
    row = lax.broadcasted_iota(jnp.int32, (taps, LANES), 0)
    acc = jnp.zeros((taps, LANES), F32)
    for j in range(taps):
        acc = acc + jnp.where(row == j, jnp.sum(dy * views[j], axis=0, keepdims=True), 0.0)
    return acc


def _silu_l2(xc, l2):
    a = jax.nn.silu(xc)
    if l2:
        a = a * lax.rsqrt(jnp.sum(a * a, axis=-1, keepdims=True) + L2_EPS)
    return a


def _dn_conv_fwd(proj, conv_w, c0, nblk, l2, name):
    s = proj.shape[0]
    pad = _conv_pad(DN_CONV)
    rows = min(CONV_TILE, s)

    def body(x_ref, w_ref, o_ref, xp):
        xp[0:pad, :] = jnp.zeros((pad, LANES), F32)
        xp[pad:, :] = x_ref[...]
        w = w_ref[...]

        def tile(i, c):
            r0, acc, _ = _conv_tile(xp, w, i, rows, DN_CONV)
            o_ref[pl.ds(r0, rows), :] = _silu_l2(acc, l2)
            return c

        lax.fori_loop(0, s // rows, tile, 0)

    return _call(body, grid=(nblk,),
                 ins=[(proj, (s, LANES), lambda c: (0, c0 + c)), (conv_w, (DN_CONV, LANES), lambda c: (0, c0 + c))],
                 outs=[((nblk, s, LANES), F32, (None, s, LANES), lambda c: (c, 0, 0))],
                 name=name, scratch=[pltpu.VMEM((s + pad, LANES), F32)])[0]


def _dn_conv_bwd(proj, conv_w, da, c0, nblk, l2, name):
    s = proj.shape[0]
    pad = _conv_pad(DN_CONV)
    rows = min(CONV_TILE, s)

    def body(x_ref, w_ref, da_ref, dx_ref, dw_ref, xp, yp):
        xp[0:pad, :] = jnp.zeros((pad, LANES), F32)
        xp[pad:, :] = x_ref[...]
        yp[s:, :] = jnp.zeros((pad, LANES), F32)
        w = w_ref[...]

        def tile(i, dw):
            r0, acc, views = _conv_tile(xp, w, i, rows, DN_CONV)
            _, vjp = jax.vjp(functools.partial(_silu_l2, l2=l2), acc)
            (dxc,) = vjp(da_ref[pl.ds(r0, rows), :])
            yp[pl.ds(r0, rows), :] = dxc
            return dw + _tap_sums(dxc, views, DN_CONV)

        dw_ref[...] = lax.fori_loop(0, s // rows, tile, jnp.zeros((DN_CONV, LANES), F32))

        def tile2(i, c):
            r0, acc = _conv_back_tile(yp, w, i, rows, DN_CONV)
            dx_ref[pl.ds(r0, rows), :] = acc.astype(dx_ref.dtype)
            return c

        lax.fori_loop(0, s // rows, tile2, 0)

    return _call(body, grid=(nblk,),
                 ins=[(proj, (s, LANES), lambda c: (0, c0 + c)), (conv_w, (DN_CONV, LANES), lambda c: (0, c0 + c)),
                      (da, (None, s, LANES), lambda c: (c, 0, 0))],
                 outs=[((s, nblk * LANES), MXU_DTYPE, (s, LANES), lambda c: (0, c)),
                       ((DN_CONV, nblk * LANES), F32, (DN_CONV, LANES), lambda c: (0, c))],
                 name=name, scratch=[pltpu.VMEM((s + pad, LANES), F32), pltpu.VMEM((s + pad, LANES), F32)])


def _cf_conv_fwd(vg, dw_w, dw_b):
    s, c2 = vg.shape
    ch = c2 // 2
    nblk = ch // LANES
    taps = dw_w.shape[0]
    pad = _conv_pad(taps)
    rows = min(CONV_TILE, s)

    def body(v_ref, g_ref, w_ref, b_ref, o_ref, xp):
        xp[0:pad, :] = jnp.zeros((pad, LANES), F32)
        xp[pad:, :] = v_ref[...] * jax.nn.sigmoid(g_ref[...])
        w = w_ref[...]
        bias = b_ref[...]

        def tile(i, c):
            r0, acc, _ = _conv_tile(xp, w, i, rows, taps)
            o_ref[pl.ds(r0, rows), :] = acc + bias
            return c

        lax.fori_loop(0, s // rows, tile, 0)

    return _call(body, grid=(nblk,),
                 ins=[(vg, (s, LANES), lambda c: (0, c)), (vg, (s, LANES), lambda c: (0, nblk + c)),
                      (dw_w, (taps, LANES), lambda c: (0, c)), (dw_b, (1, LANES), lambda c: (0, c))],
                 outs=[((s, ch), F32, (s, LANES), lambda c: (0, c))],
                 name="cf_conv_fwd", scratch=[pltpu.VMEM((s + pad, LANES), F32)])[0]


def _cf_conv_bwd(vg, dw_w, du):
    s, c2 = vg.shape
    ch = c2 // 2
    nblk = ch // LANES
    taps = dw_w.shape[0]
    pad = _conv_pad(taps)
    rows = min(CONV_TILE, s)

    def body(v_ref, g_ref, w_ref, du_ref, dv_ref, dg_ref, dw_ref, db_ref, xp, yp):
        sig = jax.nn.sigmoid(g_ref[...])
        xp[0:pad, :] = jnp.zeros((pad, LANES), F32)
        xp[pad:, :] = v_ref[...] * sig
        yp[0:s, :] = du_ref[...]
        yp[s:, :] = jnp.zeros((pad, LANES), F32)
        w = w_ref[...]
        db_ref[...] = jnp.sum(du_ref[...], axis=0, keepdims=True)

        def tile(i, dw):
            r0, _, views = _conv_tile(xp, w, i, rows, taps)
            return dw + _tap_sums(du_ref[pl.ds(r0, rows), :], views, taps)

        dw_ref[...] = lax.fori_loop(0, s // rows, tile, jnp.zeros((taps, LANES), F32))

        def tile2(i, c):
            r0, du0 = _conv_back_tile(yp, w, i, rows, taps)
            val = v_ref[pl.ds(r0, rows), :]
            sg = jax.nn.sigmoid(g_ref[pl.ds(r0, rows), :])
            dv_ref[pl.ds(r0, rows), :] = (du0 * sg).astype(dv_ref.dtype)
            dg_ref[pl.ds(r0, rows), :] = (du0 * val * sg * (1.0 - sg)).astype(dg_ref.dtype)
            return c

        lax.fori_loop(0, s // rows, tile2, 0)

    return _call(body, grid=(nblk,),
                 ins=[(vg, (s, LANES), lambda c: (0, c)), (vg, (s, LANES), lambda c: (0, nblk + c)),
                      (dw_w, (taps, LANES), lambda c: (0, c)), (du, (s, LANES), lambda c: (0, c))],
                 outs=[((s, ch), MXU_DTYPE, (s, LANES), lambda c: (0, c)),
                       ((s, ch), MXU_DTYPE, (s, LANES), lambda c: (0, c)),
                       ((taps, ch), F32, (taps, LANES), lambda c: (0, c)),
                       ((1, ch), F32, (1, LANES), lambda c: (0, c))],
                 name="cf_conv_bwd", scratch=[pltpu.VMEM((s + pad, LANES), F32), pltpu.VMEM((s + pad, LANES), F32)])


def _masks():
    r = lax.broadcasted_iota(jnp.int32, (CHUNK, CHUNK), 0)
    c = lax.broadcasted_iota(jnp.int32, (CHUNK, CHUNK), 1)
    return r >= c, r > c, r <= c


def _chunk_decay(g):
    causal, _, upper = _masks()
    gb = jnp.broadcast_to(g, (CHUNK, CHUNK))
    gam_r = _dot01(jnp.where(causal, 1.0, 0.0), gb)
    gam_s = _dot01(jnp.ones((CHUNK, CHUNK), F32), jnp.where(upper, gb, 0.0))
    dm = jnp.where(causal, jnp.exp(jnp.where(causal, gam_r - gam_s, 0.0)), 0.0)
    return gam_r[:, 0:1], dm


def _chunk_scores(q, k, beta, dm):
    _, strict, _ = _masks()
    both = _mdot(jnp.concatenate([k * beta, q * (HEAD_DIM ** -0.5)], axis=0), k, NT)
    return jnp.where(strict, both[:CHUNK] * dm, 0.0), both[CHUNK:] * dm


def _lockstep(gens):
    results = [None] * len(gens)
    alive = list(range(len(gens)))
    while alive:
        for i in list(alive):
            try:
                next(gens[i])
            except StopIteration as stop:
                results[i] = stop.value
                alive.remove(i)
    return results


def _chunk_prep_bwd(q, k, v, beta, gam, t, du, dw, daqk, dqd, dkd, dgl):
    causal, strict, _ = _masks()
    r = lax.broadcasted_iota(jnp.int32, (CHUNK, CHUNK), 0)
    c = lax.broadcasted_iota(jnp.int32, (CHUNK, CHUNK), 1)
    scale = HEAD_DIM ** -0.5
    eg = jnp.exp(gam)
    gam_last = gam[CHUNK - 1:CHUNK, :]
    rr = jnp.exp(gam_last - gam)
    kb = k * beta
    qs = q * scale
    vb = v * beta
    kbe = kb * eg
    gam_b = jnp.broadcast_to(gam, (CHUNK, CHUNK))
    gam_s = _dot01(jnp.ones((CHUNK, CHUNK), F32), jnp.where(r == c, gam_b, 0.0))
    both = _mdot(jnp.concatenate([kb, qs], axis=0), k, NT)
    duw = jnp.concatenate([du, dw], axis=1)
    dt = _mdot(duw, jnp.concatenate([vb, kbe], axis=1), NT)
    dvk = _mdot(t, duw, TN)
    yield
    dm = jnp.where(causal, jnp.exp(jnp.where(causal, gam_b - gam_s, 0.0)), 0.0)
    a = jnp.where(strict, both[:CHUNK] * dm, 0.0)
    aqk = both[CHUNK:] * dm
    dvb, dkbe = dvk[:, :HEAD_DIM], dvk[:, HEAD_DIM:]
    x = _dot3(t, dt, TN)
    yield
    da = jnp.where(strict, -_dot3(x, t, NT), 0.0)
    yield
    dkk = da * dm
    dqk = daqk * dm
    ddiff = da * a + daqk * aqk
    dboth = jnp.concatenate([dkk, dqk], axis=0)
    dkq = _mdot(dboth, k)
    dk_mm = _mdot(dboth, jnp.concatenate([kb, qs], axis=0), TN)
    colsum = _dot01(ddiff, jnp.ones((CHUNK, LANES), F32), TN, mask_first=False)[:, 0:1]
    yield
    dkb = dkq[:CHUNK] + dkbe * eg
    dk = dk_mm + dkb * beta + dkd * rr
    dq = (dkq[CHUNK:] + dqd * eg) * scale
    dbeta = jnp.sum(dkb * k, axis=-1, keepdims=True) + jnp.sum(dvb * v, axis=-1, keepdims=True)
    dv = dvb * beta
    deg = jnp.sum(dkbe * kb, axis=-1, keepdims=True) + jnp.sum(dqd * qs, axis=-1, keepdims=True)
    drr = jnp.sum(dkd * k, axis=-1, keepdims=True)
    dgam = deg * eg - drr * rr + jnp.sum(ddiff, axis=-1, keepdims=True) - colsum
    dgam_last = jnp.sum(drr * rr, axis=0, keepdims=True) + dgl[0:1, :] * jnp.exp(gam_last)
    row = lax.broadcasted_iota(jnp.int32, (CHUNK, 1), 0)
    dgam = dgam + jnp.where(row == CHUNK - 1, dgam_last, 0.0)
    dg = _dot01(jnp.where(causal, 1.0, 0.0), jnp.broadcast_to(dgam, (CHUNK, LANES)), TN)[:, 0:1]
    return dq, dk, dv, dbeta, dg


def _prep_group(s):
    nch = s // CHUNK
    return 8 if nch % 8 == 0 else (4 if nch % 4 == 0 else 1)


def _tri_solve_lanes(a_l):
    n = a_l.shape[1]
    group = 8

    def body(a_ref, t_ref):
        t_ref[...] = jnp.zeros_like(t_ref)
        col = lax.broadcasted_iota(jnp.int32, (CHUNK, n), 0)

        def row(r, carry):
            r0 = pl.multiple_of(r * CHUNK, CHUNK)

            def inner(sg, acc):
                a8 = a_ref[pl.ds(r0 + pl.multiple_of(sg * group, group), group), :]
                for j in range(group):
                    t0 = pl.multiple_of((sg * group + j) * CHUNK, CHUNK)
                    acc = acc + a8[j:j + 1, :] * t_ref[pl.ds(t0, CHUNK), :]
                return acc

            acc = lax.fori_loop(0, (r + group - 1) // group, inner, jnp.zeros((CHUNK, n), F32))
            t_ref[pl.ds(r0, CHUNK), :] = jnp.where(col == r, 1.0, 0.0) - acc
            return carry

        lax.fori_loop(0, CHUNK, row, 0)

    return pl.pallas_call(body, out_shape=jax.ShapeDtypeStruct(a_l.shape, F32), name="dn_tri_solve")(a_l)


def _dn_prep(q, k, v, beta, g):
    h, s, _ = q.shape
    cb = _prep_group(s)
    rb = cb * CHUNK
    big = lambda x: (x, (None, rb, HEAD_DIM), lambda hh, n: (hh, n, 0))
    sq = lambda x: (x, (None, rb, CHUNK), lambda hh, n: (hh, n, 0))
    col = lambda x: (x, (None, rb, 1), lambda hh, n: (hh, n, 0))
    o_big = ((h, s, HEAD_DIM), F32, (None, rb, HEAD_DIM), lambda hh, n: (hh, n, 0))
    o_sq = ((h, s, CHUNK), F32, (None, rb, CHUNK), lambda hh, n: (hh, n, 0))
    o_col = ((h, s, 1), F32, (None, rb, 1), lambda hh, n: (hh, n, 0))

    def scores(q_ref, k_ref, b_ref, g_ref, a_ref, aqk_ref, gam_ref):
        for i in range(cb):
            sl = slice(i * CHUNK, (i + 1) * CHUNK)
            gam, dm = _chunk_decay(g_ref[sl, :])
            a_ref[sl, :], aqk_ref[sl, :] = _chunk_scores(q_ref[sl, :], k_ref[sl, :], b_ref[sl, :], dm)
            gam_ref[sl, :] = gam

    a, aqk, gam = _call(scores, grid=(h, s // rb), ins=[big(q), big(k), col(beta), col(g)],
                        outs=[o_sq, o_sq, o_col], name="dn_scores")
    n_prob = h * (s // CHUNK)
    t_l = _tri_solve_lanes(jnp.transpose(a.reshape(n_prob, CHUNK * CHUNK)))
    t = jnp.transpose(t_l).reshape(h, s, CHUNK)

    def wy(k_ref, v_ref, b_ref, gam_ref, t_ref, u_ref, w_ref):
        for i in range(cb):
            sl = slice(i * CHUNK, (i + 1) * CHUNK)
            kb = k_ref[sl, :] * b_ref[sl, :]
            rhs = jnp.concatenate([v_ref[sl, :] * b_ref[sl, :], kb * jnp.exp(gam_ref[sl, :])], axis=1)
            uw = _mdot(t_ref[sl, :], rhs)
            u_ref[sl, :] = uw[:, :HEAD_DIM]
            w_ref[sl, :] = uw[:, HEAD_DIM:]

    u, w = _call(wy, grid=(h, s // rb), ins=[big(k), big(v), col(beta), col(gam), sq(t)], outs=[o_big, o_big],
                 name="dn_wy")
    return u, w, aqk, t, gam


def _dn_prep_bwd(q, k, v, beta, gam, t, du, dw, daqk, dqd, dkd, dgl):
    h, s, _ = q.shape
    cb = _prep_group(s)
    rb = cb * CHUNK

    def body(q_ref, k_ref, v_ref, b_ref, g_ref, t_ref, du_ref, dw_ref, da_ref, dqd_ref, dkd_ref, dgl_ref,
             dq_ref, dk_ref, dv_ref, db_ref, dg_ref):
        slices = [slice(i * CHUNK, (i + 1) * CHUNK) for i in range(cb)]
        results = _lockstep([_chunk_prep_bwd(
            q_ref[sl, :], k_ref[sl, :], v_ref[sl, :], b_ref[sl, :], g_ref[sl, :], t_ref[sl, :],
            du_ref[sl, :], dw_ref[sl, :], da_ref[sl, :], dqd_ref[sl, :], dkd_ref[sl, :], dgl_ref[sl, :])
            for sl in slices])
        for sl, (dq, dk, dv, dbeta, dg) in zip(slices, results):
            dq_ref[sl, :] = dq
            dk_ref[sl, :] = dk
            dv_ref[sl, :] = dv
            db_ref[sl, :] = dbeta
            dg_ref[sl, :] = dg

    big = lambda x: (x, (None, rb, HEAD_DIM), lambda hh, n: (hh, n, 0))
    sq = lambda x: (x, (None, rb, CHUNK), lambda hh, n: (hh, n, 0))
    col = lambda x: (x, (None, rb, 1), lambda hh, n: (hh, n, 0))
    o_big = ((h, s, HEAD_DIM), F32, (None, rb, HEAD_DIM), lambda hh, n: (hh, n, 0))
    o_col = ((h, s, 1), F32, (None, rb, 1), lambda hh, n: (hh, n, 0))
    return _call(body, grid=(h, s // rb),
                 ins=[big(q), big(k), big(v), col(beta), col(gam), sq(t), big(du), big(dw), sq(daqk), big(dqd), big(dkd),
                      col(dgl)],
                 outs=[o_big, o_big, o_big, o_col, o_col], name="dn_prep_bwd")


def _chunk_scaled(q, k, gam):
    gam_last = gam[CHUNK - 1:CHUNK, :]
    q_dec = q * (HEAD_DIM ** -0.5) * jnp.exp(gam)
    k_dec = k * jnp.exp(gam_last - gam)
    return q_dec, k_dec, jnp.exp(gam_last)


def _dn_scan(q, k, u, w, aqk, gam):
    h, s, _ = q.shape
    nch = s // CHUNK

    def body(q_ref, k_ref, u_ref, w_ref, a_ref, gam_ref, o_ref, st_ref, state):
        @pl.when(pl.program_id(0) == 0)
        def _():
            state[...] = jnp.zeros_like(state)

        def head(hh):
            s0 = state[hh]
            st_ref[hh] = s0
            q_dec, k_dec, gl = _chunk_scaled(q_ref[hh], k_ref[hh], gam_ref[hh])
            both = _mdot(jnp.concatenate([w_ref[hh], q_dec], axis=0), s0)
            yield
            v_new = u_ref[hh] - both[:CHUNK]
            o_ref[:, hh * HEAD_DIM:(hh + 1) * HEAD_DIM] = both[CHUNK:] + _mdot(a_ref[hh], v_new)
            state[hh] = s0 * gl + _mdot(k_dec, v_new, TN)

        _lockstep([head(hh) for hh in range(h)])

    big = lambda x: (x, (h, CHUNK, HEAD_DIM), lambda n: (0, n, 0))
    return _call(body, grid=(nch,),
                 ins=[big(q), big(k), big(u), big(w), (aqk, (h, CHUNK, CHUNK), lambda n: (0, n, 0)),
                      (gam, (h, CHUNK, 1), lambda n: (0, n, 0))],
                 outs=[((s, h * HEAD_DIM), F32, (CHUNK, h * HEAD_DIM), lambda n: (n, 0)),
                       ((nch, h, HEAD_DIM, HEAD_DIM), F32, (None, h, HEAD_DIM, HEAD_DIM), lambda n: (n, 0, 0, 0))],
                 name="dn_scan", scratch=[pltpu.VMEM((h, HEAD_DIM, HEAD_DIM), F32)])


def _dn_scan_bwd(q, k, u, w, aqk, gam, states, do):
    h, s, _ = q.shape
    nch = s // CHUNK

    def body(q_ref, k_ref, u_ref, w_ref, a_ref, gam_ref, st_ref, do_ref,
             du_ref, dw_ref, da_ref, dqd_ref, dkd_ref, dgl_ref, dstate):
        @pl.when(pl.program_id(0) == 0)
        def _():
            dstate[...] = jnp.zeros_like(dstate)

        def head(hh):
            s0 = st_ref[hh]
            ds = dstate[hh]
            doh = do_ref[:, hh * HEAD_DIM:(hh + 1) * HEAD_DIM]
            wv = w_ref[hh]
            q_dec, k_dec, gl = _chunk_scaled(q_ref[hh], k_ref[hh], gam_ref[hh])
            ws = _mdot(wv, s0)
            dv_new = _mdot(a_ref[hh], doh, TN) + _mdot(k_dec, ds)
            dqd_ref[hh] = _mdot(doh, s0, NT)
            qdo = _mdot(q_dec, doh, TN)
            tot = jnp.sum(jnp.sum(s0 * ds, axis=-1, keepdims=True), axis=0, keepdims=True)
            dgl_ref[hh] = jnp.broadcast_to(tot, (CHUNK, 1))
            yield
            v_new = u_ref[hh] - ws
            du_ref[hh] = dv_new
            dw_ref[hh] = -_mdot(dv_new, s0, NT)
            da_ref[hh] = _mdot(doh, v_new, NT)
            dkd_ref[hh] = _mdot(v_new, ds, NT)
            dstate[hh] = ds * gl + qdo - _mdot(wv, dv_new, TN)

        _lockstep([head(hh) for hh in range(h)])

    rev = lambda n: (0, nch - 1 - n, 0)
    big = lambda x: (x, (h, CHUNK, HEAD_DIM), rev)
    o_big = ((h, s, HEAD_DIM), F32, (h, CHUNK, HEAD_DIM), rev)
    return _call(body, grid=(nch,),
                 ins=[big(q), big(k), big(u), big(w), (aqk, (h, CHUNK, CHUNK), rev), (gam, (h, CHUNK, 1), rev),
                      (states, (None, h, HEAD_DIM, HEAD_DIM), lambda n: (nch - 1 - n, 0, 0, 0)),
                      (do, (CHUNK, h * HEAD_DIM), lambda n: (nch - 1 - n, 0))],
                 outs=[o_big, o_big, ((h, s, CHUNK), F32, (h, CHUNK, CHUNK), rev), o_big, o_big,
                       ((h, s, 1), F32, (h, CHUNK, 1), rev)],
                 name="dn_scan_bwd", scratch=[pltpu.VMEM((h, HEAD_DIM, HEAD_DIM), F32)])


def _gates(x, a_log, dt_b, h):
    lane = lax.broadcasted_iota(jnp.int32, x.shape, 1)
    return jnp.where(lane < h, jax.nn.sigmoid(x), -jnp.exp(a_log) * jax.nn.softplus(x + dt_b))


def _head_out(oh, zh, nw):
    on = oh * lax.rsqrt(jnp.mean(oh * oh, axis=-1, keepdims=True) + RMS_EPS) * nw
    return on * jax.nn.silu(zh)


def _to_heads(x, h):
    return jnp.transpose(x[:, :h])[:, :, None]


def _pad_lanes(x, lo):
    return jnp.zeros((1, LANES), F32).at[0, lo:lo + x.shape[0]].set(x)


def _deltanet_fwd(hin, get_w_in, conv_w, a_log, dt_bias, norm_w, get_w_out):
    h = a_log.shape[0]
    hw = h * HEAD_DIM
    w_in = get_w_in(hin)
    proj = _mm_nn(hin, w_in, F32, "dn_proj")
    q = _dn_conv_fwd(proj, conv_w, 0, h, True, "dn_conv_q")
    k = _dn_conv_fwd(proj, conv_w, h, h, True, "dn_conv_k")
    v = _dn_conv_fwd(proj, conv_w, 2 * h, h, False, "dn_conv_v")
    alp, dtp = _pad_lanes(a_log, h), _pad_lanes(dt_bias, h)

    def gates_fn(x, al, db):
        return (_gates(x, al, db, h),), ()

    (bg,), _ = _rowmap(gates_fn, [(proj, LANES, 4 * h)], [alp, dtp], [(LANES, F32)], [], "dn_gates")
    beta, g = _to_heads(bg, h), _to_heads(bg[:, h:], h)
    u, w, aqk, t, gam = _dn_prep(q, k, v, beta, g)
    o, states = _dn_scan(q, k, u, w, aqk, gam)
    nw = norm_w[None, :]

    def out_fn(o, z, nw):
        parts = [_head_out(o[:, i * HEAD_DIM:(i + 1) * HEAD_DIM], z[:, i * HEAD_DIM:(i + 1) * HEAD_DIM], nw)
                 for i in range(h)]
        return (jnp.concatenate(parts, axis=-1),), ()

    (og,), _ = _rowmap(out_fn, [o, (proj, hw, 3)], [nw], [(hw, MXU_DTYPE)], [], "dn_out")
    w_out = get_w_out(og)
    y = _mm_nn(og, w_out, F32, "dn_y")
    return y, (hin, proj, q, k, v, beta, g, u, w, aqk, t, gam, states, o, og, alp, dtp, nw, w_in, w_out)


def _deltanet_bwd(res, dy, conv_w):
    hin, proj, q, k, v, beta, g, u, w, aqk, t, gam, states, o, og, alp, dtp, nw, w_in, w_out = res
    h = q.shape[0]
    hw = h * HEAD_DIM
    s = hin.shape[0]
    d_w_out = _mm_tn(og, dy, MXU_DTYPE, "dn_dwout")
    dog = _mm_nt(dy, w_out, F32, "dn_dog")

    def out_bwd(o, z, dog, nw):
        dos, dzs = [], []
        dn = jnp.zeros((1, HEAD_DIM), F32)
        for i in range(h):
            sl = slice(i * HEAD_DIM, (i + 1) * HEAD_DIM)
            _, vjp = jax.vjp(_head_out, o[:, sl], z[:, sl], nw)
            a, b, c = vjp(dog[:, sl])
            dos.append(a)
            dzs.append(b)
            dn = dn + c
        return (jnp.concatenate(dos, axis=-1), jnp.concatenate(dzs, axis=-1)), (dn,)

    (do, dz), (d_norm_w,) = _rowmap(out_bwd, [o, (proj, hw, 3), dog], [nw], [(hw, F32), (hw, MXU_DTYPE)],
                                    [(1, HEAD_DIM)], "dn_out_bwd")
    du, dw, daqk, dqd, dkd, dgl = _dn_scan_bwd(q, k, u, w, aqk, gam, states, do)
    dq, dk, dv, dbeta, dg = _dn_prep_bwd(q, k, v, beta, gam, t, du, dw, daqk, dqd, dkd, dgl)
    dpq, dwq = _dn_conv_bwd(proj, conv_w, dq, 0, h, True, "dn_conv_q_bwd")
    dpk, dwk = _dn_conv_bwd(proj, conv_w, dk, h, h, True, "dn_conv_k_bwd")
    dpv, dwv = _dn_conv_bwd(proj, conv_w, dv, 2 * h, h, False, "dn_conv_v_bwd")
    dbg = jnp.concatenate([jnp.transpose(dbeta[:, :, 0]), jnp.transpose(dg[:, :, 0]),
                           jnp.zeros((s, LANES - 2 * h), F32)], axis=1)

    def gates_bwd(x, dbg, al, db):
        _, vjp = jax.vjp(functools.partial(_gates, h=h), x, al, db)
        gx, gal, gdb = vjp(dbg)
        return (gx,), (gal, gdb)

    (dba,), (d_alp, d_dtp) = _rowmap(gates_bwd, [(proj, LANES, 4 * h), dbg], [alp, dtp], [(LANES, MXU_DTYPE)],
                                     [(1, LANES), (1, LANES)], "dn_gates_bwd")
    dproj = jnp.concatenate([dpq, dpk, dpv, dz, dba], axis=1)
    d_w_in = _mm_tn(hin, dproj, MXU_DTYPE, "dn_dwin")
    dh = _mm_nt(dproj, w_in, F32, "dn_dh")
    d_conv_w = jnp.concatenate([dwq, dwk, dwv], axis=1)
    return dh, dict(w_in=d_w_in, w_out=d_w_out, conv_w=d_conv_w, a_log=d_alp[0, h:2 * h], dt_bias=d_dtp[0, h:2 * h],
                    norm_w=d_norm_w[0])


def _ln_silu(u, g, b):
    return jax.nn.silu(_ln(u, g, b))


def _conformer_fwd(hin, get_w_in, dw_w, dw_b, ln_g, ln_b, get_w_out):
    w_in = get_w_in(hin)
    vg = _mm_nn(hin, w_in, F32, "cf_vg")
    u1 = _cf_conv_fwd(vg, dw_w, dw_b)
    ch = u1.shape[1]

    def fn(u, g, b):
        return (_ln_silu(u, g, b),), ()

    (u2,), _ = _rowmap(fn, [u1], [ln_g, ln_b], [(ch, MXU_DTYPE)], [], "cf_ln")
    w_out = get_w_out(u2)
    y = _mm_nn(u2, w_out, F32, "cf_y")
    return y, (hin, vg, u1, u2, w_in, w_out)


def _conformer_bwd(res, dy, dw_w, ln_g, ln_b):
    hin, vg, u1, u2, w_in, w_out = res
    ch = u1.shape[1]
    d_w_out = _mm_tn(u2, dy, MXU_DTYPE, "cf_dwout")
    du2 = _mm_nt(dy, w_out, F32, "cf_du2")

    def fn(u, du2, g, b):
        _, vjp = jax.vjp(_ln_silu, u, g, b)
        gu, gg, gb = vjp(du2)
        return (gu,), (gg, gb)

    (du1,), (d_ln_g, d_ln_b) = _rowmap(fn, [u1, du2], [ln_g, ln_b], [(ch, F32)], [(1, ch), (1, ch)], "cf_ln_bwd")
    dval, dgate, d_dw_w, d_dw_b = _cf_conv_bwd(vg, dw_w, du1)
    dvg = jnp.concatenate([dval, dgate], axis=1)
    d_w_in = _mm_tn(hin, dvg, MXU_DTYPE, "cf_dwin", split_cols=True)
    dh = _mm_nt(dvg, w_in, F32, "cf_dh")
    return dh, dict(w_in=d_w_in, w_out=d_w_out, dw_w=d_dw_w, dw_b=d_dw_b[0], ln_g=d_ln_g[0], ln_b=d_ln_b[0])


def _mlp_fwd(hin, get_w1, get_w2):
    w1 = get_w1(hin)
    a, r = _mm_nn(hin, w1, MXU_DTYPE, "ff_a", relu2=True)
    w2 = get_w2(r)
    m = _mm_nn(r, w2, F32, "ff_m")
    return m, (hin, a, r, w1, w2)


def _mlp_bwd(res, dm):
    hin, a, r, w1, w2 = res
    d_w2 = _mm_tn(r, dm, MXU_DTYPE, "ff_dw2")
    da = _mm_nt(dm, w2, MXU_DTYPE, "ff_da", relu2_of=a)
    d_w1 = _mm_tn(hin, da, MXU_DTYPE, "ff_dw1", split_cols=True)
    dh = _mm_nt(da, w1, F32, "ff_dh")
    return dh, d_w1, d_w2


def _ada_fwd(c_all, ada_w):
    depth, d, nl = ada_w.shape
    tn = _tile(nl, 256)

    def body(c_ref, w_ref, o_ref, cond_ref):
        cond = jax.nn.silu(c_ref[...]).astype(MXU_DTYPE)
        cond_ref[...] = cond
        o_ref[...] = lax.dot_general(cond, w_ref[...].astype(MXU_DTYPE), (NN, ((), ())), preferred_element_type=F32)

    return _call(body, grid=(depth, nl // tn),
                 ins=[(c_all, c_all.shape, lambda l, j: (0, 0)), (ada_w, (None, d, tn), lambda l, j: (l, 0, j))],
                 outs=[((depth, N_DEV, nl), F32, (None, N_DEV, tn), lambda l, j: (l, 0, j)),
                       (c_all.shape, MXU_DTYPE, c_all.shape, lambda l, j: (0, 0))],
                 name="ada_fwd")


def _ada_bwd(cond_all, dmod_cols):
    depth, _, nl = dmod_cols.shape
    d = cond_all.shape[1]
    tn = _tile(nl, 256)

    def body(c_ref, g_ref, o_ref):
        o_ref[...] = lax.dot_general(c_ref[...], g_ref[...].astype(MXU_DTYPE), (TN, ((), ())),
                                     preferred_element_type=F32)

    return _call(body, grid=(depth, nl // tn),
                 ins=[(cond_all, cond_all.shape, lambda l, j: (0, 0)), (dmod_cols, (None, N_DEV, tn), lambda l, j: (l, 0, j))],
                 outs=[((depth, d, nl), F32, (None, d, tn), lambda l, j: (l, 0, j))], name="ada_bwd")[0]


def _peers():
    x, y, c = lax.axis_index("x"), lax.axis_index("y"), lax.axis_index("c")
    peers = []
    for k in range(1, N_DEV):
        px = 1 - x if k & 4 else x
        py = 1 - y if k & 2 else y
        pc = 1 - c if k & 1 else c
        peers.append(((px, py, pc), 4 * px + 2 * py + pc))
    return 4 * x + 2 * y + c, peers


_HBM = pl.BlockSpec(memory_space=pltpu.HBM)
_SEM = pl.BlockSpec(memory_space=pltpu.SEMAPHORE)
_ANY = pl.BlockSpec(memory_space=pl.ANY)
_EFFECT = pltpu.SideEffectType.DATAFLOW_SIDE_EFFECTING


def _xfer_start(srcs, lands, scatter, after, name):
    nt = len(srcs)

    def body(*refs):
        src, land = refs[:nt], refs[nt:2 * nt]
        sems = refs[2 * nt + 1:4 * nt + 1]
        token = refs[-1]
        me, peers = _peers()
        for t in range(nt):
            for k, (pid, plin) in enumerate(peers):
                pltpu.make_async_remote_copy(
                    src_ref=src[t].at[plin] if scatter else src[t], dst_ref=land[t].at[me],
                    send_sem=sems[2 * t].at[k], recv_sem=sems[2 * t + 1].at[k],
                    device_id=pid, device_id_type=pl.DeviceIdType.MESH).start()
        token[...] = jnp.zeros_like(token)

    out_shape = [pltpu.SemaphoreType.DMA((N_DEV - 1,)) for _ in range(2 * nt)]
    out_shape += [pltpu.HBM(a.shape, a.dtype) for a in list(srcs) + list(lands)]
    out_shape += [jax.ShapeDtypeStruct((8, LANES), F32)]
    res = pl.pallas_call(
        body, name=name, out_shape=out_shape,
        in_specs=[_HBM] * (2 * nt) + [_ANY],
        out_specs=[_SEM] * (2 * nt) + [_HBM] * (2 * nt) + [pl.BlockSpec(memory_space=pltpu.VMEM)],
        input_output_aliases={i: 2 * nt + i for i in range(2 * nt)},
        compiler_params=pltpu.CompilerParams(has_side_effects=_EFFECT),
    )(*[pltpu.with_memory_space_constraint(a, pltpu.HBM) for a in list(srcs) + list(lands)], after)
    sems, thru = res[:2 * nt], res[2 * nt:4 * nt]
    return [(sems[2 * t], sems[2 * t + 1], thru[t], thru[nt + t]) for t in range(nt)], res[-1]


def _xfer_wait(handle, scatter, after, name):
    send, recv, src, land = handle

    def body(src_ref, land_ref, send_sem, recv_sem, after_ref, src_dead, land_out):
        _, peers = _peers()
        for k, (pid, plin) in enumerate(peers):
            cp = pltpu.make_async_remote_copy(
                src_ref=src_ref.at[plin] if scatter else src_ref, dst_ref=land_ref.at[plin],
                send_sem=send_sem.at[k], recv_sem=recv_sem.at[k],
                device_id=pid, device_id_type=pl.DeviceIdType.MESH)
            cp.wait_send()
            cp.wait_recv()

    return pl.pallas_call(
        body, name=name, out_shape=(pltpu.HBM(src.shape, src.dtype), pltpu.HBM(land.shape, land.dtype)),
        in_specs=(_HBM, _HBM, _SEM, _SEM, _ANY), out_specs=(_HBM, _HBM), input_output_aliases={0: 0, 1: 1},
        compiler_params=pltpu.CompilerParams(has_side_effects=_EFFECT),
    )(src, land, send, recv, after)[1]


def _landing(x, me):
    return lax.dynamic_update_slice(lax.empty((N_DEV,) + x.shape, x.dtype), x[None], (me,) + (0,) * x.ndim)


def _landing_scatter(p, me):
    own = lax.dynamic_index_in_dim(p, me, axis=0, keepdims=True)
    return lax.dynamic_update_slice(lax.empty(p.shape, p.dtype), own, (me,) + (0,) * (p.ndim - 1))


def _exchange(arrs, scatter, name):
    nt = len(arrs)
    out_shape = [jax.ShapeDtypeStruct(a.shape if scatter else (N_DEV,) + a.shape, a.dtype) for a in arrs]

    def body(*refs):
        ins, outs = refs[:nt], refs[nt:2 * nt]
        send, recv, loc = refs[2 * nt:]
        me, peers = _peers()
        copies = []
        for t in range(nt):
            own = pltpu.make_async_copy(ins[t].at[me] if scatter else ins[t], outs[t].at[me], loc.at[t])
            own.start()
            copies.append(own)
            for k, (pid, plin) in enumerate(peers):
                cp = pltpu.make_async_remote_copy(
                    src_ref=ins[t].at[plin] if scatter else ins[t], dst_ref=outs[t].at[me],
                    send_sem=send.at[t, k], recv_sem=recv.at[t, k],
                    device_id=pid, device_id_type=pl.DeviceIdType.MESH)
                cp.start()
                copies.append(cp)
        for cp in copies:
            cp.wait()

    any_spec = pl.BlockSpec(memory_space=pl.ANY)
    return pl.pallas_call(
        body, out_shape=out_shape, in_specs=[any_spec] * nt, out_specs=[any_spec] * nt,
        scratch_shapes=[pltpu.SemaphoreType.DMA((nt, N_DEV - 1)), pltpu.SemaphoreType.DMA((nt, N_DEV - 1)),
                        pltpu.SemaphoreType.DMA((nt,))],
        name=name)(*arrs)


def _adamw_body(n_parts):
    def body(p_ref, w_ref, m_ref, v_ref, *rest):
        g_out, d_out, m_out, v_out = rest[-4:]
        g = p_ref[0].astype(F32)
        for i in range(1, n_parts):
            g = g + p_ref[i].astype(F32)
        m2 = ADAM_B1 * m_ref[...] + (1.0 - ADAM_B1) * g
        v2 = ADAM_B2 * v_ref[...] + (1.0 - ADAM_B2) * jnp.square(g)
        m_hat = m2 / (1.0 - ADAM_B1 ** ADAM_STEP)
        v_hat = v2 / (1.0 - ADAM_B2 ** ADAM_STEP)
        g_out[...] = g
        d_out[...] = -ADAM_LR * (m_hat / (jnp.sqrt(v_hat) + ADAM_EPS) + ADAM_WD * w_ref[...])
        m_out[...] = m2
        v_out[...] = v2

    return body


def _adamw_layer(parts, w, m, v, layer, prev, name):
    p, r, c = parts.shape
    tr = _tile(r, 256, 8)
    blk = pl.BlockSpec((None, tr, c), lambda i: (layer, i, 0))
    in_specs = [pl.BlockSpec((p, tr, c), lambda i: (0, i, 0)), blk, blk, blk]
    args = [parts, w, m, v]
    aliases = {}
    if prev is not None:
        in_specs += [_ANY] * 4
        args += list(prev)
        aliases = {4 + i: i for i in range(4)}
    return pl.pallas_call(
        _adamw_body(p), grid=(r // tr,), in_specs=in_specs, out_specs=[blk] * 4,
        out_shape=[jax.ShapeDtypeStruct(w.shape, F32)] * 4, input_output_aliases=aliases, name=name,
        compiler_params=_cparams(1))(*args)


def _adamw(parts, w, m, v, name):
    p, nl, r, c = parts.shape
    tr = _tile(r, 256, 8)
    body = _adamw_body(p)

    blk = (None, tr, c)
    imap = lambda l, i: (l, i, 0)
    out = ((nl, r, c), F32, blk, imap)
    return _call(body, grid=(nl, r // tr),
                 ins=[(parts, (p, None, tr, c), lambda l, i: (0, l, i, 0)), (w, blk, imap), (m, blk, imap), (v, blk, imap)],
                 outs=[out] * 4, name=name)


def _rows(x):
    return x.reshape(-1, LANES)


def _pad_rows(x, mult=8):
    r = x.shape[0]
    extra = (-r) % mult
    return jnp.pad(x, ((0, extra), (0, 0))) if extra else x


def _shard_cols(x, me, groups):
    lead = x.shape[:-1]
    xr = x.reshape(lead + (N_DEV, groups * LANES))
    xs = lax.dynamic_index_in_dim(xr, me, axis=len(lead), keepdims=False)
    return xs.reshape(N_DEV, -1, LANES)


def kernel(x, c, ada_w, ada_b, ln_g, ln_b, dn_w_in, dn_conv_w, dn_a_log, dn_dt_bias, dn_norm_w, dn_w_out, cf_w_in, cf_dw_w, cf_dw_b, cf_ln_g, cf_ln_b, cf_w_out, ff_w1, ff_w2, loss_target, m_ada_w, m_ada_b, m_ln_g, m_ln_b, m_dn_w_in, m_dn_conv_w, m_dn_a_log, m_dn_dt_bias, m_dn_norm_w, m_dn_w_out, m_cf_w_in, m_cf_dw_w, m_cf_dw_b, m_cf_ln_g, m_cf_ln_b, m_cf_w_out, m_ff_w1, m_ff_w2, v_ada_w, v_ada_b, v_ln_g, v_ln_b, v_dn_w_in, v_dn_conv_w, v_dn_a_log, v_dn_dt_bias, v_dn_norm_w, v_dn_w_out, v_cf_w_in, v_cf_dw_w, v_cf_dw_b, v_cf_ln_g, v_cf_ln_b, v_cf_w_out, v_ff_w1, v_ff_w2):
    depth, d, _ = ada_w.shape
    n_a, n_b = dn_w_in.shape[0], cf_w_in.shape[0]
    heads = dn_a_log.shape[1]
    hw = heads * HEAD_DIM
    taps = cf_dw_w.shape[1]
    s = x.shape[1]
    alpha = (2.0 * depth) ** 0.25
    me = 4 * lax.axis_index("x") + 2 * lax.axis_index("y") + lax.axis_index("c")
    xs, tgt = x[0], loss_target[0]

    small_local = [_rows(ln_g), _rows(ln_b), _rows(dn_conv_w), _rows(cf_dw_w), _rows(cf_dw_b), _rows(cf_ln_g),
                   _rows(cf_ln_b), _rows(c)]
    sizes = [a.shape[0] for a in small_local]
    packed = _pad_rows(jnp.concatenate(small_local, axis=0))
    (small_all,) = _exchange([packed], False, "comm_gather_params")
    offs = [0]
    for z in sizes:
        offs.append(offs[-1] + z)

    def small(i):
        return small_all[:, offs[i]:offs[i + 1], :]

    def unshard(piece, lead, groups):
        t = piece.reshape((N_DEV,) + lead + (groups * LANES,))
        t = jnp.moveaxis(t, 0, len(lead))
        return t.reshape(lead + (N_DEV * groups * LANES,))

    ln_g_f = unshard(small(0), (depth, 2), 1)
    ln_b_f = unshard(small(1), (depth, 2), 1)
    conv_w_f = unshard(small(2), (n_a, DN_CONV), 3 * heads // N_DEV)
    dw_w_f = unshard(small(3), (n_b, taps), 1)
    dw_b_f = unshard(small(4), (n_b,), 1)
    cf_ln_g_f = unshard(small(5), (n_b,), 1)
    cf_ln_b_f = unshard(small(6), (n_b,), 1)
    c_all = small(7).reshape(N_DEV, d)

    mod_part, cond_all = _ada_fwd(c_all, ada_w)
    (mod_all,) = _exchange([mod_part], False, "comm_gather_mod")
    mod_mine = lax.dynamic_index_in_dim(mod_all, me, axis=2, keepdims=False)
    mod_mine = jnp.moveaxis(mod_mine, 0, 1).reshape(depth, N_MOD * d)

    dn_in_cols = dn_w_in.shape[2]
    keys, shards = [], []
    for i in range(depth):
        j = i // 2
        mixer = [("dn_in", dn_w_in), ("dn_out", dn_w_out)] if i % 2 == 0 else [("cf_in", cf_w_in), ("cf_out", cf_w_out)]
        for nm, wt in mixer:
            keys.append((nm, j))
            shards.append(wt[j].astype(MXU_DTYPE))
        keys += [("ff1", i), ("ff2", i)]
        shards += [ff_w1[i].astype(MXU_DTYPE), ff_w2[i].astype(MXU_DTYPE)]
    handles, token = _xfer_start(shards, [_landing(a, me) for a in shards], False, mod_all, "gather_weights_start")
    handles = dict(zip(keys, handles))
    weights = {}

    def gathered(key, after):
        if key not in weights:
            weights[key] = _xfer_wait(handles[key], False, after, "gather_wait_%s_%d" % key)
        return weights[key]

    def get_dn_in(j):
        def get(after):
            g = gathered(("dn_in", j), after)
            w = jnp.moveaxis(g, 0, 1).reshape(d, N_DEV * dn_in_cols)
            return jnp.pad(w, ((0, 0), (0, 4 * hw + LANES - N_DEV * dn_in_cols)))
        return get

    def get_rows(key):
        return lambda after: gathered(key, after).reshape((-1, d))

    def get_cols(key):
        return lambda after: gathered(key, after)

    def add_bias(a, b):
        return (a + b,), ()

    (mod,), _ = _rowmap(add_bias, [mod_mine + token[0, 0], ada_b], [], [(N_MOD * d, F32)], [], "ada_bias")

    def mod_row(i, j):
        return mod[i:i + 1, j * d:(j + 1) * d]

    def ln_row(arr, i, j):
        return arr[i, j][None, :]

    subs = []
    h_cur = _modulate_fwd(xs, mod_row(0, 1), mod_row(0, 0))
    x_cur = xs
    last = None
    for i in range(depth):
        j = i // 2
        if i % 2 == 0:
            y, res = _deltanet_fwd(h_cur, get_dn_in(j), conv_w_f[j], dn_a_log[j], dn_dt_bias[j], dn_norm_w[j],
                                   get_rows(("dn_out", j)))
        else:
            y, res = _conformer_fwd(h_cur, get_cols(("cf_in", j)), dw_w_f[j], dw_b_f[j][None, :], cf_ln_g_f[j][None, :],
                                    cf_ln_b_f[j][None, :], get_rows(("cf_out", j)))
        p1 = (mod_row(i, 2), ln_row(ln_g_f, i, 0), ln_row(ln_b_f, i, 0), mod_row(i, 4), mod_row(i, 3))
        x_mid, h_mid = _combine_fwd(alpha, x_cur, y, *p1)
        subs.append((x_cur, y, p1, res))
        m_out, res2 = _mlp_fwd(h_mid, get_cols(("ff1", i)), get_rows(("ff2", i)))
        if i + 1 < depth:
            p2 = (mod_row(i, 5), ln_row(ln_g_f, i, 1), ln_row(ln_b_f, i, 1), mod_row(i + 1, 1), mod_row(i + 1, 0))
            x_next, h_next = _combine_fwd(alpha, x_mid, m_out, *p2)
            subs.append((x_mid, m_out, p2, res2))
            x_cur, h_cur = x_next, h_next
        else:
            p2 = (mod_row(i, 5), ln_row(ln_g_f, i, 1), ln_row(ln_b_f, i, 1))
            last = (x_mid, m_out, p2, res2)

    x_in, y_in, p_last, res_last = last
    dx, dy, (loss_acc, g_gt, g_g, g_b) = _last_fwd_bwd(alpha, x_in, y_in, tgt, *p_last)
    loss = lax.psum(loss_acc[0, 0], ("x", "y", "c"))

    d_mod = [[None] * N_MOD for _ in range(depth)]
    d_ln_g = [[None, None] for _ in range(depth)]
    d_ln_b = [[None, None] for _ in range(depth)]
    d_mod[depth - 1][5], d_ln_g[depth - 1][1], d_ln_b[depth - 1][1] = g_gt, g_g, g_b
    gw = dict(dn=[None] * n_a, cf=[None] * n_b)

    sent = {}

    def send_grads(named, tag):
        parts = [p for _, p in named]
        hs, tok = _xfer_start(parts, [_landing_scatter(p, me) for p in parts], True, parts[0], "scatter_start_" + tag)
        for (key, _), hnd in zip(named, hs):
            sent[key] = hnd
        return tok[0, 0]

    def by_rows(g):
        return g.reshape((N_DEV, g.shape[0] // N_DEV, g.shape[1]))

    def send_mlp(i, d_w1, d_w2):
        return send_grads([(("ff1", i), d_w1), (("ff2", i), by_rows(d_w2))], "ff_%d" % i)

    dh, d_w1, d_w2 = _mlp_bwd(res_last, dy)
    pin = send_mlp(depth - 1, d_w1, d_w2)
    for idx in range(len(subs) - 1, -1, -1):
        x_in, y_in, prm, res = subs[idx]
        i, second = idx // 2, idx % 2
        prm = (prm[0] + pin,) + tuple(prm[1:])
        dx, dy, (g_gt, g_g, g_b, g_sc, g_sh) = _combine_bwd(alpha, x_in, y_in, dx, dh, *prm)
        d_mod[i][5 if second else 2], d_ln_g[i][second], d_ln_b[i][second] = g_gt, g_g, g_b
        nxt_i, nxt_base = (i + 1, 0) if second else (i, 3)
        d_mod[nxt_i][nxt_base + 1], d_mod[nxt_i][nxt_base] = g_sc, g_sh
        j = i // 2
        if second:
            dh, d_w1, d_w2 = _mlp_bwd(res, dy)
            pin = send_mlp(i, d_w1, d_w2)
        elif i % 2 == 0:
            dh, gw["dn"][j] = _deltanet_bwd(res, dy, conv_w_f[j])
            d_in = gw["dn"][j]["w_in"][:, :N_DEV * dn_in_cols].reshape(d, N_DEV, dn_in_cols)
            pin = send_grads([(("dn_in", j), jnp.moveaxis(d_in, 1, 0)), (("dn_out", j), by_rows(gw["dn"][j]["w_out"]))],
                             "dn_%d" % j)
        else:
            dh, gw["cf"][j] = _conformer_bwd(res, dy, dw_w_f[j], cf_ln_g_f[j][None, :], cf_ln_b_f[j][None, :])
            pin = send_grads([(("cf_in", j), gw["cf"][j]["w_in"]), (("cf_out", j), by_rows(gw["cf"][j]["w_out"]))],
                             "cf_%d" % j)
    grad_x, g_sc, g_sh = _modulate_bwd(xs, dx, dh, mod_row(0, 1) + pin, mod_row(0, 0))
    d_mod[0][1], d_mod[0][0] = g_sc, g_sh
    d_mod_full = jnp.concatenate([jnp.concatenate(r, axis=1) for r in d_mod], axis=0)

    stacked = {"dn_w_in": ("dn_in", dn_w_in, m_dn_w_in, v_dn_w_in), "dn_w_out": ("dn_out", dn_w_out, m_dn_w_out, v_dn_w_out),
               "cf_w_in": ("cf_in", cf_w_in, m_cf_w_in, v_cf_w_in), "cf_w_out": ("cf_out", cf_w_out, m_cf_w_out, v_cf_w_out),
               "ff_w1": ("ff1", ff_w1, m_ff_w1, v_ff_w1), "ff_w2": ("ff2", ff_w2, m_ff_w2, v_ff_w2)}
    chains = {key: None for key in stacked}

    def update_layer(i):
        mixer = ["dn_w_in", "dn_w_out"] if i % 2 == 0 else ["cf_w_in", "cf_w_out"]
        for key, idx in [("ff_w1", i), ("ff_w2", i)] + [(k, i // 2) for k in mixer]:
            short, w, m, v = stacked[key]
            parts = _xfer_wait(sent[(short, idx)], True, grad_x, "scatter_wait_%s_%d" % (short, idx))
            chains[key] = _adamw_layer(parts, w, m, v, idx, chains[key], "adamw_%s_%d" % (key, idx))

    for i in range(depth - 1, 0, -1):
        update_layer(i)

    def stack_rows(lst):
        return jnp.stack(lst, axis=0)

    gs_ln_g = jnp.stack([jnp.concatenate(r, axis=0) for r in d_ln_g], axis=0)
    gs_ln_b = jnp.stack([jnp.concatenate(r, axis=0) for r in d_ln_b], axis=0)
    gs_conv_w = stack_rows([gw["dn"][j]["conv_w"] for j in range(n_a)])
    gs_dw_w = stack_rows([gw["cf"][j]["dw_w"] for j in range(n_b)])
    gs_dw_b = stack_rows([gw["cf"][j]["dw_b"] for j in range(n_b)])
    gs_cf_ln_g = stack_rows([gw["cf"][j]["ln_g"] for j in range(n_b)])
    gs_cf_ln_b = stack_rows([gw["cf"][j]["ln_b"] for j in range(n_b)])
    gs_a_log = stack_rows([_pad_lanes(gw["dn"][j]["a_log"], 0)[0] for j in range(n_a)])
    gs_dt_bias = stack_rows([_pad_lanes(gw["dn"][j]["dt_bias"], 0)[0] for j in range(n_a)])
    gs_norm_w = stack_rows([gw["dn"][j]["norm_w"] for j in range(n_a)])
    small_grads = [gs_ln_g, gs_ln_b, gs_conv_w, gs_dw_w, gs_dw_b, gs_cf_ln_g, gs_cf_ln_b, gs_a_log, gs_dt_bias,
                   gs_norm_w, d_mod_full]
    sg_rows = [_rows(a) for a in small_grads]
    sg_sizes = [a.shape[0] for a in sg_rows]
    (sg_all,) = _exchange([_pad_rows(jnp.concatenate(sg_rows, axis=0))], False, "comm_gather_small_grads")
    sg_offs = [0]
    for z in sg_sizes:
        sg_offs.append(sg_offs[-1] + z)

    def sg(i, shape):
        return sg_all[:, sg_offs[i]:sg_offs[i + 1], :].reshape((N_DEV,) + shape)

    dmod_all = sg(10, (depth, N_MOD * d))
    nl = ada_w.shape[2]
    dmod_cols = lax.dynamic_slice_in_dim(dmod_all, me * nl, nl, axis=2)
    g_ada_w = _ada_bwd(cond_all, jnp.moveaxis(dmod_cols, 0, 1))

    outs = {}

    def run_adamw(key, parts, w, m, v):
        shp = w.shape
        as3 = lambda t: t.reshape((-1,) + shp[-2:]) if t.ndim >= 3 else t.reshape((1,) + shp)
        parts3 = parts.reshape((parts.shape[0],) + as3(w).shape)
        res = _adamw(parts3, as3(w), as3(m), as3(v), "adamw_" + key)
        outs[key] = tuple(r.reshape(shp) for r in res)

    run_adamw("ada_w", g_ada_w[None], ada_w, m_ada_w, v_ada_w)

    cgroups = 3 * heads // N_DEV
    shard_parts = [
        _shard_cols(sg(0, (depth, 2, d)), me, 1), _shard_cols(sg(1, (depth, 2, d)), me, 1),
        _shard_cols(sg(2, (n_a, DN_CONV, 3 * hw)), me, cgroups), _shard_cols(sg(3, (n_b, taps, d)), me, 1),
        _shard_cols(sg(4, (n_b, d)), me, 1), _shard_cols(sg(5, (n_b, d)), me, 1), _shard_cols(sg(6, (n_b, d)), me, 1),
    ]
    repl_parts = [sg(7, (n_a, LANES)), sg(8, (n_a, LANES)), sg(9, (n_a, HEAD_DIM)),
                  sg(10, (depth, N_MOD * d)).reshape(N_DEV, -1, LANES)]
    small_parts = shard_parts + repl_parts
    sp_sizes = [a.shape[1] for a in small_parts]
    parts_packed = jnp.concatenate(small_parts, axis=1)
    extra = (-parts_packed.shape[1]) % 8
    parts_packed = jnp.pad(parts_packed, ((0, 0), (0, extra), (0, 0)))

    def pad_heads(t):
        return jnp.pad(t, ((0, 0), (0, LANES - heads)))

    def pack_state(ln_g_, ln_b_, conv_w_, dw_w_, dw_b_, cln_g_, cln_b_, a_log_, dt_b_, norm_w_, ada_b_):
        rows = [_rows(ln_g_), _rows(ln_b_), _rows(conv_w_), _rows(dw_w_), _rows(dw_b_), _rows(cln_g_), _rows(cln_b_),
                pad_heads(a_log_), pad_heads(dt_b_), norm_w_, _rows(ada_b_)]
        return _pad_rows(jnp.concatenate(rows, axis=0))

    w_s = pack_state(ln_g, ln_b, dn_conv_w, cf_dw_w, cf_dw_b, cf_ln_g, cf_ln_b, dn_a_log, dn_dt_bias, dn_norm_w, ada_b)
    m_s = pack_state(m_ln_g, m_ln_b, m_dn_conv_w, m_cf_dw_w, m_cf_dw_b, m_cf_ln_g, m_cf_ln_b, m_dn_a_log,
                     m_dn_dt_bias, m_dn_norm_w, m_ada_b)
    v_s = pack_state(v_ln_g, v_ln_b, v_dn_conv_w, v_cf_dw_w, v_cf_dw_b, v_cf_ln_g, v_cf_ln_b, v_dn_a_log,
                     v_dn_dt_bias, v_dn_norm_w, v_ada_b)
    res_s = _adamw(parts_packed[:, None], w_s[None], m_s[None], v_s[None], "adamw_small")
    sp_offs = [0]
    for z in sp_sizes:
        sp_offs.append(sp_offs[-1] + z)
    small_keys = ["ln_g", "ln_b", "dn_conv_w", "cf_dw_w", "cf_dw_b", "cf_ln_g", "cf_ln_b", "dn_a_log", "dn_dt_bias",
                  "dn_norm_w", "ada_b"]
    small_shapes = [ln_g.shape, ln_b.shape, dn_conv_w.shape, cf_dw_w.shape, cf_dw_b.shape, cf_ln_g.shape,
                    cf_ln_b.shape, dn_a_log.shape, dn_dt_bias.shape, dn_norm_w.shape, ada_b.shape]
    for n, (key, shp) in enumerate(zip(small_keys, small_shapes)):
        vals = []
        for r in res_s:
            piece = r[0, sp_offs[n]:sp_offs[n + 1], :]
            if key in ("dn_a_log", "dn_dt_bias"):
                piece = piece[:, :heads]
            vals.append(piece.reshape(shp))
        outs[key] = tuple(vals)

    update_layer(0)
    for key in stacked:
        outs[key] = tuple(chains[key])

    order = ["ada_w", "ada_b", "ln_g", "ln_b", "dn_w_in", "dn_conv_w", "dn_a_log", "dn_dt_bias", "dn_norm_w",
             "dn_w_out", "cf_w_in", "cf_dw_w", "cf_dw_b", "cf_ln_g", "cf_ln_b", "cf_w_out", "ff_w1", "ff_w2"]
    result = [loss, grad_x[None]]
    for part in range(4):
        result += [outs[k][part] for k in order]
    return tuple(result)
```

```python
import functools

import jax
import jax.numpy as jnp
from jax import lax
from jax.experimental import pallas as pl
from jax.experimental.pallas import tpu as pltpu

F32 = jnp.float32
MXU_DTYPE = jnp.bfloat16
N_DEV = 8
LANES = 128
HEAD_DIM = 128
CHUNK = 64
DN_CONV = 4
N_MOD = 6
LN_EPS = 1e-5
RMS_EPS = 1e-6
L2_EPS = 1e-6
ADAM_LR = 0.001
ADAM_B1 = 0.9
ADAM_B2 = 0.999
ADAM_EPS = 1e-08
ADAM_WD = 0.01
ADAM_STEP = 10

HI = lax.Precision.HIGHEST
NN = ((1,), (0,))
NT = ((1,), (1,))
TN = ((0,), (0,))

ROW_TILE = 256
CONV_TILE = 256


def _mdot(a, b, dims=NN):
    return lax.dot_general(a.astype(MXU_DTYPE), b.astype(MXU_DTYPE), (dims, ((), ())), preferred_element_type=F32)


def _split3(x):
    hi = x.astype(MXU_DTYPE)
    r1 = x - hi.astype(F32)
    mid = r1.astype(MXU_DTYPE)
    lo = (r1 - mid.astype(F32)).astype(MXU_DTYPE)
    return hi, mid, lo


def _dot01(a, b, dims=NN, mask_first=True):
    d = lambda p, q: lax.dot_general(p, q, (dims, ((), ())), preferred_element_type=F32)
    if mask_first:
        m = a.astype(MXU_DTYPE)
        return sum(d(m, p) for p in _split3(b))
    m = b.astype(MXU_DTYPE)
    return sum(d(p, m) for p in _split3(a))


def _dot3(a, b, dims=NN):
    ah, am, _ = _split3(a)
    bh, bm, _ = _split3(b)
    d = lambda p, q: lax.dot_general(p, q, (dims, ((), ())), preferred_element_type=F32)
    return d(ah, bh) + (d(ah, bm) + d(am, bh))


def _cparams(n):
    return pltpu.CompilerParams(dimension_semantics=("arbitrary",) * n)


def _call(body, *, grid, ins, outs, name, scratch=()):
    res = pl.pallas_call(
        body,
        grid=grid,
        in_specs=[pl.BlockSpec(b, m) for _, b, m in ins],
        out_specs=[pl.BlockSpec(b, m) for _, _, b, m in outs],
        out_shape=[jax.ShapeDtypeStruct(s, d) for s, d, _, _ in outs],
        scratch_shapes=list(scratch),
        name=name,
        compiler_params=_cparams(len(grid)),
    )(*[a for a, _, _ in ins])
    return res


def _tile(n, pref, unit=LANES):
    if n <= pref:
        return n
    t = (pref // unit) * unit
    while t > unit and n % t:
        t -= unit
    assert n % t == 0, (n, pref)
    return t


def _rowmap(fn, rows, consts, row_outs, acc_outs, name):
    rows = [r if isinstance(r, tuple) else (r, r.shape[1], 0) for r in rows]
    s = rows[0][0].shape[0]
    tm = min(ROW_TILE, s)
    nr, nc, no, na = len(rows), len(consts), len(row_outs), len(acc_outs)

    def body(*refs):
        rin, cin = refs[:nr], refs[nr:nr + nc]
        rout, aout = refs[nr + nc:nr + nc + no], refs[nr + nc + no:]
        ro, ao = fn(*[r[...] for r in rin], *[c[...] for c in cin])
        for ref, val in zip(rout, ro):
            ref[...] = val.astype(ref.dtype)
        if na:
            first = pl.program_id(0) == 0

            @pl.when(first)
            def _():
                for ref, val in zip(aout, ao):
                    ref[...] = val

            @pl.when(jnp.logical_not(first))
            def _():
                for ref, val in zip(aout, ao):
                    ref[...] += val

    ins = [(a, (tm, w), functools.partial(lambda i, cb: (i, cb), cb=cb)) for a, w, cb in rows]
    ins += [(c, c.shape, lambda i: (0, 0)) for c in consts]
    outs = [((s, w), d, (tm, w), lambda i: (i, 0)) for w, d in row_outs]
    outs += [(shp, F32, shp, lambda i: (0, 0)) for shp in acc_outs]
    res = _call(body, grid=(s // tm,), ins=ins, outs=outs, name=name)
    return res[:no], res[no:]


def _ln(z, g, b):
    mu = jnp.mean(z, -1, keepdims=True)
    var = jnp.mean(jnp.square(z - mu), -1, keepdims=True)
    return (z - mu) * lax.rsqrt(var + LN_EPS) * g + b


def _combine(alpha, x, y, gt, g, b, sc, sh):
    xn = _ln(alpha * x + (1.0 + gt) * y, g, b)
    return xn, xn * (1.0 + sc) + sh


def _modulate_fwd(x, sc, sh):
    def fn(x, sc, sh):
        return ((x * (1.0 + sc) + sh),), ()

    (h,), _ = _rowmap(fn, [x], [sc, sh], [(x.shape[1], MXU_DTYPE)], [], "modulate_fwd")
    return h


def _modulate_bwd(x, dx, dh, sc, sh):
    d = x.shape[1]

    def fn(x, dx, dh, sc, sh):
        _, vjp = jax.vjp(lambda x, sc, sh: x * (1.0 + sc) + sh, x, sc, sh)
        gx, gsc, gsh = vjp(dh)
        return (dx + gx,), (gsc, gsh)

    (gx,), (gsc, gsh) = _rowmap(fn, [x, dx, dh], [sc, sh], [(d, F32)], [(1, d), (1, d)], "modulate_bwd")
    return gx, gsc, gsh


def _combine_fwd(alpha, x, y, gt, g, b, sc, sh):
    d = x.shape[1]

    def fn(x, y, gt, g, b, sc, sh):
        return _combine(alpha, x, y, gt, g, b, sc, sh), ()

    (xn, h), _ = _rowmap(fn, [x, y], [gt, g, b, sc, sh], [(d, F32), (d, MXU_DTYPE)], [], "combine_fwd")
    return xn, h


def _combine_bwd(alpha, x, y, dxn, dh, gt, g, b, sc, sh):
    d = x.shape[1]

    def fn(x, y, dxn, dh, gt, g, b, sc, sh):
        _, vjp = jax.vjp(functools.partial(_combine, alpha), x, y, gt, g, b, sc, sh)
        gx, gy, ggt, gg, gb, gsc, gsh = vjp((dxn, dh))
        return (gx, gy), (ggt, gg, gb, gsc, gsh)

    (gx, gy), accs = _rowmap(fn, [x, y, dxn, dh], [gt, g, b, sc, sh], [(d, F32), (d, MXU_DTYPE)],
                             [(1, d)] * 5, "combine_bwd")
    return gx, gy, accs


def _last_fwd_bwd(alpha, x, y, tgt, gt, g, b):
    d = x.shape[1]

    def fn(x, y, tgt, gt, g, b):
        xn, vjp = jax.vjp(lambda x, y, gt, g, b: _ln(alpha * x + (1.0 + gt) * y, g, b), x, y, gt, g, b)
        err = xn - tgt
        gx, gy, ggt, gg, gb = vjp(err * (1.0 / d))
        rows = jnp.sum(jnp.square(err), axis=-1, keepdims=True)
        loss = (0.5 / d) * jnp.sum(rows, axis=0, keepdims=True) * jnp.ones((1, LANES), F32)
        return (gx, gy), (loss, ggt, gg, gb)

    (gx, gy), accs = _rowmap(fn, [x, y, tgt], [gt, g, b], [(d, F32), (d, MXU_DTYPE)],
                             [(1, LANES), (1, d), (1, d), (1, d)], "last_fwd_bwd")
    return gx, gy, accs


MM_VMEM_BUDGET = 40 * 2 ** 20


def _fit(options, cost):
    for o in options:
        if 2 * cost(o) <= MM_VMEM_BUDGET:
            return o
    return options[-1]


def _row_tiles(m):
    return [t for t in (2048, 1024, 512, 256) if t <= m and m % t == 0] or [m]


def _mm_call(a, a_blk, a_map, b, b_blk, b_map, outs, dims, grid, name, epi=None, extra=None, split=None):
    nk = grid[2]
    n_out = len(outs)
    n_in = 3 if extra is not None else 2

    def body(*refs):
        a_ref, b_ref = refs[0], refs[1]
        rest = refs[n_in:]
        out_refs = rest[:n_out]

        def finish(val):
            if epi == "relu2":
                out_refs[0][...] = val.astype(out_refs[0].dtype)
                out_refs[1][...] = jnp.square(jnp.maximum(val, 0.0)).astype(out_refs[1].dtype)
            elif epi == "relu2_bwd":
                out_refs[0][...] = (val * 2.0 * jnp.maximum(refs[2][...], 0.0)).astype(out_refs[0].dtype)
            elif split is not None:
                for g in range(split[0]):
                    out_refs[0][g] = val[:, g * split[1]:(g + 1) * split[1]].astype(out_refs[0].dtype)
            else:
                out_refs[0][...] = val.astype(out_refs[0].dtype)

        p = lax.dot_general(a_ref[...], b_ref[...], (dims, ((), ())), preferred_element_type=F32)
        if nk == 1:
            finish(p)
        else:
            acc = rest[n_out]
            k = pl.program_id(2)

            @pl.when(k == 0)
            def _():
                acc[...] = p

            @pl.when(k > 0)
            def _():
                acc[...] += p

            @pl.when(k == nk - 1)
            def _():
                finish(acc[...])

    if nk > 1:
        out_blk = tuple(x for x in outs[0][2] if x is not None)
        if split is not None:
            out_blk = (out_blk[1], split[0] * split[1])
        scratch = [pltpu.VMEM(out_blk, F32)]
    else:
        scratch = []
    ins = [(a, a_blk, a_map), (b, b_blk, b_map)] + ([extra] if extra is not None else [])
    return _call(body, grid=grid, ins=ins, outs=outs, name=name, scratch=scratch)


def _isz(dt):
    return jnp.dtype(dt).itemsize


def _mm_nn(a, b, out_dtype, name, relu2=False):
    m, kdim = a.shape
    if b.ndim == 2:
        n = b.shape[1]
        tn = _tile(n, 512)
        b_blk, b_map = (kdim, tn), lambda i, j, k: (0, j)
    else:
        g, _, ng = b.shape
        n = g * ng
        tn = _tile(ng, 512)
        b_blk = (None, kdim, tn)
        b_map = functools.partial(lambda i, j, k, npg: (j // npg, 0, j % npg), npg=ng // tn)
    out_bytes = (4 + _isz(out_dtype)) if relu2 else _isz(out_dtype)
    tm = _fit(_row_tiles(m), lambda t: t * kdim * _isz(a.dtype) + kdim * tn * _isz(b.dtype) + t * tn * out_bytes)
    grid = (m // tm, n // tn, 1)
    outs = [((m, n), F32 if relu2 else out_dtype, (tm, tn), lambda i, j, k: (i, j))]
    if relu2:
        outs.append(((m, n), out_dtype, (tm, tn), lambda i, j, k: (i, j)))
    res = _mm_call(a, (tm, kdim), lambda i, j, k: (i, 0), b, b_blk, b_map, outs, NN, grid, name,
                   epi="relu2" if relu2 else None)
    return res if relu2 else res[0]


def _mm_nt(a, b, out_dtype, name, relu2_of=None):
    m, n = a.shape
    extra_bytes = 4 if relu2_of is not None else 0
    if b.ndim == 2:
        kout = b.shape[0]
        to, tc, nk = _tile(kout, 512), n, 1
        b_blk, b_map = (to, tc), lambda i, j, k: (j, 0)
        acc_bytes = 0
    else:
        nk, kout, tc = b.shape
        to = _tile(kout, 1024)
        b_blk, b_map = (None, to, tc), lambda i, j, k: (k, j, 0)
        acc_bytes = 2
    tm = _fit(_row_tiles(m), lambda t: t * tc * _isz(a.dtype) + to * tc * _isz(b.dtype)
              + t * to * (_isz(out_dtype) + extra_bytes + acc_bytes))
    grid = (m // tm, kout // to, nk)
    outs = [((m, kout), out_dtype, (tm, to), lambda i, j, k: (i, j))]
    extra = (relu2_of, (tm, to), lambda i, j, k: (i, j)) if relu2_of is not None else None
    return _mm_call(a, (tm, tc), lambda i, j, k: (i, k), b, b_blk, b_map, outs, NT, grid, name,
                    epi="relu2_bwd" if relu2_of is not None else None, extra=extra)[0]


def _mm_tn(a, b, out_dtype, name, split_cols=False):
    m, kdim = a.shape
    n = b.shape[1]
    tk = _tile(kdim, 512)
    tn = _tile(n, 1024)
    if not split_cols:
        out, split = ((kdim, n), out_dtype, (tk, tn), lambda i, j, k: (i, j)), None
    else:
        ng = n // N_DEV
        if tn % ng:
            tn = _tile(ng, 512)
        if tn >= ng:
            gb = tn // ng
            out = ((N_DEV, kdim, ng), out_dtype, (gb, tk, ng), lambda i, j, k: (j, i, 0))
            split = (gb, ng)
        else:
            out = ((N_DEV, kdim, ng), out_dtype, (None, tk, tn),
                   functools.partial(lambda i, j, k, npg: (j // npg, i, j % npg), npg=ng // tn))
            split = None
    grid = (kdim // tk, n // tn, 1)
    return _mm_call(a, (m, tk), lambda i, j, k: (0, i), b, (m, tn), lambda i, j, k: (0, j), [out], TN, grid, name,
                    split=split)[0]


def _shifted(xa, off, rows):
    if off % 8 == 0:
        return xa[off:off + rows]
    return pltpu.roll(xa, xa.shape[0] - off, 0)[:rows]


def _conv_pad(taps):
    return -(-(taps - 1) // 8) * 8


def _conv_tile(xp_ref, w, i, rows, taps):
    pad = _conv_pad(taps)
    r0 = pl.multiple_of(i * rows, rows)
    xa = xp_ref[pl.ds(r0, rows + pad), :]
    views = [_shifted(xa, pad - (taps - 1) + j, rows) for j in range(taps)]
    acc = w[0:1, :] * views[0]
    for j in range(1, taps):
        acc = acc + w[j:j + 1, :] * views[j]
    return r0, acc, views


def _conv_back_tile(yp_ref, w, i, rows, taps):
    pad = _conv_pad(taps)
    r0 = pl.multiple_of(i * rows, rows)
    ya = yp_ref[pl.ds(r0, rows + pad), :]
    acc = w[taps - 1:taps, :] * ya[:rows]
    for j in range(taps - 1):
        acc = acc + w[j:j + 1, :] * _shifted(ya, taps - 1 - j, rows)
    return r0, acc


def _tap_sums(dy, views, taps):
    row = lax.broadcasted_iota(jnp.int32, (taps, LANES), 0)
    acc = jnp.zeros((taps, LANES), F32)
    for j in range(taps):
        acc = acc + jnp.where(row == j, jnp.sum(dy * views[j], axis=0, keepdims=True), 0.0)
    return acc


def _silu_l2(xc, l2):
    a = jax.nn.silu(xc)
    if l2:
        a = a * lax.rsqrt(jnp.sum(a * a, axis=-1, keepdims=True) + L2_EPS)
    return a


def _dn_conv_fwd(proj, conv_w, c0, nblk, l2, name):
    s = proj.shape[0]
    pad = _conv_pad(DN_CONV)
    rows = min(CONV_TILE, s)

    def body(x_ref, w_ref, o_ref, xp):
        xp[0:pad, :] = jnp.zeros((pad, LANES), F32)
        xp[pad:, :] = x_ref[...]
        w = w_ref[...]

        def tile(i, c):
            r0, acc, _ = _conv_tile(xp, w, i, rows, DN_CONV)
            o_ref[pl.ds(r0, rows), :] = _silu_l2(acc, l2)
            return c

        lax.fori_loop(0, s // rows, tile, 0)

    return _call(body, grid=(nblk,),
                 ins=[(proj, (s, LANES), lambda c: (0, c0 + c)), (conv_w, (DN_CONV, LANES), lambda c: (0, c0 + c))],
                 outs=[((nblk, s, LANES), F32, (None, s, LANES), lambda c: (c, 0, 0))],
                 name=name, scratch=[pltpu.VMEM((s + pad, LANES), F32)])[0]


def _dn_conv_bwd(proj, conv_w, da, c0, nblk, l2, name):
    s = proj.shape[0]
    pad = _conv_pad(DN_CONV)
    rows = min(CONV_TILE, s)

    def body(x_ref, w_ref, da_ref, dx_ref, dw_ref, xp, yp):
        xp[0:pad, :] = jnp.zeros((pad, LANES), F32)
        xp[pad:, :] = x_ref[...]
        yp[s:, :] = jnp.zeros((pad, LANES), F32)
        w = w_ref[...]

        def tile(i, dw):
            r0, acc, views = _conv_tile(xp, w, i, rows, DN_CONV)
            _, vjp = jax.vjp(functools.partial(_silu_l2, l2=l2), acc)
            (dxc,) = vjp(da_ref[pl.ds(r0, rows), :])
            yp[pl.ds(r0, rows), :] = dxc
            return dw + _tap_sums(dxc, views, DN_CONV)

        dw_ref[...] = lax.fori_loop(0, s // rows, tile, jnp.zeros((DN_CONV, LANES), F32))

        def tile2(i, c):
            r0, acc = _conv_back_tile(yp, w, i, rows, DN_CONV)
            dx_ref[pl.ds(r0, rows), :] = acc.astype(dx_ref.dtype)
            return c

        lax.fori_loop(0, s // rows, tile2, 0)

    return _call(body, grid=(nblk,),
                 ins=[(proj, (s, LANES), lambda c: (0, c0 + c)), (conv_w, (DN_CONV, LANES), lambda c: (0, c0 + c)),
                      (da, (None, s, LANES), lambda c: (c, 0, 0))],
                 outs=[((s, nblk * LANES), MXU_DTYPE, (s, LANES), lambda c: (0, c)),
                       ((DN_CONV, nblk * LANES), F32, (DN_CONV, LANES), lambda c: (0, c))],
                 name=name, scratch=[pltpu.VMEM((s + pad, LANES), F32), pltpu.VMEM((s + pad, LANES), F32)])


def _cf_conv_fwd(vg, dw_w, dw_b):
    s, c2 = vg.shape
    ch = c2 // 2
    nblk = ch // LANES
    taps = dw_w.shape[0]
    pad = _conv_pad(taps)
    rows = min(CONV_TILE, s)

    def body(v_ref, g_ref, w_ref, b_ref, o_ref, xp):
        xp[0:pad, :] = jnp.zeros((pad, LANES), F32)
        xp[pad:, :] = v_ref[...] * jax.nn.sigmoid(g_ref[...])
        w = w_ref[...]
        bias = b_ref[...]

        def tile(i, c):
            r0, acc, _ = _conv_tile(xp, w, i, rows, taps)
            o_ref[pl.ds(r0, rows), :] = acc + bias
            return c

        lax.fori_loop(0, s // rows, tile, 0)

    return _call(body, grid=(nblk,),
                 ins=[(vg, (s, LANES), lambda c: (0, c)), (vg, (s, LANES), lambda c: (0, nblk + c)),
                      (dw_w, (taps, LANES), lambda c: (0, c)), (dw_b, (1, LANES), lambda c: (0, c))],
                 outs=[((s, ch), F32, (s, LANES), lambda c: (0, c))],
                 name="cf_conv_fwd", scratch=[pltpu.VMEM((s + pad, LANES), F32)])[0]


def _cf_conv_bwd(vg, dw_w, du):
    s, c2 = vg.shape
    ch = c2 // 2
    nblk = ch // LANES
    taps = dw_w.shape[0]
    pad = _conv_pad(taps)
    rows = min(CONV_TILE, s)

    def body(v_ref, g_ref, w_ref, du_ref, dv_ref, dg_ref, dw_ref, db_ref, xp, yp):
        sig = jax.nn.sigmoid(g_ref[...])
        xp[0:pad, :] = jnp.zeros((pad, LANES), F32)
        xp[pad:, :] = v_ref[...] * sig
        yp[0:s, :] = du_ref[...]
        yp[s:, :] = jnp.zeros((pad, LANES), F32)
        w = w_ref[...]
        db_ref[...] = jnp.sum(du_ref[...], axis=0, keepdims=True)

        def tile(i, dw):
            r0, _, views = _conv_tile(xp, w, i, rows, taps)
            return dw + _tap_sums(du_ref[pl.ds(r0, rows), :], views, taps)

        dw_ref[...] = lax.fori_loop(0, s // rows, tile, jnp.zeros((taps, LANES), F32))

        def tile2(i, c):
            r0, du0 = _conv_back_tile(yp, w, i, rows, taps)
            val = v_ref[pl.ds(r0, rows), :]
            sg = jax.nn.sigmoid(g_ref[pl.ds(r0, rows), :])
            dv_ref[pl.ds(r0, rows), :] = (du0 * sg).astype(dv_ref.dtype)
            dg_ref[pl.ds(r0, rows), :] = (du0 * val * sg * (1.0 - sg)).astype(dg_ref.dtype)
            return c

        lax.fori_loop(0, s // rows, tile2, 0)

    return _call(body, grid=(nblk,),
                 ins=[(vg, (s, LANES), lambda c: (0, c)), (vg, (s, LANES), lambda c: (0, nblk + c)),
                      (dw_w, (taps, LANES), lambda c: (0, c)), (du, (s, LANES), lambda c: (0, c))],
                 outs=[((s, ch), MXU_DTYPE, (s, LANES), lambda c: (0, c)),
                       ((s, ch), MXU_DTYPE, (s, LANES), lambda c: (0, c)),
                       ((taps, ch), F32, (taps, LANES), lambda c: (0, c)),
                       ((1, ch), F32, (1, LANES), lambda c: (0, c))],
                 name="cf_conv_bwd", scratch=[pltpu.VMEM((s + pad, LANES), F32), pltpu.VMEM((s + pad, LANES), F32)])


def _masks():
    r = lax.broadcasted_iota(jnp.int32, (CHUNK, CHUNK), 0)
    c = lax.broadcasted_iota(jnp.int32, (CHUNK, CHUNK), 1)
    return r >= c, r > c, r <= c


def _chunk_decay(g):
    causal, _, upper = _masks()
    gb = jnp.broadcast_to(g, (CHUNK, CHUNK))
    gam_r = _dot01(jnp.where(causal, 1.0, 0.0), gb)
    gam_s = _dot01(jnp.ones((CHUNK, CHUNK), F32), jnp.where(upper, gb, 0.0))
    dm = jnp.where(causal, jnp.exp(jnp.where(causal, gam_r - gam_s, 0.0)), 0.0)
    return gam_r[:, 0:1], dm


def _chunk_scores(q, k, beta, dm):
    _, strict, _ = _masks()
    both = _mdot(jnp.concatenate([k * beta, q * (HEAD_DIM ** -0.5)], axis=0), k, NT)
    return jnp.where(strict, both[:CHUNK] * dm, 0.0), both[CHUNK:] * dm


def _lockstep(gens):
    results = [None] * len(gens)
    alive = list(range(len(gens)))
    while alive:
        for i in list(alive):
            try:
                next(gens[i])
            except StopIteration as stop:
                results[i] = stop.value
                alive.remove(i)
    return results


def _chunk_prep_bwd(q, k, v, beta, gam, t, du, dw, daqk, dqd, dkd, dgl):
    causal, strict, _ = _masks()
    r = lax.broadcasted_iota(jnp.int32, (CHUNK, CHUNK), 0)
    c = lax.broadcasted_iota(jnp.int32, (CHUNK, CHUNK), 1)
    scale = HEAD_DIM ** -0.5
    eg = jnp.exp(gam)
    gam_last = gam[CHUNK - 1:CHUNK, :]
    rr = jnp.exp(gam_last - gam)
    kb = k * beta
    qs = q * scale
    vb = v * beta
    kbe = kb * eg
    gam_b = jnp.broadcast_to(gam, (CHUNK, CHUNK))
    gam_s = _dot01(jnp.ones((CHUNK, CHUNK), F32), jnp.where(r == c, gam_b, 0.0))
    both = _mdot(jnp.concatenate([kb, qs], axis=0), k, NT)
    duw = jnp.concatenate([du, dw], axis=1)
    dt = _mdot(duw, jnp.concatenate([vb, kbe], axis=1), NT)
    dvk = _mdot(t, duw, TN)
    yield
    dm = jnp.where(causal, jnp.exp(jnp.where(causal, gam_b - gam_s, 0.0)), 0.0)
    a = jnp.where(strict, both[:CHUNK] * dm, 0.0)
    aqk = both[CHUNK:] * dm
    dvb, dkbe = dvk[:, :HEAD_DIM], dvk[:, HEAD_DIM:]
    x = _dot3(t, dt, TN)
    yield
    da = jnp.where(strict, -_dot3(x, t, NT), 0.0)
    yield
    dkk = da * dm
    dqk = daqk * dm
    ddiff = da * a + daqk * aqk
    dboth = jnp.concatenate([dkk, dqk], axis=0)
    dkq = _mdot(dboth, k)
    dk_mm = _mdot(dboth, jnp.concatenate([kb, qs], axis=0), TN)
    colsum = _dot01(ddiff, jnp.ones((CHUNK, LANES), F32), TN, mask_first=False)[:, 0:1]
    yield
    dkb = dkq[:CHUNK] + dkbe * eg
    dk = dk_mm + dkb * beta + dkd * rr
    dq = (dkq[CHUNK:] + dqd * eg) * scale
    dbeta = jnp.sum(dkb * k, axis=-1, keepdims=True) + jnp.sum(dvb * v, axis=-1, keepdims=True)
    dv = dvb * beta
    deg = jnp.sum(dkbe * kb, axis=-1, keepdims=True) + jnp.sum(dqd * qs, axis=-1, keepdims=True)
    drr = jnp.sum(dkd * k, axis=-1, keepdims=True)
    dgam = deg * eg - drr * rr + jnp.sum(ddiff, axis=-1, keepdims=True) - colsum
    dgam_last = jnp.sum(drr * rr, axis=0, keepdims=True) + dgl[0:1, :] * jnp.exp(gam_last)
    row = lax.broadcasted_iota(jnp.int32, (CHUNK, 1), 0)
    dgam = dgam + jnp.where(row == CHUNK - 1, dgam_last, 0.0)
    dg = _dot01(jnp.where(causal, 1.0, 0.0), jnp.broadcast_to(dgam, (CHUNK, LANES)), TN)[:, 0:1]
    return dq, dk, dv, dbeta, dg


def _prep_group(s):
    nch = s // CHUNK
    return next(c for c in (16, 8, 4, 2, 1) if nch % c == 0)


def _tri_solve_lanes(a_l):
    n = a_l.shape[1]
    group = 8

    def body(a_ref, t_ref):
        t_ref[...] = jnp.zeros_like(t_ref)
        col = lax.broadcasted_iota(jnp.int32, (CHUNK, n), 0)

        def row(r, carry):
            r0 = pl.multiple_of(r * CHUNK, CHUNK)

            def inner(sg, acc):
                a8 = a_ref[pl.ds(r0 + pl.multiple_of(sg * group, group), group), :]
                for j in range(group):
                    t0 = pl.multiple_of((sg * group + j) * CHUNK, CHUNK)
                    acc = acc + a8[j:j + 1, :] * t_ref[pl.ds(t0, CHUNK), :]
                return acc

            acc = lax.fori_loop(0, (r + group - 1) // group, inner, jnp.zeros((CHUNK, n), F32))
            t_ref[pl.ds(r0, CHUNK), :] = jnp.where(col == r, 1.0, 0.0) - acc
            return carry

        lax.fori_loop(0, CHUNK, row, 0)

    return pl.pallas_call(body, out_shape=jax.ShapeDtypeStruct(a_l.shape, F32), name="dn_tri_solve")(a_l)


def _head_cols(bg, hh, heads):
    lane = lax.broadcasted_iota(jnp.int32, bg.shape, 1)
    beta = jnp.sum(jnp.where(lane == hh, bg, 0.0), axis=-1, keepdims=True)
    g = jnp.sum(jnp.where(lane == heads + hh, bg, 0.0), axis=-1, keepdims=True)
    return beta, g


def _dn_prep(q, k, v, bg):
    h, s, _ = q.shape
    cb = _prep_group(s)
    rb = cb * CHUNK
    big = lambda x: (x, (None, rb, HEAD_DIM), lambda n, hh: (hh, n, 0))
    sq = lambda x: (x, (None, rb, CHUNK), lambda n, hh: (hh, n, 0))
    col = lambda x: (x, (None, rb, 1), lambda n, hh: (hh, n, 0))
    tok = (bg, (rb, LANES), lambda n, hh: (n, 0))
    o_big = ((h, s, HEAD_DIM), F32, (None, rb, HEAD_DIM), lambda n, hh: (hh, n, 0))
    o_sq = ((h, s, CHUNK), F32, (None, rb, CHUNK), lambda n, hh: (hh, n, 0))
    o_col = ((h, s, 1), F32, (None, rb, 1), lambda n, hh: (hh, n, 0))

    def scores(q_ref, k_ref, bg_ref, a_ref, aqk_ref, gam_ref):
        beta, g = _head_cols(bg_ref[...], pl.program_id(1), h)
        for i in range(cb):
            sl = slice(i * CHUNK, (i + 1) * CHUNK)
            gam, dm = _chunk_decay(g[sl])
            a_ref[sl, :], aqk_ref[sl, :] = _chunk_scores(q_ref[sl, :], k_ref[sl, :], beta[sl], dm)
            gam_ref[sl, :] = gam

    a, aqk, gam = _call(scores, grid=(s // rb, h), ins=[big(q), big(k), tok], outs=[o_sq, o_sq, o_col],
                        name="dn_scores")
    n_prob = h * (s // CHUNK)
    t_l = _tri_solve_lanes(jnp.transpose(a.reshape(n_prob, CHUNK * CHUNK)))
    t = jnp.transpose(t_l).reshape(h, s, CHUNK)

    def wy(k_ref, v_ref, bg_ref, gam_ref, t_ref, u_ref, w_ref):
        beta, _ = _head_cols(bg_ref[...], pl.program_id(1), h)
        for i in range(cb):
            sl = slice(i * CHUNK, (i + 1) * CHUNK)
            kb = k_ref[sl, :] * beta[sl]
            rhs = jnp.concatenate([v_ref[sl, :] * beta[sl], kb * jnp.exp(gam_ref[sl, :])], axis=1)
            uw = _mdot(t_ref[sl, :], rhs)
            u_ref[sl, :] = uw[:, :HEAD_DIM]
            w_ref[sl, :] = uw[:, HEAD_DIM:]

    u, w = _call(wy, grid=(s // rb, h), ins=[big(k), big(v), tok, col(gam), sq(t)], outs=[o_big, o_big],
                 name="dn_wy")
    return u, w, aqk, t, gam


def _dn_prep_bwd(q, k, v, bg, gam, t, du, dw, daqk, dqd, dkd, dgl):
    h, s, _ = q.shape
    cb = _prep_group(s)
    rb = cb * CHUNK

    def body(q_ref, k_ref, v_ref, bg_ref, g_ref, t_ref, du_ref, dw_ref, da_ref, dqd_ref, dkd_ref, dgl_ref,
             dq_ref, dk_ref, dv_ref, dbg_ref):
        hh = pl.program_id(1)
        beta, _ = _head_cols(bg_ref[...], hh, h)
        slices = [slice(i * CHUNK, (i + 1) * CHUNK) for i in range(cb)]
        results = _lockstep([_chunk_prep_bwd(
            q_ref[sl, :], k_ref[sl, :], v_ref[sl, :], beta[sl], g_ref[sl, :], t_ref[sl, :],
            du_ref[sl, :], dw_ref[sl, :], da_ref[sl, :], dqd_ref[sl, :], dkd_ref[sl, :], dgl_ref[sl, :])
            for sl in slices])

        @pl.when(hh == 0)
        def _():
            dbg_ref[...] = jnp.zeros_like(dbg_ref)

        lane = lax.broadcasted_iota(jnp.int32, (CHUNK, LANES), 1)
        for sl, (dq, dk, dv, dbeta, dg) in zip(slices, results):
            dq_ref[sl, :] = dq
            dk_ref[sl, :] = dk
            dv_ref[sl, :] = dv
            dbg_ref[sl, :] += jnp.where(lane == hh, dbeta, 0.0) + jnp.where(lane == h + hh, dg, 0.0)

    big = lambda x: (x, (None, rb, HEAD_DIM), lambda n, hh: (hh, n, 0))
    sq = lambda x: (x, (None, rb, CHUNK), lambda n, hh: (hh, n, 0))
    col = lambda x: (x, (None, rb, 1), lambda n, hh: (hh, n, 0))
    tok = (bg, (rb, LANES), lambda n, hh: (n, 0))
    o_big = ((h, s, HEAD_DIM), F32, (None, rb, HEAD_DIM), lambda n, hh: (hh, n, 0))
    return _call(body, grid=(s // rb, h),
                 ins=[big(q), big(k), big(v), tok, col(gam), sq(t), big(du), big(dw), sq(daqk), big(dqd), big(dkd),
                      col(dgl)],
                 outs=[o_big, o_big, o_big, ((s, LANES), F32, (rb, LANES), lambda n, hh: (n, 0))], name="dn_prep_bwd")


def _chunk_scaled(q, k, gam):
    gam_last = gam[CHUNK - 1:CHUNK, :]
    q_dec = q * (HEAD_DIM ** -0.5) * jnp.exp(gam)
    k_dec = k * jnp.exp(gam_last - gam)
    return q_dec, k_dec, jnp.exp(gam_last)


def _scan_group(s):
    return 2 if (s // CHUNK) % 2 == 0 else 1


def _dn_scan(q, k, u, w, aqk, gam):
    h, s, _ = q.shape
    nch = s // CHUNK
    sg = _scan_group(s)
    rb = sg * CHUNK

    def body(q_ref, k_ref, u_ref, w_ref, a_ref, gam_ref, o_ref, st_ref, state):
        @pl.when(pl.program_id(0) == 0)
        def _():
            state[...] = jnp.zeros_like(state)

        def head(hh, c):
            sl = slice(c * CHUNK, (c + 1) * CHUNK)
            s0 = state[hh]
            st_ref[c, hh] = s0
            q_dec, k_dec, gl = _chunk_scaled(q_ref[hh, sl, :], k_ref[hh, sl, :], gam_ref[hh, sl, :])
            both = _mdot(jnp.concatenate([w_ref[hh, sl, :], q_dec], axis=0), s0)
            yield
            v_new = u_ref[hh, sl, :] - both[:CHUNK]
            o_ref[sl, hh * HEAD_DIM:(hh + 1) * HEAD_DIM] = both[CHUNK:] + _mdot(a_ref[hh, sl, :], v_new)
            state[hh] = s0 * gl + _mdot(k_dec, v_new, TN)

        for c in range(sg):
            _lockstep([head(hh, c) for hh in range(h)])

    big = lambda x: (x, (h, rb, HEAD_DIM), lambda n: (0, n, 0))
    return _call(body, grid=(nch // sg,),
                 ins=[big(q), big(k), big(u), big(w), (aqk, (h, rb, CHUNK), lambda n: (0, n, 0)),
                      (gam, (h, rb, 1), lambda n: (0, n, 0))],
                 outs=[((s, h * HEAD_DIM), F32, (rb, h * HEAD_DIM), lambda n: (n, 0)),
                       ((nch, h, HEAD_DIM, HEAD_DIM), F32, (sg, h, HEAD_DIM, HEAD_DIM), lambda n: (n, 0, 0, 0))],
                 name="dn_scan", scratch=[pltpu.VMEM((h, HEAD_DIM, HEAD_DIM), F32)])


def _dn_scan_bwd(q, k, u, w, aqk, gam, states, do):
    h, s, _ = q.shape
    nch = s // CHUNK
    sg = _scan_group(s)
    rb = sg * CHUNK
    ngr = nch // sg

    def body(q_ref, k_ref, u_ref, w_ref, a_ref, gam_ref, st_ref, do_ref,
             du_ref, dw_ref, da_ref, dqd_ref, dkd_ref, dgl_ref, dstate):
        @pl.when(pl.program_id(0) == 0)
        def _():
            dstate[...] = jnp.zeros_like(dstate)

        def head(hh, c):
            sl = slice(c * CHUNK, (c + 1) * CHUNK)
            s0 = st_ref[c, hh]
            ds = dstate[hh]
            doh = do_ref[sl, hh * HEAD_DIM:(hh + 1) * HEAD_DIM]
            wv = w_ref[hh, sl, :]
            q_dec, k_dec, gl = _chunk_scaled(q_ref[hh, sl, :], k_ref[hh, sl, :], gam_ref[hh, sl, :])
            ws = _mdot(wv, s0)
            dv_new = _mdot(a_ref[hh, sl, :], doh, TN) + _mdot(k_dec, ds)
            dqd_ref[hh, sl, :] = _mdot(doh, s0, NT)
            qdo = _mdot(q_dec, doh, TN)
            tot = jnp.sum(jnp.sum(s0 * ds, axis=-1, keepdims=True), axis=0, keepdims=True)
            dgl_ref[hh, sl, :] = jnp.broadcast_to(tot, (CHUNK, 1))
            yield
            v_new = u_ref[hh, sl, :] - ws
            du_ref[hh, sl, :] = dv_new
            dw_ref[hh, sl, :] = -_mdot(dv_new, s0, NT)
            da_ref[hh, sl, :] = _mdot(doh, v_new, NT)
            dkd_ref[hh, sl, :] = _mdot(v_new, ds, NT)
            dstate[hh] = ds * gl + qdo - _mdot(wv, dv_new, TN)

        for c in range(sg - 1, -1, -1):
            _lockstep([head(hh, c) for hh in range(h)])

    rev = lambda n: (0, ngr - 1 - n, 0)
    big = lambda x: (x, (h, rb, HEAD_DIM), rev)
    o_big = ((h, s, HEAD_DIM), F32, (h, rb, HEAD_DIM), rev)
    return _call(body, grid=(ngr,),
                 ins=[big(q), big(k), big(u), big(w), (aqk, (h, rb, CHUNK), rev), (gam, (h, rb, 1), rev),
                      (states, (sg, h, HEAD_DIM, HEAD_DIM), lambda n: (ngr - 1 - n, 0, 0, 0)),
                      (do, (rb, h * HEAD_DIM), lambda n: (ngr - 1 - n, 0))],
                 outs=[o_big, o_big, ((h, s, CHUNK), F32, (h, rb, CHUNK), rev), o_big, o_big,
                       ((h, s, 1), F32, (h, rb, 1), rev)],
                 name="dn_scan_bwd", scratch=[pltpu.VMEM((h, HEAD_DIM, HEAD_DIM), F32)])


def _gates(x, a_log, dt_b, h):
    lane = lax.broadcasted_iota(jnp.int32, x.shape, 1)
    return jnp.where(lane < h, jax.nn.sigmoid(x), -jnp.exp(a_log) * jax.nn.softplus(x + dt_b))


def _head_out(oh, zh, nw):
    on = oh * lax.rsqrt(jnp.mean(oh * oh, axis=-1, keepdims=True) + RMS_EPS) * nw
    return on * jax.nn.silu(zh)


def _pad_lanes(x, lo):
    return jnp.zeros((1, LANES), F32).at[0, lo:lo + x.shape[0]].set(x)


def _deltanet_fwd(hin, get_w_in, conv_w, a_log, dt_bias, norm_w, get_w_out):
    h = a_log.shape[0]
    hw = h * HEAD_DIM
    w_in = get_w_in(hin)
    proj = _mm_nn(hin, w_in, F32, "dn_proj")
    q = _dn_conv_fwd(proj, conv_w, 0, h, True, "dn_conv_q")
    k = _dn_conv_fwd(proj, conv_w, h, h, True, "dn_conv_k")
    v = _dn_conv_fwd(proj, conv_w, 2 * h, h, False, "dn_conv_v")
    alp, dtp = _pad_lanes(a_log, h), _pad_lanes(dt_bias, h)

    def gates_fn(x, al, db):
        return (_gates(x, al, db, h),), ()

    (bg,), _ = _rowmap(gates_fn, [(proj, LANES, 4 * h)], [alp, dtp], [(LANES, F32)], [], "dn_gates")
    u, w, aqk, t, gam = _dn_prep(q, k, v, bg)
    o, states = _dn_scan(q, k, u, w, aqk, gam)
    nw = norm_w[None, :]

    def out_fn(o, z, nw):
        parts = [_head_out(o[:, i * HEAD_DIM:(i + 1) * HEAD_DIM], z[:, i * HEAD_DIM:(i + 1) * HEAD_DIM], nw)
                 for i in range(h)]
        return (jnp.concatenate(parts, axis=-1),), ()

    (og,), _ = _rowmap(out_fn, [o, (proj, hw, 3)], [nw], [(hw, MXU_DTYPE)], [], "dn_out")
    w_out = get_w_out(og)
    y = _mm_nn(og, w_out, F32, "dn_y")
    return y, (hin, proj, q, k, v, bg, u, w, aqk, t, gam, states, o, og, alp, dtp, nw, w_in, w_out)


def _deltanet_bwd(res, dy, conv_w):
    hin, proj, q, k, v, bg, u, w, aqk, t, gam, states, o, og, alp, dtp, nw, w_in, w_out = res
    h = q.shape[0]
    hw = h * HEAD_DIM
    s = hin.shape[0]
    d_w_out = _mm_tn(og, dy, MXU_DTYPE, "dn_dwout")
    dog = _mm_nt(dy, w_out, F32, "dn_dog")

    def out_bwd(o, z, dog, nw):
        dos, dzs = [], []
        dn = jnp.zeros((1, HEAD_DIM), F32)
        for i in range(h):
            sl = slice(i * HEAD_DIM, (i + 1) * HEAD_DIM)
            _, vjp = jax.vjp(_head_out, o[:, sl], z[:, sl], nw)
            a, b, c = vjp(dog[:, sl])
            dos.append(a)
            dzs.append(b)
            dn = dn + c
        return (jnp.concatenate(dos, axis=-1), jnp.concatenate(dzs, axis=-1)), (dn,)

    (do, dz), (d_norm_w,) = _rowmap(out_bwd, [o, (proj, hw, 3), dog], [nw], [(hw, F32), (hw, MXU_DTYPE)],
                                    [(1, HEAD_DIM)], "dn_out_bwd")
    du, dw, daqk, dqd, dkd, dgl = _dn_scan_bwd(q, k, u, w, aqk, gam, states, do)
    dq, dk, dv, dbg = _dn_prep_bwd(q, k, v, bg, gam, t, du, dw, daqk, dqd, dkd, dgl)
    dpq, dwq = _dn_conv_bwd(proj, conv_w, dq, 0, h, True, "dn_conv_q_bwd")
    dpk, dwk = _dn_conv_bwd(proj, conv_w, dk, h, h, True, "dn_conv_k_bwd")
    dpv, dwv = _dn_conv_bwd(proj, conv_w, dv, 2 * h, h, False, "dn_conv_v_bwd")

    def gates_bwd(x, dbg, al, db):
        _, vjp = jax.vjp(functools.partial(_gates, h=h), x, al, db)
        gx, gal, gdb = vjp(dbg)
        return (gx,), (gal, gdb)

    (dba,), (d_alp, d_dtp) = _rowmap(gates_bwd, [(proj, LANES, 4 * h), dbg], [alp, dtp], [(LANES, MXU_DTYPE)],
                                     [(1, LANES), (1, LANES)], "dn_gates_bwd")
    dproj = jnp.concatenate([dpq, dpk, dpv, dz, dba], axis=1)
    d_w_in = _mm_tn(hin, dproj, MXU_DTYPE, "dn_dwin")
    dh = _mm_nt(dproj, w_in, F32, "dn_dh")
    d_conv_w = jnp.concatenate([dwq, dwk, dwv], axis=1)
    return dh, dict(w_in=d_w_in, w_out=d_w_out, conv_w=d_conv_w, a_log=d_alp[0, h:2 * h], dt_bias=d_dtp[0, h:2 * h],
                    norm_w=d_norm_w[0])


def _ln_silu(u, g, b):
    return jax.nn.silu(_ln(u, g, b))


def _conformer_fwd(hin, get_w_in, dw_w, dw_b, ln_g, ln_b, get_w_out):
    w_in = get_w_in(hin)
    vg = _mm_nn(hin, w_in, F32, "cf_vg")
    u1 = _cf_conv_fwd(vg, dw_w, dw_b)
    ch = u1.shape[1]

    def fn(u, g, b):
        return (_ln_silu(u, g, b),), ()

    (u2,), _ = _rowmap(fn, [u1], [ln_g, ln_b], [(ch, MXU_DTYPE)], [], "cf_ln")
    w_out = get_w_out(u2)
    y = _mm_nn(u2, w_out, F32, "cf_y")
    return y, (hin, vg, u1, u2, w_in, w_out)


def _conformer_bwd(res, dy, dw_w, ln_g, ln_b):
    hin, vg, u1, u2, w_in, w_out = res
    ch = u1.shape[1]
    d_w_out = _mm_tn(u2, dy, MXU_DTYPE, "cf_dwout")
    du2 = _mm_nt(dy, w_out, F32, "cf_du2")

    def fn(u, du2, g, b):
        _, vjp = jax.vjp(_ln_silu, u, g, b)
        gu, gg, gb = vjp(du2)
        return (gu,), (gg, gb)

    (du1,), (d_ln_g, d_ln_b) = _rowmap(fn, [u1, du2], [ln_g, ln_b], [(ch, F32)], [(1, ch), (1, ch)], "cf_ln_bwd")
    dval, dgate, d_dw_w, d_dw_b = _cf_conv_bwd(vg, dw_w, du1)
    dvg = jnp.concatenate([dval, dgate], axis=1)
    d_w_in = _mm_tn(hin, dvg, MXU_DTYPE, "cf_dwin", split_cols=True)
    dh = _mm_nt(dvg, w_in, F32, "cf_dh")
    return dh, dict(w_in=d_w_in, w_out=d_w_out, dw_w=d_dw_w, dw_b=d_dw_b[0], ln_g=d_ln_g[0], ln_b=d_ln_b[0])


def _mlp_fwd(hin, get_w1, get_w2):
    w1 = get_w1(hin)
    a, r = _mm_nn(hin, w1, MXU_DTYPE, "ff_a", relu2=True)
    w2 = get_w2(r)
    m = _mm_nn(r, w2, F32, "ff_m")
    return m, (hin, a, r, w1, w2)


def _mlp_bwd(res, dm):
    hin, a, r, w1, w2 = res
    d_w2 = _mm_tn(r, dm, MXU_DTYPE, "ff_dw2")
    da = _mm_nt(dm, w2, MXU_DTYPE, "ff_da", relu2_of=a)
    d_w1 = _mm_tn(hin, da, MXU_DTYPE, "ff_dw1", split_cols=True)
    dh = _mm_nt(da, w1, F32, "ff_dh")
    return dh, d_w1, d_w2


def _ada_fwd(c_all, ada_w):
    depth, d, nl = ada_w.shape
    tn = _tile(nl, 256)

    def body(c_ref, w_ref, o_ref, cond_ref):
        cond = jax.nn.silu(c_ref[...]).astype(MXU_DTYPE)
        cond_ref[...] = cond
        o_ref[...] = lax.dot_general(cond, w_ref[...].astype(MXU_DTYPE), (NN, ((), ())), preferred_element_type=F32)

    return _call(body, grid=(depth, nl // tn),
                 ins=[(c_all, c_all.shape, lambda l, j: (0, 0)), (ada_w, (None, d, tn), lambda l, j: (l, 0, j))],
                 outs=[((depth, N_DEV, nl), F32, (None, N_DEV, tn), lambda l, j: (l, 0, j)),
                       (c_all.shape, MXU_DTYPE, c_all.shape, lambda l, j: (0, 0))],
                 name="ada_fwd")


def _ada_bwd(cond_all, dmod_cols):
    depth, _, nl = dmod_cols.shape
    d = cond_all.shape[1]
    tn = _tile(nl, 256)

    def body(c_ref, g_ref, o_ref):
        o_ref[...] = lax.dot_general(c_ref[...], g_ref[...].astype(MXU_DTYPE), (TN, ((), ())),
                                     preferred_element_type=F32)

    return _call(body, grid=(depth, nl // tn),
                 ins=[(cond_all, cond_all.shape, lambda l, j: (0, 0)), (dmod_cols, (None, N_DEV, tn), lambda l, j: (l, 0, j))],
                 outs=[((depth, d, nl), F32, (None, d, tn), lambda l, j: (l, 0, j))], name="ada_bwd")[0]


def _peers():
    x, y, c = lax.axis_index("x"), lax.axis_index("y"), lax.axis_index("c")
    peers = []
    for k in range(1, N_DEV):
        px = 1 - x if k & 4 else x
        py = 1 - y if k & 2 else y
        pc = 1 - c if k & 1 else c
        peers.append(((px, py, pc), 4 * px + 2 * py + pc))
    return 4 * x + 2 * y + c, peers


_HBM = pl.BlockSpec(memory_space=pltpu.HBM)
_SEM = pl.BlockSpec(memory_space=pltpu.SEMAPHORE)
_ANY = pl.BlockSpec(memory_space=pl.ANY)
_EFFECT = pltpu.SideEffectType.DATAFLOW_SIDE_EFFECTING


def _xfer_start(srcs, lands, scatter, after, name):
    nt = len(srcs)

    def body(*refs):
        src, land = refs[:nt], refs[nt:2 * nt]
        sems = refs[2 * nt + 1:4 * nt + 1]
        token = refs[-1]
        me, peers = _peers()
        for t in range(nt):
            for k, (pid, plin) in enumerate(peers):
                pltpu.make_async_remote_copy(
                    src_ref=src[t].at[plin] if scatter else src[t], dst_ref=land[t].at[me],
                    send_sem=sems[2 * t].at[k], recv_sem=sems[2 * t + 1].at[k],
                    device_id=pid, device_id_type=pl.DeviceIdType.MESH).start()
        token[...] = jnp.zeros_like(token)

    out_shape = [pltpu.SemaphoreType.DMA((N_DEV - 1,)) for _ in range(2 * nt)]
    out_shape += [pltpu.HBM(a.shape, a.dtype) for a in list(srcs) + list(lands)]
    out_shape += [jax.ShapeDtypeStruct((8, LANES), F32)]
    res = pl.pallas_call(
        body, name=name, out_shape=out_shape,
        in_specs=[_HBM] * (2 * nt) + [_ANY],
        out_specs=[_SEM] * (2 * nt) + [_HBM] * (2 * nt) + [pl.BlockSpec(memory_space=pltpu.VMEM)],
        input_output_aliases={i: 2 * nt + i for i in range(2 * nt)},
        compiler_params=pltpu.CompilerParams(has_side_effects=_EFFECT),
    )(*[pltpu.with_memory_space_constraint(a, pltpu.HBM) for a in list(srcs) + list(lands)], after)
    sems, thru = res[:2 * nt], res[2 * nt:4 * nt]
    return [(sems[2 * t], sems[2 * t + 1], thru[t], thru[nt + t]) for t in range(nt)], res[-1]


def _xfer_wait(handle, scatter, after, name):
    send, recv, src, land = handle

    def body(src_ref, land_ref, send_sem, recv_sem, after_ref, src_dead, land_out):
        _, peers = _peers()
        for k, (pid, plin) in enumerate(peers):
            cp = pltpu.make_async_remote_copy(
                src_ref=src_ref.at[plin] if scatter else src_ref, dst_ref=land_ref.at[plin],
                send_sem=send_sem.at[k], recv_sem=recv_sem.at[k],
                device_id=pid, device_id_type=pl.DeviceIdType.MESH)
            cp.wait_send()
            cp.wait_recv()

    return pl.pallas_call(
        body, name=name, out_shape=(pltpu.HBM(src.shape, src.dtype), pltpu.HBM(land.shape, land.dtype)),
        in_specs=(_HBM, _HBM, _SEM, _SEM, _ANY), out_specs=(_HBM, _HBM), input_output_aliases={0: 0, 1: 1},
        compiler_params=pltpu.CompilerParams(has_side_effects=_EFFECT),
    )(src, land, send, recv, after)[1]


def _landing(x, me):
    return lax.dynamic_update_slice(lax.empty((N_DEV,) + x.shape, x.dtype), x[None], (me,) + (0,) * x.ndim)


def _landing_scatter(p, me):
    own = lax.dynamic_index_in_dim(p, me, axis=0, keepdims=True)
    return lax.dynamic_update_slice(lax.empty(p.shape, p.dtype), own, (me,) + (0,) * (p.ndim - 1))


def _exchange(arrs, scatter, name):
    nt = len(arrs)
    out_shape = [jax.ShapeDtypeStruct(a.shape if scatter else (N_DEV,) + a.shape, a.dtype) for a in arrs]

    def body(*refs):
        ins, outs = refs[:nt], refs[nt:2 * nt]
        send, recv, loc = refs[2 * nt:]
        me, peers = _peers()
        copies = []
        for t in range(nt):
            own = pltpu.make_async_copy(ins[t].at[me] if scatter else ins[t], outs[t].at[me], loc.at[t])
            own.start()
            copies.append(own)
            for k, (pid, plin) in enumerate(peers):
                cp = pltpu.make_async_remote_copy(
                    src_ref=ins[t].at[plin] if scatter else ins[t], dst_ref=outs[t].at[me],
                    send_sem=send.at[t, k], recv_sem=recv.at[t, k],
                    device_id=pid, device_id_type=pl.DeviceIdType.MESH)
                cp.start()
                copies.append(cp)
        for cp in copies:
            cp.wait()

    any_spec = pl.BlockSpec(memory_space=pl.ANY)
    return pl.pallas_call(
        body, out_shape=out_shape, in_specs=[any_spec] * nt, out_specs=[any_spec] * nt,
        scratch_shapes=[pltpu.SemaphoreType.DMA((nt, N_DEV - 1)), pltpu.SemaphoreType.DMA((nt, N_DEV - 1)),
                        pltpu.SemaphoreType.DMA((nt,))],
        name=name)(*arrs)


def _adamw_body(n_parts):
    def body(p_ref, w_ref, m_ref, v_ref, *rest):
        g_out, d_out, m_out, v_out = rest[-4:]
        g = p_ref[0].astype(F32)
        for i in range(1, n_parts):
            g = g + p_ref[i].astype(F32)
        m2 = ADAM_B1 * m_ref[...] + (1.0 - ADAM_B1) * g
        v2 = ADAM_B2 * v_ref[...] + (1.0 - ADAM_B2) * jnp.square(g)
        m_hat = m2 / (1.0 - ADAM_B1 ** ADAM_STEP)
        v_hat = v2 / (1.0 - ADAM_B2 ** ADAM_STEP)
        g_out[...] = g
        d_out[...] = -ADAM_LR * (m_hat / (jnp.sqrt(v_hat) + ADAM_EPS) + ADAM_WD * w_ref[...])
        m_out[...] = m2
        v_out[...] = v2

    return body


def _adamw_layer(parts, w, m, v, layer, prev, name):
    p, r, c = parts.shape
    tr = _tile(r, 256, 8)
    blk = pl.BlockSpec((None, tr, c), lambda i: (layer, i, 0))
    in_specs = [pl.BlockSpec((p, tr, c), lambda i: (0, i, 0)), blk, blk, blk]
    args = [parts, w, m, v]
    aliases = {}
    if prev is not None:
        in_specs += [_ANY] * 4
        args += list(prev)
        aliases = {4 + i: i for i in range(4)}
    return pl.pallas_call(
        _adamw_body(p), grid=(r // tr,), in_specs=in_specs, out_specs=[blk] * 4,
        out_shape=[jax.ShapeDtypeStruct(w.shape, F32)] * 4, input_output_aliases=aliases, name=name,
        compiler_params=_cparams(1))(*args)


def _adamw(parts, w, m, v, name):
    p, nl, r, c = parts.shape
    tr = _tile(r, 256, 8)
    body = _adamw_body(p)

    blk = (None, tr, c)
    imap = lambda l, i: (l, i, 0)
    out = ((nl, r, c), F32, blk, imap)
    return _call(body, grid=(nl, r // tr),
                 ins=[(parts, (p, None, tr, c), lambda l, i: (0, l, i, 0)), (w, blk, imap), (m, blk, imap), (v, blk, imap)],
                 outs=[out] * 4, name=name)


def _rows(x):
    return x.reshape(-1, LANES)


def _pad_rows(x, mult=8):
    r = x.shape[0]
    extra = (-r) % mult
    return jnp.pad(x, ((0, extra), (0, 0))) if extra else x


def _shard_cols(x, me, groups):
    lead = x.shape[:-1]
    xr = x.reshape(lead + (N_DEV, groups * LANES))
    xs = lax.dynamic_index_in_dim(xr, me, axis=len(lead), keepdims=False)
    return xs.reshape(N_DEV, -1, LANES)


def kernel(x, c, ada_w, ada_b, ln_g, ln_b, dn_w_in, dn_conv_w, dn_a_log, dn_dt_bias, dn_norm_w, dn_w_out, cf_w_in, cf_dw_w, cf_dw_b, cf_ln_g, cf_ln_b, cf_w_out, ff_w1, ff_w2, loss_target, m_ada_w, m_ada_b, m_ln_g, m_ln_b, m_dn_w_in, m_dn_conv_w, m_dn_a_log, m_dn_dt_bias, m_dn_norm_w, m_dn_w_out, m_cf_w_in, m_cf_dw_w, m_cf_dw_b, m_cf_ln_g, m_cf_ln_b, m_cf_w_out, m_ff_w1, m_ff_w2, v_ada_w, v_ada_b, v_ln_g, v_ln_b, v_dn_w_in, v_dn_conv_w, v_dn_a_log, v_dn_dt_bias, v_dn_norm_w, v_dn_w_out, v_cf_w_in, v_cf_dw_w, v_cf_dw_b, v_cf_ln_g, v_cf_ln_b, v_cf_w_out, v_ff_w1, v_ff_w2):
    depth, d, _ = ada_w.shape
    n_a, n_b = dn_w_in.shape[0], cf_w_in.shape[0]
    heads = dn_a_log.shape[1]
    hw = heads * HEAD_DIM
    taps = cf_dw_w.shape[1]
    s = x.shape[1]
    alpha = (2.0 * depth) ** 0.25
    me = 4 * lax.axis_index("x") + 2 * lax.axis_index("y") + lax.axis_index("c")
    xs, tgt = x[0], loss_target[0]

    dn_in_cols = dn_w_in.shape[2]
    keys, shards = [], []
    for i in range(depth):
        j = i // 2
        mixer = [("dn_in", dn_w_in), ("dn_out", dn_w_out)] if i % 2 == 0 else [("cf_in", cf_w_in), ("cf_out", cf_w_out)]
        for nm, wt in mixer:
            keys.append((nm, j))
            shards.append(wt[j].astype(MXU_DTYPE))
        keys += [("ff1", i), ("ff2", i)]
        shards += [ff_w1[i].astype(MXU_DTYPE), ff_w2[i].astype(MXU_DTYPE)]
    first, token0 = _xfer_start(shards[:1], [_landing(shards[0], me)], False, c, "gather_weights_start_0")

    small_local = [_rows(ln_g), _rows(ln_b), _rows(dn_conv_w), _rows(cf_dw_w), _rows(cf_dw_b), _rows(cf_ln_g),
                   _rows(cf_ln_b), _rows(c)]
    sizes = [a.shape[0] for a in small_local]
    packed = _pad_rows(jnp.concatenate(small_local, axis=0)) + token0[0, 0]
    (small_all,) = _exchange([packed], False, "comm_gather_params")
    offs = [0]
    for z in sizes:
        offs.append(offs[-1] + z)

    def small(i):
        return small_all[:, offs[i]:offs[i + 1], :]

    def unshard(piece, lead, groups):
        t = piece.reshape((N_DEV,) + lead + (groups * LANES,))
        t = jnp.moveaxis(t, 0, len(lead))
        return t.reshape(lead + (N_DEV * groups * LANES,))

    ln_g_f = unshard(small(0), (depth, 2), 1)
    ln_b_f = unshard(small(1), (depth, 2), 1)
    conv_w_f = unshard(small(2), (n_a, DN_CONV), 3 * heads // N_DEV)
    dw_w_f = unshard(small(3), (n_b, taps), 1)
    dw_b_f = unshard(small(4), (n_b,), 1)
    cf_ln_g_f = unshard(small(5), (n_b,), 1)
    cf_ln_b_f = unshard(small(6), (n_b,), 1)
    c_all = small(7).reshape(N_DEV, d)

    mod_part, cond_all = _ada_fwd(c_all, ada_w)
    (mod_all,) = _exchange([mod_part], False, "comm_gather_mod")
    mod_mine = lax.dynamic_index_in_dim(mod_all, me, axis=2, keepdims=False)
    mod_mine = jnp.moveaxis(mod_mine, 0, 1).reshape(depth, N_MOD * d)

    rest, token = _xfer_start(shards[1:], [_landing(a, me) for a in shards[1:]], False, mod_all, "gather_weights_start_1")
    handles = dict(zip(keys, first + rest))
    weights = {}

    def gathered(key, after):
        if key not in weights:
            weights[key] = _xfer_wait(handles[key], False, after, "gather_wait_%s_%d" % key)
        return weights[key]

    def get_dn_in(j):
        def get(after):
            g = gathered(("dn_in", j), after)
            w = jnp.moveaxis(g, 0, 1).reshape(d, N_DEV * dn_in_cols)
            return jnp.pad(w, ((0, 0), (0, 4 * hw + LANES - N_DEV * dn_in_cols)))
        return get

    def get_rows(key):
        return lambda after: gathered(key, after).reshape((-1, d))

    def get_cols(key):
        return lambda after: gathered(key, after)

    def add_bias(a, b):
        return (a + b,), ()

    (mod,), _ = _rowmap(add_bias, [mod_mine + token[0, 0], ada_b], [], [(N_MOD * d, F32)], [], "ada_bias")

    def mod_row(i, j):
        return mod[i:i + 1, j * d:(j + 1) * d]

    def ln_row(arr, i, j):
        return arr[i, j][None, :]

    subs = []
    h_cur = _modulate_fwd(xs, mod_row(0, 1), mod_row(0, 0))
    x_cur = xs
    last = None
    for i in range(depth):
        j = i // 2
        if i % 2 == 0:
            y, res = _deltanet_fwd(h_cur, get_dn_in(j), conv_w_f[j], dn_a_log[j], dn_dt_bias[j], dn_norm_w[j],
                                   get_rows(("dn_out", j)))
        else:
            y, res = _conformer_fwd(h_cur, get_cols(("cf_in", j)), dw_w_f[j], dw_b_f[j][None, :], cf_ln_g_f[j][None, :],
                                    cf_ln_b_f[j][None, :], get_rows(("cf_out", j)))
        p1 = (mod_row(i, 2), ln_row(ln_g_f, i, 0), ln_row(ln_b_f, i, 0), mod_row(i, 4), mod_row(i, 3))
        x_mid, h_mid = _combine_fwd(alpha, x_cur, y, *p1)
        subs.append((x_cur, y, p1, res))
        m_out, res2 = _mlp_fwd(h_mid, get_cols(("ff1", i)), get_rows(("ff2", i)))
        if i + 1 < depth:
            p2 = (mod_row(i, 5), ln_row(ln_g_f, i, 1), ln_row(ln_b_f, i, 1), mod_row(i + 1, 1), mod_row(i + 1, 0))
            x_next, h_next = _combine_fwd(alpha, x_mid, m_out, *p2)
            subs.append((x_mid, m_out, p2, res2))
            x_cur, h_cur = x_next, h_next
        else:
            p2 = (mod_row(i, 5), ln_row(ln_g_f, i, 1), ln_row(ln_b_f, i, 1))
            last = (x_mid, m_out, p2, res2)

    x_in, y_in, p_last, res_last = last
    dx, dy, (loss_acc, g_gt, g_g, g_b) = _last_fwd_bwd(alpha, x_in, y_in, tgt, *p_last)
    loss = lax.psum(loss_acc[0, 0], ("x", "y", "c"))

    d_mod = [[None] * N_MOD for _ in range(depth)]
    d_ln_g = [[None, None] for _ in range(depth)]
    d_ln_b = [[None, None] for _ in range(depth)]
    d_mod[depth - 1][5], d_ln_g[depth - 1][1], d_ln_b[depth - 1][1] = g_gt, g_g, g_b
    gw = dict(dn=[None] * n_a, cf=[None] * n_b)

    sent = {}

    def send_grads(named, tag):
        parts = [p for _, p in named]
        hs, tok = _xfer_start(parts, [_landing_scatter(p, me) for p in parts], True, parts[0], "scatter_start_" + tag)
        for (key, _), hnd in zip(named, hs):
            sent[key] = hnd
        return tok[0, 0]

    def by_rows(g):
        return g.reshape((N_DEV, g.shape[0] // N_DEV, g.shape[1]))

    def send_mlp(i, d_w1, d_w2):
        return send_grads([(("ff1", i), d_w1), (("ff2", i), by_rows(d_w2))], "ff_%d" % i)

    dh, d_w1, d_w2 = _mlp_bwd(res_last, dy)
    pin = send_mlp(depth - 1, d_w1, d_w2)
    for idx in range(len(subs) - 1, -1, -1):
        x_in, y_in, prm, res = subs[idx]
        i, second = idx // 2, idx % 2
        prm = (prm[0] + pin,) + tuple(prm[1:])
        dx, dy, (g_gt, g_g, g_b, g_sc, g_sh) = _combine_bwd(alpha, x_in, y_in, dx, dh, *prm)
        d_mod[i][5 if second else 2], d_ln_g[i][second], d_ln_b[i][second] = g_gt, g_g, g_b
        nxt_i, nxt_base = (i + 1, 0) if second else (i, 3)
        d_mod[nxt_i][nxt_base + 1], d_mod[nxt_i][nxt_base] = g_sc, g_sh
        j = i // 2
        if second:
            dh, d_w1, d_w2 = _mlp_bwd(res, dy)
            pin = send_mlp(i, d_w1, d_w2)
        elif i % 2 == 0:
            dh, gw["dn"][j] = _deltanet_bwd(res, dy, conv_w_f[j])
            d_in = gw["dn"][j]["w_in"][:, :N_DEV * dn_in_cols].reshape(d, N_DEV, dn_in_cols)
            pin = send_grads([(("dn_in", j), jnp.moveaxis(d_in, 1, 0)), (("dn_out", j), by_rows(gw["dn"][j]["w_out"]))],
                             "dn_%d" % j)
        else:
            dh, gw["cf"][j] = _conformer_bwd(res, dy, dw_w_f[j], cf_ln_g_f[j][None, :], cf_ln_b_f[j][None, :])
            pin = send_grads([(("cf_in", j), gw["cf"][j]["w_in"]), (("cf_out", j), by_rows(gw["cf"][j]["w_out"]))],
                             "cf_%d" % j)
    grad_x, g_sc, g_sh = _modulate_bwd(xs, dx, dh, mod_row(0, 1) + pin, mod_row(0, 0))
    d_mod[0][1], d_mod[0][0] = g_sc, g_sh
    d_mod_full = jnp.concatenate([jnp.concatenate(r, axis=1) for r in d_mod], axis=0)

    stacked = {"dn_w_in": ("dn_in", dn_w_in, m_dn_w_in, v_dn_w_in), "dn_w_out": ("dn_out", dn_w_out, m_dn_w_out, v_dn_w_out),
               "cf_w_in": ("cf_in", cf_w_in, m_cf_w_in, v_cf_w_in), "cf_w_out": ("cf_out", cf_w_out, m_cf_w_out, v_cf_w_out),
               "ff_w1": ("ff1", ff_w1, m_ff_w1, v_ff_w1), "ff_w2": ("ff2", ff_w2, m_ff_w2, v_ff_w2)}
    chains = {key: None for key in stacked}

    def update_layer(i):
        mixer = ["dn_w_in", "dn_w_out"] if i % 2 == 0 else ["cf_w_in", "cf_w_out"]
        for key, idx in [("ff_w1", i), ("ff_w2", i)] + [(k, i // 2) for k in mixer]:
            short, w, m, v = stacked[key]
            parts = _xfer_wait(sent[(short, idx)], True, sg_token, "scatter_wait_%s_%d" % (short, idx))
            chains[key] = _adamw_layer(parts, w, m, v, idx, chains[key], "adamw_%s_%d" % (key, idx))

    def stack_rows(lst):
        return jnp.stack(lst, axis=0)

    gs_ln_g = jnp.stack([jnp.concatenate(r, axis=0) for r in d_ln_g], axis=0)
    gs_ln_b = jnp.stack([jnp.concatenate(r, axis=0) for r in d_ln_b], axis=0)
    gs_conv_w = stack_rows([gw["dn"][j]["conv_w"] for j in range(n_a)])
    gs_dw_w = stack_rows([gw["cf"][j]["dw_w"] for j in range(n_b)])
    gs_dw_b = stack_rows([gw["cf"][j]["dw_b"] for j in range(n_b)])
    gs_cf_ln_g = stack_rows([gw["cf"][j]["ln_g"] for j in range(n_b)])
    gs_cf_ln_b = stack_rows([gw["cf"][j]["ln_b"] for j in range(n_b)])
    gs_a_log = stack_rows([_pad_lanes(gw["dn"][j]["a_log"], 0)[0] for j in range(n_a)])
    gs_dt_bias = stack_rows([_pad_lanes(gw["dn"][j]["dt_bias"], 0)[0] for j in range(n_a)])
    gs_norm_w = stack_rows([gw["dn"][j]["norm_w"] for j in range(n_a)])
    small_grads = [gs_ln_g, gs_ln_b, gs_conv_w, gs_dw_w, gs_dw_b, gs_cf_ln_g, gs_cf_ln_b, gs_a_log, gs_dt_bias,
                   gs_norm_w, d_mod_full]
    sg_rows = [_rows(a) for a in small_grads]
    sg_sizes = [a.shape[0] for a in sg_rows]
    sg_packed = _pad_rows(jnp.concatenate(sg_rows, axis=0))
    (sg_handle,), sg_token = _xfer_start([sg_packed], [_landing(sg_packed, me)], False, grad_x, "gather_small_grads_start")
    for i in range(depth - 1, 0, -1):
        update_layer(i)
    updated = chains["ff_w1"]
    sg_all = _xfer_wait(sg_handle, False, grad_x if updated is None else updated[0], "gather_small_grads_wait")
    sg_offs = [0]
    for z in sg_sizes:
        sg_offs.append(sg_offs[-1] + z)

    def sg(i, shape):
        return sg_all[:, sg_offs[i]:sg_offs[i + 1], :].reshape((N_DEV,) + shape)

    dmod_all = sg(10, (depth, N_MOD * d))
    nl = ada_w.shape[2]
    dmod_cols = lax.dynamic_slice_in_dim(dmod_all, me * nl, nl, axis=2)
    g_ada_w = _ada_bwd(cond_all, jnp.moveaxis(dmod_cols, 0, 1))

    outs = {}

    def run_adamw(key, parts, w, m, v):
        shp = w.shape
        as3 = lambda t: t.reshape((-1,) + shp[-2:]) if t.ndim >= 3 else t.reshape((1,) + shp)
        parts3 = parts.reshape((parts.shape[0],) + as3(w).shape)
        res = _adamw(parts3, as3(w), as3(m), as3(v), "adamw_" + key)
        outs[key] = tuple(r.reshape(shp) for r in res)

    run_adamw("ada_w", g_ada_w[None], ada_w, m_ada_w, v_ada_w)

    cgroups = 3 * heads // N_DEV
    shard_parts = [
        _shard_cols(sg(0, (depth, 2, d)), me, 1), _shard_cols(sg(1, (depth, 2, d)), me, 1),
        _shard_cols(sg(2, (n_a, DN_CONV, 3 * hw)), me, cgroups), _shard_cols(sg(3, (n_b, taps, d)), me, 1),
        _shard_cols(sg(4, (n_b, d)), me, 1), _shard_cols(sg(5, (n_b, d)), me, 1), _shard_cols(sg(6, (n_b, d)), me, 1),
    ]
    repl_parts = [sg(7, (n_a, LANES)), sg(8, (n_a, LANES)), sg(9, (n_a, HEAD_DIM)),
                  sg(10, (depth, N_MOD * d)).reshape(N_DEV, -1, LANES)]
    small_parts = shard_parts + repl_parts
    sp_sizes = [a.shape[1] for a in small_parts]
    parts_packed = jnp.concatenate(small_parts, axis=1)
    extra = (-parts_packed.shape[1]) % 8
    parts_packed = jnp.pad(parts_packed, ((0, 0), (0, extra), (0, 0)))

    def pad_heads(t):
        return jnp.pad(t, ((0, 0), (0, LANES - heads)))

    def pack_state(ln_g_, ln_b_, conv_w_, dw_w_, dw_b_, cln_g_, cln_b_, a_log_, dt_b_, norm_w_, ada_b_):
        rows = [_rows(ln_g_), _rows(ln_b_), _rows(conv_w_), _rows(dw_w_), _rows(dw_b_), _rows(cln_g_), _rows(cln_b_),
                pad_heads(a_log_), pad_heads(dt_b_), norm_w_, _rows(ada_b_)]
        return _pad_rows(jnp.concatenate(rows, axis=0))

    w_s = pack_state(ln_g, ln_b, dn_conv_w, cf_dw_w, cf_dw_b, cf_ln_g, cf_ln_b, dn_a_log, dn_dt_bias, dn_norm_w, ada_b)
    m_s = pack_state(m_ln_g, m_ln_b, m_dn_conv_w, m_cf_dw_w, m_cf_dw_b, m_cf_ln_g, m_cf_ln_b, m_dn_a_log,
                     m_dn_dt_bias, m_dn_norm_w, m_ada_b)
    v_s = pack_state(v_ln_g, v_ln_b, v_dn_conv_w, v_cf_dw_w, v_cf_dw_b, v_cf_ln_g, v_cf_ln_b, v_dn_a_log,
                     v_dn_dt_bias, v_dn_norm_w, v_ada_b)
    res_s = _adamw(parts_packed[:, None], w_s[None], m_s[None], v_s[None], "adamw_small")
    sp_offs = [0]
    for z in sp_sizes:
        sp_offs.append(sp_offs[-1] + z)
    small_keys = ["ln_g", "ln_b", "dn_conv_w", "cf_dw_w", "cf_dw_b", "cf_ln_g", "cf_ln_b", "dn_a_log", "dn_dt_bias",
                  "dn_norm_w", "ada_b"]
    small_shapes = [ln_g.shape, ln_b.shape, dn_conv_w.shape, cf_dw_w.shape, cf_dw_b.shape, cf_ln_g.shape,
                    cf_ln_b.shape, dn_a_log.shape, dn_dt_bias.shape, dn_norm_w.shape, ada_b.shape]
    for n, (key, shp) in enumerate(zip(small_keys, small_shapes)):
        vals = []
        for r in res_s:
            piece = r[0, sp_offs[n]:sp_offs[n + 1], :]
            if key in ("dn_a_log", "dn_dt_bias"):
                piece = piece[:, :heads]
            vals.append(piece.reshape(shp))
        outs[key] = tuple(vals)

    update_layer(0)
    for key in stacked:
        outs[key] = tuple(chains[key])

    order = ["ada_w", "ada_b", "ln_g", "ln_b", "dn_w_in", "dn_conv_w", "dn_a_log", "dn_dt_bias", "dn_norm_w",
             "dn_w_out", "cf_w_in", "cf_dw_w", "cf_dw_b", "cf_ln_g", "cf_ln_b", "cf_w_out", "ff_w1", "ff_w2"]
    result = [loss, grad_x[None]]
    for part in range(4):
        result += [outs[k][part] for k in order]
    return tuple(result)
```

```python
import functools

import jax
import jax.numpy as jnp
from jax import lax
from jax.experimental import pallas as pl
from jax.experimental.pallas import tpu as pltpu

F32 = jnp.float32
MXU_DTYPE = jnp.bfloat16
N_DEV = 8
LANES = 128
HEAD_DIM = 128
CHUNK = 64
DN_CONV = 4
N_MOD = 6
LN_EPS = 1e-5
RMS_EPS = 1e-6
L2_EPS = 1e-6
ADAM_LR = 0.001
ADAM_B1 = 0.9
ADAM_B2 = 0.999
ADAM_EPS = 1e-08
ADAM_WD = 0.01
ADAM_STEP = 10

HI = lax.Precision.HIGHEST
NN = ((1,), (0,))
NT = ((1,), (1,))
TN = ((0,), (0,))

ROW_TILE = 512
CONV_TILE = 256
SHORT_CONV_TILE = 1024


def _mdot(a, b, dims=NN):
    return lax.dot_general(a.astype(MXU_DTYPE), b.astype(MXU_DTYPE), (dims, ((), ())), preferred_element_type=F32)


def _split3(x):
    hi = x.astype(MXU_DTYPE)
    r1 = x - hi.astype(F32)
    mid = r1.astype(MXU_DTYPE)
    lo = (r1 - mid.astype(F32)).astype(MXU_DTYPE)
    return hi, mid, lo


def _dot01(a, b, dims=NN, mask_first=True):
    d = lambda p, q: lax.dot_general(p, q, (dims, ((), ())), preferred_element_type=F32)
    if mask_first:
        m = a.astype(MXU_DTYPE)
        return sum(d(m, p) for p in _split3(b))
    m = b.astype(MXU_DTYPE)
    return sum(d(p, m) for p in _split3(a))


def _dot3(a, b, dims=NN):
    ah, am, _ = _split3(a)
    bh, bm, _ = _split3(b)
    d = lambda p, q: lax.dot_general(p, q, (dims, ((), ())), preferred_element_type=F32)
    return d(ah, bh) + (d(ah, bm) + d(am, bh))


def _cparams(n):
    return pltpu.CompilerParams(dimension_semantics=("arbitrary",) * n)


def _call(body, *, grid, ins, outs, name, scratch=()):
    res = pl.pallas_call(
        body,
        grid=grid,
        in_specs=[pl.BlockSpec(b, m) for _, b, m in ins],
        out_specs=[pl.BlockSpec(b, m) for _, _, b, m in outs],
        out_shape=[jax.ShapeDtypeStruct(s, d) for s, d, _, _ in outs],
        scratch_shapes=list(scratch),
        name=name,
        compiler_params=_cparams(len(grid)),
    )(*[a for a, _, _ in ins])
    return res


def _tile(n, pref, unit=LANES):
    if n <= pref:
        return n
    t = (pref // unit) * unit
    while t > unit and n % t:
        t -= unit
    assert n % t == 0, (n, pref)
    return t


def _rowmap(fn, rows, consts, row_outs, acc_outs, name):
    rows = [r if isinstance(r, tuple) else (r, r.shape[1], 0) for r in rows]
    s = rows[0][0].shape[0]
    tm = min(ROW_TILE, s)
    nr, nc, no, na = len(rows), len(consts), len(row_outs), len(acc_outs)

    def body(*refs):
        rin, cin = refs[:nr], refs[nr:nr + nc]
        rout, aout = refs[nr + nc:nr + nc + no], refs[nr + nc + no:]
        ro, ao = fn(*[r[...] for r in rin], *[c[...] for c in cin])
        for ref, val in zip(rout, ro):
            ref[...] = val.astype(ref.dtype)
        if na:
            first = pl.program_id(0) == 0

            @pl.when(first)
            def _():
                for ref, val in zip(aout, ao):
                    ref[...] = val

            @pl.when(jnp.logical_not(first))
            def _():
                for ref, val in zip(aout, ao):
                    ref[...] += val

    ins = [(a, (tm, w), functools.partial(lambda i, cb: (i, cb), cb=cb)) for a, w, cb in rows]
    ins += [(c, c.shape, lambda i: (0, 0)) for c in consts]
    outs = [((s, w), d, (tm, w), lambda i: (i, 0)) for w, d in row_outs]
    outs += [(shp, F32, shp, lambda i: (0, 0)) for shp in acc_outs]
    res = _call(body, grid=(s // tm,), ins=ins, outs=outs, name=name)
    return res[:no], res[no:]


def _ln(z, g, b):
    mu = jnp.mean(z, -1, keepdims=True)
    var = jnp.mean(jnp.square(z - mu), -1, keepdims=True)
    return (z - mu) * lax.rsqrt(var + LN_EPS) * g + b


def _combine(alpha, x, y, gt, g, b, sc, sh):
    xn = _ln(alpha * x + (1.0 + gt) * y, g, b)
    return xn, xn * (1.0 + sc) + sh


def _modulate_fwd(x, sc, sh):
    def fn(x, sc, sh):
        return ((x * (1.0 + sc) + sh),), ()

    (h,), _ = _rowmap(fn, [x], [sc, sh], [(x.shape[1], MXU_DTYPE)], [], "modulate_fwd")
    return h


def _modulate_bwd(x, dx, dh, sc, sh):
    d = x.shape[1]

    def fn(x, dx, dh, sc, sh):
        _, vjp = jax.vjp(lambda x, sc, sh: x * (1.0 + sc) + sh, x, sc, sh)
        gx, gsc, gsh = vjp(dh)
        return (dx + gx,), (gsc, gsh)

    (gx,), (gsc, gsh) = _rowmap(fn, [x, dx, dh], [sc, sh], [(d, F32)], [(1, d), (1, d)], "modulate_bwd")
    return gx, gsc, gsh


def _combine_fwd(alpha, x, y, gt, g, b, sc, sh):
    d = x.shape[1]

    def fn(x, y, gt, g, b, sc, sh):
        return _combine(alpha, x, y, gt, g, b, sc, sh), ()

    (xn, h), _ = _rowmap(fn, [x, y], [gt, g, b, sc, sh], [(d, F32), (d, MXU_DTYPE)], [], "combine_fwd")
    return xn, h


def _combine_bwd(alpha, x, y, dxn, dh, gt, g, b, sc, sh):
    d = x.shape[1]

    def fn(x, y, dxn, dh, gt, g, b, sc, sh):
        _, vjp = jax.vjp(functools.partial(_combine, alpha), x, y, gt, g, b, sc, sh)
        gx, gy, ggt, gg, gb, gsc, gsh = vjp((dxn, dh))
        return (gx, gy), (ggt, gg, gb, gsc, gsh)

    (gx, gy), accs = _rowmap(fn, [x, y, dxn, dh], [gt, g, b, sc, sh], [(d, F32), (d, MXU_DTYPE)],
                             [(1, d)] * 5, "combine_bwd")
    return gx, gy, accs


def _last_fwd_bwd(alpha, x, y, tgt, gt, g, b):
    d = x.shape[1]

    def fn(x, y, tgt, gt, g, b):
        xn, vjp = jax.vjp(lambda x, y, gt, g, b: _ln(alpha * x + (1.0 + gt) * y, g, b), x, y, gt, g, b)
        err = xn - tgt
        gx, gy, ggt, gg, gb = vjp(err * (1.0 / d))
        rows = jnp.sum(jnp.square(err), axis=-1, keepdims=True)
        loss = (0.5 / d) * jnp.sum(rows, axis=0, keepdims=True) * jnp.ones((1, LANES), F32)
        return (gx, gy), (loss, ggt, gg, gb)

    (gx, gy), accs = _rowmap(fn, [x, y, tgt], [gt, g, b], [(d, F32), (d, MXU_DTYPE)],
                             [(1, LANES), (1, d), (1, d), (1, d)], "last_fwd_bwd")
    return gx, gy, accs


MM_VMEM_BUDGET = 40 * 2 ** 20


def _fit(options, cost):
    for o in options:
        if 2 * cost(o) <= MM_VMEM_BUDGET:
            return o
    return options[-1]


def _row_tiles(m):
    return [t for t in (2048, 1024, 512, 256) if t <= m and m % t == 0] or [m]


def _mm_call(a, a_blk, a_map, b, b_blk, b_map, outs, dims, grid, name, epi=None, extra=None, split=None):
    nk = grid[2]
    n_out = len(outs)
    n_in = 3 if extra is not None else 2

    def body(*refs):
        a_ref, b_ref = refs[0], refs[1]
        rest = refs[n_in:]
        out_refs = rest[:n_out]

        def finish(val):
            if epi == "relu2":
                out_refs[0][...] = val.astype(out_refs[0].dtype)
                out_refs[1][...] = jnp.square(jnp.maximum(val, 0.0)).astype(out_refs[1].dtype)
            elif epi == "relu2_bwd":
                out_refs[0][...] = (val * 2.0 * jnp.maximum(refs[2][...], 0.0)).astype(out_refs[0].dtype)
            elif split is not None:
                for g in range(split[0]):
                    out_refs[0][g] = val[:, g * split[1]:(g + 1) * split[1]].astype(out_refs[0].dtype)
            else:
                out_refs[0][...] = val.astype(out_refs[0].dtype)

        p = lax.dot_general(a_ref[...], b_ref[...], (dims, ((), ())), preferred_element_type=F32)
        if nk == 1:
            finish(p)
        else:
            acc = rest[n_out]
            k = pl.program_id(2)

            @pl.when(k == 0)
            def _():
                acc[...] = p

            @pl.when(k > 0)
            def _():
                acc[...] += p

            @pl.when(k == nk - 1)
            def _():
                finish(acc[...])

    if nk > 1:
        out_blk = tuple(x for x in outs[0][2] if x is not None)
        if split is not None:
            out_blk = (out_blk[1], split[0] * split[1])
        scratch = [pltpu.VMEM(out_blk, F32)]
    else:
        scratch = []
    ins = [(a, a_blk, a_map), (b, b_blk, b_map)] + ([extra] if extra is not None else [])
    return _call(body, grid=grid, ins=ins, outs=outs, name=name, scratch=scratch)


def _isz(dt):
    return jnp.dtype(dt).itemsize


def _mm_nn(a, b, out_dtype, name, relu2=False):
    m, kdim = a.shape
    if b.ndim == 2:
        n = b.shape[1]
        tn = _tile(n, 1536 if n > 2048 else 512)
        b_blk, b_map = (kdim, tn), lambda i, j, k: (0, j)
    else:
        g, _, ng = b.shape
        n = g * ng
        tn = _tile(ng, 512)
        b_blk = (None, kdim, tn)
        b_map = functools.partial(lambda i, j, k, npg: (j // npg, 0, j % npg), npg=ng // tn)
    out_bytes = (4 + _isz(out_dtype)) if relu2 else _isz(out_dtype)
    tm = _fit(_row_tiles(m), lambda t: t * kdim * _isz(a.dtype) + kdim * tn * _isz(b.dtype) + t * tn * out_bytes)
    grid = (m // tm, n // tn, 1)
    outs = [((m, n), F32 if relu2 else out_dtype, (tm, tn), lambda i, j, k: (i, j))]
    if relu2:
        outs.append(((m, n), out_dtype, (tm, tn), lambda i, j, k: (i, j)))
    res = _mm_call(a, (tm, kdim), lambda i, j, k: (i, 0), b, b_blk, b_map, outs, NN, grid, name,
                   epi="relu2" if relu2 else None)
    return res if relu2 else res[0]


def _mm_nt(a, b, out_dtype, name, relu2_of=None):
    m, n = a.shape
    extra_bytes = 4 if relu2_of is not None else 0
    if b.ndim == 2:
        kout = b.shape[0]
        to, tc, nk = _tile(kout, 512), n, 1
        b_blk, b_map = (to, tc), lambda i, j, k: (j, 0)
        acc_bytes = 0
    else:
        nk, kout, tc = b.shape
        to = _tile(kout, 1024)
        b_blk, b_map = (None, to, tc), lambda i, j, k: (k, j, 0)
        acc_bytes = 2
    tm = _fit(_row_tiles(m), lambda t: t * tc * _isz(a.dtype) + to * tc * _isz(b.dtype)
              + t * to * (_isz(out_dtype) + extra_bytes + acc_bytes))
    grid = (m // tm, kout // to, nk)
    outs = [((m, kout), out_dtype, (tm, to), lambda i, j, k: (i, j))]
    extra = (relu2_of, (tm, to), lambda i, j, k: (i, j)) if relu2_of is not None else None
    return _mm_call(a, (tm, tc), lambda i, j, k: (i, k), b, b_blk, b_map, outs, NT, grid, name,
                    epi="relu2_bwd" if relu2_of is not None else None, extra=extra)[0]


def _mm_tn(a, b, out_dtype, name, split_cols=False):
    m, kdim = a.shape
    n = b.shape[1]
    tk = _tile(kdim, 512)
    tn = _tile(n, 1536)
    if not split_cols:
        out, split = ((kdim, n), out_dtype, (tk, tn), lambda i, j, k: (i, j)), None
    else:
        ng = n // N_DEV
        if tn % ng:
            tn = _tile(ng, 512)
        if tn >= ng:
            gb = tn // ng
            out = ((N_DEV, kdim, ng), out_dtype, (gb, tk, ng), lambda i, j, k: (j, i, 0))
            split = (gb, ng)
        else:
            out = ((N_DEV, kdim, ng), out_dtype, (None, tk, tn),
                   functools.partial(lambda i, j, k, npg: (j // npg, i, j % npg), npg=ng // tn))
            split = None
    grid = (kdim // tk, n // tn, 1)
    return _mm_call(a, (m, tk), lambda i, j, k: (0, i), b, (m, tn), lambda i, j, k: (0, j), [out], TN, grid, name,
                    split=split)[0]


def _shifted(xa, off, rows):
    if off % 8 == 0:
        return xa[off:off + rows]
    return pltpu.roll(xa, xa.shape[0] - off, 0)[:rows]


def _conv_pad(taps):
    return -(-(taps - 1) // 8) * 8


def _conv_tile(xp_ref, w, i, rows, taps):
    pad = _conv_pad(taps)
    r0 = pl.multiple_of(i * rows, rows)
    xa = xp_ref[pl.ds(r0, rows + pad), :]
    views = [_shifted(xa, pad - (taps - 1) + j, rows) for j in range(taps)]
    acc = w[0:1, :] * views[0]
    for j in range(1, taps):
        acc = acc + w[j:j + 1, :] * views[j]
    return r0, acc, views


def _conv_back_tile(yp_ref, w, i, rows, taps):
    pad = _conv_pad(taps)
    r0 = pl.multiple_of(i * rows, rows)
    ya = yp_ref[pl.ds(r0, rows + pad), :]
    acc = w[taps - 1:taps, :] * ya[:rows]
    for j in range(taps - 1):
        acc = acc + w[j:j + 1, :] * _shifted(ya, taps - 1 - j, rows)
    return r0, acc


def _tap_sums(dy, views, taps):
    row = lax.broadcasted_iota(jnp.int32, (taps, LANES), 0)
    acc = jnp.zeros((taps, LANES), F32)
    for j in range(taps):
        acc = acc + jnp.where(row == j, jnp.sum(dy * views[j], axis=0, keepdims=True), 0.0)
    return acc


def _silu_l2(xc, l2):
    a = jax.nn.silu(xc)
    if l2:
        a = a * lax.rsqrt(jnp.sum(a * a, axis=-1, keepdims=True) + L2_EPS)
    return a


def _dn_conv_fwd(proj, conv_w, c0, nblk, l2, name):
    s = proj.shape[0]
    pad = _conv_pad(DN_CONV)
    rows = min(SHORT_CONV_TILE, s)

    def body(x_ref, w_ref, o_ref, xp):
        xp[0:pad, :] = jnp.zeros((pad, LANES), F32)
        xp[pad:, :] = x_ref[...]
        w = w_ref[...]

        def tile(i, c):
            r0, acc, _ = _conv_tile(xp, w, i, rows, DN_CONV)
            o_ref[pl.ds(r0, rows), :] = _silu_l2(acc, l2)
            return c

        lax.fori_loop(0, s // rows, tile, 0)

    return _call(body, grid=(nblk,),
                 ins=[(proj, (s, LANES), lambda c: (0, c0 + c)), (conv_w, (DN_CONV, LANES), lambda c: (0, c0 + c))],
                 outs=[((nblk, s, LANES), F32, (None, s, LANES), lambda c: (c, 0, 0))],
                 name=name, scratch=[pltpu.VMEM((s + pad, LANES), F32)])[0]


def _dn_conv_bwd(proj, conv_w, da, c0, nblk, l2, name):
    s = proj.shape[0]
    pad = _conv_pad(DN_CONV)
    rows = min(SHORT_CONV_TILE, s)

    def body(x_ref, w_ref, da_ref, dx_ref, dw_ref, xp, yp):
        xp[0:pad, :] = jnp.zeros((pad, LANES), F32)
        xp[pad:, :] = x_ref[...]
        yp[s:, :] = jnp.zeros((pad, LANES), F32)
        w = w_ref[...]

        def tile(i, dw):
            r0, acc, views = _conv_tile(xp, w, i, rows, DN_CONV)
            _, vjp = jax.vjp(functools.partial(_silu_l2, l2=l2), acc)
            (dxc,) = vjp(da_ref[pl.ds(r0, rows), :])
            yp[pl.ds(r0, rows), :] = dxc
            return dw + _tap_sums(dxc, views, DN_CONV)

        dw_ref[...] = lax.fori_loop(0, s // rows, tile, jnp.zeros((DN_CONV, LANES), F32))

        def tile2(i, c):
            r0, acc = _conv_back_tile(yp, w, i, rows, DN_CONV)
            dx_ref[pl.ds(r0, rows), :] = acc.astype(dx_ref.dtype)
            return c

        lax.fori_loop(0, s // rows, tile2, 0)

    return _call(body, grid=(nblk,),
                 ins=[(proj, (s, LANES), lambda c: (0, c0 + c)), (conv_w, (DN_CONV, LANES), lambda c: (0, c0 + c)),
                      (da, (None, s, LANES), lambda c: (c, 0, 0))],
                 outs=[((s, nblk * LANES), MXU_DTYPE, (s, LANES), lambda c: (0, c)),
                       ((DN_CONV, nblk * LANES), F32, (DN_CONV, LANES), lambda c: (0, c))],
                 name=name, scratch=[pltpu.VMEM((s + pad, LANES), F32), pltpu.VMEM((s + pad, LANES), F32)])


def _cf_conv_fwd(vg, dw_w, dw_b):
    s, c2 = vg.shape
    ch = c2 // 2
    nblk = ch // LANES
    taps = dw_w.shape[0]
    pad = _conv_pad(taps)
    rows = min(CONV_TILE, s)

    def body(v_ref, g_ref, w_ref, b_ref, o_ref, xp):
        xp[0:pad, :] = jnp.zeros((pad, LANES), F32)
        xp[pad:, :] = v_ref[...] * jax.nn.sigmoid(g_ref[...])
        w = w_ref[...]
        bias = b_ref[...]

        def tile(i, c):
            r0, acc, _ = _conv_tile(xp, w, i, rows, taps)
            o_ref[pl.ds(r0, rows), :] = acc + bias
            return c

        lax.fori_loop(0, s // rows, tile, 0)

    return _call(body, grid=(nblk,),
                 ins=[(vg, (s, LANES), lambda c: (0, c)), (vg, (s, LANES), lambda c: (0, nblk + c)),
                      (dw_w, (taps, LANES), lambda c: (0, c)), (dw_b, (1, LANES), lambda c: (0, c))],
                 outs=[((s, ch), F32, (s, LANES), lambda c: (0, c))],
                 name="cf_conv_fwd", scratch=[pltpu.VMEM((s + pad, LANES), F32)])[0]


def _cf_conv_bwd(vg, dw_w, du):
    s, c2 = vg.shape
    ch = c2 // 2
    nblk = ch // LANES
    taps = dw_w.shape[0]
    pad = _conv_pad(taps)
    rows = min(CONV_TILE, s)

    def body(v_ref, g_ref, w_ref, du_ref, dv_ref, dg_ref, dw_ref, db_ref, xp, yp):
        sig = jax.nn.sigmoid(g_ref[...])
        xp[0:pad, :] = jnp.zeros((pad, LANES), F32)
        xp[pad:, :] = v_ref[...] * sig
        yp[0:s, :] = du_ref[...]
        yp[s:, :] = jnp.zeros((pad, LANES), F32)
        w = w_ref[...]
        db_ref[...] = jnp.sum(du_ref[...], axis=0, keepdims=True)

        def tile(i, dw):
            r0, _, views = _conv_tile(xp, w, i, rows, taps)
            return dw + _tap_sums(du_ref[pl.ds(r0, rows), :], views, taps)

        dw_ref[...] = lax.fori_loop(0, s // rows, tile, jnp.zeros((taps, LANES), F32))

        def tile2(i, c):
            r0, du0 = _conv_back_tile(yp, w, i, rows, taps)
            val = v_ref[pl.ds(r0, rows), :]
            sg = jax.nn.sigmoid(g_ref[pl.ds(r0, rows), :])
            dv_ref[pl.ds(r0, rows), :] = (du0 * sg).astype(dv_ref.dtype)
            dg_ref[pl.ds(r0, rows), :] = (du0 * val * sg * (1.0 - sg)).astype(dg_ref.dtype)
            return c

        lax.fori_loop(0, s // rows, tile2, 0)

    return _call(body, grid=(nblk,),
                 ins=[(vg, (s, LANES), lambda c: (0, c)), (vg, (s, LANES), lambda c: (0, nblk + c)),
                      (dw_w, (taps, LANES), lambda c: (0, c)), (du, (s, LANES), lambda c: (0, c))],
                 outs=[((s, ch), MXU_DTYPE, (s, LANES), lambda c: (0, c)),
                       ((s, ch), MXU_DTYPE, (s, LANES), lambda c: (0, c)),
                       ((taps, ch), F32, (taps, LANES), lambda c: (0, c)),
                       ((1, ch), F32, (1, LANES), lambda c: (0, c))],
                 name="cf_conv_bwd", scratch=[pltpu.VMEM((s + pad, LANES), F32), pltpu.VMEM((s + pad, LANES), F32)])


def _masks():
    r = lax.broadcasted_iota(jnp.int32, (CHUNK, CHUNK), 0)
    c = lax.broadcasted_iota(jnp.int32, (CHUNK, CHUNK), 1)
    return r >= c, r > c, r <= c


def _chunk_decay(g):
    causal, _, upper = _masks()
    gb = jnp.broadcast_to(g, (CHUNK, CHUNK))
    gam_r = _dot01(jnp.where(causal, 1.0, 0.0), gb)
    gam_s = _dot01(jnp.ones((CHUNK, CHUNK), F32), jnp.where(upper, gb, 0.0))
    dm = jnp.where(causal, jnp.exp(jnp.where(causal, gam_r - gam_s, 0.0)), 0.0)
    return gam_r[:, 0:1], dm


def _chunk_scores(q, k, beta, dm):
    _, strict, _ = _masks()
    both = _mdot(jnp.concatenate([k * beta, q * (HEAD_DIM ** -0.5)], axis=0), k, NT)
    return jnp.where(strict, both[:CHUNK] * dm, 0.0), both[CHUNK:] * dm


def _lockstep(gens):
    results = [None] * len(gens)
    alive = list(range(len(gens)))
    while alive:
        for i in list(alive):
            try:
                next(gens[i])
            except StopIteration as stop:
                results[i] = stop.value
                alive.remove(i)
    return results


def _chunk_prep_bwd(q, k, v, beta, gam, t, du, dw, daqk, dqd, dkd, dgl):
    causal, strict, _ = _masks()
    r = lax.broadcasted_iota(jnp.int32, (CHUNK, CHUNK), 0)
    c = lax.broadcasted_iota(jnp.int32, (CHUNK, CHUNK), 1)
    scale = HEAD_DIM ** -0.5
    eg = jnp.exp(gam)
    gam_last = gam[CHUNK - 1:CHUNK, :]
    rr = jnp.exp(gam_last - gam)
    kb = k * beta
    qs = q * scale
    vb = v * beta
    kbe = kb * eg
    gam_b = jnp.broadcast_to(gam, (CHUNK, CHUNK))
    gam_s = _dot01(jnp.ones((CHUNK, CHUNK), F32), jnp.where(r == c, gam_b, 0.0))
    both = _mdot(jnp.concatenate([kb, qs], axis=0), k, NT)
    duw = jnp.concatenate([du, dw], axis=1)
    dt = _mdot(duw, jnp.concatenate([vb, kbe], axis=1), NT)
    dvk = _mdot(t, duw, TN)
    yield
    dm = jnp.where(causal, jnp.exp(jnp.where(causal, gam_b - gam_s, 0.0)), 0.0)
    a = jnp.where(strict, both[:CHUNK] * dm, 0.0)
    aqk = both[CHUNK:] * dm
    dvb, dkbe = dvk[:, :HEAD_DIM], dvk[:, HEAD_DIM:]
    x = _dot3(t, dt, TN)
    yield
    da = jnp.where(strict, -_dot3(x, t, NT), 0.0)
    yield
    dkk = da * dm
    dqk = daqk * dm
    ddiff = da * a + daqk * aqk
    dboth = jnp.concatenate([dkk, dqk], axis=0)
    dkq = _mdot(dboth, k)
    dk_mm = _mdot(dboth, jnp.concatenate([kb, qs], axis=0), TN)
    colsum = _dot01(ddiff, jnp.ones((CHUNK, LANES), F32), TN, mask_first=False)[:, 0:1]
    yield
    dkb = dkq[:CHUNK] + dkbe * eg
    dk = dk_mm + dkb * beta + dkd * rr
    dq = (dkq[CHUNK:] + dqd * eg) * scale
    dbeta = jnp.sum(dkb * k, axis=-1, keepdims=True) + jnp.sum(dvb * v, axis=-1, keepdims=True)
    dv = dvb * beta
    deg = jnp.sum(dkbe * kb, axis=-1, keepdims=True) + jnp.sum(dqd * qs, axis=-1, keepdims=True)
    drr = jnp.sum(dkd * k, axis=-1, keepdims=True)
    dgam = deg * eg - drr * rr + jnp.sum(ddiff, axis=-1, keepdims=True) - colsum
    dgam_last = jnp.sum(drr * rr, axis=0, keepdims=True) + dgl[0:1, :] * jnp.exp(gam_last)
    row = lax.broadcasted_iota(jnp.int32, (CHUNK, 1), 0)
    dgam = dgam + jnp.where(row == CHUNK - 1, dgam_last, 0.0)
    dg = _dot01(jnp.where(causal, 1.0, 0.0), jnp.broadcast_to(dgam, (CHUNK, LANES)), TN)[:, 0:1]
    return dq, dk, dv, dbeta, dg


def _prep_group(s):
    nch = s // CHUNK
    return next(c for c in (16, 8, 4, 2, 1) if nch % c == 0)


def _tri_solve_lanes(a_l):
    n = a_l.shape[1]
    group = 8

    def body(a_ref, t_ref):
        t_ref[...] = jnp.zeros_like(t_ref)
        col = lax.broadcasted_iota(jnp.int32, (CHUNK, n), 0)

        def row(r, carry):
            r0 = pl.multiple_of(r * CHUNK, CHUNK)

            def inner(sg, acc):
                a8 = a_ref[pl.ds(r0 + pl.multiple_of(sg * group, group), group), :]
                for j in range(group):
                    t0 = pl.multiple_of((sg * group + j) * CHUNK, CHUNK)
                    acc = acc + a8[j:j + 1, :] * t_ref[pl.ds(t0, CHUNK), :]
                return acc

            acc = lax.fori_loop(0, (r + group - 1) // group, inner, jnp.zeros((CHUNK, n), F32))
            t_ref[pl.ds(r0, CHUNK), :] = jnp.where(col == r, 1.0, 0.0) - acc
            return carry

        lax.fori_loop(0, CHUNK, row, 0)

    return pl.pallas_call(body, out_shape=jax.ShapeDtypeStruct(a_l.shape, F32), name="dn_tri_solve")(a_l)


def _head_cols(bg, hh, heads):
    lane = lax.broadcasted_iota(jnp.int32, bg.shape, 1)
    beta = jnp.sum(jnp.where(lane == hh, bg, 0.0), axis=-1, keepdims=True)
    g = jnp.sum(jnp.where(lane == heads + hh, bg, 0.0), axis=-1, keepdims=True)
    return beta, g


def _dn_prep(q, k, v, bg):
    h, s, _ = q.shape
    cb = _prep_group(s)
    rb = cb * CHUNK
    big = lambda x: (x, (None, rb, HEAD_DIM), lambda n, hh: (hh, n, 0))
    sq = lambda x: (x, (None, rb, CHUNK), lambda n, hh: (hh, n, 0))
    col = lambda x: (x, (None, rb, 1), lambda n, hh: (hh, n, 0))
    tok = (bg, (rb, LANES), lambda n, hh: (n, 0))
    o_big = ((h, s, HEAD_DIM), F32, (None, rb, HEAD_DIM), lambda n, hh: (hh, n, 0))
    o_sq = ((h, s, CHUNK), F32, (None, rb, CHUNK), lambda n, hh: (hh, n, 0))
    o_col = ((h, s, 1), F32, (None, rb, 1), lambda n, hh: (hh, n, 0))

    def scores(q_ref, k_ref, bg_ref, a_ref, aqk_ref, gam_ref):
        beta, g = _head_cols(bg_ref[...], pl.program_id(1), h)
        for i in range(cb):
            sl = slice(i * CHUNK, (i + 1) * CHUNK)
            gam, dm = _chunk_decay(g[sl])
            a_ref[sl, :], aqk_ref[sl, :] = _chunk_scores(q_ref[sl, :], k_ref[sl, :], beta[sl], dm)
            gam_ref[sl, :] = gam

    a, aqk, gam = _call(scores, grid=(s // rb, h), ins=[big(q), big(k), tok], outs=[o_sq, o_sq, o_col],
                        name="dn_scores")
    n_prob = h * (s // CHUNK)
    t_l = _tri_solve_lanes(jnp.transpose(a.reshape(n_prob, CHUNK * CHUNK)))
    t = jnp.transpose(t_l).reshape(h, s, CHUNK)

    def wy(k_ref, v_ref, bg_ref, gam_ref, t_ref, u_ref, w_ref):
        beta, _ = _head_cols(bg_ref[...], pl.program_id(1), h)
        for i in range(cb):
            sl = slice(i * CHUNK, (i + 1) * CHUNK)
            kb = k_ref[sl, :] * beta[sl]
            rhs = jnp.concatenate([v_ref[sl, :] * beta[sl], kb * jnp.exp(gam_ref[sl, :])], axis=1)
            uw = _mdot(t_ref[sl, :], rhs)
            u_ref[sl, :] = uw[:, :HEAD_DIM]
            w_ref[sl, :] = uw[:, HEAD_DIM:]

    u, w = _call(wy, grid=(s // rb, h), ins=[big(k), big(v), tok, col(gam), sq(t)], outs=[o_big, o_big],
                 name="dn_wy")
    return u, w, aqk, t, gam


def _dn_prep_bwd(q, k, v, bg, gam, t, du, dw, daqk, dqd, dkd, dgl):
    h, s, _ = q.shape
    cb = _prep_group(s)
    rb = cb * CHUNK

    def body(q_ref, k_ref, v_ref, bg_ref, g_ref, t_ref, du_ref, dw_ref, da_ref, dqd_ref, dkd_ref, dgl_ref,
             dq_ref, dk_ref, dv_ref, dbg_ref):
        hh = pl.program_id(1)
        beta, _ = _head_cols(bg_ref[...], hh, h)
        slices = [slice(i * CHUNK, (i + 1) * CHUNK) for i in range(cb)]
        results = _lockstep([_chunk_prep_bwd(
            q_ref[sl, :], k_ref[sl, :], v_ref[sl, :], beta[sl], g_ref[sl, :], t_ref[sl, :],
            du_ref[sl, :], dw_ref[sl, :], da_ref[sl, :], dqd_ref[sl, :], dkd_ref[sl, :], dgl_ref[sl, :])
            for sl in slices])

        @pl.when(hh == 0)
        def _():
            dbg_ref[...] = jnp.zeros_like(dbg_ref)

        lane = lax.broadcasted_iota(jnp.int32, (CHUNK, LANES), 1)
        for sl, (dq, dk, dv, dbeta, dg) in zip(slices, results):
            dq_ref[sl, :] = dq
            dk_ref[sl, :] = dk
            dv_ref[sl, :] = dv
            dbg_ref[sl, :] += jnp.where(lane == hh, dbeta, 0.0) + jnp.where(lane == h + hh, dg, 0.0)

    big = lambda x: (x, (None, rb, HEAD_DIM), lambda n, hh: (hh, n, 0))
    sq = lambda x: (x, (None, rb, CHUNK), lambda n, hh: (hh, n, 0))
    col = lambda x: (x, (None, rb, 1), lambda n, hh: (hh, n, 0))
    tok = (bg, (rb, LANES), lambda n, hh: (n, 0))
    o_big = ((h, s, HEAD_DIM), F32, (None, rb, HEAD_DIM), lambda n, hh: (hh, n, 0))
    return _call(body, grid=(s // rb, h),
                 ins=[big(q), big(k), big(v), tok, col(gam), sq(t), big(du), big(dw), sq(daqk), big(dqd), big(dkd),
                      col(dgl)],
                 outs=[o_big, o_big, o_big, ((s, LANES), F32, (rb, LANES), lambda n, hh: (n, 0))], name="dn_prep_bwd")


def _chunk_scaled(q, k, gam):
    gam_last = gam[CHUNK - 1:CHUNK, :]
    q_dec = q * (HEAD_DIM ** -0.5) * jnp.exp(gam)
    k_dec = k * jnp.exp(gam_last - gam)
    return q_dec, k_dec, jnp.exp(gam_last)


def _scan_group(s):
    return 2 if (s // CHUNK) % 2 == 0 else 1


def _dn_scan(q, k, u, w, aqk, gam):
    h, s, _ = q.shape
    nch = s // CHUNK
    sg = _scan_group(s)
    rb = sg * CHUNK

    def body(q_ref, k_ref, u_ref, w_ref, a_ref, gam_ref, o_ref, st_ref, state):
        @pl.when(pl.program_id(0) == 0)
        def _():
            state[...] = jnp.zeros_like(state)

        def head(hh, c):
            sl = slice(c * CHUNK, (c + 1) * CHUNK)
            s0 = state[hh]
            st_ref[c, hh] = s0
            q_dec, k_dec, gl = _chunk_scaled(q_ref[hh, sl, :], k_ref[hh, sl, :], gam_ref[hh, sl, :])
            both = _mdot(jnp.concatenate([w_ref[hh, sl, :], q_dec], axis=0), s0)
            yield
            v_new = u_ref[hh, sl, :] - both[:CHUNK]
            o_ref[sl, hh * HEAD_DIM:(hh + 1) * HEAD_DIM] = both[CHUNK:] + _mdot(a_ref[hh, sl, :], v_new)
            state[hh] = s0 * gl + _mdot(k_dec, v_new, TN)

        for c in range(sg):
            _lockstep([head(hh, c) for hh in range(h)])

    big = lambda x: (x, (h, rb, HEAD_DIM), lambda n: (0, n, 0))
    return _call(body, grid=(nch // sg,),
                 ins=[big(q), big(k), big(u), big(w), (aqk, (h, rb, CHUNK), lambda n: (0, n, 0)),
                      (gam, (h, rb, 1), lambda n: (0, n, 0))],
                 outs=[((s, h * HEAD_DIM), F32, (rb, h * HEAD_DIM), lambda n: (n, 0)),
                       ((nch, h, HEAD_DIM, HEAD_DIM), F32, (sg, h, HEAD_DIM, HEAD_DIM), lambda n: (n, 0, 0, 0))],
                 name="dn_scan", scratch=[pltpu.VMEM((h, HEAD_DIM, HEAD_DIM), F32)])


def _dn_scan_bwd(q, k, u, w, aqk, gam, states, do):
    h, s, _ = q.shape
    nch = s // CHUNK
    sg = _scan_group(s)
    rb = sg * CHUNK
    ngr = nch // sg

    def body(q_ref, k_ref, u_ref, w_ref, a_ref, gam_ref, st_ref, do_ref,
             du_ref, dw_ref, da_ref, dqd_ref, dkd_ref, dgl_ref, dstate):
        @pl.when(pl.program_id(0) == 0)
        def _():
            dstate[...] = jnp.zeros_like(dstate)

        def head(hh, c):
            sl = slice(c * CHUNK, (c + 1) * CHUNK)
            s0 = st_ref[c, hh]
            ds = dstate[hh]
            doh = do_ref[sl, hh * HEAD_DIM:(hh + 1) * HEAD_DIM]
            wv = w_ref[hh, sl, :]
            q_dec, k_dec, gl = _chunk_scaled(q_ref[hh, sl, :], k_ref[hh, sl, :], gam_ref[hh, sl, :])
            ws = _mdot(wv, s0)
            dv_new = _mdot(a_ref[hh, sl, :], doh, TN) + _mdot(k_dec, ds)
            dqd_ref[hh, sl, :] = _mdot(doh, s0, NT)
            qdo = _mdot(q_dec, doh, TN)
            tot = jnp.sum(jnp.sum(s0 * ds, axis=-1, keepdims=True), axis=0, keepdims=True)
            dgl_ref[hh, sl, :] = jnp.broadcast_to(tot, (CHUNK, 1))
            yield
            v_new = u_ref[hh, sl, :] - ws
            du_ref[hh, sl, :] = dv_new
            dw_ref[hh, sl, :] = -_mdot(dv_new, s0, NT)
            da_ref[hh, sl, :] = _mdot(doh, v_new, NT)
            dkd_ref[hh, sl, :] = _mdot(v_new, ds, NT)
            dstate[hh] = ds * gl + qdo - _mdot(wv, dv_new, TN)

        for c in range(sg - 1, -1, -1):
            _lockstep([head(hh, c) for hh in range(h)])

    rev = lambda n: (0, ngr - 1 - n, 0)
    big = lambda x: (x, (h, rb, HEAD_DIM), rev)
    o_big = ((h, s, HEAD_DIM), F32, (h, rb, HEAD_DIM), rev)
    return _call(body, grid=(ngr,),
                 ins=[big(q), big(k), big(u), big(w), (aqk, (h, rb, CHUNK), rev), (gam, (h, rb, 1), rev),
                      (states, (sg, h, HEAD_DIM, HEAD_DIM), lambda n: (ngr - 1 - n, 0, 0, 0)),
                      (do, (rb, h * HEAD_DIM), lambda n: (ngr - 1 - n, 0))],
                 outs=[o_big, o_big, ((h, s, CHUNK), F32, (h, rb, CHUNK), rev), o_big, o_big,
                       ((h, s, 1), F32, (h, rb, 1), rev)],
                 name="dn_scan_bwd", scratch=[pltpu.VMEM((h, HEAD_DIM, HEAD_DIM), F32)])


def _gates(x, a_log, dt_b, h):
    lane = lax.broadcasted_iota(jnp.int32, x.shape, 1)
    return jnp.where(lane < h, jax.nn.sigmoid(x), -jnp.exp(a_log) * jax.nn.softplus(x + dt_b))


def _head_out(oh, zh, nw):
    on = oh * lax.rsqrt(jnp.mean(oh * oh, axis=-1, keepdims=True) + RMS_EPS) * nw
    return on * jax.nn.silu(zh)


def _pad_lanes(x, lo):
    return jnp.zeros((1, LANES), F32).at[0, lo:lo + x.shape[0]].set(x)


def _deltanet_fwd(hin, get_w_in, conv_w, a_log, dt_bias, norm_w, get_w_out):
    h = a_log.shape[0]
    hw = h * HEAD_DIM
    w_in = get_w_in(hin)
    proj = _mm_nn(hin, w_in, F32, "dn_proj")
    q = _dn_conv_fwd(proj, conv_w, 0, h, True, "dn_conv_q")
    k = _dn_conv_fwd(proj, conv_w, h, h, True, "dn_conv_k")
    v = _dn_conv_fwd(proj, conv_w, 2 * h, h, False, "dn_conv_v")
    alp, dtp = _pad_lanes(a_log, h), _pad_lanes(dt_bias, h)

    def gates_fn(x, al, db):
        return (_gates(x, al, db, h),), ()

    (bg,), _ = _rowmap(gates_fn, [(proj, LANES, 4 * h)], [alp, dtp], [(LANES, F32)], [], "dn_gates")
    u, w, aqk, t, gam = _dn_prep(q, k, v, bg)
    o, states = _dn_scan(q, k, u, w, aqk, gam)
    nw = norm_w[None, :]

    def out_fn(o, z, nw):
        parts = [_head_out(o[:, i * HEAD_DIM:(i + 1) * HEAD_DIM], z[:, i * HEAD_DIM:(i + 1) * HEAD_DIM], nw)
                 for i in range(h)]
        return (jnp.concatenate(parts, axis=-1),), ()

    (og,), _ = _rowmap(out_fn, [o, (proj, hw, 3)], [nw], [(hw, MXU_DTYPE)], [], "dn_out")
    w_out = get_w_out(og)
    y = _mm_nn(og, w_out, F32, "dn_y")
    return y, (hin, proj, q, k, v, bg, u, w, aqk, t, gam, states, o, og, alp, dtp, nw, w_in, w_out)


def _deltanet_bwd(res, dy, conv_w):
    hin, proj, q, k, v, bg, u, w, aqk, t, gam, states, o, og, alp, dtp, nw, w_in, w_out = res
    h = q.shape[0]
    hw = h * HEAD_DIM
    s = hin.shape[0]
    d_w_out = _mm_tn(og, dy, MXU_DTYPE, "dn_dwout")
    dog = _mm_nt(dy, w_out, F32, "dn_dog")

    def out_bwd(o, z, dog, nw):
        dos, dzs = [], []
        dn = jnp.zeros((1, HEAD_DIM), F32)
        for i in range(h):
            sl = slice(i * HEAD_DIM, (i + 1) * HEAD_DIM)
            _, vjp = jax.vjp(_head_out, o[:, sl], z[:, sl], nw)
            a, b, c = vjp(dog[:, sl])
            dos.append(a)
            dzs.append(b)
            dn = dn + c
        return (jnp.concatenate(dos, axis=-1), jnp.concatenate(dzs, axis=-1)), (dn,)

    (do, dz), (d_norm_w,) = _rowmap(out_bwd, [o, (proj, hw, 3), dog], [nw], [(hw, F32), (hw, MXU_DTYPE)],
                                    [(1, HEAD_DIM)], "dn_out_bwd")
    du, dw, daqk, dqd, dkd, dgl = _dn_scan_bwd(q, k, u, w, aqk, gam, states, do)
    dq, dk, dv, dbg = _dn_prep_bwd(q, k, v, bg, gam, t, du, dw, daqk, dqd, dkd, dgl)
    dpq, dwq = _dn_conv_bwd(proj, conv_w, dq, 0, h, True, "dn_conv_q_bwd")
    dpk, dwk = _dn_conv_bwd(proj, conv_w, dk, h, h, True, "dn_conv_k_bwd")
    dpv, dwv = _dn_conv_bwd(proj, conv_w, dv, 2 * h, h, False, "dn_conv_v_bwd")

    def gates_bwd(x, dbg, al, db):
        _, vjp = jax.vjp(functools.partial(_gates, h=h), x, al, db)
        gx, gal, gdb = vjp(dbg)
        return (gx,), (gal, gdb)

    (dba,), (d_alp, d_dtp) = _rowmap(gates_bwd, [(proj, LANES, 4 * h), dbg], [alp, dtp], [(LANES, MXU_DTYPE)],
                                     [(1, LANES), (1, LANES)], "dn_gates_bwd")
    dproj = jnp.concatenate([dpq, dpk, dpv, dz, dba], axis=1)
    d_w_in = _mm_tn(hin, dproj, MXU_DTYPE, "dn_dwin")
    dh = _mm_nt(dproj, w_in, F32, "dn_dh")
    d_conv_w = jnp.concatenate([dwq, dwk, dwv], axis=1)
    return dh, dict(w_in=d_w_in, w_out=d_w_out, conv_w=d_conv_w, a_log=d_alp[0, h:2 * h], dt_bias=d_dtp[0, h:2 * h],
                    norm_w=d_norm_w[0])


def _ln_silu(u, g, b):
    return jax.nn.silu(_ln(u, g, b))


def _conformer_fwd(hin, get_w_in, dw_w, dw_b, ln_g, ln_b, get_w_out):
    w_in = get_w_in(hin)
    vg = _mm_nn(hin, w_in, F32, "cf_vg")
    u1 = _cf_conv_fwd(vg, dw_w, dw_b)
    ch = u1.shape[1]

    def fn(u, g, b):
        return (_ln_silu(u, g, b),), ()

    (u2,), _ = _rowmap(fn, [u1], [ln_g, ln_b], [(ch, MXU_DTYPE)], [], "cf_ln")
    w_out = get_w_out(u2)
    y = _mm_nn(u2, w_out, F32, "cf_y")
    return y, (hin, vg, u1, u2, w_in, w_out)


def _conformer_bwd(res, dy, dw_w, ln_g, ln_b):
    hin, vg, u1, u2, w_in, w_out = res
    ch = u1.shape[1]
    d_w_out = _mm_tn(u2, dy, MXU_DTYPE, "cf_dwout")
    du2 = _mm_nt(dy, w_out, F32, "cf_du2")

    def fn(u, du2, g, b):
        _, vjp = jax.vjp(_ln_silu, u, g, b)
        gu, gg, gb = vjp(du2)
        return (gu,), (gg, gb)

    (du1,), (d_ln_g, d_ln_b) = _rowmap(fn, [u1, du2], [ln_g, ln_b], [(ch, F32)], [(1, ch), (1, ch)], "cf_ln_bwd")
    dval, dgate, d_dw_w, d_dw_b = _cf_conv_bwd(vg, dw_w, du1)
    dvg = jnp.concatenate([dval, dgate], axis=1)
    d_w_in = _mm_tn(hin, dvg, MXU_DTYPE, "cf_dwin", split_cols=True)
    dh = _mm_nt(dvg, w_in, F32, "cf_dh")
    return dh, dict(w_in=d_w_in, w_out=d_w_out, dw_w=d_dw_w, dw_b=d_dw_b[0], ln_g=d_ln_g[0], ln_b=d_ln_b[0])


def _mlp_fwd(hin, get_w1, get_w2):
    w1 = get_w1(hin)
    a, r = _mm_nn(hin, w1, MXU_DTYPE, "ff_a", relu2=True)
    w2 = get_w2(r)
    m = _mm_nn(r, w2, F32, "ff_m")
    return m, (hin, a, r, w1, w2)


def _mlp_bwd(res, dm):
    hin, a, r, w1, w2 = res
    d_w2 = _mm_tn(r, dm, MXU_DTYPE, "ff_dw2")
    da = _mm_nt(dm, w2, MXU_DTYPE, "ff_da", relu2_of=a)
    d_w1 = _mm_tn(hin, da, MXU_DTYPE, "ff_dw1", split_cols=True)
    dh = _mm_nt(da, w1, F32, "ff_dh")
    return dh, d_w1, d_w2


def _ada_fwd(c_all, ada_w):
    depth, d, nl = ada_w.shape
    tn = _tile(nl, 256)

    def body(c_ref, w_ref, o_ref, cond_ref):
        cond = jax.nn.silu(c_ref[...]).astype(MXU_DTYPE)
        cond_ref[...] = cond
        o_ref[...] = lax.dot_general(cond, w_ref[...].astype(MXU_DTYPE), (NN, ((), ())), preferred_element_type=F32)

    return _call(body, grid=(depth, nl // tn),
                 ins=[(c_all, c_all.shape, lambda l, j: (0, 0)), (ada_w, (None, d, tn), lambda l, j: (l, 0, j))],
                 outs=[((depth, N_DEV, nl), F32, (None, N_DEV, tn), lambda l, j: (l, 0, j)),
                       (c_all.shape, MXU_DTYPE, c_all.shape, lambda l, j: (0, 0))],
                 name="ada_fwd")


def _ada_bwd(cond_all, dmod_cols):
    depth, _, nl = dmod_cols.shape
    d = cond_all.shape[1]
    tn = _tile(nl, 256)

    def body(c_ref, g_ref, o_ref):
        o_ref[...] = lax.dot_general(c_ref[...], g_ref[...].astype(MXU_DTYPE), (TN, ((), ())),
                                     preferred_element_type=F32)

    return _call(body, grid=(depth, nl // tn),
                 ins=[(cond_all, cond_all.shape, lambda l, j: (0, 0)), (dmod_cols, (None, N_DEV, tn), lambda l, j: (l, 0, j))],
                 outs=[((depth, d, nl), F32, (None, d, tn), lambda l, j: (l, 0, j))], name="ada_bwd")[0]


def _peers():
    x, y, c = lax.axis_index("x"), lax.axis_index("y"), lax.axis_index("c")
    peers = []
    for k in range(1, N_DEV):
        px = 1 - x if k & 4 else x
        py = 1 - y if k & 2 else y
        pc = 1 - c if k & 1 else c
        peers.append(((px, py, pc), 4 * px + 2 * py + pc))
    return 4 * x + 2 * y + c, peers


_HBM = pl.BlockSpec(memory_space=pltpu.HBM)
_SEM = pl.BlockSpec(memory_space=pltpu.SEMAPHORE)
_ANY = pl.BlockSpec(memory_space=pl.ANY)
_EFFECT = pltpu.SideEffectType.DATAFLOW_SIDE_EFFECTING


def _xfer_start(srcs, lands, scatter, after, name):
    nt = len(srcs)

    def body(*refs):
        src, land = refs[:nt], refs[nt:2 * nt]
        sems = refs[2 * nt + 1:4 * nt + 1]
        token = refs[-1]
        me, peers = _peers()
        for t in range(nt):
            for k, (pid, plin) in enumerate(peers):
                pltpu.make_async_remote_copy(
                    src_ref=src[t].at[plin] if scatter else src[t], dst_ref=land[t].at[me],
                    send_sem=sems[2 * t].at[k], recv_sem=sems[2 * t + 1].at[k],
                    device_id=pid, device_id_type=pl.DeviceIdType.MESH).start()
        token[...] = jnp.zeros_like(token)

    out_shape = [pltpu.SemaphoreType.DMA((N_DEV - 1,)) for _ in range(2 * nt)]
    out_shape += [pltpu.HBM(a.shape, a.dtype) for a in list(srcs) + list(lands)]
    out_shape += [jax.ShapeDtypeStruct((8, LANES), F32)]
    res = pl.pallas_call(
        body, name=name, out_shape=out_shape,
        in_specs=[_HBM] * (2 * nt) + [_ANY],
        out_specs=[_SEM] * (2 * nt) + [_HBM] * (2 * nt) + [pl.BlockSpec(memory_space=pltpu.VMEM)],
        input_output_aliases={i: 2 * nt + i for i in range(2 * nt)},
        compiler_params=pltpu.CompilerParams(has_side_effects=_EFFECT),
    )(*[pltpu.with_memory_space_constraint(a, pltpu.HBM) for a in list(srcs) + list(lands)], after)
    sems, thru = res[:2 * nt], res[2 * nt:4 * nt]
    return [(sems[2 * t], sems[2 * t + 1], thru[t], thru[nt + t]) for t in range(nt)], res[-1]


def _xfer_wait(handle, scatter, after, name):
    send, recv, src, land = handle

    def body(src_ref, land_ref, send_sem, recv_sem, after_ref, src_dead, land_out):
        _, peers = _peers()
        for k, (pid, plin) in enumerate(peers):
            cp = pltpu.make_async_remote_copy(
                src_ref=src_ref.at[plin] if scatter else src_ref, dst_ref=land_ref.at[plin],
                send_sem=send_sem.at[k], recv_sem=recv_sem.at[k],
                device_id=pid, device_id_type=pl.DeviceIdType.MESH)
            cp.wait_send()
            cp.wait_recv()

    return pl.pallas_call(
        body, name=name, out_shape=(pltpu.HBM(src.shape, src.dtype), pltpu.HBM(land.shape, land.dtype)),
        in_specs=(_HBM, _HBM, _SEM, _SEM, _ANY), out_specs=(_HBM, _HBM), input_output_aliases={0: 0, 1: 1},
        compiler_params=pltpu.CompilerParams(has_side_effects=_EFFECT),
    )(src, land, send, recv, after)[1]


def _landing(x, me):
    return lax.dynamic_update_slice(lax.empty((N_DEV,) + x.shape, x.dtype), x[None], (me,) + (0,) * x.ndim)


def _landing_scatter(p, me):
    own = lax.dynamic_index_in_dim(p, me, axis=0, keepdims=True)
    return lax.dynamic_update_slice(lax.empty(p.shape, p.dtype), own, (me,) + (0,) * (p.ndim - 1))


def _exchange(arrs, scatter, name):
    nt = len(arrs)
    out_shape = [jax.ShapeDtypeStruct(a.shape if scatter else (N_DEV,) + a.shape, a.dtype) for a in arrs]

    def body(*refs):
        ins, outs = refs[:nt], refs[nt:2 * nt]
        send, recv, loc = refs[2 * nt:]
        me, peers = _peers()
        copies = []
        for t in range(nt):
            own = pltpu.make_async_copy(ins[t].at[me] if scatter else ins[t], outs[t].at[me], loc.at[t])
            own.start()
            copies.append(own)
            for k, (pid, plin) in enumerate(peers):
                cp = pltpu.make_async_remote_copy(
                    src_ref=ins[t].at[plin] if scatter else ins[t], dst_ref=outs[t].at[me],
                    send_sem=send.at[t, k], recv_sem=recv.at[t, k],
                    device_id=pid, device_id_type=pl.DeviceIdType.MESH)
                cp.start()
                copies.append(cp)
        for cp in copies:
            cp.wait()

    any_spec = pl.BlockSpec(memory_space=pl.ANY)
    return pl.pallas_call(
        body, out_shape=out_shape, in_specs=[any_spec] * nt, out_specs=[any_spec] * nt,
        scratch_shapes=[pltpu.SemaphoreType.DMA((nt, N_DEV - 1)), pltpu.SemaphoreType.DMA((nt, N_DEV - 1)),
                        pltpu.SemaphoreType.DMA((nt,))],
        name=name)(*arrs)


def _adamw_body(n_parts):
    def body(p_ref, w_ref, m_ref, v_ref, *rest):
        g_out, d_out, m_out, v_out = rest[-4:]
        g = p_ref[0].astype(F32)
        for i in range(1, n_parts):
            g = g + p_ref[i].astype(F32)
        m2 = ADAM_B1 * m_ref[...] + (1.0 - ADAM_B1) * g
        v2 = ADAM_B2 * v_ref[...] + (1.0 - ADAM_B2) * jnp.square(g)
        m_hat = m2 / (1.0 - ADAM_B1 ** ADAM_STEP)
        v_hat = v2 / (1.0 - ADAM_B2 ** ADAM_STEP)
        g_out[...] = g
        d_out[...] = -ADAM_LR * (m_hat / (jnp.sqrt(v_hat) + ADAM_EPS) + ADAM_WD * w_ref[...])
        m_out[...] = m2
        v_out[...] = v2

    return body


def _adamw_layer(parts, w, m, v, layer, prev, name):
    p, r, c = parts.shape
    tr = _tile(r, 256, 8)
    blk = pl.BlockSpec((None, tr, c), lambda i: (layer, i, 0))
    in_specs = [pl.BlockSpec((p, tr, c), lambda i: (0, i, 0)), blk, blk, blk]
    args = [parts, w, m, v]
    aliases = {}
    if prev is not None:
        in_specs += [_ANY] * 4
        args += list(prev)
        aliases = {4 + i: i for i in range(4)}
    return pl.pallas_call(
        _adamw_body(p), grid=(r // tr,), in_specs=in_specs, out_specs=[blk] * 4,
        out_shape=[jax.ShapeDtypeStruct(w.shape, F32)] * 4, input_output_aliases=aliases, name=name,
        compiler_params=_cparams(1))(*args)


def _adamw(parts, w, m, v, name):
    p, nl, r, c = parts.shape
    tr = _tile(r, 256, 8)
    body = _adamw_body(p)

    blk = (None, tr, c)
    imap = lambda l, i: (l, i, 0)
    out = ((nl, r, c), F32, blk, imap)
    return _call(body, grid=(nl, r // tr),
                 ins=[(parts, (p, None, tr, c), lambda l, i: (0, l, i, 0)), (w, blk, imap), (m, blk, imap), (v, blk, imap)],
                 outs=[out] * 4, name=name)


def _rows(x):
    return x.reshape(-1, LANES)


def _pad_rows(x, mult=8):
    r = x.shape[0]
    extra = (-r) % mult
    return jnp.pad(x, ((0, extra), (0, 0))) if extra else x


def _shard_cols(x, me, groups):
    lead = x.shape[:-1]
    xr = x.reshape(lead + (N_DEV, groups * LANES))
    xs = lax.dynamic_index_in_dim(xr, me, axis=len(lead), keepdims=False)
    return xs.reshape(N_DEV, -1, LANES)


def kernel(x, c, ada_w, ada_b, ln_g, ln_b, dn_w_in, dn_conv_w, dn_a_log, dn_dt_bias, dn_norm_w, dn_w_out, cf_w_in, cf_dw_w, cf_dw_b, cf_ln_g, cf_ln_b, cf_w_out, ff_w1, ff_w2, loss_target, m_ada_w, m_ada_b, m_ln_g, m_ln_b, m_dn_w_in, m_dn_conv_w, m_dn_a_log, m_dn_dt_bias, m_dn_norm_w, m_dn_w_out, m_cf_w_in, m_cf_dw_w, m_cf_dw_b, m_cf_ln_g, m_cf_ln_b, m_cf_w_out, m_ff_w1, m_ff_w2, v_ada_w, v_ada_b, v_ln_g, v_ln_b, v_dn_w_in, v_dn_conv_w, v_dn_a_log, v_dn_dt_bias, v_dn_norm_w, v_dn_w_out, v_cf_w_in, v_cf_dw_w, v_cf_dw_b, v_cf_ln_g, v_cf_ln_b, v_cf_w_out, v_ff_w1, v_ff_w2):
    depth, d, _ = ada_w.shape
    n_a, n_b = dn_w_in.shape[0], cf_w_in.shape[0]
    heads = dn_a_log.shape[1]
    hw = heads * HEAD_DIM
    taps = cf_dw_w.shape[1]
    s = x.shape[1]
    alpha = (2.0 * depth) ** 0.25
    me = 4 * lax.axis_index("x") + 2 * lax.axis_index("y") + lax.axis_index("c")
    xs, tgt = x[0], loss_target[0]

    dn_in_cols = dn_w_in.shape[2]
    keys, shards = [], []
    for i in range(depth):
        j = i // 2
        mixer = [("dn_in", dn_w_in), ("dn_out", dn_w_out)] if i % 2 == 0 else [("cf_in", cf_w_in), ("cf_out", cf_w_out)]
        for nm, wt in mixer:
            keys.append((nm, j))
            shards.append(wt[j].astype(MXU_DTYPE))
        keys += [("ff1", i), ("ff2", i)]
        shards += [ff_w1[i].astype(MXU_DTYPE), ff_w2[i].astype(MXU_DTYPE)]

    small_local = [_rows(ln_g), _rows(ln_b), _rows(dn_conv_w), _rows(cf_dw_w), _rows(cf_dw_b), _rows(cf_ln_g),
                   _rows(cf_ln_b), _rows(c)]
    sizes = [a.shape[0] for a in small_local]
    packed = _pad_rows(jnp.concatenate(small_local, axis=0))
    (small_all,) = _exchange([packed], False, "comm_gather_params")
    offs = [0]
    for z in sizes:
        offs.append(offs[-1] + z)

    def small(i):
        return small_all[:, offs[i]:offs[i + 1], :]

    def unshard(piece, lead, groups):
        t = piece.reshape((N_DEV,) + lead + (groups * LANES,))
        t = jnp.moveaxis(t, 0, len(lead))
        return t.reshape(lead + (N_DEV * groups * LANES,))

    ln_g_f = unshard(small(0), (depth, 2), 1)
    ln_b_f = unshard(small(1), (depth, 2), 1)
    conv_w_f = unshard(small(2), (n_a, DN_CONV), 3 * heads // N_DEV)
    dw_w_f = unshard(small(3), (n_b, taps), 1)
    dw_b_f = unshard(small(4), (n_b,), 1)
    cf_ln_g_f = unshard(small(5), (n_b,), 1)
    cf_ln_b_f = unshard(small(6), (n_b,), 1)
    c_all = small(7).reshape(N_DEV, d)

    mod_part, cond_all = _ada_fwd(c_all, ada_w)
    (mod_all,) = _exchange([mod_part], False, "comm_gather_mod")
    mod_mine = lax.dynamic_index_in_dim(mod_all, me, axis=2, keepdims=False)
    mod_mine = jnp.moveaxis(mod_mine, 0, 1).reshape(depth, N_MOD * d)

    handles, token = _xfer_start(shards, [_landing(a, me) for a in shards], False, mod_all, "gather_weights_start")
    handles = dict(zip(keys, handles))
    weights = {}

    def gathered(key, after):
        if key not in weights:
            weights[key] = _xfer_wait(handles[key], False, after, "gather_wait_%s_%d" % key)
        return weights[key]

    def get_dn_in(j):
        def get(after):
            g = gathered(("dn_in", j), after)
            w = jnp.moveaxis(g, 0, 1).reshape(d, N_DEV * dn_in_cols)
            return jnp.pad(w, ((0, 0), (0, 4 * hw + LANES - N_DEV * dn_in_cols)))
        return get

    def get_rows(key):
        return lambda after: gathered(key, after).reshape((-1, d))

    def get_cols(key):
        return lambda after: gathered(key, after)

    def add_bias(a, b):
        return (a + b,), ()

    (mod,), _ = _rowmap(add_bias, [mod_mine + token[0, 0], ada_b], [], [(N_MOD * d, F32)], [], "ada_bias")

    def mod_row(i, j):
        return mod[i:i + 1, j * d:(j + 1) * d]

    def ln_row(arr, i, j):
        return arr[i, j][None, :]

    subs = []
    h_cur = _modulate_fwd(xs, mod_row(0, 1), mod_row(0, 0))
    x_cur = xs
    last = None
    for i in range(depth):
        j = i // 2
        if i % 2 == 0:
            y, res = _deltanet_fwd(h_cur, get_dn_in(j), conv_w_f[j], dn_a_log[j], dn_dt_bias[j], dn_norm_w[j],
                                   get_rows(("dn_out", j)))
        else:
            y, res = _conformer_fwd(h_cur, get_cols(("cf_in", j)), dw_w_f[j], dw_b_f[j][None, :], cf_ln_g_f[j][None, :],
                                    cf_ln_b_f[j][None, :], get_rows(("cf_out", j)))
        p1 = (mod_row(i, 2), ln_row(ln_g_f, i, 0), ln_row(ln_b_f, i, 0), mod_row(i, 4), mod_row(i, 3))
        x_mid, h_mid = _combine_fwd(alpha, x_cur, y, *p1)
        subs.append((x_cur, y, p1, res))
        m_out, res2 = _mlp_fwd(h_mid, get_cols(("ff1", i)), get_rows(("ff2", i)))
        if i + 1 < depth:
            p2 = (mod_row(i, 5), ln_row(ln_g_f, i, 1), ln_row(ln_b_f, i, 1), mod_row(i + 1, 1), mod_row(i + 1, 0))
            x_next, h_next = _combine_fwd(alpha, x_mid, m_out, *p2)
            subs.append((x_mid, m_out, p2, res2))
            x_cur, h_cur = x_next, h_next
        else:
            p2 = (mod_row(i, 5), ln_row(ln_g_f, i, 1), ln_row(ln_b_f, i, 1))
            last = (x_mid, m_out, p2, res2)

    x_in, y_in, p_last, res_last = last
    dx, dy, (loss_acc, g_gt, g_g, g_b) = _last_fwd_bwd(alpha, x_in, y_in, tgt, *p_last)
    loss = lax.psum(loss_acc[0, 0], ("x", "y", "c"))

    d_mod = [[None] * N_MOD for _ in range(depth)]
    d_ln_g = [[None, None] for _ in range(depth)]
    d_ln_b = [[None, None] for _ in range(depth)]
    d_mod[depth - 1][5], d_ln_g[depth - 1][1], d_ln_b[depth - 1][1] = g_gt, g_g, g_b
    gw = dict(dn=[None] * n_a, cf=[None] * n_b)

    sent = {}

    def send_grads(named, tag):
        parts = [p for _, p in named]
        hs, tok = _xfer_start(parts, [_landing_scatter(p, me) for p in parts], True, parts[0], "scatter_start_" + tag)
        for (key, _), hnd in zip(named, hs):
            sent[key] = hnd
        return tok[0, 0]

    def by_rows(g):
        return g.reshape((N_DEV, g.shape[0] // N_DEV, g.shape[1]))

    def send_mlp(i, d_w1, d_w2):
        return send_grads([(("ff1", i), d_w1), (("ff2", i), by_rows(d_w2))], "ff_%d" % i)

    dh, d_w1, d_w2 = _mlp_bwd(res_last, dy)
    pin = send_mlp(depth - 1, d_w1, d_w2)
    for idx in range(len(subs) - 1, -1, -1):
        x_in, y_in, prm, res = subs[idx]
        i, second = idx // 2, idx % 2
        prm = (prm[0] + pin,) + tuple(prm[1:])
        dx, dy, (g_gt, g_g, g_b, g_sc, g_sh) = _combine_bwd(alpha, x_in, y_in, dx, dh, *prm)
        d_mod[i][5 if second else 2], d_ln_g[i][second], d_ln_b[i][second] = g_gt, g_g, g_b
        nxt_i, nxt_base = (i + 1, 0) if second else (i, 3)
        d_mod[nxt_i][nxt_base + 1], d_mod[nxt_i][nxt_base] = g_sc, g_sh
        j = i // 2
        if second:
            dh, d_w1, d_w2 = _mlp_bwd(res, dy)
            pin = send_mlp(i, d_w1, d_w2)
        elif i % 2 == 0:
            dh, gw["dn"][j] = _deltanet_bwd(res, dy, conv_w_f[j])
            d_in = gw["dn"][j]["w_in"][:, :N_DEV * dn_in_cols].reshape(d, N_DEV, dn_in_cols)
            pin = send_grads([(("dn_in", j), jnp.moveaxis(d_in, 1, 0)), (("dn_out", j), by_rows(gw["dn"][j]["w_out"]))],
                             "dn_%d" % j)
        else:
            dh, gw["cf"][j] = _conformer_bwd(res, dy, dw_w_f[j], cf_ln_g_f[j][None, :], cf_ln_b_f[j][None, :])
            pin = send_grads([(("cf_in", j), gw["cf"][j]["w_in"]), (("cf_out", j), by_rows(gw["cf"][j]["w_out"]))],
                             "cf_%d" % j)
    grad_x, g_sc, g_sh = _modulate_bwd(xs, dx, dh, mod_row(0, 1) + pin, mod_row(0, 0))
    d_mod[0][1], d_mod[0][0] = g_sc, g_sh
    d_mod_full = jnp.concatenate([jnp.concatenate(r, axis=1) for r in d_mod], axis=0)

    stacked = {"dn_w_in": ("dn_in", dn_w_in, m_dn_w_in, v_dn_w_in), "dn_w_out": ("dn_out", dn_w_out, m_dn_w_out, v_dn_w_out),
               "cf_w_in": ("cf_in", cf_w_in, m_cf_w_in, v_cf_w_in), "cf_w_out": ("cf_out", cf_w_out, m_cf_w_out, v_cf_w_out),
               "ff_w1": ("ff1", ff_w1, m_ff_w1, v_ff_w1), "ff_w2": ("ff2", ff_w2, m_ff_w2, v_ff_w2)}
    chains = {key: None for key in stacked}

    def update_layer(i):
        mixer = ["dn_w_in", "dn_w_out"] if i % 2 == 0 else ["cf_w_in", "cf_w_out"]
        for key, idx in [("ff_w1", i), ("ff_w2", i)] + [(k, i // 2) for k in mixer]:
            short, w, m, v = stacked[key]
            parts = _xfer_wait(sent[(short, idx)], True, sg_token, "scatter_wait_%s_%d" % (short, idx))
            chains[key] = _adamw_layer(parts, w, m, v, idx, chains[key], "adamw_%s_%d" % (key, idx))

    def stack_rows(lst):
        return jnp.stack(lst, axis=0)

    gs_ln_g = jnp.stack([jnp.concatenate(r, axis=0) for r in d_ln_g], axis=0)
    gs_ln_b = jnp.stack([jnp.concatenate(r, axis=0) for r in d_ln_b], axis=0)
    gs_conv_w = stack_rows([gw["dn"][j]["conv_w"] for j in range(n_a)])
    gs_dw_w = stack_rows([gw["cf"][j]["dw_w"] for j in range(n_b)])
    gs_dw_b = stack_rows([gw["cf"][j]["dw_b"] for j in range(n_b)])
    gs_cf_ln_g = stack_rows([gw["cf"][j]["ln_g"] for j in range(n_b)])
    gs_cf_ln_b = stack_rows([gw["cf"][j]["ln_b"] for j in range(n_b)])
    gs_a_log = stack_rows([_pad_lanes(gw["dn"][j]["a_log"], 0)[0] for j in range(n_a)])
    gs_dt_bias = stack_rows([_pad_lanes(gw["dn"][j]["dt_bias"], 0)[0] for j in range(n_a)])
    gs_norm_w = stack_rows([gw["dn"][j]["norm_w"] for j in range(n_a)])
    small_grads = [gs_ln_g, gs_ln_b, gs_conv_w, gs_dw_w, gs_dw_b, gs_cf_ln_g, gs_cf_ln_b, gs_a_log, gs_dt_bias,
                   gs_norm_w, d_mod_full]
    sg_rows = [_rows(a) for a in small_grads]
    sg_sizes = [a.shape[0] for a in sg_rows]
    sg_packed = _pad_rows(jnp.concatenate(sg_rows, axis=0))
    (sg_handle,), sg_token = _xfer_start([sg_packed], [_landing(sg_packed, me)], False, grad_x, "gather_small_grads_start")
    for i in range(depth - 1, -1, -1):
        update_layer(i)
    sg_all = _xfer_wait(sg_handle, False, chains["ff_w1"][0], "gather_small_grads_wait")
    sg_offs = [0]
    for z in sg_sizes:
        sg_offs.append(sg_offs[-1] + z)

    def sg(i, shape):
        return sg_all[:, sg_offs[i]:sg_offs[i + 1], :].reshape((N_DEV,) + shape)

    dmod_all = sg(10, (depth, N_MOD * d))
    nl = ada_w.shape[2]
    dmod_cols = lax.dynamic_slice_in_dim(dmod_all, me * nl, nl, axis=2)
    g_ada_w = _ada_bwd(cond_all, jnp.moveaxis(dmod_cols, 0, 1))

    outs = {}

    def run_adamw(key, parts, w, m, v):
        shp = w.shape
        as3 = lambda t: t.reshape((-1,) + shp[-2:]) if t.ndim >= 3 else t.reshape((1,) + shp)
        parts3 = parts.reshape((parts.shape[0],) + as3(w).shape)
        res = _adamw(parts3, as3(w), as3(m), as3(v), "adamw_" + key)
        outs[key] = tuple(r.reshape(shp) for r in res)

    run_adamw("ada_w", g_ada_w[None], ada_w, m_ada_w, v_ada_w)

    cgroups = 3 * heads // N_DEV
    shard_parts = [
        _shard_cols(sg(0, (depth, 2, d)), me, 1), _shard_cols(sg(1, (depth, 2, d)), me, 1),
        _shard_cols(sg(2, (n_a, DN_CONV, 3 * hw)), me, cgroups), _shard_cols(sg(3, (n_b, taps, d)), me, 1),
        _shard_cols(sg(4, (n_b, d)), me, 1), _shard_cols(sg(5, (n_b, d)), me, 1), _shard_cols(sg(6, (n_b, d)), me, 1),
    ]
    repl_parts = [sg(7, (n_a, LANES)), sg(8, (n_a, LANES)), sg(9, (n_a, HEAD_DIM)),
                  sg(10, (depth, N_MOD * d)).reshape(N_DEV, -1, LANES)]
    small_parts = shard_parts + repl_parts
    sp_sizes = [a.shape[1] for a in small_parts]
    parts_packed = jnp.concatenate(small_parts, axis=1)
    extra = (-parts_packed.shape[1]) % 8
    parts_packed = jnp.pad(parts_packed, ((0, 0), (0, extra), (0, 0)))

    def pad_heads(t):
        return jnp.pad(t, ((0, 0), (0, LANES - heads)))

    def pack_state(ln_g_, ln_b_, conv_w_, dw_w_, dw_b_, cln_g_, cln_b_, a_log_, dt_b_, norm_w_, ada_b_):
        rows = [_rows(ln_g_), _rows(ln_b_), _rows(conv_w_), _rows(dw_w_), _rows(dw_b_), _rows(cln_g_), _rows(cln_b_),
                pad_heads(a_log_), pad_heads(dt_b_), norm_w_, _rows(ada_b_)]
        return _pad_rows(jnp.concatenate(rows, axis=0))

    w_s = pack_state(ln_g, ln_b, dn_conv_w, cf_dw_w, cf_dw_b, cf_ln_g, cf_ln_b, dn_a_log, dn_dt_bias, dn_norm_w, ada_b)
    m_s = pack_state(m_ln_g, m_ln_b, m_dn_conv_w, m_cf_dw_w, m_cf_dw_b, m_cf_ln_g, m_cf_ln_b, m_dn_a_log,
                     m_dn_dt_bias, m_dn_norm_w, m_ada_b)
    v_s = pack_state(v_ln_g, v_ln_b, v_dn_conv_w, v_cf_dw_w, v_cf_dw_b, v_cf_ln_g, v_cf_ln_b, v_dn_a_log,
                     v_dn_dt_bias, v_dn_norm_w, v_ada_b)
    res_s = _adamw(parts_packed[:, None], w_s[None], m_s[None], v_s[None], "adamw_small")
    sp_offs = [0]
    for z in sp_sizes:
        sp_offs.append(sp_offs[-1] + z)
    small_keys = ["ln_g", "ln_b", "dn_conv_w", "cf_dw_w", "cf_dw_b", "cf_ln_g", "cf_ln_b", "dn_a_log", "dn_dt_bias",
                  "dn_norm_w", "ada_b"]
    small_shapes = [ln_g.shape, ln_b.shape, dn_conv_w.shape, cf_dw_w.shape, cf_dw_b.shape, cf_ln_g.shape,
                    cf_ln_b.shape, dn_a_log.shape, dn_dt_bias.shape, dn_norm_w.shape, ada_b.shape]
    for n, (key, shp) in enumerate(zip(small_keys, small_shapes)):
        vals = []
        for r in res_s:
            piece = r[0, sp_offs[n]:sp_offs[n + 1], :]
            if key in ("dn_a_log", "dn_dt_bias"):
                piece = piece[:, :heads]
            vals.append(piece.reshape(shp))
        outs[key] = tuple(vals)

    for key in stacked:
        outs[key] = tuple(chains[key])

    order = ["ada_w", "ada_b", "ln_g", "ln_b", "dn_w_in", "dn_conv_w", "dn_a_log", "dn_dt_bias", "dn_norm_w",
             "dn_w_out", "cf_w_in", "cf_dw_w", "cf_dw_b", "cf_ln_g", "cf_ln_b", "cf_w_out", "ff_w1", "ff_w2"]
    result = [loss, grad_x[None]]
    for part in range(4):
        result += [outs[k][part] for k in order]
    return tuple(result)
```

```python
import functools

import jax
import jax.numpy as jnp
from jax import lax
from jax.experimental import pallas as pl
from jax.experimental.pallas import tpu as pltpu

F32 = jnp.float32
MXU_DTYPE = jnp.bfloat16
N_DEV = 8
LANES = 128
HEAD_DIM = 128
CHUNK = 64
DN_CONV = 4
N_MOD = 6
LN_EPS = 1e-5
RMS_EPS = 1e-6
L2_EPS = 1e-6
ADAM_LR = 0.001
ADAM_B1 = 0.9
ADAM_B2 = 0.999
ADAM_EPS = 1e-08
ADAM_WD = 0.01
ADAM_STEP = 10

HI = lax.Precision.HIGHEST
NN = ((1,), (0,))
NT = ((1,), (1,))
TN = ((0,), (0,))

ROW_TILE = 512
CONV_TILE = 256
SHORT_CONV_TILE = 1024


def _mdot(a, b, dims=NN):
    return lax.dot_general(a.astype(MXU_DTYPE), b.astype(MXU_DTYPE), (dims, ((), ())), preferred_element_type=F32)


def _split3(x):
    hi = x.astype(MXU_DTYPE)
    r1 = x - hi.astype(F32)
    mid = r1.astype(MXU_DTYPE)
    lo = (r1 - mid.astype(F32)).astype(MXU_DTYPE)
    return hi, mid, lo


def _dot01(a, b, dims=NN, mask_first=True):
    d = lambda p, q: lax.dot_general(p, q, (dims, ((), ())), preferred_element_type=F32)
    if mask_first:
        m = a.astype(MXU_DTYPE)
        return sum(d(m, p) for p in _split3(b))
    m = b.astype(MXU_DTYPE)
    return sum(d(p, m) for p in _split3(a))


def _dot3(a, b, dims=NN):
    ah, am, _ = _split3(a)
    bh, bm, _ = _split3(b)
    d = lambda p, q: lax.dot_general(p, q, (dims, ((), ())), preferred_element_type=F32)
    return d(ah, bh) + (d(ah, bm) + d(am, bh))


def _cparams(n):
    return pltpu.CompilerParams(dimension_semantics=("arbitrary",) * n)


def _call(body, *, grid, ins, outs, name, scratch=()):
    res = pl.pallas_call(
        body,
        grid=grid,
        in_specs=[pl.BlockSpec(b, m) for _, b, m in ins],
        out_specs=[pl.BlockSpec(b, m) for _, _, b, m in outs],
        out_shape=[jax.ShapeDtypeStruct(s, d) for s, d, _, _ in outs],
        scratch_shapes=list(scratch),
        name=name,
        compiler_params=_cparams(len(grid)),
    )(*[a for a, _, _ in ins])
    return res


def _tile(n, pref, unit=LANES):
    if n <= pref:
        return n
    t = (pref // unit) * unit
    while t > unit and n % t:
        t -= unit
    assert n % t == 0, (n, pref)
    return t


def _rowmap(fn, rows, consts, row_outs, acc_outs, name):
    rows = [r if isinstance(r, tuple) else (r, r.shape[1], 0) for r in rows]
    s = rows[0][0].shape[0]
    tm = min(ROW_TILE, s)
    nr, nc, no, na = len(rows), len(consts), len(row_outs), len(acc_outs)

    def body(*refs):
        rin, cin = refs[:nr], refs[nr:nr + nc]
        rout, aout = refs[nr + nc:nr + nc + no], refs[nr + nc + no:]
        ro, ao = fn(*[r[...] for r in rin], *[c[...] for c in cin])
        for ref, val in zip(rout, ro):
            ref[...] = val.astype(ref.dtype)
        if na:
            first = pl.program_id(0) == 0

            @pl.when(first)
            def _():
                for ref, val in zip(aout, ao):
                    ref[...] = val

            @pl.when(jnp.logical_not(first))
            def _():
                for ref, val in zip(aout, ao):
                    ref[...] += val

    ins = [(a, (tm, w), functools.partial(lambda i, cb: (i, cb), cb=cb)) for a, w, cb in rows]
    ins += [(c, c.shape, lambda i: (0, 0)) for c in consts]
    outs = [((s, w), d, (tm, w), lambda i: (i, 0)) for w, d in row_outs]
    outs += [(shp, F32, shp, lambda i: (0, 0)) for shp in acc_outs]
    res = _call(body, grid=(s // tm,), ins=ins, outs=outs, name=name)
    return res[:no], res[no:]


def _ln(z, g, b):
    mu = jnp.mean(z, -1, keepdims=True)
    var = jnp.mean(jnp.square(z - mu), -1, keepdims=True)
    return (z - mu) * lax.rsqrt(var + LN_EPS) * g + b


def _combine(alpha, x, y, gt, g, b, sc, sh):
    xn = _ln(alpha * x + (1.0 + gt) * y, g, b)
    return xn, xn * (1.0 + sc) + sh


def _modulate_fwd(x, sc, sh):
    def fn(x, sc, sh):
        return ((x * (1.0 + sc) + sh),), ()

    (h,), _ = _rowmap(fn, [x], [sc, sh], [(x.shape[1], MXU_DTYPE)], [], "modulate_fwd")
    return h


def _modulate_bwd(x, dx, dh, sc, sh):
    d = x.shape[1]

    def fn(x, dx, dh, sc, sh):
        _, vjp = jax.vjp(lambda x, sc, sh: x * (1.0 + sc) + sh, x, sc, sh)
        gx, gsc, gsh = vjp(dh)
        return (dx + gx,), (gsc, gsh)

    (gx,), (gsc, gsh) = _rowmap(fn, [x, dx, dh], [sc, sh], [(d, F32)], [(1, d), (1, d)], "modulate_bwd")
    return gx, gsc, gsh


def _combine_fwd(alpha, x, y, gt, g, b, sc, sh):
    d = x.shape[1]

    def fn(x, y, gt, g, b, sc, sh):
        return _combine(alpha, x, y, gt, g, b, sc, sh), ()

    (xn, h), _ = _rowmap(fn, [x, y], [gt, g, b, sc, sh], [(d, F32), (d, MXU_DTYPE)], [], "combine_fwd")
    return xn, h


def _combine_bwd(alpha, x, y, dxn, dh, gt, g, b, sc, sh):
    d = x.shape[1]

    def fn(x, y, dxn, dh, gt, g, b, sc, sh):
        _, vjp = jax.vjp(functools.partial(_combine, alpha), x, y, gt, g, b, sc, sh)
        gx, gy, ggt, gg, gb, gsc, gsh = vjp((dxn, dh))
        return (gx, gy), (ggt, gg, gb, gsc, gsh)

    (gx, gy), accs = _rowmap(fn, [x, y, dxn, dh], [gt, g, b, sc, sh], [(d, F32), (d, MXU_DTYPE)],
                             [(1, d)] * 5, "combine_bwd")
    return gx, gy, accs


def _last_fwd_bwd(alpha, x, y, tgt, gt, g, b):
    d = x.shape[1]

    def fn(x, y, tgt, gt, g, b):
        xn, vjp = jax.vjp(lambda x, y, gt, g, b: _ln(alpha * x + (1.0 + gt) * y, g, b), x, y, gt, g, b)
        err = xn - tgt
        gx, gy, ggt, gg, gb = vjp(err * (1.0 / d))
        rows = jnp.sum(jnp.square(err), axis=-1, keepdims=True)
        loss = (0.5 / d) * jnp.sum(rows, axis=0, keepdims=True) * jnp.ones((1, LANES), F32)
        return (gx, gy), (loss, ggt, gg, gb)

    (gx, gy), accs = _rowmap(fn, [x, y, tgt], [gt, g, b], [(d, F32), (d, MXU_DTYPE)],
                             [(1, LANES), (1, d), (1, d), (1, d)], "last_fwd_bwd")
    return gx, gy, accs


MM_VMEM_BUDGET = 40 * 2 ** 20


def _fit(options, cost):
    for o in options:
        if 2 * cost(o) <= MM_VMEM_BUDGET:
            return o
    return options[-1]


def _row_tiles(m):
    return [t for t in (2048, 1024, 512, 256) if t <= m and m % t == 0] or [m]


def _mm_call(a, a_blk, a_map, b, b_blk, b_map, outs, dims, grid, name, epi=None, extra=None, split=None):
    nk = grid[2]
    n_out = len(outs)
    n_in = 3 if extra is not None else 2

    def body(*refs):
        a_ref, b_ref = refs[0], refs[1]
        rest = refs[n_in:]
        out_refs = rest[:n_out]

        def finish(val):
            if epi == "relu2":
                out_refs[0][...] = val.astype(out_refs[0].dtype)
                out_refs[1][...] = jnp.square(jnp.maximum(val, 0.0)).astype(out_refs[1].dtype)
            elif epi == "relu2_bwd":
                out_refs[0][...] = (val * 2.0 * jnp.maximum(refs[2][...], 0.0)).astype(out_refs[0].dtype)
            elif split is not None:
                for g in range(split[0]):
                    out_refs[0][g] = val[:, g * split[1]:(g + 1) * split[1]].astype(out_refs[0].dtype)
            else:
                out_refs[0][...] = val.astype(out_refs[0].dtype)

        p = lax.dot_general(a_ref[...], b_ref[...], (dims, ((), ())), preferred_element_type=F32)
        if nk == 1:
            finish(p)
        else:
            acc = rest[n_out]
            k = pl.program_id(2)

            @pl.when(k == 0)
            def _():
                acc[...] = p

            @pl.when(k > 0)
            def _():
                acc[...] += p

            @pl.when(k == nk - 1)
            def _():
                finish(acc[...])

    if nk > 1:
        out_blk = tuple(x for x in outs[0][2] if x is not None)
        if split is not None:
            out_blk = (out_blk[1], split[0] * split[1])
        scratch = [pltpu.VMEM(out_blk, F32)]
    else:
        scratch = []
    ins = [(a, a_blk, a_map), (b, b_blk, b_map)] + ([extra] if extra is not None else [])
    return _call(body, grid=grid, ins=ins, outs=outs, name=name, scratch=scratch)


def _isz(dt):
    return jnp.dtype(dt).itemsize


def _mm_nn(a, b, out_dtype, name, relu2=False):
    m, kdim = a.shape
    if b.ndim == 2:
        n = b.shape[1]
        tn = _tile(n, 1536 if n > 2048 else 512)
        b_blk, b_map = (kdim, tn), lambda i, j, k: (0, j)
    else:
        g, _, ng = b.shape
        n = g * ng
        tn = _tile(ng, 512)
        b_blk = (None, kdim, tn)
        b_map = functools.partial(lambda i, j, k, npg: (j // npg, 0, j % npg), npg=ng // tn)
    out_bytes = (4 + _isz(out_dtype)) if relu2 else _isz(out_dtype)
    tm = _fit(_row_tiles(m), lambda t: t * kdim * _isz(a.dtype) + kdim * tn * _isz(b.dtype) + t * tn * out_bytes)
    grid = (m // tm, n // tn, 1)
    outs = [((m, n), F32 if relu2 else out_dtype, (tm, tn), lambda i, j, k: (i, j))]
    if relu2:
        outs.append(((m, n), out_dtype, (tm, tn), lambda i, j, k: (i, j)))
    res = _mm_call(a, (tm, kdim), lambda i, j, k: (i, 0), b, b_blk, b_map, outs, NN, grid, name,
                   epi="relu2" if relu2 else None)
    return res if relu2 else res[0]


def _mm_nt(a, b, out_dtype, name, relu2_of=None):
    m, n = a.shape
    extra_bytes = 4 if relu2_of is not None else 0
    if b.ndim == 2:
        kout = b.shape[0]
        to, tc, nk = _tile(kout, 512), n, 1
        b_blk, b_map = (to, tc), lambda i, j, k: (j, 0)
        acc_bytes = 0
    else:
        nk, kout, tc = b.shape
        to = _tile(kout, 1024)
        b_blk, b_map = (None, to, tc), lambda i, j, k: (k, j, 0)
        acc_bytes = 2
    tm = _fit(_row_tiles(m), lambda t: t * tc * _isz(a.dtype) + to * tc * _isz(b.dtype)
              + t * to * (_isz(out_dtype) + extra_bytes + acc_bytes))
    grid = (m // tm, kout // to, nk)
    outs = [((m, kout), out_dtype, (tm, to), lambda i, j, k: (i, j))]
    extra = (relu2_of, (tm, to), lambda i, j, k: (i, j)) if relu2_of is not None else None
    return _mm_call(a, (tm, tc), lambda i, j, k: (i, k), b, b_blk, b_map, outs, NT, grid, name,
                    epi="relu2_bwd" if relu2_of is not None else None, extra=extra)[0]


def _mm_tn(a, b, out_dtype, name, split_cols=False):
    m, kdim = a.shape
    n = b.shape[1]
    tk = _tile(kdim, 512)
    tn = _tile(n, 1536)
    if not split_cols:
        out, split = ((kdim, n), out_dtype, (tk, tn), lambda i, j, k: (i, j)), None
    else:
        ng = n // N_DEV
        if tn % ng:
            tn = _tile(ng, 512)
        if tn >= ng:
            gb = tn // ng
            out = ((N_DEV, kdim, ng), out_dtype, (gb, tk, ng), lambda i, j, k: (j, i, 0))
            split = (gb, ng)
        else:
            out = ((N_DEV, kdim, ng), out_dtype, (None, tk, tn),
                   functools.partial(lambda i, j, k, npg: (j // npg, i, j % npg), npg=ng // tn))
            split = None
    grid = (kdim // tk, n // tn, 1)
    return _mm_call(a, (m, tk), lambda i, j, k: (0, i), b, (m, tn), lambda i, j, k: (0, j), [out], TN, grid, name,
                    split=split)[0]


def _shifted(xa, off, rows):
    if off % 8 == 0:
        return xa[off:off + rows]
    return pltpu.roll(xa, xa.shape[0] - off, 0)[:rows]


def _conv_pad(taps):
    return -(-(taps - 1) // 8) * 8


def _conv_tile(xp_ref, w, i, rows, taps):
    pad = _conv_pad(taps)
    r0 = pl.multiple_of(i * rows, rows)
    xa = xp_ref[pl.ds(r0, rows + pad), :]
    views = [_shifted(xa, pad - (taps - 1) + j, rows) for j in range(taps)]
    acc = w[0:1, :] * views[0]
    for j in range(1, taps):
        acc = acc + w[j:j + 1, :] * views[j]
    return r0, acc, views


def _conv_back_tile(yp_ref, w, i, rows, taps):
    pad = _conv_pad(taps)
    r0 = pl.multiple_of(i * rows, rows)
    ya = yp_ref[pl.ds(r0, rows + pad), :]
    acc = w[taps - 1:taps, :] * ya[:rows]
    for j in range(taps - 1):
        acc = acc + w[j:j + 1, :] * _shifted(ya, taps - 1 - j, rows)
    return r0, acc


def _tap_sums(dy, views, taps):
    row = lax.broadcasted_iota(jnp.int32, (taps, LANES), 0)
    acc = jnp.zeros((taps, LANES), F32)
    for j in range(taps):
        acc = acc + jnp.where(row == j, jnp.sum(dy * views[j], axis=0, keepdims=True), 0.0)
    return acc


def _silu_l2(xc, l2):
    a = jax.nn.silu(xc)
    if l2:
        a = a * lax.rsqrt(jnp.sum(a * a, axis=-1, keepdims=True) + L2_EPS)
    return a


def _dn_conv_fwd(proj, conv_w, c0, nblk, l2, name):
    s = proj.shape[0]
    pad = _conv_pad(DN_CONV)
    rows = min(SHORT_CONV_TILE, s)

    def body(x_ref, w_ref, o_ref, xp):
        xp[0:pad, :] = jnp.zeros((pad, LANES), F32)
        xp[pad:, :] = x_ref[...]
        w = w_ref[...]

        def tile(i, c):
            r0, acc, _ = _conv_tile(xp, w, i, rows, DN_CONV)
            o_ref[pl.ds(r0, rows), :] = _silu_l2(acc, l2)
            return c

        lax.fori_loop(0, s // rows, tile, 0)

    return _call(body, grid=(nblk,),
                 ins=[(proj, (s, LANES), lambda c: (0, c0 + c)), (conv_w, (DN_CONV, LANES), lambda c: (0, c0 + c))],
                 outs=[((nblk, s, LANES), F32, (None, s, LANES), lambda c: (c, 0, 0))],
                 name=name, scratch=[pltpu.VMEM((s + pad, LANES), F32)])[0]


def _dn_conv_bwd(proj, conv_w, da, c0, nblk, l2, name):
    s = proj.shape[0]
    pad = _conv_pad(DN_CONV)
    rows = min(SHORT_CONV_TILE, s)

    def body(x_ref, w_ref, da_ref, dx_ref, dw_ref, xp, yp):
        xp[0:pad, :] = jnp.zeros((pad, LANES), F32)
        xp[pad:, :] = x_ref[...]
        yp[s:, :] = jnp.zeros((pad, LANES), F32)
        w = w_ref[...]

        def tile(i, dw):
            r0, acc, views = _conv_tile(xp, w, i, rows, DN_CONV)
            _, vjp = jax.vjp(functools.partial(_silu_l2, l2=l2), acc)
            (dxc,) = vjp(da_ref[pl.ds(r0, rows), :])
            yp[pl.ds(r0, rows), :] = dxc
            return dw + _tap_sums(dxc, views, DN_CONV)

        dw_ref[...] = lax.fori_loop(0, s // rows, tile, jnp.zeros((DN_CONV, LANES), F32))

        def tile2(i, c):
            r0, acc = _conv_back_tile(yp, w, i, rows, DN_CONV)
            dx_ref[pl.ds(r0, rows), :] = acc.astype(dx_ref.dtype)
            return c

        lax.fori_loop(0, s // rows, tile2, 0)

    return _call(body, grid=(nblk,),
                 ins=[(proj, (s, LANES), lambda c: (0, c0 + c)), (conv_w, (DN_CONV, LANES), lambda c: (0, c0 + c)),
                      (da, (None, s, LANES), lambda c: (c, 0, 0))],
                 outs=[((s, nblk * LANES), MXU_DTYPE, (s, LANES), lambda c: (0, c)),
                       ((DN_CONV, nblk * LANES), F32, (DN_CONV, LANES), lambda c: (0, c))],
                 name=name, scratch=[pltpu.VMEM((s + pad, LANES), F32), pltpu.VMEM((s + pad, LANES), F32)])


def _cf_conv_fwd(vg, dw_w, dw_b):
    s, c2 = vg.shape
    ch = c2 // 2
    nblk = ch // LANES
    taps = dw_w.shape[0]
    pad = _conv_pad(taps)
    rows = min(CONV_TILE, s)

    def body(v_ref, g_ref, w_ref, b_ref, o_ref, xp):
        xp[0:pad, :] = jnp.zeros((pad, LANES), F32)
        xp[pad:, :] = v_ref[...] * jax.nn.sigmoid(g_ref[...])
        w = w_ref[...]
        bias = b_ref[...]

        def tile(i, c):
            r0, acc, _ = _conv_tile(xp, w, i, rows, taps)
            o_ref[pl.ds(r0, rows), :] = acc + bias
            return c

        lax.fori_loop(0, s // rows, tile, 0)

    return _call(body, grid=(nblk,),
                 ins=[(vg, (s, LANES), lambda c: (0, c)), (vg, (s, LANES), lambda c: (0, nblk + c)),
                      (dw_w, (taps, LANES), lambda c: (0, c)), (dw_b, (1, LANES), lambda c: (0, c))],
                 outs=[((s, ch), F32, (s, LANES), lambda c: (0, c))],
                 name="cf_conv_fwd", scratch=[pltpu.VMEM((s + pad, LANES), F32)])[0]


def _cf_conv_bwd(vg, dw_w, du):
    s, c2 = vg.shape
    ch = c2 // 2
    nblk = ch // LANES
    taps = dw_w.shape[0]
    pad = _conv_pad(taps)
    rows = min(CONV_TILE, s)

    def body(v_ref, g_ref, w_ref, du_ref, dv_ref, dg_ref, dw_ref, db_ref, xp, yp):
        sig = jax.nn.sigmoid(g_ref[...])
        xp[0:pad, :] = jnp.zeros((pad, LANES), F32)
        xp[pad:, :] = v_ref[...] * sig
        yp[0:s, :] = du_ref[...]
        yp[s:, :] = jnp.zeros((pad, LANES), F32)
        w = w_ref[...]
        db_ref[...] = jnp.sum(du_ref[...], axis=0, keepdims=True)

        def tile(i, dw):
            r0, _, views = _conv_tile(xp, w, i, rows, taps)
            return dw + _tap_sums(du_ref[pl.ds(r0, rows), :], views, taps)

        dw_ref[...] = lax.fori_loop(0, s // rows, tile, jnp.zeros((taps, LANES), F32))

        def tile2(i, c):
            r0, du0 = _conv_back_tile(yp, w, i, rows, taps)
            val = v_ref[pl.ds(r0, rows), :]
            sg = jax.nn.sigmoid(g_ref[pl.ds(r0, rows), :])
            dv_ref[pl.ds(r0, rows), :] = (du0 * sg).astype(dv_ref.dtype)
            dg_ref[pl.ds(r0, rows), :] = (du0 * val * sg * (1.0 - sg)).astype(dg_ref.dtype)
            return c

        lax.fori_loop(0, s // rows, tile2, 0)

    return _call(body, grid=(nblk,),
                 ins=[(vg, (s, LANES), lambda c: (0, c)), (vg, (s, LANES), lambda c: (0, nblk + c)),
                      (dw_w, (taps, LANES), lambda c: (0, c)), (du, (s, LANES), lambda c: (0, c))],
                 outs=[((s, ch), MXU_DTYPE, (s, LANES), lambda c: (0, c)),
                       ((s, ch), MXU_DTYPE, (s, LANES), lambda c: (0, c)),
                       ((taps, ch), F32, (taps, LANES), lambda c: (0, c)),
                       ((1, ch), F32, (1, LANES), lambda c: (0, c))],
                 name="cf_conv_bwd", scratch=[pltpu.VMEM((s + pad, LANES), F32), pltpu.VMEM((s + pad, LANES), F32)])


def _masks():
    r = lax.broadcasted_iota(jnp.int32, (CHUNK, CHUNK), 0)
    c = lax.broadcasted_iota(jnp.int32, (CHUNK, CHUNK), 1)
    return r >= c, r > c, r <= c


def _chunk_decay(g):
    causal, _, upper = _masks()
    gb = jnp.broadcast_to(g, (CHUNK, CHUNK))
    gam_r = _dot01(jnp.where(causal, 1.0, 0.0), gb)
    gam_s = _dot01(jnp.ones((CHUNK, CHUNK), F32), jnp.where(upper, gb, 0.0))
    dm = jnp.where(causal, jnp.exp(jnp.where(causal, gam_r - gam_s, 0.0)), 0.0)
    return gam_r[:, 0:1], dm


def _chunk_scores(q, k, beta, dm):
    _, strict, _ = _masks()
    both = _mdot(jnp.concatenate([k * beta, q * (HEAD_DIM ** -0.5)], axis=0), k, NT)
    return jnp.where(strict, both[:CHUNK] * dm, 0.0), both[CHUNK:] * dm


def _lockstep(gens):
    results = [None] * len(gens)
    alive = list(range(len(gens)))
    while alive:
        for i in list(alive):
            try:
                next(gens[i])
            except StopIteration as stop:
                results[i] = stop.value
                alive.remove(i)
    return results


def _chunk_prep_bwd(q, k, v, beta, gam, t, du, dw, daqk, dqd, dkd, dgl):
    causal, strict, _ = _masks()
    r = lax.broadcasted_iota(jnp.int32, (CHUNK, CHUNK), 0)
    c = lax.broadcasted_iota(jnp.int32, (CHUNK, CHUNK), 1)
    scale = HEAD_DIM ** -0.5
    eg = jnp.exp(gam)
    gam_last = gam[CHUNK - 1:CHUNK, :]
    rr = jnp.exp(gam_last - gam)
    kb = k * beta
    qs = q * scale
    vb = v * beta
    kbe = kb * eg
    gam_b = jnp.broadcast_to(gam, (CHUNK, CHUNK))
    gam_s = _dot01(jnp.ones((CHUNK, CHUNK), F32), jnp.where(r == c, gam_b, 0.0))
    both = _mdot(jnp.concatenate([kb, qs], axis=0), k, NT)
    duw = jnp.concatenate([du, dw], axis=1)
    dt = _mdot(duw, jnp.concatenate([vb, kbe], axis=1), NT)
    dvk = _mdot(t, duw, TN)
    yield
    dm = jnp.where(causal, jnp.exp(jnp.where(causal, gam_b - gam_s, 0.0)), 0.0)
    a = jnp.where(strict, both[:CHUNK] * dm, 0.0)
    aqk = both[CHUNK:] * dm
    dvb, dkbe = dvk[:, :HEAD_DIM], dvk[:, HEAD_DIM:]
    x = _dot3(t, dt, TN)
    yield
    da = jnp.where(strict, -_dot3(x, t, NT), 0.0)
    yield
    dkk = da * dm
    dqk = daqk * dm
    ddiff = da * a + daqk * aqk
    dboth = jnp.concatenate([dkk, dqk], axis=0)
    dkq = _mdot(dboth, k)
    dk_mm = _mdot(dboth, jnp.concatenate([kb, qs], axis=0), TN)
    colsum = _dot01(ddiff, jnp.ones((CHUNK, LANES), F32), TN, mask_first=False)[:, 0:1]
    yield
    dkb = dkq[:CHUNK] + dkbe * eg
    dk = dk_mm + dkb * beta + dkd * rr
    dq = (dkq[CHUNK:] + dqd * eg) * scale
    dbeta = jnp.sum(dkb * k, axis=-1, keepdims=True) + jnp.sum(dvb * v, axis=-1, keepdims=True)
    dv = dvb * beta
    deg = jnp.sum(dkbe * kb, axis=-1, keepdims=True) + jnp.sum(dqd * qs, axis=-1, keepdims=True)
    drr = jnp.sum(dkd * k, axis=-1, keepdims=True)
    dgam = deg * eg - drr * rr + jnp.sum(ddiff, axis=-1, keepdims=True) - colsum
    dgam_last = jnp.sum(drr * rr, axis=0, keepdims=True) + dgl[0:1, :] * jnp.exp(gam_last)
    row = lax.broadcasted_iota(jnp.int32, (CHUNK, 1), 0)
    dgam = dgam + jnp.where(row == CHUNK - 1, dgam_last, 0.0)
    dg = _dot01(jnp.where(causal, 1.0, 0.0), jnp.broadcast_to(dgam, (CHUNK, LANES)), TN)[:, 0:1]
    return dq, dk, dv, dbeta, dg


def _prep_group(s):
    nch = s // CHUNK
    return next(c for c in (16, 8, 4, 2, 1) if nch % c == 0)


def _tri_solve_lanes(a_l):
    n = a_l.shape[1]
    group = 8

    def body(a_ref, t_ref):
        t_ref[...] = jnp.zeros_like(t_ref)
        col = lax.broadcasted_iota(jnp.int32, (CHUNK, n), 0)

        def row(r, carry):
            r0 = pl.multiple_of(r * CHUNK, CHUNK)

            def inner(sg, acc):
                a8 = a_ref[pl.ds(r0 + pl.multiple_of(sg * group, group), group), :]
                for j in range(group):
                    t0 = pl.multiple_of((sg * group + j) * CHUNK, CHUNK)
                    acc = acc + a8[j:j + 1, :] * t_ref[pl.ds(t0, CHUNK), :]
                return acc

            acc = lax.fori_loop(0, (r + group - 1) // group, inner, jnp.zeros((CHUNK, n), F32))
            t_ref[pl.ds(r0, CHUNK), :] = jnp.where(col == r, 1.0, 0.0) - acc
            return carry

        lax.fori_loop(0, CHUNK, row, 0)

    return pl.pallas_call(body, out_shape=jax.ShapeDtypeStruct(a_l.shape, F32), name="dn_tri_solve")(a_l)


def _head_cols(bg, hh, heads):
    lane = lax.broadcasted_iota(jnp.int32, bg.shape, 1)
    beta = jnp.sum(jnp.where(lane == hh, bg, 0.0), axis=-1, keepdims=True)
    g = jnp.sum(jnp.where(lane == heads + hh, bg, 0.0), axis=-1, keepdims=True)
    return beta, g


def _dn_prep(q, k, v, bg):
    h, s, _ = q.shape
    cb = _prep_group(s)
    rb = cb * CHUNK
    big = lambda x: (x, (None, rb, HEAD_DIM), lambda n, hh: (hh, n, 0))
    sq = lambda x: (x, (None, rb, CHUNK), lambda n, hh: (hh, n, 0))
    col = lambda x: (x, (None, rb, 1), lambda n, hh: (hh, n, 0))
    tok = (bg, (rb, LANES), lambda n, hh: (n, 0))
    o_big = ((h, s, HEAD_DIM), F32, (None, rb, HEAD_DIM), lambda n, hh: (hh, n, 0))
    o_sq = ((h, s, CHUNK), F32, (None, rb, CHUNK), lambda n, hh: (hh, n, 0))
    o_col = ((h, s, 1), F32, (None, rb, 1), lambda n, hh: (hh, n, 0))

    def scores(q_ref, k_ref, bg_ref, a_ref, aqk_ref, gam_ref):
        beta, g = _head_cols(bg_ref[...], pl.program_id(1), h)
        for i in range(cb):
            sl = slice(i * CHUNK, (i + 1) * CHUNK)
            gam, dm = _chunk_decay(g[sl])
            a_ref[sl, :], aqk_ref[sl, :] = _chunk_scores(q_ref[sl, :], k_ref[sl, :], beta[sl], dm)
            gam_ref[sl, :] = gam

    a, aqk, gam = _call(scores, grid=(s // rb, h), ins=[big(q), big(k), tok], outs=[o_sq, o_sq, o_col],
                        name="dn_scores")
    n_prob = h * (s // CHUNK)
    t_l = _tri_solve_lanes(jnp.transpose(a.reshape(n_prob, CHUNK * CHUNK)))
    t = jnp.transpose(t_l).reshape(h, s, CHUNK)

    def wy(k_ref, v_ref, bg_ref, gam_ref, t_ref, u_ref, w_ref):
        beta, _ = _head_cols(bg_ref[...], pl.program_id(1), h)
        for i in range(cb):
            sl = slice(i * CHUNK, (i + 1) * CHUNK)
            kb = k_ref[sl, :] * beta[sl]
            rhs = jnp.concatenate([v_ref[sl, :] * beta[sl], kb * jnp.exp(gam_ref[sl, :])], axis=1)
            uw = _mdot(t_ref[sl, :], rhs)
            u_ref[sl, :] = uw[:, :HEAD_DIM]
            w_ref[sl, :] = uw[:, HEAD_DIM:]

    u, w = _call(wy, grid=(s // rb, h), ins=[big(k), big(v), tok, col(gam), sq(t)], outs=[o_big, o_big],
                 name="dn_wy")
    return u, w, aqk, t, gam


def _dn_prep_bwd(q, k, v, bg, gam, t, du, dw, daqk, dqd, dkd, dgl):
    h, s, _ = q.shape
    cb = _prep_group(s)
    rb = cb * CHUNK

    def body(q_ref, k_ref, v_ref, bg_ref, g_ref, t_ref, du_ref, dw_ref, da_ref, dqd_ref, dkd_ref, dgl_ref,
             dq_ref, dk_ref, dv_ref, dbg_ref):
        hh = pl.program_id(1)
        beta, _ = _head_cols(bg_ref[...], hh, h)
        slices = [slice(i * CHUNK, (i + 1) * CHUNK) for i in range(cb)]
        results = _lockstep([_chunk_prep_bwd(
            q_ref[sl, :], k_ref[sl, :], v_ref[sl, :], beta[sl], g_ref[sl, :], t_ref[sl, :],
            du_ref[sl, :], dw_ref[sl, :], da_ref[sl, :], dqd_ref[sl, :], dkd_ref[sl, :], dgl_ref[sl, :])
            for sl in slices])

        @pl.when(hh == 0)
        def _():
            dbg_ref[...] = jnp.zeros_like(dbg_ref)

        lane = lax.broadcasted_iota(jnp.int32, (CHUNK, LANES), 1)
        for sl, (dq, dk, dv, dbeta, dg) in zip(slices, results):
            dq_ref[sl, :] = dq
            dk_ref[sl, :] = dk
            dv_ref[sl, :] = dv
            dbg_ref[sl, :] += jnp.where(lane == hh, dbeta, 0.0) + jnp.where(lane == h + hh, dg, 0.0)

    big = lambda x: (x, (None, rb, HEAD_DIM), lambda n, hh: (hh, n, 0))
    sq = lambda x: (x, (None, rb, CHUNK), lambda n, hh: (hh, n, 0))
    col = lambda x: (x, (None, rb, 1), lambda n, hh: (hh, n, 0))
    tok = (bg, (rb, LANES), lambda n, hh: (n, 0))
    o_big = ((h, s, HEAD_DIM), F32, (None, rb, HEAD_DIM), lambda n, hh: (hh, n, 0))
    return _call(body, grid=(s // rb, h),
                 ins=[big(q), big(k), big(v), tok, col(gam), sq(t), big(du), big(dw), sq(daqk), big(dqd), big(dkd),
                      col(dgl)],
                 outs=[o_big, o_big, o_big, ((s, LANES), F32, (rb, LANES), lambda n, hh: (n, 0))], name="dn_prep_bwd")


def _chunk_scaled(q, k, gam):
    gam_last = gam[CHUNK - 1:CHUNK, :]
    q_dec = q * (HEAD_DIM ** -0.5) * jnp.exp(gam)
    k_dec = k * jnp.exp(gam_last - gam)
    return q_dec, k_dec, jnp.exp(gam_last)


def _scan_group(s):
    return 2 if (s // CHUNK) % 2 == 0 else 1


def _dn_scan(q, k, u, w, aqk, gam):
    h, s, _ = q.shape
    nch = s // CHUNK
    sg = _scan_group(s)
    rb = sg * CHUNK

    def body(q_ref, k_ref, u_ref, w_ref, a_ref, gam_ref, o_ref, st_ref, state):
        @pl.when(pl.program_id(0) == 0)
        def _():
            state[...] = jnp.zeros_like(state)

        def head(hh, c):
            sl = slice(c * CHUNK, (c + 1) * CHUNK)
            s0 = state[hh]
            st_ref[c, hh] = s0
            q_dec, k_dec, gl = _chunk_scaled(q_ref[hh, sl, :], k_ref[hh, sl, :], gam_ref[hh, sl, :])
            both = _mdot(jnp.concatenate([w_ref[hh, sl, :], q_dec], axis=0), s0)
            yield
            v_new = u_ref[hh, sl, :] - both[:CHUNK]
            o_ref[sl, hh * HEAD_DIM:(hh + 1) * HEAD_DIM] = both[CHUNK:] + _mdot(a_ref[hh, sl, :], v_new)
            state[hh] = s0 * gl + _mdot(k_dec, v_new, TN)

        for c in range(sg):
            _lockstep([head(hh, c) for hh in range(h)])

    big = lambda x: (x, (h, rb, HEAD_DIM), lambda n: (0, n, 0))
    return _call(body, grid=(nch // sg,),
                 ins=[big(q), big(k), big(u), big(w), (aqk, (h, rb, CHUNK), lambda n: (0, n, 0)),
                      (gam, (h, rb, 1), lambda n: (0, n, 0))],
                 outs=[((s, h * HEAD_DIM), F32, (rb, h * HEAD_DIM), lambda n: (n, 0)),
                       ((nch, h, HEAD_DIM, HEAD_DIM), F32, (sg, h, HEAD_DIM, HEAD_DIM), lambda n: (n, 0, 0, 0))],
                 name="dn_scan", scratch=[pltpu.VMEM((h, HEAD_DIM, HEAD_DIM), F32)])


def _dn_scan_bwd(q, k, u, w, aqk, gam, states, do):
    h, s, _ = q.shape
    nch = s // CHUNK
    sg = _scan_group(s)
    rb = sg * CHUNK
    ngr = nch // sg

    def body(q_ref, k_ref, u_ref, w_ref, a_ref, gam_ref, st_ref, do_ref,
             du_ref, dw_ref, da_ref, dqd_ref, dkd_ref, dgl_ref, dstate):
        @pl.when(pl.program_id(0) == 0)
        def _():
            dstate[...] = jnp.zeros_like(dstate)

        def head(hh, c):
            sl = slice(c * CHUNK, (c + 1) * CHUNK)
            s0 = st_ref[c, hh]
            ds = dstate[hh]
            doh = do_ref[sl, hh * HEAD_DIM:(hh + 1) * HEAD_DIM]
            wv = w_ref[hh, sl, :]
            q_dec, k_dec, gl = _chunk_scaled(q_ref[hh, sl, :], k_ref[hh, sl, :], gam_ref[hh, sl, :])
            ws = _mdot(wv, s0)
            dv_new = _mdot(a_ref[hh, sl, :], doh, TN) + _mdot(k_dec, ds)
            dqd_ref[hh, sl, :] = _mdot(doh, s0, NT)
            qdo = _mdot(q_dec, doh, TN)
            tot = jnp.sum(jnp.sum(s0 * ds, axis=-1, keepdims=True), axis=0, keepdims=True)
            dgl_ref[hh, sl, :] = jnp.broadcast_to(tot, (CHUNK, 1))
            yield
            v_new = u_ref[hh, sl, :] - ws
            du_ref[hh, sl, :] = dv_new
            dw_ref[hh, sl, :] = -_mdot(dv_new, s0, NT)
            da_ref[hh, sl, :] = _mdot(doh, v_new, NT)
            dkd_ref[hh, sl, :] = _mdot(v_new, ds, NT)
            dstate[hh] = ds * gl + qdo - _mdot(wv, dv_new, TN)

        for c in range(sg - 1, -1, -1):
            _lockstep([head(hh, c) for hh in range(h)])

    rev = lambda n: (0, ngr - 1 - n, 0)
    big = lambda x: (x, (h, rb, HEAD_DIM), rev)
    o_big = ((h, s, HEAD_DIM), F32, (h, rb, HEAD_DIM), rev)
    return _call(body, grid=(ngr,),
                 ins=[big(q), big(k), big(u), big(w), (aqk, (h, rb, CHUNK), rev), (gam, (h, rb, 1), rev),
                      (states, (sg, h, HEAD_DIM, HEAD_DIM), lambda n: (ngr - 1 - n, 0, 0, 0)),
                      (do, (rb, h * HEAD_DIM), lambda n: (ngr - 1 - n, 0))],
                 outs=[o_big, o_big, ((h, s, CHUNK), F32, (h, rb, CHUNK), rev), o_big, o_big,
                       ((h, s, 1), F32, (h, rb, 1), rev)],
                 name="dn_scan_bwd", scratch=[pltpu.VMEM((h, HEAD_DIM, HEAD_DIM), F32)])


def _gates(x, a_log, dt_b, h):
    lane = lax.broadcasted_iota(jnp.int32, x.shape, 1)
    return jnp.where(lane < h, jax.nn.sigmoid(x), -jnp.exp(a_log) * jax.nn.softplus(x + dt_b))


def _head_out(oh, zh, nw):
    on = oh * lax.rsqrt(jnp.mean(oh * oh, axis=-1, keepdims=True) + RMS_EPS) * nw
    return on * jax.nn.silu(zh)


def _pad_lanes(x, lo):
    return jnp.zeros((1, LANES), F32).at[0, lo:lo + x.shape[0]].set(x)


def _deltanet_fwd(hin, get_w_in, conv_w, a_log, dt_bias, norm_w, get_w_out):
    h = a_log.shape[0]
    hw = h * HEAD_DIM
    w_in = get_w_in(hin)
    proj = _mm_nn(hin, w_in, F32, "dn_proj")
    q = _dn_conv_fwd(proj, conv_w, 0, h, True, "dn_conv_q")
    k = _dn_conv_fwd(proj, conv_w, h, h, True, "dn_conv_k")
    v = _dn_conv_fwd(proj, conv_w, 2 * h, h, False, "dn_conv_v")
    alp, dtp = _pad_lanes(a_log, h), _pad_lanes(dt_bias, h)

    def gates_fn(x, al, db):
        return (_gates(x, al, db, h),), ()

    (bg,), _ = _rowmap(gates_fn, [(proj, LANES, 4 * h)], [alp, dtp], [(LANES, F32)], [], "dn_gates")
    u, w, aqk, t, gam = _dn_prep(q, k, v, bg)
    o, states = _dn_scan(q, k, u, w, aqk, gam)
    nw = norm_w[None, :]

    def out_fn(o, z, nw):
        parts = [_head_out(o[:, i * HEAD_DIM:(i + 1) * HEAD_DIM], z[:, i * HEAD_DIM:(i + 1) * HEAD_DIM], nw)
                 for i in range(h)]
        return (jnp.concatenate(parts, axis=-1),), ()

    (og,), _ = _rowmap(out_fn, [o, (proj, hw, 3)], [nw], [(hw, MXU_DTYPE)], [], "dn_out")
    w_out = get_w_out(og)
    y = _mm_nn(og, w_out, F32, "dn_y")
    return y, (hin, proj, q, k, v, bg, u, w, aqk, t, gam, states, o, og, alp, dtp, nw, w_in, w_out)


def _deltanet_bwd(res, dy, conv_w):
    hin, proj, q, k, v, bg, u, w, aqk, t, gam, states, o, og, alp, dtp, nw, w_in, w_out = res
    h = q.shape[0]
    hw = h * HEAD_DIM
    s = hin.shape[0]
    d_w_out = _mm_tn(og, dy, MXU_DTYPE, "dn_dwout")
    dog = _mm_nt(dy, w_out, F32, "dn_dog")

    def out_bwd(o, z, dog, nw):
        dos, dzs = [], []
        dn = jnp.zeros((1, HEAD_DIM), F32)
        for i in range(h):
            sl = slice(i * HEAD_DIM, (i + 1) * HEAD_DIM)
            _, vjp = jax.vjp(_head_out, o[:, sl], z[:, sl], nw)
            a, b, c = vjp(dog[:, sl])
            dos.append(a)
            dzs.append(b)
            dn = dn + c
        return (jnp.concatenate(dos, axis=-1), jnp.concatenate(dzs, axis=-1)), (dn,)

    (do, dz), (d_norm_w,) = _rowmap(out_bwd, [o, (proj, hw, 3), dog], [nw], [(hw, F32), (hw, MXU_DTYPE)],
                                    [(1, HEAD_DIM)], "dn_out_bwd")
    du, dw, daqk, dqd, dkd, dgl = _dn_scan_bwd(q, k, u, w, aqk, gam, states, do)
    dq, dk, dv, dbg = _dn_prep_bwd(q, k, v, bg, gam, t, du, dw, daqk, dqd, dkd, dgl)
    dpq, dwq = _dn_conv_bwd(proj, conv_w, dq, 0, h, True, "dn_conv_q_bwd")
    dpk, dwk = _dn_conv_bwd(proj, conv_w, dk, h, h, True, "dn_conv_k_bwd")
    dpv, dwv = _dn_conv_bwd(proj, conv_w, dv, 2 * h, h, False, "dn_conv_v_bwd")

    def gates_bwd(x, dbg, al, db):
        _, vjp = jax.vjp(functools.partial(_gates, h=h), x, al, db)
        gx, gal, gdb = vjp(dbg)
        return (gx,), (gal, gdb)

    (dba,), (d_alp, d_dtp) = _rowmap(gates_bwd, [(proj, LANES, 4 * h), dbg], [alp, dtp], [(LANES, MXU_DTYPE)],
                                     [(1, LANES), (1, LANES)], "dn_gates_bwd")
    dproj = jnp.concatenate([dpq, dpk, dpv, dz, dba], axis=1)
    d_w_in = _mm_tn(hin, dproj, MXU_DTYPE, "dn_dwin")
    dh = _mm_nt(dproj, w_in, F32, "dn_dh")
    d_conv_w = jnp.concatenate([dwq, dwk, dwv], axis=1)
    return dh, dict(w_in=d_w_in, w_out=d_w_out, conv_w=d_conv_w, a_log=d_alp[0, h:2 * h], dt_bias=d_dtp[0, h:2 * h],
                    norm_w=d_norm_w[0])


def _ln_silu(u, g, b):
    return jax.nn.silu(_ln(u, g, b))


def _conformer_fwd(hin, get_w_in, dw_w, dw_b, ln_g, ln_b, get_w_out):
    w_in = get_w_in(hin)
    vg = _mm_nn(hin, w_in, F32, "cf_vg")
    u1 = _cf_conv_fwd(vg, dw_w, dw_b)
    ch = u1.shape[1]

    def fn(u, g, b):
        return (_ln_silu(u, g, b),), ()

    (u2,), _ = _rowmap(fn, [u1], [ln_g, ln_b], [(ch, MXU_DTYPE)], [], "cf_ln")
    w_out = get_w_out(u2)
    y = _mm_nn(u2, w_out, F32, "cf_y")
    return y, (hin, vg, u1, u2, w_in, w_out)


def _conformer_bwd(res, dy, dw_w, ln_g, ln_b):
    hin, vg, u1, u2, w_in, w_out = res
    ch = u1.shape[1]
    d_w_out = _mm_tn(u2, dy, MXU_DTYPE, "cf_dwout")
    du2 = _mm_nt(dy, w_out, F32, "cf_du2")

    def fn(u, du2, g, b):
        _, vjp = jax.vjp(_ln_silu, u, g, b)
        gu, gg, gb = vjp(du2)
        return (gu,), (gg, gb)

    (du1,), (d_ln_g, d_ln_b) = _rowmap(fn, [u1, du2], [ln_g, ln_b], [(ch, F32)], [(1, ch), (1, ch)], "cf_ln_bwd")
    dval, dgate, d_dw_w, d_dw_b = _cf_conv_bwd(vg, dw_w, du1)
    dvg = jnp.concatenate([dval, dgate], axis=1)
    d_w_in = _mm_tn(hin, dvg, MXU_DTYPE, "cf_dwin", split_cols=True)
    dh = _mm_nt(dvg, w_in, F32, "cf_dh")
    return dh, dict(w_in=d_w_in, w_out=d_w_out, dw_w=d_dw_w, dw_b=d_dw_b[0], ln_g=d_ln_g[0], ln_b=d_ln_b[0])


def _mlp_fwd(hin, get_w1, get_w2):
    w1 = get_w1(hin)
    a, r = _mm_nn(hin, w1, MXU_DTYPE, "ff_a", relu2=True)
    w2 = get_w2(r)
    m = _mm_nn(r, w2, F32, "ff_m")
    return m, (hin, a, r, w1, w2)


def _mlp_bwd(res, dm):
    hin, a, r, w1, w2 = res
    d_w2 = _mm_tn(r, dm, MXU_DTYPE, "ff_dw2")
    da = _mm_nt(dm, w2, MXU_DTYPE, "ff_da", relu2_of=a)
    d_w1 = _mm_tn(hin, da, MXU_DTYPE, "ff_dw1", split_cols=True)
    dh = _mm_nt(da, w1, F32, "ff_dh")
    return dh, d_w1, d_w2


def _ada_fwd(c_all, ada_w):
    depth, d, nl = ada_w.shape
    tn = _tile(nl, 256)

    def body(c_ref, w_ref, o_ref, cond_ref):
        cond = jax.nn.silu(c_ref[...]).astype(MXU_DTYPE)
        cond_ref[...] = cond
        o_ref[...] = lax.dot_general(cond, w_ref[...].astype(MXU_DTYPE), (NN, ((), ())), preferred_element_type=F32)

    return _call(body, grid=(depth, nl // tn),
                 ins=[(c_all, c_all.shape, lambda l, j: (0, 0)), (ada_w, (None, d, tn), lambda l, j: (l, 0, j))],
                 outs=[((depth, N_DEV, nl), F32, (None, N_DEV, tn), lambda l, j: (l, 0, j)),
                       (c_all.shape, MXU_DTYPE, c_all.shape, lambda l, j: (0, 0))],
                 name="ada_fwd")


def _ada_bwd(cond_all, dmod_cols):
    depth, _, nl = dmod_cols.shape
    d = cond_all.shape[1]
    tn = _tile(nl, 256)

    def body(c_ref, g_ref, o_ref):
        o_ref[...] = lax.dot_general(c_ref[...], g_ref[...].astype(MXU_DTYPE), (TN, ((), ())),
                                     preferred_element_type=F32)

    return _call(body, grid=(depth, nl // tn),
                 ins=[(cond_all, cond_all.shape, lambda l, j: (0, 0)), (dmod_cols, (None, N_DEV, tn), lambda l, j: (l, 0, j))],
                 outs=[((depth, d, nl), F32, (None, d, tn), lambda l, j: (l, 0, j))], name="ada_bwd")[0]


def _peers():
    x, y, c = lax.axis_index("x"), lax.axis_index("y"), lax.axis_index("c")
    peers = []
    for k in range(1, N_DEV):
        px = 1 - x if k & 4 else x
        py = 1 - y if k & 2 else y
        pc = 1 - c if k & 1 else c
        peers.append(((px, py, pc), 4 * px + 2 * py + pc))
    return 4 * x + 2 * y + c, peers


_HBM = pl.BlockSpec(memory_space=pltpu.HBM)
_SEM = pl.BlockSpec(memory_space=pltpu.SEMAPHORE)
_ANY = pl.BlockSpec(memory_space=pl.ANY)
_EFFECT = pltpu.SideEffectType.DATAFLOW_SIDE_EFFECTING


def _xfer_start(srcs, lands, scatter, after, name):
    nt = len(srcs)

    def body(*refs):
        src, land = refs[:nt], refs[nt:2 * nt]
        sems = refs[2 * nt + 1:4 * nt + 1]
        token = refs[-1]
        me, peers = _peers()
        for t in range(nt):
            for k, (pid, plin) in enumerate(peers):
                pltpu.make_async_remote_copy(
                    src_ref=src[t].at[plin] if scatter else src[t], dst_ref=land[t].at[me],
                    send_sem=sems[2 * t].at[k], recv_sem=sems[2 * t + 1].at[k],
                    device_id=pid, device_id_type=pl.DeviceIdType.MESH).start()
        token[...] = jnp.zeros_like(token)

    out_shape = [pltpu.SemaphoreType.DMA((N_DEV - 1,)) for _ in range(2 * nt)]
    out_shape += [pltpu.HBM(a.shape, a.dtype) for a in list(srcs) + list(lands)]
    out_shape += [jax.ShapeDtypeStruct((8, LANES), F32)]
    res = pl.pallas_call(
        body, name=name, out_shape=out_shape,
        in_specs=[_HBM] * (2 * nt) + [_ANY],
        out_specs=[_SEM] * (2 * nt) + [_HBM] * (2 * nt) + [pl.BlockSpec(memory_space=pltpu.VMEM)],
        input_output_aliases={i: 2 * nt + i for i in range(2 * nt)},
        compiler_params=pltpu.CompilerParams(has_side_effects=_EFFECT),
    )(*[pltpu.with_memory_space_constraint(a, pltpu.HBM) for a in list(srcs) + list(lands)], after)
    sems, thru = res[:2 * nt], res[2 * nt:4 * nt]
    return [(sems[2 * t], sems[2 * t + 1], thru[t], thru[nt + t]) for t in range(nt)], res[-1]


def _xfer_wait(handle, scatter, after, name):
    send, recv, src, land = handle

    def body(src_ref, land_ref, send_sem, recv_sem, after_ref, src_dead, land_out):
        _, peers = _peers()
        for k, (pid, plin) in enumerate(peers):
            cp = pltpu.make_async_remote_copy(
                src_ref=src_ref.at[plin] if scatter else src_ref, dst_ref=land_ref.at[plin],
                send_sem=send_sem.at[k], recv_sem=recv_sem.at[k],
                device_id=pid, device_id_type=pl.DeviceIdType.MESH)
            cp.wait_send()
            cp.wait_recv()

    return pl.pallas_call(
        body, name=name, out_shape=(pltpu.HBM(src.shape, src.dtype), pltpu.HBM(land.shape, land.dtype)),
        in_specs=(_HBM, _HBM, _SEM, _SEM, _ANY), out_specs=(_HBM, _HBM), input_output_aliases={0: 0, 1: 1},
        compiler_params=pltpu.CompilerParams(has_side_effects=_EFFECT),
    )(src, land, send, recv, after)[1]


def _landing(x, me):
    return lax.dynamic_update_slice(lax.empty((N_DEV,) + x.shape, x.dtype), x[None], (me,) + (0,) * x.ndim)


def _landing_scatter(p, me):
    own = lax.dynamic_index_in_dim(p, me, axis=0, keepdims=True)
    return lax.dynamic_update_slice(lax.empty(p.shape, p.dtype), own, (me,) + (0,) * (p.ndim - 1))


def _chip_peers():
    x, y, c = lax.axis_index("x"), lax.axis_index("y"), lax.axis_index("c")
    lin = lambda px, py, pc: 4 * px + 2 * py + pc
    sibling = ((x, y, 1 - c), lin(x, y, 1 - c))
    chips = [((1 - x, y, c), lin(1 - x, y, c)), ((x, 1 - y, c), lin(x, 1 - y, c)),
             ((1 - x, 1 - y, c), lin(1 - x, 1 - y, c))]
    return lin(x, y, c), sibling, chips


N_CHIPS_OTHER = 3


def _gather2_start(srcs, lands, after, name):
    nt = len(srcs)

    def body(*refs):
        src, land = refs[:nt], refs[nt:2 * nt]
        sems = refs[2 * nt + 1:5 * nt + 1]
        token = refs[-1]
        me, sibling, chips = _chip_peers()
        for t in range(nt):
            send, recv_ici, recv_sib = sems[3 * t], sems[3 * t + 1], sems[3 * t + 2]
            pltpu.make_async_remote_copy(src_ref=src[t], dst_ref=land[t].at[me], send_sem=send.at[0],
                                         recv_sem=recv_sib.at[0], device_id=sibling[0],
                                         device_id_type=pl.DeviceIdType.MESH).start()
            for j, (pid, _) in enumerate(chips):
                pltpu.make_async_remote_copy(src_ref=src[t], dst_ref=land[t].at[me], send_sem=send.at[1 + j],
                                             recv_sem=recv_ici.at[j], device_id=pid,
                                             device_id_type=pl.DeviceIdType.MESH).start()
        token[...] = jnp.zeros_like(token)

    out_shape = []
    for _ in range(nt):
        out_shape += [pltpu.SemaphoreType.DMA((1 + N_CHIPS_OTHER,)), pltpu.SemaphoreType.DMA((N_CHIPS_OTHER,)),
                      pltpu.SemaphoreType.DMA((1,))]
    out_shape += [pltpu.HBM(a.shape, a.dtype) for a in list(srcs) + list(lands)]
    out_shape += [jax.ShapeDtypeStruct((8, LANES), F32)]
    res = pl.pallas_call(
        body, name=name, out_shape=out_shape,
        in_specs=[_HBM] * (2 * nt) + [_ANY],
        out_specs=[_SEM] * (3 * nt) + [_HBM] * (2 * nt) + [pl.BlockSpec(memory_space=pltpu.VMEM)],
        input_output_aliases={i: 3 * nt + i for i in range(2 * nt)},
        compiler_params=pltpu.CompilerParams(has_side_effects=_EFFECT),
    )(*[pltpu.with_memory_space_constraint(a, pltpu.HBM) for a in list(srcs) + list(lands)], after)
    sems, thru = res[:3 * nt], res[3 * nt:5 * nt]
    return [(sems[3 * t], sems[3 * t + 1], sems[3 * t + 2], thru[t], thru[nt + t]) for t in range(nt)], res[-1]


def _gather2_relay(handles, after, name):
    nt = len(handles)

    def body(*refs):
        src, land = refs[:nt], refs[nt:2 * nt]
        send1, recv_ici = refs[2 * nt:3 * nt], refs[3 * nt:4 * nt]
        outs = refs[4 * nt + 1:]
        send2, recv2 = outs[:nt], outs[nt:2 * nt]
        me, sibling, chips = _chip_peers()
        for t in range(nt):
            pltpu.make_async_remote_copy(src_ref=src[t], dst_ref=land[t].at[me], send_sem=send1[t].at[0],
                                         recv_sem=recv_ici[t].at[0], device_id=sibling[0],
                                         device_id_type=pl.DeviceIdType.MESH).wait_send()
            for j, (pid, plin) in enumerate(chips):
                arrived = pltpu.make_async_remote_copy(src_ref=src[t], dst_ref=land[t].at[plin], send_sem=send1[t].at[1 + j],
                                                       recv_sem=recv_ici[t].at[j], device_id=pid,
                                                       device_id_type=pl.DeviceIdType.MESH)
                arrived.wait_send()
                arrived.wait_recv()
                pltpu.make_async_remote_copy(src_ref=land[t].at[plin], dst_ref=land[t].at[plin], send_sem=send2[t].at[j],
                                             recv_sem=recv2[t].at[j], device_id=sibling[0],
                                             device_id_type=pl.DeviceIdType.MESH).start()

    srcs = [h[3] for h in handles]
    lands = [h[4] for h in handles]
    out_shape = [pltpu.SemaphoreType.DMA((N_CHIPS_OTHER,)) for _ in range(2 * nt)]
    out_shape += [pltpu.HBM(a.shape, a.dtype) for a in srcs + lands]
    res = pl.pallas_call(
        body, name=name, out_shape=out_shape,
        in_specs=[_HBM] * (2 * nt) + [_SEM] * (2 * nt) + [_ANY],
        out_specs=[_SEM] * (2 * nt) + [_HBM] * (2 * nt),
        input_output_aliases={i: 2 * nt + i for i in range(2 * nt)},
        compiler_params=pltpu.CompilerParams(has_side_effects=_EFFECT),
    )(*srcs, *lands, *[h[0] for h in handles], *[h[1] for h in handles], after)
    return [(handles[t][2], res[t], res[nt + t], res[3 * nt + t]) for t in range(nt)]


def _gather2_wait(handle, after, name):
    recv_sib, send2, recv2, land = handle

    def body(land_ref, recv_sib_sem, send2_sem, recv2_sem, after_ref, land_out):
        me, sibling, chips = _chip_peers()
        pltpu.make_async_remote_copy(src_ref=land_ref.at[me], dst_ref=land_ref.at[sibling[1]], send_sem=send2_sem.at[0],
                                     recv_sem=recv_sib_sem.at[0], device_id=sibling[0],
                                     device_id_type=pl.DeviceIdType.MESH).wait_recv()
        for j, (pid, plin) in enumerate(chips):
            relayed = pltpu.make_async_remote_copy(src_ref=land_ref.at[plin], dst_ref=land_ref.at[plin], send_sem=send2_sem.at[j],
                                                   recv_sem=recv2_sem.at[j], device_id=sibling[0],
                                                   device_id_type=pl.DeviceIdType.MESH)
            relayed.wait_send()
            relayed.wait_recv()

    return pl.pallas_call(
        body, name=name, out_shape=pltpu.HBM(land.shape, land.dtype),
        in_specs=(_HBM, _SEM, _SEM, _SEM, _ANY), out_specs=_HBM, input_output_aliases={0: 0},
        compiler_params=pltpu.CompilerParams(has_side_effects=_EFFECT),
    )(land, recv_sib, send2, recv2, after)


def _exchange(arrs, scatter, name):
    nt = len(arrs)
    out_shape = [jax.ShapeDtypeStruct(a.shape if scatter else (N_DEV,) + a.shape, a.dtype) for a in arrs]

    def body(*refs):
        ins, outs = refs[:nt], refs[nt:2 * nt]
        send, recv, loc = refs[2 * nt:]
        me, peers = _peers()
        copies = []
        for t in range(nt):
            own = pltpu.make_async_copy(ins[t].at[me] if scatter else ins[t], outs[t].at[me], loc.at[t])
            own.start()
            copies.append(own)
            for k, (pid, plin) in enumerate(peers):
                cp = pltpu.make_async_remote_copy(
                    src_ref=ins[t].at[plin] if scatter else ins[t], dst_ref=outs[t].at[me],
                    send_sem=send.at[t, k], recv_sem=recv.at[t, k],
                    device_id=pid, device_id_type=pl.DeviceIdType.MESH)
                cp.start()
                copies.append(cp)
        for cp in copies:
            cp.wait()

    any_spec = pl.BlockSpec(memory_space=pl.ANY)
    return pl.pallas_call(
        body, out_shape=out_shape, in_specs=[any_spec] * nt, out_specs=[any_spec] * nt,
        scratch_shapes=[pltpu.SemaphoreType.DMA((nt, N_DEV - 1)), pltpu.SemaphoreType.DMA((nt, N_DEV - 1)),
                        pltpu.SemaphoreType.DMA((nt,))],
        name=name)(*arrs)


def _adamw_body(n_parts):
    def body(p_ref, w_ref, m_ref, v_ref, *rest):
        g_out, d_out, m_out, v_out = rest[-4:]
        g = p_ref[0].astype(F32)
        for i in range(1, n_parts):
            g = g + p_ref[i].astype(F32)
        m2 = ADAM_B1 * m_ref[...] + (1.0 - ADAM_B1) * g
        v2 = ADAM_B2 * v_ref[...] + (1.0 - ADAM_B2) * jnp.square(g)
        m_hat = m2 / (1.0 - ADAM_B1 ** ADAM_STEP)
        v_hat = v2 / (1.0 - ADAM_B2 ** ADAM_STEP)
        g_out[...] = g
        d_out[...] = -ADAM_LR * (m_hat / (jnp.sqrt(v_hat) + ADAM_EPS) + ADAM_WD * w_ref[...])
        m_out[...] = m2
        v_out[...] = v2

    return body


def _adamw_layer(parts, w, m, v, layer, prev, name):
    p, r, c = parts.shape
    tr = _tile(r, 256, 8)
    blk = pl.BlockSpec((None, tr, c), lambda i: (layer, i, 0))
    in_specs = [pl.BlockSpec((p, tr, c), lambda i: (0, i, 0)), blk, blk, blk]
    args = [parts, w, m, v]
    aliases = {}
    if prev is not None:
        in_specs += [_ANY] * 4
        args += list(prev)
        aliases = {4 + i: i for i in range(4)}
    return pl.pallas_call(
        _adamw_body(p), grid=(r // tr,), in_specs=in_specs, out_specs=[blk] * 4,
        out_shape=[jax.ShapeDtypeStruct(w.shape, F32)] * 4, input_output_aliases=aliases, name=name,
        compiler_params=_cparams(1))(*args)


def _adamw(parts, w, m, v, name):
    p, nl, r, c = parts.shape
    tr = _tile(r, 256, 8)
    body = _adamw_body(p)

    blk = (None, tr, c)
    imap = lambda l, i: (l, i, 0)
    out = ((nl, r, c), F32, blk, imap)
    return _call(body, grid=(nl, r // tr),
                 ins=[(parts, (p, None, tr, c), lambda l, i: (0, l, i, 0)), (w, blk, imap), (m, blk, imap), (v, blk, imap)],
                 outs=[out] * 4, name=name)


def _rows(x):
    return x.reshape(-1, LANES)


def _pad_rows(x, mult=8):
    r = x.shape[0]
    extra = (-r) % mult
    return jnp.pad(x, ((0, extra), (0, 0))) if extra else x


def _shard_cols(x, me, groups):
    lead = x.shape[:-1]
    xr = x.reshape(lead + (N_DEV, groups * LANES))
    xs = lax.dynamic_index_in_dim(xr, me, axis=len(lead), keepdims=False)
    return xs.reshape(N_DEV, -1, LANES)


def kernel(x, c, ada_w, ada_b, ln_g, ln_b, dn_w_in, dn_conv_w, dn_a_log, dn_dt_bias, dn_norm_w, dn_w_out, cf_w_in, cf_dw_w, cf_dw_b, cf_ln_g, cf_ln_b, cf_w_out, ff_w1, ff_w2, loss_target, m_ada_w, m_ada_b, m_ln_g, m_ln_b, m_dn_w_in, m_dn_conv_w, m_dn_a_log, m_dn_dt_bias, m_dn_norm_w, m_dn_w_out, m_cf_w_in, m_cf_dw_w, m_cf_dw_b, m_cf_ln_g, m_cf_ln_b, m_cf_w_out, m_ff_w1, m_ff_w2, v_ada_w, v_ada_b, v_ln_g, v_ln_b, v_dn_w_in, v_dn_conv_w, v_dn_a_log, v_dn_dt_bias, v_dn_norm_w, v_dn_w_out, v_cf_w_in, v_cf_dw_w, v_cf_dw_b, v_cf_ln_g, v_cf_ln_b, v_cf_w_out, v_ff_w1, v_ff_w2):
    depth, d, _ = ada_w.shape
    n_a, n_b = dn_w_in.shape[0], cf_w_in.shape[0]
    heads = dn_a_log.shape[1]
    hw = heads * HEAD_DIM
    taps = cf_dw_w.shape[1]
    s = x.shape[1]
    alpha = (2.0 * depth) ** 0.25
    me = 4 * lax.axis_index("x") + 2 * lax.axis_index("y") + lax.axis_index("c")
    xs, tgt = x[0], loss_target[0]

    dn_in_cols = dn_w_in.shape[2]
    keys, shards = [], []
    for i in range(depth):
        j = i // 2
        mixer = [("dn_in", dn_w_in), ("dn_out", dn_w_out)] if i % 2 == 0 else [("cf_in", cf_w_in), ("cf_out", cf_w_out)]
        for nm, wt in mixer:
            keys.append((nm, j))
            shards.append(wt[j].astype(MXU_DTYPE))
        keys += [("ff1", i), ("ff2", i)]
        shards += [ff_w1[i].astype(MXU_DTYPE), ff_w2[i].astype(MXU_DTYPE)]

    small_local = [_rows(ln_g), _rows(ln_b), _rows(dn_conv_w), _rows(cf_dw_w), _rows(cf_dw_b), _rows(cf_ln_g),
                   _rows(cf_ln_b), _rows(c)]
    sizes = [a.shape[0] for a in small_local]
    packed = _pad_rows(jnp.concatenate(small_local, axis=0))
    (small_all,) = _exchange([packed], False, "comm_gather_params")
    offs = [0]
    for z in sizes:
        offs.append(offs[-1] + z)

    def small(i):
        return small_all[:, offs[i]:offs[i + 1], :]

    def unshard(piece, lead, groups):
        t = piece.reshape((N_DEV,) + lead + (groups * LANES,))
        t = jnp.moveaxis(t, 0, len(lead))
        return t.reshape(lead + (N_DEV * groups * LANES,))

    ln_g_f = unshard(small(0), (depth, 2), 1)
    ln_b_f = unshard(small(1), (depth, 2), 1)
    conv_w_f = unshard(small(2), (n_a, DN_CONV), 3 * heads // N_DEV)
    dw_w_f = unshard(small(3), (n_b, taps), 1)
    dw_b_f = unshard(small(4), (n_b,), 1)
    cf_ln_g_f = unshard(small(5), (n_b,), 1)
    cf_ln_b_f = unshard(small(6), (n_b,), 1)
    c_all = small(7).reshape(N_DEV, d)

    mod_part, cond_all = _ada_fwd(c_all, ada_w)
    (mod_all,) = _exchange([mod_part], False, "comm_gather_mod")
    mod_mine = lax.dynamic_index_in_dim(mod_all, me, axis=2, keepdims=False)
    mod_mine = jnp.moveaxis(mod_mine, 0, 1).reshape(depth, N_MOD * d)

    handles, token = _gather2_start(shards, [_landing(a, me) for a in shards], mod_all, "gather_weights_start")
    handles = dict(zip(keys, handles))
    groups = [keys[:1], keys[1:4]] + [keys[4 * i:4 * i + 4] for i in range(1, depth)]
    group_of = {k: n for n, grp in enumerate(groups) for k in grp}
    relayed, weights = {}, {}

    def relay(n, after):
        if n < len(groups) and groups[n][0] not in relayed:
            hs = _gather2_relay([handles[k] for k in groups[n]], after, "gather_relay_%d" % n)
            relayed.update(zip(groups[n], hs))

    relay(0, token)

    def gathered(key, after):
        if key not in weights:
            relay(group_of[key], after)
            if key[0] == "ff1":
                relay(key[1] + 2, after)
            weights[key] = _gather2_wait(relayed[key], after, "gather_wait_%s_%d" % key)
        return weights[key]

    def get_dn_in(j):
        def get(after):
            g = gathered(("dn_in", j), after)
            w = jnp.moveaxis(g, 0, 1).reshape(d, N_DEV * dn_in_cols)
            return jnp.pad(w, ((0, 0), (0, 4 * hw + LANES - N_DEV * dn_in_cols)))
        return get

    def get_rows(key):
        return lambda after: gathered(key, after).reshape((-1, d))

    def get_cols(key):
        return lambda after: gathered(key, after)

    def add_bias(a, b):
        return (a + b,), ()

    (mod,), _ = _rowmap(add_bias, [mod_mine + token[0, 0], ada_b], [], [(N_MOD * d, F32)], [], "ada_bias")

    def mod_row(i, j):
        return mod[i:i + 1, j * d:(j + 1) * d]

    def ln_row(arr, i, j):
        return arr[i, j][None, :]

    subs = []
    h_cur = _modulate_fwd(xs, mod_row(0, 1), mod_row(0, 0))
    x_cur = xs
    last = None
    for i in range(depth):
        j = i // 2
        if i % 2 == 0:
            y, res = _deltanet_fwd(h_cur, get_dn_in(j), conv_w_f[j], dn_a_log[j], dn_dt_bias[j], dn_norm_w[j],
                                   get_rows(("dn_out", j)))
        else:
            y, res = _conformer_fwd(h_cur, get_cols(("cf_in", j)), dw_w_f[j], dw_b_f[j][None, :], cf_ln_g_f[j][None, :],
                                    cf_ln_b_f[j][None, :], get_rows(("cf_out", j)))
        p1 = (mod_row(i, 2), ln_row(ln_g_f, i, 0), ln_row(ln_b_f, i, 0), mod_row(i, 4), mod_row(i, 3))
        x_mid, h_mid = _combine_fwd(alpha, x_cur, y, *p1)
        subs.append((x_cur, y, p1, res))
        m_out, res2 = _mlp_fwd(h_mid, get_cols(("ff1", i)), get_rows(("ff2", i)))
        if i + 1 < depth:
            p2 = (mod_row(i, 5), ln_row(ln_g_f, i, 1), ln_row(ln_b_f, i, 1), mod_row(i + 1, 1), mod_row(i + 1, 0))
            x_next, h_next = _combine_fwd(alpha, x_mid, m_out, *p2)
            subs.append((x_mid, m_out, p2, res2))
            x_cur, h_cur = x_next, h_next
        else:
            p2 = (mod_row(i, 5), ln_row(ln_g_f, i, 1), ln_row(ln_b_f, i, 1))
            last = (x_mid, m_out, p2, res2)

    x_in, y_in, p_last, res_last = last
    dx, dy, (loss_acc, g_gt, g_g, g_b) = _last_fwd_bwd(alpha, x_in, y_in, tgt, *p_last)
    loss = lax.psum(loss_acc[0, 0], ("x", "y", "c"))

    d_mod = [[None] * N_MOD for _ in range(depth)]
    d_ln_g = [[None, None] for _ in range(depth)]
    d_ln_b = [[None, None] for _ in range(depth)]
    d_mod[depth - 1][5], d_ln_g[depth - 1][1], d_ln_b[depth - 1][1] = g_gt, g_g, g_b
    gw = dict(dn=[None] * n_a, cf=[None] * n_b)

    sent = {}

    def send_grads(named, tag):
        parts = [p for _, p in named]
        hs, tok = _xfer_start(parts, [_landing_scatter(p, me) for p in parts], True, parts[0], "scatter_start_" + tag)
        for (key, _), hnd in zip(named, hs):
            sent[key] = hnd
        return tok[0, 0]

    def by_rows(g):
        return g.reshape((N_DEV, g.shape[0] // N_DEV, g.shape[1]))

    def send_mlp(i, d_w1, d_w2):
        return send_grads([(("ff1", i), d_w1), (("ff2", i), by_rows(d_w2))], "ff_%d" % i)

    dh, d_w1, d_w2 = _mlp_bwd(res_last, dy)
    pin = send_mlp(depth - 1, d_w1, d_w2)
    for idx in range(len(subs) - 1, -1, -1):
        x_in, y_in, prm, res = subs[idx]
        i, second = idx // 2, idx % 2
        prm = (prm[0] + pin,) + tuple(prm[1:])
        dx, dy, (g_gt, g_g, g_b, g_sc, g_sh) = _combine_bwd(alpha, x_in, y_in, dx, dh, *prm)
        d_mod[i][5 if second else 2], d_ln_g[i][second], d_ln_b[i][second] = g_gt, g_g, g_b
        nxt_i, nxt_base = (i + 1, 0) if second else (i, 3)
        d_mod[nxt_i][nxt_base + 1], d_mod[nxt_i][nxt_base] = g_sc, g_sh
        j = i // 2
        if second:
            dh, d_w1, d_w2 = _mlp_bwd(res, dy)
            pin = send_mlp(i, d_w1, d_w2)
        elif i % 2 == 0:
            dh, gw["dn"][j] = _deltanet_bwd(res, dy, conv_w_f[j])
            d_in = gw["dn"][j]["w_in"][:, :N_DEV * dn_in_cols].reshape(d, N_DEV, dn_in_cols)
            pin = send_grads([(("dn_in", j), jnp.moveaxis(d_in, 1, 0)), (("dn_out", j), by_rows(gw["dn"][j]["w_out"]))],
                             "dn_%d" % j)
        else:
            dh, gw["cf"][j] = _conformer_bwd(res, dy, dw_w_f[j], cf_ln_g_f[j][None, :], cf_ln_b_f[j][None, :])
            pin = send_grads([(("cf_in", j), gw["cf"][j]["w_in"]), (("cf_out", j), by_rows(gw["cf"][j]["w_out"]))],
                             "cf_%d" % j)
    grad_x, g_sc, g_sh = _modulate_bwd(xs, dx, dh, mod_row(0, 1) + pin, mod_row(0, 0))
    d_mod[0][1], d_mod[0][0] = g_sc, g_sh
    d_mod_full = jnp.concatenate([jnp.concatenate(r, axis=1) for r in d_mod], axis=0)

    stacked = {"dn_w_in": ("dn_in", dn_w_in, m_dn_w_in, v_dn_w_in), "dn_w_out": ("dn_out", dn_w_out, m_dn_w_out, v_dn_w_out),
               "cf_w_in": ("cf_in", cf_w_in, m_cf_w_in, v_cf_w_in), "cf_w_out": ("cf_out", cf_w_out, m_cf_w_out, v_cf_w_out),
               "ff_w1": ("ff1", ff_w1, m_ff_w1, v_ff_w1), "ff_w2": ("ff2", ff_w2, m_ff_w2, v_ff_w2)}
    chains = {key: None for key in stacked}

    def update_layer(i):
        mixer = ["dn_w_in", "dn_w_out"] if i % 2 == 0 else ["cf_w_in", "cf_w_out"]
        for key, idx in [("ff_w1", i), ("ff_w2", i)] + [(k, i // 2) for k in mixer]:
            short, w, m, v = stacked[key]
            parts = _xfer_wait(sent[(short, idx)], True, sg_token, "scatter_wait_%s_%d" % (short, idx))
            chains[key] = _adamw_layer(parts, w, m, v, idx, chains[key], "adamw_%s_%d" % (key, idx))

    def stack_rows(lst):
        return jnp.stack(lst, axis=0)

    gs_ln_g = jnp.stack([jnp.concatenate(r, axis=0) for r in d_ln_g], axis=0)
    gs_ln_b = jnp.stack([jnp.concatenate(r, axis=0) for r in d_ln_b], axis=0)
    gs_conv_w = stack_rows([gw["dn"][j]["conv_w"] for j in range(n_a)])
    gs_dw_w = stack_rows([gw["cf"][j]["dw_w"] for j in range(n_b)])
    gs_dw_b = stack_rows([gw["cf"][j]["dw_b"] for j in range(n_b)])
    gs_cf_ln_g = stack_rows([gw["cf"][j]["ln_g"] for j in range(n_b)])
    gs_cf_ln_b = stack_rows([gw["cf"][j]["ln_b"] for j in range(n_b)])
    gs_a_log = stack_rows([_pad_lanes(gw["dn"][j]["a_log"], 0)[0] for j in range(n_a)])
    gs_dt_bias = stack_rows([_pad_lanes(gw["dn"][j]["dt_bias"], 0)[0] for j in range(n_a)])
    gs_norm_w = stack_rows([gw["dn"][j]["norm_w"] for j in range(n_a)])
    small_grads = [gs_ln_g, gs_ln_b, gs_conv_w, gs_dw_w, gs_dw_b, gs_cf_ln_g, gs_cf_ln_b, gs_a_log, gs_dt_bias,
                   gs_norm_w, d_mod_full]
    sg_rows = [_rows(a) for a in small_grads]
    sg_sizes = [a.shape[0] for a in sg_rows]
    sg_packed = _pad_rows(jnp.concatenate(sg_rows, axis=0))
    (sg_handle,), sg_token = _xfer_start([sg_packed], [_landing(sg_packed, me)], False, grad_x, "gather_small_grads_start")
    for i in range(depth - 1, -1, -1):
        update_layer(i)
    sg_all = _xfer_wait(sg_handle, False, chains["ff_w1"][0], "gather_small_grads_wait")
    sg_offs = [0]
    for z in sg_sizes:
        sg_offs.append(sg_offs[-1] + z)

    def sg(i, shape):
        return sg_all[:, sg_offs[i]:sg_offs[i + 1], :].reshape((N_DEV,) + shape)

    dmod_all = sg(10, (depth, N_MOD * d))
    nl = ada_w.shape[2]
    dmod_cols = lax.dynamic_slice_in_dim(dmod_all, me * nl, nl, axis=2)
    g_ada_w = _ada_bwd(cond_all, jnp.moveaxis(dmod_cols, 0, 1))

    outs = {}

    def run_adamw(key, parts, w, m, v):
        shp = w.shape
        as3 = lambda t: t.reshape((-1,) + shp[-2:]) if t.ndim >= 3 else t.reshape((1,) + shp)
        parts3 = parts.reshape((parts.shape[0],) + as3(w).shape)
        res = _adamw(parts3, as3(w), as3(m), as3(v), "adamw_" + key)
        outs[key] = tuple(r.reshape(shp) for r in res)

    run_adamw("ada_w", g_ada_w[None], ada_w, m_ada_w, v_ada_w)

    cgroups = 3 * heads // N_DEV
    shard_parts = [
        _shard_cols(sg(0, (depth, 2, d)), me, 1), _shard_cols(sg(1, (depth, 2, d)), me, 1),
        _shard_cols(sg(2, (n_a, DN_CONV, 3 * hw)), me, cgroups), _shard_cols(sg(3, (n_b, taps, d)), me, 1),
        _shard_cols(sg(4, (n_b, d)), me, 1), _shard_cols(sg(5, (n_b, d)), me, 1), _shard_cols(sg(6, (n_b, d)), me, 1),
    ]
    repl_parts = [sg(7, (n_a, LANES)), sg(8, (n_a, LANES)), sg(9, (n_a, HEAD_DIM)),
                  sg(10, (depth, N_MOD * d)).reshape(N_DEV, -1, LANES)]
    small_parts = shard_parts + repl_parts
    sp_sizes = [a.shape[1] for a in small_parts]
    parts_packed = jnp.concatenate(small_parts, axis=1)
    extra = (-parts_packed.shape[1]) % 8
    parts_packed = jnp.pad(parts_packed, ((0, 0), (0, extra), (0, 0)))

    def pad_heads(t):
        return jnp.pad(t, ((0, 0), (0, LANES - heads)))

    def pack_state(ln_g_, ln_b_, conv_w_, dw_w_, dw_b_, cln_g_, cln_b_, a_log_, dt_b_, norm_w_, ada_b_):
        rows = [_rows(ln_g_), _rows(ln_b_), _rows(conv_w_), _rows(dw_w_), _rows(dw_b_), _rows(cln_g_), _rows(cln_b_),
                pad_heads(a_log_), pad_heads(dt_b_), norm_w_, _rows(ada_b_)]
        return _pad_rows(jnp.concatenate(rows, axis=0))

    w_s = pack_state(ln_g, ln_b, dn_conv_w, cf_dw_w, cf_dw_b, cf_ln_g, cf_ln_b, dn_a_log, dn_dt_bias, dn_norm_w, ada_b)
    m_s = pack_state(m_ln_g, m_ln_b, m_dn_conv_w, m_cf_dw_w, m_cf_dw_b, m_cf_ln_g, m_cf_ln_b, m_dn_a_log,
                     m_dn_dt_bias, m_dn_norm_w, m_ada_b)
    v_s = pack_state(v_ln_g, v_ln_b, v_dn_conv_w, v_cf_dw_w, v_cf_dw_b, v_cf_ln_g, v_cf_ln_b, v_dn_a_log,
                     v_dn_dt_bias, v_dn_norm_w, v_ada_b)
    res_s = _adamw(parts_packed[:, None], w_s[None], m_s[None], v_s[None], "adamw_small")
    sp_offs = [0]
    for z in sp_sizes:
        sp_offs.append(sp_offs[-1] + z)
    small_keys = ["ln_g", "ln_b", "dn_conv_w", "cf_dw_w", "cf_dw_b", "cf_ln_g", "cf_ln_b", "dn_a_log", "dn_dt_bias",
                  "dn_norm_w", "ada_b"]
    small_shapes = [ln_g.shape, ln_b.shape, dn_conv_w.shape, cf_dw_w.shape, cf_dw_b.shape, cf_ln_g.shape,
                    cf_ln_b.shape, dn_a_log.shape, dn_dt_bias.shape, dn_norm_w.shape, ada_b.shape]
    for n, (key, shp) in enumerate(zip(small_keys, small_shapes)):
        vals = []
        for r in res_s:
            piece = r[0, sp_offs[n]:sp_offs[n + 1], :]
            if key in ("dn_a_log", "dn_dt_bias"):
                piece = piece[:, :heads]
            vals.append(piece.reshape(shp))
        outs[key] = tuple(vals)

    for key in stacked:
        outs[key] = tuple(chains[key])

    order = ["ada_w", "ada_b", "ln_g", "ln_b", "dn_w_in", "dn_conv_w", "dn_a_log", "dn_dt_bias", "dn_norm_w",
             "dn_w_out", "cf_w_in", "cf_dw_w", "cf_dw_b", "cf_ln_g", "cf_ln_b", "cf_w_out", "ff_w1", "ff_w2"]
    result = [loss, grad_x[None]]
    for part in range(4):
        result += [outs[k][part] for k in order]
    return tuple(result)
```

```python
import functools

import jax
import jax.numpy as jnp
from jax import lax
from jax.experimental import pallas as pl
from jax.experimental.pallas import tpu as pltpu

F32 = jnp.float32
MXU_DTYPE = jnp.bfloat16
N_DEV = 8
LANES = 128
HEAD_DIM = 128
CHUNK = 64
DN_CONV = 4
N_MOD = 6
LN_EPS = 1e-5
RMS_EPS = 1e-6
L2_EPS = 1e-6
ADAM_LR = 0.001
ADAM_B1 = 0.9
ADAM_B2 = 0.999
ADAM_EPS = 1e-08
ADAM_WD = 0.01
ADAM_STEP = 10

HI = lax.Precision.HIGHEST
NN = ((1,), (0,))
NT = ((1,), (1,))
TN = ((0,), (0,))

ROW_TILE = 512
CONV_TILE = 256
SHORT_CONV_TILE = 1024


def _mdot(a, b, dims=NN):
    return lax.dot_general(a.astype(MXU_DTYPE), b.astype(MXU_DTYPE), (dims, ((), ())), preferred_element_type=F32)


def _split3(x):
    hi = x.astype(MXU_DTYPE)
    r1 = x - hi.astype(F32)
    mid = r1.astype(MXU_DTYPE)
    lo = (r1 - mid.astype(F32)).astype(MXU_DTYPE)
    return hi, mid, lo


def _dot01(a, b, dims=NN, mask_first=True):
    d = lambda p, q: lax.dot_general(p, q, (dims, ((), ())), preferred_element_type=F32)
    if mask_first:
        m = a.astype(MXU_DTYPE)
        return sum(d(m, p) for p in _split3(b))
    m = b.astype(MXU_DTYPE)
    return sum(d(p, m) for p in _split3(a))


def _dot3(a, b, dims=NN):
    ah, am, _ = _split3(a)
    bh, bm, _ = _split3(b)
    d = lambda p, q: lax.dot_general(p, q, (dims, ((), ())), preferred_element_type=F32)
    return d(ah, bh) + (d(ah, bm) + d(am, bh))


def _cparams(n):
    return pltpu.CompilerParams(dimension_semantics=("arbitrary",) * n)


def _call(body, *, grid, ins, outs, name, scratch=()):
    res = pl.pallas_call(
        body,
        grid=grid,
        in_specs=[pl.BlockSpec(memory_space=pl.ANY) if b is None else pl.BlockSpec(b, m) for _, b, m in ins],
        out_specs=[pl.BlockSpec(b, m) for _, _, b, m in outs],
        out_shape=[jax.ShapeDtypeStruct(s, d) for s, d, _, _ in outs],
        scratch_shapes=list(scratch),
        name=name,
        compiler_params=_cparams(len(grid)),
    )(*[a for a, _, _ in ins])
    return res


def _tile(n, pref, unit=LANES):
    if n <= pref:
        return n
    t = (pref // unit) * unit
    while t > unit and n % t:
        t -= unit
    assert n % t == 0, (n, pref)
    return t


def _rowmap(fn, rows, consts, row_outs, acc_outs, name, pin=None):
    rows = [r if isinstance(r, tuple) else (r, r.shape[1], 0) for r in rows]
    s = rows[0][0].shape[0]
    tm = min(ROW_TILE, s)
    nr, nc, no, na = len(rows), len(consts), len(row_outs), len(acc_outs)
    npin = 0 if pin is None else 1

    def body(*refs):
        rin, cin = refs[:nr], refs[nr:nr + nc]
        refs = refs[:nr + nc] + refs[nr + nc + npin:]
        rout, aout = refs[nr + nc:nr + nc + no], refs[nr + nc + no:]
        ro, ao = fn(*[r[...] for r in rin], *[c[...] for c in cin])
        for ref, val in zip(rout, ro):
            ref[...] = val.astype(ref.dtype)
        if na:
            first = pl.program_id(0) == 0

            @pl.when(first)
            def _():
                for ref, val in zip(aout, ao):
                    ref[...] = val

            @pl.when(jnp.logical_not(first))
            def _():
                for ref, val in zip(aout, ao):
                    ref[...] += val

    ins = [(a, (tm, w), functools.partial(lambda i, cb: (i, cb), cb=cb)) for a, w, cb in rows]
    for c in consts:
        if isinstance(c, tuple):
            ins.append((c[0], (None, 1, c[0].shape[2]), functools.partial(lambda i, n: (n, 0, 0), n=c[1])))
        else:
            ins.append((c, c.shape, lambda i: (0, 0)))
    if pin is not None:
        ins.append((pin, None, None))
    outs = [((s, w), d, (tm, w), lambda i: (i, 0)) for w, d in row_outs]
    outs += [(shp, F32, shp, lambda i: (0, 0)) for shp in acc_outs]
    res = _call(body, grid=(s // tm,), ins=ins, outs=outs, name=name)
    return res[:no], res[no:]


def _ln(z, g, b):
    mu = jnp.mean(z, -1, keepdims=True)
    var = jnp.mean(jnp.square(z - mu), -1, keepdims=True)
    return (z - mu) * lax.rsqrt(var + LN_EPS) * g + b


def _combine(alpha, x, y, gt, g, b, sc, sh):
    xn = _ln(alpha * x + (1.0 + gt) * y, g, b)
    return xn, xn * (1.0 + sc) + sh


def _modulate_fwd(x, sc, sh):
    def fn(x, sc, sh):
        return ((x * (1.0 + sc) + sh),), ()

    (h,), _ = _rowmap(fn, [x], [sc, sh], [(x.shape[1], MXU_DTYPE)], [], "modulate_fwd")
    return h


def _modulate_bwd(x, dx, dh, sc, sh, pin=None):
    d = x.shape[1]

    def fn(x, dx, dh, sc, sh):
        _, vjp = jax.vjp(lambda x, sc, sh: x * (1.0 + sc) + sh, x, sc, sh)
        gx, gsc, gsh = vjp(dh)
        return (dx + gx,), (gsc, gsh)

    (gx,), (gsc, gsh) = _rowmap(fn, [x, dx, dh], [sc, sh], [(d, F32)], [(1, d), (1, d)], "modulate_bwd", pin=pin)
    return gx, gsc, gsh


def _combine_fwd(alpha, x, y, gt, g, b, sc, sh):
    d = x.shape[1]

    def fn(x, y, gt, g, b, sc, sh):
        return _combine(alpha, x, y, gt, g, b, sc, sh), ()

    (xn, h), _ = _rowmap(fn, [x, y], [gt, g, b, sc, sh], [(d, F32), (d, MXU_DTYPE)], [], "combine_fwd")
    return xn, h


def _combine_bwd(alpha, x, y, dxn, dh, gt, g, b, sc, sh, pin=None):
    d = x.shape[1]

    def fn(x, y, dxn, dh, gt, g, b, sc, sh):
        _, vjp = jax.vjp(functools.partial(_combine, alpha), x, y, gt, g, b, sc, sh)
        gx, gy, ggt, gg, gb, gsc, gsh = vjp((dxn, dh))
        return (gx, gy), (ggt, gg, gb, gsc, gsh)

    (gx, gy), accs = _rowmap(fn, [x, y, dxn, dh], [gt, g, b, sc, sh], [(d, F32), (d, MXU_DTYPE)],
                             [(1, d)] * 5, "combine_bwd", pin=pin)
    return gx, gy, accs


def _last_fwd_bwd(alpha, x, y, tgt, gt, g, b):
    d = x.shape[1]

    def fn(x, y, tgt, gt, g, b):
        xn, vjp = jax.vjp(lambda x, y, gt, g, b: _ln(alpha * x + (1.0 + gt) * y, g, b), x, y, gt, g, b)
        err = xn - tgt
        gx, gy, ggt, gg, gb = vjp(err * (1.0 / d))
        rows = jnp.sum(jnp.square(err), axis=-1, keepdims=True)
        loss = (0.5 / d) * jnp.sum(rows, axis=0, keepdims=True) * jnp.ones((1, LANES), F32)
        return (gx, gy), (loss, ggt, gg, gb)

    (gx, gy), accs = _rowmap(fn, [x, y, tgt], [gt, g, b], [(d, F32), (d, MXU_DTYPE)],
                             [(1, LANES), (1, d), (1, d), (1, d)], "last_fwd_bwd")
    return gx, gy, accs


MM_VMEM_BUDGET = 40 * 2 ** 20


def _fit(options, cost):
    for o in options:
        if 2 * cost(o) <= MM_VMEM_BUDGET:
            return o
    return options[-1]


def _row_tiles(m):
    return [t for t in (2048, 1024, 512, 256) if t <= m and m % t == 0] or [m]


def _mm_call(a, a_blk, a_map, b, b_blk, b_map, outs, dims, grid, name, epi=None, extra=None, split=None):
    nk = grid[2]
    n_out = len(outs)
    n_in = 3 if extra is not None else 2

    def body(*refs):
        a_ref, b_ref = refs[0], refs[1]
        rest = refs[n_in:]
        out_refs = rest[:n_out]

        def finish(val):
            if epi == "relu2":
                out_refs[0][...] = val.astype(out_refs[0].dtype)
                out_refs[1][...] = jnp.square(jnp.maximum(val, 0.0)).astype(out_refs[1].dtype)
            elif epi == "relu2_bwd":
                out_refs[0][...] = (val * 2.0 * jnp.maximum(refs[2][...], 0.0)).astype(out_refs[0].dtype)
            elif split is not None:
                for g in range(split[0]):
                    out_refs[0][g] = val[:, g * split[1]:(g + 1) * split[1]].astype(out_refs[0].dtype)
            else:
                out_refs[0][...] = val.astype(out_refs[0].dtype)

        p = lax.dot_general(a_ref[...], b_ref[...], (dims, ((), ())), preferred_element_type=F32)
        if nk == 1:
            finish(p)
        else:
            acc = rest[n_out]
            k = pl.program_id(2)

            @pl.when(k == 0)
            def _():
                acc[...] = p

            @pl.when(k > 0)
            def _():
                acc[...] += p

            @pl.when(k == nk - 1)
            def _():
                finish(acc[...])

    if nk > 1:
        out_blk = tuple(x for x in outs[0][2] if x is not None)
        if split is not None:
            out_blk = (out_blk[1], split[0] * split[1])
        scratch = [pltpu.VMEM(out_blk, F32)]
    else:
        scratch = []
    ins = [(a, a_blk, a_map), (b, b_blk, b_map)] + ([extra] if extra is not None else [])
    return _call(body, grid=grid, ins=ins, outs=outs, name=name, scratch=scratch)


def _isz(dt):
    return jnp.dtype(dt).itemsize


def _mm_nn(a, b, out_dtype, name, relu2=False):
    m, kdim = a.shape
    if b.ndim == 2:
        n = b.shape[1]
        tn = _tile(n, 1536 if n > 2048 else 512)
        b_blk, b_map = (kdim, tn), lambda i, j, k: (0, j)
    else:
        g, _, ng = b.shape
        n = g * ng
        tn = _tile(ng, 512)
        b_blk = (None, kdim, tn)
        b_map = functools.partial(lambda i, j, k, npg: (j // npg, 0, j % npg), npg=ng // tn)
    out_bytes = (4 + _isz(out_dtype)) if relu2 else _isz(out_dtype)
    tm = _fit(_row_tiles(m), lambda t: t * kdim * _isz(a.dtype) + kdim * tn * _isz(b.dtype) + t * tn * out_bytes)
    grid = (m // tm, n // tn, 1)
    outs = [((m, n), F32 if relu2 else out_dtype, (tm, tn), lambda i, j, k: (i, j))]
    if relu2:
        outs.append(((m, n), out_dtype, (tm, tn), lambda i, j, k: (i, j)))
    res = _mm_call(a, (tm, kdim), lambda i, j, k: (i, 0), b, b_blk, b_map, outs, NN, grid, name,
                   epi="relu2" if relu2 else None)
    return res if relu2 else res[0]


def _mm_nt(a, b, out_dtype, name, relu2_of=None):
    m, n = a.shape
    extra_bytes = 4 if relu2_of is not None else 0
    if b.ndim == 2:
        kout = b.shape[0]
        to, tc, nk = _tile(kout, 512), n, 1
        b_blk, b_map = (to, tc), lambda i, j, k: (j, 0)
        acc_bytes = 0
    else:
        nk, kout, tc = b.shape
        to = _tile(kout, 1024)
        b_blk, b_map = (None, to, tc), lambda i, j, k: (k, j, 0)
        acc_bytes = 2
    tm = _fit(_row_tiles(m), lambda t: t * tc * _isz(a.dtype) + to * tc * _isz(b.dtype)
              + t * to * (_isz(out_dtype) + extra_bytes + acc_bytes))
    grid = (m // tm, kout // to, nk)
    outs = [((m, kout), out_dtype, (tm, to), lambda i, j, k: (i, j))]
    extra = (relu2_of, (tm, to), lambda i, j, k: (i, j)) if relu2_of is not None else None
    return _mm_call(a, (tm, tc), lambda i, j, k: (i, k), b, b_blk, b_map, outs, NT, grid, name,
                    epi="relu2_bwd" if relu2_of is not None else None, extra=extra)[0]


def _mm_tn(a, b, out_dtype, name, split_cols=False):
    m, kdim = a.shape
    n = b.shape[1]
    tk = _tile(kdim, 512)
    tn = _tile(n, 1536)
    if not split_cols:
        out, split = ((kdim, n), out_dtype, (tk, tn), lambda i, j, k: (i, j)), None
    else:
        ng = n // N_DEV
        if tn % ng:
            tn = _tile(ng, 512)
        if tn >= ng:
            gb = tn // ng
            out = ((N_DEV, kdim, ng), out_dtype, (gb, tk, ng), lambda i, j, k: (j, i, 0))
            split = (gb, ng)
        else:
            out = ((N_DEV, kdim, ng), out_dtype, (None, tk, tn),
                   functools.partial(lambda i, j, k, npg: (j // npg, i, j % npg), npg=ng // tn))
            split = None
    grid = (kdim // tk, n // tn, 1)
    return _mm_call(a, (m, tk), lambda i, j, k: (0, i), b, (m, tn), lambda i, j, k: (0, j), [out], TN, grid, name,
                    split=split)[0]


def _shifted(xa, off, rows):
    if off % 8 == 0:
        return xa[off:off + rows]
    return pltpu.roll(xa, xa.shape[0] - off, 0)[:rows]


def _conv_pad(taps):
    return -(-(taps - 1) // 8) * 8


def _conv_tile(xp_ref, w, i, rows, taps):
    pad = _conv_pad(taps)
    r0 = pl.multiple_of(i * rows, rows)
    xa = xp_ref[pl.ds(r0, rows + pad), :]
    views = [_shifted(xa, pad - (taps - 1) + j, rows) for j in range(taps)]
    acc = w[0:1, :] * views[0]
    for j in range(1, taps):
        acc = acc + w[j:j + 1, :] * views[j]
    return r0, acc, views


def _conv_back_tile(yp_ref, w, i, rows, taps):
    pad = _conv_pad(taps)
    r0 = pl.multiple_of(i * rows, rows)
    ya = yp_ref[pl.ds(r0, rows + pad), :]
    acc = w[taps - 1:taps, :] * ya[:rows]
    for j in range(taps - 1):
        acc = acc + w[j:j + 1, :] * _shifted(ya, taps - 1 - j, rows)
    return r0, acc


def _tap_sums(dy, views, taps):
    row = lax.broadcasted_iota(jnp.int32, (taps, LANES), 0)
    acc = jnp.zeros((taps, LANES), F32)
    for j in range(taps):
        acc = acc + jnp.where(row == j, jnp.sum(dy * views[j], axis=0, keepdims=True), 0.0)
    return acc


def _silu_l2(xc, l2):
    a = jax.nn.silu(xc)
    if l2:
        a = a * lax.rsqrt(jnp.sum(a * a, axis=-1, keepdims=True) + L2_EPS)
    return a


def _dn_conv_fwd(proj, conv_w, c0, nblk, l2, name):
    s = proj.shape[0]
    pad = _conv_pad(DN_CONV)
    rows = min(SHORT_CONV_TILE, s)

    def body(x_ref, w_ref, o_ref, xp):
        xp[0:pad, :] = jnp.zeros((pad, LANES), F32)
        xp[pad:, :] = x_ref[...]
        w = w_ref[...]

        def tile(i, c):
            r0, acc, _ = _conv_tile(xp, w, i, rows, DN_CONV)
            o_ref[pl.ds(r0, rows), :] = _silu_l2(acc, l2)
            return c

        lax.fori_loop(0, s // rows, tile, 0)

    return _call(body, grid=(nblk,),
                 ins=[(proj, (s, LANES), lambda c: (0, c0 + c)), (conv_w, (DN_CONV, LANES), lambda c: (0, c0 + c))],
                 outs=[((nblk, s, LANES), F32, (None, s, LANES), lambda c: (c, 0, 0))],
                 name=name, scratch=[pltpu.VMEM((s + pad, LANES), F32)])[0]


def _dn_conv_bwd(proj, conv_w, da, c0, nblk, l2, name):
    s = proj.shape[0]
    pad = _conv_pad(DN_CONV)
    rows = min(SHORT_CONV_TILE, s)

    def body(x_ref, w_ref, da_ref, dx_ref, dw_ref, xp, yp):
        xp[0:pad, :] = jnp.zeros((pad, LANES), F32)
        xp[pad:, :] = x_ref[...]
        yp[s:, :] = jnp.zeros((pad, LANES), F32)
        w = w_ref[...]

        def tile(i, dw):
            r0, acc, views = _conv_tile(xp, w, i, rows, DN_CONV)
            _, vjp = jax.vjp(functools.partial(_silu_l2, l2=l2), acc)
            (dxc,) = vjp(da_ref[pl.ds(r0, rows), :])
            yp[pl.ds(r0, rows), :] = dxc
            return dw + _tap_sums(dxc, views, DN_CONV)

        dw_ref[...] = lax.fori_loop(0, s // rows, tile, jnp.zeros((DN_CONV, LANES), F32))

        def tile2(i, c):
            r0, acc = _conv_back_tile(yp, w, i, rows, DN_CONV)
            dx_ref[pl.ds(r0, rows), :] = acc.astype(dx_ref.dtype)
            return c

        lax.fori_loop(0, s // rows, tile2, 0)

    return _call(body, grid=(nblk,),
                 ins=[(proj, (s, LANES), lambda c: (0, c0 + c)), (conv_w, (DN_CONV, LANES), lambda c: (0, c0 + c)),
                      (da, (None, s, LANES), lambda c: (c, 0, 0))],
                 outs=[((s, nblk * LANES), MXU_DTYPE, (s, LANES), lambda c: (0, c)),
                       ((DN_CONV, nblk * LANES), F32, (DN_CONV, LANES), lambda c: (0, c))],
                 name=name, scratch=[pltpu.VMEM((s + pad, LANES), F32), pltpu.VMEM((s + pad, LANES), F32)])


def _cf_conv_fwd(vg, dw_w, dw_b):
    s, c2 = vg.shape
    ch = c2 // 2
    nblk = ch // LANES
    taps = dw_w.shape[0]
    pad = _conv_pad(taps)
    rows = min(CONV_TILE, s)

    def body(v_ref, g_ref, w_ref, b_ref, o_ref, xp):
        xp[0:pad, :] = jnp.zeros((pad, LANES), F32)
        xp[pad:, :] = v_ref[...] * jax.nn.sigmoid(g_ref[...])
        w = w_ref[...]
        bias = b_ref[...]

        def tile(i, c):
            r0, acc, _ = _conv_tile(xp, w, i, rows, taps)
            o_ref[pl.ds(r0, rows), :] = acc + bias
            return c

        lax.fori_loop(0, s // rows, tile, 0)

    return _call(body, grid=(nblk,),
                 ins=[(vg, (s, LANES), lambda c: (0, c)), (vg, (s, LANES), lambda c: (0, nblk + c)),
                      (dw_w, (taps, LANES), lambda c: (0, c)), (dw_b, (1, LANES), lambda c: (0, c))],
                 outs=[((s, ch), F32, (s, LANES), lambda c: (0, c))],
                 name="cf_conv_fwd", scratch=[pltpu.VMEM((s + pad, LANES), F32)])[0]


def _cf_conv_bwd(vg, dw_w, du):
    s, c2 = vg.shape
    ch = c2 // 2
    nblk = ch // LANES
    taps = dw_w.shape[0]
    pad = _conv_pad(taps)
    rows = min(CONV_TILE, s)

    def body(v_ref, g_ref, w_ref, du_ref, dv_ref, dg_ref, dw_ref, db_ref, xp, yp):
        sig = jax.nn.sigmoid(g_ref[...])
        xp[0:pad, :] = jnp.zeros((pad, LANES), F32)
        xp[pad:, :] = v_ref[...] * sig
        yp[0:s, :] = du_ref[...]
        yp[s:, :] = jnp.zeros((pad, LANES), F32)
        w = w_ref[...]
        db_ref[...] = jnp.sum(du_ref[...], axis=0, keepdims=True)

        def tile(i, dw):
            r0, _, views = _conv_tile(xp, w, i, rows, taps)
            return dw + _tap_sums(du_ref[pl.ds(r0, rows), :], views, taps)

        dw_ref[...] = lax.fori_loop(0, s // rows, tile, jnp.zeros((taps, LANES), F32))

        def tile2(i, c):
            r0, du0 = _conv_back_tile(yp, w, i, rows, taps)
            val = v_ref[pl.ds(r0, rows), :]
            sg = jax.nn.sigmoid(g_ref[pl.ds(r0, rows), :])
            dv_ref[pl.ds(r0, rows), :] = (du0 * sg).astype(dv_ref.dtype)
            dg_ref[pl.ds(r0, rows), :] = (du0 * val * sg * (1.0 - sg)).astype(dg_ref.dtype)
            return c

        lax.fori_loop(0, s // rows, tile2, 0)

    return _call(body, grid=(nblk,),
                 ins=[(vg, (s, LANES), lambda c: (0, c)), (vg, (s, LANES), lambda c: (0, nblk + c)),
                      (dw_w, (taps, LANES), lambda c: (0, c)), (du, (s, LANES), lambda c: (0, c))],
                 outs=[((s, ch), MXU_DTYPE, (s, LANES), lambda c: (0, c)),
                       ((s, ch), MXU_DTYPE, (s, LANES), lambda c: (0, c)),
                       ((taps, ch), F32, (taps, LANES), lambda c: (0, c)),
                       ((1, ch), F32, (1, LANES), lambda c: (0, c))],
                 name="cf_conv_bwd", scratch=[pltpu.VMEM((s + pad, LANES), F32), pltpu.VMEM((s + pad, LANES), F32)])


def _masks():
    r = lax.broadcasted_iota(jnp.int32, (CHUNK, CHUNK), 0)
    c = lax.broadcasted_iota(jnp.int32, (CHUNK, CHUNK), 1)
    return r >= c, r > c, r <= c


def _chunk_decay(g):
    causal, _, upper = _masks()
    gb = jnp.broadcast_to(g, (CHUNK, CHUNK))
    gam_r = _dot01(jnp.where(causal, 1.0, 0.0), gb)
    gam_s = _dot01(jnp.ones((CHUNK, CHUNK), F32), jnp.where(upper, gb, 0.0))
    dm = jnp.where(causal, jnp.exp(jnp.where(causal, gam_r - gam_s, 0.0)), 0.0)
    return gam_r[:, 0:1], dm


def _chunk_scores(q, k, beta, dm):
    _, strict, _ = _masks()
    both = _mdot(jnp.concatenate([k * beta, q * (HEAD_DIM ** -0.5)], axis=0), k, NT)
    return jnp.where(strict, both[:CHUNK] * dm, 0.0), both[CHUNK:] * dm


def _lockstep(gens):
    results = [None] * len(gens)
    alive = list(range(len(gens)))
    while alive:
        for i in list(alive):
            try:
                next(gens[i])
            except StopIteration as stop:
                results[i] = stop.value
                alive.remove(i)
    return results


def _chunk_prep_bwd(q, k, v, beta, gam, t, du, dw, daqk, dqd, dkd, dgl):
    causal, strict, _ = _masks()
    r = lax.broadcasted_iota(jnp.int32, (CHUNK, CHUNK), 0)
    c = lax.broadcasted_iota(jnp.int32, (CHUNK, CHUNK), 1)
    scale = HEAD_DIM ** -0.5
    eg = jnp.exp(gam)
    gam_last = gam[CHUNK - 1:CHUNK, :]
    rr = jnp.exp(gam_last - gam)
    kb = k * beta
    qs = q * scale
    vb = v * beta
    kbe = kb * eg
    gam_b = jnp.broadcast_to(gam, (CHUNK, CHUNK))
    gam_s = _dot01(jnp.ones((CHUNK, CHUNK), F32), jnp.where(r == c, gam_b, 0.0))
    both = _mdot(jnp.concatenate([kb, qs], axis=0), k, NT)
    duw = jnp.concatenate([du, dw], axis=1)
    dt = _mdot(duw, jnp.concatenate([vb, kbe], axis=1), NT)
    dvk = _mdot(t, duw, TN)
    yield
    dm = jnp.where(causal, jnp.exp(jnp.where(causal, gam_b - gam_s, 0.0)), 0.0)
    a = jnp.where(strict, both[:CHUNK] * dm, 0.0)
    aqk = both[CHUNK:] * dm
    dvb, dkbe = dvk[:, :HEAD_DIM], dvk[:, HEAD_DIM:]
    x = _dot3(t, dt, TN)
    yield
    da = jnp.where(strict, -_dot3(x, t, NT), 0.0)
    yield
    dkk = da * dm
    dqk = daqk * dm
    ddiff = da * a + daqk * aqk
    dboth = jnp.concatenate([dkk, dqk], axis=0)
    dkq = _mdot(dboth, k)
    dk_mm = _mdot(dboth, jnp.concatenate([kb, qs], axis=0), TN)
    colsum = _dot01(ddiff, jnp.ones((CHUNK, LANES), F32), TN, mask_first=False)[:, 0:1]
    yield
    dkb = dkq[:CHUNK] + dkbe * eg
    dk = dk_mm + dkb * beta + dkd * rr
    dq = (dkq[CHUNK:] + dqd * eg) * scale
    dbeta = jnp.sum(dkb * k, axis=-1, keepdims=True) + jnp.sum(dvb * v, axis=-1, keepdims=True)
    dv = dvb * beta
    deg = jnp.sum(dkbe * kb, axis=-1, keepdims=True) + jnp.sum(dqd * qs, axis=-1, keepdims=True)
    drr = jnp.sum(dkd * k, axis=-1, keepdims=True)
    dgam = deg * eg - drr * rr + jnp.sum(ddiff, axis=-1, keepdims=True) - colsum
    dgam_last = jnp.sum(drr * rr, axis=0, keepdims=True) + dgl[0:1, :] * jnp.exp(gam_last)
    row = lax.broadcasted_iota(jnp.int32, (CHUNK, 1), 0)
    dgam = dgam + jnp.where(row == CHUNK - 1, dgam_last, 0.0)
    dg = _dot01(jnp.where(causal, 1.0, 0.0), jnp.broadcast_to(dgam, (CHUNK, LANES)), TN)[:, 0:1]
    return dq, dk, dv, dbeta, dg


def _prep_group(s):
    nch = s // CHUNK
    return next(c for c in (16, 8, 4, 2, 1) if nch % c == 0)


def _tri_solve_lanes(a_l):
    n = a_l.shape[1]
    group = 8

    def body(a_ref, t_ref):
        t_ref[...] = jnp.zeros_like(t_ref)
        col = lax.broadcasted_iota(jnp.int32, (CHUNK, n), 0)

        def row(r, carry):
            r0 = pl.multiple_of(r * CHUNK, CHUNK)

            def inner(sg, acc):
                a8 = a_ref[pl.ds(r0 + pl.multiple_of(sg * group, group), group), :]
                for j in range(group):
                    t0 = pl.multiple_of((sg * group + j) * CHUNK, CHUNK)
                    acc = acc + a8[j:j + 1, :] * t_ref[pl.ds(t0, CHUNK), :]
                return acc

            acc = lax.fori_loop(0, (r + group - 1) // group, inner, jnp.zeros((CHUNK, n), F32))
            t_ref[pl.ds(r0, CHUNK), :] = jnp.where(col == r, 1.0, 0.0) - acc
            return carry

        lax.fori_loop(0, CHUNK, row, 0)

    return pl.pallas_call(body, out_shape=jax.ShapeDtypeStruct(a_l.shape, F32), name="dn_tri_solve")(a_l)


def _head_cols(bg, hh, heads):
    lane = lax.broadcasted_iota(jnp.int32, bg.shape, 1)
    beta = jnp.sum(jnp.where(lane == hh, bg, 0.0), axis=-1, keepdims=True)
    g = jnp.sum(jnp.where(lane == heads + hh, bg, 0.0), axis=-1, keepdims=True)
    return beta, g


def _dn_prep(q, k, v, bg):
    h, s, _ = q.shape
    cb = _prep_group(s)
    rb = cb * CHUNK
    big = lambda x: (x, (None, rb, HEAD_DIM), lambda n, hh: (hh, n, 0))
    sq = lambda x: (x, (None, rb, CHUNK), lambda n, hh: (hh, n, 0))
    col = lambda x: (x, (None, rb, 1), lambda n, hh: (hh, n, 0))
    tok = (bg, (rb, LANES), lambda n, hh: (n, 0))
    o_big = ((h, s, HEAD_DIM), F32, (None, rb, HEAD_DIM), lambda n, hh: (hh, n, 0))
    o_sq = ((h, s, CHUNK), F32, (None, rb, CHUNK), lambda n, hh: (hh, n, 0))
    o_col = ((h, s, 1), F32, (None, rb, 1), lambda n, hh: (hh, n, 0))

    def scores(q_ref, k_ref, bg_ref, a_ref, aqk_ref, gam_ref):
        beta, g = _head_cols(bg_ref[...], pl.program_id(1), h)
        for i in range(cb):
            sl = slice(i * CHUNK, (i + 1) * CHUNK)
            gam, dm = _chunk_decay(g[sl])
            a_ref[sl, :], aqk_ref[sl, :] = _chunk_scores(q_ref[sl, :], k_ref[sl, :], beta[sl], dm)
            gam_ref[sl, :] = gam

    a, aqk, gam = _call(scores, grid=(s // rb, h), ins=[big(q), big(k), tok], outs=[o_sq, o_sq, o_col],
                        name="dn_scores")
    n_prob = h * (s // CHUNK)
    t_l = _tri_solve_lanes(jnp.transpose(a.reshape(n_prob, CHUNK * CHUNK)))
    t = jnp.transpose(t_l).reshape(h, s, CHUNK)

    def wy(k_ref, v_ref, bg_ref, gam_ref, t_ref, u_ref, w_ref):
        beta, _ = _head_cols(bg_ref[...], pl.program_id(1), h)
        for i in range(cb):
            sl = slice(i * CHUNK, (i + 1) * CHUNK)
            kb = k_ref[sl, :] * beta[sl]
            rhs = jnp.concatenate([v_ref[sl, :] * beta[sl], kb * jnp.exp(gam_ref[sl, :])], axis=1)
            uw = _mdot(t_ref[sl, :], rhs)
            u_ref[sl, :] = uw[:, :HEAD_DIM]
            w_ref[sl, :] = uw[:, HEAD_DIM:]

    u, w = _call(wy, grid=(s // rb, h), ins=[big(k), big(v), tok, col(gam), sq(t)], outs=[o_big, o_big],
                 name="dn_wy")
    return u, w, aqk, t, gam


def _dn_prep_bwd(q, k, v, bg, gam, t, du, dw, daqk, dqd, dkd, dgl):
    h, s, _ = q.shape
    cb = _prep_group(s)
    rb = cb * CHUNK

    def body(q_ref, k_ref, v_ref, bg_ref, g_ref, t_ref, du_ref, dw_ref, da_ref, dqd_ref, dkd_ref, dgl_ref,
             dq_ref, dk_ref, dv_ref, dbg_ref):
        hh = pl.program_id(1)
        beta, _ = _head_cols(bg_ref[...], hh, h)
        slices = [slice(i * CHUNK, (i + 1) * CHUNK) for i in range(cb)]
        results = _lockstep([_chunk_prep_bwd(
            q_ref[sl, :], k_ref[sl, :], v_ref[sl, :], beta[sl], g_ref[sl, :], t_ref[sl, :],
            du_ref[sl, :], dw_ref[sl, :], da_ref[sl, :], dqd_ref[sl, :], dkd_ref[sl, :], dgl_ref[sl, :])
            for sl in slices])

        @pl.when(hh == 0)
        def _():
            dbg_ref[...] = jnp.zeros_like(dbg_ref)

        lane = lax.broadcasted_iota(jnp.int32, (CHUNK, LANES), 1)
        for sl, (dq, dk, dv, dbeta, dg) in zip(slices, results):
            dq_ref[sl, :] = dq
            dk_ref[sl, :] = dk
            dv_ref[sl, :] = dv
            dbg_ref[sl, :] += jnp.where(lane == hh, dbeta, 0.0) + jnp.where(lane == h + hh, dg, 0.0)

    big = lambda x: (x, (None, rb, HEAD_DIM), lambda n, hh: (hh, n, 0))
    sq = lambda x: (x, (None, rb, CHUNK), lambda n, hh: (hh, n, 0))
    col = lambda x: (x, (None, rb, 1), lambda n, hh: (hh, n, 0))
    tok = (bg, (rb, LANES), lambda n, hh: (n, 0))
    o_big = ((h, s, HEAD_DIM), F32, (None, rb, HEAD_DIM), lambda n, hh: (hh, n, 0))
    return _call(body, grid=(s // rb, h),
                 ins=[big(q), big(k), big(v), tok, col(gam), sq(t), big(du), big(dw), sq(daqk), big(dqd), big(dkd),
                      col(dgl)],
                 outs=[o_big, o_big, o_big, ((s, LANES), F32, (rb, LANES), lambda n, hh: (n, 0))], name="dn_prep_bwd")


def _chunk_scaled(q, k, gam):
    gam_last = gam[CHUNK - 1:CHUNK, :]
    q_dec = q * (HEAD_DIM ** -0.5) * jnp.exp(gam)
    k_dec = k * jnp.exp(gam_last - gam)
    return q_dec, k_dec, jnp.exp(gam_last)


def _scan_group(s):
    return 2 if (s // CHUNK) % 2 == 0 else 1


def _dn_scan(q, k, u, w, aqk, gam):
    h, s, _ = q.shape
    nch = s // CHUNK
    sg = _scan_group(s)
    rb = sg * CHUNK

    def body(q_ref, k_ref, u_ref, w_ref, a_ref, gam_ref, o_ref, st_ref, state):
        @pl.when(pl.program_id(0) == 0)
        def _():
            state[...] = jnp.zeros_like(state)

        def head(hh, c):
            sl = slice(c * CHUNK, (c + 1) * CHUNK)
            s0 = state[hh]
            st_ref[c, hh] = s0
            q_dec, k_dec, gl = _chunk_scaled(q_ref[hh, sl, :], k_ref[hh, sl, :], gam_ref[hh, sl, :])
            both = _mdot(jnp.concatenate([w_ref[hh, sl, :], q_dec], axis=0), s0)
            yield
            v_new = u_ref[hh, sl, :] - both[:CHUNK]
            o_ref[sl, hh * HEAD_DIM:(hh + 1) * HEAD_DIM] = both[CHUNK:] + _mdot(a_ref[hh, sl, :], v_new)
            state[hh] = s0 * gl + _mdot(k_dec, v_new, TN)

        for c in range(sg):
            _lockstep([head(hh, c) for hh in range(h)])

    big = lambda x: (x, (h, rb, HEAD_DIM), lambda n: (0, n, 0))
    return _call(body, grid=(nch // sg,),
                 ins=[big(q), big(k), big(u), big(w), (aqk, (h, rb, CHUNK), lambda n: (0, n, 0)),
                      (gam, (h, rb, 1), lambda n: (0, n, 0))],
                 outs=[((s, h * HEAD_DIM), F32, (rb, h * HEAD_DIM), lambda n: (n, 0)),
                       ((nch, h, HEAD_DIM, HEAD_DIM), F32, (sg, h, HEAD_DIM, HEAD_DIM), lambda n: (n, 0, 0, 0))],
                 name="dn_scan", scratch=[pltpu.VMEM((h, HEAD_DIM, HEAD_DIM), F32)])


def _dn_scan_bwd(q, k, u, w, aqk, gam, states, do):
    h, s, _ = q.shape
    nch = s // CHUNK
    sg = _scan_group(s)
    rb = sg * CHUNK
    ngr = nch // sg

    def body(q_ref, k_ref, u_ref, w_ref, a_ref, gam_ref, st_ref, do_ref,
             du_ref, dw_ref, da_ref, dqd_ref, dkd_ref, dgl_ref, dstate):
        @pl.when(pl.program_id(0) == 0)
        def _():
            dstate[...] = jnp.zeros_like(dstate)

        def head(hh, c):
            sl = slice(c * CHUNK, (c + 1) * CHUNK)
            s0 = st_ref[c, hh]
            ds = dstate[hh]
            doh = do_ref[sl, hh * HEAD_DIM:(hh + 1) * HEAD_DIM]
            wv = w_ref[hh, sl, :]
            q_dec, k_dec, gl = _chunk_scaled(q_ref[hh, sl, :], k_ref[hh, sl, :], gam_ref[hh, sl, :])
            ws = _mdot(wv, s0)
            dv_new = _mdot(a_ref[hh, sl, :], doh, TN) + _mdot(k_dec, ds)
            dqd_ref[hh, sl, :] = _mdot(doh, s0, NT)
            qdo = _mdot(q_dec, doh, TN)
            tot = jnp.sum(jnp.sum(s0 * ds, axis=-1, keepdims=True), axis=0, keepdims=True)
            dgl_ref[hh, sl, :] = jnp.broadcast_to(tot, (CHUNK, 1))
            yield
            v_new = u_ref[hh, sl, :] - ws
            du_ref[hh, sl, :] = dv_new
            dw_ref[hh, sl, :] = -_mdot(dv_new, s0, NT)
            da_ref[hh, sl, :] = _mdot(doh, v_new, NT)
            dkd_ref[hh, sl, :] = _mdot(v_new, ds, NT)
            dstate[hh] = ds * gl + qdo - _mdot(wv, dv_new, TN)

        for c in range(sg - 1, -1, -1):
            _lockstep([head(hh, c) for hh in range(h)])

    rev = lambda n: (0, ngr - 1 - n, 0)
    big = lambda x: (x, (h, rb, HEAD_DIM), rev)
    o_big = ((h, s, HEAD_DIM), F32, (h, rb, HEAD_DIM), rev)
    return _call(body, grid=(ngr,),
                 ins=[big(q), big(k), big(u), big(w), (aqk, (h, rb, CHUNK), rev), (gam, (h, rb, 1), rev),
                      (states, (sg, h, HEAD_DIM, HEAD_DIM), lambda n: (ngr - 1 - n, 0, 0, 0)),
                      (do, (rb, h * HEAD_DIM), lambda n: (ngr - 1 - n, 0))],
                 outs=[o_big, o_big, ((h, s, CHUNK), F32, (h, rb, CHUNK), rev), o_big, o_big,
                       ((h, s, 1), F32, (h, rb, 1), rev)],
                 name="dn_scan_bwd", scratch=[pltpu.VMEM((h, HEAD_DIM, HEAD_DIM), F32)])


def _gates(x, a_log, dt_b, h):
    lane = lax.broadcasted_iota(jnp.int32, x.shape, 1)
    return jnp.where(lane < h, jax.nn.sigmoid(x), -jnp.exp(a_log) * jax.nn.softplus(x + dt_b))


def _head_out(oh, zh, nw):
    on = oh * lax.rsqrt(jnp.mean(oh * oh, axis=-1, keepdims=True) + RMS_EPS) * nw
    return on * jax.nn.silu(zh)


def _pad_lanes(x, lo):
    return jnp.zeros((1, LANES), F32).at[0, lo:lo + x.shape[0]].set(x)


def _deltanet_fwd(hin, get_w_in, conv_w, a_log, dt_bias, norm_w, get_w_out):
    h = a_log.shape[0]
    hw = h * HEAD_DIM
    w_in = get_w_in(hin)
    proj = _mm_nn(hin, w_in, F32, "dn_proj")
    q = _dn_conv_fwd(proj, conv_w, 0, h, True, "dn_conv_q")
    k = _dn_conv_fwd(proj, conv_w, h, h, True, "dn_conv_k")
    v = _dn_conv_fwd(proj, conv_w, 2 * h, h, False, "dn_conv_v")
    alp, dtp = _pad_lanes(a_log, h), _pad_lanes(dt_bias, h)

    def gates_fn(x, al, db):
        return (_gates(x, al, db, h),), ()

    (bg,), _ = _rowmap(gates_fn, [(proj, LANES, 4 * h)], [alp, dtp], [(LANES, F32)], [], "dn_gates")
    u, w, aqk, t, gam = _dn_prep(q, k, v, bg)
    o, states = _dn_scan(q, k, u, w, aqk, gam)
    nw = norm_w[None, :]

    def out_fn(o, z, nw):
        parts = [_head_out(o[:, i * HEAD_DIM:(i + 1) * HEAD_DIM], z[:, i * HEAD_DIM:(i + 1) * HEAD_DIM], nw)
                 for i in range(h)]
        return (jnp.concatenate(parts, axis=-1),), ()

    (og,), _ = _rowmap(out_fn, [o, (proj, hw, 3)], [nw], [(hw, MXU_DTYPE)], [], "dn_out")
    w_out = get_w_out(og)
    y = _mm_nn(og, w_out, F32, "dn_y")
    return y, (hin, proj, q, k, v, bg, u, w, aqk, t, gam, states, o, og, alp, dtp, nw, w_in, w_out)


def _deltanet_bwd(res, dy, conv_w):
    hin, proj, q, k, v, bg, u, w, aqk, t, gam, states, o, og, alp, dtp, nw, w_in, w_out = res
    h = q.shape[0]
    hw = h * HEAD_DIM
    s = hin.shape[0]
    d_w_out = _mm_tn(og, dy, MXU_DTYPE, "dn_dwout")
    dog = _mm_nt(dy, w_out, F32, "dn_dog")

    def out_bwd(o, z, dog, nw):
        dos, dzs = [], []
        dn = jnp.zeros((1, HEAD_DIM), F32)
        for i in range(h):
            sl = slice(i * HEAD_DIM, (i + 1) * HEAD_DIM)
            _, vjp = jax.vjp(_head_out, o[:, sl], z[:, sl], nw)
            a, b, c = vjp(dog[:, sl])
            dos.append(a)
            dzs.append(b)
            dn = dn + c
        return (jnp.concatenate(dos, axis=-1), jnp.concatenate(dzs, axis=-1)), (dn,)

    (do, dz), (d_norm_w,) = _rowmap(out_bwd, [o, (proj, hw, 3), dog], [nw], [(hw, F32), (hw, MXU_DTYPE)],
                                    [(1, HEAD_DIM)], "dn_out_bwd")
    du, dw, daqk, dqd, dkd, dgl = _dn_scan_bwd(q, k, u, w, aqk, gam, states, do)
    dq, dk, dv, dbg = _dn_prep_bwd(q, k, v, bg, gam, t, du, dw, daqk, dqd, dkd, dgl)
    dpq, dwq = _dn_conv_bwd(proj, conv_w, dq, 0, h, True, "dn_conv_q_bwd")
    dpk, dwk = _dn_conv_bwd(proj, conv_w, dk, h, h, True, "dn_conv_k_bwd")
    dpv, dwv = _dn_conv_bwd(proj, conv_w, dv, 2 * h, h, False, "dn_conv_v_bwd")

    def gates_bwd(x, dbg, al, db):
        _, vjp = jax.vjp(functools.partial(_gates, h=h), x, al, db)
        gx, gal, gdb = vjp(dbg)
        return (gx,), (gal, gdb)

    (dba,), (d_alp, d_dtp) = _rowmap(gates_bwd, [(proj, LANES, 4 * h), dbg], [alp, dtp], [(LANES, MXU_DTYPE)],
                                     [(1, LANES), (1, LANES)], "dn_gates_bwd")
    dproj = jnp.concatenate([dpq, dpk, dpv, dz, dba], axis=1)
    d_w_in = _mm_tn(hin, dproj, MXU_DTYPE, "dn_dwin")
    dh = _mm_nt(dproj, w_in, F32, "dn_dh")
    d_conv_w = jnp.concatenate([dwq, dwk, dwv], axis=1)
    return dh, dict(w_in=d_w_in, w_out=d_w_out, conv_w=d_conv_w, a_log=d_alp[0, h:2 * h], dt_bias=d_dtp[0, h:2 * h],
                    norm_w=d_norm_w[0])


def _ln_silu(u, g, b):
    return jax.nn.silu(_ln(u, g, b))


def _conformer_fwd(hin, get_w_in, dw_w, dw_b, ln_g, ln_b, get_w_out):
    w_in = get_w_in(hin)
    vg = _mm_nn(hin, w_in, F32, "cf_vg")
    u1 = _cf_conv_fwd(vg, dw_w, dw_b)
    ch = u1.shape[1]

    def fn(u, g, b):
        return (_ln_silu(u, g, b),), ()

    (u2,), _ = _rowmap(fn, [u1], [ln_g, ln_b], [(ch, MXU_DTYPE)], [], "cf_ln")
    w_out = get_w_out(u2)
    y = _mm_nn(u2, w_out, F32, "cf_y")
    return y, (hin, vg, u1, u2, w_in, w_out)


def _conformer_bwd(res, dy, dw_w, ln_g, ln_b):
    hin, vg, u1, u2, w_in, w_out = res
    ch = u1.shape[1]
    d_w_out = _mm_tn(u2, dy, MXU_DTYPE, "cf_dwout")
    du2 = _mm_nt(dy, w_out, F32, "cf_du2")

    def fn(u, du2, g, b):
        _, vjp = jax.vjp(_ln_silu, u, g, b)
        gu, gg, gb = vjp(du2)
        return (gu,), (gg, gb)

    (du1,), (d_ln_g, d_ln_b) = _rowmap(fn, [u1, du2], [ln_g, ln_b], [(ch, F32)], [(1, ch), (1, ch)], "cf_ln_bwd")
    dval, dgate, d_dw_w, d_dw_b = _cf_conv_bwd(vg, dw_w, du1)
    dvg = jnp.concatenate([dval, dgate], axis=1)
    d_w_in = _mm_tn(hin, dvg, MXU_DTYPE, "cf_dwin", split_cols=True)
    dh = _mm_nt(dvg, w_in, F32, "cf_dh")
    return dh, dict(w_in=d_w_in, w_out=d_w_out, dw_w=d_dw_w, dw_b=d_dw_b[0], ln_g=d_ln_g[0], ln_b=d_ln_b[0])


def _mlp_fwd(hin, get_w1, get_w2):
    w1 = get_w1(hin)
    a, r = _mm_nn(hin, w1, MXU_DTYPE, "ff_a", relu2=True)
    w2 = get_w2(r)
    m = _mm_nn(r, w2, F32, "ff_m")
    return m, (hin, a, r, w1, w2)


def _mlp_bwd(res, dm):
    hin, a, r, w1, w2 = res
    d_w2 = _mm_tn(r, dm, MXU_DTYPE, "ff_dw2")
    da = _mm_nt(dm, w2, MXU_DTYPE, "ff_da", relu2_of=a)
    d_w1 = _mm_tn(hin, da, MXU_DTYPE, "ff_dw1", split_cols=True)
    dh = _mm_nt(da, w1, F32, "ff_dh")
    return dh, d_w1, d_w2


def _ada_fwd(c_all, ada_w):
    depth, d, nl = ada_w.shape
    tn = _tile(nl, 256)

    def body(c_ref, w_ref, o_ref, cond_ref):
        cond = jax.nn.silu(c_ref[...]).astype(MXU_DTYPE)
        cond_ref[...] = cond
        o_ref[...] = lax.dot_general(cond, w_ref[...].astype(MXU_DTYPE), (NN, ((), ())), preferred_element_type=F32)

    return _call(body, grid=(depth, nl // tn),
                 ins=[(c_all, c_all.shape, lambda l, j: (0, 0)), (ada_w, (None, d, tn), lambda l, j: (l, 0, j))],
                 outs=[((depth, N_DEV, nl), F32, (None, N_DEV, tn), lambda l, j: (l, 0, j)),
                       (c_all.shape, MXU_DTYPE, c_all.shape, lambda l, j: (0, 0))],
                 name="ada_fwd")


def _ada_bwd(cond_all, dmod_cols):
    depth, _, nl = dmod_cols.shape
    d = cond_all.shape[1]
    tn = _tile(nl, 256)

    def body(c_ref, g_ref, o_ref):
        o_ref[...] = lax.dot_general(c_ref[...], g_ref[...].astype(MXU_DTYPE), (TN, ((), ())),
                                     preferred_element_type=F32)

    return _call(body, grid=(depth, nl // tn),
                 ins=[(cond_all, cond_all.shape, lambda l, j: (0, 0)), (dmod_cols, (None, N_DEV, tn), lambda l, j: (l, 0, j))],
                 outs=[((depth, d, nl), F32, (None, d, tn), lambda l, j: (l, 0, j))], name="ada_bwd")[0]


def _peers():
    x, y, c = lax.axis_index("x"), lax.axis_index("y"), lax.axis_index("c")
    peers = []
    for k in range(1, N_DEV):
        px = 1 - x if k & 4 else x
        py = 1 - y if k & 2 else y
        pc = 1 - c if k & 1 else c
        peers.append(((px, py, pc), 4 * px + 2 * py + pc))
    return 4 * x + 2 * y + c, peers


_HBM = pl.BlockSpec(memory_space=pltpu.HBM)
_SEM = pl.BlockSpec(memory_space=pltpu.SEMAPHORE)
_ANY = pl.BlockSpec(memory_space=pl.ANY)
_EFFECT = pltpu.SideEffectType.DATAFLOW_SIDE_EFFECTING


def _xfer_start(srcs, lands, scatter, after, name):
    nt = len(srcs)

    def body(*refs):
        src, land = refs[:nt], refs[nt:2 * nt]
        sems = refs[2 * nt + 1:4 * nt + 1]
        token = refs[-1]
        me, peers = _peers()
        for t in range(nt):
            for k, (pid, plin) in enumerate(peers):
                pltpu.make_async_remote_copy(
                    src_ref=src[t].at[plin] if scatter else src[t], dst_ref=land[t].at[me],
                    send_sem=sems[2 * t].at[k], recv_sem=sems[2 * t + 1].at[k],
                    device_id=pid, device_id_type=pl.DeviceIdType.MESH).start()
        token[...] = jnp.zeros_like(token)

    out_shape = [pltpu.SemaphoreType.DMA((N_DEV - 1,)) for _ in range(2 * nt)]
    out_shape += [pltpu.HBM(a.shape, a.dtype) for a in lands]
    out_shape += [jax.ShapeDtypeStruct((8, LANES), F32)]
    srcs = [pltpu.with_memory_space_constraint(a, pltpu.HBM) for a in srcs]
    res = pl.pallas_call(
        body, name=name, out_shape=out_shape,
        in_specs=[_HBM] * (2 * nt) + [_ANY],
        out_specs=[_SEM] * (2 * nt) + [_HBM] * nt + [pl.BlockSpec(memory_space=pltpu.VMEM)],
        input_output_aliases={nt + i: 2 * nt + i for i in range(nt)},
        compiler_params=pltpu.CompilerParams(has_side_effects=_EFFECT),
    )(*srcs, *[pltpu.with_memory_space_constraint(a, pltpu.HBM) for a in lands], after)
    sems, thru = res[:2 * nt], res[2 * nt:3 * nt]
    return [(sems[2 * t], sems[2 * t + 1], srcs[t], thru[t]) for t in range(nt)], res[-1]


def _xfer_wait(handle, scatter, after, name):
    send, recv, src, land = handle

    def body(src_ref, land_ref, send_sem, recv_sem, after_ref, land_out):
        _, peers = _peers()
        for k, (pid, plin) in enumerate(peers):
            cp = pltpu.make_async_remote_copy(
                src_ref=src_ref.at[plin] if scatter else src_ref, dst_ref=land_ref.at[plin],
                send_sem=send_sem.at[k], recv_sem=recv_sem.at[k],
                device_id=pid, device_id_type=pl.DeviceIdType.MESH)
            cp.wait_send()
            cp.wait_recv()

    return pl.pallas_call(
        body, name=name, out_shape=pltpu.HBM(land.shape, land.dtype),
        in_specs=(_HBM, _HBM, _SEM, _SEM, _ANY), out_specs=_HBM, input_output_aliases={1: 0},
        compiler_params=pltpu.CompilerParams(has_side_effects=_EFFECT),
    )(src, land, send, recv, after)


def _landing(x, me):
    return lax.dynamic_update_slice(lax.empty((N_DEV,) + x.shape, x.dtype), x[None], (me,) + (0,) * x.ndim)


def _landing_scatter(p, me):
    own = lax.dynamic_index_in_dim(p, me, axis=0, keepdims=True)
    return lax.dynamic_update_slice(lax.empty(p.shape, p.dtype), own, (me,) + (0,) * (p.ndim - 1))


def _chip_peers():
    x, y, c = lax.axis_index("x"), lax.axis_index("y"), lax.axis_index("c")
    lin = lambda px, py, pc: 4 * px + 2 * py + pc
    sibling = ((x, y, 1 - c), lin(x, y, 1 - c))
    chips = [((1 - x, y, c), lin(1 - x, y, c)), ((x, 1 - y, c), lin(x, 1 - y, c)),
             ((1 - x, 1 - y, c), lin(1 - x, 1 - y, c))]
    return lin(x, y, c), sibling, chips


N_CHIPS_OTHER = 3


def _gather2_start(srcs, lands, after, name):
    nt = len(srcs)

    def body(*refs):
        src, land = refs[:nt], refs[nt:2 * nt]
        sems = refs[2 * nt + 1:5 * nt + 1]
        token = refs[-1]
        me, sibling, chips = _chip_peers()
        for t in range(nt):
            send, recv_ici, recv_sib = sems[3 * t], sems[3 * t + 1], sems[3 * t + 2]
            pltpu.make_async_remote_copy(src_ref=src[t], dst_ref=land[t].at[me], send_sem=send.at[0],
                                         recv_sem=recv_sib.at[0], device_id=sibling[0],
                                         device_id_type=pl.DeviceIdType.MESH).start()
            for j, (pid, _) in enumerate(chips):
                pltpu.make_async_remote_copy(src_ref=src[t], dst_ref=land[t].at[me], send_sem=send.at[1 + j],
                                             recv_sem=recv_ici.at[j], device_id=pid,
                                             device_id_type=pl.DeviceIdType.MESH).start()
        token[...] = jnp.zeros_like(token)

    out_shape = []
    for _ in range(nt):
        out_shape += [pltpu.SemaphoreType.DMA((1 + N_CHIPS_OTHER,)), pltpu.SemaphoreType.DMA((N_CHIPS_OTHER,)),
                      pltpu.SemaphoreType.DMA((1,))]
    out_shape += [pltpu.HBM(a.shape, a.dtype) for a in list(srcs) + list(lands)]
    out_shape += [jax.ShapeDtypeStruct((8, LANES), F32)]
    res = pl.pallas_call(
        body, name=name, out_shape=out_shape,
        in_specs=[_HBM] * (2 * nt) + [_ANY],
        out_specs=[_SEM] * (3 * nt) + [_HBM] * (2 * nt) + [pl.BlockSpec(memory_space=pltpu.VMEM)],
        input_output_aliases={i: 3 * nt + i for i in range(2 * nt)},
        compiler_params=pltpu.CompilerParams(has_side_effects=_EFFECT),
    )(*[pltpu.with_memory_space_constraint(a, pltpu.HBM) for a in list(srcs) + list(lands)], after)
    sems, thru = res[:3 * nt], res[3 * nt:5 * nt]
    return [(sems[3 * t], sems[3 * t + 1], sems[3 * t + 2], thru[t], thru[nt + t]) for t in range(nt)], res[-1]


def _gather2_relay(handles, after, name):
    nt = len(handles)

    def body(*refs):
        src, land = refs[:nt], refs[nt:2 * nt]
        send1, recv_ici = refs[2 * nt:3 * nt], refs[3 * nt:4 * nt]
        outs = refs[4 * nt + 1:]
        send2, recv2 = outs[:nt], outs[nt:2 * nt]
        me, sibling, chips = _chip_peers()
        for t in range(nt):
            pltpu.make_async_remote_copy(src_ref=src[t], dst_ref=land[t].at[me], send_sem=send1[t].at[0],
                                         recv_sem=recv_ici[t].at[0], device_id=sibling[0],
                                         device_id_type=pl.DeviceIdType.MESH).wait_send()
            for j, (pid, plin) in enumerate(chips):
                arrived = pltpu.make_async_remote_copy(src_ref=src[t], dst_ref=land[t].at[plin], send_sem=send1[t].at[1 + j],
                                                       recv_sem=recv_ici[t].at[j], device_id=pid,
                                                       device_id_type=pl.DeviceIdType.MESH)
                arrived.wait_send()
                arrived.wait_recv()
                pltpu.make_async_remote_copy(src_ref=land[t].at[plin], dst_ref=land[t].at[plin], send_sem=send2[t].at[j],
                                             recv_sem=recv2[t].at[j], device_id=sibling[0],
                                             device_id_type=pl.DeviceIdType.MESH).start()

    srcs = [h[3] for h in handles]
    lands = [h[4] for h in handles]
    out_shape = [pltpu.SemaphoreType.DMA((N_CHIPS_OTHER,)) for _ in range(2 * nt)]
    out_shape += [pltpu.HBM(a.shape, a.dtype) for a in srcs + lands]
    res = pl.pallas_call(
        body, name=name, out_shape=out_shape,
        in_specs=[_HBM] * (2 * nt) + [_SEM] * (2 * nt) + [_ANY],
        out_specs=[_SEM] * (2 * nt) + [_HBM] * (2 * nt),
        input_output_aliases={i: 2 * nt + i for i in range(2 * nt)},
        compiler_params=pltpu.CompilerParams(has_side_effects=_EFFECT),
    )(*srcs, *lands, *[h[0] for h in handles], *[h[1] for h in handles], after)
    return [(handles[t][2], res[t], res[nt + t], res[3 * nt + t]) for t in range(nt)]


def _gather2_wait(handle, after, name):
    recv_sib, send2, recv2, land = handle

    def body(land_ref, recv_sib_sem, send2_sem, recv2_sem, after_ref, land_out):
        me, sibling, chips = _chip_peers()
        pltpu.make_async_remote_copy(src_ref=land_ref.at[me], dst_ref=land_ref.at[sibling[1]], send_sem=send2_sem.at[0],
                                     recv_sem=recv_sib_sem.at[0], device_id=sibling[0],
                                     device_id_type=pl.DeviceIdType.MESH).wait_recv()
        for j, (pid, plin) in enumerate(chips):
            relayed = pltpu.make_async_remote_copy(src_ref=land_ref.at[plin], dst_ref=land_ref.at[plin], send_sem=send2_sem.at[j],
                                                   recv_sem=recv2_sem.at[j], device_id=sibling[0],
                                                   device_id_type=pl.DeviceIdType.MESH)
            relayed.wait_send()
            relayed.wait_recv()

    return pl.pallas_call(
        body, name=name, out_shape=pltpu.HBM(land.shape, land.dtype),
        in_specs=(_HBM, _SEM, _SEM, _SEM, _ANY), out_specs=_HBM, input_output_aliases={0: 0},
        compiler_params=pltpu.CompilerParams(has_side_effects=_EFFECT),
    )(land, recv_sib, send2, recv2, after)


def _exchange(arrs, scatter, name):
    nt = len(arrs)
    out_shape = [jax.ShapeDtypeStruct(a.shape if scatter else (N_DEV,) + a.shape, a.dtype) for a in arrs]

    def body(*refs):
        ins, outs = refs[:nt], refs[nt:2 * nt]
        send, recv, loc = refs[2 * nt:]
        me, peers = _peers()
        copies = []
        for t in range(nt):
            own = pltpu.make_async_copy(ins[t].at[me] if scatter else ins[t], outs[t].at[me], loc.at[t])
            own.start()
            copies.append(own)
            for k, (pid, plin) in enumerate(peers):
                cp = pltpu.make_async_remote_copy(
                    src_ref=ins[t].at[plin] if scatter else ins[t], dst_ref=outs[t].at[me],
                    send_sem=send.at[t, k], recv_sem=recv.at[t, k],
                    device_id=pid, device_id_type=pl.DeviceIdType.MESH)
                cp.start()
                copies.append(cp)
        for cp in copies:
            cp.wait()

    any_spec = pl.BlockSpec(memory_space=pl.ANY)
    return pl.pallas_call(
        body, out_shape=out_shape, in_specs=[any_spec] * nt, out_specs=[any_spec] * nt,
        scratch_shapes=[pltpu.SemaphoreType.DMA((nt, N_DEV - 1)), pltpu.SemaphoreType.DMA((nt, N_DEV - 1)),
                        pltpu.SemaphoreType.DMA((nt,))],
        name=name)(*arrs)


def _adamw_body(n_parts):
    def body(p_ref, w_ref, m_ref, v_ref, *rest):
        g_out, d_out, m_out, v_out = rest[-4:]
        g = p_ref[0].astype(F32)
        for i in range(1, n_parts):
            g = g + p_ref[i].astype(F32)
        m2 = ADAM_B1 * m_ref[...] + (1.0 - ADAM_B1) * g
        v2 = ADAM_B2 * v_ref[...] + (1.0 - ADAM_B2) * jnp.square(g)
        m_hat = m2 / (1.0 - ADAM_B1 ** ADAM_STEP)
        v_hat = v2 / (1.0 - ADAM_B2 ** ADAM_STEP)
        g_out[...] = g
        d_out[...] = -ADAM_LR * (m_hat / (jnp.sqrt(v_hat) + ADAM_EPS) + ADAM_WD * w_ref[...])
        m_out[...] = m2
        v_out[...] = v2

    return body


def _adamw_layer(parts, w, m, v, layer, prev, name):
    p, r, c = parts.shape
    tr = _tile(r, 256, 8)
    blk = pl.BlockSpec((None, tr, c), lambda i: (layer, i, 0))
    in_specs = [pl.BlockSpec((p, tr, c), lambda i: (0, i, 0)), blk, blk, blk]
    args = [parts, w, m, v]
    aliases = {}
    if prev is not None:
        in_specs += [_ANY] * 4
        args += list(prev)
        aliases = {4 + i: i for i in range(4)}
    return pl.pallas_call(
        _adamw_body(p), grid=(r // tr,), in_specs=in_specs, out_specs=[blk] * 4,
        out_shape=[jax.ShapeDtypeStruct(w.shape, F32)] * 4, input_output_aliases=aliases, name=name,
        compiler_params=_cparams(1))(*args)


def _adamw(parts, w, m, v, name):
    p, nl, r, c = parts.shape
    tr = _tile(r, 256, 8)
    body = _adamw_body(p)

    blk = (None, tr, c)
    imap = lambda l, i: (l, i, 0)
    out = ((nl, r, c), F32, blk, imap)
    return _call(body, grid=(nl, r // tr),
                 ins=[(parts, (p, None, tr, c), lambda l, i: (0, l, i, 0)), (w, blk, imap), (m, blk, imap), (v, blk, imap)],
                 outs=[out] * 4, name=name)


def _rows(x):
    return x.reshape(-1, LANES)


def _pad_rows(x, mult=8):
    r = x.shape[0]
    extra = (-r) % mult
    return jnp.pad(x, ((0, extra), (0, 0))) if extra else x


def _shard_cols(x, me, groups):
    lead = x.shape[:-1]
    xr = x.reshape(lead + (N_DEV, groups * LANES))
    xs = lax.dynamic_index_in_dim(xr, me, axis=len(lead), keepdims=False)
    return xs.reshape(N_DEV, -1, LANES)


def kernel(x, c, ada_w, ada_b, ln_g, ln_b, dn_w_in, dn_conv_w, dn_a_log, dn_dt_bias, dn_norm_w, dn_w_out, cf_w_in, cf_dw_w, cf_dw_b, cf_ln_g, cf_ln_b, cf_w_out, ff_w1, ff_w2, loss_target, m_ada_w, m_ada_b, m_ln_g, m_ln_b, m_dn_w_in, m_dn_conv_w, m_dn_a_log, m_dn_dt_bias, m_dn_norm_w, m_dn_w_out, m_cf_w_in, m_cf_dw_w, m_cf_dw_b, m_cf_ln_g, m_cf_ln_b, m_cf_w_out, m_ff_w1, m_ff_w2, v_ada_w, v_ada_b, v_ln_g, v_ln_b, v_dn_w_in, v_dn_conv_w, v_dn_a_log, v_dn_dt_bias, v_dn_norm_w, v_dn_w_out, v_cf_w_in, v_cf_dw_w, v_cf_dw_b, v_cf_ln_g, v_cf_ln_b, v_cf_w_out, v_ff_w1, v_ff_w2):
    depth, d, _ = ada_w.shape
    n_a, n_b = dn_w_in.shape[0], cf_w_in.shape[0]
    heads = dn_a_log.shape[1]
    hw = heads * HEAD_DIM
    taps = cf_dw_w.shape[1]
    s = x.shape[1]
    alpha = (2.0 * depth) ** 0.25
    me = 4 * lax.axis_index("x") + 2 * lax.axis_index("y") + lax.axis_index("c")
    xs, tgt = x[0], loss_target[0]

    dn_in_cols = dn_w_in.shape[2]
    keys, shards = [], []
    for i in range(depth):
        j = i // 2
        mixer = [("dn_in", dn_w_in), ("dn_out", dn_w_out)] if i % 2 == 0 else [("cf_in", cf_w_in), ("cf_out", cf_w_out)]
        for nm, wt in mixer:
            keys.append((nm, j))
            shards.append(wt[j].astype(MXU_DTYPE))
        keys += [("ff1", i), ("ff2", i)]
        shards += [ff_w1[i].astype(MXU_DTYPE), ff_w2[i].astype(MXU_DTYPE)]

    small_local = [_rows(ln_g), _rows(ln_b), _rows(dn_conv_w), _rows(cf_dw_w), _rows(cf_dw_b), _rows(cf_ln_g),
                   _rows(cf_ln_b), _rows(c)]
    sizes = [a.shape[0] for a in small_local]
    packed = _pad_rows(jnp.concatenate(small_local, axis=0))
    (small_all,) = _exchange([packed], False, "comm_gather_params")
    offs = [0]
    for z in sizes:
        offs.append(offs[-1] + z)

    def small(i):
        return small_all[:, offs[i]:offs[i + 1], :]

    def unshard(piece, lead, groups):
        t = piece.reshape((N_DEV,) + lead + (groups * LANES,))
        t = jnp.moveaxis(t, 0, len(lead))
        return t.reshape(lead + (N_DEV * groups * LANES,))

    ln_g_f = unshard(small(0), (depth, 2), 1)
    ln_b_f = unshard(small(1), (depth, 2), 1)
    conv_w_f = unshard(small(2), (n_a, DN_CONV), 3 * heads // N_DEV)
    dw_w_f = unshard(small(3), (n_b, taps), 1)
    dw_b_f = unshard(small(4), (n_b,), 1)
    cf_ln_g_f = unshard(small(5), (n_b,), 1)
    cf_ln_b_f = unshard(small(6), (n_b,), 1)
    c_all = small(7).reshape(N_DEV, d)

    mod_part, cond_all = _ada_fwd(c_all, ada_w)
    (mod_all,) = _exchange([mod_part], False, "comm_gather_mod")
    mod_mine = lax.dynamic_index_in_dim(mod_all, me, axis=2, keepdims=False)
    mod_mine = jnp.moveaxis(mod_mine, 0, 1).reshape(depth, N_MOD * d)

    handles, token = _gather2_start(shards, [_landing(a, me) for a in shards], mod_all, "gather_weights_start")
    handles = dict(zip(keys, handles))
    groups = [keys[:1], keys[1:4]] + [keys[4 * i:4 * i + 4] for i in range(1, depth)]
    group_of = {k: n for n, grp in enumerate(groups) for k in grp}
    relayed, weights = {}, {}

    def relay(n, after):
        if n < len(groups) and groups[n][0] not in relayed:
            hs = _gather2_relay([handles[k] for k in groups[n]], after, "gather_relay_%d" % n)
            relayed.update(zip(groups[n], hs))

    relay(0, token)

    def gathered(key, after):
        if key not in weights:
            relay(group_of[key], after)
            if key[0] == "ff1":
                relay(key[1] + 2, after)
            weights[key] = _gather2_wait(relayed[key], after, "gather_wait_%s_%d" % key)
        return weights[key]

    def get_dn_in(j):
        def get(after):
            g = gathered(("dn_in", j), after)
            w = jnp.moveaxis(g, 0, 1).reshape(d, N_DEV * dn_in_cols)
            return jnp.pad(w, ((0, 0), (0, 4 * hw + LANES - N_DEV * dn_in_cols)))
        return get

    def get_rows(key):
        return lambda after: gathered(key, after).reshape((-1, d))

    def get_cols(key):
        return lambda after: gathered(key, after)

    def add_bias(a, b):
        return (a + b,), ()

    (mod,), _ = _rowmap(add_bias, [mod_mine, ada_b], [], [(N_MOD * d, F32)], [], "ada_bias", pin=token)
    mod_rows = mod.reshape(depth * N_MOD, 1, d)
    ln_g_rows = ln_g_f.reshape(depth * 2, 1, d)
    ln_b_rows = ln_b_f.reshape(depth * 2, 1, d)

    def mod_row(i, j):
        return (mod_rows, i * N_MOD + j)

    def ln_row(rows, i, j):
        return (rows, i * 2 + j)

    subs = []
    h_cur = _modulate_fwd(xs, mod_row(0, 1), mod_row(0, 0))
    x_cur = xs
    last = None
    for i in range(depth):
        j = i // 2
        if i % 2 == 0:
            y, res = _deltanet_fwd(h_cur, get_dn_in(j), conv_w_f[j], dn_a_log[j], dn_dt_bias[j], dn_norm_w[j],
                                   get_rows(("dn_out", j)))
        else:
            y, res = _conformer_fwd(h_cur, get_cols(("cf_in", j)), dw_w_f[j], dw_b_f[j][None, :], cf_ln_g_f[j][None, :],
                                    cf_ln_b_f[j][None, :], get_rows(("cf_out", j)))
        p1 = (mod_row(i, 2), ln_row(ln_g_rows, i,0), ln_row(ln_b_rows, i,0), mod_row(i, 4), mod_row(i, 3))
        x_mid, h_mid = _combine_fwd(alpha, x_cur, y, *p1)
        subs.append((x_cur, y, p1, res))
        m_out, res2 = _mlp_fwd(h_mid, get_cols(("ff1", i)), get_rows(("ff2", i)))
        if i + 1 < depth:
            p2 = (mod_row(i, 5), ln_row(ln_g_rows, i,1), ln_row(ln_b_rows, i,1), mod_row(i + 1, 1), mod_row(i + 1, 0))
            x_next, h_next = _combine_fwd(alpha, x_mid, m_out, *p2)
            subs.append((x_mid, m_out, p2, res2))
            x_cur, h_cur = x_next, h_next
        else:
            p2 = (mod_row(i, 5), ln_row(ln_g_rows, i,1), ln_row(ln_b_rows, i,1))
            last = (x_mid, m_out, p2, res2)

    x_in, y_in, p_last, res_last = last
    dx, dy, (loss_acc, g_gt, g_g, g_b) = _last_fwd_bwd(alpha, x_in, y_in, tgt, *p_last)
    loss = lax.psum(loss_acc[0, 0], ("x", "y", "c"))

    d_mod = [[None] * N_MOD for _ in range(depth)]
    d_ln_g = [[None, None] for _ in range(depth)]
    d_ln_b = [[None, None] for _ in range(depth)]
    d_mod[depth - 1][5], d_ln_g[depth - 1][1], d_ln_b[depth - 1][1] = g_gt, g_g, g_b
    gw = dict(dn=[None] * n_a, cf=[None] * n_b)

    sent = {}

    def send_grads(named, tag):
        parts = [p for _, p in named]
        hs, tok = _xfer_start(parts, [_landing_scatter(p, me) for p in parts], True, parts[0], "scatter_start_" + tag)
        for (key, _), hnd in zip(named, hs):
            sent[key] = hnd
        return tok

    def by_rows(g):
        return g.reshape((N_DEV, g.shape[0] // N_DEV, g.shape[1]))

    def send_mlp(i, d_w1, d_w2):
        return send_grads([(("ff1", i), d_w1), (("ff2", i), by_rows(d_w2))], "ff_%d" % i)

    dh, d_w1, d_w2 = _mlp_bwd(res_last, dy)
    pin = send_mlp(depth - 1, d_w1, d_w2)
    for idx in range(len(subs) - 1, -1, -1):
        x_in, y_in, prm, res = subs[idx]
        i, second = idx // 2, idx % 2
        dx, dy, (g_gt, g_g, g_b, g_sc, g_sh) = _combine_bwd(alpha, x_in, y_in, dx, dh, *prm, pin=pin)
        d_mod[i][5 if second else 2], d_ln_g[i][second], d_ln_b[i][second] = g_gt, g_g, g_b
        nxt_i, nxt_base = (i + 1, 0) if second else (i, 3)
        d_mod[nxt_i][nxt_base + 1], d_mod[nxt_i][nxt_base] = g_sc, g_sh
        j = i // 2
        if second:
            dh, d_w1, d_w2 = _mlp_bwd(res, dy)
            pin = send_mlp(i, d_w1, d_w2)
        elif i % 2 == 0:
            dh, gw["dn"][j] = _deltanet_bwd(res, dy, conv_w_f[j])
            d_in = gw["dn"][j]["w_in"][:, :N_DEV * dn_in_cols].reshape(d, N_DEV, dn_in_cols)
            pin = send_grads([(("dn_in", j), jnp.moveaxis(d_in, 1, 0)), (("dn_out", j), by_rows(gw["dn"][j]["w_out"]))],
                             "dn_%d" % j)
        else:
            dh, gw["cf"][j] = _conformer_bwd(res, dy, dw_w_f[j], cf_ln_g_f[j][None, :], cf_ln_b_f[j][None, :])
            pin = send_grads([(("cf_in", j), gw["cf"][j]["w_in"]), (("cf_out", j), by_rows(gw["cf"][j]["w_out"]))],
                             "cf_%d" % j)
    grad_x, g_sc, g_sh = _modulate_bwd(xs, dx, dh, mod_row(0, 1), mod_row(0, 0), pin=pin)
    d_mod[0][1], d_mod[0][0] = g_sc, g_sh
    d_mod_full = jnp.concatenate([jnp.concatenate(r, axis=1) for r in d_mod], axis=0)

    stacked = {"dn_w_in": ("dn_in", dn_w_in, m_dn_w_in, v_dn_w_in), "dn_w_out": ("dn_out", dn_w_out, m_dn_w_out, v_dn_w_out),
               "cf_w_in": ("cf_in", cf_w_in, m_cf_w_in, v_cf_w_in), "cf_w_out": ("cf_out", cf_w_out, m_cf_w_out, v_cf_w_out),
               "ff_w1": ("ff1", ff_w1, m_ff_w1, v_ff_w1), "ff_w2": ("ff2", ff_w2, m_ff_w2, v_ff_w2)}
    chains = {key: None for key in stacked}

    def update_layer(i):
        mixer = ["dn_w_in", "dn_w_out"] if i % 2 == 0 else ["cf_w_in", "cf_w_out"]
        for key, idx in [("ff_w1", i), ("ff_w2", i)] + [(k, i // 2) for k in mixer]:
            short, w, m, v = stacked[key]
            parts = _xfer_wait(sent[(short, idx)], True, sg_token, "scatter_wait_%s_%d" % (short, idx))
            chains[key] = _adamw_layer(parts, w, m, v, idx, chains[key], "adamw_%s_%d" % (key, idx))

    def stack_rows(lst):
        return jnp.stack(lst, axis=0)

    gs_ln_g = jnp.stack([jnp.concatenate(r, axis=0) for r in d_ln_g], axis=0)
    gs_ln_b = jnp.stack([jnp.concatenate(r, axis=0) for r in d_ln_b], axis=0)
    gs_conv_w = stack_rows([gw["dn"][j]["conv_w"] for j in range(n_a)])
    gs_dw_w = stack_rows([gw["cf"][j]["dw_w"] for j in range(n_b)])
    gs_dw_b = stack_rows([gw["cf"][j]["dw_b"] for j in range(n_b)])
    gs_cf_ln_g = stack_rows([gw["cf"][j]["ln_g"] for j in range(n_b)])
    gs_cf_ln_b = stack_rows([gw["cf"][j]["ln_b"] for j in range(n_b)])
    gs_a_log = stack_rows([_pad_lanes(gw["dn"][j]["a_log"], 0)[0] for j in range(n_a)])
    gs_dt_bias = stack_rows([_pad_lanes(gw["dn"][j]["dt_bias"], 0)[0] for j in range(n_a)])
    gs_norm_w = stack_rows([gw["dn"][j]["norm_w"] for j in range(n_a)])
    small_grads = [gs_ln_g, gs_ln_b, gs_conv_w, gs_dw_w, gs_dw_b, gs_cf_ln_g, gs_cf_ln_b, gs_a_log, gs_dt_bias,
                   gs_norm_w, d_mod_full]
    sg_rows = [_rows(a) for a in small_grads]
    sg_sizes = [a.shape[0] for a in sg_rows]
    sg_packed = _pad_rows(jnp.concatenate(sg_rows, axis=0))
    (sg_handle,), sg_token = _xfer_start([sg_packed], [_landing(sg_packed, me)], False, grad_x, "gather_small_grads_start")
    for i in range(depth - 1, -1, -1):
        update_layer(i)
    sg_all = _xfer_wait(sg_handle, False, chains["ff_w1"][0], "gather_small_grads_wait")
    sg_offs = [0]
    for z in sg_sizes:
        sg_offs.append(sg_offs[-1] + z)

    def sg(i, shape):
        return sg_all[:, sg_offs[i]:sg_offs[i + 1], :].reshape((N_DEV,) + shape)

    dmod_all = sg(10, (depth, N_MOD * d))
    nl = ada_w.shape[2]
    dmod_cols = lax.dynamic_slice_in_dim(dmod_all, me * nl, nl, axis=2)
    g_ada_w = _ada_bwd(cond_all, jnp.moveaxis(dmod_cols, 0, 1))

    outs = {}

    def run_adamw(key, parts, w, m, v):
        shp = w.shape
        as3 = lambda t: t.reshape((-1,) + shp[-2:]) if t.ndim >= 3 else t.reshape((1,) + shp)
        parts3 = parts.reshape((parts.shape[0],) + as3(w).shape)
        res = _adamw(parts3, as3(w), as3(m), as3(v), "adamw_" + key)
        outs[key] = tuple(r.reshape(shp) for r in res)

    run_adamw("ada_w", g_ada_w[None], ada_w, m_ada_w, v_ada_w)

    cgroups = 3 * heads // N_DEV
    shard_parts = [
        _shard_cols(sg(0, (depth, 2, d)), me, 1), _shard_cols(sg(1, (depth, 2, d)), me, 1),
        _shard_cols(sg(2, (n_a, DN_CONV, 3 * hw)), me, cgroups), _shard_cols(sg(3, (n_b, taps, d)), me, 1),
        _shard_cols(sg(4, (n_b, d)), me, 1), _shard_cols(sg(5, (n_b, d)), me, 1), _shard_cols(sg(6, (n_b, d)), me, 1),
    ]
    repl_parts = [sg(7, (n_a, LANES)), sg(8, (n_a, LANES)), sg(9, (n_a, HEAD_DIM)),
                  sg(10, (depth, N_MOD * d)).reshape(N_DEV, -1, LANES)]
    small_parts = shard_parts + repl_parts
    sp_sizes = [a.shape[1] for a in small_parts]
    parts_packed = jnp.concatenate(small_parts, axis=1)
    extra = (-parts_packed.shape[1]) % 8
    parts_packed = jnp.pad(parts_packed, ((0, 0), (0, extra), (0, 0)))

    def pad_heads(t):
        return jnp.pad(t, ((0, 0), (0, LANES - heads)))

    def pack_state(ln_g_, ln_b_, conv_w_, dw_w_, dw_b_, cln_g_, cln_b_, a_log_, dt_b_, norm_w_, ada_b_):
        rows = [_rows(ln_g_), _rows(ln_b_), _rows(conv_w_), _rows(dw_w_), _rows(dw_b_), _rows(cln_g_), _rows(cln_b_),
                pad_heads(a_log_), pad_heads(dt_b_), norm_w_, _rows(ada_b_)]
        return _pad_rows(jnp.concatenate(rows, axis=0))

    w_s = pack_state(ln_g, ln_b, dn_conv_w, cf_dw_w, cf_dw_b, cf_ln_g, cf_ln_b, dn_a_log, dn_dt_bias, dn_norm_w, ada_b)
    m_s = pack_state(m_ln_g, m_ln_b, m_dn_conv_w, m_cf_dw_w, m_cf_dw_b, m_cf_ln_g, m_cf_ln_b, m_dn_a_log,
                     m_dn_dt_bias, m_dn_norm_w, m_ada_b)
    v_s = pack_state(v_ln_g, v_ln_b, v_dn_conv_w, v_cf_dw_w, v_cf_dw_b, v_cf_ln_g, v_cf_ln_b, v_dn_a_log,
                     v_dn_dt_bias, v_dn_norm_w, v_ada_b)
    res_s = _adamw(parts_packed[:, None], w_s[None], m_s[None], v_s[None], "adamw_small")
    sp_offs = [0]
    for z in sp_sizes:
        sp_offs.append(sp_offs[-1] + z)
    small_keys = ["ln_g", "ln_b", "dn_conv_w", "cf_dw_w", "cf_dw_b", "cf_ln_g", "cf_ln_b", "dn_a_log", "dn_dt_bias",
                  "dn_norm_w", "ada_b"]
    small_shapes = [ln_g.shape, ln_b.shape, dn_conv_w.shape, cf_dw_w.shape, cf_dw_b.shape, cf_ln_g.shape,
                    cf_ln_b.shape, dn_a_log.shape, dn_dt_bias.shape, dn_norm_w.shape, ada_b.shape]
    for n, (key, shp) in enumerate(zip(small_keys, small_shapes)):
        vals = []
        for r in res_s:
            piece = r[0, sp_offs[n]:sp_offs[n + 1], :]
            if key in ("dn_a_log", "dn_dt_bias"):
                piece = piece[:, :heads]
            vals.append(piece.reshape(shp))
        outs[key] = tuple(vals)

    for key in stacked:
        outs[key] = tuple(chains[key])

    order = ["ada_w", "ada_b", "ln_g", "ln_b", "dn_w_in", "dn_conv_w", "dn_a_log", "dn_dt_bias", "dn_norm_w",
             "dn_w_out", "cf_w_in", "cf_dw_w", "cf_dw_b", "cf_ln_g", "cf_ln_b", "cf_w_out", "ff_w1", "ff_w2"]
    result = [loss, grad_x[None]]
    for part in range(4):
        result += [outs[k][part] for k in order]
    return tuple(result)
```

```python
import functools

import jax
import jax.numpy as jnp
from jax import lax
from jax.experimental import pallas as pl
from jax.experimental.pallas import tpu as pltpu

F32 = jnp.float32
MXU_DTYPE = jnp.bfloat16
N_DEV = 8
LANES = 128
HEAD_DIM = 128
CHUNK = 64
DN_CONV = 4
N_MOD = 6
LN_EPS = 1e-5
RMS_EPS = 1e-6
L2_EPS = 1e-6
ADAM_LR = 0.001
ADAM_B1 = 0.9
ADAM_B2 = 0.999
ADAM_EPS = 1e-08
ADAM_WD = 0.01
ADAM_STEP = 10

HI = lax.Precision.HIGHEST
NN = ((1,), (0,))
NT = ((1,), (1,))
TN = ((0,), (0,))

ROW_TILE = 512
CONV_TILE = 256
SHORT_CONV_TILE = 1024


def _mdot(a, b, dims=NN):
    return lax.dot_general(a.astype(MXU_DTYPE), b.astype(MXU_DTYPE), (dims, ((), ())), preferred_element_type=F32)


def _split3(x):
    hi = x.astype(MXU_DTYPE)
    r1 = x - hi.astype(F32)
    mid = r1.astype(MXU_DTYPE)
    lo = (r1 - mid.astype(F32)).astype(MXU_DTYPE)
    return hi, mid, lo


def _dot01(a, b, dims=NN, mask_first=True):
    d = lambda p, q: lax.dot_general(p, q, (dims, ((), ())), preferred_element_type=F32)
    if mask_first:
        m = a.astype(MXU_DTYPE)
        return sum(d(m, p) for p in _split3(b))
    m = b.astype(MXU_DTYPE)
    return sum(d(p, m) for p in _split3(a))


def _dot3(a, b, dims=NN):
    ah, am, _ = _split3(a)
    bh, bm, _ = _split3(b)
    d = lambda p, q: lax.dot_general(p, q, (dims, ((), ())), preferred_element_type=F32)
    return d(ah, bh) + (d(ah, bm) + d(am, bh))


def _cparams(n):
    return pltpu.CompilerParams(dimension_semantics=("arbitrary",) * n)


def _call(body, *, grid, ins, outs, name, scratch=()):
    res = pl.pallas_call(
        body,
        grid=grid,
        in_specs=[pl.BlockSpec(memory_space=pl.ANY) if b is None else pl.BlockSpec(b, m) for _, b, m in ins],
        out_specs=[pl.BlockSpec(b, m) for _, _, b, m in outs],
        out_shape=[jax.ShapeDtypeStruct(s, d) for s, d, _, _ in outs],
        scratch_shapes=list(scratch),
        name=name,
        compiler_params=_cparams(len(grid)),
    )(*[a for a, _, _ in ins])
    return res


def _tile(n, pref, unit=LANES):
    if n <= pref:
        return n
    t = (pref // unit) * unit
    while t > unit and n % t:
        t -= unit
    assert n % t == 0, (n, pref)
    return t


def _rowmap(fn, rows, consts, row_outs, acc_outs, name, pin=None):
    rows = [r if isinstance(r, tuple) else (r, r.shape[1], 0) for r in rows]
    s = rows[0][0].shape[0]
    tm = min(ROW_TILE, s)
    nr, nc, no, na = len(rows), len(consts), len(row_outs), len(acc_outs)
    npin = 0 if pin is None else 1

    def body(*refs):
        rin, cin = refs[:nr], refs[nr:nr + nc]
        refs = refs[:nr + nc] + refs[nr + nc + npin:]
        rout, aout = refs[nr + nc:nr + nc + no], refs[nr + nc + no:]
        ro, ao = fn(*[r[...] for r in rin], *[c[...] for c in cin])
        for ref, val in zip(rout, ro):
            ref[...] = val.astype(ref.dtype)
        if na:
            first = pl.program_id(0) == 0

            @pl.when(first)
            def _():
                for ref, val in zip(aout, ao):
                    ref[...] = val

            @pl.when(jnp.logical_not(first))
            def _():
                for ref, val in zip(aout, ao):
                    ref[...] += val

    ins = [(a, (tm, w), functools.partial(lambda i, cb: (i, cb), cb=cb)) for a, w, cb in rows]
    for c in consts:
        if isinstance(c, tuple):
            ins.append((c[0], (None, 1, c[0].shape[2]), functools.partial(lambda i, n: (n, 0, 0), n=c[1])))
        else:
            ins.append((c, c.shape, lambda i: (0, 0)))
    if pin is not None:
        ins.append((pin, None, None))
    outs = [((s, w), d, (tm, w), lambda i: (i, 0)) for w, d in row_outs]
    outs += [(shp, F32, shp, lambda i: (0, 0)) for shp in acc_outs]
    res = _call(body, grid=(s // tm,), ins=ins, outs=outs, name=name)
    return res[:no], res[no:]


def _ln(z, g, b):
    mu = jnp.mean(z, -1, keepdims=True)
    var = jnp.mean(jnp.square(z - mu), -1, keepdims=True)
    return (z - mu) * lax.rsqrt(var + LN_EPS) * g + b


def _combine(alpha, x, y, gt, g, b, sc, sh):
    xn = _ln(alpha * x + (1.0 + gt) * y, g, b)
    return xn, xn * (1.0 + sc) + sh


def _modulate_fwd(x, sc, sh):
    def fn(x, sc, sh):
        return ((x * (1.0 + sc) + sh),), ()

    (h,), _ = _rowmap(fn, [x], [sc, sh], [(x.shape[1], MXU_DTYPE)], [], "modulate_fwd")
    return h


def _modulate_bwd(x, dx, dh, sc, sh, pin=None):
    d = x.shape[1]

    def fn(x, dx, dh, sc, sh):
        _, vjp = jax.vjp(lambda x, sc, sh: x * (1.0 + sc) + sh, x, sc, sh)
        gx, gsc, gsh = vjp(dh)
        return (dx + gx,), (gsc, gsh)

    (gx,), (gsc, gsh) = _rowmap(fn, [x, dx, dh], [sc, sh], [(d, F32)], [(1, d), (1, d)], "modulate_bwd", pin=pin)
    return gx, gsc, gsh


def _combine_fwd(alpha, x, y, gt, g, b, sc, sh):
    d = x.shape[1]

    def fn(x, y, gt, g, b, sc, sh):
        return _combine(alpha, x, y, gt, g, b, sc, sh), ()

    (xn, h), _ = _rowmap(fn, [x, y], [gt, g, b, sc, sh], [(d, F32), (d, MXU_DTYPE)], [], "combine_fwd")
    return xn, h


def _combine_bwd(alpha, x, y, dxn, dh, gt, g, b, sc, sh, pin=None):
    d = x.shape[1]

    def fn(x, y, dxn, dh, gt, g, b, sc, sh):
        _, vjp = jax.vjp(functools.partial(_combine, alpha), x, y, gt, g, b, sc, sh)
        gx, gy, ggt, gg, gb, gsc, gsh = vjp((dxn, dh))
        return (gx, gy), (ggt, gg, gb, gsc, gsh)

    (gx, gy), accs = _rowmap(fn, [x, y, dxn, dh], [gt, g, b, sc, sh], [(d, F32), (d, MXU_DTYPE)],
                             [(1, d)] * 5, "combine_bwd", pin=pin)
    return gx, gy, accs


def _last_fwd_bwd(alpha, x, y, tgt, gt, g, b):
    d = x.shape[1]

    def fn(x, y, tgt, gt, g, b):
        xn, vjp = jax.vjp(lambda x, y, gt, g, b: _ln(alpha * x + (1.0 + gt) * y, g, b), x, y, gt, g, b)
        err = xn - tgt
        gx, gy, ggt, gg, gb = vjp(err * (1.0 / d))
        rows = jnp.sum(jnp.square(err), axis=-1, keepdims=True)
        loss = (0.5 / d) * jnp.sum(rows, axis=0, keepdims=True) * jnp.ones((1, LANES), F32)
        return (gx, gy), (loss, ggt, gg, gb)

    (gx, gy), accs = _rowmap(fn, [x, y, tgt], [gt, g, b], [(d, F32), (d, MXU_DTYPE)],
                             [(1, LANES), (1, d), (1, d), (1, d)], "last_fwd_bwd")
    return gx, gy, accs


MM_VMEM_BUDGET = 40 * 2 ** 20


def _fit(options, cost):
    for o in options:
        if 2 * cost(o) <= MM_VMEM_BUDGET:
            return o
    return options[-1]


def _row_tiles(m):
    return [t for t in (2048, 1024, 512, 256) if t <= m and m % t == 0] or [m]


def _mm_call(a, a_blk, a_map, b, b_blk, b_map, outs, dims, grid, name, epi=None, extra=None, split=None, blocks=None):
    nk = grid[2]
    n_out = len(outs)
    n_in = 3 if extra is not None else 2

    def body(*refs):
        a_ref, b_ref = refs[0], refs[1]
        rest = refs[n_in:]
        out_refs = rest[:n_out]

        def finish(val):
            if epi == "relu2":
                out_refs[0][...] = jnp.square(jnp.maximum(val, 0.0)).astype(out_refs[0].dtype)
            elif epi == "relu2_bwd":
                out_refs[0][...] = (val * 2.0 * jnp.sqrt(refs[2][...].astype(F32))).astype(out_refs[0].dtype)
            elif split is not None:
                for g in range(split[0]):
                    out_refs[0][g] = val[:, g * split[1]:(g + 1) * split[1]].astype(out_refs[0].dtype)
            else:
                out_refs[0][...] = val.astype(out_refs[0].dtype)

        if blocks is None:
            p = lax.dot_general(a_ref[...], b_ref[...], (dims, ((), ())), preferred_element_type=F32)
        else:
            p = None
            for g in range(blocks[0]):
                part = lax.dot_general(a_ref[:, g * blocks[1]:(g + 1) * blocks[1]], b_ref[g], (dims, ((), ())),
                                       preferred_element_type=F32)
                p = part if p is None else p + part
        if nk == 1:
            finish(p)
        else:
            acc = rest[n_out]
            k = pl.program_id(2)

            @pl.when(k == 0)
            def _():
                acc[...] = p

            @pl.when(k > 0)
            def _():
                acc[...] += p

            @pl.when(k == nk - 1)
            def _():
                finish(acc[...])

    if nk > 1:
        out_blk = tuple(x for x in outs[0][2] if x is not None)
        if split is not None:
            out_blk = (out_blk[1], split[0] * split[1])
        scratch = [pltpu.VMEM(out_blk, F32)]
    else:
        scratch = []
    ins = [(a, a_blk, a_map), (b, b_blk, b_map)] + ([extra] if extra is not None else [])
    return _call(body, grid=grid, ins=ins, outs=outs, name=name, scratch=scratch)


def _isz(dt):
    return jnp.dtype(dt).itemsize


def _mm_nn(a, b, out_dtype, name, relu2=False):
    m, kdim = a.shape
    if b.ndim == 2:
        n = b.shape[1]
        tn = _tile(n, 1536 if n > 2048 else 512)
        b_blk, b_map = (kdim, tn), lambda i, j, k: (0, j)
    else:
        g, _, ng = b.shape
        n = g * ng
        tn = _tile(ng, 512)
        b_blk = (None, kdim, tn)
        b_map = functools.partial(lambda i, j, k, npg: (j // npg, 0, j % npg), npg=ng // tn)
    tm = _fit(_row_tiles(m), lambda t: t * kdim * _isz(a.dtype) + kdim * tn * _isz(b.dtype) + t * tn * _isz(out_dtype))
    grid = (m // tm, n // tn, 1)
    outs = [((m, n), out_dtype, (tm, tn), lambda i, j, k: (i, j))]
    return _mm_call(a, (tm, kdim), lambda i, j, k: (i, 0), b, b_blk, b_map, outs, NN, grid, name,
                    epi="relu2" if relu2 else None)[0]


def _mm_nt(a, b, out_dtype, name, relu2_sq=None):
    m, n = a.shape
    extra_bytes = _isz(relu2_sq.dtype) if relu2_sq is not None else 0
    if b.ndim == 2:
        kout = b.shape[0]
        to = _tile(kout, 512)
        b_blk, b_map, blocks = (to, n), lambda i, j, k: (j, 0), None
    else:
        g, kout, ng = b.shape
        to = _tile(kout, 512)
        b_blk, b_map, blocks = (g, to, ng), lambda i, j, k: (0, j, 0), (g, ng)
    tm = _fit(_row_tiles(m), lambda t: t * n * _isz(a.dtype) + to * n * _isz(b.dtype)
              + t * to * (_isz(out_dtype) + extra_bytes))
    grid = (m // tm, kout // to, 1)
    outs = [((m, kout), out_dtype, (tm, to), lambda i, j, k: (i, j))]
    extra = (relu2_sq, (tm, to), lambda i, j, k: (i, j)) if relu2_sq is not None else None
    return _mm_call(a, (tm, n), lambda i, j, k: (i, 0), b, b_blk, b_map, outs, NT, grid, name,
                    epi="relu2_bwd" if relu2_sq is not None else None, extra=extra, blocks=blocks)[0]


def _mm_tn(a, b, out_dtype, name, split_cols=False):
    m, kdim = a.shape
    n = b.shape[1]
    tk = _tile(kdim, 512)
    tn = _tile(n, 1536)
    if not split_cols:
        out, split = ((kdim, n), out_dtype, (tk, tn), lambda i, j, k: (i, j)), None
    else:
        ng = n // N_DEV
        if tn % ng:
            tn = _tile(ng, 512)
        if tn >= ng:
            gb = tn // ng
            out = ((N_DEV, kdim, ng), out_dtype, (gb, tk, ng), lambda i, j, k: (j, i, 0))
            split = (gb, ng)
        else:
            out = ((N_DEV, kdim, ng), out_dtype, (None, tk, tn),
                   functools.partial(lambda i, j, k, npg: (j // npg, i, j % npg), npg=ng // tn))
            split = None
    grid = (kdim // tk, n // tn, 1)
    return _mm_call(a, (m, tk), lambda i, j, k: (0, i), b, (m, tn), lambda i, j, k: (0, j), [out], TN, grid, name,
                    split=split)[0]


def _shifted(xa, off, rows):
    if off % 8 == 0:
        return xa[off:off + rows]
    return pltpu.roll(xa, xa.shape[0] - off, 0)[:rows]


def _conv_pad(taps):
    return -(-(taps - 1) // 8) * 8


def _conv_tile(xp_ref, w, i, rows, taps):
    pad = _conv_pad(taps)
    r0 = pl.multiple_of(i * rows, rows)
    xa = xp_ref[pl.ds(r0, rows + pad), :]
    views = [_shifted(xa, pad - (taps - 1) + j, rows) for j in range(taps)]
    acc = w[0:1, :] * views[0]
    for j in range(1, taps):
        acc = acc + w[j:j + 1, :] * views[j]
    return r0, acc, views


def _conv_back_tile(yp_ref, w, i, rows, taps):
    pad = _conv_pad(taps)
    r0 = pl.multiple_of(i * rows, rows)
    ya = yp_ref[pl.ds(r0, rows + pad), :]
    acc = w[taps - 1:taps, :] * ya[:rows]
    for j in range(taps - 1):
        acc = acc + w[j:j + 1, :] * _shifted(ya, taps - 1 - j, rows)
    return r0, acc


def _tap_sums(dy, views, taps):
    row = lax.broadcasted_iota(jnp.int32, (taps, LANES), 0)
    acc = jnp.zeros((taps, LANES), F32)
    for j in range(taps):
        acc = acc + jnp.where(row == j, jnp.sum(dy * views[j], axis=0, keepdims=True), 0.0)
    return acc


def _silu_l2(xc, l2):
    a = jax.nn.silu(xc)
    if l2:
        a = a * lax.rsqrt(jnp.sum(a * a, axis=-1, keepdims=True) + L2_EPS)
    return a


def _dn_conv_fwd(proj, conv_w, c0, nblk, l2, name):
    s = proj.shape[0]
    pad = _conv_pad(DN_CONV)
    rows = min(SHORT_CONV_TILE, s)

    def body(x_ref, w_ref, o_ref, xp):
        xp[0:pad, :] = jnp.zeros((pad, LANES), F32)
        xp[pad:, :] = x_ref[...]
        w = w_ref[...]

        def tile(i, c):
            r0, acc, _ = _conv_tile(xp, w, i, rows, DN_CONV)
            o_ref[pl.ds(r0, rows), :] = _silu_l2(acc, l2)
            return c

        lax.fori_loop(0, s // rows, tile, 0)

    return _call(body, grid=(nblk,),
                 ins=[(proj, (s, LANES), lambda c: (0, c0 + c)), (conv_w, (DN_CONV, LANES), lambda c: (0, c0 + c))],
                 outs=[((nblk, s, LANES), F32, (None, s, LANES), lambda c: (c, 0, 0))],
                 name=name, scratch=[pltpu.VMEM((s + pad, LANES), F32)])[0]


def _dn_conv_bwd(proj, conv_w, da, c0, nblk, l2, name):
    s = proj.shape[0]
    pad = _conv_pad(DN_CONV)
    rows = min(SHORT_CONV_TILE, s)

    def body(x_ref, w_ref, da_ref, dx_ref, dw_ref, xp, yp):
        xp[0:pad, :] = jnp.zeros((pad, LANES), F32)
        xp[pad:, :] = x_ref[...]
        yp[s:, :] = jnp.zeros((pad, LANES), F32)
        w = w_ref[...]

        def tile(i, dw):
            r0, acc, views = _conv_tile(xp, w, i, rows, DN_CONV)
            _, vjp = jax.vjp(functools.partial(_silu_l2, l2=l2), acc)
            (dxc,) = vjp(da_ref[pl.ds(r0, rows), :])
            yp[pl.ds(r0, rows), :] = dxc
            return dw + _tap_sums(dxc, views, DN_CONV)

        dw_ref[...] = lax.fori_loop(0, s // rows, tile, jnp.zeros((DN_CONV, LANES), F32))

        def tile2(i, c):
            r0, acc = _conv_back_tile(yp, w, i, rows, DN_CONV)
            dx_ref[pl.ds(r0, rows), :] = acc.astype(dx_ref.dtype)
            return c

        lax.fori_loop(0, s // rows, tile2, 0)

    return _call(body, grid=(nblk,),
                 ins=[(proj, (s, LANES), lambda c: (0, c0 + c)), (conv_w, (DN_CONV, LANES), lambda c: (0, c0 + c)),
                      (da, (None, s, LANES), lambda c: (c, 0, 0))],
                 outs=[((s, nblk * LANES), MXU_DTYPE, (s, LANES), lambda c: (0, c)),
                       ((DN_CONV, nblk * LANES), F32, (DN_CONV, LANES), lambda c: (0, c))],
                 name=name, scratch=[pltpu.VMEM((s + pad, LANES), F32), pltpu.VMEM((s + pad, LANES), F32)])


def _cf_conv_fwd(vg, dw_w, dw_b):
    s, c2 = vg.shape
    ch = c2 // 2
    nblk = ch // LANES
    taps = dw_w.shape[0]
    pad = _conv_pad(taps)
    rows = min(CONV_TILE, s)

    def body(v_ref, g_ref, w_ref, b_ref, o_ref, xp):
        xp[0:pad, :] = jnp.zeros((pad, LANES), F32)
        xp[pad:, :] = v_ref[...] * jax.nn.sigmoid(g_ref[...])
        w = w_ref[...]
        bias = b_ref[...]

        def tile(i, c):
            r0, acc, _ = _conv_tile(xp, w, i, rows, taps)
            o_ref[pl.ds(r0, rows), :] = acc + bias
            return c

        lax.fori_loop(0, s // rows, tile, 0)

    return _call(body, grid=(nblk,),
                 ins=[(vg, (s, LANES), lambda c: (0, c)), (vg, (s, LANES), lambda c: (0, nblk + c)),
                      (dw_w, (taps, LANES), lambda c: (0, c)), (dw_b, (1, LANES), lambda c: (0, c))],
                 outs=[((s, ch), F32, (s, LANES), lambda c: (0, c))],
                 name="cf_conv_fwd", scratch=[pltpu.VMEM((s + pad, LANES), F32)])[0]


def _cf_conv_bwd(vg, dw_w, du):
    s, c2 = vg.shape
    ch = c2 // 2
    nblk = ch // LANES
    taps = dw_w.shape[0]
    pad = _conv_pad(taps)
    rows = min(CONV_TILE, s)

    def body(v_ref, g_ref, w_ref, du_ref, dv_ref, dg_ref, dw_ref, db_ref, xp, yp):
        sig = jax.nn.sigmoid(g_ref[...])
        xp[0:pad, :] = jnp.zeros((pad, LANES), F32)
        xp[pad:, :] = v_ref[...] * sig
        yp[0:s, :] = du_ref[...]
        yp[s:, :] = jnp.zeros((pad, LANES), F32)
        w = w_ref[...]
        db_ref[...] = jnp.sum(du_ref[...], axis=0, keepdims=True)

        def tile(i, dw):
            r0, _, views = _conv_tile(xp, w, i, rows, taps)
            return dw + _tap_sums(du_ref[pl.ds(r0, rows), :], views, taps)

        dw_ref[...] = lax.fori_loop(0, s // rows, tile, jnp.zeros((taps, LANES), F32))

        def tile2(i, c):
            r0, du0 = _conv_back_tile(yp, w, i, rows, taps)
            val = v_ref[pl.ds(r0, rows), :]
            sg = jax.nn.sigmoid(g_ref[pl.ds(r0, rows), :])
            dv_ref[pl.ds(r0, rows), :] = (du0 * sg).astype(dv_ref.dtype)
            dg_ref[pl.ds(r0, rows), :] = (du0 * val * sg * (1.0 - sg)).astype(dg_ref.dtype)
            return c

        lax.fori_loop(0, s // rows, tile2, 0)

    return _call(body, grid=(nblk,),
                 ins=[(vg, (s, LANES), lambda c: (0, c)), (vg, (s, LANES), lambda c: (0, nblk + c)),
                      (dw_w, (taps, LANES), lambda c: (0, c)), (du, (s, LANES), lambda c: (0, c))],
                 outs=[((s, ch), MXU_DTYPE, (s, LANES), lambda c: (0, c)),
                       ((s, ch), MXU_DTYPE, (s, LANES), lambda c: (0, c)),
                       ((taps, ch), F32, (taps, LANES), lambda c: (0, c)),
                       ((1, ch), F32, (1, LANES), lambda c: (0, c))],
                 name="cf_conv_bwd", scratch=[pltpu.VMEM((s + pad, LANES), F32), pltpu.VMEM((s + pad, LANES), F32)])


def _masks():
    r = lax.broadcasted_iota(jnp.int32, (CHUNK, CHUNK), 0)
    c = lax.broadcasted_iota(jnp.int32, (CHUNK, CHUNK), 1)
    return r >= c, r > c, r <= c


def _chunk_decay(g):
    causal, _, upper = _masks()
    gb = jnp.broadcast_to(g, (CHUNK, CHUNK))
    gam_r = _dot01(jnp.where(causal, 1.0, 0.0), gb)
    gam_s = _dot01(jnp.ones((CHUNK, CHUNK), F32), jnp.where(upper, gb, 0.0))
    dm = jnp.where(causal, jnp.exp(jnp.where(causal, gam_r - gam_s, 0.0)), 0.0)
    return gam_r[:, 0:1], dm


def _chunk_scores(q, k, beta, dm):
    _, strict, _ = _masks()
    both = _mdot(jnp.concatenate([k * beta, q * (HEAD_DIM ** -0.5)], axis=0), k, NT)
    return jnp.where(strict, both[:CHUNK] * dm, 0.0), both[CHUNK:] * dm


def _lockstep(gens):
    results = [None] * len(gens)
    alive = list(range(len(gens)))
    while alive:
        for i in list(alive):
            try:
                next(gens[i])
            except StopIteration as stop:
                results[i] = stop.value
                alive.remove(i)
    return results


def _chunk_prep_bwd(q, k, v, beta, gam, t, du, dw, daqk, dqd, dkd, dgl):
    causal, strict, _ = _masks()
    r = lax.broadcasted_iota(jnp.int32, (CHUNK, CHUNK), 0)
    c = lax.broadcasted_iota(jnp.int32, (CHUNK, CHUNK), 1)
    scale = HEAD_DIM ** -0.5
    eg = jnp.exp(gam)
    gam_last = gam[CHUNK - 1:CHUNK, :]
    rr = jnp.exp(gam_last - gam)
    kb = k * beta
    qs = q * scale
    vb = v * beta
    kbe = kb * eg
    gam_b = jnp.broadcast_to(gam, (CHUNK, CHUNK))
    gam_s = _dot01(jnp.ones((CHUNK, CHUNK), F32), jnp.where(r == c, gam_b, 0.0))
    both = _mdot(jnp.concatenate([kb, qs], axis=0), k, NT)
    duw = jnp.concatenate([du, dw], axis=1)
    dt = _mdot(duw, jnp.concatenate([vb, kbe], axis=1), NT)
    dvk = _mdot(t, duw, TN)
    yield
    dm = jnp.where(causal, jnp.exp(jnp.where(causal, gam_b - gam_s, 0.0)), 0.0)
    a = jnp.where(strict, both[:CHUNK] * dm, 0.0)
    aqk = both[CHUNK:] * dm
    dvb, dkbe = dvk[:, :HEAD_DIM], dvk[:, HEAD_DIM:]
    x = _dot3(t, dt, TN)
    yield
    da = jnp.where(strict, -_dot3(x, t, NT), 0.0)
    yield
    dkk = da * dm
    dqk = daqk * dm
    ddiff = da * a + daqk * aqk
    dboth = jnp.concatenate([dkk, dqk], axis=0)
    dkq = _mdot(dboth, k)
    dk_mm = _mdot(dboth, jnp.concatenate([kb, qs], axis=0), TN)
    colsum = _dot01(ddiff, jnp.ones((CHUNK, LANES), F32), TN, mask_first=False)[:, 0:1]
    yield
    dkb = dkq[:CHUNK] + dkbe * eg
    dk = dk_mm + dkb * beta + dkd * rr
    dq = (dkq[CHUNK:] + dqd * eg) * scale
    dbeta = jnp.sum(dkb * k, axis=-1, keepdims=True) + jnp.sum(dvb * v, axis=-1, keepdims=True)
    dv = dvb * beta
    deg = jnp.sum(dkbe * kb, axis=-1, keepdims=True) + jnp.sum(dqd * qs, axis=-1, keepdims=True)
    drr = jnp.sum(dkd * k, axis=-1, keepdims=True)
    dgam = deg * eg - drr * rr + jnp.sum(ddiff, axis=-1, keepdims=True) - colsum
    dgam_last = jnp.sum(drr * rr, axis=0, keepdims=True) + dgl[0:1, :] * jnp.exp(gam_last)
    row = lax.broadcasted_iota(jnp.int32, (CHUNK, 1), 0)
    dgam = dgam + jnp.where(row == CHUNK - 1, dgam_last, 0.0)
    dg = _dot01(jnp.where(causal, 1.0, 0.0), jnp.broadcast_to(dgam, (CHUNK, LANES)), TN)[:, 0:1]
    return dq, dk, dv, dbeta, dg


def _prep_group(s):
    nch = s // CHUNK
    return next(c for c in (16, 8, 4, 2, 1) if nch % c == 0)


def _tri_solve_lanes(a_l):
    n = a_l.shape[1]
    group = 8

    def body(a_ref, t_ref):
        t_ref[...] = jnp.zeros_like(t_ref)
        col = lax.broadcasted_iota(jnp.int32, (CHUNK, n), 0)

        def row(r, carry):
            r0 = pl.multiple_of(r * CHUNK, CHUNK)

            def inner(sg, acc):
                a8 = a_ref[pl.ds(r0 + pl.multiple_of(sg * group, group), group), :]
                for j in range(group):
                    t0 = pl.multiple_of((sg * group + j) * CHUNK, CHUNK)
                    acc = acc + a8[j:j + 1, :] * t_ref[pl.ds(t0, CHUNK), :]
                return acc

            acc = lax.fori_loop(0, (r + group - 1) // group, inner, jnp.zeros((CHUNK, n), F32))
            t_ref[pl.ds(r0, CHUNK), :] = jnp.where(col == r, 1.0, 0.0) - acc
            return carry

        lax.fori_loop(0, CHUNK, row, 0)

    return pl.pallas_call(body, out_shape=jax.ShapeDtypeStruct(a_l.shape, F32), name="dn_tri_solve")(a_l)


def _head_cols(bg, hh, heads):
    lane = lax.broadcasted_iota(jnp.int32, bg.shape, 1)
    beta = jnp.sum(jnp.where(lane == hh, bg, 0.0), axis=-1, keepdims=True)
    g = jnp.sum(jnp.where(lane == heads + hh, bg, 0.0), axis=-1, keepdims=True)
    return beta, g


def _dn_prep(q, k, v, bg):
    h, s, _ = q.shape
    cb = _prep_group(s)
    rb = cb * CHUNK
    big = lambda x: (x, (None, rb, HEAD_DIM), lambda n, hh: (hh, n, 0))
    sq = lambda x: (x, (None, rb, CHUNK), lambda n, hh: (hh, n, 0))
    col = lambda x: (x, (None, rb, 1), lambda n, hh: (hh, n, 0))
    tok = (bg, (rb, LANES), lambda n, hh: (n, 0))
    o_big = ((h, s, HEAD_DIM), F32, (None, rb, HEAD_DIM), lambda n, hh: (hh, n, 0))
    o_sq = ((h, s, CHUNK), F32, (None, rb, CHUNK), lambda n, hh: (hh, n, 0))
    o_col = ((h, s, 1), F32, (None, rb, 1), lambda n, hh: (hh, n, 0))

    def scores(q_ref, k_ref, bg_ref, a_ref, aqk_ref, gam_ref):
        beta, g = _head_cols(bg_ref[...], pl.program_id(1), h)
        for i in range(cb):
            sl = slice(i * CHUNK, (i + 1) * CHUNK)
            gam, dm = _chunk_decay(g[sl])
            a_ref[sl, :], aqk_ref[sl, :] = _chunk_scores(q_ref[sl, :], k_ref[sl, :], beta[sl], dm)
            gam_ref[sl, :] = gam

    a, aqk, gam = _call(scores, grid=(s // rb, h), ins=[big(q), big(k), tok], outs=[o_sq, o_sq, o_col],
                        name="dn_scores")
    n_prob = h * (s // CHUNK)
    t_l = _tri_solve_lanes(jnp.transpose(a.reshape(n_prob, CHUNK * CHUNK)))
    t = jnp.transpose(t_l).reshape(h, s, CHUNK)

    def wy(k_ref, v_ref, bg_ref, gam_ref, t_ref, u_ref, w_ref):
        beta, _ = _head_cols(bg_ref[...], pl.program_id(1), h)
        for i in range(cb):
            sl = slice(i * CHUNK, (i + 1) * CHUNK)
            kb = k_ref[sl, :] * beta[sl]
            rhs = jnp.concatenate([v_ref[sl, :] * beta[sl], kb * jnp.exp(gam_ref[sl, :])], axis=1)
            uw = _mdot(t_ref[sl, :], rhs)
            u_ref[sl, :] = uw[:, :HEAD_DIM]
            w_ref[sl, :] = uw[:, HEAD_DIM:]

    u, w = _call(wy, grid=(s // rb, h), ins=[big(k), big(v), tok, col(gam), sq(t)], outs=[o_big, o_big],
                 name="dn_wy")
    return u, w, aqk, t, gam


def _dn_prep_bwd(q, k, v, bg, gam, t, du, dw, daqk, dqd, dkd, dgl):
    h, s, _ = q.shape
    cb = _prep_group(s)
    rb = cb * CHUNK

    def body(q_ref, k_ref, v_ref, bg_ref, g_ref, t_ref, du_ref, dw_ref, da_ref, dqd_ref, dkd_ref, dgl_ref,
             dq_ref, dk_ref, dv_ref, dbg_ref):
        hh = pl.program_id(1)
        beta, _ = _head_cols(bg_ref[...], hh, h)
        slices = [slice(i * CHUNK, (i + 1) * CHUNK) for i in range(cb)]
        results = _lockstep([_chunk_prep_bwd(
            q_ref[sl, :], k_ref[sl, :], v_ref[sl, :], beta[sl], g_ref[sl, :], t_ref[sl, :],
            du_ref[sl, :], dw_ref[sl, :], da_ref[sl, :], dqd_ref[sl, :], dkd_ref[sl, :], dgl_ref[sl, :])
            for sl in slices])

        @pl.when(hh == 0)
        def _():
            dbg_ref[...] = jnp.zeros_like(dbg_ref)

        lane = lax.broadcasted_iota(jnp.int32, (CHUNK, LANES), 1)
        for sl, (dq, dk, dv, dbeta, dg) in zip(slices, results):
            dq_ref[sl, :] = dq
            dk_ref[sl, :] = dk
            dv_ref[sl, :] = dv
            dbg_ref[sl, :] += jnp.where(lane == hh, dbeta, 0.0) + jnp.where(lane == h + hh, dg, 0.0)

    big = lambda x: (x, (None, rb, HEAD_DIM), lambda n, hh: (hh, n, 0))
    sq = lambda x: (x, (None, rb, CHUNK), lambda n, hh: (hh, n, 0))
    col = lambda x: (x, (None, rb, 1), lambda n, hh: (hh, n, 0))
    tok = (bg, (rb, LANES), lambda n, hh: (n, 0))
    o_big = ((h, s, HEAD_DIM), F32, (None, rb, HEAD_DIM), lambda n, hh: (hh, n, 0))
    return _call(body, grid=(s // rb, h),
                 ins=[big(q), big(k), big(v), tok, col(gam), sq(t), big(du), big(dw), sq(daqk), big(dqd), big(dkd),
                      col(dgl)],
                 outs=[o_big, o_big, o_big, ((s, LANES), F32, (rb, LANES), lambda n, hh: (n, 0))], name="dn_prep_bwd")


def _chunk_scaled(q, k, gam):
    gam_last = gam[CHUNK - 1:CHUNK, :]
    q_dec = q * (HEAD_DIM ** -0.5) * jnp.exp(gam)
    k_dec = k * jnp.exp(gam_last - gam)
    return q_dec, k_dec, jnp.exp(gam_last)


def _scan_group(s):
    return 2 if (s // CHUNK) % 2 == 0 else 1


def _dn_scan(q, k, u, w, aqk, gam):
    h, s, _ = q.shape
    nch = s // CHUNK
    sg = _scan_group(s)
    rb = sg * CHUNK

    def body(q_ref, k_ref, u_ref, w_ref, a_ref, gam_ref, o_ref, st_ref, state):
        @pl.when(pl.program_id(0) == 0)
        def _():
            state[...] = jnp.zeros_like(state)

        def head(hh, c):
            sl = slice(c * CHUNK, (c + 1) * CHUNK)
            s0 = state[hh]
            st_ref[c, hh] = s0
            q_dec, k_dec, gl = _chunk_scaled(q_ref[hh, sl, :], k_ref[hh, sl, :], gam_ref[hh, sl, :])
            both = _mdot(jnp.concatenate([w_ref[hh, sl, :], q_dec], axis=0), s0)
            yield
            v_new = u_ref[hh, sl, :] - both[:CHUNK]
            o_ref[sl, hh * HEAD_DIM:(hh + 1) * HEAD_DIM] = both[CHUNK:] + _mdot(a_ref[hh, sl, :], v_new)
            state[hh] = s0 * gl + _mdot(k_dec, v_new, TN)

        for c in range(sg):
            _lockstep([head(hh, c) for hh in range(h)])

    big = lambda x: (x, (h, rb, HEAD_DIM), lambda n: (0, n, 0))
    return _call(body, grid=(nch // sg,),
                 ins=[big(q), big(k), big(u), big(w), (aqk, (h, rb, CHUNK), lambda n: (0, n, 0)),
                      (gam, (h, rb, 1), lambda n: (0, n, 0))],
                 outs=[((s, h * HEAD_DIM), F32, (rb, h * HEAD_DIM), lambda n: (n, 0)),
                       ((nch, h, HEAD_DIM, HEAD_DIM), F32, (sg, h, HEAD_DIM, HEAD_DIM), lambda n: (n, 0, 0, 0))],
                 name="dn_scan", scratch=[pltpu.VMEM((h, HEAD_DIM, HEAD_DIM), F32)])


def _dn_scan_bwd(q, k, u, w, aqk, gam, states, do):
    h, s, _ = q.shape
    nch = s // CHUNK
    sg = _scan_group(s)
    rb = sg * CHUNK
    ngr = nch // sg

    def body(q_ref, k_ref, u_ref, w_ref, a_ref, gam_ref, st_ref, do_ref,
             du_ref, dw_ref, da_ref, dqd_ref, dkd_ref, dgl_ref, dstate):
        @pl.when(pl.program_id(0) == 0)
        def _():
            dstate[...] = jnp.zeros_like(dstate)

        def head(hh, c):
            sl = slice(c * CHUNK, (c + 1) * CHUNK)
            s0 = st_ref[c, hh]
            ds = dstate[hh]
            doh = do_ref[sl, hh * HEAD_DIM:(hh + 1) * HEAD_DIM]
            wv = w_ref[hh, sl, :]
            q_dec, k_dec, gl = _chunk_scaled(q_ref[hh, sl, :], k_ref[hh, sl, :], gam_ref[hh, sl, :])
            ws = _mdot(wv, s0)
            dv_new = _mdot(a_ref[hh, sl, :], doh, TN) + _mdot(k_dec, ds)
            dqd_ref[hh, sl, :] = _mdot(doh, s0, NT)
            qdo = _mdot(q_dec, doh, TN)
            tot = jnp.sum(jnp.sum(s0 * ds, axis=-1, keepdims=True), axis=0, keepdims=True)
            dgl_ref[hh, sl, :] = jnp.broadcast_to(tot, (CHUNK, 1))
            yield
            v_new = u_ref[hh, sl, :] - ws
            du_ref[hh, sl, :] = dv_new
            dw_ref[hh, sl, :] = -_mdot(dv_new, s0, NT)
            da_ref[hh, sl, :] = _mdot(doh, v_new, NT)
            dkd_ref[hh, sl, :] = _mdot(v_new, ds, NT)
            dstate[hh] = ds * gl + qdo - _mdot(wv, dv_new, TN)

        for c in range(sg - 1, -1, -1):
            _lockstep([head(hh, c) for hh in range(h)])

    rev = lambda n: (0, ngr - 1 - n, 0)
    big = lambda x: (x, (h, rb, HEAD_DIM), rev)
    o_big = ((h, s, HEAD_DIM), F32, (h, rb, HEAD_DIM), rev)
    return _call(body, grid=(ngr,),
                 ins=[big(q), big(k), big(u), big(w), (aqk, (h, rb, CHUNK), rev), (gam, (h, rb, 1), rev),
                      (states, (sg, h, HEAD_DIM, HEAD_DIM), lambda n: (ngr - 1 - n, 0, 0, 0)),
                      (do, (rb, h * HEAD_DIM), lambda n: (ngr - 1 - n, 0))],
                 outs=[o_big, o_big, ((h, s, CHUNK), F32, (h, rb, CHUNK), rev), o_big, o_big,
                       ((h, s, 1), F32, (h, rb, 1), rev)],
                 name="dn_scan_bwd", scratch=[pltpu.VMEM((h, HEAD_DIM, HEAD_DIM), F32)])


def _gates(x, a_log, dt_b, h):
    lane = lax.broadcasted_iota(jnp.int32, x.shape, 1)
    return jnp.where(lane < h, jax.nn.sigmoid(x), -jnp.exp(a_log) * jax.nn.softplus(x + dt_b))


def _head_out(oh, zh, nw):
    on = oh * lax.rsqrt(jnp.mean(oh * oh, axis=-1, keepdims=True) + RMS_EPS) * nw
    return on * jax.nn.silu(zh)


def _pad_lanes(x, lo):
    return jnp.zeros((1, LANES), F32).at[0, lo:lo + x.shape[0]].set(x)


def _deltanet_fwd(hin, get_w_in, conv_w, a_log, dt_bias, norm_w, get_w_out):
    h = a_log.shape[0]
    hw = h * HEAD_DIM
    w_in = get_w_in(hin)
    proj = _mm_nn(hin, w_in, F32, "dn_proj")
    q = _dn_conv_fwd(proj, conv_w, 0, h, True, "dn_conv_q")
    k = _dn_conv_fwd(proj, conv_w, h, h, True, "dn_conv_k")
    v = _dn_conv_fwd(proj, conv_w, 2 * h, h, False, "dn_conv_v")
    alp, dtp = _pad_lanes(a_log, h), _pad_lanes(dt_bias, h)

    def gates_fn(x, al, db):
        return (_gates(x, al, db, h),), ()

    (bg,), _ = _rowmap(gates_fn, [(proj, LANES, 4 * h)], [alp, dtp], [(LANES, F32)], [], "dn_gates")
    u, w, aqk, t, gam = _dn_prep(q, k, v, bg)
    o, states = _dn_scan(q, k, u, w, aqk, gam)
    nw = norm_w[None, :]

    def out_fn(o, z, nw):
        parts = [_head_out(o[:, i * HEAD_DIM:(i + 1) * HEAD_DIM], z[:, i * HEAD_DIM:(i + 1) * HEAD_DIM], nw)
                 for i in range(h)]
        return (jnp.concatenate(parts, axis=-1),), ()

    (og,), _ = _rowmap(out_fn, [o, (proj, hw, 3)], [nw], [(hw, MXU_DTYPE)], [], "dn_out")
    w_out = get_w_out(og)
    y = _mm_nn(og, w_out, F32, "dn_y")
    return y, (hin, proj, q, k, v, bg, u, w, aqk, t, gam, states, o, og, alp, dtp, nw, w_in, w_out)


def _deltanet_bwd(res, dy, conv_w):
    hin, proj, q, k, v, bg, u, w, aqk, t, gam, states, o, og, alp, dtp, nw, w_in, w_out = res
    h = q.shape[0]
    hw = h * HEAD_DIM
    s = hin.shape[0]
    d_w_out = _mm_tn(og, dy, MXU_DTYPE, "dn_dwout")
    dog = _mm_nt(dy, w_out, F32, "dn_dog")

    def out_bwd(o, z, dog, nw):
        dos, dzs = [], []
        dn = jnp.zeros((1, HEAD_DIM), F32)
        for i in range(h):
            sl = slice(i * HEAD_DIM, (i + 1) * HEAD_DIM)
            _, vjp = jax.vjp(_head_out, o[:, sl], z[:, sl], nw)
            a, b, c = vjp(dog[:, sl])
            dos.append(a)
            dzs.append(b)
            dn = dn + c
        return (jnp.concatenate(dos, axis=-1), jnp.concatenate(dzs, axis=-1)), (dn,)

    (do, dz), (d_norm_w,) = _rowmap(out_bwd, [o, (proj, hw, 3), dog], [nw], [(hw, F32), (hw, MXU_DTYPE)],
                                    [(1, HEAD_DIM)], "dn_out_bwd")
    du, dw, daqk, dqd, dkd, dgl = _dn_scan_bwd(q, k, u, w, aqk, gam, states, do)
    dq, dk, dv, dbg = _dn_prep_bwd(q, k, v, bg, gam, t, du, dw, daqk, dqd, dkd, dgl)
    dpq, dwq = _dn_conv_bwd(proj, conv_w, dq, 0, h, True, "dn_conv_q_bwd")
    dpk, dwk = _dn_conv_bwd(proj, conv_w, dk, h, h, True, "dn_conv_k_bwd")
    dpv, dwv = _dn_conv_bwd(proj, conv_w, dv, 2 * h, h, False, "dn_conv_v_bwd")

    def gates_bwd(x, dbg, al, db):
        _, vjp = jax.vjp(functools.partial(_gates, h=h), x, al, db)
        gx, gal, gdb = vjp(dbg)
        return (gx,), (gal, gdb)

    (dba,), (d_alp, d_dtp) = _rowmap(gates_bwd, [(proj, LANES, 4 * h), dbg], [alp, dtp], [(LANES, MXU_DTYPE)],
                                     [(1, LANES), (1, LANES)], "dn_gates_bwd")
    dproj = jnp.concatenate([dpq, dpk, dpv, dz, dba], axis=1)
    d_w_in = _mm_tn(hin, dproj, MXU_DTYPE, "dn_dwin")
    dh = _mm_nt(dproj, w_in, F32, "dn_dh")
    d_conv_w = jnp.concatenate([dwq, dwk, dwv], axis=1)
    return dh, dict(w_in=d_w_in, w_out=d_w_out, conv_w=d_conv_w, a_log=d_alp[0, h:2 * h], dt_bias=d_dtp[0, h:2 * h],
                    norm_w=d_norm_w[0])


def _ln_silu(u, g, b):
    return jax.nn.silu(_ln(u, g, b))


def _conformer_fwd(hin, get_w_in, dw_w, dw_b, ln_g, ln_b, get_w_out):
    w_in = get_w_in(hin)
    vg = _mm_nn(hin, w_in, F32, "cf_vg")
    u1 = _cf_conv_fwd(vg, dw_w, dw_b)
    ch = u1.shape[1]

    def fn(u, g, b):
        return (_ln_silu(u, g, b),), ()

    (u2,), _ = _rowmap(fn, [u1], [ln_g, ln_b], [(ch, MXU_DTYPE)], [], "cf_ln")
    w_out = get_w_out(u2)
    y = _mm_nn(u2, w_out, F32, "cf_y")
    return y, (hin, vg, u1, u2, w_in, w_out)


def _conformer_bwd(res, dy, dw_w, ln_g, ln_b):
    hin, vg, u1, u2, w_in, w_out = res
    ch = u1.shape[1]
    d_w_out = _mm_tn(u2, dy, MXU_DTYPE, "cf_dwout")
    du2 = _mm_nt(dy, w_out, F32, "cf_du2")

    def fn(u, du2, g, b):
        _, vjp = jax.vjp(_ln_silu, u, g, b)
        gu, gg, gb = vjp(du2)
        return (gu,), (gg, gb)

    (du1,), (d_ln_g, d_ln_b) = _rowmap(fn, [u1, du2], [ln_g, ln_b], [(ch, F32)], [(1, ch), (1, ch)], "cf_ln_bwd")
    dval, dgate, d_dw_w, d_dw_b = _cf_conv_bwd(vg, dw_w, du1)
    dvg = jnp.concatenate([dval, dgate], axis=1)
    d_w_in = _mm_tn(hin, dvg, MXU_DTYPE, "cf_dwin", split_cols=True)
    dh = _mm_nt(dvg, w_in, F32, "cf_dh")
    return dh, dict(w_in=d_w_in, w_out=d_w_out, dw_w=d_dw_w, dw_b=d_dw_b[0], ln_g=d_ln_g[0], ln_b=d_ln_b[0])


def _mlp_fwd(hin, get_w1, get_w2):
    w1 = get_w1(hin)
    r = _mm_nn(hin, w1, MXU_DTYPE, "ff_a", relu2=True)
    w2 = get_w2(r)
    m = _mm_nn(r, w2, F32, "ff_m")
    return m, (hin, r, w1, w2)


def _mlp_bwd(res, dm):
    hin, r, w1, w2 = res
    d_w2 = _mm_tn(r, dm, MXU_DTYPE, "ff_dw2")
    da = _mm_nt(dm, w2, MXU_DTYPE, "ff_da", relu2_sq=r)
    d_w1 = _mm_tn(hin, da, MXU_DTYPE, "ff_dw1", split_cols=True)
    dh = _mm_nt(da, w1, F32, "ff_dh")
    return dh, d_w1, d_w2


def _ada_fwd(c_all, ada_w):
    depth, d, nl = ada_w.shape
    tn = _tile(nl, 256)

    def body(c_ref, w_ref, o_ref, cond_ref):
        cond = jax.nn.silu(c_ref[...]).astype(MXU_DTYPE)
        cond_ref[...] = cond
        o_ref[...] = lax.dot_general(cond, w_ref[...].astype(MXU_DTYPE), (NN, ((), ())), preferred_element_type=F32)

    return _call(body, grid=(depth, nl // tn),
                 ins=[(c_all, c_all.shape, lambda l, j: (0, 0)), (ada_w, (None, d, tn), lambda l, j: (l, 0, j))],
                 outs=[((depth, N_DEV, nl), F32, (None, N_DEV, tn), lambda l, j: (l, 0, j)),
                       (c_all.shape, MXU_DTYPE, c_all.shape, lambda l, j: (0, 0))],
                 name="ada_fwd")


def _ada_bwd(cond_all, dmod_cols):
    depth, _, nl = dmod_cols.shape
    d = cond_all.shape[1]
    tn = _tile(nl, 256)

    def body(c_ref, g_ref, o_ref):
        o_ref[...] = lax.dot_general(c_ref[...], g_ref[...].astype(MXU_DTYPE), (TN, ((), ())),
                                     preferred_element_type=F32)

    return _call(body, grid=(depth, nl // tn),
                 ins=[(cond_all, cond_all.shape, lambda l, j: (0, 0)), (dmod_cols, (None, N_DEV, tn), lambda l, j: (l, 0, j))],
                 outs=[((depth, d, nl), F32, (None, d, tn), lambda l, j: (l, 0, j))], name="ada_bwd")[0]


def _peers():
    x, y, c = lax.axis_index("x"), lax.axis_index("y"), lax.axis_index("c")
    peers = []
    for k in range(1, N_DEV):
        px = 1 - x if k & 4 else x
        py = 1 - y if k & 2 else y
        pc = 1 - c if k & 1 else c
        peers.append(((px, py, pc), 4 * px + 2 * py + pc))
    return 4 * x + 2 * y + c, peers


_HBM = pl.BlockSpec(memory_space=pltpu.HBM)
_SEM = pl.BlockSpec(memory_space=pltpu.SEMAPHORE)
_ANY = pl.BlockSpec(memory_space=pl.ANY)
_EFFECT = pltpu.SideEffectType.DATAFLOW_SIDE_EFFECTING


def _xfer_start(srcs, lands, scatter, after, name):
    nt = len(srcs)

    def body(*refs):
        src, land = refs[:nt], refs[nt:2 * nt]
        sems = refs[2 * nt + 1:4 * nt + 1]
        token = refs[-1]
        me, peers = _peers()
        for t in range(nt):
            for k, (pid, plin) in enumerate(peers):
                pltpu.make_async_remote_copy(
                    src_ref=src[t].at[plin] if scatter else src[t], dst_ref=land[t].at[me],
                    send_sem=sems[2 * t].at[k], recv_sem=sems[2 * t + 1].at[k],
                    device_id=pid, device_id_type=pl.DeviceIdType.MESH).start()
        token[...] = jnp.zeros_like(token)

    out_shape = [pltpu.SemaphoreType.DMA((N_DEV - 1,)) for _ in range(2 * nt)]
    out_shape += [pltpu.HBM(a.shape, a.dtype) for a in lands]
    out_shape += [jax.ShapeDtypeStruct((8, LANES), F32)]
    srcs = [pltpu.with_memory_space_constraint(a, pltpu.HBM) for a in srcs]
    res = pl.pallas_call(
        body, name=name, out_shape=out_shape,
        in_specs=[_HBM] * (2 * nt) + [_ANY],
        out_specs=[_SEM] * (2 * nt) + [_HBM] * nt + [pl.BlockSpec(memory_space=pltpu.VMEM)],
        input_output_aliases={nt + i: 2 * nt + i for i in range(nt)},
        compiler_params=pltpu.CompilerParams(has_side_effects=_EFFECT),
    )(*srcs, *[pltpu.with_memory_space_constraint(a, pltpu.HBM) for a in lands], after)
    sems, thru = res[:2 * nt], res[2 * nt:3 * nt]
    return [(sems[2 * t], sems[2 * t + 1], srcs[t], thru[t]) for t in range(nt)], res[-1]


def _xfer_wait(handle, scatter, after, name):
    send, recv, src, land = handle

    def body(src_ref, land_ref, send_sem, recv_sem, after_ref, land_out):
        _, peers = _peers()
        for k, (pid, plin) in enumerate(peers):
            cp = pltpu.make_async_remote_copy(
                src_ref=src_ref.at[plin] if scatter else src_ref, dst_ref=land_ref.at[plin],
                send_sem=send_sem.at[k], recv_sem=recv_sem.at[k],
                device_id=pid, device_id_type=pl.DeviceIdType.MESH)
            cp.wait_send()
            cp.wait_recv()

    return pl.pallas_call(
        body, name=name, out_shape=pltpu.HBM(land.shape, land.dtype),
        in_specs=(_HBM, _HBM, _SEM, _SEM, _ANY), out_specs=_HBM, input_output_aliases={1: 0},
        compiler_params=pltpu.CompilerParams(has_side_effects=_EFFECT),
    )(src, land, send, recv, after)


def _landing(x, me):
    return lax.dynamic_update_slice(lax.empty((N_DEV,) + x.shape, x.dtype), x[None], (me,) + (0,) * x.ndim)


def _chip_peers():
    x, y, c = lax.axis_index("x"), lax.axis_index("y"), lax.axis_index("c")
    lin = lambda px, py, pc: 4 * px + 2 * py + pc
    sibling = ((x, y, 1 - c), lin(x, y, 1 - c))
    chips = [((1 - x, y, c), lin(1 - x, y, c)), ((x, 1 - y, c), lin(x, 1 - y, c)),
             ((1 - x, 1 - y, c), lin(1 - x, 1 - y, c))]
    return lin(x, y, c), sibling, chips


N_CHIPS_OTHER = 3


def _gather2_start(srcs, lands, after, name):
    nt = len(srcs)

    def body(*refs):
        src, land = refs[:nt], refs[nt:2 * nt]
        sems = refs[2 * nt + 1:5 * nt + 1]
        token = refs[-1]
        me, sibling, chips = _chip_peers()
        for t in range(nt):
            send, recv_ici, recv_sib = sems[3 * t], sems[3 * t + 1], sems[3 * t + 2]
            pltpu.make_async_remote_copy(src_ref=src[t], dst_ref=land[t].at[me], send_sem=send.at[0],
                                         recv_sem=recv_sib.at[0], device_id=sibling[0],
                                         device_id_type=pl.DeviceIdType.MESH).start()
            for j, (pid, _) in enumerate(chips):
                pltpu.make_async_remote_copy(src_ref=src[t], dst_ref=land[t].at[me], send_sem=send.at[1 + j],
                                             recv_sem=recv_ici.at[j], device_id=pid,
                                             device_id_type=pl.DeviceIdType.MESH).start()
        token[...] = jnp.zeros_like(token)

    out_shape = []
    for _ in range(nt):
        out_shape += [pltpu.SemaphoreType.DMA((1 + N_CHIPS_OTHER,)), pltpu.SemaphoreType.DMA((N_CHIPS_OTHER,)),
                      pltpu.SemaphoreType.DMA((1,))]
    out_shape += [pltpu.HBM(a.shape, a.dtype) for a in list(srcs) + list(lands)]
    out_shape += [jax.ShapeDtypeStruct((8, LANES), F32)]
    res = pl.pallas_call(
        body, name=name, out_shape=out_shape,
        in_specs=[_HBM] * (2 * nt) + [_ANY],
        out_specs=[_SEM] * (3 * nt) + [_HBM] * (2 * nt) + [pl.BlockSpec(memory_space=pltpu.VMEM)],
        input_output_aliases={i: 3 * nt + i for i in range(2 * nt)},
        compiler_params=pltpu.CompilerParams(has_side_effects=_EFFECT),
    )(*[pltpu.with_memory_space_constraint(a, pltpu.HBM) for a in list(srcs) + list(lands)], after)
    sems, thru = res[:3 * nt], res[3 * nt:5 * nt]
    return [(sems[3 * t], sems[3 * t + 1], sems[3 * t + 2], thru[t], thru[nt + t]) for t in range(nt)], res[-1]


def _gather2_relay(handles, after, name):
    nt = len(handles)

    def body(*refs):
        src, land = refs[:nt], refs[nt:2 * nt]
        send1, recv_ici = refs[2 * nt:3 * nt], refs[3 * nt:4 * nt]
        outs = refs[4 * nt + 1:]
        send2, recv2 = outs[:nt], outs[nt:2 * nt]
        me, sibling, chips = _chip_peers()
        for t in range(nt):
            pltpu.make_async_remote_copy(src_ref=src[t], dst_ref=land[t].at[me], send_sem=send1[t].at[0],
                                         recv_sem=recv_ici[t].at[0], device_id=sibling[0],
                                         device_id_type=pl.DeviceIdType.MESH).wait_send()
            for j, (pid, plin) in enumerate(chips):
                arrived = pltpu.make_async_remote_copy(src_ref=src[t], dst_ref=land[t].at[plin], send_sem=send1[t].at[1 + j],
                                                       recv_sem=recv_ici[t].at[j], device_id=pid,
                                                       device_id_type=pl.DeviceIdType.MESH)
                arrived.wait_send()
                arrived.wait_recv()
                pltpu.make_async_remote_copy(src_ref=land[t].at[plin], dst_ref=land[t].at[plin], send_sem=send2[t].at[j],
                                             recv_sem=recv2[t].at[j], device_id=sibling[0],
                                             device_id_type=pl.DeviceIdType.MESH).start()

    srcs = [h[3] for h in handles]
    lands = [h[4] for h in handles]
    out_shape = [pltpu.SemaphoreType.DMA((N_CHIPS_OTHER,)) for _ in range(2 * nt)]
    out_shape += [pltpu.HBM(a.shape, a.dtype) for a in srcs + lands]
    res = pl.pallas_call(
        body, name=name, out_shape=out_shape,
        in_specs=[_HBM] * (2 * nt) + [_SEM] * (2 * nt) + [_ANY],
        out_specs=[_SEM] * (2 * nt) + [_HBM] * (2 * nt),
        input_output_aliases={i: 2 * nt + i for i in range(2 * nt)},
        compiler_params=pltpu.CompilerParams(has_side_effects=_EFFECT),
    )(*srcs, *lands, *[h[0] for h in handles], *[h[1] for h in handles], after)
    return [(handles[t][2], res[t], res[nt + t], res[3 * nt + t]) for t in range(nt)]


def _gather2_wait(handle, after, name):
    recv_sib, send2, recv2, land = handle

    def body(land_ref, recv_sib_sem, send2_sem, recv2_sem, after_ref, land_out):
        me, sibling, chips = _chip_peers()
        pltpu.make_async_remote_copy(src_ref=land_ref.at[me], dst_ref=land_ref.at[sibling[1]], send_sem=send2_sem.at[0],
                                     recv_sem=recv_sib_sem.at[0], device_id=sibling[0],
                                     device_id_type=pl.DeviceIdType.MESH).wait_recv()
        for j, (pid, plin) in enumerate(chips):
            relayed = pltpu.make_async_remote_copy(src_ref=land_ref.at[plin], dst_ref=land_ref.at[plin], send_sem=send2_sem.at[j],
                                                   recv_sem=recv2_sem.at[j], device_id=sibling[0],
                                                   device_id_type=pl.DeviceIdType.MESH)
            relayed.wait_send()
            relayed.wait_recv()

    return pl.pallas_call(
        body, name=name, out_shape=pltpu.HBM(land.shape, land.dtype),
        in_specs=(_HBM, _SEM, _SEM, _SEM, _ANY), out_specs=_HBM, input_output_aliases={0: 0},
        compiler_params=pltpu.CompilerParams(has_side_effects=_EFFECT),
    )(land, recv_sib, send2, recv2, after)


def _exchange(arrs, scatter, name):
    nt = len(arrs)
    out_shape = [jax.ShapeDtypeStruct(a.shape if scatter else (N_DEV,) + a.shape, a.dtype) for a in arrs]

    def body(*refs):
        ins, outs = refs[:nt], refs[nt:2 * nt]
        send, recv, loc = refs[2 * nt:]
        me, peers = _peers()
        copies = []
        for t in range(nt):
            own = pltpu.make_async_copy(ins[t].at[me] if scatter else ins[t], outs[t].at[me], loc.at[t])
            own.start()
            copies.append(own)
            for k, (pid, plin) in enumerate(peers):
                cp = pltpu.make_async_remote_copy(
                    src_ref=ins[t].at[plin] if scatter else ins[t], dst_ref=outs[t].at[me],
                    send_sem=send.at[t, k], recv_sem=recv.at[t, k],
                    device_id=pid, device_id_type=pl.DeviceIdType.MESH)
                cp.start()
                copies.append(cp)
        for cp in copies:
            cp.wait()

    any_spec = pl.BlockSpec(memory_space=pl.ANY)
    return pl.pallas_call(
        body, out_shape=out_shape, in_specs=[any_spec] * nt, out_specs=[any_spec] * nt,
        scratch_shapes=[pltpu.SemaphoreType.DMA((nt, N_DEV - 1)), pltpu.SemaphoreType.DMA((nt, N_DEV - 1)),
                        pltpu.SemaphoreType.DMA((nt,))],
        name=name)(*arrs)


def _adamw_body(n_parts, stacked=True):
    def body(p_ref, w_ref, m_ref, v_ref, *rest):
        g_out, d_out, m_out, v_out = rest[-4:]
        part = (lambda i: p_ref[i]) if stacked else (lambda i: p_ref[i][...])
        g = part(0).astype(F32)
        for i in range(1, n_parts):
            g = g + part(i).astype(F32)
        m2 = ADAM_B1 * m_ref[...] + (1.0 - ADAM_B1) * g
        v2 = ADAM_B2 * v_ref[...] + (1.0 - ADAM_B2) * jnp.square(g)
        m_hat = m2 / (1.0 - ADAM_B1 ** ADAM_STEP)
        v_hat = v2 / (1.0 - ADAM_B2 ** ADAM_STEP)
        g_out[...] = g
        d_out[...] = -ADAM_LR * (m_hat / (jnp.sqrt(v_hat) + ADAM_EPS) + ADAM_WD * w_ref[...])
        m_out[...] = m2
        v_out[...] = v2

    return body


def _adamw_layer(own, land, me, w, m, v, layer, prev, name):
    _, r, c = own.shape
    tr = _tile(r, 256, 8)
    blk = pl.BlockSpec((None, tr, c), lambda i, me_ref: (layer, i, 0))
    share = lambda k: pl.BlockSpec((None, tr, c), lambda i, me_ref: (me_ref[0] ^ k, i, 0))
    in_specs = [share(k) for k in range(N_DEV)] + [blk, blk, blk]
    args = [own] + [land] * (N_DEV - 1) + [w, m, v]
    aliases = {}
    if prev is not None:
        in_specs += [_ANY] * 4
        args += list(prev)
        aliases = {1 + N_DEV + 3 + i: i for i in range(4)}

    def body(me_ref, *refs):
        refs = (refs[:N_DEV],) + refs[N_DEV:]
        _adamw_body(N_DEV, stacked=False)(*refs)

    return pl.pallas_call(
        body,
        grid_spec=pltpu.PrefetchScalarGridSpec(num_scalar_prefetch=1, grid=(r // tr,), in_specs=in_specs,
                                               out_specs=[blk] * 4),
        out_shape=[jax.ShapeDtypeStruct(w.shape, F32)] * 4, input_output_aliases=aliases, name=name,
        compiler_params=_cparams(1))(me, *args)


def _adamw(parts, w, m, v, name):
    p, nl, r, c = parts.shape
    tr = _tile(r, 256, 8)
    body = _adamw_body(p)

    blk = (None, tr, c)
    imap = lambda l, i: (l, i, 0)
    out = ((nl, r, c), F32, blk, imap)
    return _call(body, grid=(nl, r // tr),
                 ins=[(parts, (p, None, tr, c), lambda l, i: (0, l, i, 0)), (w, blk, imap), (m, blk, imap), (v, blk, imap)],
                 outs=[out] * 4, name=name)


def _rows(x):
    return x.reshape(-1, LANES)


def _pad_rows(x, mult=8):
    r = x.shape[0]
    extra = (-r) % mult
    return jnp.pad(x, ((0, extra), (0, 0))) if extra else x


def _shard_cols(x, me, groups):
    lead = x.shape[:-1]
    xr = x.reshape(lead + (N_DEV, groups * LANES))
    xs = lax.dynamic_index_in_dim(xr, me, axis=len(lead), keepdims=False)
    return xs.reshape(N_DEV, -1, LANES)


def kernel(x, c, ada_w, ada_b, ln_g, ln_b, dn_w_in, dn_conv_w, dn_a_log, dn_dt_bias, dn_norm_w, dn_w_out, cf_w_in, cf_dw_w, cf_dw_b, cf_ln_g, cf_ln_b, cf_w_out, ff_w1, ff_w2, loss_target, m_ada_w, m_ada_b, m_ln_g, m_ln_b, m_dn_w_in, m_dn_conv_w, m_dn_a_log, m_dn_dt_bias, m_dn_norm_w, m_dn_w_out, m_cf_w_in, m_cf_dw_w, m_cf_dw_b, m_cf_ln_g, m_cf_ln_b, m_cf_w_out, m_ff_w1, m_ff_w2, v_ada_w, v_ada_b, v_ln_g, v_ln_b, v_dn_w_in, v_dn_conv_w, v_dn_a_log, v_dn_dt_bias, v_dn_norm_w, v_dn_w_out, v_cf_w_in, v_cf_dw_w, v_cf_dw_b, v_cf_ln_g, v_cf_ln_b, v_cf_w_out, v_ff_w1, v_ff_w2):
    depth, d, _ = ada_w.shape
    n_a, n_b = dn_w_in.shape[0], cf_w_in.shape[0]
    heads = dn_a_log.shape[1]
    hw = heads * HEAD_DIM
    taps = cf_dw_w.shape[1]
    s = x.shape[1]
    alpha = (2.0 * depth) ** 0.25
    me = 4 * lax.axis_index("x") + 2 * lax.axis_index("y") + lax.axis_index("c")
    me_arr = jnp.reshape(me, (1,)).astype(jnp.int32)
    xs, tgt = x[0], loss_target[0]

    dn_in_cols = dn_w_in.shape[2]
    keys, shards = [], []
    for i in range(depth):
        j = i // 2
        mixer = [("dn_in", dn_w_in), ("dn_out", dn_w_out)] if i % 2 == 0 else [("cf_in", cf_w_in), ("cf_out", cf_w_out)]
        for nm, wt in mixer:
            keys.append((nm, j))
            shards.append(wt[j].astype(MXU_DTYPE))
        keys += [("ff1", i), ("ff2", i)]
        shards += [ff_w1[i].astype(MXU_DTYPE), ff_w2[i].astype(MXU_DTYPE)]

    small_local = [_rows(ln_g), _rows(ln_b), _rows(dn_conv_w), _rows(cf_dw_w), _rows(cf_dw_b), _rows(cf_ln_g),
                   _rows(cf_ln_b), _rows(c)]
    sizes = [a.shape[0] for a in small_local]
    packed = _pad_rows(jnp.concatenate(small_local, axis=0))
    (small_all,) = _exchange([packed], False, "comm_gather_params")
    offs = [0]
    for z in sizes:
        offs.append(offs[-1] + z)

    def small(i):
        return small_all[:, offs[i]:offs[i + 1], :]

    def unshard(piece, lead, groups):
        t = piece.reshape((N_DEV,) + lead + (groups * LANES,))
        t = jnp.moveaxis(t, 0, len(lead))
        return t.reshape(lead + (N_DEV * groups * LANES,))

    ln_g_f = unshard(small(0), (depth, 2), 1)
    ln_b_f = unshard(small(1), (depth, 2), 1)
    conv_w_f = unshard(small(2), (n_a, DN_CONV), 3 * heads // N_DEV)
    dw_w_f = unshard(small(3), (n_b, taps), 1)
    dw_b_f = unshard(small(4), (n_b,), 1)
    cf_ln_g_f = unshard(small(5), (n_b,), 1)
    cf_ln_b_f = unshard(small(6), (n_b,), 1)
    c_all = small(7).reshape(N_DEV, d)

    mod_part, cond_all = _ada_fwd(c_all, ada_w)
    (mod_all,) = _exchange([mod_part], False, "comm_gather_mod")
    mod_mine = lax.dynamic_index_in_dim(mod_all, me, axis=2, keepdims=False)
    mod_mine = jnp.moveaxis(mod_mine, 0, 1).reshape(depth, N_MOD * d)

    handles, token = _gather2_start(shards, [_landing(a, me) for a in shards], mod_all, "gather_weights_start")
    handles = dict(zip(keys, handles))
    groups = [keys[:1], keys[1:4]] + [keys[4 * i:4 * i + 4] for i in range(1, depth)]
    group_of = {k: n for n, grp in enumerate(groups) for k in grp}
    relayed, weights = {}, {}

    def relay(n, after):
        if n < len(groups) and groups[n][0] not in relayed:
            hs = _gather2_relay([handles[k] for k in groups[n]], after, "gather_relay_%d" % n)
            relayed.update(zip(groups[n], hs))

    relay(0, token)

    def gathered(key, after):
        if key not in weights:
            relay(group_of[key], after)
            if key[0] == "ff1":
                relay(key[1] + 2, after)
            weights[key] = _gather2_wait(relayed[key], after, "gather_wait_%s_%d" % key)
        return weights[key]

    def get_dn_in(j):
        def get(after):
            g = gathered(("dn_in", j), after)
            w = jnp.moveaxis(g, 0, 1).reshape(d, N_DEV * dn_in_cols)
            return jnp.pad(w, ((0, 0), (0, 4 * hw + LANES - N_DEV * dn_in_cols)))
        return get

    def get_rows(key):
        return lambda after: gathered(key, after).reshape((-1, d))

    def get_cols(key):
        return lambda after: gathered(key, after)

    def add_bias(a, b):
        return (a + b,), ()

    (mod,), _ = _rowmap(add_bias, [mod_mine, ada_b], [], [(N_MOD * d, F32)], [], "ada_bias", pin=token)
    mod_rows = mod.reshape(depth * N_MOD, 1, d)
    ln_g_rows = ln_g_f.reshape(depth * 2, 1, d)
    ln_b_rows = ln_b_f.reshape(depth * 2, 1, d)

    def mod_row(i, j):
        return (mod_rows, i * N_MOD + j)

    def ln_row(rows, i, j):
        return (rows, i * 2 + j)

    subs = []
    h_cur = _modulate_fwd(xs, mod_row(0, 1), mod_row(0, 0))
    x_cur = xs
    last = None
    for i in range(depth):
        j = i // 2
        if i % 2 == 0:
            y, res = _deltanet_fwd(h_cur, get_dn_in(j), conv_w_f[j], dn_a_log[j], dn_dt_bias[j], dn_norm_w[j],
                                   get_rows(("dn_out", j)))
        else:
            y, res = _conformer_fwd(h_cur, get_cols(("cf_in", j)), dw_w_f[j], dw_b_f[j][None, :], cf_ln_g_f[j][None, :],
                                    cf_ln_b_f[j][None, :], get_rows(("cf_out", j)))
        p1 = (mod_row(i, 2), ln_row(ln_g_rows, i, 0), ln_row(ln_b_rows, i, 0), mod_row(i, 4), mod_row(i, 3))
        x_mid, h_mid = _combine_fwd(alpha, x_cur, y, *p1)
        subs.append((x_cur, y, p1, res))
        m_out, res2 = _mlp_fwd(h_mid, get_cols(("ff1", i)), get_rows(("ff2", i)))
        if i + 1 < depth:
            p2 = (mod_row(i, 5), ln_row(ln_g_rows, i, 1), ln_row(ln_b_rows, i, 1), mod_row(i + 1, 1), mod_row(i + 1, 0))
            x_next, h_next = _combine_fwd(alpha, x_mid, m_out, *p2)
            subs.append((x_mid, m_out, p2, res2))
            x_cur, h_cur = x_next, h_next
        else:
            p2 = (mod_row(i, 5), ln_row(ln_g_rows, i, 1), ln_row(ln_b_rows, i, 1))
            last = (x_mid, m_out, p2, res2)

    x_in, y_in, p_last, res_last = last
    dx, dy, (loss_acc, g_gt, g_g, g_b) = _last_fwd_bwd(alpha, x_in, y_in, tgt, *p_last)
    loss = lax.psum(loss_acc[0, 0], ("x", "y", "c"))

    d_mod = [[None] * N_MOD for _ in range(depth)]
    d_ln_g = [[None, None] for _ in range(depth)]
    d_ln_b = [[None, None] for _ in range(depth)]
    d_mod[depth - 1][5], d_ln_g[depth - 1][1], d_ln_b[depth - 1][1] = g_gt, g_g, g_b
    gw = dict(dn=[None] * n_a, cf=[None] * n_b)

    sent = {}

    def send_grads(named, tag):
        parts = [p for _, p in named]
        hs, tok = _xfer_start(parts, [lax.empty(p.shape, p.dtype) for p in parts], True, parts[0], "scatter_start_" + tag)
        for (key, _), hnd in zip(named, hs):
            sent[key] = hnd
        return tok

    def by_rows(g):
        return g.reshape((N_DEV, g.shape[0] // N_DEV, g.shape[1]))

    def send_mlp(i, d_w1, d_w2):
        return send_grads([(("ff1", i), d_w1), (("ff2", i), by_rows(d_w2))], "ff_%d" % i)

    dh, d_w1, d_w2 = _mlp_bwd(res_last, dy)
    pin = send_mlp(depth - 1, d_w1, d_w2)
    for idx in range(len(subs) - 1, -1, -1):
        x_in, y_in, prm, res = subs[idx]
        i, second = idx // 2, idx % 2
        dx, dy, (g_gt, g_g, g_b, g_sc, g_sh) = _combine_bwd(alpha, x_in, y_in, dx, dh, *prm, pin=pin)
        d_mod[i][5 if second else 2], d_ln_g[i][second], d_ln_b[i][second] = g_gt, g_g, g_b
        nxt_i, nxt_base = (i + 1, 0) if second else (i, 3)
        d_mod[nxt_i][nxt_base + 1], d_mod[nxt_i][nxt_base] = g_sc, g_sh
        j = i // 2
        if second:
            dh, d_w1, d_w2 = _mlp_bwd(res, dy)
            pin = send_mlp(i, d_w1, d_w2)
        elif i % 2 == 0:
            dh, gw["dn"][j] = _deltanet_bwd(res, dy, conv_w_f[j])
            d_in = gw["dn"][j]["w_in"][:, :N_DEV * dn_in_cols].reshape(d, N_DEV, dn_in_cols)
            pin = send_grads([(("dn_in", j), jnp.moveaxis(d_in, 1, 0)), (("dn_out", j), by_rows(gw["dn"][j]["w_out"]))],
                             "dn_%d" % j)
        else:
            dh, gw["cf"][j] = _conformer_bwd(res, dy, dw_w_f[j], cf_ln_g_f[j][None, :], cf_ln_b_f[j][None, :])
            pin = send_grads([(("cf_in", j), gw["cf"][j]["w_in"]), (("cf_out", j), by_rows(gw["cf"][j]["w_out"]))],
                             "cf_%d" % j)
    grad_x, g_sc, g_sh = _modulate_bwd(xs, dx, dh, mod_row(0, 1), mod_row(0, 0), pin=pin)
    d_mod[0][1], d_mod[0][0] = g_sc, g_sh
    d_mod_full = jnp.concatenate([jnp.concatenate(r, axis=1) for r in d_mod], axis=0)

    stacked = {"dn_w_in": ("dn_in", dn_w_in, m_dn_w_in, v_dn_w_in), "dn_w_out": ("dn_out", dn_w_out, m_dn_w_out, v_dn_w_out),
               "cf_w_in": ("cf_in", cf_w_in, m_cf_w_in, v_cf_w_in), "cf_w_out": ("cf_out", cf_w_out, m_cf_w_out, v_cf_w_out),
               "ff_w1": ("ff1", ff_w1, m_ff_w1, v_ff_w1), "ff_w2": ("ff2", ff_w2, m_ff_w2, v_ff_w2)}
    chains = {key: None for key in stacked}

    def update_layer(i):
        mixer = ["dn_w_in", "dn_w_out"] if i % 2 == 0 else ["cf_w_in", "cf_w_out"]
        for key, idx in [("ff_w1", i), ("ff_w2", i)] + [(k, i // 2) for k in mixer]:
            short, w, m, v = stacked[key]
            land = _xfer_wait(sent[(short, idx)], True, sg_token, "scatter_wait_%s_%d" % (short, idx))
            chains[key] = _adamw_layer(sent[(short, idx)][2], land, me_arr, w, m, v, idx, chains[key],
                                       "adamw_%s_%d" % (key, idx))

    def stack_rows(lst):
        return jnp.stack(lst, axis=0)

    gs_ln_g = jnp.stack([jnp.concatenate(r, axis=0) for r in d_ln_g], axis=0)
    gs_ln_b = jnp.stack([jnp.concatenate(r, axis=0) for r in d_ln_b], axis=0)
    gs_conv_w = stack_rows([gw["dn"][j]["conv_w"] for j in range(n_a)])
    gs_dw_w = stack_rows([gw["cf"][j]["dw_w"] for j in range(n_b)])
    gs_dw_b = stack_rows([gw["cf"][j]["dw_b"] for j in range(n_b)])
    gs_cf_ln_g = stack_rows([gw["cf"][j]["ln_g"] for j in range(n_b)])
    gs_cf_ln_b = stack_rows([gw["cf"][j]["ln_b"] for j in range(n_b)])
    gs_a_log = stack_rows([_pad_lanes(gw["dn"][j]["a_log"], 0)[0] for j in range(n_a)])
    gs_dt_bias = stack_rows([_pad_lanes(gw["dn"][j]["dt_bias"], 0)[0] for j in range(n_a)])
    gs_norm_w = stack_rows([gw["dn"][j]["norm_w"] for j in range(n_a)])
    small_grads = [gs_ln_g, gs_ln_b, gs_conv_w, gs_dw_w, gs_dw_b, gs_cf_ln_g, gs_cf_ln_b, gs_a_log, gs_dt_bias,
                   gs_norm_w, d_mod_full]
    sg_rows = [_rows(a) for a in small_grads]
    sg_sizes = [a.shape[0] for a in sg_rows]
    sg_packed = _pad_rows(jnp.concatenate(sg_rows, axis=0))
    (sg_handle,), sg_token = _xfer_start([sg_packed], [_landing(sg_packed, me)], False, grad_x, "gather_small_grads_start")
    for i in range(depth - 1, -1, -1):
        update_layer(i)
    sg_all = _xfer_wait(sg_handle, False, chains["ff_w1"][0], "gather_small_grads_wait")
    sg_offs = [0]
    for z in sg_sizes:
        sg_offs.append(sg_offs[-1] + z)

    def sg(i, shape):
        return sg_all[:, sg_offs[i]:sg_offs[i + 1], :].reshape((N_DEV,) + shape)

    dmod_all = sg(10, (depth, N_MOD * d))
    nl = ada_w.shape[2]
    dmod_cols = lax.dynamic_slice_in_dim(dmod_all, me * nl, nl, axis=2)
    g_ada_w = _ada_bwd(cond_all, jnp.moveaxis(dmod_cols, 0, 1))

    outs = {}

    def run_adamw(key, parts, w, m, v):
        shp = w.shape
        as3 = lambda t: t.reshape((-1,) + shp[-2:]) if t.ndim >= 3 else t.reshape((1,) + shp)
        parts3 = parts.reshape((parts.shape[0],) + as3(w).shape)
        res = _adamw(parts3, as3(w), as3(m), as3(v), "adamw_" + key)
        outs[key] = tuple(r.reshape(shp) for r in res)

    run_adamw("ada_w", g_ada_w[None], ada_w, m_ada_w, v_ada_w)

    cgroups = 3 * heads // N_DEV
    shard_parts = [
        _shard_cols(sg(0, (depth, 2, d)), me, 1), _shard_cols(sg(1, (depth, 2, d)), me, 1),
        _shard_cols(sg(2, (n_a, DN_CONV, 3 * hw)), me, cgroups), _shard_cols(sg(3, (n_b, taps, d)), me, 1),
        _shard_cols(sg(4, (n_b, d)), me, 1), _shard_cols(sg(5, (n_b, d)), me, 1), _shard_cols(sg(6, (n_b, d)), me, 1),
    ]
    repl_parts = [sg(7, (n_a, LANES)), sg(8, (n_a, LANES)), sg(9, (n_a, HEAD_DIM)),
                  sg(10, (depth, N_MOD * d)).reshape(N_DEV, -1, LANES)]
    small_parts = shard_parts + repl_parts
    sp_sizes = [a.shape[1] for a in small_parts]
    parts_packed = jnp.concatenate(small_parts, axis=1)
    extra = (-parts_packed.shape[1]) % 8
    parts_packed = jnp.pad(parts_packed, ((0, 0), (0, extra), (0, 0)))

    def pad_heads(t):
        return jnp.pad(t, ((0, 0), (0, LANES - heads)))

    def pack_state(ln_g_, ln_b_, conv_w_, dw_w_, dw_b_, cln_g_, cln_b_, a_log_, dt_b_, norm_w_, ada_b_):
        rows = [_rows(ln_g_), _rows(ln_b_), _rows(conv_w_), _rows(dw_w_), _rows(dw_b_), _rows(cln_g_), _rows(cln_b_),
                pad_heads(a_log_), pad_heads(dt_b_), norm_w_, _rows(ada_b_)]
        return _pad_rows(jnp.concatenate(rows, axis=0))

    w_s = pack_state(ln_g, ln_b, dn_conv_w, cf_dw_w, cf_dw_b, cf_ln_g, cf_ln_b, dn_a_log, dn_dt_bias, dn_norm_w, ada_b)
    m_s = pack_state(m_ln_g, m_ln_b, m_dn_conv_w, m_cf_dw_w, m_cf_dw_b, m_cf_ln_g, m_cf_ln_b, m_dn_a_log,
                     m_dn_dt_bias, m_dn_norm_w, m_ada_b)
    v_s = pack_state(v_ln_g, v_ln_b, v_dn_conv_w, v_cf_dw_w, v_cf_dw_b, v_cf_ln_g, v_cf_ln_b, v_dn_a_log,
                     v_dn_dt_bias, v_dn_norm_w, v_ada_b)
    res_s = _adamw(parts_packed[:, None], w_s[None], m_s[None], v_s[None], "adamw_small")
    sp_offs = [0]
    for z in sp_sizes:
        sp_offs.append(sp_offs[-1] + z)
    small_keys = ["ln_g", "ln_b", "dn_conv_w", "cf_dw_w", "cf_dw_b", "cf_ln_g", "cf_ln_b", "dn_a_log", "dn_dt_bias",
                  "dn_norm_w", "ada_b"]
    small_shapes = [ln_g.shape, ln_b.shape, dn_conv_w.shape, cf_dw_w.shape, cf_dw_b.shape, cf_ln_g.shape,
                    cf_ln_b.shape, dn_a_log.shape, dn_dt_bias.shape, dn_norm_w.shape, ada_b.shape]
    for n, (key, shp) in enumerate(zip(small_keys, small_shapes)):
        vals = []
        for r in res_s:
            piece = r[0, sp_offs[n]:sp_offs[n + 1], :]
            if key in ("dn_a_log", "dn_dt_bias"):
                piece = piece[:, :heads]
            vals.append(piece.reshape(shp))
        outs[key] = tuple(vals)

    for key in stacked:
        outs[key] = tuple(chains[key])

    order = ["ada_w", "ada_b", "ln_g", "ln_b", "dn_w_in", "dn_conv_w", "dn_a_log", "dn_dt_bias", "dn_norm_w",
             "dn_w_out", "cf_w_in", "cf_dw_w", "cf_dw_b", "cf_ln_g", "cf_ln_b", "cf_w_out", "ff_w1", "ff_w2"]
    result = [loss, grad_x[None]]
    for part in range(4):
        result += [outs[k][part] for k in order]
    return tuple(result)
```

```python
import functools

import jax
import jax.numpy as jnp
from jax import lax
from jax.experimental import pallas as pl
from jax.experimental.pallas import tpu as pltpu

F32 = jnp.float32
MXU_DTYPE = jnp.bfloat16
N_DEV = 8
LANES = 128
HEAD_DIM = 128
CHUNK = 64
DN_CONV = 4
N_MOD = 6
LN_EPS = 1e-5
RMS_EPS = 1e-6
L2_EPS = 1e-6
ADAM_LR = 0.001
ADAM_B1 = 0.9
ADAM_B2 = 0.999
ADAM_EPS = 1e-08
ADAM_WD = 0.01
ADAM_STEP = 10

NN = ((1,), (0,))
NT = ((1,), (1,))
TN = ((0,), (0,))

ROW_TILE = 512
CONV_TILE = 256
SHORT_CONV_TILE = 1024


def _mdot(a, b, dims=NN):
    return lax.dot_general(a.astype(MXU_DTYPE), b.astype(MXU_DTYPE), (dims, ((), ())), preferred_element_type=F32)


def _split3(x):
    hi = x.astype(MXU_DTYPE)
    r1 = x - hi.astype(F32)
    mid = r1.astype(MXU_DTYPE)
    lo = (r1 - mid.astype(F32)).astype(MXU_DTYPE)
    return hi, mid, lo


def _dot01(a, b, dims=NN, mask_first=True):
    d = lambda p, q: lax.dot_general(p, q, (dims, ((), ())), preferred_element_type=F32)
    if mask_first:
        m = a.astype(MXU_DTYPE)
        return sum(d(m, p) for p in _split3(b))
    m = b.astype(MXU_DTYPE)
    return sum(d(p, m) for p in _split3(a))


def _cparams(n):
    return pltpu.CompilerParams(dimension_semantics=("arbitrary",) * n)


def _call(body, *, grid, ins, outs, name, scratch=()):
    res = pl.pallas_call(
        body,
        grid=grid,
        in_specs=[pl.BlockSpec(memory_space=pl.ANY) if b is None else pl.BlockSpec(b, m) for _, b, m in ins],
        out_specs=[pl.BlockSpec(b, m) for _, _, b, m in outs],
        out_shape=[jax.ShapeDtypeStruct(s, d) for s, d, _, _ in outs],
        scratch_shapes=list(scratch),
        name=name,
        compiler_params=_cparams(len(grid)),
    )(*[a for a, _, _ in ins])
    return res


def _tile(n, pref, unit=LANES):
    if n <= pref:
        return n
    t = (pref // unit) * unit
    while t > unit and n % t:
        t -= unit
    assert n % t == 0, (n, pref)
    return t


def _rowmap(fn, rows, consts, row_outs, acc_outs, name, pin=None):
    rows = [r if isinstance(r, tuple) else (r, r.shape[1], 0) for r in rows]
    s = rows[0][0].shape[0]
    tm = min(ROW_TILE, s)
    nr, nc, no, na = len(rows), len(consts), len(row_outs), len(acc_outs)
    npin = 0 if pin is None else 1

    def body(*refs):
        rin, cin = refs[:nr], refs[nr:nr + nc]
        refs = refs[:nr + nc] + refs[nr + nc + npin:]
        rout, aout = refs[nr + nc:nr + nc + no], refs[nr + nc + no:]
        ro, ao = fn(*[r[...] for r in rin], *[c[...] for c in cin])
        for ref, val in zip(rout, ro):
            ref[...] = val.astype(ref.dtype)
        if na:
            first = pl.program_id(0) == 0

            @pl.when(first)
            def _():
                for ref, val in zip(aout, ao):
                    ref[...] = val

            @pl.when(jnp.logical_not(first))
            def _():
                for ref, val in zip(aout, ao):
                    ref[...] += val

    ins = [(a, (tm, w), functools.partial(lambda i, cb: (i, cb), cb=cb)) for a, w, cb in rows]
    for c in consts:
        if isinstance(c, tuple):
            ins.append((c[0], (None, 1, c[0].shape[2]), functools.partial(lambda i, n: (n, 0, 0), n=c[1])))
        else:
            ins.append((c, c.shape, lambda i: (0, 0)))
    if pin is not None:
        ins.append((pin, None, None))
    outs = [((s, w), d, (tm, w), lambda i: (i, 0)) for w, d in row_outs]
    outs += [(shp, F32, shp, lambda i: (0, 0)) for shp in acc_outs]
    res = _call(body, grid=(s // tm,), ins=ins, outs=outs, name=name)
    return res[:no], res[no:]


def _ln(z, g, b):
    mu = jnp.mean(z, -1, keepdims=True)
    var = jnp.mean(jnp.square(z - mu), -1, keepdims=True)
    return (z - mu) * lax.rsqrt(var + LN_EPS) * g + b


def _combine(alpha, x, y, gt, g, b, sc, sh):
    xn = _ln(alpha * x + (1.0 + gt) * y, g, b)
    return xn, xn * (1.0 + sc) + sh


def _modulate_fwd(x, sc, sh):
    def fn(x, sc, sh):
        return ((x * (1.0 + sc) + sh),), ()

    (h,), _ = _rowmap(fn, [x], [sc, sh], [(x.shape[1], MXU_DTYPE)], [], "modulate_fwd")
    return h


def _modulate_bwd(x, dx, dh, sc, sh, pin=None):
    d = x.shape[1]

    def fn(x, dx, dh, sc, sh):
        _, vjp = jax.vjp(lambda x, sc, sh: x * (1.0 + sc) + sh, x, sc, sh)
        gx, gsc, gsh = vjp(dh)
        return (dx + gx,), (gsc, gsh)

    (gx,), (gsc, gsh) = _rowmap(fn, [x, dx, dh], [sc, sh], [(d, F32)], [(1, d), (1, d)], "modulate_bwd", pin=pin)
    return gx, gsc, gsh


def _combine_fwd(alpha, x, y, gt, g, b, sc, sh):
    d = x.shape[1]

    def fn(x, y, gt, g, b, sc, sh):
        return _combine(alpha, x, y, gt, g, b, sc, sh), ()

    (xn, h), _ = _rowmap(fn, [x, y], [gt, g, b, sc, sh], [(d, F32), (d, MXU_DTYPE)], [], "combine_fwd")
    return xn, h


def _combine_bwd(alpha, x, y, dxn, dh, gt, g, b, sc, sh, pin=None):
    d = x.shape[1]

    def fn(x, y, dxn, dh, gt, g, b, sc, sh):
        _, vjp = jax.vjp(functools.partial(_combine, alpha), x, y, gt, g, b, sc, sh)
        gx, gy, ggt, gg, gb, gsc, gsh = vjp((dxn, dh))
        return (gx, gy), (ggt, gg, gb, gsc, gsh)

    (gx, gy), accs = _rowmap(fn, [x, y, dxn, dh], [gt, g, b, sc, sh], [(d, F32), (d, MXU_DTYPE)],
                             [(1, d)] * 5, "combine_bwd", pin=pin)
    return gx, gy, accs


def _last_fwd_bwd(alpha, x, y, tgt, gt, g, b):
    d = x.shape[1]

    def fn(x, y, tgt, gt, g, b):
        xn, vjp = jax.vjp(lambda x, y, gt, g, b: _ln(alpha * x + (1.0 + gt) * y, g, b), x, y, gt, g, b)
        err = xn - tgt
        gx, gy, ggt, gg, gb = vjp(err * (1.0 / d))
        rows = jnp.sum(jnp.square(err), axis=-1, keepdims=True)
        loss = (0.5 / d) * jnp.sum(rows, axis=0, keepdims=True) * jnp.ones((1, LANES), F32)
        return (gx, gy), (loss, ggt, gg, gb)

    (gx, gy), accs = _rowmap(fn, [x, y, tgt], [gt, g, b], [(d, F32), (d, MXU_DTYPE)],
                             [(1, LANES), (1, d), (1, d), (1, d)], "last_fwd_bwd")
    return gx, gy, accs


MM_VMEM_BUDGET = 40 * 2 ** 20


def _fit(options, cost):
    for o in options:
        if 2 * cost(o) <= MM_VMEM_BUDGET:
            return o
    return options[-1]


def _row_tiles(m):
    return [t for t in (2048, 1024, 512, 256) if t <= m and m % t == 0] or [m]


def _mm_call(a, a_blk, a_map, b, b_blk, b_map, outs, dims, grid, name, epi=None, extra=None, split=None, blocks=None):
    nk = grid[2]
    n_out = len(outs)
    n_in = 3 if extra is not None else 2

    def body(*refs):
        a_ref, b_ref = refs[0], refs[1]
        rest = refs[n_in:]
        out_refs = rest[:n_out]

        def finish(val):
            if epi == "relu2":
                out_refs[0][...] = jnp.square(jnp.maximum(val, 0.0)).astype(out_refs[0].dtype)
            elif epi == "relu2_bwd":
                sq = refs[2][...].astype(F32)
                root = jnp.where(sq > 0.0, sq * lax.rsqrt(sq), 0.0)
                out_refs[0][...] = (val * 2.0 * root).astype(out_refs[0].dtype)
            elif split is not None:
                for g in range(split[0]):
                    out_refs[0][g] = val[:, g * split[1]:(g + 1) * split[1]].astype(out_refs[0].dtype)
            else:
                out_refs[0][...] = val.astype(out_refs[0].dtype)

        if blocks is None:
            p = lax.dot_general(a_ref[...], b_ref[...], (dims, ((), ())), preferred_element_type=F32)
        else:
            p = None
            for g in range(blocks[0]):
                part = lax.dot_general(a_ref[:, g * blocks[1]:(g + 1) * blocks[1]], b_ref[g], (dims, ((), ())),
                                       preferred_element_type=F32)
                p = part if p is None else p + part
        if nk == 1:
            finish(p)
        else:
            acc = rest[n_out]
            k = pl.program_id(2)

            @pl.when(k == 0)
            def _():
                acc[...] = p

            @pl.when(k > 0)
            def _():
                acc[...] += p

            @pl.when(k == nk - 1)
            def _():
                finish(acc[...])

    if nk > 1:
        out_blk = tuple(x for x in outs[0][2] if x is not None)
        if split is not None:
            out_blk = (out_blk[1], split[0] * split[1])
        scratch = [pltpu.VMEM(out_blk, F32)]
    else:
        scratch = []
    ins = [(a, a_blk, a_map), (b, b_blk, b_map)] + ([extra] if extra is not None else [])
    return _call(body, grid=grid, ins=ins, outs=outs, name=name, scratch=scratch)


def _isz(dt):
    return jnp.dtype(dt).itemsize


def _mm_nn(a, b, out_dtype, name, relu2=False):
    m, kdim = a.shape
    if b.ndim == 2:
        n = b.shape[1]
        tn = _tile(n, 1536 if n > 2048 else 512)
        b_blk, b_map = (kdim, tn), lambda i, j, k: (0, j)
    else:
        g, _, ng = b.shape
        n = g * ng
        tn = _tile(ng, 512)
        b_blk = (None, kdim, tn)
        b_map = functools.partial(lambda i, j, k, npg: (j // npg, 0, j % npg), npg=ng // tn)
    tm = _fit(_row_tiles(m), lambda t: t * kdim * _isz(a.dtype) + kdim * tn * _isz(b.dtype) + t * tn * _isz(out_dtype))
    grid = (m // tm, n // tn, 1)
    outs = [((m, n), out_dtype, (tm, tn), lambda i, j, k: (i, j))]
    return _mm_call(a, (tm, kdim), lambda i, j, k: (i, 0), b, b_blk, b_map, outs, NN, grid, name,
                    epi="relu2" if relu2 else None)[0]


def _mm_nt(a, b, out_dtype, name, relu2_sq=None):
    m, n = a.shape
    extra_bytes = _isz(relu2_sq.dtype) if relu2_sq is not None else 0
    if b.ndim == 2:
        kout = b.shape[0]
        to = _tile(kout, 512)
        b_blk, b_map, blocks = (to, n), lambda i, j, k: (j, 0), None
    else:
        g, kout, ng = b.shape
        to = _tile(kout, 512)
        b_blk, b_map, blocks = (g, to, ng), lambda i, j, k: (0, j, 0), (g, ng)
    tm = _fit(_row_tiles(m), lambda t: t * n * _isz(a.dtype) + to * n * _isz(b.dtype)
              + t * to * (_isz(out_dtype) + extra_bytes))
    grid = (m // tm, kout // to, 1)
    outs = [((m, kout), out_dtype, (tm, to), lambda i, j, k: (i, j))]
    extra = (relu2_sq, (tm, to), lambda i, j, k: (i, j)) if relu2_sq is not None else None
    return _mm_call(a, (tm, n), lambda i, j, k: (i, 0), b, b_blk, b_map, outs, NT, grid, name,
                    epi="relu2_bwd" if relu2_sq is not None else None, extra=extra, blocks=blocks)[0]


def _mm_tn(a, b, out_dtype, name, split_cols=False):
    m, kdim = a.shape
    n = b.shape[1]
    tk = _tile(kdim, 512)
    tn = _tile(n, 1536)
    if not split_cols:
        out, split = ((kdim, n), out_dtype, (tk, tn), lambda i, j, k: (i, j)), None
    else:
        ng = n // N_DEV
        if tn % ng:
            tn = _tile(ng, 512)
        if tn >= ng:
            gb = tn // ng
            out = ((N_DEV, kdim, ng), out_dtype, (gb, tk, ng), lambda i, j, k: (j, i, 0))
            split = (gb, ng)
        else:
            out = ((N_DEV, kdim, ng), out_dtype, (None, tk, tn),
                   functools.partial(lambda i, j, k, npg: (j // npg, i, j % npg), npg=ng // tn))
            split = None
    grid = (kdim // tk, n // tn, 1)
    return _mm_call(a, (m, tk), lambda i, j, k: (0, i), b, (m, tn), lambda i, j, k: (0, j), [out], TN, grid, name,
                    split=split)[0]


def _shifted(xa, off, rows):
    if off % 8 == 0:
        return xa[off:off + rows]
    return pltpu.roll(xa, xa.shape[0] - off, 0)[:rows]


def _conv_pad(taps):
    return -(-(taps - 1) // 8) * 8


def _conv_tile(xp_ref, w, i, rows, taps):
    pad = _conv_pad(taps)
    r0 = pl.multiple_of(i * rows, rows)
    xa = xp_ref[pl.ds(r0, rows + pad), :]
    views = [_shifted(xa, pad - (taps - 1) + j, rows) for j in range(taps)]
    acc = w[0:1, :] * views[0]
    for j in range(1, taps):
        acc = acc + w[j:j + 1, :] * views[j]
    return r0, acc, views


def _conv_back_tile(yp_ref, w, i, rows, taps):
    pad = _conv_pad(taps)
    r0 = pl.multiple_of(i * rows, rows)
    ya = yp_ref[pl.ds(r0, rows + pad), :]
    acc = w[taps - 1:taps, :] * ya[:rows]
    for j in range(taps - 1):
        acc = acc + w[j:j + 1, :] * _shifted(ya, taps - 1 - j, rows)
    return r0, acc


def _tap_sums(dy, views, taps):
    row = lax.broadcasted_iota(jnp.int32, (taps, LANES), 0)
    acc = jnp.zeros((taps, LANES), F32)
    for j in range(taps):
        acc = acc + jnp.where(row == j, jnp.sum(dy * views[j], axis=0, keepdims=True), 0.0)
    return acc


def _silu_l2(xc, l2):
    a = jax.nn.silu(xc)
    if l2:
        a = a * lax.rsqrt(jnp.sum(a * a, axis=-1, keepdims=True) + L2_EPS)
    return a


def _dn_conv_fwd(proj, conv_w, c0, nblk, l2, name):
    s = proj.shape[0]
    pad = _conv_pad(DN_CONV)
    rows = min(SHORT_CONV_TILE, s)

    def body(x_ref, w_ref, o_ref, xp):
        xp[0:pad, :] = jnp.zeros((pad, LANES), F32)
        xp[pad:, :] = x_ref[...]
        w = w_ref[...]

        def tile(i, c):
            r0, acc, _ = _conv_tile(xp, w, i, rows, DN_CONV)
            o_ref[pl.ds(r0, rows), :] = _silu_l2(acc, l2)
            return c

        lax.fori_loop(0, s // rows, tile, 0)

    return _call(body, grid=(nblk,),
                 ins=[(proj, (s, LANES), lambda c: (0, c0 + c)), (conv_w, (DN_CONV, LANES), lambda c: (0, c0 + c))],
                 outs=[((nblk, s, LANES), F32, (None, s, LANES), lambda c: (c, 0, 0))],
                 name=name, scratch=[pltpu.VMEM((s + pad, LANES), F32)])[0]


def _dn_conv_bwd(proj, conv_w, da, c0, nblk, l2, name):
    s = proj.shape[0]
    pad = _conv_pad(DN_CONV)
    rows = min(SHORT_CONV_TILE, s)

    def body(x_ref, w_ref, da_ref, dx_ref, dw_ref, xp, yp):
        xp[0:pad, :] = jnp.zeros((pad, LANES), F32)
        xp[pad:, :] = x_ref[...]
        yp[s:, :] = jnp.zeros((pad, LANES), F32)
        w = w_ref[...]

        def tile(i, dw):
            r0, acc, views = _conv_tile(xp, w, i, rows, DN_CONV)
            _, vjp = jax.vjp(functools.partial(_silu_l2, l2=l2), acc)
            (dxc,) = vjp(da_ref[pl.ds(r0, rows), :])
            yp[pl.ds(r0, rows), :] = dxc
            return dw + _tap_sums(dxc, views, DN_CONV)

        dw_ref[...] = lax.fori_loop(0, s // rows, tile, jnp.zeros((DN_CONV, LANES), F32))

        def tile2(i, c):
            r0, acc = _conv_back_tile(yp, w, i, rows, DN_CONV)
            dx_ref[pl.ds(r0, rows), :] = acc.astype(dx_ref.dtype)
            return c

        lax.fori_loop(0, s // rows, tile2, 0)

    return _call(body, grid=(nblk,),
                 ins=[(proj, (s, LANES), lambda c: (0, c0 + c)), (conv_w, (DN_CONV, LANES), lambda c: (0, c0 + c)),
                      (da, (None, s, LANES), lambda c: (c, 0, 0))],
                 outs=[((s, nblk * LANES), MXU_DTYPE, (s, LANES), lambda c: (0, c)),
                       ((DN_CONV, nblk * LANES), F32, (DN_CONV, LANES), lambda c: (0, c))],
                 name=name, scratch=[pltpu.VMEM((s + pad, LANES), F32), pltpu.VMEM((s + pad, LANES), F32)])


def _cf_conv_fwd(vg, dw_w, dw_b):
    s, c2 = vg.shape
    ch = c2 // 2
    nblk = ch // LANES
    taps = dw_w.shape[0]
    pad = _conv_pad(taps)
    rows = min(CONV_TILE, s)

    def body(v_ref, g_ref, w_ref, b_ref, o_ref, xp):
        xp[0:pad, :] = jnp.zeros((pad, LANES), F32)
        xp[pad:, :] = v_ref[...] * jax.nn.sigmoid(g_ref[...])
        w = w_ref[...]
        bias = b_ref[...]

        def tile(i, c):
            r0, acc, _ = _conv_tile(xp, w, i, rows, taps)
            o_ref[pl.ds(r0, rows), :] = acc + bias
            return c

        lax.fori_loop(0, s // rows, tile, 0)

    return _call(body, grid=(nblk,),
                 ins=[(vg, (s, LANES), lambda c: (0, c)), (vg, (s, LANES), lambda c: (0, nblk + c)),
                      (dw_w, (taps, LANES), lambda c: (0, c)), (dw_b, (1, LANES), lambda c: (0, c))],
                 outs=[((s, ch), F32, (s, LANES), lambda c: (0, c))],
                 name="cf_conv_fwd", scratch=[pltpu.VMEM((s + pad, LANES), F32)])[0]


def _cf_conv_bwd(vg, dw_w, du):
    s, c2 = vg.shape
    ch = c2 // 2
    nblk = ch // LANES
    taps = dw_w.shape[0]
    pad = _conv_pad(taps)
    rows = min(CONV_TILE, s)

    def body(v_ref, g_ref, w_ref, du_ref, dv_ref, dg_ref, dw_ref, db_ref, xp, yp):
        sig = jax.nn.sigmoid(g_ref[...])
        xp[0:pad, :] = jnp.zeros((pad, LANES), F32)
        xp[pad:, :] = v_ref[...] * sig
        yp[0:s, :] = du_ref[...]
        yp[s:, :] = jnp.zeros((pad, LANES), F32)
        w = w_ref[...]
        db_ref[...] = jnp.sum(du_ref[...], axis=0, keepdims=True)

        def tile(i, dw):
            r0, _, views = _conv_tile(xp, w, i, rows, taps)
            return dw + _tap_sums(du_ref[pl.ds(r0, rows), :], views, taps)

        dw_ref[...] = lax.fori_loop(0, s // rows, tile, jnp.zeros((taps, LANES), F32))

        def tile2(i, c):
            r0, du0 = _conv_back_tile(yp, w, i, rows, taps)
            val = v_ref[pl.ds(r0, rows), :]
            sg = jax.nn.sigmoid(g_ref[pl.ds(r0, rows), :])
            dv_ref[pl.ds(r0, rows), :] = (du0 * sg).astype(dv_ref.dtype)
            dg_ref[pl.ds(r0, rows), :] = (du0 * val * sg * (1.0 - sg)).astype(dg_ref.dtype)
            return c

        lax.fori_loop(0, s // rows, tile2, 0)

    return _call(body, grid=(nblk,),
                 ins=[(vg, (s, LANES), lambda c: (0, c)), (vg, (s, LANES), lambda c: (0, nblk + c)),
                      (dw_w, (taps, LANES), lambda c: (0, c)), (du, (s, LANES), lambda c: (0, c))],
                 outs=[((s, ch), MXU_DTYPE, (s, LANES), lambda c: (0, c)),
                       ((s, ch), MXU_DTYPE, (s, LANES), lambda c: (0, c)),
                       ((taps, ch), F32, (taps, LANES), lambda c: (0, c)),
                       ((1, ch), F32, (1, LANES), lambda c: (0, c))],
                 name="cf_conv_bwd", scratch=[pltpu.VMEM((s + pad, LANES), F32), pltpu.VMEM((s + pad, LANES), F32)])


def _masks():
    r = lax.broadcasted_iota(jnp.int32, (CHUNK, CHUNK), 0)
    c = lax.broadcasted_iota(jnp.int32, (CHUNK, CHUNK), 1)
    return r >= c, r > c, r <= c


def _chunk_decay(g):
    causal, _, upper = _masks()
    gb = jnp.broadcast_to(g, (CHUNK, CHUNK))
    gam_r = _dot01(jnp.where(causal, 1.0, 0.0), gb)
    gam_s = _dot01(jnp.ones((CHUNK, CHUNK), F32), jnp.where(upper, gb, 0.0))
    dm = jnp.where(causal, jnp.exp(jnp.where(causal, gam_r - gam_s, 0.0)), 0.0)
    return gam_r[:, 0:1], dm


def _chunk_scores(q, k, beta, dm):
    _, strict, _ = _masks()
    both = _mdot(jnp.concatenate([k * beta, q * (HEAD_DIM ** -0.5)], axis=0), k, NT)
    return jnp.where(strict, both[:CHUNK] * dm, 0.0), both[CHUNK:] * dm


def _lockstep(gens):
    results = [None] * len(gens)
    alive = list(range(len(gens)))
    while alive:
        for i in list(alive):
            try:
                next(gens[i])
            except StopIteration as stop:
                results[i] = stop.value
                alive.remove(i)
    return results


def _chunk_prep_bwd(q, k, v, beta, gam, t, du, dw, daqk, dqd, dkd, dgl):
    causal, strict, _ = _masks()
    r = lax.broadcasted_iota(jnp.int32, (CHUNK, CHUNK), 0)
    c = lax.broadcasted_iota(jnp.int32, (CHUNK, CHUNK), 1)
    scale = HEAD_DIM ** -0.5
    eg = jnp.exp(gam)
    gam_last = gam[CHUNK - 1:CHUNK, :]
    rr = jnp.exp(gam_last - gam)
    kb = k * beta
    qs = q * scale
    vb = v * beta
    kbe = kb * eg
    gam_b = jnp.broadcast_to(gam, (CHUNK, CHUNK))
    gam_s = _dot01(jnp.ones((CHUNK, CHUNK), F32), jnp.where(r == c, gam_b, 0.0))
    both = _mdot(jnp.concatenate([kb, qs], axis=0), k, NT)
    duw = jnp.concatenate([du, dw], axis=1)
    dt = _mdot(duw, jnp.concatenate([vb, kbe], axis=1), NT)
    dvk = _mdot(t, duw, TN)
    yield
    dm = jnp.where(causal, jnp.exp(jnp.where(causal, gam_b - gam_s, 0.0)), 0.0)
    a = jnp.where(strict, both[:CHUNK] * dm, 0.0)
    aqk = both[CHUNK:] * dm
    dvb, dkbe = dvk[:, :HEAD_DIM], dvk[:, HEAD_DIM:]
    x = _mdot(t, dt, TN)
    yield
    da = jnp.where(strict, -_mdot(x, t, NT), 0.0)
    yield
    dkk = da * dm
    dqk = daqk * dm
    ddiff = da * a + daqk * aqk
    dboth = jnp.concatenate([dkk, dqk], axis=0)
    dkq = _mdot(dboth, k)
    dk_mm = _mdot(dboth, jnp.concatenate([kb, qs], axis=0), TN)
    colsum = _dot01(ddiff, jnp.ones((CHUNK, LANES), F32), TN, mask_first=False)[:, 0:1]
    yield
    dkb = dkq[:CHUNK] + dkbe * eg
    dk = dk_mm + dkb * beta + dkd * rr
    dq = (dkq[CHUNK:] + dqd * eg) * scale
    dbeta = jnp.sum(dkb * k, axis=-1, keepdims=True) + jnp.sum(dvb * v, axis=-1, keepdims=True)
    dv = dvb * beta
    deg = jnp.sum(dkbe * kb, axis=-1, keepdims=True) + jnp.sum(dqd * qs, axis=-1, keepdims=True)
    drr = jnp.sum(dkd * k, axis=-1, keepdims=True)
    dgam = deg * eg - drr * rr + jnp.sum(ddiff, axis=-1, keepdims=True) - colsum
    dgam_last = jnp.sum(drr * rr, axis=0, keepdims=True) + dgl[0:1, :] * jnp.exp(gam_last)
    row = lax.broadcasted_iota(jnp.int32, (CHUNK, 1), 0)
    dgam = dgam + jnp.where(row == CHUNK - 1, dgam_last, 0.0)
    dg = _dot01(jnp.where(causal, 1.0, 0.0), jnp.broadcast_to(dgam, (CHUNK, LANES)), TN)[:, 0:1]
    return dq, dk, dv, dbeta, dg


def _prep_group(s):
    nch = s // CHUNK
    return next(c for c in (16, 8, 4, 2, 1) if nch % c == 0)


def _tri_solve_lanes(a_l):
    n = a_l.shape[1]
    group = 8

    def body(a_ref, t_ref):
        t_ref[...] = jnp.zeros_like(t_ref)
        col = lax.broadcasted_iota(jnp.int32, (CHUNK, n), 0)

        def row(r, carry):
            r0 = pl.multiple_of(r * CHUNK, CHUNK)

            def inner(sg, acc):
                a8 = a_ref[pl.ds(r0 + pl.multiple_of(sg * group, group), group), :]
                for j in range(group):
                    t0 = pl.multiple_of((sg * group + j) * CHUNK, CHUNK)
                    acc = acc + a8[j:j + 1, :] * t_ref[pl.ds(t0, CHUNK), :]
                return acc

            acc = lax.fori_loop(0, (r + group - 1) // group, inner, jnp.zeros((CHUNK, n), F32))
            t_ref[pl.ds(r0, CHUNK), :] = jnp.where(col == r, 1.0, 0.0) - acc
            return carry

        lax.fori_loop(0, CHUNK, row, 0)

    return pl.pallas_call(body, out_shape=jax.ShapeDtypeStruct(a_l.shape, F32), name="dn_tri_solve")(a_l)


def _head_cols(bg, hh, heads):
    lane = lax.broadcasted_iota(jnp.int32, bg.shape, 1)
    beta = jnp.sum(jnp.where(lane == hh, bg, 0.0), axis=-1, keepdims=True)
    g = jnp.sum(jnp.where(lane == heads + hh, bg, 0.0), axis=-1, keepdims=True)
    return beta, g


def _dn_prep(q, k, v, bg):
    h, s, _ = q.shape
    cb = _prep_group(s)
    rb = cb * CHUNK
    big = lambda x: (x, (None, rb, HEAD_DIM), lambda n, hh: (hh, n, 0))
    sq = lambda x: (x, (None, rb, CHUNK), lambda n, hh: (hh, n, 0))
    col = lambda x: (x, (None, rb, 1), lambda n, hh: (hh, n, 0))
    tok = (bg, (rb, LANES), lambda n, hh: (n, 0))
    o_big = ((h, s, HEAD_DIM), F32, (None, rb, HEAD_DIM), lambda n, hh: (hh, n, 0))
    o_sq = ((h, s, CHUNK), F32, (None, rb, CHUNK), lambda n, hh: (hh, n, 0))
    o_col = ((h, s, 1), F32, (None, rb, 1), lambda n, hh: (hh, n, 0))

    def scores(q_ref, k_ref, bg_ref, a_ref, aqk_ref, gam_ref):
        beta, g = _head_cols(bg_ref[...], pl.program_id(1), h)
        for i in range(cb):
            sl = slice(i * CHUNK, (i + 1) * CHUNK)
            gam, dm = _chunk_decay(g[sl])
            a_ref[sl, :], aqk_ref[sl, :] = _chunk_scores(q_ref[sl, :], k_ref[sl, :], beta[sl], dm)
            gam_ref[sl, :] = gam

    a, aqk, gam = _call(scores, grid=(s // rb, h), ins=[big(q), big(k), tok], outs=[o_sq, o_sq, o_col],
                        name="dn_scores")
    n_prob = h * (s // CHUNK)
    t_l = _tri_solve_lanes(jnp.transpose(a.reshape(n_prob, CHUNK * CHUNK)))
    t = jnp.transpose(t_l).reshape(h, s, CHUNK)

    def wy(k_ref, v_ref, bg_ref, gam_ref, t_ref, u_ref, w_ref):
        beta, _ = _head_cols(bg_ref[...], pl.program_id(1), h)
        for i in range(cb):
            sl = slice(i * CHUNK, (i + 1) * CHUNK)
            kb = k_ref[sl, :] * beta[sl]
            rhs = jnp.concatenate([v_ref[sl, :] * beta[sl], kb * jnp.exp(gam_ref[sl, :])], axis=1)
            uw = _mdot(t_ref[sl, :], rhs)
            u_ref[sl, :] = uw[:, :HEAD_DIM]
            w_ref[sl, :] = uw[:, HEAD_DIM:]

    u, w = _call(wy, grid=(s // rb, h), ins=[big(k), big(v), tok, col(gam), sq(t)], outs=[o_big, o_big],
                 name="dn_wy")
    return u, w, aqk, t, gam


def _dn_prep_bwd(q, k, v, bg, gam, t, du, dw, daqk, dqd, dkd, dgl):
    h, s, _ = q.shape
    cb = _prep_group(s)
    rb = cb * CHUNK

    def body(q_ref, k_ref, v_ref, bg_ref, g_ref, t_ref, du_ref, dw_ref, da_ref, dqd_ref, dkd_ref, dgl_ref,
             dq_ref, dk_ref, dv_ref, dbg_ref):
        hh = pl.program_id(1)
        beta, _ = _head_cols(bg_ref[...], hh, h)
        slices = [slice(i * CHUNK, (i + 1) * CHUNK) for i in range(cb)]
        results = _lockstep([_chunk_prep_bwd(
            q_ref[sl, :], k_ref[sl, :], v_ref[sl, :], beta[sl], g_ref[sl, :], t_ref[sl, :],
            du_ref[sl, :], dw_ref[sl, :], da_ref[sl, :], dqd_ref[sl, :], dkd_ref[sl, :], dgl_ref[sl, :])
            for sl in slices])

        @pl.when(hh == 0)
        def _():
            dbg_ref[...] = jnp.zeros_like(dbg_ref)

        lane = lax.broadcasted_iota(jnp.int32, (CHUNK, LANES), 1)
        for sl, (dq, dk, dv, dbeta, dg) in zip(slices, results):
            dq_ref[sl, :] = dq
            dk_ref[sl, :] = dk
            dv_ref[sl, :] = dv
            dbg_ref[sl, :] += jnp.where(lane == hh, dbeta, 0.0) + jnp.where(lane == h + hh, dg, 0.0)

    big = lambda x: (x, (None, rb, HEAD_DIM), lambda n, hh: (hh, n, 0))
    sq = lambda x: (x, (None, rb, CHUNK), lambda n, hh: (hh, n, 0))
    col = lambda x: (x, (None, rb, 1), lambda n, hh: (hh, n, 0))
    tok = (bg, (rb, LANES), lambda n, hh: (n, 0))
    o_big = ((h, s, HEAD_DIM), F32, (None, rb, HEAD_DIM), lambda n, hh: (hh, n, 0))
    return _call(body, grid=(s // rb, h),
                 ins=[big(q), big(k), big(v), tok, col(gam), sq(t), big(du), big(dw), sq(daqk), big(dqd), big(dkd),
                      col(dgl)],
                 outs=[o_big, o_big, o_big, ((s, LANES), F32, (rb, LANES), lambda n, hh: (n, 0))], name="dn_prep_bwd")


def _chunk_scaled(q, k, gam):
    gam_last = gam[CHUNK - 1:CHUNK, :]
    q_dec = q * (HEAD_DIM ** -0.5) * jnp.exp(gam)
    k_dec = k * jnp.exp(gam_last - gam)
    return q_dec, k_dec, jnp.exp(gam_last)


def _scan_group(s):
    return 2 if (s // CHUNK) % 2 == 0 else 1


def _dn_scan(q, k, u, w, aqk, gam):
    h, s, _ = q.shape
    nch = s // CHUNK
    sg = _scan_group(s)
    rb = sg * CHUNK

    def body(q_ref, k_ref, u_ref, w_ref, a_ref, gam_ref, o_ref, st_ref, state):
        @pl.when(pl.program_id(0) == 0)
        def _():
            state[...] = jnp.zeros_like(state)

        def head(hh, c):
            sl = slice(c * CHUNK, (c + 1) * CHUNK)
            s0 = state[hh]
            st_ref[c, hh] = s0
            q_dec, k_dec, gl = _chunk_scaled(q_ref[hh, sl, :], k_ref[hh, sl, :], gam_ref[hh, sl, :])
            both = _mdot(jnp.concatenate([w_ref[hh, sl, :], q_dec], axis=0), s0)
            yield
            v_new = u_ref[hh, sl, :] - both[:CHUNK]
            o_ref[sl, hh * HEAD_DIM:(hh + 1) * HEAD_DIM] = both[CHUNK:] + _mdot(a_ref[hh, sl, :], v_new)
            state[hh] = s0 * gl + _mdot(k_dec, v_new, TN)

        for c in range(sg):
            _lockstep([head(hh, c) for hh in range(h)])

    big = lambda x: (x, (h, rb, HEAD_DIM), lambda n: (0, n, 0))
    return _call(body, grid=(nch // sg,),
                 ins=[big(q), big(k), big(u), big(w), (aqk, (h, rb, CHUNK), lambda n: (0, n, 0)),
                      (gam, (h, rb, 1), lambda n: (0, n, 0))],
                 outs=[((s, h * HEAD_DIM), F32, (rb, h * HEAD_DIM), lambda n: (n, 0)),
                       ((nch, h, HEAD_DIM, HEAD_DIM), F32, (sg, h, HEAD_DIM, HEAD_DIM), lambda n: (n, 0, 0, 0))],
                 name="dn_scan", scratch=[pltpu.VMEM((h, HEAD_DIM, HEAD_DIM), F32)])


def _dn_scan_bwd(q, k, u, w, aqk, gam, states, do):
    h, s, _ = q.shape
    nch = s // CHUNK
    sg = _scan_group(s)
    rb = sg * CHUNK
    ngr = nch // sg

    def body(q_ref, k_ref, u_ref, w_ref, a_ref, gam_ref, st_ref, do_ref,
             du_ref, dw_ref, da_ref, dqd_ref, dkd_ref, dgl_ref, dstate):
        @pl.when(pl.program_id(0) == 0)
        def _():
            dstate[...] = jnp.zeros_like(dstate)

        def head(hh, c):
            sl = slice(c * CHUNK, (c + 1) * CHUNK)
            s0 = st_ref[c, hh]
            ds = dstate[hh]
            doh = do_ref[sl, hh * HEAD_DIM:(hh + 1) * HEAD_DIM]
            wv = w_ref[hh, sl, :]
            q_dec, k_dec, gl = _chunk_scaled(q_ref[hh, sl, :], k_ref[hh, sl, :], gam_ref[hh, sl, :])
            ws = _mdot(wv, s0)
            dv_new = _mdot(a_ref[hh, sl, :], doh, TN) + _mdot(k_dec, ds)
            dqd_ref[hh, sl, :] = _mdot(doh, s0, NT)
            qdo = _mdot(q_dec, doh, TN)
            tot = jnp.sum(jnp.sum(s0 * ds, axis=-1, keepdims=True), axis=0, keepdims=True)
            dgl_ref[hh, sl, :] = jnp.broadcast_to(tot, (CHUNK, 1))
            yield
            v_new = u_ref[hh, sl, :] - ws
            du_ref[hh, sl, :] = dv_new
            dw_ref[hh, sl, :] = -_mdot(dv_new, s0, NT)
            da_ref[hh, sl, :] = _mdot(doh, v_new, NT)
            dkd_ref[hh, sl, :] = _mdot(v_new, ds, NT)
            dstate[hh] = ds * gl + qdo - _mdot(wv, dv_new, TN)

        for c in range(sg - 1, -1, -1):
            _lockstep([head(hh, c) for hh in range(h)])

    rev = lambda n: (0, ngr - 1 - n, 0)
    big = lambda x: (x, (h, rb, HEAD_DIM), rev)
    o_big = ((h, s, HEAD_DIM), F32, (h, rb, HEAD_DIM), rev)
    return _call(body, grid=(ngr,),
                 ins=[big(q), big(k), big(u), big(w), (aqk, (h, rb, CHUNK), rev), (gam, (h, rb, 1), rev),
                      (states, (sg, h, HEAD_DIM, HEAD_DIM), lambda n: (ngr - 1 - n, 0, 0, 0)),
                      (do, (rb, h * HEAD_DIM), lambda n: (ngr - 1 - n, 0))],
                 outs=[o_big, o_big, ((h, s, CHUNK), F32, (h, rb, CHUNK), rev), o_big, o_big,
                       ((h, s, 1), F32, (h, rb, 1), rev)],
                 name="dn_scan_bwd", scratch=[pltpu.VMEM((h, HEAD_DIM, HEAD_DIM), F32)])


def _gates(x, a_log, dt_b, h):
    lane = lax.broadcasted_iota(jnp.int32, x.shape, 1)
    return jnp.where(lane < h, jax.nn.sigmoid(x), -jnp.exp(a_log) * jax.nn.softplus(x + dt_b))


def _head_out(oh, zh, nw):
    on = oh * lax.rsqrt(jnp.mean(oh * oh, axis=-1, keepdims=True) + RMS_EPS) * nw
    return on * jax.nn.silu(zh)


def _pad_lanes(x, lo):
    return jnp.zeros((1, LANES), F32).at[0, lo:lo + x.shape[0]].set(x)


def _deltanet_fwd(hin, get_w_in, conv_w, a_log, dt_bias, norm_w, get_w_out):
    h = a_log.shape[0]
    hw = h * HEAD_DIM
    w_in = get_w_in(hin)
    proj = _mm_nn(hin, w_in, F32, "dn_proj")
    q = _dn_conv_fwd(proj, conv_w, 0, h, True, "dn_conv_q")
    k = _dn_conv_fwd(proj, conv_w, h, h, True, "dn_conv_k")
    v = _dn_conv_fwd(proj, conv_w, 2 * h, h, False, "dn_conv_v")
    alp, dtp = _pad_lanes(a_log, h), _pad_lanes(dt_bias, h)

    def gates_fn(x, al, db):
        return (_gates(x, al, db, h),), ()

    (bg,), _ = _rowmap(gates_fn, [(proj, LANES, 4 * h)], [alp, dtp], [(LANES, F32)], [], "dn_gates")
    u, w, aqk, t, gam = _dn_prep(q, k, v, bg)
    o, states = _dn_scan(q, k, u, w, aqk, gam)
    nw = norm_w[None, :]

    def out_fn(o, z, nw):
        parts = [_head_out(o[:, i * HEAD_DIM:(i + 1) * HEAD_DIM], z[:, i * HEAD_DIM:(i + 1) * HEAD_DIM], nw)
                 for i in range(h)]
        return (jnp.concatenate(parts, axis=-1),), ()

    (og,), _ = _rowmap(out_fn, [o, (proj, hw, 3)], [nw], [(hw, MXU_DTYPE)], [], "dn_out")
    w_out = get_w_out(og)
    y = _mm_nn(og, w_out, F32, "dn_y")
    return y, (hin, proj, q, k, v, bg, u, w, aqk, t, gam, states, o, og, alp, dtp, nw, w_in, w_out)


def _deltanet_bwd(res, dy, conv_w):
    hin, proj, q, k, v, bg, u, w, aqk, t, gam, states, o, og, alp, dtp, nw, w_in, w_out = res
    h = q.shape[0]
    hw = h * HEAD_DIM
    s = hin.shape[0]
    d_w_out = _mm_tn(og, dy, MXU_DTYPE, "dn_dwout")
    dog = _mm_nt(dy, w_out, F32, "dn_dog")

    def out_bwd(o, z, dog, nw):
        dos, dzs = [], []
        dn = jnp.zeros((1, HEAD_DIM), F32)
        for i in range(h):
            sl = slice(i * HEAD_DIM, (i + 1) * HEAD_DIM)
            _, vjp = jax.vjp(_head_out, o[:, sl], z[:, sl], nw)
            a, b, c = vjp(dog[:, sl])
            dos.append(a)
            dzs.append(b)
            dn = dn + c
        return (jnp.concatenate(dos, axis=-1), jnp.concatenate(dzs, axis=-1)), (dn,)

    (do, dz), (d_norm_w,) = _rowmap(out_bwd, [o, (proj, hw, 3), dog], [nw], [(hw, F32), (hw, MXU_DTYPE)],
                                    [(1, HEAD_DIM)], "dn_out_bwd")
    du, dw, daqk, dqd, dkd, dgl = _dn_scan_bwd(q, k, u, w, aqk, gam, states, do)
    dq, dk, dv, dbg = _dn_prep_bwd(q, k, v, bg, gam, t, du, dw, daqk, dqd, dkd, dgl)
    dpq, dwq = _dn_conv_bwd(proj, conv_w, dq, 0, h, True, "dn_conv_q_bwd")
    dpk, dwk = _dn_conv_bwd(proj, conv_w, dk, h, h, True, "dn_conv_k_bwd")
    dpv, dwv = _dn_conv_bwd(proj, conv_w, dv, 2 * h, h, False, "dn_conv_v_bwd")

    def gates_bwd(x, dbg, al, db):
        _, vjp = jax.vjp(functools.partial(_gates, h=h), x, al, db)
        gx, gal, gdb = vjp(dbg)
        return (gx,), (gal, gdb)

    (dba,), (d_alp, d_dtp) = _rowmap(gates_bwd, [(proj, LANES, 4 * h), dbg], [alp, dtp], [(LANES, MXU_DTYPE)],
                                     [(1, LANES), (1, LANES)], "dn_gates_bwd")
    dproj = jnp.concatenate([dpq, dpk, dpv, dz, dba], axis=1)
    d_w_in = _mm_tn(hin, dproj, MXU_DTYPE, "dn_dwin")
    dh = _mm_nt(dproj, w_in, F32, "dn_dh")
    d_conv_w = jnp.concatenate([dwq, dwk, dwv], axis=1)
    return dh, dict(w_in=d_w_in, w_out=d_w_out, conv_w=d_conv_w, a_log=d_alp[0, h:2 * h], dt_bias=d_dtp[0, h:2 * h],
                    norm_w=d_norm_w[0])


def _ln_silu(u, g, b):
    return jax.nn.silu(_ln(u, g, b))


def _conformer_fwd(hin, get_w_in, dw_w, dw_b, ln_g, ln_b, get_w_out):
    w_in = get_w_in(hin)
    vg = _mm_nn(hin, w_in, F32, "cf_vg")
    u1 = _cf_conv_fwd(vg, dw_w, dw_b)
    ch = u1.shape[1]

    def fn(u, g, b):
        return (_ln_silu(u, g, b),), ()

    (u2,), _ = _rowmap(fn, [u1], [ln_g, ln_b], [(ch, MXU_DTYPE)], [], "cf_ln")
    w_out = get_w_out(u2)
    y = _mm_nn(u2, w_out, F32, "cf_y")
    return y, (hin, vg, u1, u2, w_in, w_out)


def _conformer_bwd(res, dy, dw_w, ln_g, ln_b):
    hin, vg, u1, u2, w_in, w_out = res
    ch = u1.shape[1]
    d_w_out = _mm_tn(u2, dy, MXU_DTYPE, "cf_dwout")
    du2 = _mm_nt(dy, w_out, F32, "cf_du2")

    def fn(u, du2, g, b):
        _, vjp = jax.vjp(_ln_silu, u, g, b)
        gu, gg, gb = vjp(du2)
        return (gu,), (gg, gb)

    (du1,), (d_ln_g, d_ln_b) = _rowmap(fn, [u1, du2], [ln_g, ln_b], [(ch, F32)], [(1, ch), (1, ch)], "cf_ln_bwd")
    dval, dgate, d_dw_w, d_dw_b = _cf_conv_bwd(vg, dw_w, du1)
    dvg = jnp.concatenate([dval, dgate], axis=1)
    d_w_in = _mm_tn(hin, dvg, MXU_DTYPE, "cf_dwin", split_cols=True)
    dh = _mm_nt(dvg, w_in, F32, "cf_dh")
    return dh, dict(w_in=d_w_in, w_out=d_w_out, dw_w=d_dw_w, dw_b=d_dw_b[0], ln_g=d_ln_g[0], ln_b=d_ln_b[0])


def _mlp_fwd(hin, get_w1, get_w2):
    w1 = get_w1(hin)
    r = _mm_nn(hin, w1, MXU_DTYPE, "ff_a", relu2=True)
    w2 = get_w2(r)
    m = _mm_nn(r, w2, F32, "ff_m")
    return m, (hin, r, w1, w2)


def _mlp_bwd(res, dm):
    hin, r, w1, w2 = res
    d_w2 = _mm_tn(r, dm, MXU_DTYPE, "ff_dw2")
    da = _mm_nt(dm, w2, MXU_DTYPE, "ff_da", relu2_sq=r)
    d_w1 = _mm_tn(hin, da, MXU_DTYPE, "ff_dw1", split_cols=True)
    dh = _mm_nt(da, w1, F32, "ff_dh")
    return dh, d_w1, d_w2


def _ada_fwd(c_all, ada_w):
    depth, d, nl = ada_w.shape
    tn = _tile(nl, 256)

    def body(c_ref, w_ref, o_ref, cond_ref):
        cond = jax.nn.silu(c_ref[...]).astype(MXU_DTYPE)
        cond_ref[...] = cond
        o_ref[...] = lax.dot_general(cond, w_ref[...].astype(MXU_DTYPE), (NN, ((), ())), preferred_element_type=F32)

    return _call(body, grid=(depth, nl // tn),
                 ins=[(c_all, c_all.shape, lambda l, j: (0, 0)), (ada_w, (None, d, tn), lambda l, j: (l, 0, j))],
                 outs=[((depth, N_DEV, nl), F32, (None, N_DEV, tn), lambda l, j: (l, 0, j)),
                       (c_all.shape, MXU_DTYPE, c_all.shape, lambda l, j: (0, 0))],
                 name="ada_fwd")


def _ada_bwd(cond_all, dmod_cols):
    depth, _, nl = dmod_cols.shape
    d = cond_all.shape[1]
    tn = _tile(nl, 256)

    def body(c_ref, g_ref, o_ref):
        o_ref[...] = lax.dot_general(c_ref[...], g_ref[...].astype(MXU_DTYPE), (TN, ((), ())),
                                     preferred_element_type=F32)

    return _call(body, grid=(depth, nl // tn),
                 ins=[(cond_all, cond_all.shape, lambda l, j: (0, 0)), (dmod_cols, (None, N_DEV, tn), lambda l, j: (l, 0, j))],
                 outs=[((depth, d, nl), F32, (None, d, tn), lambda l, j: (l, 0, j))], name="ada_bwd")[0]


def _peers():
    x, y, c = lax.axis_index("x"), lax.axis_index("y"), lax.axis_index("c")
    peers = []
    for k in range(1, N_DEV):
        px = 1 - x if k & 4 else x
        py = 1 - y if k & 2 else y
        pc = 1 - c if k & 1 else c
        peers.append(((px, py, pc), 4 * px + 2 * py + pc))
    return 4 * x + 2 * y + c, peers


_HBM = pl.BlockSpec(memory_space=pltpu.HBM)
_SEM = pl.BlockSpec(memory_space=pltpu.SEMAPHORE)
_ANY = pl.BlockSpec(memory_space=pl.ANY)
_EFFECT = pltpu.SideEffectType.DATAFLOW_SIDE_EFFECTING


def _xfer_start(srcs, lands, scatter, after, name):
    nt = len(srcs)

    def body(*refs):
        src, land = refs[:nt], refs[nt:2 * nt]
        sems = refs[2 * nt + 1:4 * nt + 1]
        token = refs[-1]
        me, peers = _peers()
        for t in range(nt):
            for k, (pid, plin) in enumerate(peers):
                pltpu.make_async_remote_copy(
                    src_ref=src[t].at[plin] if scatter else src[t], dst_ref=land[t].at[me],
                    send_sem=sems[2 * t].at[k], recv_sem=sems[2 * t + 1].at[k],
                    device_id=pid, device_id_type=pl.DeviceIdType.MESH).start()
        token[...] = jnp.zeros_like(token)

    out_shape = [pltpu.SemaphoreType.DMA((N_DEV - 1,)) for _ in range(2 * nt)]
    out_shape += [pltpu.HBM(a.shape, a.dtype) for a in lands]
    out_shape += [jax.ShapeDtypeStruct((8, LANES), F32)]
    srcs = [pltpu.with_memory_space_constraint(a, pltpu.HBM) for a in srcs]
    res = pl.pallas_call(
        body, name=name, out_shape=out_shape,
        in_specs=[_HBM] * (2 * nt) + [_ANY],
        out_specs=[_SEM] * (2 * nt) + [_HBM] * nt + [pl.BlockSpec(memory_space=pltpu.VMEM)],
        input_output_aliases={nt + i: 2 * nt + i for i in range(nt)},
        compiler_params=pltpu.CompilerParams(has_side_effects=_EFFECT),
    )(*srcs, *[pltpu.with_memory_space_constraint(a, pltpu.HBM) for a in lands], after)
    sems, thru = res[:2 * nt], res[2 * nt:3 * nt]
    return [(sems[2 * t], sems[2 * t + 1], srcs[t], thru[t]) for t in range(nt)], res[-1]


def _xfer_wait(handle, scatter, after, name):
    send, recv, src, land = handle

    def body(src_ref, land_ref, send_sem, recv_sem, after_ref, land_out):
        _, peers = _peers()
        for k, (pid, plin) in enumerate(peers):
            cp = pltpu.make_async_remote_copy(
                src_ref=src_ref.at[plin] if scatter else src_ref, dst_ref=land_ref.at[plin],
                send_sem=send_sem.at[k], recv_sem=recv_sem.at[k],
                device_id=pid, device_id_type=pl.DeviceIdType.MESH)
            cp.wait_send()
            cp.wait_recv()

    return pl.pallas_call(
        body, name=name, out_shape=pltpu.HBM(land.shape, land.dtype),
        in_specs=(_HBM, _HBM, _SEM, _SEM, _ANY), out_specs=_HBM, input_output_aliases={1: 0},
        compiler_params=pltpu.CompilerParams(has_side_effects=_EFFECT),
    )(src, land, send, recv, after)


def _landing(x, me):
    return lax.dynamic_update_slice(lax.empty((N_DEV,) + x.shape, x.dtype), x[None], (me,) + (0,) * x.ndim)


def _chip_peers():
    x, y, c = lax.axis_index("x"), lax.axis_index("y"), lax.axis_index("c")
    lin = lambda px, py, pc: 4 * px + 2 * py + pc
    sibling = ((x, y, 1 - c), lin(x, y, 1 - c))
    chips = [((1 - x, y, c), lin(1 - x, y, c)), ((x, 1 - y, c), lin(x, 1 - y, c)),
             ((1 - x, 1 - y, c), lin(1 - x, 1 - y, c))]
    return lin(x, y, c), sibling, chips


N_CHIPS_OTHER = 3


def _gather2_start(srcs, lands, after, name):
    nt = len(srcs)

    def body(*refs):
        src, land = refs[:nt], refs[nt:2 * nt]
        sems = refs[2 * nt + 1:5 * nt + 1]
        token = refs[-1]
        me, sibling, chips = _chip_peers()
        for t in range(nt):
            send, recv_ici, recv_sib = sems[3 * t], sems[3 * t + 1], sems[3 * t + 2]
            pltpu.make_async_remote_copy(src_ref=src[t], dst_ref=land[t].at[me], send_sem=send.at[0],
                                         recv_sem=recv_sib.at[0], device_id=sibling[0],
                                         device_id_type=pl.DeviceIdType.MESH).start()
            for j, (pid, _) in enumerate(chips):
                pltpu.make_async_remote_copy(src_ref=src[t], dst_ref=land[t].at[me], send_sem=send.at[1 + j],
                                             recv_sem=recv_ici.at[j], device_id=pid,
                                             device_id_type=pl.DeviceIdType.MESH).start()
        token[...] = jnp.zeros_like(token)

    out_shape = []
    for _ in range(nt):
        out_shape += [pltpu.SemaphoreType.DMA((1 + N_CHIPS_OTHER,)), pltpu.SemaphoreType.DMA((N_CHIPS_OTHER,)),
                      pltpu.SemaphoreType.DMA((1,))]
    out_shape += [pltpu.HBM(a.shape, a.dtype) for a in list(srcs) + list(lands)]
    out_shape += [jax.ShapeDtypeStruct((8, LANES), F32)]
    res = pl.pallas_call(
        body, name=name, out_shape=out_shape,
        in_specs=[_HBM] * (2 * nt) + [_ANY],
        out_specs=[_SEM] * (3 * nt) + [_HBM] * (2 * nt) + [pl.BlockSpec(memory_space=pltpu.VMEM)],
        input_output_aliases={i: 3 * nt + i for i in range(2 * nt)},
        compiler_params=pltpu.CompilerParams(has_side_effects=_EFFECT),
    )(*[pltpu.with_memory_space_constraint(a, pltpu.HBM) for a in list(srcs) + list(lands)], after)
    sems, thru = res[:3 * nt], res[3 * nt:5 * nt]
    return [(sems[3 * t], sems[3 * t + 1], sems[3 * t + 2], thru[t], thru[nt + t]) for t in range(nt)], res[-1]


def _gather2_relay(handles, after, name):
    nt = len(handles)

    def body(*refs):
        src, land = refs[:nt], refs[nt:2 * nt]
        send1, recv_ici = refs[2 * nt:3 * nt], refs[3 * nt:4 * nt]
        outs = refs[4 * nt + 1:]
        send2, recv2 = outs[:nt], outs[nt:2 * nt]
        me, sibling, chips = _chip_peers()
        for t in range(nt):
            pltpu.make_async_remote_copy(src_ref=src[t], dst_ref=land[t].at[me], send_sem=send1[t].at[0],
                                         recv_sem=recv_ici[t].at[0], device_id=sibling[0],
                                         device_id_type=pl.DeviceIdType.MESH).wait_send()
            for j, (pid, plin) in enumerate(chips):
                arrived = pltpu.make_async_remote_copy(src_ref=src[t], dst_ref=land[t].at[plin], send_sem=send1[t].at[1 + j],
                                                       recv_sem=recv_ici[t].at[j], device_id=pid,
                                                       device_id_type=pl.DeviceIdType.MESH)
                arrived.wait_send()
                arrived.wait_recv()
                pltpu.make_async_remote_copy(src_ref=land[t].at[plin], dst_ref=land[t].at[plin], send_sem=send2[t].at[j],
                                             recv_sem=recv2[t].at[j], device_id=sibling[0],
                                             device_id_type=pl.DeviceIdType.MESH).start()

    srcs = [h[3] for h in handles]
    lands = [h[4] for h in handles]
    out_shape = [pltpu.SemaphoreType.DMA((N_CHIPS_OTHER,)) for _ in range(2 * nt)]
    out_shape += [pltpu.HBM(a.shape, a.dtype) for a in srcs + lands]
    res = pl.pallas_call(
        body, name=name, out_shape=out_shape,
        in_specs=[_HBM] * (2 * nt) + [_SEM] * (2 * nt) + [_ANY],
        out_specs=[_SEM] * (2 * nt) + [_HBM] * (2 * nt),
        input_output_aliases={i: 2 * nt + i for i in range(2 * nt)},
        compiler_params=pltpu.CompilerParams(has_side_effects=_EFFECT),
    )(*srcs, *lands, *[h[0] for h in handles], *[h[1] for h in handles], after)
    return [(handles[t][2], res[t], res[nt + t], res[3 * nt + t]) for t in range(nt)]


def _gather2_wait(handle, after, name):
    recv_sib, send2, recv2, land = handle

    def body(land_ref, recv_sib_sem, send2_sem, recv2_sem, after_ref, land_out):
        me, sibling, chips = _chip_peers()
        pltpu.make_async_remote_copy(src_ref=land_ref.at[me], dst_ref=land_ref.at[sibling[1]], send_sem=send2_sem.at[0],
                                     recv_sem=recv_sib_sem.at[0], device_id=sibling[0],
                                     device_id_type=pl.DeviceIdType.MESH).wait_recv()
        for j, (pid, plin) in enumerate(chips):
            relayed = pltpu.make_async_remote_copy(src_ref=land_ref.at[plin], dst_ref=land_ref.at[plin], send_sem=send2_sem.at[j],
                                                   recv_sem=recv2_sem.at[j], device_id=sibling[0],
                                                   device_id_type=pl.DeviceIdType.MESH)
            relayed.wait_send()
            relayed.wait_recv()

    return pl.pallas_call(
        body, name=name, out_shape=pltpu.HBM(land.shape, land.dtype),
        in_specs=(_HBM, _SEM, _SEM, _SEM, _ANY), out_specs=_HBM, input_output_aliases={0: 0},
        compiler_params=pltpu.CompilerParams(has_side_effects=_EFFECT),
    )(land, recv_sib, send2, recv2, after)


def _exchange(arrs, scatter, name):
    nt = len(arrs)
    out_shape = [jax.ShapeDtypeStruct(a.shape if scatter else (N_DEV,) + a.shape, a.dtype) for a in arrs]

    def body(*refs):
        ins, outs = refs[:nt], refs[nt:2 * nt]
        send, recv, loc = refs[2 * nt:]
        me, peers = _peers()
        copies = []
        for t in range(nt):
            own = pltpu.make_async_copy(ins[t].at[me] if scatter else ins[t], outs[t].at[me], loc.at[t])
            own.start()
            copies.append(own)
            for k, (pid, plin) in enumerate(peers):
                cp = pltpu.make_async_remote_copy(
                    src_ref=ins[t].at[plin] if scatter else ins[t], dst_ref=outs[t].at[me],
                    send_sem=send.at[t, k], recv_sem=recv.at[t, k],
                    device_id=pid, device_id_type=pl.DeviceIdType.MESH)
                cp.start()
                copies.append(cp)
        for cp in copies:
            cp.wait()

    any_spec = pl.BlockSpec(memory_space=pl.ANY)
    return pl.pallas_call(
        body, out_shape=out_shape, in_specs=[any_spec] * nt, out_specs=[any_spec] * nt,
        scratch_shapes=[pltpu.SemaphoreType.DMA((nt, N_DEV - 1)), pltpu.SemaphoreType.DMA((nt, N_DEV - 1)),
                        pltpu.SemaphoreType.DMA((nt,))],
        name=name)(*arrs)


def _adamw_body(n_parts, stacked=True):
    def body(p_ref, w_ref, m_ref, v_ref, *rest):
        g_out, d_out, m_out, v_out = rest[-4:]
        part = (lambda i: p_ref[i]) if stacked else (lambda i: p_ref[i][...])
        g = part(0).astype(F32)
        for i in range(1, n_parts):
            g = g + part(i).astype(F32)
        m2 = ADAM_B1 * m_ref[...] + (1.0 - ADAM_B1) * g
        v2 = ADAM_B2 * v_ref[...] + (1.0 - ADAM_B2) * jnp.square(g)
        m_hat = m2 / (1.0 - ADAM_B1 ** ADAM_STEP)
        v_hat = v2 / (1.0 - ADAM_B2 ** ADAM_STEP)
        g_out[...] = g
        d_out[...] = -ADAM_LR * (m_hat / (jnp.sqrt(v_hat) + ADAM_EPS) + ADAM_WD * w_ref[...])
        m_out[...] = m2
        v_out[...] = v2

    return body


def _adamw_layer(own, land, me, w, m, v, layer, prev, name):
    _, r, c = own.shape
    tr = _tile(r, 256, 8)
    blk = pl.BlockSpec((None, tr, c), lambda i, me_ref: (layer, i, 0))
    share = lambda k: pl.BlockSpec((None, tr, c), lambda i, me_ref: (me_ref[0] ^ k, i, 0))
    in_specs = [share(k) for k in range(N_DEV)] + [blk, blk, blk]
    args = [own] + [land] * (N_DEV - 1) + [w, m, v]
    aliases = {}
    if prev is not None:
        in_specs += [_ANY] * 4
        args += list(prev)
        aliases = {1 + N_DEV + 3 + i: i for i in range(4)}

    def body(me_ref, *refs):
        refs = (refs[:N_DEV],) + refs[N_DEV:]
        _adamw_body(N_DEV, stacked=False)(*refs)

    return pl.pallas_call(
        body,
        grid_spec=pltpu.PrefetchScalarGridSpec(num_scalar_prefetch=1, grid=(r // tr,), in_specs=in_specs,
                                               out_specs=[blk] * 4),
        out_shape=[jax.ShapeDtypeStruct(w.shape, F32)] * 4, input_output_aliases=aliases, name=name,
        compiler_params=_cparams(1))(me, *args)


def _adamw(parts, w, m, v, name):
    p, nl, r, c = parts.shape
    tr = _tile(r, 256, 8)
    body = _adamw_body(p)

    blk = (None, tr, c)
    imap = lambda l, i: (l, i, 0)
    out = ((nl, r, c), F32, blk, imap)
    return _call(body, grid=(nl, r // tr),
                 ins=[(parts, (p, None, tr, c), lambda l, i: (0, l, i, 0)), (w, blk, imap), (m, blk, imap), (v, blk, imap)],
                 outs=[out] * 4, name=name)


def _rows(x):
    return x.reshape(-1, LANES)


def _pad_rows(x, mult=8):
    r = x.shape[0]
    extra = (-r) % mult
    return jnp.pad(x, ((0, extra), (0, 0))) if extra else x


def _shard_cols(x, me, groups):
    lead = x.shape[:-1]
    xr = x.reshape(lead + (N_DEV, groups * LANES))
    xs = lax.dynamic_index_in_dim(xr, me, axis=len(lead), keepdims=False)
    return xs.reshape(N_DEV, -1, LANES)


def kernel(x, c, ada_w, ada_b, ln_g, ln_b, dn_w_in, dn_conv_w, dn_a_log, dn_dt_bias, dn_norm_w, dn_w_out, cf_w_in, cf_dw_w, cf_dw_b, cf_ln_g, cf_ln_b, cf_w_out, ff_w1, ff_w2, loss_target, m_ada_w, m_ada_b, m_ln_g, m_ln_b, m_dn_w_in, m_dn_conv_w, m_dn_a_log, m_dn_dt_bias, m_dn_norm_w, m_dn_w_out, m_cf_w_in, m_cf_dw_w, m_cf_dw_b, m_cf_ln_g, m_cf_ln_b, m_cf_w_out, m_ff_w1, m_ff_w2, v_ada_w, v_ada_b, v_ln_g, v_ln_b, v_dn_w_in, v_dn_conv_w, v_dn_a_log, v_dn_dt_bias, v_dn_norm_w, v_dn_w_out, v_cf_w_in, v_cf_dw_w, v_cf_dw_b, v_cf_ln_g, v_cf_ln_b, v_cf_w_out, v_ff_w1, v_ff_w2):
    depth, d, _ = ada_w.shape
    n_a, n_b = dn_w_in.shape[0], cf_w_in.shape[0]
    heads = dn_a_log.shape[1]
    hw = heads * HEAD_DIM
    taps = cf_dw_w.shape[1]
    s = x.shape[1]
    alpha = (2.0 * depth) ** 0.25
    me = 4 * lax.axis_index("x") + 2 * lax.axis_index("y") + lax.axis_index("c")
    me_arr = jnp.reshape(me, (1,)).astype(jnp.int32)
    xs, tgt = x[0], loss_target[0]

    dn_in_cols = dn_w_in.shape[2]
    keys, shards = [], []
    for i in range(depth):
        j = i // 2
        mixer = [("dn_in", dn_w_in), ("dn_out", dn_w_out)] if i % 2 == 0 else [("cf_in", cf_w_in), ("cf_out", cf_w_out)]
        for nm, wt in mixer:
            keys.append((nm, j))
            shards.append(wt[j].astype(MXU_DTYPE))
        keys += [("ff1", i), ("ff2", i)]
        shards += [ff_w1[i].astype(MXU_DTYPE), ff_w2[i].astype(MXU_DTYPE)]

    small_local = [_rows(ln_g), _rows(ln_b), _rows(dn_conv_w), _rows(cf_dw_w), _rows(cf_dw_b), _rows(cf_ln_g),
                   _rows(cf_ln_b), _rows(c)]
    sizes = [a.shape[0] for a in small_local]
    packed = _pad_rows(jnp.concatenate(small_local, axis=0))
    (small_all,) = _exchange([packed], False, "comm_gather_params")
    offs = [0]
    for z in sizes:
        offs.append(offs[-1] + z)

    def small(i):
        return small_all[:, offs[i]:offs[i + 1], :]

    def unshard(piece, lead, groups):
        t = piece.reshape((N_DEV,) + lead + (groups * LANES,))
        t = jnp.moveaxis(t, 0, len(lead))
        return t.reshape(lead + (N_DEV * groups * LANES,))

    ln_g_f = unshard(small(0), (depth, 2), 1)
    ln_b_f = unshard(small(1), (depth, 2), 1)
    conv_w_f = unshard(small(2), (n_a, DN_CONV), 3 * heads // N_DEV)
    dw_w_f = unshard(small(3), (n_b, taps), 1)
    dw_b_f = unshard(small(4), (n_b,), 1)
    cf_ln_g_f = unshard(small(5), (n_b,), 1)
    cf_ln_b_f = unshard(small(6), (n_b,), 1)
    c_all = small(7).reshape(N_DEV, d)

    mod_part, cond_all = _ada_fwd(c_all, ada_w)
    (mod_all,) = _exchange([mod_part], False, "comm_gather_mod")
    mod_mine = lax.dynamic_index_in_dim(mod_all, me, axis=2, keepdims=False)
    mod_mine = jnp.moveaxis(mod_mine, 0, 1).reshape(depth, N_MOD * d)

    handles, token = _gather2_start(shards, [_landing(a, me) for a in shards], mod_all, "gather_weights_start")
    handles = dict(zip(keys, handles))
    groups = [keys[:1], keys[1:4]] + [keys[4 * i:4 * i + 4] for i in range(1, depth)]
    group_of = {k: n for n, grp in enumerate(groups) for k in grp}
    relayed, weights = {}, {}

    def relay(n, after):
        if n < len(groups) and groups[n][0] not in relayed:
            hs = _gather2_relay([handles[k] for k in groups[n]], after, "gather_relay_%d" % n)
            relayed.update(zip(groups[n], hs))

    relay(0, token)

    def gathered(key, after):
        if key not in weights:
            relay(group_of[key], after)
            if key[0] == "ff1":
                relay(key[1] + 2, after)
            weights[key] = _gather2_wait(relayed[key], after, "gather_wait_%s_%d" % key)
        return weights[key]

    def get_dn_in(j):
        def get(after):
            g = gathered(("dn_in", j), after)
            w = jnp.moveaxis(g, 0, 1).reshape(d, N_DEV * dn_in_cols)
            return jnp.pad(w, ((0, 0), (0, 4 * hw + LANES - N_DEV * dn_in_cols)))
        return get

    def get_rows(key):
        return lambda after: gathered(key, after).reshape((-1, d))

    def get_cols(key):
        return lambda after: gathered(key, after)

    def add_bias(a, b):
        return (a + b,), ()

    (mod,), _ = _rowmap(add_bias, [mod_mine, ada_b], [], [(N_MOD * d, F32)], [], "ada_bias", pin=token)
    mod_rows = mod.reshape(depth * N_MOD, 1, d)
    ln_g_rows = ln_g_f.reshape(depth * 2, 1, d)
    ln_b_rows = ln_b_f.reshape(depth * 2, 1, d)

    def mod_row(i, j):
        return (mod_rows, i * N_MOD + j)

    def ln_row(rows, i, j):
        return (rows, i * 2 + j)

    subs = []
    h_cur = _modulate_fwd(xs, mod_row(0, 1), mod_row(0, 0))
    x_cur = xs
    last = None
    for i in range(depth):
        j = i // 2
        if i % 2 == 0:
            y, res = _deltanet_fwd(h_cur, get_dn_in(j), conv_w_f[j], dn_a_log[j], dn_dt_bias[j], dn_norm_w[j],
                                   get_rows(("dn_out", j)))
        else:
            y, res = _conformer_fwd(h_cur, get_cols(("cf_in", j)), dw_w_f[j], dw_b_f[j][None, :], cf_ln_g_f[j][None, :],
                                    cf_ln_b_f[j][None, :], get_rows(("cf_out", j)))
        p1 = (mod_row(i, 2), ln_row(ln_g_rows, i, 0), ln_row(ln_b_rows, i, 0), mod_row(i, 4), mod_row(i, 3))
        x_mid, h_mid = _combine_fwd(alpha, x_cur, y, *p1)
        subs.append((x_cur, y, p1, res))
        m_out, res2 = _mlp_fwd(h_mid, get_cols(("ff1", i)), get_rows(("ff2", i)))
        if i + 1 < depth:
            p2 = (mod_row(i, 5), ln_row(ln_g_rows, i, 1), ln_row(ln_b_rows, i, 1), mod_row(i + 1, 1), mod_row(i + 1, 0))
            x_next, h_next = _combine_fwd(alpha, x_mid, m_out, *p2)
            subs.append((x_mid, m_out, p2, res2))
            x_cur, h_cur = x_next, h_next
        else:
            p2 = (mod_row(i, 5), ln_row(ln_g_rows, i, 1), ln_row(ln_b_rows, i, 1))
            last = (x_mid, m_out, p2, res2)

    x_in, y_in, p_last, res_last = last
    dx, dy, (loss_acc, g_gt, g_g, g_b) = _last_fwd_bwd(alpha, x_in, y_in, tgt, *p_last)
    loss = lax.psum(loss_acc[0, 0], ("x", "y", "c"))

    d_mod = [[None] * N_MOD for _ in range(depth)]
    d_ln_g = [[None, None] for _ in range(depth)]
    d_ln_b = [[None, None] for _ in range(depth)]
    d_mod[depth - 1][5], d_ln_g[depth - 1][1], d_ln_b[depth - 1][1] = g_gt, g_g, g_b
    gw = dict(dn=[None] * n_a, cf=[None] * n_b)

    sent = {}

    def send_grads(named, tag):
        parts = [p for _, p in named]
        hs, tok = _xfer_start(parts, [lax.empty(p.shape, p.dtype) for p in parts], True, parts[0], "scatter_start_" + tag)
        for (key, _), hnd in zip(named, hs):
            sent[key] = hnd
        return tok

    def by_rows(g):
        return g.reshape((N_DEV, g.shape[0] // N_DEV, g.shape[1]))

    def send_mlp(i, d_w1, d_w2):
        return send_grads([(("ff1", i), d_w1), (("ff2", i), by_rows(d_w2))], "ff_%d" % i)

    dh, d_w1, d_w2 = _mlp_bwd(res_last, dy)
    pin = send_mlp(depth - 1, d_w1, d_w2)
    for idx in range(len(subs) - 1, -1, -1):
        x_in, y_in, prm, res = subs[idx]
        i, second = idx // 2, idx % 2
        dx, dy, (g_gt, g_g, g_b, g_sc, g_sh) = _combine_bwd(alpha, x_in, y_in, dx, dh, *prm, pin=pin)
        d_mod[i][5 if second else 2], d_ln_g[i][second], d_ln_b[i][second] = g_gt, g_g, g_b
        nxt_i, nxt_base = (i + 1, 0) if second else (i, 3)
        d_mod[nxt_i][nxt_base + 1], d_mod[nxt_i][nxt_base] = g_sc, g_sh
        j = i // 2
        if second:
            dh, d_w1, d_w2 = _mlp_bwd(res, dy)
            pin = send_mlp(i, d_w1, d_w2)
        elif i % 2 == 0:
            dh, gw["dn"][j] = _deltanet_bwd(res, dy, conv_w_f[j])
            d_in = gw["dn"][j]["w_in"][:, :N_DEV * dn_in_cols].reshape(d, N_DEV, dn_in_cols)
            pin = send_grads([(("dn_in", j), jnp.moveaxis(d_in, 1, 0)), (("dn_out", j), by_rows(gw["dn"][j]["w_out"]))],
                             "dn_%d" % j)
        else:
            dh, gw["cf"][j] = _conformer_bwd(res, dy, dw_w_f[j], cf_ln_g_f[j][None, :], cf_ln_b_f[j][None, :])
            pin = send_grads([(("cf_in", j), gw["cf"][j]["w_in"]), (("cf_out", j), by_rows(gw["cf"][j]["w_out"]))],
                             "cf_%d" % j)
    grad_x, g_sc, g_sh = _modulate_bwd(xs, dx, dh, mod_row(0, 1), mod_row(0, 0), pin=pin)
    d_mod[0][1], d_mod[0][0] = g_sc, g_sh
    d_mod_full = jnp.concatenate([jnp.concatenate(r, axis=1) for r in d_mod], axis=0)

    stacked = {"dn_w_in": ("dn_in", dn_w_in, m_dn_w_in, v_dn_w_in), "dn_w_out": ("dn_out", dn_w_out, m_dn_w_out, v_dn_w_out),
               "cf_w_in": ("cf_in", cf_w_in, m_cf_w_in, v_cf_w_in), "cf_w_out": ("cf_out", cf_w_out, m_cf_w_out, v_cf_w_out),
               "ff_w1": ("ff1", ff_w1, m_ff_w1, v_ff_w1), "ff_w2": ("ff2", ff_w2, m_ff_w2, v_ff_w2)}
    chains = {key: None for key in stacked}

    def update_layer(i):
        mixer = ["dn_w_in", "dn_w_out"] if i % 2 == 0 else ["cf_w_in", "cf_w_out"]
        for key, idx in [("ff_w1", i), ("ff_w2", i)] + [(k, i // 2) for k in mixer]:
            short, w, m, v = stacked[key]
            land = _xfer_wait(sent[(short, idx)], True, sg_token, "scatter_wait_%s_%d" % (short, idx))
            chains[key] = _adamw_layer(sent[(short, idx)][2], land, me_arr, w, m, v, idx, chains[key],
                                       "adamw_%s_%d" % (key, idx))

    def stack_rows(lst):
        return jnp.stack(lst, axis=0)

    gs_ln_g = jnp.stack([jnp.concatenate(r, axis=0) for r in d_ln_g], axis=0)
    gs_ln_b = jnp.stack([jnp.concatenate(r, axis=0) for r in d_ln_b], axis=0)
    gs_conv_w = stack_rows([gw["dn"][j]["conv_w"] for j in range(n_a)])
    gs_dw_w = stack_rows([gw["cf"][j]["dw_w"] for j in range(n_b)])
    gs_dw_b = stack_rows([gw["cf"][j]["dw_b"] for j in range(n_b)])
    gs_cf_ln_g = stack_rows([gw["cf"][j]["ln_g"] for j in range(n_b)])
    gs_cf_ln_b = stack_rows([gw["cf"][j]["ln_b"] for j in range(n_b)])
    gs_a_log = stack_rows([_pad_lanes(gw["dn"][j]["a_log"], 0)[0] for j in range(n_a)])
    gs_dt_bias = stack_rows([_pad_lanes(gw["dn"][j]["dt_bias"], 0)[0] for j in range(n_a)])
    gs_norm_w = stack_rows([gw["dn"][j]["norm_w"] for j in range(n_a)])
    small_grads = [gs_ln_g, gs_ln_b, gs_conv_w, gs_dw_w, gs_dw_b, gs_cf_ln_g, gs_cf_ln_b, gs_a_log, gs_dt_bias,
                   gs_norm_w, d_mod_full]
    sg_rows = [_rows(a) for a in small_grads]
    sg_sizes = [a.shape[0] for a in sg_rows]
    sg_packed = _pad_rows(jnp.concatenate(sg_rows, axis=0))
    (sg_handle,), sg_token = _xfer_start([sg_packed], [_landing(sg_packed, me)], False, grad_x, "gather_small_grads_start")
    for i in range(depth - 1, -1, -1):
        update_layer(i)
    sg_all = _xfer_wait(sg_handle, False, chains["ff_w1"][0], "gather_small_grads_wait")
    sg_offs = [0]
    for z in sg_sizes:
        sg_offs.append(sg_offs[-1] + z)

    def sg(i, shape):
        return sg_all[:, sg_offs[i]:sg_offs[i + 1], :].reshape((N_DEV,) + shape)

    dmod_all = sg(10, (depth, N_MOD * d))
    nl = ada_w.shape[2]
    dmod_cols = lax.dynamic_slice_in_dim(dmod_all, me * nl, nl, axis=2)
    g_ada_w = _ada_bwd(cond_all, jnp.moveaxis(dmod_cols, 0, 1))

    outs = {}

    def run_adamw(key, parts, w, m, v):
        shp = w.shape
        as3 = lambda t: t.reshape((-1,) + shp[-2:]) if t.ndim >= 3 else t.reshape((1,) + shp)
        parts3 = parts.reshape((parts.shape[0],) + as3(w).shape)
        res = _adamw(parts3, as3(w), as3(m), as3(v), "adamw_" + key)
        outs[key] = tuple(r.reshape(shp) for r in res)

    run_adamw("ada_w", g_ada_w[None], ada_w, m_ada_w, v_ada_w)

    cgroups = 3 * heads // N_DEV
    shard_parts = [
        _shard_cols(sg(0, (depth, 2, d)), me, 1), _shard_cols(sg(1, (depth, 2, d)), me, 1),
        _shard_cols(sg(2, (n_a, DN_CONV, 3 * hw)), me, cgroups), _shard_cols(sg(3, (n_b, taps, d)), me, 1),
        _shard_cols(sg(4, (n_b, d)), me, 1), _shard_cols(sg(5, (n_b, d)), me, 1), _shard_cols(sg(6, (n_b, d)), me, 1),
    ]
    repl_parts = [sg(7, (n_a, LANES)), sg(8, (n_a, LANES)), sg(9, (n_a, HEAD_DIM)),
                  sg(10, (depth, N_MOD * d)).reshape(N_DEV, -1, LANES)]
    small_parts = shard_parts + repl_parts
    sp_sizes = [a.shape[1] for a in small_parts]
    parts_packed = jnp.concatenate(small_parts, axis=1)
    extra = (-parts_packed.shape[1]) % 8
    parts_packed = jnp.pad(parts_packed, ((0, 0), (0, extra), (0, 0)))

    def pad_heads(t):
        return jnp.pad(t, ((0, 0), (0, LANES - heads)))

    def pack_state(ln_g_, ln_b_, conv_w_, dw_w_, dw_b_, cln_g_, cln_b_, a_log_, dt_b_, norm_w_, ada_b_):
        rows = [_rows(ln_g_), _rows(ln_b_), _rows(conv_w_), _rows(dw_w_), _rows(dw_b_), _rows(cln_g_), _rows(cln_b_),
                pad_heads(a_log_), pad_heads(dt_b_), norm_w_, _rows(ada_b_)]
        return _pad_rows(jnp.concatenate(rows, axis=0))

    w_s = pack_state(ln_g, ln_b, dn_conv_w, cf_dw_w, cf_dw_b, cf_ln_g, cf_ln_b, dn_a_log, dn_dt_bias, dn_norm_w, ada_b)
    m_s = pack_state(m_ln_g, m_ln_b, m_dn_conv_w, m_cf_dw_w, m_cf_dw_b, m_cf_ln_g, m_cf_ln_b, m_dn_a_log,
                     m_dn_dt_bias, m_dn_norm_w, m_ada_b)
    v_s = pack_state(v_ln_g, v_ln_b, v_dn_conv_w, v_cf_dw_w, v_cf_dw_b, v_cf_ln_g, v_cf_ln_b, v_dn_a_log,
                     v_dn_dt_bias, v_dn_norm_w, v_ada_b)
    res_s = _adamw(parts_packed[:, None], w_s[None], m_s[None], v_s[None], "adamw_small")
    sp_offs = [0]
    for z in sp_sizes:
        sp_offs.append(sp_offs[-1] + z)
    small_keys = ["ln_g", "ln_b", "dn_conv_w", "cf_dw_w", "cf_dw_b", "cf_ln_g", "cf_ln_b", "dn_a_log", "dn_dt_bias",
                  "dn_norm_w", "ada_b"]
    small_shapes = [ln_g.shape, ln_b.shape, dn_conv_w.shape, cf_dw_w.shape, cf_dw_b.shape, cf_ln_g.shape,
                    cf_ln_b.shape, dn_a_log.shape, dn_dt_bias.shape, dn_norm_w.shape, ada_b.shape]
    for n, (key, shp) in enumerate(zip(small_keys, small_shapes)):
        vals = []
        for r in res_s:
            piece = r[0, sp_offs[n]:sp_offs[n + 1], :]
            if key in ("dn_a_log", "dn_dt_bias"):
                piece = piece[:, :heads]
            vals.append(piece.reshape(shp))
        outs[key] = tuple(vals)

    for key in stacked:
        outs[key] = tuple(chains[key])

    order = ["ada_w", "ada_b", "ln_g", "ln_b", "dn_w_in", "dn_conv_w", "dn_a_log", "dn_dt_bias", "dn_norm_w",
             "dn_w_out", "cf_w_in", "cf_dw_w", "cf_dw_b", "cf_ln_g", "cf_ln_b", "cf_w_out", "ff_w1", "ff_w2"]
    result = [loss, grad_x[None]]
    for part in range(4):
        result += [outs[k][part] for k in order]
    return tuple(result)
```

```python
import functools

import jax
import jax.numpy as jnp
from jax import lax
from jax.experimental import pallas as pl
from jax.experimental.pallas import tpu as pltpu

F32 = jnp.float32
MXU_DTYPE = jnp.bfloat16
N_DEV = 8
LANES = 128
HEAD_DIM = 128
CHUNK = 64
DN_CONV = 4
N_MOD = 6
LN_EPS = 1e-5
RMS_EPS = 1e-6
L2_EPS = 1e-6
ADAM_LR = 0.001
ADAM_B1 = 0.9
ADAM_B2 = 0.999
ADAM_EPS = 1e-08
ADAM_WD = 0.01
ADAM_STEP = 10

NN = ((1,), (0,))
NT = ((1,), (1,))
TN = ((0,), (0,))

ROW_TILE = 512
CONV_TILE = 256
SHORT_CONV_TILE = 1024


def _mdot(a, b, dims=NN):
    return lax.dot_general(a.astype(MXU_DTYPE), b.astype(MXU_DTYPE), (dims, ((), ())), preferred_element_type=F32)


def _split3(x):
    hi = x.astype(MXU_DTYPE)
    r1 = x - hi.astype(F32)
    mid = r1.astype(MXU_DTYPE)
    lo = (r1 - mid.astype(F32)).astype(MXU_DTYPE)
    return hi, mid, lo


def _dot01(a, b, dims=NN, mask_first=True):
    d = lambda p, q: lax.dot_general(p, q, (dims, ((), ())), preferred_element_type=F32)
    if mask_first:
        m = a.astype(MXU_DTYPE)
        return sum(d(m, p) for p in _split3(b))
    m = b.astype(MXU_DTYPE)
    return sum(d(p, m) for p in _split3(a))


def _cparams(n):
    return pltpu.CompilerParams(dimension_semantics=("arbitrary",) * n)


def _call(body, *, grid, ins, outs, name, scratch=()):
    res = pl.pallas_call(
        body,
        grid=grid,
        in_specs=[pl.BlockSpec(memory_space=pl.ANY) if b is None else pl.BlockSpec(b, m) for _, b, m in ins],
        out_specs=[pl.BlockSpec(b, m) for _, _, b, m in outs],
        out_shape=[jax.ShapeDtypeStruct(s, d) for s, d, _, _ in outs],
        scratch_shapes=list(scratch),
        name=name,
        compiler_params=_cparams(len(grid)),
    )(*[a for a, _, _ in ins])
    return res


def _tile(n, pref, unit=LANES):
    if n <= pref:
        return n
    t = (pref // unit) * unit
    while t > unit and n % t:
        t -= unit
    assert n % t == 0, (n, pref)
    return t


def _rowmap(fn, rows, consts, row_outs, acc_outs, name, pin=None):
    rows = [r if isinstance(r, tuple) else (r, r.shape[1], 0) for r in rows]
    s = rows[0][0].shape[0]
    tm = min(ROW_TILE, s)
    nr, nc, no, na = len(rows), len(consts), len(row_outs), len(acc_outs)
    npin = 0 if pin is None else 1

    def body(*refs):
        rin, cin = refs[:nr], refs[nr:nr + nc]
        refs = refs[:nr + nc] + refs[nr + nc + npin:]
        rout, aout = refs[nr + nc:nr + nc + no], refs[nr + nc + no:]
        ro, ao = fn(*[r[...] for r in rin], *[c[...] for c in cin])
        for ref, val in zip(rout, ro):
            ref[...] = val.astype(ref.dtype)
        if na:
            first = pl.program_id(0) == 0

            @pl.when(first)
            def _():
                for ref, val in zip(aout, ao):
                    ref[...] = val

            @pl.when(jnp.logical_not(first))
            def _():
                for ref, val in zip(aout, ao):
                    ref[...] += val

    ins = [(a, (tm, w), functools.partial(lambda i, cb: (i, cb), cb=cb)) for a, w, cb in rows]
    for c in consts:
        if isinstance(c, tuple):
            ins.append((c[0], (None, 1, c[0].shape[2]), functools.partial(lambda i, n: (n, 0, 0), n=c[1])))
        else:
            ins.append((c, c.shape, lambda i: (0, 0)))
    if pin is not None:
        ins.append((pin, None, None))
    outs = [((s, w), d, (tm, w), lambda i: (i, 0)) for w, d in row_outs]
    outs += [(shp, F32, shp, lambda i: (0, 0)) for shp in acc_outs]
    res = _call(body, grid=(s // tm,), ins=ins, outs=outs, name=name)
    return res[:no], res[no:]


def _ln(z, g, b):
    mu = jnp.mean(z, -1, keepdims=True)
    var = jnp.mean(jnp.square(z - mu), -1, keepdims=True)
    return (z - mu) * lax.rsqrt(var + LN_EPS) * g + b


def _combine(alpha, x, y, gt, g, b, sc, sh):
    xn = _ln(alpha * x + (1.0 + gt) * y, g, b)
    return xn, xn * (1.0 + sc) + sh


def _modulate_fwd(x, sc, sh):
    def fn(x, sc, sh):
        return ((x * (1.0 + sc) + sh),), ()

    (h,), _ = _rowmap(fn, [x], [sc, sh], [(x.shape[1], MXU_DTYPE)], [], "modulate_fwd")
    return h


def _modulate_bwd(x, dx, dh, sc, sh, pin=None):
    d = x.shape[1]

    def fn(x, dx, dh, sc, sh):
        _, vjp = jax.vjp(lambda x, sc, sh: x * (1.0 + sc) + sh, x, sc, sh)
        gx, gsc, gsh = vjp(dh)
        return (dx + gx,), (gsc, gsh)

    (gx,), (gsc, gsh) = _rowmap(fn, [x, dx, dh], [sc, sh], [(d, F32)], [(1, d), (1, d)], "modulate_bwd", pin=pin)
    return gx, gsc, gsh


def _combine_fwd(alpha, x, y, gt, g, b, sc, sh):
    d = x.shape[1]

    def fn(x, y, gt, g, b, sc, sh):
        return _combine(alpha, x, y, gt, g, b, sc, sh), ()

    (xn, h), _ = _rowmap(fn, [x, y], [gt, g, b, sc, sh], [(d, F32), (d, MXU_DTYPE)], [], "combine_fwd")
    return xn, h


def _combine_bwd(alpha, x, y, dxn, dh, gt, g, b, sc, sh, pin=None):
    d = x.shape[1]

    def fn(x, y, dxn, dh, gt, g, b, sc, sh):
        _, vjp = jax.vjp(functools.partial(_combine, alpha), x, y, gt, g, b, sc, sh)
        gx, gy, ggt, gg, gb, gsc, gsh = vjp((dxn, dh))
        return (gx, gy), (ggt, gg, gb, gsc, gsh)

    (gx, gy), accs = _rowmap(fn, [x, y, dxn, dh], [gt, g, b, sc, sh], [(d, F32), (d, MXU_DTYPE)],
                             [(1, d)] * 5, "combine_bwd", pin=pin)
    return gx, gy, accs


def _last_fwd_bwd(alpha, x, y, tgt, gt, g, b):
    d = x.shape[1]

    def fn(x, y, tgt, gt, g, b):
        xn, vjp = jax.vjp(lambda x, y, gt, g, b: _ln(alpha * x + (1.0 + gt) * y, g, b), x, y, gt, g, b)
        err = xn - tgt
        gx, gy, ggt, gg, gb = vjp(err * (1.0 / d))
        rows = jnp.sum(jnp.square(err), axis=-1, keepdims=True)
        loss = (0.5 / d) * jnp.sum(rows, axis=0, keepdims=True) * jnp.ones((1, LANES), F32)
        return (gx, gy), (loss, ggt, gg, gb)

    (gx, gy), accs = _rowmap(fn, [x, y, tgt], [gt, g, b], [(d, F32), (d, MXU_DTYPE)],
                             [(1, LANES), (1, d), (1, d), (1, d)], "last_fwd_bwd")
    return gx, gy, accs


MM_VMEM_BUDGET = 40 * 2 ** 20


def _fit(options, cost):
    for o in options:
        if 2 * cost(o) <= MM_VMEM_BUDGET:
            return o
    return options[-1]


def _row_tiles(m):
    return [t for t in (2048, 1024, 512, 256) if t <= m and m % t == 0] or [m]


def _mm_call(a, a_blk, a_map, b, b_blk, b_map, outs, dims, grid, name, epi=None, extra=None, split=None, blocks=None):
    nk = grid[2]
    n_out = len(outs)
    n_in = 3 if extra is not None else 2

    def body(*refs):
        a_ref, b_ref = refs[0], refs[1]
        rest = refs[n_in:]
        out_refs = rest[:n_out]

        def finish(val):
            if epi == "relu2":
                out_refs[0][...] = jnp.square(jnp.maximum(val, 0.0)).astype(out_refs[0].dtype)
            elif epi == "relu2_bwd":
                sq = refs[2][...].astype(F32)
                root = jnp.where(sq > 0.0, sq * lax.rsqrt(sq), 0.0)
                out_refs[0][...] = (val * 2.0 * root).astype(out_refs[0].dtype)
            elif split is not None:
                for g in range(split[0]):
                    out_refs[0][g] = val[:, g * split[1]:(g + 1) * split[1]].astype(out_refs[0].dtype)
            else:
                out_refs[0][...] = val.astype(out_refs[0].dtype)

        if blocks is None:
            p = lax.dot_general(a_ref[...], b_ref[...], (dims, ((), ())), preferred_element_type=F32)
        else:
            p = None
            for g in range(blocks[0]):
                part = lax.dot_general(a_ref[:, g * blocks[1]:(g + 1) * blocks[1]], b_ref[g], (dims, ((), ())),
                                       preferred_element_type=F32)
                p = part if p is None else p + part
        if nk == 1:
            finish(p)
        else:
            acc = rest[n_out]
            k = pl.program_id(2)

            @pl.when(k == 0)
            def _():
                acc[...] = p

            @pl.when(k > 0)
            def _():
                acc[...] += p

            @pl.when(k == nk - 1)
            def _():
                finish(acc[...])

    if nk > 1:
        out_blk = tuple(x for x in outs[0][2] if x is not None)
        if split is not None:
            out_blk = (out_blk[1], split[0] * split[1])
        scratch = [pltpu.VMEM(out_blk, F32)]
    else:
        scratch = []
    ins = [(a, a_blk, a_map), (b, b_blk, b_map)] + ([extra] if extra is not None else [])
    return _call(body, grid=grid, ins=ins, outs=outs, name=name, scratch=scratch)


def _isz(dt):
    return jnp.dtype(dt).itemsize


def _mm_nn(a, b, out_dtype, name, relu2=False):
    m, kdim = a.shape
    if b.ndim == 2:
        n = b.shape[1]
        tn = _tile(n, 1536 if n > 2048 else 512)
        b_blk, b_map = (kdim, tn), lambda i, j, k: (0, j)
    else:
        g, _, ng = b.shape
        n = g * ng
        tn = _tile(ng, 512)
        b_blk = (None, kdim, tn)
        b_map = functools.partial(lambda i, j, k, npg: (j // npg, 0, j % npg), npg=ng // tn)
    tm = _fit(_row_tiles(m), lambda t: t * kdim * _isz(a.dtype) + kdim * tn * _isz(b.dtype) + t * tn * _isz(out_dtype))
    grid = (m // tm, n // tn, 1)
    outs = [((m, n), out_dtype, (tm, tn), lambda i, j, k: (i, j))]
    return _mm_call(a, (tm, kdim), lambda i, j, k: (i, 0), b, b_blk, b_map, outs, NN, grid, name,
                    epi="relu2" if relu2 else None)[0]


def _mm_nt(a, b, out_dtype, name, relu2_sq=None):
    m, n = a.shape
    extra_bytes = _isz(relu2_sq.dtype) if relu2_sq is not None else 0
    if b.ndim == 2:
        kout = b.shape[0]
        to = _tile(kout, 512)
        b_blk, b_map, blocks = (to, n), lambda i, j, k: (j, 0), None
    else:
        g, kout, ng = b.shape
        to = _tile(kout, 512)
        b_blk, b_map, blocks = (g, to, ng), lambda i, j, k: (0, j, 0), (g, ng)
    tm = _fit(_row_tiles(m), lambda t: t * n * _isz(a.dtype) + to * n * _isz(b.dtype)
              + t * to * (_isz(out_dtype) + extra_bytes))
    grid = (m // tm, kout // to, 1)
    outs = [((m, kout), out_dtype, (tm, to), lambda i, j, k: (i, j))]
    extra = (relu2_sq, (tm, to), lambda i, j, k: (i, j)) if relu2_sq is not None else None
    return _mm_call(a, (tm, n), lambda i, j, k: (i, 0), b, b_blk, b_map, outs, NT, grid, name,
                    epi="relu2_bwd" if relu2_sq is not None else None, extra=extra, blocks=blocks)[0]


def _mm_tn(a, b, out_dtype, name, split_cols=False):
    m, kdim = a.shape
    n = b.shape[1]
    tk = _tile(kdim, 512)
    tn = _tile(n, 1536)
    if not split_cols:
        out, split = ((kdim, n), out_dtype, (tk, tn), lambda i, j, k: (i, j)), None
    else:
        ng = n // N_DEV
        if tn % ng:
            tn = _tile(ng, 512)
        if tn >= ng:
            gb = tn // ng
            out = ((N_DEV, kdim, ng), out_dtype, (gb, tk, ng), lambda i, j, k: (j, i, 0))
            split = (gb, ng)
        else:
            out = ((N_DEV, kdim, ng), out_dtype, (None, tk, tn),
                   functools.partial(lambda i, j, k, npg: (j // npg, i, j % npg), npg=ng // tn))
            split = None
    grid = (kdim // tk, n // tn, 1)
    return _mm_call(a, (m, tk), lambda i, j, k: (0, i), b, (m, tn), lambda i, j, k: (0, j), [out], TN, grid, name,
                    split=split)[0]


def _shifted(xa, off, rows):
    if off % 8 == 0:
        return xa[off:off + rows]
    return pltpu.roll(xa, xa.shape[0] - off, 0)[:rows]


def _conv_pad(taps):
    return -(-(taps - 1) // 8) * 8


def _conv_tile(xp_ref, w, i, rows, taps):
    pad = _conv_pad(taps)
    r0 = pl.multiple_of(i * rows, rows)
    xa = xp_ref[pl.ds(r0, rows + pad), :]
    views = [_shifted(xa, pad - (taps - 1) + j, rows) for j in range(taps)]
    acc = w[0:1, :] * views[0]
    for j in range(1, taps):
        acc = acc + w[j:j + 1, :] * views[j]
    return r0, acc, views


def _conv_back_tile(yp_ref, w, i, rows, taps):
    pad = _conv_pad(taps)
    r0 = pl.multiple_of(i * rows, rows)
    ya = yp_ref[pl.ds(r0, rows + pad), :]
    acc = w[taps - 1:taps, :] * ya[:rows]
    for j in range(taps - 1):
        acc = acc + w[j:j + 1, :] * _shifted(ya, taps - 1 - j, rows)
    return r0, acc


def _tap_sums(dy, views, taps):
    row = lax.broadcasted_iota(jnp.int32, (taps, LANES), 0)
    acc = jnp.zeros((taps, LANES), F32)
    for j in range(taps):
        acc = acc + jnp.where(row == j, jnp.sum(dy * views[j], axis=0, keepdims=True), 0.0)
    return acc


def _silu_l2(xc, l2):
    a = jax.nn.silu(xc)
    if l2:
        a = a * lax.rsqrt(jnp.sum(a * a, axis=-1, keepdims=True) + L2_EPS)
    return a


def _dn_conv_fwd(proj, conv_w, c0, nblk, l2, name):
    s = proj.shape[0]
    pad = _conv_pad(DN_CONV)
    rows = min(SHORT_CONV_TILE, s)

    def body(x_ref, w_ref, o_ref, xp):
        xp[0:pad, :] = jnp.zeros((pad, LANES), F32)
        xp[pad:, :] = x_ref[...]
        w = w_ref[...]

        def tile(i, c):
            r0, acc, _ = _conv_tile(xp, w, i, rows, DN_CONV)
            o_ref[pl.ds(r0, rows), :] = _silu_l2(acc, l2)
            return c

        lax.fori_loop(0, s // rows, tile, 0)

    return _call(body, grid=(nblk,),
                 ins=[(proj, (s, LANES), lambda c: (0, c0 + c)), (conv_w, (DN_CONV, LANES), lambda c: (0, c0 + c))],
                 outs=[((nblk, s, LANES), F32, (None, s, LANES), lambda c: (c, 0, 0))],
                 name=name, scratch=[pltpu.VMEM((s + pad, LANES), F32)])[0]


def _dn_conv_bwd(proj, conv_w, da, c0, nblk, l2, name):
    s = proj.shape[0]
    pad = _conv_pad(DN_CONV)
    rows = min(SHORT_CONV_TILE, s)

    def body(x_ref, w_ref, da_ref, dx_ref, dw_ref, xp, yp):
        xp[0:pad, :] = jnp.zeros((pad, LANES), F32)
        xp[pad:, :] = x_ref[...]
        yp[s:, :] = jnp.zeros((pad, LANES), F32)
        w = w_ref[...]

        def tile(i, dw):
            r0, acc, views = _conv_tile(xp, w, i, rows, DN_CONV)
            _, vjp = jax.vjp(functools.partial(_silu_l2, l2=l2), acc)
            (dxc,) = vjp(da_ref[pl.ds(r0, rows), :])
            yp[pl.ds(r0, rows), :] = dxc
            return dw + _tap_sums(dxc, views, DN_CONV)

        dw_ref[...] = lax.fori_loop(0, s // rows, tile, jnp.zeros((DN_CONV, LANES), F32))

        def tile2(i, c):
            r0, acc = _conv_back_tile(yp, w, i, rows, DN_CONV)
            dx_ref[pl.ds(r0, rows), :] = acc.astype(dx_ref.dtype)
            return c

        lax.fori_loop(0, s // rows, tile2, 0)

    return _call(body, grid=(nblk,),
                 ins=[(proj, (s, LANES), lambda c: (0, c0 + c)), (conv_w, (DN_CONV, LANES), lambda c: (0, c0 + c)),
                      (da, (None, s, LANES), lambda c: (c, 0, 0))],
                 outs=[((s, nblk * LANES), MXU_DTYPE, (s, LANES), lambda c: (0, c)),
                       ((DN_CONV, nblk * LANES), F32, (DN_CONV, LANES), lambda c: (0, c))],
                 name=name, scratch=[pltpu.VMEM((s + pad, LANES), F32), pltpu.VMEM((s + pad, LANES), F32)])


def _cf_conv_fwd(vg, dw_w, dw_b):
    s, c2 = vg.shape
    ch = c2 // 2
    nblk = ch // LANES
    taps = dw_w.shape[0]
    pad = _conv_pad(taps)
    rows = min(CONV_TILE, s)

    def body(v_ref, g_ref, w_ref, b_ref, o_ref, xp):
        xp[0:pad, :] = jnp.zeros((pad, LANES), F32)
        xp[pad:, :] = v_ref[...] * jax.nn.sigmoid(g_ref[...])
        w = w_ref[...]
        bias = b_ref[...]

        def tile(i, c):
            r0, acc, _ = _conv_tile(xp, w, i, rows, taps)
            o_ref[pl.ds(r0, rows), :] = acc + bias
            return c

        lax.fori_loop(0, s // rows, tile, 0)

    return _call(body, grid=(nblk,),
                 ins=[(vg, (s, LANES), lambda c: (0, c)), (vg, (s, LANES), lambda c: (0, nblk + c)),
                      (dw_w, (taps, LANES), lambda c: (0, c)), (dw_b, (1, LANES), lambda c: (0, c))],
                 outs=[((s, ch), F32, (s, LANES), lambda c: (0, c))],
                 name="cf_conv_fwd", scratch=[pltpu.VMEM((s + pad, LANES), F32)])[0]


def _cf_conv_bwd(vg, dw_w, du):
    s, c2 = vg.shape
    ch = c2 // 2
    nblk = ch // LANES
    taps = dw_w.shape[0]
    pad = _conv_pad(taps)
    rows = min(CONV_TILE, s)

    def body(v_ref, g_ref, w_ref, du_ref, dv_ref, dg_ref, dw_ref, db_ref, xp, yp):
        sig = jax.nn.sigmoid(g_ref[...])
        xp[0:pad, :] = jnp.zeros((pad, LANES), F32)
        xp[pad:, :] = v_ref[...] * sig
        yp[0:s, :] = du_ref[...]
        yp[s:, :] = jnp.zeros((pad, LANES), F32)
        w = w_ref[...]
        db_ref[...] = jnp.sum(du_ref[...], axis=0, keepdims=True)

        def tile(i, dw):
            r0, _, views = _conv_tile(xp, w, i, rows, taps)
            return dw + _tap_sums(du_ref[pl.ds(r0, rows), :], views, taps)

        dw_ref[...] = lax.fori_loop(0, s // rows, tile, jnp.zeros((taps, LANES), F32))

        def tile2(i, c):
            r0, du0 = _conv_back_tile(yp, w, i, rows, taps)
            val = v_ref[pl.ds(r0, rows), :]
            sg = jax.nn.sigmoid(g_ref[pl.ds(r0, rows), :])
            dv_ref[pl.ds(r0, rows), :] = (du0 * sg).astype(dv_ref.dtype)
            dg_ref[pl.ds(r0, rows), :] = (du0 * val * sg * (1.0 - sg)).astype(dg_ref.dtype)
            return c

        lax.fori_loop(0, s // rows, tile2, 0)

    return _call(body, grid=(nblk,),
                 ins=[(vg, (s, LANES), lambda c: (0, c)), (vg, (s, LANES), lambda c: (0, nblk + c)),
                      (dw_w, (taps, LANES), lambda c: (0, c)), (du, (s, LANES), lambda c: (0, c))],
                 outs=[((s, ch), MXU_DTYPE, (s, LANES), lambda c: (0, c)),
                       ((s, ch), MXU_DTYPE, (s, LANES), lambda c: (0, c)),
                       ((taps, ch), F32, (taps, LANES), lambda c: (0, c)),
                       ((1, ch), F32, (1, LANES), lambda c: (0, c))],
                 name="cf_conv_bwd", scratch=[pltpu.VMEM((s + pad, LANES), F32), pltpu.VMEM((s + pad, LANES), F32)])


def _masks():
    r = lax.broadcasted_iota(jnp.int32, (CHUNK, CHUNK), 0)
    c = lax.broadcasted_iota(jnp.int32, (CHUNK, CHUNK), 1)
    return r >= c, r > c, r <= c


def _chunk_decay(g):
    causal, _, upper = _masks()
    gb = jnp.broadcast_to(g, (CHUNK, CHUNK))
    gam_r = _dot01(jnp.where(causal, 1.0, 0.0), gb)
    gam_s = _dot01(jnp.ones((CHUNK, CHUNK), F32), jnp.where(upper, gb, 0.0))
    dm = jnp.where(causal, jnp.exp(jnp.where(causal, gam_r - gam_s, 0.0)), 0.0)
    return gam_r[:, 0:1], dm


def _chunk_scores(q, k, beta, dm):
    _, strict, _ = _masks()
    both = _mdot(jnp.concatenate([k * beta, q * (HEAD_DIM ** -0.5)], axis=0), k, NT)
    return jnp.where(strict, both[:CHUNK] * dm, 0.0), both[CHUNK:] * dm


def _lockstep(gens):
    results = [None] * len(gens)
    alive = list(range(len(gens)))
    while alive:
        for i in list(alive):
            try:
                next(gens[i])
            except StopIteration as stop:
                results[i] = stop.value
                alive.remove(i)
    return results


def _chunk_prep_bwd(q, k, v, beta, gam, t, du, dw, daqk, dqd, dkd, dgl):
    causal, strict, _ = _masks()
    r = lax.broadcasted_iota(jnp.int32, (CHUNK, CHUNK), 0)
    c = lax.broadcasted_iota(jnp.int32, (CHUNK, CHUNK), 1)
    scale = HEAD_DIM ** -0.5
    eg = jnp.exp(gam)
    gam_last = gam[CHUNK - 1:CHUNK, :]
    rr = jnp.exp(gam_last - gam)
    kb = k * beta
    qs = q * scale
    vb = v * beta
    kbe = kb * eg
    gam_b = jnp.broadcast_to(gam, (CHUNK, CHUNK))
    gam_s = _dot01(jnp.ones((CHUNK, CHUNK), F32), jnp.where(r == c, gam_b, 0.0))
    both = _mdot(jnp.concatenate([kb, qs], axis=0), k, NT)
    duw = jnp.concatenate([du, dw], axis=1)
    dt = _mdot(duw, jnp.concatenate([vb, kbe], axis=1), NT)
    dvk = _mdot(t, duw, TN)
    yield
    dm = jnp.where(causal, jnp.exp(jnp.where(causal, gam_b - gam_s, 0.0)), 0.0)
    a = jnp.where(strict, both[:CHUNK] * dm, 0.0)
    aqk = both[CHUNK:] * dm
    dvb, dkbe = dvk[:, :HEAD_DIM], dvk[:, HEAD_DIM:]
    x = _mdot(t, dt, TN)
    yield
    da = jnp.where(strict, -_mdot(x, t, NT), 0.0)
    yield
    dkk = da * dm
    dqk = daqk * dm
    ddiff = da * a + daqk * aqk
    dboth = jnp.concatenate([dkk, dqk], axis=0)
    dkq = _mdot(dboth, k)
    dk_mm = _mdot(dboth, jnp.concatenate([kb, qs], axis=0), TN)
    colsum = _dot01(ddiff, jnp.ones((CHUNK, LANES), F32), TN, mask_first=False)[:, 0:1]
    yield
    dkb = dkq[:CHUNK] + dkbe * eg
    dk = dk_mm + dkb * beta + dkd * rr
    dq = (dkq[CHUNK:] + dqd * eg) * scale
    dbeta = jnp.sum(dkb * k, axis=-1, keepdims=True) + jnp.sum(dvb * v, axis=-1, keepdims=True)
    dv = dvb * beta
    deg = jnp.sum(dkbe * kb, axis=-1, keepdims=True) + jnp.sum(dqd * qs, axis=-1, keepdims=True)
    drr = jnp.sum(dkd * k, axis=-1, keepdims=True)
    dgam = deg * eg - drr * rr + jnp.sum(ddiff, axis=-1, keepdims=True) - colsum
    dgam_last = jnp.sum(drr * rr, axis=0, keepdims=True) + dgl[0:1, :] * jnp.exp(gam_last)
    row = lax.broadcasted_iota(jnp.int32, (CHUNK, 1), 0)
    dgam = dgam + jnp.where(row == CHUNK - 1, dgam_last, 0.0)
    dg = _dot01(jnp.where(causal, 1.0, 0.0), jnp.broadcast_to(dgam, (CHUNK, LANES)), TN)[:, 0:1]
    return dq, dk, dv, dbeta, dg


def _prep_group(s):
    nch = s // CHUNK
    return next(c for c in (16, 8, 4, 2, 1) if nch % c == 0)


def _tri_solve_lanes(a_l):
    n = a_l.shape[1]
    group = 8

    def body(a_ref, t_ref):
        t_ref[...] = jnp.zeros_like(t_ref)
        col = lax.broadcasted_iota(jnp.int32, (CHUNK, n), 0)

        def row(r, carry):
            r0 = pl.multiple_of(r * CHUNK, CHUNK)

            def inner(sg, acc):
                a8 = a_ref[pl.ds(r0 + pl.multiple_of(sg * group, group), group), :]
                for j in range(group):
                    t0 = pl.multiple_of((sg * group + j) * CHUNK, CHUNK)
                    acc = acc + a8[j:j + 1, :] * t_ref[pl.ds(t0, CHUNK), :]
                return acc

            acc = lax.fori_loop(0, (r + group - 1) // group, inner, jnp.zeros((CHUNK, n), F32))
            t_ref[pl.ds(r0, CHUNK), :] = jnp.where(col == r, 1.0, 0.0) - acc
            return carry

        lax.fori_loop(0, CHUNK, row, 0)

    return pl.pallas_call(body, out_shape=jax.ShapeDtypeStruct(a_l.shape, F32), name="dn_tri_solve")(a_l)


def _head_cols(bg, hh, heads):
    lane = lax.broadcasted_iota(jnp.int32, bg.shape, 1)
    beta = jnp.sum(jnp.where(lane == hh, bg, 0.0), axis=-1, keepdims=True)
    g = jnp.sum(jnp.where(lane == heads + hh, bg, 0.0), axis=-1, keepdims=True)
    return beta, g


def _dn_prep(q, k, v, bg):
    h, s, _ = q.shape
    cb = _prep_group(s)
    rb = cb * CHUNK
    big = lambda x: (x, (None, rb, HEAD_DIM), lambda n, hh: (hh, n, 0))
    sq = lambda x: (x, (None, rb, CHUNK), lambda n, hh: (hh, n, 0))
    col = lambda x: (x, (None, rb, 1), lambda n, hh: (hh, n, 0))
    tok = (bg, (rb, LANES), lambda n, hh: (n, 0))
    o_big = ((h, s, HEAD_DIM), F32, (None, rb, HEAD_DIM), lambda n, hh: (hh, n, 0))
    o_sq = ((h, s, CHUNK), F32, (None, rb, CHUNK), lambda n, hh: (hh, n, 0))
    o_col = ((h, s, 1), F32, (None, rb, 1), lambda n, hh: (hh, n, 0))

    def scores(q_ref, k_ref, bg_ref, a_ref, aqk_ref, gam_ref):
        beta, g = _head_cols(bg_ref[...], pl.program_id(1), h)
        for i in range(cb):
            sl = slice(i * CHUNK, (i + 1) * CHUNK)
            gam, dm = _chunk_decay(g[sl])
            a_ref[sl, :], aqk_ref[sl, :] = _chunk_scores(q_ref[sl, :], k_ref[sl, :], beta[sl], dm)
            gam_ref[sl, :] = gam

    a, aqk, gam = _call(scores, grid=(s // rb, h), ins=[big(q), big(k), tok], outs=[o_sq, o_sq, o_col],
                        name="dn_scores")
    n_prob = h * (s // CHUNK)
    t_l = _tri_solve_lanes(jnp.transpose(a.reshape(n_prob, CHUNK * CHUNK)))
    t = jnp.transpose(t_l).reshape(h, s, CHUNK)

    def wy(k_ref, v_ref, bg_ref, gam_ref, t_ref, u_ref, w_ref):
        beta, _ = _head_cols(bg_ref[...], pl.program_id(1), h)
        for i in range(cb):
            sl = slice(i * CHUNK, (i + 1) * CHUNK)
            kb = k_ref[sl, :] * beta[sl]
            rhs = jnp.concatenate([v_ref[sl, :] * beta[sl], kb * jnp.exp(gam_ref[sl, :])], axis=1)
            uw = _mdot(t_ref[sl, :], rhs)
            u_ref[sl, :] = uw[:, :HEAD_DIM]
            w_ref[sl, :] = uw[:, HEAD_DIM:]

    u, w = _call(wy, grid=(s // rb, h), ins=[big(k), big(v), tok, col(gam), sq(t)], outs=[o_big, o_big],
                 name="dn_wy")
    return u, w, aqk, t, gam


def _dn_prep_bwd(q, k, v, bg, gam, t, du, dw, daqk, dqd, dkd, dgl):
    h, s, _ = q.shape
    cb = _prep_group(s)
    rb = cb * CHUNK

    def body(q_ref, k_ref, v_ref, bg_ref, g_ref, t_ref, du_ref, dw_ref, da_ref, dqd_ref, dkd_ref, dgl_ref,
             dq_ref, dk_ref, dv_ref, dbg_ref):
        hh = pl.program_id(1)
        beta, _ = _head_cols(bg_ref[...], hh, h)
        slices = [slice(i * CHUNK, (i + 1) * CHUNK) for i in range(cb)]
        results = _lockstep([_chunk_prep_bwd(
            q_ref[sl, :], k_ref[sl, :], v_ref[sl, :], beta[sl], g_ref[sl, :], t_ref[sl, :],
            du_ref[sl, :], dw_ref[sl, :], da_ref[sl, :], dqd_ref[sl, :], dkd_ref[sl, :], dgl_ref[sl, :])
            for sl in slices])

        @pl.when(hh == 0)
        def _():
            dbg_ref[...] = jnp.zeros_like(dbg_ref)

        lane = lax.broadcasted_iota(jnp.int32, (CHUNK, LANES), 1)
        for sl, (dq, dk, dv, dbeta, dg) in zip(slices, results):
            dq_ref[sl, :] = dq
            dk_ref[sl, :] = dk
            dv_ref[sl, :] = dv
            dbg_ref[sl, :] += jnp.where(lane == hh, dbeta, 0.0) + jnp.where(lane == h + hh, dg, 0.0)

    big = lambda x: (x, (None, rb, HEAD_DIM), lambda n, hh: (hh, n, 0))
    sq = lambda x: (x, (None, rb, CHUNK), lambda n, hh: (hh, n, 0))
    col = lambda x: (x, (None, rb, 1), lambda n, hh: (hh, n, 0))
    tok = (bg, (rb, LANES), lambda n, hh: (n, 0))
    o_big = ((h, s, HEAD_DIM), F32, (None, rb, HEAD_DIM), lambda n, hh: (hh, n, 0))
    return _call(body, grid=(s // rb, h),
                 ins=[big(q), big(k), big(v), tok, col(gam), sq(t), big(du), big(dw), sq(daqk), big(dqd), big(dkd),
                      col(dgl)],
                 outs=[o_big, o_big, o_big, ((s, LANES), F32, (rb, LANES), lambda n, hh: (n, 0))], name="dn_prep_bwd")


def _chunk_scaled(q, k, gam):
    gam_last = gam[CHUNK - 1:CHUNK, :]
    q_dec = q * (HEAD_DIM ** -0.5) * jnp.exp(gam)
    k_dec = k * jnp.exp(gam_last - gam)
    return q_dec, k_dec, jnp.exp(gam_last)


def _scan_group(s):
    return 2 if (s // CHUNK) % 2 == 0 else 1


def _dn_scan(q, k, u, w, aqk, gam):
    h, s, _ = q.shape
    nch = s // CHUNK
    sg = _scan_group(s)
    rb = sg * CHUNK

    def body(q_ref, k_ref, u_ref, w_ref, a_ref, gam_ref, o_ref, st_ref, state):
        @pl.when(pl.program_id(0) == 0)
        def _():
            state[...] = jnp.zeros_like(state)

        def head(hh, c):
            sl = slice(c * CHUNK, (c + 1) * CHUNK)
            s0 = state[hh]
            st_ref[c, hh] = s0
            q_dec, k_dec, gl = _chunk_scaled(q_ref[hh, sl, :], k_ref[hh, sl, :], gam_ref[hh, sl, :])
            both = _mdot(jnp.concatenate([w_ref[hh, sl, :], q_dec], axis=0), s0)
            yield
            v_new = u_ref[hh, sl, :] - both[:CHUNK]
            o_ref[sl, hh * HEAD_DIM:(hh + 1) * HEAD_DIM] = both[CHUNK:] + _mdot(a_ref[hh, sl, :], v_new)
            state[hh] = s0 * gl + _mdot(k_dec, v_new, TN)

        for c in range(sg):
            _lockstep([head(hh, c) for hh in range(h)])

    big = lambda x: (x, (h, rb, HEAD_DIM), lambda n: (0, n, 0))
    return _call(body, grid=(nch // sg,),
                 ins=[big(q), big(k), big(u), big(w), (aqk, (h, rb, CHUNK), lambda n: (0, n, 0)),
                      (gam, (h, rb, 1), lambda n: (0, n, 0))],
                 outs=[((s, h * HEAD_DIM), F32, (rb, h * HEAD_DIM), lambda n: (n, 0)),
                       ((nch, h, HEAD_DIM, HEAD_DIM), F32, (sg, h, HEAD_DIM, HEAD_DIM), lambda n: (n, 0, 0, 0))],
                 name="dn_scan", scratch=[pltpu.VMEM((h, HEAD_DIM, HEAD_DIM), F32)])


def _dn_scan_bwd(q, k, u, w, aqk, gam, states, do):
    h, s, _ = q.shape
    nch = s // CHUNK
    sg = _scan_group(s)
    rb = sg * CHUNK
    ngr = nch // sg

    def body(q_ref, k_ref, u_ref, w_ref, a_ref, gam_ref, st_ref, do_ref,
             du_ref, dw_ref, da_ref, dqd_ref, dkd_ref, dgl_ref, dstate):
        @pl.when(pl.program_id(0) == 0)
        def _():
            dstate[...] = jnp.zeros_like(dstate)

        def head(hh, c):
            sl = slice(c * CHUNK, (c + 1) * CHUNK)
            s0 = st_ref[c, hh]
            ds = dstate[hh]
            doh = do_ref[sl, hh * HEAD_DIM:(hh + 1) * HEAD_DIM]
            wv = w_ref[hh, sl, :]
            q_dec, k_dec, gl = _chunk_scaled(q_ref[hh, sl, :], k_ref[hh, sl, :], gam_ref[hh, sl, :])
            ws = _mdot(wv, s0)
            dv_new = _mdot(a_ref[hh, sl, :], doh, TN) + _mdot(k_dec, ds)
            dqd_ref[hh, sl, :] = _mdot(doh, s0, NT)
            qdo = _mdot(q_dec, doh, TN)
            tot = jnp.sum(jnp.sum(s0 * ds, axis=-1, keepdims=True), axis=0, keepdims=True)
            dgl_ref[hh, sl, :] = jnp.broadcast_to(tot, (CHUNK, 1))
            yield
            v_new = u_ref[hh, sl, :] - ws
            du_ref[hh, sl, :] = dv_new
            dw_ref[hh, sl, :] = -_mdot(dv_new, s0, NT)
            da_ref[hh, sl, :] = _mdot(doh, v_new, NT)
            dkd_ref[hh, sl, :] = _mdot(v_new, ds, NT)
            dstate[hh] = ds * gl + qdo - _mdot(wv, dv_new, TN)

        for c in range(sg - 1, -1, -1):
            _lockstep([head(hh, c) for hh in range(h)])

    rev = lambda n: (0, ngr - 1 - n, 0)
    big = lambda x: (x, (h, rb, HEAD_DIM), rev)
    o_big = ((h, s, HEAD_DIM), F32, (h, rb, HEAD_DIM), rev)
    return _call(body, grid=(ngr,),
                 ins=[big(q), big(k), big(u), big(w), (aqk, (h, rb, CHUNK), rev), (gam, (h, rb, 1), rev),
                      (states, (sg, h, HEAD_DIM, HEAD_DIM), lambda n: (ngr - 1 - n, 0, 0, 0)),
                      (do, (rb, h * HEAD_DIM), lambda n: (ngr - 1 - n, 0))],
                 outs=[o_big, o_big, ((h, s, CHUNK), F32, (h, rb, CHUNK), rev), o_big, o_big,
                       ((h, s, 1), F32, (h, rb, 1), rev)],
                 name="dn_scan_bwd", scratch=[pltpu.VMEM((h, HEAD_DIM, HEAD_DIM), F32)])


def _gates(x, a_log, dt_b, h):
    lane = lax.broadcasted_iota(jnp.int32, x.shape, 1)
    return jnp.where(lane < h, jax.nn.sigmoid(x), -jnp.exp(a_log) * jax.nn.softplus(x + dt_b))


def _head_out(oh, zh, nw):
    on = oh * lax.rsqrt(jnp.mean(oh * oh, axis=-1, keepdims=True) + RMS_EPS) * nw
    return on * jax.nn.silu(zh)


def _pad_lanes(x, lo):
    return jnp.zeros((1, LANES), F32).at[0, lo:lo + x.shape[0]].set(x)


def _deltanet_fwd(hin, get_w_in, conv_w, a_log, dt_bias, norm_w, get_w_out):
    h = a_log.shape[0]
    hw = h * HEAD_DIM
    w_in = get_w_in(hin)
    proj = _mm_nn(hin, w_in, F32, "dn_proj")
    q = _dn_conv_fwd(proj, conv_w, 0, h, True, "dn_conv_q")
    k = _dn_conv_fwd(proj, conv_w, h, h, True, "dn_conv_k")
    v = _dn_conv_fwd(proj, conv_w, 2 * h, h, False, "dn_conv_v")
    alp, dtp = _pad_lanes(a_log, h), _pad_lanes(dt_bias, h)

    def gates_fn(x, al, db):
        return (_gates(x, al, db, h),), ()

    (bg,), _ = _rowmap(gates_fn, [(proj, LANES, 4 * h)], [alp, dtp], [(LANES, F32)], [], "dn_gates")
    u, w, aqk, t, gam = _dn_prep(q, k, v, bg)
    o, states = _dn_scan(q, k, u, w, aqk, gam)
    nw = norm_w[None, :]

    def out_fn(o, z, nw):
        parts = [_head_out(o[:, i * HEAD_DIM:(i + 1) * HEAD_DIM], z[:, i * HEAD_DIM:(i + 1) * HEAD_DIM], nw)
                 for i in range(h)]
        return (jnp.concatenate(parts, axis=-1),), ()

    (og,), _ = _rowmap(out_fn, [o, (proj, hw, 3)], [nw], [(hw, MXU_DTYPE)], [], "dn_out")
    w_out = get_w_out(og)
    y = _mm_nn(og, w_out, F32, "dn_y")
    return y, (hin, proj, q, k, v, bg, u, w, aqk, t, gam, states, o, og, alp, dtp, nw, w_in, w_out)


def _deltanet_bwd(res, dy, conv_w):
    hin, proj, q, k, v, bg, u, w, aqk, t, gam, states, o, og, alp, dtp, nw, w_in, w_out = res
    h = q.shape[0]
    hw = h * HEAD_DIM
    s = hin.shape[0]
    d_w_out = _mm_tn(og, dy, MXU_DTYPE, "dn_dwout")
    dog = _mm_nt(dy, w_out, F32, "dn_dog")

    def out_bwd(o, z, dog, nw):
        dos, dzs = [], []
        dn = jnp.zeros((1, HEAD_DIM), F32)
        for i in range(h):
            sl = slice(i * HEAD_DIM, (i + 1) * HEAD_DIM)
            _, vjp = jax.vjp(_head_out, o[:, sl], z[:, sl], nw)
            a, b, c = vjp(dog[:, sl])
            dos.append(a)
            dzs.append(b)
            dn = dn + c
        return (jnp.concatenate(dos, axis=-1), jnp.concatenate(dzs, axis=-1)), (dn,)

    (do, dz), (d_norm_w,) = _rowmap(out_bwd, [o, (proj, hw, 3), dog], [nw], [(hw, F32), (hw, MXU_DTYPE)],
                                    [(1, HEAD_DIM)], "dn_out_bwd")
    du, dw, daqk, dqd, dkd, dgl = _dn_scan_bwd(q, k, u, w, aqk, gam, states, do)
    dq, dk, dv, dbg = _dn_prep_bwd(q, k, v, bg, gam, t, du, dw, daqk, dqd, dkd, dgl)
    dpq, dwq = _dn_conv_bwd(proj, conv_w, dq, 0, h, True, "dn_conv_q_bwd")
    dpk, dwk = _dn_conv_bwd(proj, conv_w, dk, h, h, True, "dn_conv_k_bwd")
    dpv, dwv = _dn_conv_bwd(proj, conv_w, dv, 2 * h, h, False, "dn_conv_v_bwd")

    def gates_bwd(x, dbg, al, db):
        _, vjp = jax.vjp(functools.partial(_gates, h=h), x, al, db)
        gx, gal, gdb = vjp(dbg)
        return (gx,), (gal, gdb)

    (dba,), (d_alp, d_dtp) = _rowmap(gates_bwd, [(proj, LANES, 4 * h), dbg], [alp, dtp], [(LANES, MXU_DTYPE)],
                                     [(1, LANES), (1, LANES)], "dn_gates_bwd")
    dproj = jnp.concatenate([dpq, dpk, dpv, dz, dba], axis=1)
    d_w_in = _mm_tn(hin, dproj, MXU_DTYPE, "dn_dwin")
    dh = _mm_nt(dproj, w_in, F32, "dn_dh")
    d_conv_w = jnp.concatenate([dwq, dwk, dwv], axis=1)
    return dh, dict(w_in=d_w_in, w_out=d_w_out, conv_w=d_conv_w, a_log=d_alp[0, h:2 * h], dt_bias=d_dtp[0, h:2 * h],
                    norm_w=d_norm_w[0])


def _ln_silu(u, g, b):
    return jax.nn.silu(_ln(u, g, b))


def _conformer_fwd(hin, get_w_in, dw_w, dw_b, ln_g, ln_b, get_w_out):
    w_in = get_w_in(hin)
    vg = _mm_nn(hin, w_in, F32, "cf_vg")
    u1 = _cf_conv_fwd(vg, dw_w, dw_b)
    ch = u1.shape[1]

    def fn(u, g, b):
        return (_ln_silu(u, g, b),), ()

    (u2,), _ = _rowmap(fn, [u1], [ln_g, ln_b], [(ch, MXU_DTYPE)], [], "cf_ln")
    w_out = get_w_out(u2)
    y = _mm_nn(u2, w_out, F32, "cf_y")
    return y, (hin, vg, u1, u2, w_in, w_out)


def _conformer_bwd(res, dy, dw_w, ln_g, ln_b):
    hin, vg, u1, u2, w_in, w_out = res
    ch = u1.shape[1]
    d_w_out = _mm_tn(u2, dy, MXU_DTYPE, "cf_dwout")
    du2 = _mm_nt(dy, w_out, F32, "cf_du2")

    def fn(u, du2, g, b):
        _, vjp = jax.vjp(_ln_silu, u, g, b)
        gu, gg, gb = vjp(du2)
        return (gu,), (gg, gb)

    (du1,), (d_ln_g, d_ln_b) = _rowmap(fn, [u1, du2], [ln_g, ln_b], [(ch, F32)], [(1, ch), (1, ch)], "cf_ln_bwd")
    dval, dgate, d_dw_w, d_dw_b = _cf_conv_bwd(vg, dw_w, du1)
    dvg = jnp.concatenate([dval, dgate], axis=1)
    d_w_in = _mm_tn(hin, dvg, MXU_DTYPE, "cf_dwin", split_cols=True)
    dh = _mm_nt(dvg, w_in, F32, "cf_dh")
    return dh, dict(w_in=d_w_in, w_out=d_w_out, dw_w=d_dw_w, dw_b=d_dw_b[0], ln_g=d_ln_g[0], ln_b=d_ln_b[0])


def _mlp_fwd(hin, get_w1, get_w2):
    w1 = get_w1(hin)
    r = _mm_nn(hin, w1, MXU_DTYPE, "ff_a", relu2=True)
    w2 = get_w2(r)
    m = _mm_nn(r, w2, F32, "ff_m")
    return m, (hin, r, w1, w2)


def _mlp_bwd(res, dm):
    hin, r, w1, w2 = res
    d_w2 = _mm_tn(r, dm, MXU_DTYPE, "ff_dw2")
    da = _mm_nt(dm, w2, MXU_DTYPE, "ff_da", relu2_sq=r)
    d_w1 = _mm_tn(hin, da, MXU_DTYPE, "ff_dw1", split_cols=True)
    dh = _mm_nt(da, w1, F32, "ff_dh")
    return dh, d_w1, d_w2


def _ada_fwd(c_all, ada_w):
    depth, d, nl = ada_w.shape
    tn = _tile(nl, 256)

    def body(c_ref, w_ref, o_ref, cond_ref):
        cond = jax.nn.silu(c_ref[...]).astype(MXU_DTYPE)
        cond_ref[...] = cond
        o_ref[...] = lax.dot_general(cond, w_ref[...].astype(MXU_DTYPE), (NN, ((), ())), preferred_element_type=F32)

    return _call(body, grid=(depth, nl // tn),
                 ins=[(c_all, c_all.shape, lambda l, j: (0, 0)), (ada_w, (None, d, tn), lambda l, j: (l, 0, j))],
                 outs=[((depth, N_DEV, nl), F32, (None, N_DEV, tn), lambda l, j: (l, 0, j)),
                       (c_all.shape, MXU_DTYPE, c_all.shape, lambda l, j: (0, 0))],
                 name="ada_fwd")


def _ada_bwd(cond_all, dmod_cols):
    depth, _, nl = dmod_cols.shape
    d = cond_all.shape[1]
    tn = _tile(nl, 256)

    def body(c_ref, g_ref, o_ref):
        o_ref[...] = lax.dot_general(c_ref[...], g_ref[...].astype(MXU_DTYPE), (TN, ((), ())),
                                     preferred_element_type=F32)

    return _call(body, grid=(depth, nl // tn),
                 ins=[(cond_all, cond_all.shape, lambda l, j: (0, 0)), (dmod_cols, (None, N_DEV, tn), lambda l, j: (l, 0, j))],
                 outs=[((depth, d, nl), F32, (None, d, tn), lambda l, j: (l, 0, j))], name="ada_bwd")[0]


def _peers():
    x, y, c = lax.axis_index("x"), lax.axis_index("y"), lax.axis_index("c")
    peers = []
    for k in range(1, N_DEV):
        px = 1 - x if k & 4 else x
        py = 1 - y if k & 2 else y
        pc = 1 - c if k & 1 else c
        peers.append(((px, py, pc), 4 * px + 2 * py + pc))
    return 4 * x + 2 * y + c, peers


_HBM = pl.BlockSpec(memory_space=pltpu.HBM)
_SEM = pl.BlockSpec(memory_space=pltpu.SEMAPHORE)
_ANY = pl.BlockSpec(memory_space=pl.ANY)
_EFFECT = pltpu.SideEffectType.DATAFLOW_SIDE_EFFECTING


def _xfer_start(srcs, lands, scatter, after, name):
    nt = len(srcs)

    def body(*refs):
        src, land = refs[:nt], refs[nt:2 * nt]
        sems = refs[2 * nt + 1:4 * nt + 1]
        token = refs[-1]
        me, peers = _peers()
        for t in range(nt):
            for k, (pid, plin) in enumerate(peers):
                pltpu.make_async_remote_copy(
                    src_ref=src[t].at[plin] if scatter else src[t], dst_ref=land[t].at[me],
                    send_sem=sems[2 * t].at[k], recv_sem=sems[2 * t + 1].at[k],
                    device_id=pid, device_id_type=pl.DeviceIdType.MESH).start()
        token[...] = jnp.zeros_like(token)

    out_shape = [pltpu.SemaphoreType.DMA((N_DEV - 1,)) for _ in range(2 * nt)]
    out_shape += [pltpu.HBM(a.shape, a.dtype) for a in lands]
    out_shape += [jax.ShapeDtypeStruct((8, LANES), F32)]
    srcs = [pltpu.with_memory_space_constraint(a, pltpu.HBM) for a in srcs]
    res = pl.pallas_call(
        body, name=name, out_shape=out_shape,
        in_specs=[_HBM] * (2 * nt) + [_ANY],
        out_specs=[_SEM] * (2 * nt) + [_HBM] * nt + [pl.BlockSpec(memory_space=pltpu.VMEM)],
        input_output_aliases={nt + i: 2 * nt + i for i in range(nt)},
        compiler_params=pltpu.CompilerParams(has_side_effects=_EFFECT),
    )(*srcs, *[pltpu.with_memory_space_constraint(a, pltpu.HBM) for a in lands], after)
    sems, thru = res[:2 * nt], res[2 * nt:3 * nt]
    return [(sems[2 * t], sems[2 * t + 1], srcs[t], thru[t]) for t in range(nt)], res[-1]


def _xfer_wait(handle, scatter, after, name):
    send, recv, src, land = handle
    after = list(after) if isinstance(after, (list, tuple)) else [after]

    def body(src_ref, land_ref, send_sem, recv_sem, *rest):
        _, peers = _peers()
        for k, (pid, plin) in enumerate(peers):
            cp = pltpu.make_async_remote_copy(
                src_ref=src_ref.at[plin] if scatter else src_ref, dst_ref=land_ref.at[plin],
                send_sem=send_sem.at[k], recv_sem=recv_sem.at[k],
                device_id=pid, device_id_type=pl.DeviceIdType.MESH)
            cp.wait_send()
            cp.wait_recv()

    return pl.pallas_call(
        body, name=name, out_shape=pltpu.HBM(land.shape, land.dtype),
        in_specs=(_HBM, _HBM, _SEM, _SEM) + (_ANY,) * len(after), out_specs=_HBM, input_output_aliases={1: 0},
        compiler_params=pltpu.CompilerParams(has_side_effects=_EFFECT),
    )(src, land, send, recv, *after)


def _landing(x, me):
    return lax.dynamic_update_slice(lax.empty((N_DEV,) + x.shape, x.dtype), x[None], (me,) + (0,) * x.ndim)


def _chip_peers():
    x, y, c = lax.axis_index("x"), lax.axis_index("y"), lax.axis_index("c")
    lin = lambda px, py, pc: 4 * px + 2 * py + pc
    sibling = ((x, y, 1 - c), lin(x, y, 1 - c))
    chips = [((1 - x, y, c), lin(1 - x, y, c)), ((x, 1 - y, c), lin(x, 1 - y, c)),
             ((1 - x, 1 - y, c), lin(1 - x, 1 - y, c))]
    return lin(x, y, c), sibling, chips


N_CHIPS_OTHER = 3


def _gather2_start(srcs, lands, after, name):
    nt = len(srcs)

    def body(*refs):
        src, land = refs[:nt], refs[nt:2 * nt]
        sems = refs[2 * nt + 1:5 * nt + 1]
        token = refs[-1]
        me, sibling, chips = _chip_peers()
        for t in range(nt):
            send, recv_ici, recv_sib = sems[3 * t], sems[3 * t + 1], sems[3 * t + 2]
            pltpu.make_async_remote_copy(src_ref=src[t], dst_ref=land[t].at[me], send_sem=send.at[0],
                                         recv_sem=recv_sib.at[0], device_id=sibling[0],
                                         device_id_type=pl.DeviceIdType.MESH).start()
            for j, (pid, _) in enumerate(chips):
                pltpu.make_async_remote_copy(src_ref=src[t], dst_ref=land[t].at[me], send_sem=send.at[1 + j],
                                             recv_sem=recv_ici.at[j], device_id=pid,
                                             device_id_type=pl.DeviceIdType.MESH).start()
        token[...] = jnp.zeros_like(token)

    out_shape = []
    for _ in range(nt):
        out_shape += [pltpu.SemaphoreType.DMA((1 + N_CHIPS_OTHER,)), pltpu.SemaphoreType.DMA((N_CHIPS_OTHER,)),
                      pltpu.SemaphoreType.DMA((1,))]
    out_shape += [pltpu.HBM(a.shape, a.dtype) for a in list(srcs) + list(lands)]
    out_shape += [jax.ShapeDtypeStruct((8, LANES), F32)]
    res = pl.pallas_call(
        body, name=name, out_shape=out_shape,
        in_specs=[_HBM] * (2 * nt) + [_ANY],
        out_specs=[_SEM] * (3 * nt) + [_HBM] * (2 * nt) + [pl.BlockSpec(memory_space=pltpu.VMEM)],
        input_output_aliases={i: 3 * nt + i for i in range(2 * nt)},
        compiler_params=pltpu.CompilerParams(has_side_effects=_EFFECT),
    )(*[pltpu.with_memory_space_constraint(a, pltpu.HBM) for a in list(srcs) + list(lands)], after)
    sems, thru = res[:3 * nt], res[3 * nt:5 * nt]
    return [(sems[3 * t], sems[3 * t + 1], sems[3 * t + 2], thru[t], thru[nt + t]) for t in range(nt)], res[-1]


def _gather2_relay(handles, after, name):
    nt = len(handles)

    def body(*refs):
        src, land = refs[:nt], refs[nt:2 * nt]
        send1, recv_ici = refs[2 * nt:3 * nt], refs[3 * nt:4 * nt]
        outs = refs[4 * nt + 1:]
        send2, recv2 = outs[:nt], outs[nt:2 * nt]
        me, sibling, chips = _chip_peers()
        for t in range(nt):
            pltpu.make_async_remote_copy(src_ref=src[t], dst_ref=land[t].at[me], send_sem=send1[t].at[0],
                                         recv_sem=recv_ici[t].at[0], device_id=sibling[0],
                                         device_id_type=pl.DeviceIdType.MESH).wait_send()
            for j, (pid, plin) in enumerate(chips):
                arrived = pltpu.make_async_remote_copy(src_ref=src[t], dst_ref=land[t].at[plin], send_sem=send1[t].at[1 + j],
                                                       recv_sem=recv_ici[t].at[j], device_id=pid,
                                                       device_id_type=pl.DeviceIdType.MESH)
                arrived.wait_send()
                arrived.wait_recv()
                pltpu.make_async_remote_copy(src_ref=land[t].at[plin], dst_ref=land[t].at[plin], send_sem=send2[t].at[j],
                                             recv_sem=recv2[t].at[j], device_id=sibling[0],
                                             device_id_type=pl.DeviceIdType.MESH).start()

    srcs = [h[3] for h in handles]
    lands = [h[4] for h in handles]
    out_shape = [pltpu.SemaphoreType.DMA((N_CHIPS_OTHER,)) for _ in range(2 * nt)]
    out_shape += [pltpu.HBM(a.shape, a.dtype) for a in srcs + lands]
    res = pl.pallas_call(
        body, name=name, out_shape=out_shape,
        in_specs=[_HBM] * (2 * nt) + [_SEM] * (2 * nt) + [_ANY],
        out_specs=[_SEM] * (2 * nt) + [_HBM] * (2 * nt),
        input_output_aliases={i: 2 * nt + i for i in range(2 * nt)},
        compiler_params=pltpu.CompilerParams(has_side_effects=_EFFECT),
    )(*srcs, *lands, *[h[0] for h in handles], *[h[1] for h in handles], after)
    return [(handles[t][2], res[t], res[nt + t], res[3 * nt + t]) for t in range(nt)]


def _gather2_wait(handle, after, name):
    recv_sib, send2, recv2, land = handle

    def body(land_ref, recv_sib_sem, send2_sem, recv2_sem, after_ref, land_out):
        me, sibling, chips = _chip_peers()
        pltpu.make_async_remote_copy(src_ref=land_ref.at[me], dst_ref=land_ref.at[sibling[1]], send_sem=send2_sem.at[0],
                                     recv_sem=recv_sib_sem.at[0], device_id=sibling[0],
                                     device_id_type=pl.DeviceIdType.MESH).wait_recv()
        for j, (pid, plin) in enumerate(chips):
            relayed = pltpu.make_async_remote_copy(src_ref=land_ref.at[plin], dst_ref=land_ref.at[plin], send_sem=send2_sem.at[j],
                                                   recv_sem=recv2_sem.at[j], device_id=sibling[0],
                                                   device_id_type=pl.DeviceIdType.MESH)
            relayed.wait_send()
            relayed.wait_recv()

    return pl.pallas_call(
        body, name=name, out_shape=pltpu.HBM(land.shape, land.dtype),
        in_specs=(_HBM, _SEM, _SEM, _SEM, _ANY), out_specs=_HBM, input_output_aliases={0: 0},
        compiler_params=pltpu.CompilerParams(has_side_effects=_EFFECT),
    )(land, recv_sib, send2, recv2, after)


def _exchange(arrs, scatter, name):
    nt = len(arrs)
    out_shape = [jax.ShapeDtypeStruct(a.shape if scatter else (N_DEV,) + a.shape, a.dtype) for a in arrs]

    def body(*refs):
        ins, outs = refs[:nt], refs[nt:2 * nt]
        send, recv, loc = refs[2 * nt:]
        me, peers = _peers()
        copies = []
        for t in range(nt):
            own = pltpu.make_async_copy(ins[t].at[me] if scatter else ins[t], outs[t].at[me], loc.at[t])
            own.start()
            copies.append(own)
            for k, (pid, plin) in enumerate(peers):
                cp = pltpu.make_async_remote_copy(
                    src_ref=ins[t].at[plin] if scatter else ins[t], dst_ref=outs[t].at[me],
                    send_sem=send.at[t, k], recv_sem=recv.at[t, k],
                    device_id=pid, device_id_type=pl.DeviceIdType.MESH)
                cp.start()
                copies.append(cp)
        for cp in copies:
            cp.wait()

    any_spec = pl.BlockSpec(memory_space=pl.ANY)
    return pl.pallas_call(
        body, out_shape=out_shape, in_specs=[any_spec] * nt, out_specs=[any_spec] * nt,
        scratch_shapes=[pltpu.SemaphoreType.DMA((nt, N_DEV - 1)), pltpu.SemaphoreType.DMA((nt, N_DEV - 1)),
                        pltpu.SemaphoreType.DMA((nt,))],
        name=name)(*arrs)


def _adamw_body(n_parts, stacked=True):
    def body(p_ref, w_ref, m_ref, v_ref, *rest):
        g_out, d_out, m_out, v_out = rest[-4:]
        part = (lambda i: p_ref[i]) if stacked else (lambda i: p_ref[i][...])
        g = part(0).astype(F32)
        for i in range(1, n_parts):
            g = g + part(i).astype(F32)
        m2 = ADAM_B1 * m_ref[...] + (1.0 - ADAM_B1) * g
        v2 = ADAM_B2 * v_ref[...] + (1.0 - ADAM_B2) * jnp.square(g)
        m_hat = m2 / (1.0 - ADAM_B1 ** ADAM_STEP)
        v_hat = v2 / (1.0 - ADAM_B2 ** ADAM_STEP)
        g_out[...] = g
        d_out[...] = -ADAM_LR * (m_hat / (jnp.sqrt(v_hat) + ADAM_EPS) + ADAM_WD * w_ref[...])
        m_out[...] = m2
        v_out[...] = v2

    return body


def _adamw_layer(own, land, me, w, m, v, layer, prev, name):
    _, r, c = own.shape
    tr = _tile(r, 256, 8)
    blk = pl.BlockSpec((None, tr, c), lambda i, me_ref: (layer, i, 0))
    share = lambda k: pl.BlockSpec((None, tr, c), lambda i, me_ref: (me_ref[0] ^ k, i, 0))
    in_specs = [share(k) for k in range(N_DEV)] + [blk, blk, blk]
    args = [own] + [land] * (N_DEV - 1) + [w, m, v]
    aliases = {}
    if prev is not None:
        in_specs += [_ANY] * 4
        args += list(prev)
        aliases = {1 + N_DEV + 3 + i: i for i in range(4)}

    def body(me_ref, *refs):
        refs = (refs[:N_DEV],) + refs[N_DEV:]
        _adamw_body(N_DEV, stacked=False)(*refs)

    return pl.pallas_call(
        body,
        grid_spec=pltpu.PrefetchScalarGridSpec(num_scalar_prefetch=1, grid=(r // tr,), in_specs=in_specs,
                                               out_specs=[blk] * 4),
        out_shape=[jax.ShapeDtypeStruct(w.shape, F32)] * 4, input_output_aliases=aliases, name=name,
        compiler_params=_cparams(1))(me, *args)


def _adamw(parts, w, m, v, name):
    p, nl, r, c = parts.shape
    tr = _tile(r, 256, 8)
    body = _adamw_body(p)

    blk = (None, tr, c)
    imap = lambda l, i: (l, i, 0)
    out = ((nl, r, c), F32, blk, imap)
    return _call(body, grid=(nl, r // tr),
                 ins=[(parts, (p, None, tr, c), lambda l, i: (0, l, i, 0)), (w, blk, imap), (m, blk, imap), (v, blk, imap)],
                 outs=[out] * 4, name=name)


def _rows(x):
    return x.reshape(-1, LANES)


def _pad_rows(x, mult=8):
    r = x.shape[0]
    extra = (-r) % mult
    return jnp.pad(x, ((0, extra), (0, 0))) if extra else x


def _shard_cols(x, me, groups):
    lead = x.shape[:-1]
    xr = x.reshape(lead + (N_DEV, groups * LANES))
    xs = lax.dynamic_index_in_dim(xr, me, axis=len(lead), keepdims=False)
    return xs.reshape(N_DEV, -1, LANES)


def kernel(x, c, ada_w, ada_b, ln_g, ln_b, dn_w_in, dn_conv_w, dn_a_log, dn_dt_bias, dn_norm_w, dn_w_out, cf_w_in, cf_dw_w, cf_dw_b, cf_ln_g, cf_ln_b, cf_w_out, ff_w1, ff_w2, loss_target, m_ada_w, m_ada_b, m_ln_g, m_ln_b, m_dn_w_in, m_dn_conv_w, m_dn_a_log, m_dn_dt_bias, m_dn_norm_w, m_dn_w_out, m_cf_w_in, m_cf_dw_w, m_cf_dw_b, m_cf_ln_g, m_cf_ln_b, m_cf_w_out, m_ff_w1, m_ff_w2, v_ada_w, v_ada_b, v_ln_g, v_ln_b, v_dn_w_in, v_dn_conv_w, v_dn_a_log, v_dn_dt_bias, v_dn_norm_w, v_dn_w_out, v_cf_w_in, v_cf_dw_w, v_cf_dw_b, v_cf_ln_g, v_cf_ln_b, v_cf_w_out, v_ff_w1, v_ff_w2):
    depth, d, _ = ada_w.shape
    n_a, n_b = dn_w_in.shape[0], cf_w_in.shape[0]
    heads = dn_a_log.shape[1]
    hw = heads * HEAD_DIM
    taps = cf_dw_w.shape[1]
    s = x.shape[1]
    alpha = (2.0 * depth) ** 0.25
    me = 4 * lax.axis_index("x") + 2 * lax.axis_index("y") + lax.axis_index("c")
    me_arr = jnp.reshape(me, (1,)).astype(jnp.int32)
    xs, tgt = x[0], loss_target[0]

    dn_in_cols = dn_w_in.shape[2]
    keys, shards = [], []
    for i in range(depth):
        j = i // 2
        mixer = [("dn_in", dn_w_in), ("dn_out", dn_w_out)] if i % 2 == 0 else [("cf_in", cf_w_in), ("cf_out", cf_w_out)]
        for nm, wt in mixer:
            keys.append((nm, j))
            shards.append(wt[j].astype(MXU_DTYPE))
        keys += [("ff1", i), ("ff2", i)]
        shards += [ff_w1[i].astype(MXU_DTYPE), ff_w2[i].astype(MXU_DTYPE)]

    small_local = [_rows(ln_g), _rows(ln_b), _rows(dn_conv_w), _rows(cf_dw_w), _rows(cf_dw_b), _rows(cf_ln_g),
                   _rows(cf_ln_b), _rows(c)]
    sizes = [a.shape[0] for a in small_local]
    packed = _pad_rows(jnp.concatenate(small_local, axis=0))
    (small_all,) = _exchange([packed], False, "comm_gather_params")
    offs = [0]
    for z in sizes:
        offs.append(offs[-1] + z)

    def small(i):
        return small_all[:, offs[i]:offs[i + 1], :]

    def unshard(piece, lead, groups):
        t = piece.reshape((N_DEV,) + lead + (groups * LANES,))
        t = jnp.moveaxis(t, 0, len(lead))
        return t.reshape(lead + (N_DEV * groups * LANES,))

    ln_g_f = unshard(small(0), (depth, 2), 1)
    ln_b_f = unshard(small(1), (depth, 2), 1)
    conv_w_f = unshard(small(2), (n_a, DN_CONV), 3 * heads // N_DEV)
    dw_w_f = unshard(small(3), (n_b, taps), 1)
    dw_b_f = unshard(small(4), (n_b,), 1)
    cf_ln_g_f = unshard(small(5), (n_b,), 1)
    cf_ln_b_f = unshard(small(6), (n_b,), 1)
    c_all = small(7).reshape(N_DEV, d)

    mod_part, cond_all = _ada_fwd(c_all, ada_w)
    (mod_all,) = _exchange([mod_part], False, "comm_gather_mod")
    mod_mine = lax.dynamic_index_in_dim(mod_all, me, axis=2, keepdims=False)
    mod_mine = jnp.moveaxis(mod_mine, 0, 1).reshape(depth, N_MOD * d)

    handles, token = _gather2_start(shards, [_landing(a, me) for a in shards], mod_all, "gather_weights_start")
    handles = dict(zip(keys, handles))
    groups = [keys[:1], keys[1:4]] + [keys[4 * i:4 * i + 4] for i in range(1, depth)]
    group_of = {k: n for n, grp in enumerate(groups) for k in grp}
    relayed, weights = {}, {}

    def relay(n, after):
        if n < len(groups) and groups[n][0] not in relayed:
            hs = _gather2_relay([handles[k] for k in groups[n]], after, "gather_relay_%d" % n)
            relayed.update(zip(groups[n], hs))

    relay(0, token)

    def gathered(key, after):
        if key not in weights:
            relay(group_of[key], after)
            if key[0] == "ff1":
                relay(key[1] + 2, after)
            weights[key] = _gather2_wait(relayed[key], after, "gather_wait_%s_%d" % key)
        return weights[key]

    def get_dn_in(j):
        def get(after):
            g = gathered(("dn_in", j), after)
            w = jnp.moveaxis(g, 0, 1).reshape(d, N_DEV * dn_in_cols)
            return jnp.pad(w, ((0, 0), (0, 4 * hw + LANES - N_DEV * dn_in_cols)))
        return get

    def get_rows(key):
        return lambda after: gathered(key, after).reshape((-1, d))

    def get_cols(key):
        return lambda after: gathered(key, after)

    def add_bias(a, b):
        return (a + b,), ()

    (mod,), _ = _rowmap(add_bias, [mod_mine, ada_b], [], [(N_MOD * d, F32)], [], "ada_bias", pin=token)
    mod_rows = mod.reshape(depth * N_MOD, 1, d)
    ln_g_rows = ln_g_f.reshape(depth * 2, 1, d)
    ln_b_rows = ln_b_f.reshape(depth * 2, 1, d)

    def mod_row(i, j):
        return (mod_rows, i * N_MOD + j)

    def ln_row(rows, i, j):
        return (rows, i * 2 + j)

    subs = []
    h_cur = _modulate_fwd(xs, mod_row(0, 1), mod_row(0, 0))
    x_cur = xs
    last = None
    for i in range(depth):
        j = i // 2
        if i % 2 == 0:
            y, res = _deltanet_fwd(h_cur, get_dn_in(j), conv_w_f[j], dn_a_log[j], dn_dt_bias[j], dn_norm_w[j],
                                   get_rows(("dn_out", j)))
        else:
            y, res = _conformer_fwd(h_cur, get_cols(("cf_in", j)), dw_w_f[j], dw_b_f[j][None, :], cf_ln_g_f[j][None, :],
                                    cf_ln_b_f[j][None, :], get_rows(("cf_out", j)))
        p1 = (mod_row(i, 2), ln_row(ln_g_rows, i, 0), ln_row(ln_b_rows, i, 0), mod_row(i, 4), mod_row(i, 3))
        x_mid, h_mid = _combine_fwd(alpha, x_cur, y, *p1)
        subs.append((x_cur, y, p1, res))
        m_out, res2 = _mlp_fwd(h_mid, get_cols(("ff1", i)), get_rows(("ff2", i)))
        if i + 1 < depth:
            p2 = (mod_row(i, 5), ln_row(ln_g_rows, i, 1), ln_row(ln_b_rows, i, 1), mod_row(i + 1, 1), mod_row(i + 1, 0))
            x_next, h_next = _combine_fwd(alpha, x_mid, m_out, *p2)
            subs.append((x_mid, m_out, p2, res2))
            x_cur, h_cur = x_next, h_next
        else:
            p2 = (mod_row(i, 5), ln_row(ln_g_rows, i, 1), ln_row(ln_b_rows, i, 1))
            last = (x_mid, m_out, p2, res2)

    x_in, y_in, p_last, res_last = last
    dx, dy, (loss_acc, g_gt, g_g, g_b) = _last_fwd_bwd(alpha, x_in, y_in, tgt, *p_last)
    loss = lax.psum(loss_acc[0, 0], ("x", "y", "c"))

    d_mod = [[None] * N_MOD for _ in range(depth)]
    d_ln_g = [[None, None] for _ in range(depth)]
    d_ln_b = [[None, None] for _ in range(depth)]
    d_mod[depth - 1][5], d_ln_g[depth - 1][1], d_ln_b[depth - 1][1] = g_gt, g_g, g_b
    gw = dict(dn=[None] * n_a, cf=[None] * n_b)

    sent = {}

    def send_grads(named, tag):
        parts = [p for _, p in named]
        hs, tok = _xfer_start(parts, [lax.empty(p.shape, p.dtype) for p in parts], True, parts[0], "scatter_start_" + tag)
        for (key, _), hnd in zip(named, hs):
            sent[key] = hnd
        return tok

    def by_rows(g):
        return g.reshape((N_DEV, g.shape[0] // N_DEV, g.shape[1]))

    def send_mlp(i, d_w1, d_w2):
        return send_grads([(("ff1", i), d_w1), (("ff2", i), by_rows(d_w2))], "ff_%d" % i)

    dh, d_w1, d_w2 = _mlp_bwd(res_last, dy)
    pin = send_mlp(depth - 1, d_w1, d_w2)
    for idx in range(len(subs) - 1, -1, -1):
        x_in, y_in, prm, res = subs[idx]
        i, second = idx // 2, idx % 2
        dx, dy, (g_gt, g_g, g_b, g_sc, g_sh) = _combine_bwd(alpha, x_in, y_in, dx, dh, *prm, pin=pin)
        d_mod[i][5 if second else 2], d_ln_g[i][second], d_ln_b[i][second] = g_gt, g_g, g_b
        nxt_i, nxt_base = (i + 1, 0) if second else (i, 3)
        d_mod[nxt_i][nxt_base + 1], d_mod[nxt_i][nxt_base] = g_sc, g_sh
        j = i // 2
        if second:
            dh, d_w1, d_w2 = _mlp_bwd(res, dy)
            pin = send_mlp(i, d_w1, d_w2)
        elif i % 2 == 0:
            dh, gw["dn"][j] = _deltanet_bwd(res, dy, conv_w_f[j])
            d_in = gw["dn"][j]["w_in"][:, :N_DEV * dn_in_cols].reshape(d, N_DEV, dn_in_cols)
            pin = send_grads([(("dn_in", j), jnp.moveaxis(d_in, 1, 0)), (("dn_out", j), by_rows(gw["dn"][j]["w_out"]))],
                             "dn_%d" % j)
        else:
            dh, gw["cf"][j] = _conformer_bwd(res, dy, dw_w_f[j], cf_ln_g_f[j][None, :], cf_ln_b_f[j][None, :])
            pin = send_grads([(("cf_in", j), gw["cf"][j]["w_in"]), (("cf_out", j), by_rows(gw["cf"][j]["w_out"]))],
                             "cf_%d" % j)
    grad_x, g_sc, g_sh = _modulate_bwd(xs, dx, dh, mod_row(0, 1), mod_row(0, 0), pin=pin)
    d_mod[0][1], d_mod[0][0] = g_sc, g_sh
    d_mod_full = jnp.concatenate([jnp.concatenate(r, axis=1) for r in d_mod], axis=0)

    stacked = {"dn_w_in": ("dn_in", dn_w_in, m_dn_w_in, v_dn_w_in), "dn_w_out": ("dn_out", dn_w_out, m_dn_w_out, v_dn_w_out),
               "cf_w_in": ("cf_in", cf_w_in, m_cf_w_in, v_cf_w_in), "cf_w_out": ("cf_out", cf_w_out, m_cf_w_out, v_cf_w_out),
               "ff_w1": ("ff1", ff_w1, m_ff_w1, v_ff_w1), "ff_w2": ("ff2", ff_w2, m_ff_w2, v_ff_w2)}
    chains = {key: None for key in stacked}

    def update_layer(i):
        mixer = ["dn_w_in", "dn_w_out"] if i % 2 == 0 else ["cf_w_in", "cf_w_out"]
        for key, idx in [("ff_w1", i), ("ff_w2", i)] + [(k, i // 2) for k in mixer]:
            short, w, m, v = stacked[key]
            land = _xfer_wait(sent[(short, idx)], True, sg_token, "scatter_wait_%s_%d" % (short, idx))
            chains[key] = _adamw_layer(sent[(short, idx)][2], land, me_arr, w, m, v, idx, chains[key],
                                       "adamw_%s_%d" % (key, idx))

    def stack_rows(lst):
        return jnp.stack(lst, axis=0)

    gs_ln_g = jnp.stack([jnp.concatenate(r, axis=0) for r in d_ln_g], axis=0)
    gs_ln_b = jnp.stack([jnp.concatenate(r, axis=0) for r in d_ln_b], axis=0)
    gs_conv_w = stack_rows([gw["dn"][j]["conv_w"] for j in range(n_a)])
    gs_dw_w = stack_rows([gw["cf"][j]["dw_w"] for j in range(n_b)])
    gs_dw_b = stack_rows([gw["cf"][j]["dw_b"] for j in range(n_b)])
    gs_cf_ln_g = stack_rows([gw["cf"][j]["ln_g"] for j in range(n_b)])
    gs_cf_ln_b = stack_rows([gw["cf"][j]["ln_b"] for j in range(n_b)])
    gs_a_log = stack_rows([_pad_lanes(gw["dn"][j]["a_log"], 0)[0] for j in range(n_a)])
    gs_dt_bias = stack_rows([_pad_lanes(gw["dn"][j]["dt_bias"], 0)[0] for j in range(n_a)])
    gs_norm_w = stack_rows([gw["dn"][j]["norm_w"] for j in range(n_a)])
    small_grads = [gs_ln_g, gs_ln_b, gs_conv_w, gs_dw_w, gs_dw_b, gs_cf_ln_g, gs_cf_ln_b, gs_a_log, gs_dt_bias,
                   gs_norm_w, d_mod_full]
    sg_rows = [_rows(a) for a in small_grads]
    sg_sizes = [a.shape[0] for a in sg_rows]
    sg_packed = _pad_rows(jnp.concatenate(sg_rows, axis=0))
    (sg_handle,), sg_token = _xfer_start([sg_packed], [_landing(sg_packed, me)], False, grad_x, "gather_small_grads_start")
    for i in range(depth - 1, -1, -1):
        update_layer(i)
    sg_all = _xfer_wait(sg_handle, False, [chains[key][0] for key in stacked], "gather_small_grads_wait")
    sg_offs = [0]
    for z in sg_sizes:
        sg_offs.append(sg_offs[-1] + z)

    def sg(i, shape):
        return sg_all[:, sg_offs[i]:sg_offs[i + 1], :].reshape((N_DEV,) + shape)

    dmod_all = sg(10, (depth, N_MOD * d))
    nl = ada_w.shape[2]
    dmod_cols = lax.dynamic_slice_in_dim(dmod_all, me * nl, nl, axis=2)
    g_ada_w = _ada_bwd(cond_all, jnp.moveaxis(dmod_cols, 0, 1))

    outs = {}

    def run_adamw(key, parts, w, m, v):
        shp = w.shape
        as3 = lambda t: t.reshape((-1,) + shp[-2:]) if t.ndim >= 3 else t.reshape((1,) + shp)
        parts3 = parts.reshape((parts.shape[0],) + as3(w).shape)
        res = _adamw(parts3, as3(w), as3(m), as3(v), "adamw_" + key)
        outs[key] = tuple(r.reshape(shp) for r in res)

    run_adamw("ada_w", g_ada_w[None], ada_w, m_ada_w, v_ada_w)

    cgroups = 3 * heads // N_DEV
    shard_parts = [
        _shard_cols(sg(0, (depth, 2, d)), me, 1), _shard_cols(sg(1, (depth, 2, d)), me, 1),
        _shard_cols(sg(2, (n_a, DN_CONV, 3 * hw)), me, cgroups), _shard_cols(sg(3, (n_b, taps, d)), me, 1),
        _shard_cols(sg(4, (n_b, d)), me, 1), _shard_cols(sg(5, (n_b, d)), me, 1), _shard_cols(sg(6, (n_b, d)), me, 1),
    ]
    repl_parts = [sg(7, (n_a, LANES)), sg(8, (n_a, LANES)), sg(9, (n_a, HEAD_DIM)),
                  sg(10, (depth, N_MOD * d)).reshape(N_DEV, -1, LANES)]
    small_parts = shard_parts + repl_parts
    sp_sizes = [a.shape[1] for a in small_parts]
    sp_offs = [0]
    for z in sp_sizes:
        sp_offs.append(sp_offs[-1] + z)
    parts_packed = jnp.zeros((N_DEV, sp_offs[-1] + (-sp_offs[-1]) % 8, LANES), F32)
    for off, part in zip(sp_offs, small_parts):
        parts_packed = lax.dynamic_update_slice(parts_packed, part, (0, off, 0))

    def pad_heads(t):
        return jnp.pad(t, ((0, 0), (0, LANES - heads)))

    def pack_state(ln_g_, ln_b_, conv_w_, dw_w_, dw_b_, cln_g_, cln_b_, a_log_, dt_b_, norm_w_, ada_b_):
        rows = [_rows(ln_g_), _rows(ln_b_), _rows(conv_w_), _rows(dw_w_), _rows(dw_b_), _rows(cln_g_), _rows(cln_b_),
                pad_heads(a_log_), pad_heads(dt_b_), norm_w_, _rows(ada_b_)]
        return _pad_rows(jnp.concatenate(rows, axis=0))

    w_s = pack_state(ln_g, ln_b, dn_conv_w, cf_dw_w, cf_dw_b, cf_ln_g, cf_ln_b, dn_a_log, dn_dt_bias, dn_norm_w, ada_b)
    m_s = pack_state(m_ln_g, m_ln_b, m_dn_conv_w, m_cf_dw_w, m_cf_dw_b, m_cf_ln_g, m_cf_ln_b, m_dn_a_log,
                     m_dn_dt_bias, m_dn_norm_w, m_ada_b)
    v_s = pack_state(v_ln_g, v_ln_b, v_dn_conv_w, v_cf_dw_w, v_cf_dw_b, v_cf_ln_g, v_cf_ln_b, v_dn_a_log,
                     v_dn_dt_bias, v_dn_norm_w, v_ada_b)
    res_s = _adamw(parts_packed[:, None], w_s[None], m_s[None], v_s[None], "adamw_small")
    small_keys = ["ln_g", "ln_b", "dn_conv_w", "cf_dw_w", "cf_dw_b", "cf_ln_g", "cf_ln_b", "dn_a_log", "dn_dt_bias",
                  "dn_norm_w", "ada_b"]
    small_shapes = [ln_g.shape, ln_b.shape, dn_conv_w.shape, cf_dw_w.shape, cf_dw_b.shape, cf_ln_g.shape,
                    cf_ln_b.shape, dn_a_log.shape, dn_dt_bias.shape, dn_norm_w.shape, ada_b.shape]
    for n, (key, shp) in enumerate(zip(small_keys, small_shapes)):
        vals = []
        for r in res_s:
            piece = r[0, sp_offs[n]:sp_offs[n + 1], :]
            if key in ("dn_a_log", "dn_dt_bias"):
                piece = piece[:, :heads]
            vals.append(piece.reshape(shp))
        outs[key] = tuple(vals)

    for key in stacked:
        outs[key] = tuple(chains[key])

    order = ["ada_w", "ada_b", "ln_g", "ln_b", "dn_w_in", "dn_conv_w", "dn_a_log", "dn_dt_bias", "dn_norm_w",
             "dn_w_out", "cf_w_in", "cf_dw_w", "cf_dw_b", "cf_ln_g", "cf_ln_b", "cf_w_out", "ff_w1", "ff_w2"]
    result = [loss, grad_x[None]]
    for part in range(4):
        result += [outs[k][part] for k in order]
    return tuple(result)
```

```python
import functools

import jax
import jax.numpy as jnp
from jax import lax
from jax.experimental import pallas as pl
from jax.experimental.pallas import tpu as pltpu

F32 = jnp.float32
MXU_DTYPE = jnp.bfloat16
N_DEV = 8
LANES = 128
HEAD_DIM = 128
CHUNK = 64
DN_CONV = 4
N_MOD = 6
LN_EPS = 1e-5
RMS_EPS = 1e-6
L2_EPS = 1e-6
ADAM_LR = 0.001
ADAM_B1 = 0.9
ADAM_B2 = 0.999
ADAM_EPS = 1e-08
ADAM_WD = 0.01
ADAM_STEP = 10

NN = ((1,), (0,))
NT = ((1,), (1,))
TN = ((0,), (0,))

ROW_TILE = 512
CONV_TILE = 256
SHORT_CONV_TILE = 1024


def _mdot(a, b, dims=NN):
    return lax.dot_general(a.astype(MXU_DTYPE), b.astype(MXU_DTYPE), (dims, ((), ())), preferred_element_type=F32)


def _split3(x):
    hi = x.astype(MXU_DTYPE)
    r1 = x - hi.astype(F32)
    mid = r1.astype(MXU_DTYPE)
    lo = (r1 - mid.astype(F32)).astype(MXU_DTYPE)
    return hi, mid, lo


def _dot01(a, b, dims=NN, mask_first=True):
    d = lambda p, q: lax.dot_general(p, q, (dims, ((), ())), preferred_element_type=F32)
    if mask_first:
        m = a.astype(MXU_DTYPE)
        return sum(d(m, p) for p in _split3(b))
    m = b.astype(MXU_DTYPE)
    return sum(d(p, m) for p in _split3(a))


def _cparams(n):
    return pltpu.CompilerParams(dimension_semantics=("arbitrary",) * n)


def _call(body, *, grid, ins, outs, name, scratch=()):
    res = pl.pallas_call(
        body,
        grid=grid,
        in_specs=[pl.BlockSpec(memory_space=pl.ANY) if b is None else pl.BlockSpec(b, m) for _, b, m in ins],
        out_specs=[pl.BlockSpec(b, m) for _, _, b, m in outs],
        out_shape=[jax.ShapeDtypeStruct(s, d) for s, d, _, _ in outs],
        scratch_shapes=list(scratch),
        name=name,
        compiler_params=_cparams(len(grid)),
    )(*[a for a, _, _ in ins])
    return res


def _tile(n, pref, unit=LANES):
    if n <= pref:
        return n
    t = (pref // unit) * unit
    while t > unit and n % t:
        t -= unit
    assert n % t == 0, (n, pref)
    return t


def _rowmap(fn, rows, consts, row_outs, acc_outs, name, pin=None):
    rows = [r if isinstance(r, tuple) else (r, r.shape[1], 0) for r in rows]
    s = rows[0][0].shape[0]
    tm = min(ROW_TILE, s)
    nr, nc, no, na = len(rows), len(consts), len(row_outs), len(acc_outs)
    npin = 0 if pin is None else 1

    def body(*refs):
        rin, cin = refs[:nr], refs[nr:nr + nc]
        refs = refs[:nr + nc] + refs[nr + nc + npin:]
        rout, aout = refs[nr + nc:nr + nc + no], refs[nr + nc + no:]
        ro, ao = fn(*[r[...] for r in rin], *[c[...] for c in cin])
        for ref, val in zip(rout, ro):
            ref[...] = val.astype(ref.dtype)
        if na:
            first = pl.program_id(0) == 0

            @pl.when(first)
            def _():
                for ref, val in zip(aout, ao):
                    ref[...] = val

            @pl.when(jnp.logical_not(first))
            def _():
                for ref, val in zip(aout, ao):
                    ref[...] += val

    ins = [(a, (tm, w), functools.partial(lambda i, cb: (i, cb), cb=cb)) for a, w, cb in rows]
    for c in consts:
        if isinstance(c, tuple):
            ins.append((c[0], (None, 1, c[0].shape[2]), functools.partial(lambda i, n: (n, 0, 0), n=c[1])))
        else:
            ins.append((c, c.shape, lambda i: (0, 0)))
    if pin is not None:
        ins.append((pin, None, None))
    outs = [((s, w), d, (tm, w), lambda i: (i, 0)) for w, d in row_outs]
    outs += [(shp, F32, shp, lambda i: (0, 0)) for shp in acc_outs]
    res = _call(body, grid=(s // tm,), ins=ins, outs=outs, name=name)
    return res[:no], res[no:]


def _ln(z, g, b):
    mu = jnp.mean(z, -1, keepdims=True)
    var = jnp.mean(jnp.square(z - mu), -1, keepdims=True)
    return (z - mu) * lax.rsqrt(var + LN_EPS) * g + b


def _combine(alpha, x, y, gt, g, b, sc, sh):
    xn = _ln(alpha * x + (1.0 + gt) * y, g, b)
    return xn, xn * (1.0 + sc) + sh


def _modulate_fwd(x, sc, sh):
    def fn(x, sc, sh):
        return ((x * (1.0 + sc) + sh),), ()

    (h,), _ = _rowmap(fn, [x], [sc, sh], [(x.shape[1], MXU_DTYPE)], [], "modulate_fwd")
    return h


def _modulate_bwd(x, dx, dh, sc, sh, pin=None):
    d = x.shape[1]

    def fn(x, dx, dh, sc, sh):
        _, vjp = jax.vjp(lambda x, sc, sh: x * (1.0 + sc) + sh, x, sc, sh)
        gx, gsc, gsh = vjp(dh)
        return (dx + gx,), (gsc, gsh)

    (gx,), (gsc, gsh) = _rowmap(fn, [x, dx, dh], [sc, sh], [(d, F32)], [(1, d), (1, d)], "modulate_bwd", pin=pin)
    return gx, gsc, gsh


def _combine_fwd(alpha, x, y, gt, g, b, sc, sh):
    d = x.shape[1]

    def fn(x, y, gt, g, b, sc, sh):
        return _combine(alpha, x, y, gt, g, b, sc, sh), ()

    (xn, h), _ = _rowmap(fn, [x, y], [gt, g, b, sc, sh], [(d, F32), (d, MXU_DTYPE)], [], "combine_fwd")
    return xn, h


def _combine_bwd(alpha, x, y, dxn, dh, gt, g, b, sc, sh, pin=None):
    d = x.shape[1]

    def fn(x, y, dxn, dh, gt, g, b, sc, sh):
        _, vjp = jax.vjp(functools.partial(_combine, alpha), x, y, gt, g, b, sc, sh)
        gx, gy, ggt, gg, gb, gsc, gsh = vjp((dxn, dh))
        return (gx, gy), (ggt, gg, gb, gsc, gsh)

    (gx, gy), accs = _rowmap(fn, [x, y, dxn, dh], [gt, g, b, sc, sh], [(d, F32), (d, MXU_DTYPE)],
                             [(1, d)] * 5, "combine_bwd", pin=pin)
    return gx, gy, accs


def _last_fwd_bwd(alpha, x, y, tgt, gt, g, b):
    d = x.shape[1]

    def fn(x, y, tgt, gt, g, b):
        xn, vjp = jax.vjp(lambda x, y, gt, g, b: _ln(alpha * x + (1.0 + gt) * y, g, b), x, y, gt, g, b)
        err = xn - tgt
        gx, gy, ggt, gg, gb = vjp(err * (1.0 / d))
        rows = jnp.sum(jnp.square(err), axis=-1, keepdims=True)
        loss = (0.5 / d) * jnp.sum(rows, axis=0, keepdims=True) * jnp.ones((1, LANES), F32)
        return (gx, gy), (loss, ggt, gg, gb)

    (gx, gy), accs = _rowmap(fn, [x, y, tgt], [gt, g, b], [(d, F32), (d, MXU_DTYPE)],
                             [(1, LANES), (1, d), (1, d), (1, d)], "last_fwd_bwd")
    return gx, gy, accs


MM_VMEM_BUDGET = 40 * 2 ** 20


def _fit(options, cost):
    for o in options:
        if 2 * cost(o) <= MM_VMEM_BUDGET:
            return o
    return options[-1]


def _row_tiles(m):
    return [t for t in (2048, 1024, 512, 256) if t <= m and m % t == 0] or [m]


def _mm_call(a, a_blk, a_map, b, b_blk, b_map, outs, dims, grid, name, epi=None, extra=None, split=None, blocks=None):
    nk = grid[2]
    n_out = len(outs)
    n_in = 3 if extra is not None else 2

    def body(*refs):
        a_ref, b_ref = refs[0], refs[1]
        rest = refs[n_in:]
        out_refs = rest[:n_out]

        def finish(val):
            if epi == "relu2":
                out_refs[0][...] = jnp.square(jnp.maximum(val, 0.0)).astype(out_refs[0].dtype)
            elif epi == "relu2_bwd":
                sq = refs[2][...].astype(F32)
                root = jnp.where(sq > 0.0, sq * lax.rsqrt(sq), 0.0)
                out_refs[0][...] = (val * 2.0 * root).astype(out_refs[0].dtype)
            elif split is not None:
                for g in range(split[0]):
                    out_refs[0][g] = val[:, g * split[1]:(g + 1) * split[1]].astype(out_refs[0].dtype)
            else:
                out_refs[0][...] = val.astype(out_refs[0].dtype)

        if blocks is None:
            p = lax.dot_general(a_ref[...], b_ref[...], (dims, ((), ())), preferred_element_type=F32)
        else:
            p = None
            for g in range(blocks[0]):
                part = lax.dot_general(a_ref[:, g * blocks[1]:(g + 1) * blocks[1]], b_ref[g], (dims, ((), ())),
                                       preferred_element_type=F32)
                p = part if p is None else p + part
        if nk == 1:
            finish(p)
        else:
            acc = rest[n_out]
            k = pl.program_id(2)

            @pl.when(k == 0)
            def _():
                acc[...] = p

            @pl.when(k > 0)
            def _():
                acc[...] += p

            @pl.when(k == nk - 1)
            def _():
                finish(acc[...])

    if nk > 1:
        out_blk = tuple(x for x in outs[0][2] if x is not None)
        if split is not None:
            out_blk = (out_blk[1], split[0] * split[1])
        scratch = [pltpu.VMEM(out_blk, F32)]
    else:
        scratch = []
    ins = [(a, a_blk, a_map), (b, b_blk, b_map)] + ([extra] if extra is not None else [])
    return _call(body, grid=grid, ins=ins, outs=outs, name=name, scratch=scratch)


def _isz(dt):
    return jnp.dtype(dt).itemsize


def _mm_nn(a, b, out_dtype, name, relu2=False):
    m, kdim = a.shape
    if b.ndim == 2:
        n = b.shape[1]
        tn = _tile(n, 1536 if n > 2048 else 512)
        b_blk, b_map = (kdim, tn), lambda i, j, k: (0, j)
    else:
        g, _, ng = b.shape
        n = g * ng
        tn = _tile(ng, 512)
        b_blk = (None, kdim, tn)
        b_map = functools.partial(lambda i, j, k, npg: (j // npg, 0, j % npg), npg=ng // tn)
    tm = _fit(_row_tiles(m), lambda t: t * kdim * _isz(a.dtype) + kdim * tn * _isz(b.dtype) + t * tn * _isz(out_dtype))
    grid = (m // tm, n // tn, 1)
    outs = [((m, n), out_dtype, (tm, tn), lambda i, j, k: (i, j))]
    return _mm_call(a, (tm, kdim), lambda i, j, k: (i, 0), b, b_blk, b_map, outs, NN, grid, name,
                    epi="relu2" if relu2 else None)[0]


def _mm_nt(a, b, out_dtype, name, relu2_sq=None):
    m, n = a.shape
    extra_bytes = _isz(relu2_sq.dtype) if relu2_sq is not None else 0
    if b.ndim == 2:
        kout = b.shape[0]
        to = _tile(kout, 512)
        b_blk, b_map, blocks = (to, n), lambda i, j, k: (j, 0), None
    else:
        g, kout, ng = b.shape
        to = _tile(kout, 512)
        b_blk, b_map, blocks = (g, to, ng), lambda i, j, k: (0, j, 0), (g, ng)
    tm = _fit(_row_tiles(m), lambda t: t * n * _isz(a.dtype) + to * n * _isz(b.dtype)
              + t * to * (_isz(out_dtype) + extra_bytes))
    grid = (m // tm, kout // to, 1)
    outs = [((m, kout), out_dtype, (tm, to), lambda i, j, k: (i, j))]
    extra = (relu2_sq, (tm, to), lambda i, j, k: (i, j)) if relu2_sq is not None else None
    return _mm_call(a, (tm, n), lambda i, j, k: (i, 0), b, b_blk, b_map, outs, NT, grid, name,
                    epi="relu2_bwd" if relu2_sq is not None else None, extra=extra, blocks=blocks)[0]


def _mm_tn(a, b, out_dtype, name, split_cols=False):
    m, kdim = a.shape
    n = b.shape[1]
    tk = _tile(kdim, 512)
    tn = _tile(n, 1536)
    if not split_cols:
        out, split = ((kdim, n), out_dtype, (tk, tn), lambda i, j, k: (i, j)), None
    else:
        ng = n // N_DEV
        if tn % ng:
            tn = _tile(ng, 512)
        if tn >= ng:
            gb = tn // ng
            out = ((N_DEV, kdim, ng), out_dtype, (gb, tk, ng), lambda i, j, k: (j, i, 0))
            split = (gb, ng)
        else:
            out = ((N_DEV, kdim, ng), out_dtype, (None, tk, tn),
                   functools.partial(lambda i, j, k, npg: (j // npg, i, j % npg), npg=ng // tn))
            split = None
    grid = (kdim // tk, n // tn, 1)
    return _mm_call(a, (m, tk), lambda i, j, k: (0, i), b, (m, tn), lambda i, j, k: (0, j), [out], TN, grid, name,
                    split=split)[0]


def _shifted(xa, off, rows):
    if off % 8 == 0:
        return xa[off:off + rows]
    return pltpu.roll(xa, xa.shape[0] - off, 0)[:rows]


def _conv_pad(taps):
    return -(-(taps - 1) // 8) * 8


def _conv_tile(xp_ref, w, i, rows, taps):
    pad = _conv_pad(taps)
    r0 = pl.multiple_of(i * rows, rows)
    xa = xp_ref[pl.ds(r0, rows + pad), :]
    views = [_shifted(xa, pad - (taps - 1) + j, rows) for j in range(taps)]
    acc = w[0:1, :] * views[0]
    for j in range(1, taps):
        acc = acc + w[j:j + 1, :] * views[j]
    return r0, acc, views


def _conv_back_tile(yp_ref, w, i, rows, taps):
    pad = _conv_pad(taps)
    r0 = pl.multiple_of(i * rows, rows)
    ya = yp_ref[pl.ds(r0, rows + pad), :]
    acc = w[taps - 1:taps, :] * ya[:rows]
    for j in range(taps - 1):
        acc = acc + w[j:j + 1, :] * _shifted(ya, taps - 1 - j, rows)
    return r0, acc


def _tap_sums(dy, views, taps):
    row = lax.broadcasted_iota(jnp.int32, (taps, LANES), 0)
    acc = jnp.zeros((taps, LANES), F32)
    for j in range(taps):
        acc = acc + jnp.where(row == j, jnp.sum(dy * views[j], axis=0, keepdims=True), 0.0)
    return acc


def _silu_l2(xc, l2):
    a = jax.nn.silu(xc)
    if l2:
        a = a * lax.rsqrt(jnp.sum(a * a, axis=-1, keepdims=True) + L2_EPS)
    return a


def _dn_conv_fwd(proj, conv_w, c0, nblk, l2, name):
    s = proj.shape[0]
    pad = _conv_pad(DN_CONV)
    rows = min(SHORT_CONV_TILE, s)

    def body(x_ref, w_ref, o_ref, xp):
        xp[0:pad, :] = jnp.zeros((pad, LANES), F32)
        xp[pad:, :] = x_ref[...]
        w = w_ref[...]

        def tile(i, c):
            r0, acc, _ = _conv_tile(xp, w, i, rows, DN_CONV)
            o_ref[pl.ds(r0, rows), :] = _silu_l2(acc, l2)
            return c

        lax.fori_loop(0, s // rows, tile, 0)

    return _call(body, grid=(nblk,),
                 ins=[(proj, (s, LANES), lambda c: (0, c0 + c)), (conv_w, (DN_CONV, LANES), lambda c: (0, c0 + c))],
                 outs=[((nblk, s, LANES), F32, (None, s, LANES), lambda c: (c, 0, 0))],
                 name=name, scratch=[pltpu.VMEM((s + pad, LANES), F32)])[0]


def _dn_conv_bwd(proj, conv_w, da, c0, nblk, l2, name):
    s = proj.shape[0]
    pad = _conv_pad(DN_CONV)
    rows = min(SHORT_CONV_TILE, s)

    def body(x_ref, w_ref, da_ref, dx_ref, dw_ref, xp, yp):
        xp[0:pad, :] = jnp.zeros((pad, LANES), F32)
        xp[pad:, :] = x_ref[...]
        yp[s:, :] = jnp.zeros((pad, LANES), F32)
        w = w_ref[...]

        def tile(i, dw):
            r0, acc, views = _conv_tile(xp, w, i, rows, DN_CONV)
            _, vjp = jax.vjp(functools.partial(_silu_l2, l2=l2), acc)
            (dxc,) = vjp(da_ref[pl.ds(r0, rows), :])
            yp[pl.ds(r0, rows), :] = dxc
            return dw + _tap_sums(dxc, views, DN_CONV)

        dw_ref[...] = lax.fori_loop(0, s // rows, tile, jnp.zeros((DN_CONV, LANES), F32))

        def tile2(i, c):
            r0, acc = _conv_back_tile(yp, w, i, rows, DN_CONV)
            dx_ref[pl.ds(r0, rows), :] = acc.astype(dx_ref.dtype)
            return c

        lax.fori_loop(0, s // rows, tile2, 0)

    return _call(body, grid=(nblk,),
                 ins=[(proj, (s, LANES), lambda c: (0, c0 + c)), (conv_w, (DN_CONV, LANES), lambda c: (0, c0 + c)),
                      (da, (None, s, LANES), lambda c: (c, 0, 0))],
                 outs=[((s, nblk * LANES), MXU_DTYPE, (s, LANES), lambda c: (0, c)),
                       ((DN_CONV, nblk * LANES), F32, (DN_CONV, LANES), lambda c: (0, c))],
                 name=name, scratch=[pltpu.VMEM((s + pad, LANES), F32), pltpu.VMEM((s + pad, LANES), F32)])


def _cf_conv_fwd(vg, dw_w, dw_b):
    s, c2 = vg.shape
    ch = c2 // 2
    nblk = ch // LANES
    taps = dw_w.shape[0]
    pad = _conv_pad(taps)
    rows = min(CONV_TILE, s)

    def body(v_ref, g_ref, w_ref, b_ref, o_ref, xp):
        xp[0:pad, :] = jnp.zeros((pad, LANES), F32)
        xp[pad:, :] = v_ref[...] * jax.nn.sigmoid(g_ref[...])
        w = w_ref[...]
        bias = b_ref[...]

        def tile(i, c):
            r0, acc, _ = _conv_tile(xp, w, i, rows, taps)
            o_ref[pl.ds(r0, rows), :] = acc + bias
            return c

        lax.fori_loop(0, s // rows, tile, 0)

    return _call(body, grid=(nblk,),
                 ins=[(vg, (s, LANES), lambda c: (0, c)), (vg, (s, LANES), lambda c: (0, nblk + c)),
                      (dw_w, (taps, LANES), lambda c: (0, c)), (dw_b, (1, LANES), lambda c: (0, c))],
                 outs=[((s, ch), F32, (s, LANES), lambda c: (0, c))],
                 name="cf_conv_fwd", scratch=[pltpu.VMEM((s + pad, LANES), F32)])[0]


def _cf_conv_bwd(vg, dw_w, du):
    s, c2 = vg.shape
    ch = c2 // 2
    nblk = ch // LANES
    taps = dw_w.shape[0]
    pad = _conv_pad(taps)
    rows = min(CONV_TILE, s)

    def body(v_ref, g_ref, w_ref, du_ref, dv_ref, dg_ref, dw_ref, db_ref, xp, yp):
        sig = jax.nn.sigmoid(g_ref[...])
        xp[0:pad, :] = jnp.zeros((pad, LANES), F32)
        xp[pad:, :] = v_ref[...] * sig
        yp[0:s, :] = du_ref[...]
        yp[s:, :] = jnp.zeros((pad, LANES), F32)
        w = w_ref[...]
        db_ref[...] = jnp.sum(du_ref[...], axis=0, keepdims=True)

        def tile(i, dw):
            r0, _, views = _conv_tile(xp, w, i, rows, taps)
            return dw + _tap_sums(du_ref[pl.ds(r0, rows), :], views, taps)

        dw_ref[...] = lax.fori_loop(0, s // rows, tile, jnp.zeros((taps, LANES), F32))

        def tile2(i, c):
            r0, du0 = _conv_back_tile(yp, w, i, rows, taps)
            val = v_ref[pl.ds(r0, rows), :]
            sg = jax.nn.sigmoid(g_ref[pl.ds(r0, rows), :])
            dv_ref[pl.ds(r0, rows), :] = (du0 * sg).astype(dv_ref.dtype)
            dg_ref[pl.ds(r0, rows), :] = (du0 * val * sg * (1.0 - sg)).astype(dg_ref.dtype)
            return c

        lax.fori_loop(0, s // rows, tile2, 0)

    return _call(body, grid=(nblk,),
                 ins=[(vg, (s, LANES), lambda c: (0, c)), (vg, (s, LANES), lambda c: (0, nblk + c)),
                      (dw_w, (taps, LANES), lambda c: (0, c)), (du, (s, LANES), lambda c: (0, c))],
                 outs=[((s, ch), MXU_DTYPE, (s, LANES), lambda c: (0, c)),
                       ((s, ch), MXU_DTYPE, (s, LANES), lambda c: (0, c)),
                       ((taps, ch), F32, (taps, LANES), lambda c: (0, c)),
                       ((1, ch), F32, (1, LANES), lambda c: (0, c))],
                 name="cf_conv_bwd", scratch=[pltpu.VMEM((s + pad, LANES), F32), pltpu.VMEM((s + pad, LANES), F32)])


def _masks():
    r = lax.broadcasted_iota(jnp.int32, (CHUNK, CHUNK), 0)
    c = lax.broadcasted_iota(jnp.int32, (CHUNK, CHUNK), 1)
    return r >= c, r > c, r <= c


def _chunk_decay(g):
    causal, _, upper = _masks()
    gb = jnp.broadcast_to(g, (CHUNK, CHUNK))
    gam_r = _dot01(jnp.where(causal, 1.0, 0.0), gb)
    gam_s = _dot01(jnp.ones((CHUNK, CHUNK), F32), jnp.where(upper, gb, 0.0))
    dm = jnp.where(causal, jnp.exp(jnp.where(causal, gam_r - gam_s, 0.0)), 0.0)
    return gam_r[:, 0:1], dm


def _chunk_scores(q, k, beta, dm):
    _, strict, _ = _masks()
    both = _mdot(jnp.concatenate([k * beta, q * (HEAD_DIM ** -0.5)], axis=0), k, NT)
    return jnp.where(strict, both[:CHUNK] * dm, 0.0), both[CHUNK:] * dm


def _lockstep(gens):
    results = [None] * len(gens)
    alive = list(range(len(gens)))
    while alive:
        for i in list(alive):
            try:
                next(gens[i])
            except StopIteration as stop:
                results[i] = stop.value
                alive.remove(i)
    return results


def _chunk_prep_bwd(q, k, v, beta, gam, t, du, dw, daqk, dqd, dkd, dgl):
    causal, strict, _ = _masks()
    r = lax.broadcasted_iota(jnp.int32, (CHUNK, CHUNK), 0)
    c = lax.broadcasted_iota(jnp.int32, (CHUNK, CHUNK), 1)
    scale = HEAD_DIM ** -0.5
    eg = jnp.exp(gam)
    gam_last = gam[CHUNK - 1:CHUNK, :]
    rr = jnp.exp(gam_last - gam)
    kb = k * beta
    qs = q * scale
    vb = v * beta
    kbe = kb * eg
    gam_b = jnp.broadcast_to(gam, (CHUNK, CHUNK))
    gam_s = _dot01(jnp.ones((CHUNK, CHUNK), F32), jnp.where(r == c, gam_b, 0.0))
    both = _mdot(jnp.concatenate([kb, qs], axis=0), k, NT)
    duw = jnp.concatenate([du, dw], axis=1)
    dt = _mdot(duw, jnp.concatenate([vb, kbe], axis=1), NT)
    dvk = _mdot(t, duw, TN)
    yield
    dm = jnp.where(causal, jnp.exp(jnp.where(causal, gam_b - gam_s, 0.0)), 0.0)
    a = jnp.where(strict, both[:CHUNK] * dm, 0.0)
    aqk = both[CHUNK:] * dm
    dvb, dkbe = dvk[:, :HEAD_DIM], dvk[:, HEAD_DIM:]
    x = _mdot(t, dt, TN)
    yield
    da = jnp.where(strict, -_mdot(x, t, NT), 0.0)
    yield
    dkk = da * dm
    dqk = daqk * dm
    ddiff = da * a + daqk * aqk
    dboth = jnp.concatenate([dkk, dqk], axis=0)
    dkq = _mdot(dboth, k)
    dk_mm = _mdot(dboth, jnp.concatenate([kb, qs], axis=0), TN)
    colsum = _dot01(ddiff, jnp.ones((CHUNK, LANES), F32), TN, mask_first=False)[:, 0:1]
    yield
    dkb = dkq[:CHUNK] + dkbe * eg
    dk = dk_mm + dkb * beta + dkd * rr
    dq = (dkq[CHUNK:] + dqd * eg) * scale
    dbeta = jnp.sum(dkb * k, axis=-1, keepdims=True) + jnp.sum(dvb * v, axis=-1, keepdims=True)
    dv = dvb * beta
    deg = jnp.sum(dkbe * kb, axis=-1, keepdims=True) + jnp.sum(dqd * qs, axis=-1, keepdims=True)
    drr = jnp.sum(dkd * k, axis=-1, keepdims=True)
    dgam = deg * eg - drr * rr + jnp.sum(ddiff, axis=-1, keepdims=True) - colsum
    dgam_last = jnp.sum(drr * rr, axis=0, keepdims=True) + dgl[0:1, :] * jnp.exp(gam_last)
    row = lax.broadcasted_iota(jnp.int32, (CHUNK, 1), 0)
    dgam = dgam + jnp.where(row == CHUNK - 1, dgam_last, 0.0)
    dg = _dot01(jnp.where(causal, 1.0, 0.0), jnp.broadcast_to(dgam, (CHUNK, LANES)), TN)[:, 0:1]
    return dq, dk, dv, dbeta, dg


def _prep_group(s):
    nch = s // CHUNK
    return next(c for c in (16, 8, 4, 2, 1) if nch % c == 0)


def _tri_solve_lanes(a_l):
    n = a_l.shape[1]
    group = 8

    def body(a_ref, t_ref):
        t_ref[...] = jnp.zeros_like(t_ref)
        col = lax.broadcasted_iota(jnp.int32, (CHUNK, n), 0)

        def row(r, carry):
            r0 = pl.multiple_of(r * CHUNK, CHUNK)

            def inner(sg, acc):
                a8 = a_ref[pl.ds(r0 + pl.multiple_of(sg * group, group), group), :]
                for j in range(group):
                    t0 = pl.multiple_of((sg * group + j) * CHUNK, CHUNK)
                    acc = acc + a8[j:j + 1, :] * t_ref[pl.ds(t0, CHUNK), :]
                return acc

            acc = lax.fori_loop(0, (r + group - 1) // group, inner, jnp.zeros((CHUNK, n), F32))
            t_ref[pl.ds(r0, CHUNK), :] = jnp.where(col == r, 1.0, 0.0) - acc
            return carry

        lax.fori_loop(0, CHUNK, row, 0)

    return pl.pallas_call(body, out_shape=jax.ShapeDtypeStruct(a_l.shape, F32), name="dn_tri_solve")(a_l)


def _head_cols(bg, hh, heads):
    lane = lax.broadcasted_iota(jnp.int32, bg.shape, 1)
    beta = jnp.sum(jnp.where(lane == hh, bg, 0.0), axis=-1, keepdims=True)
    g = jnp.sum(jnp.where(lane == heads + hh, bg, 0.0), axis=-1, keepdims=True)
    return beta, g


def _dn_prep(q, k, v, bg):
    h, s, _ = q.shape
    cb = _prep_group(s)
    rb = cb * CHUNK
    big = lambda x: (x, (None, rb, HEAD_DIM), lambda n, hh: (hh, n, 0))
    sq = lambda x: (x, (None, rb, CHUNK), lambda n, hh: (hh, n, 0))
    col = lambda x: (x, (None, rb, 1), lambda n, hh: (hh, n, 0))
    tok = (bg, (rb, LANES), lambda n, hh: (n, 0))
    o_big = ((h, s, HEAD_DIM), F32, (None, rb, HEAD_DIM), lambda n, hh: (hh, n, 0))
    o_sq = ((h, s, CHUNK), F32, (None, rb, CHUNK), lambda n, hh: (hh, n, 0))
    o_col = ((h, s, 1), F32, (None, rb, 1), lambda n, hh: (hh, n, 0))

    def scores(q_ref, k_ref, bg_ref, a_ref, aqk_ref, gam_ref):
        beta, g = _head_cols(bg_ref[...], pl.program_id(1), h)
        for i in range(cb):
            sl = slice(i * CHUNK, (i + 1) * CHUNK)
            gam, dm = _chunk_decay(g[sl])
            a_ref[sl, :], aqk_ref[sl, :] = _chunk_scores(q_ref[sl, :], k_ref[sl, :], beta[sl], dm)
            gam_ref[sl, :] = gam

    a, aqk, gam = _call(scores, grid=(s // rb, h), ins=[big(q), big(k), tok], outs=[o_sq, o_sq, o_col],
                        name="dn_scores")
    n_prob = h * (s // CHUNK)
    t_l = _tri_solve_lanes(jnp.transpose(a.reshape(n_prob, CHUNK * CHUNK)))
    t = jnp.transpose(t_l).reshape(h, s, CHUNK)

    def wy(k_ref, v_ref, bg_ref, gam_ref, t_ref, u_ref, w_ref):
        beta, _ = _head_cols(bg_ref[...], pl.program_id(1), h)
        for i in range(cb):
            sl = slice(i * CHUNK, (i + 1) * CHUNK)
            kb = k_ref[sl, :] * beta[sl]
            rhs = jnp.concatenate([v_ref[sl, :] * beta[sl], kb * jnp.exp(gam_ref[sl, :])], axis=1)
            uw = _mdot(t_ref[sl, :], rhs)
            u_ref[sl, :] = uw[:, :HEAD_DIM]
            w_ref[sl, :] = uw[:, HEAD_DIM:]

    u, w = _call(wy, grid=(s // rb, h), ins=[big(k), big(v), tok, col(gam), sq(t)], outs=[o_big, o_big],
                 name="dn_wy")
    return u, w, aqk, t, gam


def _dn_prep_bwd(q, k, v, bg, gam, t, du, dw, daqk, dqd, dkd, dgl):
    h, s, _ = q.shape
    cb = _prep_group(s)
    rb = cb * CHUNK

    def body(q_ref, k_ref, v_ref, bg_ref, g_ref, t_ref, du_ref, dw_ref, da_ref, dqd_ref, dkd_ref, dgl_ref,
             dq_ref, dk_ref, dv_ref, dbg_ref):
        hh = pl.program_id(1)
        beta, _ = _head_cols(bg_ref[...], hh, h)
        slices = [slice(i * CHUNK, (i + 1) * CHUNK) for i in range(cb)]
        results = _lockstep([_chunk_prep_bwd(
            q_ref[sl, :], k_ref[sl, :], v_ref[sl, :], beta[sl], g_ref[sl, :], t_ref[sl, :],
            du_ref[sl, :], dw_ref[sl, :], da_ref[sl, :], dqd_ref[sl, :], dkd_ref[sl, :], dgl_ref[sl, :])
            for sl in slices])

        @pl.when(hh == 0)
        def _():
            dbg_ref[...] = jnp.zeros_like(dbg_ref)

        lane = lax.broadcasted_iota(jnp.int32, (CHUNK, LANES), 1)
        for sl, (dq, dk, dv, dbeta, dg) in zip(slices, results):
            dq_ref[sl, :] = dq
            dk_ref[sl, :] = dk
            dv_ref[sl, :] = dv
            dbg_ref[sl, :] += jnp.where(lane == hh, dbeta, 0.0) + jnp.where(lane == h + hh, dg, 0.0)

    big = lambda x: (x, (None, rb, HEAD_DIM), lambda n, hh: (hh, n, 0))
    sq = lambda x: (x, (None, rb, CHUNK), lambda n, hh: (hh, n, 0))
    col = lambda x: (x, (None, rb, 1), lambda n, hh: (hh, n, 0))
    tok = (bg, (rb, LANES), lambda n, hh: (n, 0))
    o_big = ((h, s, HEAD_DIM), F32, (None, rb, HEAD_DIM), lambda n, hh: (hh, n, 0))
    return _call(body, grid=(s // rb, h),
                 ins=[big(q), big(k), big(v), tok, col(gam), sq(t), big(du), big(dw), sq(daqk), big(dqd), big(dkd),
                      col(dgl)],
                 outs=[o_big, o_big, o_big, ((s, LANES), F32, (rb, LANES), lambda n, hh: (n, 0))], name="dn_prep_bwd")


def _chunk_scaled(q, k, gam):
    gam_last = gam[CHUNK - 1:CHUNK, :]
    q_dec = q * (HEAD_DIM ** -0.5) * jnp.exp(gam)
    k_dec = k * jnp.exp(gam_last - gam)
    return q_dec, k_dec, jnp.exp(gam_last)


def _scan_group(s):
    return 2 if (s // CHUNK) % 2 == 0 else 1


def _dn_scan(q, k, u, w, aqk, gam):
    h, s, _ = q.shape
    nch = s // CHUNK
    sg = _scan_group(s)
    rb = sg * CHUNK

    def body(q_ref, k_ref, u_ref, w_ref, a_ref, gam_ref, o_ref, st_ref, state):
        @pl.when(pl.program_id(0) == 0)
        def _():
            state[...] = jnp.zeros_like(state)

        def head(hh, c):
            sl = slice(c * CHUNK, (c + 1) * CHUNK)
            s0 = state[hh]
            st_ref[c, hh] = s0
            q_dec, k_dec, gl = _chunk_scaled(q_ref[hh, sl, :], k_ref[hh, sl, :], gam_ref[hh, sl, :])
            both = _mdot(jnp.concatenate([w_ref[hh, sl, :], q_dec], axis=0), s0)
            yield
            v_new = u_ref[hh, sl, :] - both[:CHUNK]
            o_ref[sl, hh * HEAD_DIM:(hh + 1) * HEAD_DIM] = both[CHUNK:] + _mdot(a_ref[hh, sl, :], v_new)
            state[hh] = s0 * gl + _mdot(k_dec, v_new, TN)

        for c in range(sg):
            _lockstep([head(hh, c) for hh in range(h)])

    big = lambda x: (x, (h, rb, HEAD_DIM), lambda n: (0, n, 0))
    return _call(body, grid=(nch // sg,),
                 ins=[big(q), big(k), big(u), big(w), (aqk, (h, rb, CHUNK), lambda n: (0, n, 0)),
                      (gam, (h, rb, 1), lambda n: (0, n, 0))],
                 outs=[((s, h * HEAD_DIM), F32, (rb, h * HEAD_DIM), lambda n: (n, 0)),
                       ((nch, h, HEAD_DIM, HEAD_DIM), F32, (sg, h, HEAD_DIM, HEAD_DIM), lambda n: (n, 0, 0, 0))],
                 name="dn_scan", scratch=[pltpu.VMEM((h, HEAD_DIM, HEAD_DIM), F32)])


def _dn_scan_bwd(q, k, u, w, aqk, gam, states, do):
    h, s, _ = q.shape
    nch = s // CHUNK
    sg = _scan_group(s)
    rb = sg * CHUNK
    ngr = nch // sg

    def body(q_ref, k_ref, u_ref, w_ref, a_ref, gam_ref, st_ref, do_ref,
             du_ref, dw_ref, da_ref, dqd_ref, dkd_ref, dgl_ref, dstate):
        @pl.when(pl.program_id(0) == 0)
        def _():
            dstate[...] = jnp.zeros_like(dstate)

        def head(hh, c):
            sl = slice(c * CHUNK, (c + 1) * CHUNK)
            s0 = st_ref[c, hh]
            ds = dstate[hh]
            doh = do_ref[sl, hh * HEAD_DIM:(hh + 1) * HEAD_DIM]
            wv = w_ref[hh, sl, :]
            q_dec, k_dec, gl = _chunk_scaled(q_ref[hh, sl, :], k_ref[hh, sl, :], gam_ref[hh, sl, :])
            ws = _mdot(wv, s0)
            dv_new = _mdot(a_ref[hh, sl, :], doh, TN) + _mdot(k_dec, ds)
            dqd_ref[hh, sl, :] = _mdot(doh, s0, NT)
            qdo = _mdot(q_dec, doh, TN)
            tot = jnp.sum(jnp.sum(s0 * ds, axis=-1, keepdims=True), axis=0, keepdims=True)
            dgl_ref[hh, sl, :] = jnp.broadcast_to(tot, (CHUNK, 1))
            yield
            v_new = u_ref[hh, sl, :] - ws
            du_ref[hh, sl, :] = dv_new
            dw_ref[hh, sl, :] = -_mdot(dv_new, s0, NT)
            da_ref[hh, sl, :] = _mdot(doh, v_new, NT)
            dkd_ref[hh, sl, :] = _mdot(v_new, ds, NT)
            dstate[hh] = ds * gl + qdo - _mdot(wv, dv_new, TN)

        for c in range(sg - 1, -1, -1):
            _lockstep([head(hh, c) for hh in range(h)])

    rev = lambda n: (0, ngr - 1 - n, 0)
    big = lambda x: (x, (h, rb, HEAD_DIM), rev)
    o_big = ((h, s, HEAD_DIM), F32, (h, rb, HEAD_DIM), rev)
    return _call(body, grid=(ngr,),
                 ins=[big(q), big(k), big(u), big(w), (aqk, (h, rb, CHUNK), rev), (gam, (h, rb, 1), rev),
                      (states, (sg, h, HEAD_DIM, HEAD_DIM), lambda n: (ngr - 1 - n, 0, 0, 0)),
                      (do, (rb, h * HEAD_DIM), lambda n: (ngr - 1 - n, 0))],
                 outs=[o_big, o_big, ((h, s, CHUNK), F32, (h, rb, CHUNK), rev), o_big, o_big,
                       ((h, s, 1), F32, (h, rb, 1), rev)],
                 name="dn_scan_bwd", scratch=[pltpu.VMEM((h, HEAD_DIM, HEAD_DIM), F32)])


def _gates(x, a_log, dt_b, h):
    lane = lax.broadcasted_iota(jnp.int32, x.shape, 1)
    return jnp.where(lane < h, jax.nn.sigmoid(x), -jnp.exp(a_log) * jax.nn.softplus(x + dt_b))


def _head_out(oh, zh, nw):
    on = oh * lax.rsqrt(jnp.mean(oh * oh, axis=-1, keepdims=True) + RMS_EPS) * nw
    return on * jax.nn.silu(zh)


def _pad_lanes(x, lo):
    return jnp.zeros((1, LANES), F32).at[0, lo:lo + x.shape[0]].set(x)


def _deltanet_fwd(hin, get_w_in, conv_w, a_log, dt_bias, norm_w, get_w_out):
    h = a_log.shape[0]
    hw = h * HEAD_DIM
    w_in = get_w_in(hin)
    proj = _mm_nn(hin, w_in, F32, "dn_proj")
    q = _dn_conv_fwd(proj, conv_w, 0, h, True, "dn_conv_q")
    k = _dn_conv_fwd(proj, conv_w, h, h, True, "dn_conv_k")
    v = _dn_conv_fwd(proj, conv_w, 2 * h, h, False, "dn_conv_v")
    alp, dtp = _pad_lanes(a_log, h), _pad_lanes(dt_bias, h)

    def gates_fn(x, al, db):
        return (_gates(x, al, db, h),), ()

    (bg,), _ = _rowmap(gates_fn, [(proj, LANES, 4 * h)], [alp, dtp], [(LANES, F32)], [], "dn_gates")
    u, w, aqk, t, gam = _dn_prep(q, k, v, bg)
    o, states = _dn_scan(q, k, u, w, aqk, gam)
    nw = norm_w[None, :]

    def out_fn(o, z, nw):
        parts = [_head_out(o[:, i * HEAD_DIM:(i + 1) * HEAD_DIM], z[:, i * HEAD_DIM:(i + 1) * HEAD_DIM], nw)
                 for i in range(h)]
        return (jnp.concatenate(parts, axis=-1),), ()

    (og,), _ = _rowmap(out_fn, [o, (proj, hw, 3)], [nw], [(hw, MXU_DTYPE)], [], "dn_out")
    w_out = get_w_out(og)
    y = _mm_nn(og, w_out, F32, "dn_y")
    return y, (hin, proj, q, k, v, bg, u, w, aqk, t, gam, states, o, og, alp, dtp, nw, w_in, w_out)


def _deltanet_bwd(res, dy, conv_w):
    hin, proj, q, k, v, bg, u, w, aqk, t, gam, states, o, og, alp, dtp, nw, w_in, w_out = res
    h = q.shape[0]
    hw = h * HEAD_DIM
    s = hin.shape[0]
    d_w_out = _mm_tn(og, dy, MXU_DTYPE, "dn_dwout")
    dog = _mm_nt(dy, w_out, F32, "dn_dog")

    def out_bwd(o, z, dog, nw):
        dos, dzs = [], []
        dn = jnp.zeros((1, HEAD_DIM), F32)
        for i in range(h):
            sl = slice(i * HEAD_DIM, (i + 1) * HEAD_DIM)
            _, vjp = jax.vjp(_head_out, o[:, sl], z[:, sl], nw)
            a, b, c = vjp(dog[:, sl])
            dos.append(a)
            dzs.append(b)
            dn = dn + c
        return (jnp.concatenate(dos, axis=-1), jnp.concatenate(dzs, axis=-1)), (dn,)

    (do, dz), (d_norm_w,) = _rowmap(out_bwd, [o, (proj, hw, 3), dog], [nw], [(hw, F32), (hw, MXU_DTYPE)],
                                    [(1, HEAD_DIM)], "dn_out_bwd")
    du, dw, daqk, dqd, dkd, dgl = _dn_scan_bwd(q, k, u, w, aqk, gam, states, do)
    dq, dk, dv, dbg = _dn_prep_bwd(q, k, v, bg, gam, t, du, dw, daqk, dqd, dkd, dgl)
    dpq, dwq = _dn_conv_bwd(proj, conv_w, dq, 0, h, True, "dn_conv_q_bwd")
    dpk, dwk = _dn_conv_bwd(proj, conv_w, dk, h, h, True, "dn_conv_k_bwd")
    dpv, dwv = _dn_conv_bwd(proj, conv_w, dv, 2 * h, h, False, "dn_conv_v_bwd")

    def gates_bwd(x, dbg, al, db):
        _, vjp = jax.vjp(functools.partial(_gates, h=h), x, al, db)
        gx, gal, gdb = vjp(dbg)
        return (gx,), (gal, gdb)

    (dba,), (d_alp, d_dtp) = _rowmap(gates_bwd, [(proj, LANES, 4 * h), dbg], [alp, dtp], [(LANES, MXU_DTYPE)],
                                     [(1, LANES), (1, LANES)], "dn_gates_bwd")
    dproj = jnp.concatenate([dpq, dpk, dpv, dz, dba], axis=1)
    d_w_in = _mm_tn(hin, dproj, MXU_DTYPE, "dn_dwin")
    dh = _mm_nt(dproj, w_in, F32, "dn_dh")
    d_conv_w = jnp.concatenate([dwq, dwk, dwv], axis=1)
    return dh, dict(w_in=d_w_in, w_out=d_w_out, conv_w=d_conv_w, a_log=d_alp[0, h:2 * h], dt_bias=d_dtp[0, h:2 * h],
                    norm_w=d_norm_w[0])


def _ln_silu(u, g, b):
    return jax.nn.silu(_ln(u, g, b))


def _conformer_fwd(hin, get_w_in, dw_w, dw_b, ln_g, ln_b, get_w_out):
    w_in = get_w_in(hin)
    vg = _mm_nn(hin, w_in, F32, "cf_vg")
    u1 = _cf_conv_fwd(vg, dw_w, dw_b)
    ch = u1.shape[1]

    def fn(u, g, b):
        return (_ln_silu(u, g, b),), ()

    (u2,), _ = _rowmap(fn, [u1], [ln_g, ln_b], [(ch, MXU_DTYPE)], [], "cf_ln")
    w_out = get_w_out(u2)
    y = _mm_nn(u2, w_out, F32, "cf_y")
    return y, (hin, vg, u1, u2, w_in, w_out)


def _conformer_bwd(res, dy, dw_w, ln_g, ln_b):
    hin, vg, u1, u2, w_in, w_out = res
    ch = u1.shape[1]
    d_w_out = _mm_tn(u2, dy, MXU_DTYPE, "cf_dwout")
    du2 = _mm_nt(dy, w_out, F32, "cf_du2")

    def fn(u, du2, g, b):
        _, vjp = jax.vjp(_ln_silu, u, g, b)
        gu, gg, gb = vjp(du2)
        return (gu,), (gg, gb)

    (du1,), (d_ln_g, d_ln_b) = _rowmap(fn, [u1, du2], [ln_g, ln_b], [(ch, F32)], [(1, ch), (1, ch)], "cf_ln_bwd")
    dval, dgate, d_dw_w, d_dw_b = _cf_conv_bwd(vg, dw_w, du1)
    dvg = jnp.concatenate([dval, dgate], axis=1)
    d_w_in = _mm_tn(hin, dvg, MXU_DTYPE, "cf_dwin", split_cols=True)
    dh = _mm_nt(dvg, w_in, F32, "cf_dh")
    return dh, dict(w_in=d_w_in, w_out=d_w_out, dw_w=d_dw_w, dw_b=d_dw_b[0], ln_g=d_ln_g[0], ln_b=d_ln_b[0])


def _mlp_fwd(hin, get_w1, get_w2):
    w1 = get_w1(hin)
    r = _mm_nn(hin, w1, MXU_DTYPE, "ff_a", relu2=True)
    w2 = get_w2(r)
    m = _mm_nn(r, w2, F32, "ff_m")
    return m, (hin, r, w1, w2)


def _mlp_bwd(res, dm):
    hin, r, w1, w2 = res
    d_w2 = _mm_tn(r, dm, MXU_DTYPE, "ff_dw2")
    da = _mm_nt(dm, w2, MXU_DTYPE, "ff_da", relu2_sq=r)
    d_w1 = _mm_tn(hin, da, MXU_DTYPE, "ff_dw1", split_cols=True)
    dh = _mm_nt(da, w1, F32, "ff_dh")
    return dh, d_w1, d_w2


def _ada_fwd(c_all, ada_w):
    depth, d, nl = ada_w.shape
    tn = _tile(nl, 256)

    def body(c_ref, w_ref, o_ref, cond_ref):
        cond = jax.nn.silu(c_ref[...]).astype(MXU_DTYPE)
        cond_ref[...] = cond
        o_ref[...] = lax.dot_general(cond, w_ref[...].astype(MXU_DTYPE), (NN, ((), ())), preferred_element_type=F32)

    return _call(body, grid=(depth, nl // tn),
                 ins=[(c_all, c_all.shape, lambda l, j: (0, 0)), (ada_w, (None, d, tn), lambda l, j: (l, 0, j))],
                 outs=[((depth, N_DEV, nl), F32, (None, N_DEV, tn), lambda l, j: (l, 0, j)),
                       (c_all.shape, MXU_DTYPE, c_all.shape, lambda l, j: (0, 0))],
                 name="ada_fwd")


def _ada_bwd(cond_all, dmod_cols):
    depth, _, nl = dmod_cols.shape
    d = cond_all.shape[1]
    tn = _tile(nl, 256)

    def body(c_ref, g_ref, o_ref):
        o_ref[...] = lax.dot_general(c_ref[...], g_ref[...].astype(MXU_DTYPE), (TN, ((), ())),
                                     preferred_element_type=F32)

    return _call(body, grid=(depth, nl // tn),
                 ins=[(cond_all, cond_all.shape, lambda l, j: (0, 0)), (dmod_cols, (None, N_DEV, tn), lambda l, j: (l, 0, j))],
                 outs=[((depth, d, nl), F32, (None, d, tn), lambda l, j: (l, 0, j))], name="ada_bwd")[0]


def _peers():
    x, y, c = lax.axis_index("x"), lax.axis_index("y"), lax.axis_index("c")
    peers = []
    for k in range(1, N_DEV):
        px = 1 - x if k & 4 else x
        py = 1 - y if k & 2 else y
        pc = 1 - c if k & 1 else c
        peers.append(((px, py, pc), 4 * px + 2 * py + pc))
    return 4 * x + 2 * y + c, peers


_HBM = pl.BlockSpec(memory_space=pltpu.HBM)
_SEM = pl.BlockSpec(memory_space=pltpu.SEMAPHORE)
_ANY = pl.BlockSpec(memory_space=pl.ANY)
_EFFECT = pltpu.SideEffectType.DATAFLOW_SIDE_EFFECTING


def _xfer_start(srcs, lands, scatter, after, name):
    nt = len(srcs)

    def body(*refs):
        src, land = refs[:nt], refs[nt:2 * nt]
        sems = refs[2 * nt + 1:4 * nt + 1]
        token = refs[-1]
        me, peers = _peers()
        for t in range(nt):
            for k, (pid, plin) in enumerate(peers):
                pltpu.make_async_remote_copy(
                    src_ref=src[t].at[plin] if scatter else src[t], dst_ref=land[t].at[me],
                    send_sem=sems[2 * t].at[k], recv_sem=sems[2 * t + 1].at[k],
                    device_id=pid, device_id_type=pl.DeviceIdType.MESH).start()
        token[...] = jnp.zeros_like(token)

    out_shape = [pltpu.SemaphoreType.DMA((N_DEV - 1,)) for _ in range(2 * nt)]
    out_shape += [pltpu.HBM(a.shape, a.dtype) for a in lands]
    out_shape += [jax.ShapeDtypeStruct((8, LANES), F32)]
    srcs = [pltpu.with_memory_space_constraint(a, pltpu.HBM) for a in srcs]
    res = pl.pallas_call(
        body, name=name, out_shape=out_shape,
        in_specs=[_HBM] * (2 * nt) + [_ANY],
        out_specs=[_SEM] * (2 * nt) + [_HBM] * nt + [pl.BlockSpec(memory_space=pltpu.VMEM)],
        input_output_aliases={nt + i: 2 * nt + i for i in range(nt)},
        compiler_params=pltpu.CompilerParams(has_side_effects=_EFFECT),
    )(*srcs, *[pltpu.with_memory_space_constraint(a, pltpu.HBM) for a in lands], after)
    sems, thru = res[:2 * nt], res[2 * nt:3 * nt]
    return [(sems[2 * t], sems[2 * t + 1], srcs[t], thru[t]) for t in range(nt)], res[-1]


def _xfer_wait(handle, scatter, after, name):
    send, recv, src, land = handle

    def body(src_ref, land_ref, send_sem, recv_sem, after_ref, land_out):
        _, peers = _peers()
        for k, (pid, plin) in enumerate(peers):
            cp = pltpu.make_async_remote_copy(
                src_ref=src_ref.at[plin] if scatter else src_ref, dst_ref=land_ref.at[plin],
                send_sem=send_sem.at[k], recv_sem=recv_sem.at[k],
                device_id=pid, device_id_type=pl.DeviceIdType.MESH)
            cp.wait_send()
            cp.wait_recv()

    return pl.pallas_call(
        body, name=name, out_shape=pltpu.HBM(land.shape, land.dtype),
        in_specs=(_HBM, _HBM, _SEM, _SEM, _ANY), out_specs=_HBM, input_output_aliases={1: 0},
        compiler_params=pltpu.CompilerParams(has_side_effects=_EFFECT),
    )(src, land, send, recv, after)


def _landing(x, me):
    return lax.dynamic_update_slice(lax.empty((N_DEV,) + x.shape, x.dtype), x[None], (me,) + (0,) * x.ndim)


def _chip_peers():
    x, y, c = lax.axis_index("x"), lax.axis_index("y"), lax.axis_index("c")
    lin = lambda px, py, pc: 4 * px + 2 * py + pc
    sibling = ((x, y, 1 - c), lin(x, y, 1 - c))
    chips = [((1 - x, y, c), lin(1 - x, y, c)), ((x, 1 - y, c), lin(x, 1 - y, c)),
             ((1 - x, 1 - y, c), lin(1 - x, 1 - y, c))]
    return lin(x, y, c), sibling, chips


N_CHIPS_OTHER = 3


def _gather2_start(srcs, lands, after, name):
    nt = len(srcs)

    def body(*refs):
        src, land = refs[:nt], refs[nt:2 * nt]
        sems = refs[2 * nt + 1:5 * nt + 1]
        token = refs[-1]
        me, sibling, chips = _chip_peers()
        for t in range(nt):
            send, recv_ici, recv_sib = sems[3 * t], sems[3 * t + 1], sems[3 * t + 2]
            pltpu.make_async_remote_copy(src_ref=src[t], dst_ref=land[t].at[me], send_sem=send.at[0],
                                         recv_sem=recv_sib.at[0], device_id=sibling[0],
                                         device_id_type=pl.DeviceIdType.MESH).start()
            for j, (pid, _) in enumerate(chips):
                pltpu.make_async_remote_copy(src_ref=src[t], dst_ref=land[t].at[me], send_sem=send.at[1 + j],
                                             recv_sem=recv_ici.at[j], device_id=pid,
                                             device_id_type=pl.DeviceIdType.MESH).start()
        token[...] = jnp.zeros_like(token)

    out_shape = []
    for _ in range(nt):
        out_shape += [pltpu.SemaphoreType.DMA((1 + N_CHIPS_OTHER,)), pltpu.SemaphoreType.DMA((N_CHIPS_OTHER,)),
                      pltpu.SemaphoreType.DMA((1,))]
    out_shape += [pltpu.HBM(a.shape, a.dtype) for a in list(srcs) + list(lands)]
    out_shape += [jax.ShapeDtypeStruct((8, LANES), F32)]
    res = pl.pallas_call(
        body, name=name, out_shape=out_shape,
        in_specs=[_HBM] * (2 * nt) + [_ANY],
        out_specs=[_SEM] * (3 * nt) + [_HBM] * (2 * nt) + [pl.BlockSpec(memory_space=pltpu.VMEM)],
        input_output_aliases={i: 3 * nt + i for i in range(2 * nt)},
        compiler_params=pltpu.CompilerParams(has_side_effects=_EFFECT),
    )(*[pltpu.with_memory_space_constraint(a, pltpu.HBM) for a in list(srcs) + list(lands)], after)
    sems, thru = res[:3 * nt], res[3 * nt:5 * nt]
    return [(sems[3 * t], sems[3 * t + 1], sems[3 * t + 2], thru[t], thru[nt + t]) for t in range(nt)], res[-1]


def _gather2_relay(handles, after, name):
    nt = len(handles)

    def body(*refs):
        src, land = refs[:nt], refs[nt:2 * nt]
        send1, recv_ici = refs[2 * nt:3 * nt], refs[3 * nt:4 * nt]
        outs = refs[4 * nt + 1:]
        send2, recv2 = outs[:nt], outs[nt:2 * nt]
        me, sibling, chips = _chip_peers()
        for t in range(nt):
            pltpu.make_async_remote_copy(src_ref=src[t], dst_ref=land[t].at[me], send_sem=send1[t].at[0],
                                         recv_sem=recv_ici[t].at[0], device_id=sibling[0],
                                         device_id_type=pl.DeviceIdType.MESH).wait_send()
            for j, (pid, plin) in enumerate(chips):
                arrived = pltpu.make_async_remote_copy(src_ref=src[t], dst_ref=land[t].at[plin], send_sem=send1[t].at[1 + j],
                                                       recv_sem=recv_ici[t].at[j], device_id=pid,
                                                       device_id_type=pl.DeviceIdType.MESH)
                arrived.wait_send()
                arrived.wait_recv()
                pltpu.make_async_remote_copy(src_ref=land[t].at[plin], dst_ref=land[t].at[plin], send_sem=send2[t].at[j],
                                             recv_sem=recv2[t].at[j], device_id=sibling[0],
                                             device_id_type=pl.DeviceIdType.MESH).start()

    srcs = [h[3] for h in handles]
    lands = [h[4] for h in handles]
    out_shape = [pltpu.SemaphoreType.DMA((N_CHIPS_OTHER,)) for _ in range(2 * nt)]
    out_shape += [pltpu.HBM(a.shape, a.dtype) for a in srcs + lands]
    res = pl.pallas_call(
        body, name=name, out_shape=out_shape,
        in_specs=[_HBM] * (2 * nt) + [_SEM] * (2 * nt) + [_ANY],
        out_specs=[_SEM] * (2 * nt) + [_HBM] * (2 * nt),
        input_output_aliases={i: 2 * nt + i for i in range(2 * nt)},
        compiler_params=pltpu.CompilerParams(has_side_effects=_EFFECT),
    )(*srcs, *lands, *[h[0] for h in handles], *[h[1] for h in handles], after)
    return [(handles[t][2], res[t], res[nt + t], res[3 * nt + t]) for t in range(nt)]


def _gather2_wait(handle, after, name):
    recv_sib, send2, recv2, land = handle

    def body(land_ref, recv_sib_sem, send2_sem, recv2_sem, after_ref, land_out):
        me, sibling, chips = _chip_peers()
        pltpu.make_async_remote_copy(src_ref=land_ref.at[me], dst_ref=land_ref.at[sibling[1]], send_sem=send2_sem.at[0],
                                     recv_sem=recv_sib_sem.at[0], device_id=sibling[0],
                                     device_id_type=pl.DeviceIdType.MESH).wait_recv()
        for j, (pid, plin) in enumerate(chips):
            relayed = pltpu.make_async_remote_copy(src_ref=land_ref.at[plin], dst_ref=land_ref.at[plin], send_sem=send2_sem.at[j],
                                                   recv_sem=recv2_sem.at[j], device_id=sibling[0],
                                                   device_id_type=pl.DeviceIdType.MESH)
            relayed.wait_send()
            relayed.wait_recv()

    return pl.pallas_call(
        body, name=name, out_shape=pltpu.HBM(land.shape, land.dtype),
        in_specs=(_HBM, _SEM, _SEM, _SEM, _ANY), out_specs=_HBM, input_output_aliases={0: 0},
        compiler_params=pltpu.CompilerParams(has_side_effects=_EFFECT),
    )(land, recv_sib, send2, recv2, after)


def _exchange(arrs, scatter, name):
    nt = len(arrs)
    out_shape = [jax.ShapeDtypeStruct(a.shape if scatter else (N_DEV,) + a.shape, a.dtype) for a in arrs]

    def body(*refs):
        ins, outs = refs[:nt], refs[nt:2 * nt]
        send, recv, loc = refs[2 * nt:]
        me, peers = _peers()
        copies = []
        for t in range(nt):
            own = pltpu.make_async_copy(ins[t].at[me] if scatter else ins[t], outs[t].at[me], loc.at[t])
            own.start()
            copies.append(own)
            for k, (pid, plin) in enumerate(peers):
                cp = pltpu.make_async_remote_copy(
                    src_ref=ins[t].at[plin] if scatter else ins[t], dst_ref=outs[t].at[me],
                    send_sem=send.at[t, k], recv_sem=recv.at[t, k],
                    device_id=pid, device_id_type=pl.DeviceIdType.MESH)
                cp.start()
                copies.append(cp)
        for cp in copies:
            cp.wait()

    any_spec = pl.BlockSpec(memory_space=pl.ANY)
    return pl.pallas_call(
        body, out_shape=out_shape, in_specs=[any_spec] * nt, out_specs=[any_spec] * nt,
        scratch_shapes=[pltpu.SemaphoreType.DMA((nt, N_DEV - 1)), pltpu.SemaphoreType.DMA((nt, N_DEV - 1)),
                        pltpu.SemaphoreType.DMA((nt,))],
        name=name)(*arrs)


def _adamw_body(n_parts, stacked=True):
    def body(p_ref, w_ref, m_ref, v_ref, *rest):
        g_out, d_out, m_out, v_out = rest[-4:]
        part = (lambda i: p_ref[i]) if stacked else (lambda i: p_ref[i][...])
        g = part(0).astype(F32)
        for i in range(1, n_parts):
            g = g + part(i).astype(F32)
        m2 = ADAM_B1 * m_ref[...] + (1.0 - ADAM_B1) * g
        v2 = ADAM_B2 * v_ref[...] + (1.0 - ADAM_B2) * jnp.square(g)
        m_hat = m2 / (1.0 - ADAM_B1 ** ADAM_STEP)
        v_hat = v2 / (1.0 - ADAM_B2 ** ADAM_STEP)
        g_out[...] = g
        d_out[...] = -ADAM_LR * (m_hat / (jnp.sqrt(v_hat) + ADAM_EPS) + ADAM_WD * w_ref[...])
        m_out[...] = m2
        v_out[...] = v2

    return body


def _adamw_layer(own, land, me, w, m, v, layer, prev, name):
    _, r, c = own.shape
    tr = _tile(r, 256, 8)
    blk = pl.BlockSpec((None, tr, c), lambda i, me_ref: (layer, i, 0))
    share = lambda k: pl.BlockSpec((None, tr, c), lambda i, me_ref: (me_ref[0] ^ k, i, 0))
    in_specs = [share(k) for k in range(N_DEV)] + [blk, blk, blk]
    args = [own] + [land] * (N_DEV - 1) + [w, m, v]
    aliases = {}
    if prev is not None:
        in_specs += [_ANY] * 4
        args += list(prev)
        aliases = {1 + N_DEV + 3 + i: i for i in range(4)}

    def body(me_ref, *refs):
        token_ref = refs[-1]
        refs = (refs[:N_DEV],) + refs[N_DEV:-1]
        _adamw_body(N_DEV, stacked=False)(*refs)
        token_ref[...] = jnp.zeros(token_ref.shape, F32)

    token_blk = pl.BlockSpec((8, LANES), lambda i, me_ref: (0, 0))
    res = pl.pallas_call(
        body,
        grid_spec=pltpu.PrefetchScalarGridSpec(num_scalar_prefetch=1, grid=(r // tr,), in_specs=in_specs,
                                               out_specs=[blk] * 4 + [token_blk]),
        out_shape=[jax.ShapeDtypeStruct(w.shape, F32)] * 4 + [jax.ShapeDtypeStruct((8, LANES), F32)],
        input_output_aliases=aliases, name=name, compiler_params=_cparams(1))(me, *args)
    return res[:4], res[4]


def _adamw(parts, w, m, v, name):
    p, nl, r, c = parts.shape
    tr = _tile(r, 256, 8)
    body = _adamw_body(p)

    blk = (None, tr, c)
    imap = lambda l, i: (l, i, 0)
    out = ((nl, r, c), F32, blk, imap)
    return _call(body, grid=(nl, r // tr),
                 ins=[(parts, (p, None, tr, c), lambda l, i: (0, l, i, 0)), (w, blk, imap), (m, blk, imap), (v, blk, imap)],
                 outs=[out] * 4, name=name)


def _rows(x):
    return x.reshape(-1, LANES)


def _pad_rows(x, mult=8):
    r = x.shape[0]
    extra = (-r) % mult
    return jnp.pad(x, ((0, extra), (0, 0))) if extra else x


def kernel(x, c, ada_w, ada_b, ln_g, ln_b, dn_w_in, dn_conv_w, dn_a_log, dn_dt_bias, dn_norm_w, dn_w_out, cf_w_in, cf_dw_w, cf_dw_b, cf_ln_g, cf_ln_b, cf_w_out, ff_w1, ff_w2, loss_target, m_ada_w, m_ada_b, m_ln_g, m_ln_b, m_dn_w_in, m_dn_conv_w, m_dn_a_log, m_dn_dt_bias, m_dn_norm_w, m_dn_w_out, m_cf_w_in, m_cf_dw_w, m_cf_dw_b, m_cf_ln_g, m_cf_ln_b, m_cf_w_out, m_ff_w1, m_ff_w2, v_ada_w, v_ada_b, v_ln_g, v_ln_b, v_dn_w_in, v_dn_conv_w, v_dn_a_log, v_dn_dt_bias, v_dn_norm_w, v_dn_w_out, v_cf_w_in, v_cf_dw_w, v_cf_dw_b, v_cf_ln_g, v_cf_ln_b, v_cf_w_out, v_ff_w1, v_ff_w2):
    depth, d, _ = ada_w.shape
    n_a, n_b = dn_w_in.shape[0], cf_w_in.shape[0]
    heads = dn_a_log.shape[1]
    hw = heads * HEAD_DIM
    taps = cf_dw_w.shape[1]
    s = x.shape[1]
    alpha = (2.0 * depth) ** 0.25
    me = 4 * lax.axis_index("x") + 2 * lax.axis_index("y") + lax.axis_index("c")
    me_arr = jnp.reshape(me, (1,)).astype(jnp.int32)
    xs, tgt = x[0], loss_target[0]

    dn_in_cols = dn_w_in.shape[2]
    keys, shards = [], []
    for i in range(depth):
        j = i // 2
        mixer = [("dn_in", dn_w_in), ("dn_out", dn_w_out)] if i % 2 == 0 else [("cf_in", cf_w_in), ("cf_out", cf_w_out)]
        for nm, wt in mixer:
            keys.append((nm, j))
            shards.append(wt[j].astype(MXU_DTYPE))
        keys += [("ff1", i), ("ff2", i)]
        shards += [ff_w1[i].astype(MXU_DTYPE), ff_w2[i].astype(MXU_DTYPE)]

    small_local = [_rows(ln_g), _rows(ln_b), _rows(dn_conv_w), _rows(cf_dw_w), _rows(cf_dw_b), _rows(cf_ln_g),
                   _rows(cf_ln_b), _rows(c)]
    sizes = [a.shape[0] for a in small_local]
    packed = _pad_rows(jnp.concatenate(small_local, axis=0))
    (small_all,) = _exchange([packed], False, "comm_gather_params")
    offs = [0]
    for z in sizes:
        offs.append(offs[-1] + z)

    def small(i):
        return small_all[:, offs[i]:offs[i + 1], :]

    def unshard(piece, lead, groups):
        t = piece.reshape((N_DEV,) + lead + (groups * LANES,))
        t = jnp.moveaxis(t, 0, len(lead))
        return t.reshape(lead + (N_DEV * groups * LANES,))

    ln_g_f = unshard(small(0), (depth, 2), 1)
    ln_b_f = unshard(small(1), (depth, 2), 1)
    conv_w_f = unshard(small(2), (n_a, DN_CONV), 3 * heads // N_DEV)
    dw_w_f = unshard(small(3), (n_b, taps), 1)
    dw_b_f = unshard(small(4), (n_b,), 1)
    cf_ln_g_f = unshard(small(5), (n_b,), 1)
    cf_ln_b_f = unshard(small(6), (n_b,), 1)
    c_all = small(7).reshape(N_DEV, d)

    mod_part, cond_all = _ada_fwd(c_all, ada_w)
    (mod_all,) = _exchange([mod_part], False, "comm_gather_mod")
    mod_mine = lax.dynamic_index_in_dim(mod_all, me, axis=2, keepdims=False)
    mod_mine = jnp.moveaxis(mod_mine, 0, 1).reshape(depth, N_MOD * d)

    handles, token = _gather2_start(shards, [_landing(a, me) for a in shards], mod_all, "gather_weights_start")
    handles = dict(zip(keys, handles))
    groups = [keys[:1], keys[1:4]] + [keys[4 * i:4 * i + 4] for i in range(1, depth)]
    group_of = {k: n for n, grp in enumerate(groups) for k in grp}
    relayed, weights = {}, {}

    def relay(n, after):
        if n < len(groups) and groups[n][0] not in relayed:
            hs = _gather2_relay([handles[k] for k in groups[n]], after, "gather_relay_%d" % n)
            relayed.update(zip(groups[n], hs))

    relay(0, token)

    def gathered(key, after):
        if key not in weights:
            relay(group_of[key], after)
            if key[0] == "ff1":
                relay(key[1] + 2, after)
            weights[key] = _gather2_wait(relayed[key], after, "gather_wait_%s_%d" % key)
        return weights[key]

    def get_dn_in(j):
        def get(after):
            g = gathered(("dn_in", j), after)
            w = jnp.moveaxis(g, 0, 1).reshape(d, N_DEV * dn_in_cols)
            return jnp.pad(w, ((0, 0), (0, 4 * hw + LANES - N_DEV * dn_in_cols)))
        return get

    def get_rows(key):
        return lambda after: gathered(key, after).reshape((-1, d))

    def get_cols(key):
        return lambda after: gathered(key, after)

    def add_bias(a, b):
        return (a + b,), ()

    (mod,), _ = _rowmap(add_bias, [mod_mine, ada_b], [], [(N_MOD * d, F32)], [], "ada_bias", pin=token)
    mod_rows = mod.reshape(depth * N_MOD, 1, d)
    ln_g_rows = ln_g_f.reshape(depth * 2, 1, d)
    ln_b_rows = ln_b_f.reshape(depth * 2, 1, d)

    def mod_row(i, j):
        return (mod_rows, i * N_MOD + j)

    def ln_row(rows, i, j):
        return (rows, i * 2 + j)

    subs = []
    h_cur = _modulate_fwd(xs, mod_row(0, 1), mod_row(0, 0))
    x_cur = xs
    last = None
    for i in range(depth):
        j = i // 2
        if i % 2 == 0:
            y, res = _deltanet_fwd(h_cur, get_dn_in(j), conv_w_f[j], dn_a_log[j], dn_dt_bias[j], dn_norm_w[j],
                                   get_rows(("dn_out", j)))
        else:
            y, res = _conformer_fwd(h_cur, get_cols(("cf_in", j)), dw_w_f[j], dw_b_f[j][None, :], cf_ln_g_f[j][None, :],
                                    cf_ln_b_f[j][None, :], get_rows(("cf_out", j)))
        p1 = (mod_row(i, 2), ln_row(ln_g_rows, i, 0), ln_row(ln_b_rows, i, 0), mod_row(i, 4), mod_row(i, 3))
        x_mid, h_mid = _combine_fwd(alpha, x_cur, y, *p1)
        subs.append((x_cur, y, p1, res))
        m_out, res2 = _mlp_fwd(h_mid, get_cols(("ff1", i)), get_rows(("ff2", i)))
        if i + 1 < depth:
            p2 = (mod_row(i, 5), ln_row(ln_g_rows, i, 1), ln_row(ln_b_rows, i, 1), mod_row(i + 1, 1), mod_row(i + 1, 0))
            x_next, h_next = _combine_fwd(alpha, x_mid, m_out, *p2)
            subs.append((x_mid, m_out, p2, res2))
            x_cur, h_cur = x_next, h_next
        else:
            p2 = (mod_row(i, 5), ln_row(ln_g_rows, i, 1), ln_row(ln_b_rows, i, 1))
            last = (x_mid, m_out, p2, res2)

    x_in, y_in, p_last, res_last = last
    dx, dy, (loss_acc, g_gt, g_g, g_b) = _last_fwd_bwd(alpha, x_in, y_in, tgt, *p_last)
    loss = lax.psum(loss_acc[0, 0], ("x", "y", "c"))

    d_mod = [[None] * N_MOD for _ in range(depth)]
    d_ln_g = [[None, None] for _ in range(depth)]
    d_ln_b = [[None, None] for _ in range(depth)]
    d_mod[depth - 1][5], d_ln_g[depth - 1][1], d_ln_b[depth - 1][1] = g_gt, g_g, g_b
    gw = dict(dn=[None] * n_a, cf=[None] * n_b)

    sent = {}

    def send_grads(named, tag):
        parts = [p for _, p in named]
        hs, tok = _xfer_start(parts, [lax.empty(p.shape, p.dtype) for p in parts], True, parts[0], "scatter_start_" + tag)
        for (key, _), hnd in zip(named, hs):
            sent[key] = hnd
        return tok

    def by_rows(g):
        return g.reshape((N_DEV, g.shape[0] // N_DEV, g.shape[1]))

    def send_mlp(i, d_w1, d_w2):
        return send_grads([(("ff1", i), d_w1), (("ff2", i), by_rows(d_w2))], "ff_%d" % i)

    dh, d_w1, d_w2 = _mlp_bwd(res_last, dy)
    pin = send_mlp(depth - 1, d_w1, d_w2)
    for idx in range(len(subs) - 1, -1, -1):
        x_in, y_in, prm, res = subs[idx]
        i, second = idx // 2, idx % 2
        dx, dy, (g_gt, g_g, g_b, g_sc, g_sh) = _combine_bwd(alpha, x_in, y_in, dx, dh, *prm, pin=pin)
        d_mod[i][5 if second else 2], d_ln_g[i][second], d_ln_b[i][second] = g_gt, g_g, g_b
        nxt_i, nxt_base = (i + 1, 0) if second else (i, 3)
        d_mod[nxt_i][nxt_base + 1], d_mod[nxt_i][nxt_base] = g_sc, g_sh
        j = i // 2
        if second:
            dh, d_w1, d_w2 = _mlp_bwd(res, dy)
            pin = send_mlp(i, d_w1, d_w2)
        elif i % 2 == 0:
            dh, gw["dn"][j] = _deltanet_bwd(res, dy, conv_w_f[j])
            d_in = gw["dn"][j]["w_in"][:, :N_DEV * dn_in_cols].reshape(d, N_DEV, dn_in_cols)
            pin = send_grads([(("dn_in", j), jnp.moveaxis(d_in, 1, 0)), (("dn_out", j), by_rows(gw["dn"][j]["w_out"]))],
                             "dn_%d" % j)
        else:
            dh, gw["cf"][j] = _conformer_bwd(res, dy, dw_w_f[j], cf_ln_g_f[j][None, :], cf_ln_b_f[j][None, :])
            pin = send_grads([(("cf_in", j), gw["cf"][j]["w_in"]), (("cf_out", j), by_rows(gw["cf"][j]["w_out"]))],
                             "cf_%d" % j)
    grad_x, g_sc, g_sh = _modulate_bwd(xs, dx, dh, mod_row(0, 1), mod_row(0, 0), pin=pin)
    d_mod[0][1], d_mod[0][0] = g_sc, g_sh
    d_mod_full = jnp.concatenate([jnp.concatenate(r, axis=1) for r in d_mod], axis=0)

    stacked = {"dn_w_in": ("dn_in", dn_w_in, m_dn_w_in, v_dn_w_in), "dn_w_out": ("dn_out", dn_w_out, m_dn_w_out, v_dn_w_out),
               "cf_w_in": ("cf_in", cf_w_in, m_cf_w_in, v_cf_w_in), "cf_w_out": ("cf_out", cf_w_out, m_cf_w_out, v_cf_w_out),
               "ff_w1": ("ff1", ff_w1, m_ff_w1, v_ff_w1), "ff_w2": ("ff2", ff_w2, m_ff_w2, v_ff_w2)}
    chains = {key: None for key in stacked}

    def update_layer(i, token):
        mixer = ["dn_w_in", "dn_w_out"] if i % 2 == 0 else ["cf_w_in", "cf_w_out"]
        for key, idx in [("ff_w1", i), ("ff_w2", i)] + [(k, i // 2) for k in mixer]:
            short, w, m, v = stacked[key]
            land = _xfer_wait(sent[(short, idx)], True, token, "scatter_wait_%s_%d" % (short, idx))
            chains[key], token = _adamw_layer(sent[(short, idx)][2], land, me_arr, w, m, v, idx, chains[key],
                                              "adamw_%s_%d" % (key, idx))
        return token

    def stack_rows(lst):
        return jnp.stack(lst, axis=0)

    gs_ln_g = jnp.stack([jnp.concatenate(r, axis=0) for r in d_ln_g], axis=0)
    gs_ln_b = jnp.stack([jnp.concatenate(r, axis=0) for r in d_ln_b], axis=0)
    gs_conv_w = stack_rows([gw["dn"][j]["conv_w"] for j in range(n_a)])
    gs_dw_w = stack_rows([gw["cf"][j]["dw_w"] for j in range(n_b)])
    gs_dw_b = stack_rows([gw["cf"][j]["dw_b"] for j in range(n_b)])
    gs_cf_ln_g = stack_rows([gw["cf"][j]["ln_g"] for j in range(n_b)])
    gs_cf_ln_b = stack_rows([gw["cf"][j]["ln_b"] for j in range(n_b)])
    gs_a_log = stack_rows([_pad_lanes(gw["dn"][j]["a_log"], 0)[0] for j in range(n_a)])
    gs_dt_bias = stack_rows([_pad_lanes(gw["dn"][j]["dt_bias"], 0)[0] for j in range(n_a)])
    gs_norm_w = stack_rows([gw["dn"][j]["norm_w"] for j in range(n_a)])
    small_grads = [gs_ln_g, gs_ln_b, gs_conv_w, gs_dw_w, gs_dw_b, gs_cf_ln_g, gs_cf_ln_b, gs_a_log, gs_dt_bias,
                   gs_norm_w, d_mod_full]
    sg_rows = [_rows(a) for a in small_grads]
    sg_sizes = [a.shape[0] for a in sg_rows]
    sg_packed = _pad_rows(jnp.concatenate(sg_rows, axis=0))
    (sg_handle,), sg_token = _xfer_start([sg_packed], [_landing(sg_packed, me)], False, grad_x, "gather_small_grads_start")
    for i in range(depth - 1, -1, -1):
        sg_token = update_layer(i, sg_token)
    sg_all = _xfer_wait(sg_handle, False, sg_token, "gather_small_grads_wait")
    sg_offs = [0]
    for z in sg_sizes:
        sg_offs.append(sg_offs[-1] + z)

    def sg(i, shape):
        return sg_all[:, sg_offs[i]:sg_offs[i + 1], :].reshape((N_DEV,) + shape)

    dmod_all = sg(10, (depth, N_MOD * d))
    nl = ada_w.shape[2]
    dmod_cols = lax.dynamic_slice_in_dim(dmod_all, me * nl, nl, axis=2)
    g_ada_w = _ada_bwd(cond_all, jnp.moveaxis(dmod_cols, 0, 1))

    outs = {}

    def run_adamw(key, parts, w, m, v):
        shp = w.shape
        as3 = lambda t: t.reshape((-1,) + shp[-2:]) if t.ndim >= 3 else t.reshape((1,) + shp)
        parts3 = parts.reshape((parts.shape[0],) + as3(w).shape)
        res = _adamw(parts3, as3(w), as3(m), as3(v), "adamw_" + key)
        outs[key] = tuple(r.reshape(shp) for r in res)

    run_adamw("ada_w", g_ada_w[None], ada_w, m_ada_w, v_ada_w)

    cgroups = 3 * heads // N_DEV
    n_sharded = 7

    def my_cols(first, last, groups):
        x = sg_all[:, sg_offs[first]:sg_offs[last], :].reshape(N_DEV, -1, N_DEV, groups, LANES)
        return lax.dynamic_index_in_dim(x, me, axis=2, keepdims=False).reshape(N_DEV, -1, LANES)

    small_parts = [my_cols(0, 2, 1), my_cols(2, 3, cgroups), my_cols(3, n_sharded, 1),
                   sg_all[:, sg_offs[n_sharded]:sg_offs[-1], :]]
    sp_offs = [0]
    for n, z in enumerate(sg_sizes):
        sp_offs.append(sp_offs[-1] + (z // N_DEV if n < n_sharded else z))
    parts_packed = jnp.zeros((N_DEV, sp_offs[-1] + (-sp_offs[-1]) % 8, LANES), F32)
    at = 0
    for part in small_parts:
        parts_packed = lax.dynamic_update_slice(parts_packed, part, (0, at, 0))
        at += part.shape[1]

    def pad_heads(t):
        return jnp.pad(t, ((0, 0), (0, LANES - heads)))

    def pack_state(ln_g_, ln_b_, conv_w_, dw_w_, dw_b_, cln_g_, cln_b_, a_log_, dt_b_, norm_w_, ada_b_):
        rows = [_rows(ln_g_), _rows(ln_b_), _rows(conv_w_), _rows(dw_w_), _rows(dw_b_), _rows(cln_g_), _rows(cln_b_),
                pad_heads(a_log_), pad_heads(dt_b_), norm_w_, _rows(ada_b_)]
        return _pad_rows(jnp.concatenate(rows, axis=0))

    w_s = pack_state(ln_g, ln_b, dn_conv_w, cf_dw_w, cf_dw_b, cf_ln_g, cf_ln_b, dn_a_log, dn_dt_bias, dn_norm_w, ada_b)
    m_s = pack_state(m_ln_g, m_ln_b, m_dn_conv_w, m_cf_dw_w, m_cf_dw_b, m_cf_ln_g, m_cf_ln_b, m_dn_a_log,
                     m_dn_dt_bias, m_dn_norm_w, m_ada_b)
    v_s = pack_state(v_ln_g, v_ln_b, v_dn_conv_w, v_cf_dw_w, v_cf_dw_b, v_cf_ln_g, v_cf_ln_b, v_dn_a_log,
                     v_dn_dt_bias, v_dn_norm_w, v_ada_b)
    res_s = _adamw(parts_packed[:, None], w_s[None], m_s[None], v_s[None], "adamw_small")
    small_keys = ["ln_g", "ln_b", "dn_conv_w", "cf_dw_w", "cf_dw_b", "cf_ln_g", "cf_ln_b", "dn_a_log", "dn_dt_bias",
                  "dn_norm_w", "ada_b"]
    small_shapes = [ln_g.shape, ln_b.shape, dn_conv_w.shape, cf_dw_w.shape, cf_dw_b.shape, cf_ln_g.shape,
                    cf_ln_b.shape, dn_a_log.shape, dn_dt_bias.shape, dn_norm_w.shape, ada_b.shape]
    for n, (key, shp) in enumerate(zip(small_keys, small_shapes)):
        vals = []
        for r in res_s:
            piece = r[0, sp_offs[n]:sp_offs[n + 1], :]
            if key in ("dn_a_log", "dn_dt_bias"):
                piece = piece[:, :heads]
            vals.append(piece.reshape(shp))
        outs[key] = tuple(vals)

    for key in stacked:
        outs[key] = tuple(chains[key])

    order = ["ada_w", "ada_b", "ln_g", "ln_b", "dn_w_in", "dn_conv_w", "dn_a_log", "dn_dt_bias", "dn_norm_w",
             "dn_w_out", "cf_w_in", "cf_dw_w", "cf_dw_b", "cf_ln_g", "cf_ln_b", "cf_w_out", "ff_w1", "ff_w2"]
    result = [loss, grad_x[None]]
    for part in range(4):
        result += [outs[k][part] for k in order]
    return tuple(result)
```

```python
import functools

import jax
import jax.numpy as jnp
from jax import lax
from jax.experimental import pallas as pl
from jax.experimental.pallas import tpu as pltpu

F32 = jnp.float32
MXU_DTYPE = jnp.bfloat16
N_DEV = 8
LANES = 128
HEAD_DIM = 128
CHUNK = 64
DN_CONV = 4
N_MOD = 6
LN_EPS = 1e-5
RMS_EPS = 1e-6
L2_EPS = 1e-6
ADAM_LR = 0.001
ADAM_B1 = 0.9
ADAM_B2 = 0.999
ADAM_EPS = 1e-08
ADAM_WD = 0.01
ADAM_STEP = 10

NN = ((1,), (0,))
NT = ((1,), (1,))
TN = ((0,), (0,))

ROW_TILE = 512
CONV_TILE = 256
SHORT_CONV_TILE = 1024


def _mdot(a, b, dims=NN):
    return lax.dot_general(a.astype(MXU_DTYPE), b.astype(MXU_DTYPE), (dims, ((), ())), preferred_element_type=F32)


def _split3(x):
    hi = x.astype(MXU_DTYPE)
    r1 = x - hi.astype(F32)
    mid = r1.astype(MXU_DTYPE)
    lo = (r1 - mid.astype(F32)).astype(MXU_DTYPE)
    return hi, mid, lo


def _dot01(a, b, dims=NN, mask_first=True):
    d = lambda p, q: lax.dot_general(p, q, (dims, ((), ())), preferred_element_type=F32)
    if mask_first:
        m = a.astype(MXU_DTYPE)
        return sum(d(m, p) for p in _split3(b))
    m = b.astype(MXU_DTYPE)
    return sum(d(p, m) for p in _split3(a))


def _cparams(n):
    return pltpu.CompilerParams(dimension_semantics=("arbitrary",) * n)


def _call(body, *, grid, ins, outs, name, scratch=()):
    res = pl.pallas_call(
        body,
        grid=grid,
        in_specs=[pl.BlockSpec(memory_space=pl.ANY) if b is None else pl.BlockSpec(b, m) for _, b, m in ins],
        out_specs=[pl.BlockSpec(b, m) for _, _, b, m in outs],
        out_shape=[jax.ShapeDtypeStruct(s, d) for s, d, _, _ in outs],
        scratch_shapes=list(scratch),
        name=name,
        compiler_params=_cparams(len(grid)),
    )(*[a for a, _, _ in ins])
    return res


def _tile(n, pref, unit=LANES):
    if n <= pref:
        return n
    t = (pref // unit) * unit
    while t > unit and n % t:
        t -= unit
    assert n % t == 0, (n, pref)
    return t


def _rowmap(fn, rows, consts, row_outs, acc_outs, name, pin=None):
    rows = [r if isinstance(r, tuple) else (r, r.shape[1], 0) for r in rows]
    s = rows[0][0].shape[0]
    tm = min(ROW_TILE, s)
    nr, nc, no, na = len(rows), len(consts), len(row_outs), len(acc_outs)
    npin = 0 if pin is None else 1

    def body(*refs):
        rin, cin = refs[:nr], refs[nr:nr + nc]
        refs = refs[:nr + nc] + refs[nr + nc + npin:]
        rout, aout = refs[nr + nc:nr + nc + no], refs[nr + nc + no:]
        ro, ao = fn(*[r[...] for r in rin], *[c[...] for c in cin])
        for ref, val in zip(rout, ro):
            ref[...] = val.astype(ref.dtype)
        if na:
            first = pl.program_id(0) == 0

            @pl.when(first)
            def _():
                for ref, val in zip(aout, ao):
                    ref[...] = val

            @pl.when(jnp.logical_not(first))
            def _():
                for ref, val in zip(aout, ao):
                    ref[...] += val

    ins = [(a, (tm, w), functools.partial(lambda i, cb: (i, cb), cb=cb)) for a, w, cb in rows]
    for c in consts:
        if isinstance(c, tuple):
            ins.append((c[0], (None, 1, c[0].shape[2]), functools.partial(lambda i, n: (n, 0, 0), n=c[1])))
        else:
            ins.append((c, c.shape, lambda i: (0, 0)))
    if pin is not None:
        ins.append((pin, None, None))
    outs = [((s, w), d, (tm, w), lambda i: (i, 0)) for w, d in row_outs]
    outs += [(shp, F32, shp, lambda i: (0, 0)) for shp in acc_outs]
    res = _call(body, grid=(s // tm,), ins=ins, outs=outs, name=name)
    return res[:no], res[no:]


def _ln(z, g, b):
    mu = jnp.mean(z, -1, keepdims=True)
    var = jnp.mean(jnp.square(z - mu), -1, keepdims=True)
    return (z - mu) * lax.rsqrt(var + LN_EPS) * g + b


def _combine(alpha, x, y, gt, g, b, sc, sh):
    xn = _ln(alpha * x + (1.0 + gt) * y, g, b)
    return xn, xn * (1.0 + sc) + sh


def _modulate_fwd(x, sc, sh):
    def fn(x, sc, sh):
        return ((x * (1.0 + sc) + sh),), ()

    (h,), _ = _rowmap(fn, [x], [sc, sh], [(x.shape[1], MXU_DTYPE)], [], "modulate_fwd")
    return h


def _modulate_bwd(x, dx, dh, sc, sh, pin=None):
    d = x.shape[1]

    def fn(x, dx, dh, sc, sh):
        _, vjp = jax.vjp(lambda x, sc, sh: x * (1.0 + sc) + sh, x, sc, sh)
        gx, gsc, gsh = vjp(dh)
        return (dx + gx,), (gsc, gsh)

    (gx,), (gsc, gsh) = _rowmap(fn, [x, dx, dh], [sc, sh], [(d, F32)], [(1, d), (1, d)], "modulate_bwd", pin=pin)
    return gx, gsc, gsh


def _combine_fwd(alpha, x, y, gt, g, b, sc, sh):
    d = x.shape[1]

    def fn(x, y, gt, g, b, sc, sh):
        return _combine(alpha, x, y, gt, g, b, sc, sh), ()

    (xn, h), _ = _rowmap(fn, [x, y], [gt, g, b, sc, sh], [(d, F32), (d, MXU_DTYPE)], [], "combine_fwd")
    return xn, h


def _combine_bwd(alpha, x, y, dxn, dh, gt, g, b, sc, sh, pin=None):
    d = x.shape[1]

    def fn(x, y, dxn, dh, gt, g, b, sc, sh):
        _, vjp = jax.vjp(functools.partial(_combine, alpha), x, y, gt, g, b, sc, sh)
        gx, gy, ggt, gg, gb, gsc, gsh = vjp((dxn, dh))
        return (gx, gy), (ggt, gg, gb, gsc, gsh)

    (gx, gy), accs = _rowmap(fn, [x, y, dxn, dh], [gt, g, b, sc, sh], [(d, F32), (d, MXU_DTYPE)],
                             [(1, d)] * 5, "combine_bwd", pin=pin)
    return gx, gy, accs


def _last_fwd_bwd(alpha, x, y, tgt, gt, g, b):
    d = x.shape[1]

    def fn(x, y, tgt, gt, g, b):
        xn, vjp = jax.vjp(lambda x, y, gt, g, b: _ln(alpha * x + (1.0 + gt) * y, g, b), x, y, gt, g, b)
        err = xn - tgt
        gx, gy, ggt, gg, gb = vjp(err * (1.0 / d))
        rows = jnp.sum(jnp.square(err), axis=-1, keepdims=True)
        loss = (0.5 / d) * jnp.sum(rows, axis=0, keepdims=True) * jnp.ones((1, LANES), F32)
        return (gx, gy), (loss, ggt, gg, gb)

    (gx, gy), accs = _rowmap(fn, [x, y, tgt], [gt, g, b], [(d, F32), (d, MXU_DTYPE)],
                             [(1, LANES), (1, d), (1, d), (1, d)], "last_fwd_bwd")
    return gx, gy, accs


MM_VMEM_BUDGET = 40 * 2 ** 20


def _fit(options, cost):
    for o in options:
        if 2 * cost(o) <= MM_VMEM_BUDGET:
            return o
    return options[-1]


def _row_tiles(m):
    return [t for t in (2048, 1024, 512, 256) if t <= m and m % t == 0] or [m]


def _mm_call(a, a_blk, a_map, b, b_blk, b_map, outs, dims, grid, name, epi=None, extra=None, split=None, blocks=None,
             pin=None):
    nk = grid[2]
    n_out = len(outs)
    n_in = 2 + (extra is not None) + (pin is not None)

    def body(*refs):
        a_ref, b_ref = refs[0], refs[1]
        rest = refs[n_in:]
        out_refs = rest[:n_out]

        def finish(val):
            if epi == "relu2":
                out_refs[0][...] = jnp.square(jnp.maximum(val, 0.0)).astype(out_refs[0].dtype)
            elif epi == "relu2_bwd":
                sq = refs[2][...].astype(F32)
                root = jnp.where(sq > 0.0, sq * lax.rsqrt(sq), 0.0)
                out_refs[0][...] = (val * 2.0 * root).astype(out_refs[0].dtype)
            elif split is not None:
                for g in range(split[0]):
                    out_refs[0][g] = val[:, g * split[1]:(g + 1) * split[1]].astype(out_refs[0].dtype)
            else:
                out_refs[0][...] = val.astype(out_refs[0].dtype)

        if blocks is None:
            p = lax.dot_general(a_ref[...], b_ref[...], (dims, ((), ())), preferred_element_type=F32)
        else:
            p = None
            for g in range(blocks[0]):
                part = lax.dot_general(a_ref[:, g * blocks[1]:(g + 1) * blocks[1]], b_ref[g], (dims, ((), ())),
                                       preferred_element_type=F32)
                p = part if p is None else p + part
        if nk == 1:
            finish(p)
        else:
            acc = rest[n_out]
            k = pl.program_id(2)

            @pl.when(k == 0)
            def _():
                acc[...] = p

            @pl.when(k > 0)
            def _():
                acc[...] += p

            @pl.when(k == nk - 1)
            def _():
                finish(acc[...])

    if nk > 1:
        out_blk = tuple(x for x in outs[0][2] if x is not None)
        if split is not None:
            out_blk = (out_blk[1], split[0] * split[1])
        scratch = [pltpu.VMEM(out_blk, F32)]
    else:
        scratch = []
    ins = [(a, a_blk, a_map), (b, b_blk, b_map)] + ([extra] if extra is not None else [])
    ins += [(pin, None, None)] if pin is not None else []
    return _call(body, grid=grid, ins=ins, outs=outs, name=name, scratch=scratch)


def _isz(dt):
    return jnp.dtype(dt).itemsize


def _mm_nn(a, b, out_dtype, name, relu2=False):
    m, kdim = a.shape
    if b.ndim == 2:
        n = b.shape[1]
        tn = _tile(n, 1536 if n > 2048 else 512)
        b_blk, b_map = (kdim, tn), lambda i, j, k: (0, j)
    else:
        g, _, ng = b.shape
        n = g * ng
        tn = _tile(ng, 512)
        b_blk = (None, kdim, tn)
        b_map = functools.partial(lambda i, j, k, npg: (j // npg, 0, j % npg), npg=ng // tn)
    tm = _fit(_row_tiles(m), lambda t: t * kdim * _isz(a.dtype) + kdim * tn * _isz(b.dtype) + t * tn * _isz(out_dtype))
    grid = (m // tm, n // tn, 1)
    outs = [((m, n), out_dtype, (tm, tn), lambda i, j, k: (i, j))]
    return _mm_call(a, (tm, kdim), lambda i, j, k: (i, 0), b, b_blk, b_map, outs, NN, grid, name,
                    epi="relu2" if relu2 else None)[0]


def _mm_nt(a, b, out_dtype, name, relu2_sq=None, pin=None):
    m, n = a.shape
    extra_bytes = _isz(relu2_sq.dtype) if relu2_sq is not None else 0
    if b.ndim == 2:
        kout = b.shape[0]
        to = _tile(kout, 512)
        b_blk, b_map, blocks = (to, n), lambda i, j, k: (j, 0), None
    else:
        g, kout, ng = b.shape
        to = _tile(kout, 512)
        b_blk, b_map, blocks = (g, to, ng), lambda i, j, k: (0, j, 0), (g, ng)
    tm = _fit(_row_tiles(m), lambda t: t * n * _isz(a.dtype) + to * n * _isz(b.dtype)
              + t * to * (_isz(out_dtype) + extra_bytes))
    grid = (m // tm, kout // to, 1)
    outs = [((m, kout), out_dtype, (tm, to), lambda i, j, k: (i, j))]
    extra = (relu2_sq, (tm, to), lambda i, j, k: (i, j)) if relu2_sq is not None else None
    return _mm_call(a, (tm, n), lambda i, j, k: (i, 0), b, b_blk, b_map, outs, NT, grid, name,
                    epi="relu2_bwd" if relu2_sq is not None else None, extra=extra, blocks=blocks, pin=pin)[0]


def _mm_tn(a, b, out_dtype, name, split_cols=False):
    m, kdim = a.shape
    n = b.shape[1]
    tk = _tile(kdim, 512)
    tn = _tile(n, 1536)
    if not split_cols:
        out, split = ((kdim, n), out_dtype, (tk, tn), lambda i, j, k: (i, j)), None
    else:
        ng = n // N_DEV
        if tn % ng:
            tn = _tile(ng, 512)
        if tn >= ng:
            gb = tn // ng
            out = ((N_DEV, kdim, ng), out_dtype, (gb, tk, ng), lambda i, j, k: (j, i, 0))
            split = (gb, ng)
        else:
            out = ((N_DEV, kdim, ng), out_dtype, (None, tk, tn),
                   functools.partial(lambda i, j, k, npg: (j // npg, i, j % npg), npg=ng // tn))
            split = None
    grid = (kdim // tk, n // tn, 1)
    return _mm_call(a, (m, tk), lambda i, j, k: (0, i), b, (m, tn), lambda i, j, k: (0, j), [out], TN, grid, name,
                    split=split)[0]


def _shifted(xa, off, rows):
    if off % 8 == 0:
        return xa[off:off + rows]
    return pltpu.roll(xa, xa.shape[0] - off, 0)[:rows]


def _conv_pad(taps):
    return -(-(taps - 1) // 8) * 8


def _conv_tile(xp_ref, w, i, rows, taps):
    pad = _conv_pad(taps)
    r0 = pl.multiple_of(i * rows, rows)
    xa = xp_ref[pl.ds(r0, rows + pad), :]
    views = [_shifted(xa, pad - (taps - 1) + j, rows) for j in range(taps)]
    acc = w[0:1, :] * views[0]
    for j in range(1, taps):
        acc = acc + w[j:j + 1, :] * views[j]
    return r0, acc, views


def _conv_back_tile(yp_ref, w, i, rows, taps):
    pad = _conv_pad(taps)
    r0 = pl.multiple_of(i * rows, rows)
    ya = yp_ref[pl.ds(r0, rows + pad), :]
    acc = w[taps - 1:taps, :] * ya[:rows]
    for j in range(taps - 1):
        acc = acc + w[j:j + 1, :] * _shifted(ya, taps - 1 - j, rows)
    return r0, acc


def _tap_sums(dy, views, taps):
    row = lax.broadcasted_iota(jnp.int32, (taps, LANES), 0)
    acc = jnp.zeros((taps, LANES), F32)
    for j in range(taps):
        acc = acc + jnp.where(row == j, jnp.sum(dy * views[j], axis=0, keepdims=True), 0.0)
    return acc


def _silu_l2(xc, l2):
    a = jax.nn.silu(xc)
    if l2:
        a = a * lax.rsqrt(jnp.sum(a * a, axis=-1, keepdims=True) + L2_EPS)
    return a


def _dn_conv_fwd(proj, conv_w, c0, nblk, l2, name):
    s = proj.shape[0]
    pad = _conv_pad(DN_CONV)
    rows = min(SHORT_CONV_TILE, s)

    def body(x_ref, w_ref, o_ref, xp):
        xp[0:pad, :] = jnp.zeros((pad, LANES), F32)
        xp[pad:, :] = x_ref[...]
        w = w_ref[...]

        def tile(i, c):
            r0, acc, _ = _conv_tile(xp, w, i, rows, DN_CONV)
            o_ref[pl.ds(r0, rows), :] = _silu_l2(acc, l2)
            return c

        lax.fori_loop(0, s // rows, tile, 0)

    return _call(body, grid=(nblk,),
                 ins=[(proj, (s, LANES), lambda c: (0, c0 + c)), (conv_w, (DN_CONV, LANES), lambda c: (0, c0 + c))],
                 outs=[((nblk, s, LANES), F32, (None, s, LANES), lambda c: (c, 0, 0))],
                 name=name, scratch=[pltpu.VMEM((s + pad, LANES), F32)])[0]


def _dn_conv_bwd(proj, conv_w, da, c0, nblk, l2, name):
    s = proj.shape[0]
    pad = _conv_pad(DN_CONV)
    rows = min(SHORT_CONV_TILE, s)

    def body(x_ref, w_ref, da_ref, dx_ref, dw_ref, xp, yp):
        xp[0:pad, :] = jnp.zeros((pad, LANES), F32)
        xp[pad:, :] = x_ref[...]
        yp[s:, :] = jnp.zeros((pad, LANES), F32)
        w = w_ref[...]

        def tile(i, dw):
            r0, acc, views = _conv_tile(xp, w, i, rows, DN_CONV)
            _, vjp = jax.vjp(functools.partial(_silu_l2, l2=l2), acc)
            (dxc,) = vjp(da_ref[pl.ds(r0, rows), :])
            yp[pl.ds(r0, rows), :] = dxc
            return dw + _tap_sums(dxc, views, DN_CONV)

        dw_ref[...] = lax.fori_loop(0, s // rows, tile, jnp.zeros((DN_CONV, LANES), F32))

        def tile2(i, c):
            r0, acc = _conv_back_tile(yp, w, i, rows, DN_CONV)
            dx_ref[pl.ds(r0, rows), :] = acc.astype(dx_ref.dtype)
            return c

        lax.fori_loop(0, s // rows, tile2, 0)

    return _call(body, grid=(nblk,),
                 ins=[(proj, (s, LANES), lambda c: (0, c0 + c)), (conv_w, (DN_CONV, LANES), lambda c: (0, c0 + c)),
                      (da, (None, s, LANES), lambda c: (c, 0, 0))],
                 outs=[((s, nblk * LANES), MXU_DTYPE, (s, LANES), lambda c: (0, c)),
                       ((DN_CONV, nblk * LANES), F32, (DN_CONV, LANES), lambda c: (0, c))],
                 name=name, scratch=[pltpu.VMEM((s + pad, LANES), F32), pltpu.VMEM((s + pad, LANES), F32)])


def _cf_conv_fwd(vg, dw_w, dw_b):
    s, c2 = vg.shape
    ch = c2 // 2
    nblk = ch // LANES
    taps = dw_w.shape[0]
    pad = _conv_pad(taps)
    rows = min(CONV_TILE, s)

    def body(v_ref, g_ref, w_ref, b_ref, o_ref, xp):
        xp[0:pad, :] = jnp.zeros((pad, LANES), F32)
        xp[pad:, :] = v_ref[...] * jax.nn.sigmoid(g_ref[...])
        w = w_ref[...]
        bias = b_ref[...]

        def tile(i, c):
            r0, acc, _ = _conv_tile(xp, w, i, rows, taps)
            o_ref[pl.ds(r0, rows), :] = acc + bias
            return c

        lax.fori_loop(0, s // rows, tile, 0)

    return _call(body, grid=(nblk,),
                 ins=[(vg, (s, LANES), lambda c: (0, c)), (vg, (s, LANES), lambda c: (0, nblk + c)),
                      (dw_w, (taps, LANES), lambda c: (0, c)), (dw_b, (1, LANES), lambda c: (0, c))],
                 outs=[((s, ch), F32, (s, LANES), lambda c: (0, c))],
                 name="cf_conv_fwd", scratch=[pltpu.VMEM((s + pad, LANES), F32)])[0]


def _cf_conv_bwd(vg, dw_w, du):
    s, c2 = vg.shape
    ch = c2 // 2
    nblk = ch // LANES
    taps = dw_w.shape[0]
    pad = _conv_pad(taps)
    rows = min(CONV_TILE, s)

    def body(v_ref, g_ref, w_ref, du_ref, dv_ref, dg_ref, dw_ref, db_ref, xp, yp):
        sig = jax.nn.sigmoid(g_ref[...])
        xp[0:pad, :] = jnp.zeros((pad, LANES), F32)
        xp[pad:, :] = v_ref[...] * sig
        yp[0:s, :] = du_ref[...]
        yp[s:, :] = jnp.zeros((pad, LANES), F32)
        w = w_ref[...]
        db_ref[...] = jnp.sum(du_ref[...], axis=0, keepdims=True)

        def tile(i, dw):
            r0, _, views = _conv_tile(xp, w, i, rows, taps)
            return dw + _tap_sums(du_ref[pl.ds(r0, rows), :], views, taps)

        dw_ref[...] = lax.fori_loop(0, s // rows, tile, jnp.zeros((taps, LANES), F32))

        def tile2(i, c):
            r0, du0 = _conv_back_tile(yp, w, i, rows, taps)
            val = v_ref[pl.ds(r0, rows), :]
            sg = jax.nn.sigmoid(g_ref[pl.ds(r0, rows), :])
            dv_ref[pl.ds(r0, rows), :] = (du0 * sg).astype(dv_ref.dtype)
            dg_ref[pl.ds(r0, rows), :] = (du0 * val * sg * (1.0 - sg)).astype(dg_ref.dtype)
            return c

        lax.fori_loop(0, s // rows, tile2, 0)

    return _call(body, grid=(nblk,),
                 ins=[(vg, (s, LANES), lambda c: (0, c)), (vg, (s, LANES), lambda c: (0, nblk + c)),
                      (dw_w, (taps, LANES), lambda c: (0, c)), (du, (s, LANES), lambda c: (0, c))],
                 outs=[((s, ch), MXU_DTYPE, (s, LANES), lambda c: (0, c)),
                       ((s, ch), MXU_DTYPE, (s, LANES), lambda c: (0, c)),
                       ((taps, ch), F32, (taps, LANES), lambda c: (0, c)),
                       ((1, ch), F32, (1, LANES), lambda c: (0, c))],
                 name="cf_conv_bwd", scratch=[pltpu.VMEM((s + pad, LANES), F32), pltpu.VMEM((s + pad, LANES), F32)])


def _masks():
    r = lax.broadcasted_iota(jnp.int32, (CHUNK, CHUNK), 0)
    c = lax.broadcasted_iota(jnp.int32, (CHUNK, CHUNK), 1)
    return r >= c, r > c, r <= c


def _chunk_decay(g):
    causal, _, upper = _masks()
    gb = jnp.broadcast_to(g, (CHUNK, CHUNK))
    gam_r = _dot01(jnp.where(causal, 1.0, 0.0), gb)
    gam_s = _dot01(jnp.ones((CHUNK, CHUNK), F32), jnp.where(upper, gb, 0.0))
    dm = jnp.where(causal, jnp.exp(jnp.where(causal, gam_r - gam_s, 0.0)), 0.0)
    return gam_r[:, 0:1], dm


def _chunk_scores(q, k, beta, dm):
    _, strict, _ = _masks()
    both = _mdot(jnp.concatenate([k * beta, q * (HEAD_DIM ** -0.5)], axis=0), k, NT)
    return jnp.where(strict, both[:CHUNK] * dm, 0.0), both[CHUNK:] * dm


def _lockstep(gens):
    results = [None] * len(gens)
    alive = list(range(len(gens)))
    while alive:
        for i in list(alive):
            try:
                next(gens[i])
            except StopIteration as stop:
                results[i] = stop.value
                alive.remove(i)
    return results


def _chunk_prep_bwd(q, k, v, beta, gam, t, du, dw, daqk, dqd, dkd, dgl):
    causal, strict, _ = _masks()
    r = lax.broadcasted_iota(jnp.int32, (CHUNK, CHUNK), 0)
    c = lax.broadcasted_iota(jnp.int32, (CHUNK, CHUNK), 1)
    scale = HEAD_DIM ** -0.5
    eg = jnp.exp(gam)
    gam_last = gam[CHUNK - 1:CHUNK, :]
    rr = jnp.exp(gam_last - gam)
    kb = k * beta
    qs = q * scale
    vb = v * beta
    kbe = kb * eg
    gam_b = jnp.broadcast_to(gam, (CHUNK, CHUNK))
    gam_s = _dot01(jnp.ones((CHUNK, CHUNK), F32), jnp.where(r == c, gam_b, 0.0))
    both = _mdot(jnp.concatenate([kb, qs], axis=0), k, NT)
    duw = jnp.concatenate([du, dw], axis=1)
    dt = _mdot(duw, jnp.concatenate([vb, kbe], axis=1), NT)
    dvk = _mdot(t, duw, TN)
    yield
    dm = jnp.where(causal, jnp.exp(jnp.where(causal, gam_b - gam_s, 0.0)), 0.0)
    a = jnp.where(strict, both[:CHUNK] * dm, 0.0)
    aqk = both[CHUNK:] * dm
    dvb, dkbe = dvk[:, :HEAD_DIM], dvk[:, HEAD_DIM:]
    x = _mdot(t, dt, TN)
    yield
    da = jnp.where(strict, -_mdot(x, t, NT), 0.0)
    yield
    dkk = da * dm
    dqk = daqk * dm
    ddiff = da * a + daqk * aqk
    dboth = jnp.concatenate([dkk, dqk], axis=0)
    dkq = _mdot(dboth, k)
    dk_mm = _mdot(dboth, jnp.concatenate([kb, qs], axis=0), TN)
    colsum = _dot01(ddiff, jnp.ones((CHUNK, LANES), F32), TN, mask_first=False)[:, 0:1]
    yield
    dkb = dkq[:CHUNK] + dkbe * eg
    dk = dk_mm + dkb * beta + dkd * rr
    dq = (dkq[CHUNK:] + dqd * eg) * scale
    dbeta = jnp.sum(dkb * k, axis=-1, keepdims=True) + jnp.sum(dvb * v, axis=-1, keepdims=True)
    dv = dvb * beta
    deg = jnp.sum(dkbe * kb, axis=-1, keepdims=True) + jnp.sum(dqd * qs, axis=-1, keepdims=True)
    drr = jnp.sum(dkd * k, axis=-1, keepdims=True)
    dgam = deg * eg - drr * rr + jnp.sum(ddiff, axis=-1, keepdims=True) - colsum
    dgam_last = jnp.sum(drr * rr, axis=0, keepdims=True) + dgl[0:1, :] * jnp.exp(gam_last)
    row = lax.broadcasted_iota(jnp.int32, (CHUNK, 1), 0)
    dgam = dgam + jnp.where(row == CHUNK - 1, dgam_last, 0.0)
    dg = _dot01(jnp.where(causal, 1.0, 0.0), jnp.broadcast_to(dgam, (CHUNK, LANES)), TN)[:, 0:1]
    return dq, dk, dv, dbeta, dg


def _prep_group(s):
    nch = s // CHUNK
    return next(c for c in (16, 8, 4, 2, 1) if nch % c == 0)


def _tri_solve_lanes(a_l):
    n = a_l.shape[1]
    group = 8

    def body(a_ref, t_ref):
        t_ref[...] = jnp.zeros_like(t_ref)
        col = lax.broadcasted_iota(jnp.int32, (CHUNK, n), 0)

        def row(r, carry):
            r0 = pl.multiple_of(r * CHUNK, CHUNK)

            def inner(sg, acc):
                a8 = a_ref[pl.ds(r0 + pl.multiple_of(sg * group, group), group), :]
                for j in range(group):
                    t0 = pl.multiple_of((sg * group + j) * CHUNK, CHUNK)
                    acc = acc + a8[j:j + 1, :] * t_ref[pl.ds(t0, CHUNK), :]
                return acc

            acc = lax.fori_loop(0, (r + group - 1) // group, inner, jnp.zeros((CHUNK, n), F32))
            t_ref[pl.ds(r0, CHUNK), :] = jnp.where(col == r, 1.0, 0.0) - acc
            return carry

        lax.fori_loop(0, CHUNK, row, 0)

    return pl.pallas_call(body, out_shape=jax.ShapeDtypeStruct(a_l.shape, F32), name="dn_tri_solve")(a_l)


def _head_cols(bg, hh, heads):
    lane = lax.broadcasted_iota(jnp.int32, bg.shape, 1)
    beta = jnp.sum(jnp.where(lane == hh, bg, 0.0), axis=-1, keepdims=True)
    g = jnp.sum(jnp.where(lane == heads + hh, bg, 0.0), axis=-1, keepdims=True)
    return beta, g


def _dn_prep(q, k, v, bg):
    h, s, _ = q.shape
    cb = _prep_group(s)
    rb = cb * CHUNK
    big = lambda x: (x, (None, rb, HEAD_DIM), lambda n, hh: (hh, n, 0))
    sq = lambda x: (x, (None, rb, CHUNK), lambda n, hh: (hh, n, 0))
    col = lambda x: (x, (None, rb, 1), lambda n, hh: (hh, n, 0))
    tok = (bg, (rb, LANES), lambda n, hh: (n, 0))
    o_big = ((h, s, HEAD_DIM), F32, (None, rb, HEAD_DIM), lambda n, hh: (hh, n, 0))
    o_sq = ((h, s, CHUNK), F32, (None, rb, CHUNK), lambda n, hh: (hh, n, 0))
    o_col = ((h, s, 1), F32, (None, rb, 1), lambda n, hh: (hh, n, 0))

    def scores(q_ref, k_ref, bg_ref, a_ref, aqk_ref, gam_ref):
        beta, g = _head_cols(bg_ref[...], pl.program_id(1), h)
        for i in range(cb):
            sl = slice(i * CHUNK, (i + 1) * CHUNK)
            gam, dm = _chunk_decay(g[sl])
            a_ref[sl, :], aqk_ref[sl, :] = _chunk_scores(q_ref[sl, :], k_ref[sl, :], beta[sl], dm)
            gam_ref[sl, :] = gam

    a, aqk, gam = _call(scores, grid=(s // rb, h), ins=[big(q), big(k), tok], outs=[o_sq, o_sq, o_col],
                        name="dn_scores")
    n_prob = h * (s // CHUNK)
    t_l = _tri_solve_lanes(jnp.transpose(a.reshape(n_prob, CHUNK * CHUNK)))
    t = jnp.transpose(t_l).reshape(h, s, CHUNK)

    def wy(k_ref, v_ref, bg_ref, gam_ref, t_ref, u_ref, w_ref):
        beta, _ = _head_cols(bg_ref[...], pl.program_id(1), h)
        for i in range(cb):
            sl = slice(i * CHUNK, (i + 1) * CHUNK)
            kb = k_ref[sl, :] * beta[sl]
            rhs = jnp.concatenate([v_ref[sl, :] * beta[sl], kb * jnp.exp(gam_ref[sl, :])], axis=1)
            uw = _mdot(t_ref[sl, :], rhs)
            u_ref[sl, :] = uw[:, :HEAD_DIM]
            w_ref[sl, :] = uw[:, HEAD_DIM:]

    u, w = _call(wy, grid=(s // rb, h), ins=[big(k), big(v), tok, col(gam), sq(t)], outs=[o_big, o_big],
                 name="dn_wy")
    return u, w, aqk, t, gam


def _dn_prep_bwd(q, k, v, bg, gam, t, du, dw, daqk, dqd, dkd, dgl):
    h, s, _ = q.shape
    cb = _prep_group(s)
    rb = cb * CHUNK

    def body(q_ref, k_ref, v_ref, bg_ref, g_ref, t_ref, du_ref, dw_ref, da_ref, dqd_ref, dkd_ref, dgl_ref,
             dq_ref, dk_ref, dv_ref, dbg_ref):
        hh = pl.program_id(1)
        beta, _ = _head_cols(bg_ref[...], hh, h)
        slices = [slice(i * CHUNK, (i + 1) * CHUNK) for i in range(cb)]
        results = _lockstep([_chunk_prep_bwd(
            q_ref[sl, :], k_ref[sl, :], v_ref[sl, :], beta[sl], g_ref[sl, :], t_ref[sl, :],
            du_ref[sl, :], dw_ref[sl, :], da_ref[sl, :], dqd_ref[sl, :], dkd_ref[sl, :], dgl_ref[sl, :])
            for sl in slices])

        @pl.when(hh == 0)
        def _():
            dbg_ref[...] = jnp.zeros_like(dbg_ref)

        lane = lax.broadcasted_iota(jnp.int32, (CHUNK, LANES), 1)
        for sl, (dq, dk, dv, dbeta, dg) in zip(slices, results):
            dq_ref[sl, :] = dq
            dk_ref[sl, :] = dk
            dv_ref[sl, :] = dv
            dbg_ref[sl, :] += jnp.where(lane == hh, dbeta, 0.0) + jnp.where(lane == h + hh, dg, 0.0)

    big = lambda x: (x, (None, rb, HEAD_DIM), lambda n, hh: (hh, n, 0))
    sq = lambda x: (x, (None, rb, CHUNK), lambda n, hh: (hh, n, 0))
    col = lambda x: (x, (None, rb, 1), lambda n, hh: (hh, n, 0))
    tok = (bg, (rb, LANES), lambda n, hh: (n, 0))
    o_big = ((h, s, HEAD_DIM), F32, (None, rb, HEAD_DIM), lambda n, hh: (hh, n, 0))
    return _call(body, grid=(s // rb, h),
                 ins=[big(q), big(k), big(v), tok, col(gam), sq(t), big(du), big(dw), sq(daqk), big(dqd), big(dkd),
                      col(dgl)],
                 outs=[o_big, o_big, o_big, ((s, LANES), F32, (rb, LANES), lambda n, hh: (n, 0))], name="dn_prep_bwd")


def _chunk_scaled(q, k, gam):
    gam_last = gam[CHUNK - 1:CHUNK, :]
    q_dec = q * (HEAD_DIM ** -0.5) * jnp.exp(gam)
    k_dec = k * jnp.exp(gam_last - gam)
    return q_dec, k_dec, jnp.exp(gam_last)


def _scan_group(s):
    return 2 if (s // CHUNK) % 2 == 0 else 1


def _dn_scan(q, k, u, w, aqk, gam):
    h, s, _ = q.shape
    nch = s // CHUNK
    sg = _scan_group(s)
    rb = sg * CHUNK

    def body(q_ref, k_ref, u_ref, w_ref, a_ref, gam_ref, o_ref, st_ref, state):
        @pl.when(pl.program_id(0) == 0)
        def _():
            state[...] = jnp.zeros_like(state)

        def head(hh, c):
            sl = slice(c * CHUNK, (c + 1) * CHUNK)
            s0 = state[hh]
            st_ref[c, hh] = s0
            q_dec, k_dec, gl = _chunk_scaled(q_ref[hh, sl, :], k_ref[hh, sl, :], gam_ref[hh, sl, :])
            both = _mdot(jnp.concatenate([w_ref[hh, sl, :], q_dec], axis=0), s0)
            yield
            v_new = u_ref[hh, sl, :] - both[:CHUNK]
            o_ref[sl, hh * HEAD_DIM:(hh + 1) * HEAD_DIM] = both[CHUNK:] + _mdot(a_ref[hh, sl, :], v_new)
            state[hh] = s0 * gl + _mdot(k_dec, v_new, TN)

        for c in range(sg):
            _lockstep([head(hh, c) for hh in range(h)])

    big = lambda x: (x, (h, rb, HEAD_DIM), lambda n: (0, n, 0))
    return _call(body, grid=(nch // sg,),
                 ins=[big(q), big(k), big(u), big(w), (aqk, (h, rb, CHUNK), lambda n: (0, n, 0)),
                      (gam, (h, rb, 1), lambda n: (0, n, 0))],
                 outs=[((s, h * HEAD_DIM), F32, (rb, h * HEAD_DIM), lambda n: (n, 0)),
                       ((nch, h, HEAD_DIM, HEAD_DIM), F32, (sg, h, HEAD_DIM, HEAD_DIM), lambda n: (n, 0, 0, 0))],
                 name="dn_scan", scratch=[pltpu.VMEM((h, HEAD_DIM, HEAD_DIM), F32)])


def _dn_scan_bwd(q, k, u, w, aqk, gam, states, do):
    h, s, _ = q.shape
    nch = s // CHUNK
    sg = _scan_group(s)
    rb = sg * CHUNK
    ngr = nch // sg

    def body(q_ref, k_ref, u_ref, w_ref, a_ref, gam_ref, st_ref, do_ref,
             du_ref, dw_ref, da_ref, dqd_ref, dkd_ref, dgl_ref, dstate):
        @pl.when(pl.program_id(0) == 0)
        def _():
            dstate[...] = jnp.zeros_like(dstate)

        def head(hh, c):
            sl = slice(c * CHUNK, (c + 1) * CHUNK)
            s0 = st_ref[c, hh]
            ds = dstate[hh]
            doh = do_ref[sl, hh * HEAD_DIM:(hh + 1) * HEAD_DIM]
            wv = w_ref[hh, sl, :]
            q_dec, k_dec, gl = _chunk_scaled(q_ref[hh, sl, :], k_ref[hh, sl, :], gam_ref[hh, sl, :])
            ws = _mdot(wv, s0)
            dv_new = _mdot(a_ref[hh, sl, :], doh, TN) + _mdot(k_dec, ds)
            dqd_ref[hh, sl, :] = _mdot(doh, s0, NT)
            qdo = _mdot(q_dec, doh, TN)
            tot = jnp.sum(jnp.sum(s0 * ds, axis=-1, keepdims=True), axis=0, keepdims=True)
            dgl_ref[hh, sl, :] = jnp.broadcast_to(tot, (CHUNK, 1))
            yield
            v_new = u_ref[hh, sl, :] - ws
            du_ref[hh, sl, :] = dv_new
            dw_ref[hh, sl, :] = -_mdot(dv_new, s0, NT)
            da_ref[hh, sl, :] = _mdot(doh, v_new, NT)
            dkd_ref[hh, sl, :] = _mdot(v_new, ds, NT)
            dstate[hh] = ds * gl + qdo - _mdot(wv, dv_new, TN)

        for c in range(sg - 1, -1, -1):
            _lockstep([head(hh, c) for hh in range(h)])

    rev = lambda n: (0, ngr - 1 - n, 0)
    big = lambda x: (x, (h, rb, HEAD_DIM), rev)
    o_big = ((h, s, HEAD_DIM), F32, (h, rb, HEAD_DIM), rev)
    return _call(body, grid=(ngr,),
                 ins=[big(q), big(k), big(u), big(w), (aqk, (h, rb, CHUNK), rev), (gam, (h, rb, 1), rev),
                      (states, (sg, h, HEAD_DIM, HEAD_DIM), lambda n: (ngr - 1 - n, 0, 0, 0)),
                      (do, (rb, h * HEAD_DIM), lambda n: (ngr - 1 - n, 0))],
                 outs=[o_big, o_big, ((h, s, CHUNK), F32, (h, rb, CHUNK), rev), o_big, o_big,
                       ((h, s, 1), F32, (h, rb, 1), rev)],
                 name="dn_scan_bwd", scratch=[pltpu.VMEM((h, HEAD_DIM, HEAD_DIM), F32)])


def _gates(x, a_log, dt_b, h):
    lane = lax.broadcasted_iota(jnp.int32, x.shape, 1)
    return jnp.where(lane < h, jax.nn.sigmoid(x), -jnp.exp(a_log) * jax.nn.softplus(x + dt_b))


def _head_out(oh, zh, nw):
    on = oh * lax.rsqrt(jnp.mean(oh * oh, axis=-1, keepdims=True) + RMS_EPS) * nw
    return on * jax.nn.silu(zh)


def _pad_lanes(x, lo):
    return jnp.zeros((1, LANES), F32).at[0, lo:lo + x.shape[0]].set(x)


def _deltanet_fwd(hin, get_w_in, conv_w, a_log, dt_bias, norm_w, get_w_out):
    h = a_log.shape[0]
    hw = h * HEAD_DIM
    w_in = get_w_in(hin)
    proj = _mm_nn(hin, w_in, F32, "dn_proj")
    q = _dn_conv_fwd(proj, conv_w, 0, h, True, "dn_conv_q")
    k = _dn_conv_fwd(proj, conv_w, h, h, True, "dn_conv_k")
    v = _dn_conv_fwd(proj, conv_w, 2 * h, h, False, "dn_conv_v")
    alp, dtp = _pad_lanes(a_log, h), _pad_lanes(dt_bias, h)

    def gates_fn(x, al, db):
        return (_gates(x, al, db, h),), ()

    (bg,), _ = _rowmap(gates_fn, [(proj, LANES, 4 * h)], [alp, dtp], [(LANES, F32)], [], "dn_gates")
    u, w, aqk, t, gam = _dn_prep(q, k, v, bg)
    o, states = _dn_scan(q, k, u, w, aqk, gam)
    nw = norm_w[None, :]

    def out_fn(o, z, nw):
        parts = [_head_out(o[:, i * HEAD_DIM:(i + 1) * HEAD_DIM], z[:, i * HEAD_DIM:(i + 1) * HEAD_DIM], nw)
                 for i in range(h)]
        return (jnp.concatenate(parts, axis=-1),), ()

    (og,), _ = _rowmap(out_fn, [o, (proj, hw, 3)], [nw], [(hw, MXU_DTYPE)], [], "dn_out")
    w_out = get_w_out(og)
    y = _mm_nn(og, w_out, F32, "dn_y")
    return y, (hin, proj, q, k, v, bg, u, w, aqk, t, gam, states, o, og, alp, dtp, nw, w_in, w_out)


def _deltanet_bwd(res, dy, conv_w, send):
    hin, proj, q, k, v, bg, u, w, aqk, t, gam, states, o, og, alp, dtp, nw, w_in, w_out = res
    h = q.shape[0]
    hw = h * HEAD_DIM
    s = hin.shape[0]
    d_w_out = _mm_tn(og, dy, MXU_DTYPE, "dn_dwout")
    dog = _mm_nt(dy, w_out, F32, "dn_dog")

    def out_bwd(o, z, dog, nw):
        dos, dzs = [], []
        dn = jnp.zeros((1, HEAD_DIM), F32)
        for i in range(h):
            sl = slice(i * HEAD_DIM, (i + 1) * HEAD_DIM)
            _, vjp = jax.vjp(_head_out, o[:, sl], z[:, sl], nw)
            a, b, c = vjp(dog[:, sl])
            dos.append(a)
            dzs.append(b)
            dn = dn + c
        return (jnp.concatenate(dos, axis=-1), jnp.concatenate(dzs, axis=-1)), (dn,)

    (do, dz), (d_norm_w,) = _rowmap(out_bwd, [o, (proj, hw, 3), dog], [nw], [(hw, F32), (hw, MXU_DTYPE)],
                                    [(1, HEAD_DIM)], "dn_out_bwd")
    du, dw, daqk, dqd, dkd, dgl = _dn_scan_bwd(q, k, u, w, aqk, gam, states, do)
    dq, dk, dv, dbg = _dn_prep_bwd(q, k, v, bg, gam, t, du, dw, daqk, dqd, dkd, dgl)
    dpq, dwq = _dn_conv_bwd(proj, conv_w, dq, 0, h, True, "dn_conv_q_bwd")
    dpk, dwk = _dn_conv_bwd(proj, conv_w, dk, h, h, True, "dn_conv_k_bwd")
    dpv, dwv = _dn_conv_bwd(proj, conv_w, dv, 2 * h, h, False, "dn_conv_v_bwd")

    def gates_bwd(x, dbg, al, db):
        _, vjp = jax.vjp(functools.partial(_gates, h=h), x, al, db)
        gx, gal, gdb = vjp(dbg)
        return (gx,), (gal, gdb)

    (dba,), (d_alp, d_dtp) = _rowmap(gates_bwd, [(proj, LANES, 4 * h), dbg], [alp, dtp], [(LANES, MXU_DTYPE)],
                                     [(1, LANES), (1, LANES)], "dn_gates_bwd")
    dproj = jnp.concatenate([dpq, dpk, dpv, dz, dba], axis=1)
    d_w_in = _mm_tn(hin, dproj, MXU_DTYPE, "dn_dwin")
    token = send(d_w_in, d_w_out)
    dh = _mm_nt(dproj, w_in, F32, "dn_dh", pin=token)
    d_conv_w = jnp.concatenate([dwq, dwk, dwv], axis=1)
    return dh, dict(conv_w=d_conv_w, a_log=d_alp[0, h:2 * h], dt_bias=d_dtp[0, h:2 * h], norm_w=d_norm_w[0]), token


def _ln_silu(u, g, b):
    return jax.nn.silu(_ln(u, g, b))


def _conformer_fwd(hin, get_w_in, dw_w, dw_b, ln_g, ln_b, get_w_out):
    w_in = get_w_in(hin)
    vg = _mm_nn(hin, w_in, F32, "cf_vg")
    u1 = _cf_conv_fwd(vg, dw_w, dw_b)
    ch = u1.shape[1]

    def fn(u, g, b):
        return (_ln_silu(u, g, b),), ()

    (u2,), _ = _rowmap(fn, [u1], [ln_g, ln_b], [(ch, MXU_DTYPE)], [], "cf_ln")
    w_out = get_w_out(u2)
    y = _mm_nn(u2, w_out, F32, "cf_y")
    return y, (hin, vg, u1, u2, w_in, w_out)


def _conformer_bwd(res, dy, dw_w, ln_g, ln_b):
    hin, vg, u1, u2, w_in, w_out = res
    ch = u1.shape[1]
    d_w_out = _mm_tn(u2, dy, MXU_DTYPE, "cf_dwout")
    du2 = _mm_nt(dy, w_out, F32, "cf_du2")

    def fn(u, du2, g, b):
        _, vjp = jax.vjp(_ln_silu, u, g, b)
        gu, gg, gb = vjp(du2)
        return (gu,), (gg, gb)

    (du1,), (d_ln_g, d_ln_b) = _rowmap(fn, [u1, du2], [ln_g, ln_b], [(ch, F32)], [(1, ch), (1, ch)], "cf_ln_bwd")
    dval, dgate, d_dw_w, d_dw_b = _cf_conv_bwd(vg, dw_w, du1)
    dvg = jnp.concatenate([dval, dgate], axis=1)
    d_w_in = _mm_tn(hin, dvg, MXU_DTYPE, "cf_dwin", split_cols=True)
    dh = _mm_nt(dvg, w_in, F32, "cf_dh")
    return dh, dict(w_in=d_w_in, w_out=d_w_out, dw_w=d_dw_w, dw_b=d_dw_b[0], ln_g=d_ln_g[0], ln_b=d_ln_b[0])


def _mlp_fwd(hin, get_w1, get_w2):
    w1 = get_w1(hin)
    r = _mm_nn(hin, w1, MXU_DTYPE, "ff_a", relu2=True)
    w2 = get_w2(r)
    m = _mm_nn(r, w2, F32, "ff_m")
    return m, (hin, r, w1, w2)


def _mlp_bwd(res, dm):
    hin, r, w1, w2 = res
    d_w2 = _mm_tn(r, dm, MXU_DTYPE, "ff_dw2")
    da = _mm_nt(dm, w2, MXU_DTYPE, "ff_da", relu2_sq=r)
    d_w1 = _mm_tn(hin, da, MXU_DTYPE, "ff_dw1", split_cols=True)
    dh = _mm_nt(da, w1, F32, "ff_dh")
    return dh, d_w1, d_w2


def _ada_fwd(c_all, ada_w):
    depth, d, nl = ada_w.shape
    tn = _tile(nl, 256)

    def body(c_ref, w_ref, o_ref, cond_ref):
        cond = jax.nn.silu(c_ref[...]).astype(MXU_DTYPE)
        cond_ref[...] = cond
        o_ref[...] = lax.dot_general(cond, w_ref[...].astype(MXU_DTYPE), (NN, ((), ())), preferred_element_type=F32)

    return _call(body, grid=(depth, nl // tn),
                 ins=[(c_all, c_all.shape, lambda l, j: (0, 0)), (ada_w, (None, d, tn), lambda l, j: (l, 0, j))],
                 outs=[((depth, N_DEV, nl), F32, (None, N_DEV, tn), lambda l, j: (l, 0, j)),
                       (c_all.shape, MXU_DTYPE, c_all.shape, lambda l, j: (0, 0))],
                 name="ada_fwd")


def _ada_bwd(cond_all, dmod_cols):
    depth, _, nl = dmod_cols.shape
    d = cond_all.shape[1]
    tn = _tile(nl, 256)

    def body(c_ref, g_ref, o_ref):
        o_ref[...] = lax.dot_general(c_ref[...], g_ref[...].astype(MXU_DTYPE), (TN, ((), ())),
                                     preferred_element_type=F32)

    return _call(body, grid=(depth, nl // tn),
                 ins=[(cond_all, cond_all.shape, lambda l, j: (0, 0)), (dmod_cols, (None, N_DEV, tn), lambda l, j: (l, 0, j))],
                 outs=[((depth, d, nl), F32, (None, d, tn), lambda l, j: (l, 0, j))], name="ada_bwd")[0]


def _peers():
    x, y, c = lax.axis_index("x"), lax.axis_index("y"), lax.axis_index("c")
    peers = []
    for k in range(1, N_DEV):
        px = 1 - x if k & 4 else x
        py = 1 - y if k & 2 else y
        pc = 1 - c if k & 1 else c
        peers.append(((px, py, pc), 4 * px + 2 * py + pc))
    return 4 * x + 2 * y + c, peers


_HBM = pl.BlockSpec(memory_space=pltpu.HBM)
_SEM = pl.BlockSpec(memory_space=pltpu.SEMAPHORE)
_ANY = pl.BlockSpec(memory_space=pl.ANY)
_EFFECT = pltpu.SideEffectType.DATAFLOW_SIDE_EFFECTING


def _xfer_start(srcs, lands, scatter, after, name):
    nt = len(srcs)

    def body(*refs):
        src, land = refs[:nt], refs[nt:2 * nt]
        sems = refs[2 * nt + 1:4 * nt + 1]
        token = refs[-1]
        me, peers = _peers()
        for t in range(nt):
            for k, (pid, plin) in enumerate(peers):
                pltpu.make_async_remote_copy(
                    src_ref=src[t].at[plin] if scatter else src[t], dst_ref=land[t].at[me],
                    send_sem=sems[2 * t].at[k], recv_sem=sems[2 * t + 1].at[k],
                    device_id=pid, device_id_type=pl.DeviceIdType.MESH).start()
        token[...] = jnp.zeros_like(token)

    out_shape = [pltpu.SemaphoreType.DMA((N_DEV - 1,)) for _ in range(2 * nt)]
    out_shape += [pltpu.HBM(a.shape, a.dtype) for a in lands]
    out_shape += [jax.ShapeDtypeStruct((8, LANES), F32)]
    srcs = [pltpu.with_memory_space_constraint(a, pltpu.HBM) for a in srcs]
    res = pl.pallas_call(
        body, name=name, out_shape=out_shape,
        in_specs=[_HBM] * (2 * nt) + [_ANY],
        out_specs=[_SEM] * (2 * nt) + [_HBM] * nt + [pl.BlockSpec(memory_space=pltpu.VMEM)],
        input_output_aliases={nt + i: 2 * nt + i for i in range(nt)},
        compiler_params=pltpu.CompilerParams(has_side_effects=_EFFECT),
    )(*srcs, *[pltpu.with_memory_space_constraint(a, pltpu.HBM) for a in lands], after)
    sems, thru = res[:2 * nt], res[2 * nt:3 * nt]
    return [(sems[2 * t], sems[2 * t + 1], srcs[t], thru[t]) for t in range(nt)], res[-1]


def _xfer_wait(handle, scatter, after, name):
    send, recv, src, land = handle

    def body(src_ref, land_ref, send_sem, recv_sem, after_ref, land_out):
        _, peers = _peers()
        for k, (pid, plin) in enumerate(peers):
            cp = pltpu.make_async_remote_copy(
                src_ref=src_ref.at[plin] if scatter else src_ref, dst_ref=land_ref.at[plin],
                send_sem=send_sem.at[k], recv_sem=recv_sem.at[k],
                device_id=pid, device_id_type=pl.DeviceIdType.MESH)
            cp.wait_send()
            cp.wait_recv()

    return pl.pallas_call(
        body, name=name, out_shape=pltpu.HBM(land.shape, land.dtype),
        in_specs=(_HBM, _HBM, _SEM, _SEM, _ANY), out_specs=_HBM, input_output_aliases={1: 0},
        compiler_params=pltpu.CompilerParams(has_side_effects=_EFFECT),
    )(src, land, send, recv, after)


def _landing(x, me):
    return lax.dynamic_update_slice(lax.empty((N_DEV,) + x.shape, x.dtype), x[None], (me,) + (0,) * x.ndim)


def _chip_peers():
    x, y, c = lax.axis_index("x"), lax.axis_index("y"), lax.axis_index("c")
    lin = lambda px, py, pc: 4 * px + 2 * py + pc
    sibling = ((x, y, 1 - c), lin(x, y, 1 - c))
    chips = [((1 - x, y, c), lin(1 - x, y, c)), ((x, 1 - y, c), lin(x, 1 - y, c)),
             ((1 - x, 1 - y, c), lin(1 - x, 1 - y, c))]
    return lin(x, y, c), sibling, chips


N_CHIPS_OTHER = 3


def _gather2_start(srcs, lands, after, name):
    nt = len(srcs)

    def body(*refs):
        src, land = refs[:nt], refs[nt:2 * nt]
        sems = refs[2 * nt + 1:5 * nt + 1]
        token = refs[-1]
        me, sibling, chips = _chip_peers()
        for t in range(nt):
            send, recv_ici, recv_sib = sems[3 * t], sems[3 * t + 1], sems[3 * t + 2]
            pltpu.make_async_remote_copy(src_ref=src[t], dst_ref=land[t].at[me], send_sem=send.at[0],
                                         recv_sem=recv_sib.at[0], device_id=sibling[0],
                                         device_id_type=pl.DeviceIdType.MESH).start()
            for j, (pid, _) in enumerate(chips):
                pltpu.make_async_remote_copy(src_ref=src[t], dst_ref=land[t].at[me], send_sem=send.at[1 + j],
                                             recv_sem=recv_ici.at[j], device_id=pid,
                                             device_id_type=pl.DeviceIdType.MESH).start()
        token[...] = jnp.zeros_like(token)

    out_shape = []
    for _ in range(nt):
        out_shape += [pltpu.SemaphoreType.DMA((1 + N_CHIPS_OTHER,)), pltpu.SemaphoreType.DMA((N_CHIPS_OTHER,)),
                      pltpu.SemaphoreType.DMA((1,))]
    out_shape += [pltpu.HBM(a.shape, a.dtype) for a in list(srcs) + list(lands)]
    out_shape += [jax.ShapeDtypeStruct((8, LANES), F32)]
    res = pl.pallas_call(
        body, name=name, out_shape=out_shape,
        in_specs=[_HBM] * (2 * nt) + [_ANY],
        out_specs=[_SEM] * (3 * nt) + [_HBM] * (2 * nt) + [pl.BlockSpec(memory_space=pltpu.VMEM)],
        input_output_aliases={i: 3 * nt + i for i in range(2 * nt)},
        compiler_params=pltpu.CompilerParams(has_side_effects=_EFFECT),
    )(*[pltpu.with_memory_space_constraint(a, pltpu.HBM) for a in list(srcs) + list(lands)], after)
    sems, thru = res[:3 * nt], res[3 * nt:5 * nt]
    return [(sems[3 * t], sems[3 * t + 1], sems[3 * t + 2], thru[t], thru[nt + t]) for t in range(nt)], res[-1]


def _gather2_relay(handles, after, name):
    nt = len(handles)

    def body(*refs):
        src, land = refs[:nt], refs[nt:2 * nt]
        send1, recv_ici = refs[2 * nt:3 * nt], refs[3 * nt:4 * nt]
        outs = refs[4 * nt + 1:]
        send2, recv2 = outs[:nt], outs[nt:2 * nt]
        me, sibling, chips = _chip_peers()
        for t in range(nt):
            pltpu.make_async_remote_copy(src_ref=src[t], dst_ref=land[t].at[me], send_sem=send1[t].at[0],
                                         recv_sem=recv_ici[t].at[0], device_id=sibling[0],
                                         device_id_type=pl.DeviceIdType.MESH).wait_send()
            for j, (pid, plin) in enumerate(chips):
                arrived = pltpu.make_async_remote_copy(src_ref=src[t], dst_ref=land[t].at[plin], send_sem=send1[t].at[1 + j],
                                                       recv_sem=recv_ici[t].at[j], device_id=pid,
                                                       device_id_type=pl.DeviceIdType.MESH)
                arrived.wait_send()
                arrived.wait_recv()
                pltpu.make_async_remote_copy(src_ref=land[t].at[plin], dst_ref=land[t].at[plin], send_sem=send2[t].at[j],
                                             recv_sem=recv2[t].at[j], device_id=sibling[0],
                                             device_id_type=pl.DeviceIdType.MESH).start()

    srcs = [h[3] for h in handles]
    lands = [h[4] for h in handles]
    out_shape = [pltpu.SemaphoreType.DMA((N_CHIPS_OTHER,)) for _ in range(2 * nt)]
    out_shape += [pltpu.HBM(a.shape, a.dtype) for a in srcs + lands]
    res = pl.pallas_call(
        body, name=name, out_shape=out_shape,
        in_specs=[_HBM] * (2 * nt) + [_SEM] * (2 * nt) + [_ANY],
        out_specs=[_SEM] * (2 * nt) + [_HBM] * (2 * nt),
        input_output_aliases={i: 2 * nt + i for i in range(2 * nt)},
        compiler_params=pltpu.CompilerParams(has_side_effects=_EFFECT),
    )(*srcs, *lands, *[h[0] for h in handles], *[h[1] for h in handles], after)
    return [(handles[t][2], res[t], res[nt + t], res[3 * nt + t]) for t in range(nt)]


def _gather2_wait(handle, after, name):
    recv_sib, send2, recv2, land = handle

    def body(land_ref, recv_sib_sem, send2_sem, recv2_sem, after_ref, land_out):
        me, sibling, chips = _chip_peers()
        pltpu.make_async_remote_copy(src_ref=land_ref.at[me], dst_ref=land_ref.at[sibling[1]], send_sem=send2_sem.at[0],
                                     recv_sem=recv_sib_sem.at[0], device_id=sibling[0],
                                     device_id_type=pl.DeviceIdType.MESH).wait_recv()
        for j, (pid, plin) in enumerate(chips):
            relayed = pltpu.make_async_remote_copy(src_ref=land_ref.at[plin], dst_ref=land_ref.at[plin], send_sem=send2_sem.at[j],
                                                   recv_sem=recv2_sem.at[j], device_id=sibling[0],
                                                   device_id_type=pl.DeviceIdType.MESH)
            relayed.wait_send()
            relayed.wait_recv()

    return pl.pallas_call(
        body, name=name, out_shape=pltpu.HBM(land.shape, land.dtype),
        in_specs=(_HBM, _SEM, _SEM, _SEM, _ANY), out_specs=_HBM, input_output_aliases={0: 0},
        compiler_params=pltpu.CompilerParams(has_side_effects=_EFFECT),
    )(land, recv_sib, send2, recv2, after)


def _exchange(arrs, scatter, name):
    nt = len(arrs)
    out_shape = [jax.ShapeDtypeStruct(a.shape if scatter else (N_DEV,) + a.shape, a.dtype) for a in arrs]

    def body(*refs):
        ins, outs = refs[:nt], refs[nt:2 * nt]
        send, recv, loc = refs[2 * nt:]
        me, peers = _peers()
        copies = []
        for t in range(nt):
            own = pltpu.make_async_copy(ins[t].at[me] if scatter else ins[t], outs[t].at[me], loc.at[t])
            own.start()
            copies.append(own)
            for k, (pid, plin) in enumerate(peers):
                cp = pltpu.make_async_remote_copy(
                    src_ref=ins[t].at[plin] if scatter else ins[t], dst_ref=outs[t].at[me],
                    send_sem=send.at[t, k], recv_sem=recv.at[t, k],
                    device_id=pid, device_id_type=pl.DeviceIdType.MESH)
                cp.start()
                copies.append(cp)
        for cp in copies:
            cp.wait()

    any_spec = pl.BlockSpec(memory_space=pl.ANY)
    return pl.pallas_call(
        body, out_shape=out_shape, in_specs=[any_spec] * nt, out_specs=[any_spec] * nt,
        scratch_shapes=[pltpu.SemaphoreType.DMA((nt, N_DEV - 1)), pltpu.SemaphoreType.DMA((nt, N_DEV - 1)),
                        pltpu.SemaphoreType.DMA((nt,))],
        name=name)(*arrs)


def _adamw_body(n_parts, stacked=True):
    def body(p_ref, w_ref, m_ref, v_ref, *rest):
        g_out, d_out, m_out, v_out = rest[-4:]
        part = (lambda i: p_ref[i]) if stacked else (lambda i: p_ref[i][...])
        g = part(0).astype(F32)
        for i in range(1, n_parts):
            g = g + part(i).astype(F32)
        m2 = ADAM_B1 * m_ref[...] + (1.0 - ADAM_B1) * g
        v2 = ADAM_B2 * v_ref[...] + (1.0 - ADAM_B2) * jnp.square(g)
        m_hat = m2 / (1.0 - ADAM_B1 ** ADAM_STEP)
        v_hat = v2 / (1.0 - ADAM_B2 ** ADAM_STEP)
        g_out[...] = g
        d_out[...] = -ADAM_LR * (m_hat / (jnp.sqrt(v_hat) + ADAM_EPS) + ADAM_WD * w_ref[...])
        m_out[...] = m2
        v_out[...] = v2

    return body


def _adamw_layer(own, land, me, w, m, v, layer, prev, name):
    _, r, c = own.shape
    tr = _tile(r, 256, 8)
    blk = pl.BlockSpec((None, tr, c), lambda i, me_ref: (layer, i, 0))
    share = lambda k: pl.BlockSpec((None, tr, c), lambda i, me_ref: (me_ref[0] ^ k, i, 0))
    in_specs = [share(k) for k in range(N_DEV)] + [blk, blk, blk]
    args = [own] + [land] * (N_DEV - 1) + [w, m, v]
    aliases = {}
    if prev is not None:
        in_specs += [_ANY] * 4
        args += list(prev)
        aliases = {1 + N_DEV + 3 + i: i for i in range(4)}

    def body(me_ref, *refs):
        token_ref = refs[-1]
        refs = (refs[:N_DEV],) + refs[N_DEV:-1]
        _adamw_body(N_DEV, stacked=False)(*refs)
        token_ref[...] = jnp.zeros(token_ref.shape, F32)

    token_blk = pl.BlockSpec((8, LANES), lambda i, me_ref: (0, 0))
    res = pl.pallas_call(
        body,
        grid_spec=pltpu.PrefetchScalarGridSpec(num_scalar_prefetch=1, grid=(r // tr,), in_specs=in_specs,
                                               out_specs=[blk] * 4 + [token_blk]),
        out_shape=[jax.ShapeDtypeStruct(w.shape, F32)] * 4 + [jax.ShapeDtypeStruct((8, LANES), F32)],
        input_output_aliases=aliases, name=name, compiler_params=_cparams(1))(me, *args)
    return res[:4], res[4]


def _adamw(parts, w, m, v, name):
    p, nl, r, c = parts.shape
    tr = _tile(r, 256, 8)
    body = _adamw_body(p)

    blk = (None, tr, c)
    imap = lambda l, i: (l, i, 0)
    out = ((nl, r, c), F32, blk, imap)
    return _call(body, grid=(nl, r // tr),
                 ins=[(parts, (p, None, tr, c), lambda l, i: (0, l, i, 0)), (w, blk, imap), (m, blk, imap), (v, blk, imap)],
                 outs=[out] * 4, name=name)


def _rows(x):
    return x.reshape(-1, LANES)


def _pad_rows(x, mult=8):
    r = x.shape[0]
    extra = (-r) % mult
    return jnp.pad(x, ((0, extra), (0, 0))) if extra else x


def kernel(x, c, ada_w, ada_b, ln_g, ln_b, dn_w_in, dn_conv_w, dn_a_log, dn_dt_bias, dn_norm_w, dn_w_out, cf_w_in, cf_dw_w, cf_dw_b, cf_ln_g, cf_ln_b, cf_w_out, ff_w1, ff_w2, loss_target, m_ada_w, m_ada_b, m_ln_g, m_ln_b, m_dn_w_in, m_dn_conv_w, m_dn_a_log, m_dn_dt_bias, m_dn_norm_w, m_dn_w_out, m_cf_w_in, m_cf_dw_w, m_cf_dw_b, m_cf_ln_g, m_cf_ln_b, m_cf_w_out, m_ff_w1, m_ff_w2, v_ada_w, v_ada_b, v_ln_g, v_ln_b, v_dn_w_in, v_dn_conv_w, v_dn_a_log, v_dn_dt_bias, v_dn_norm_w, v_dn_w_out, v_cf_w_in, v_cf_dw_w, v_cf_dw_b, v_cf_ln_g, v_cf_ln_b, v_cf_w_out, v_ff_w1, v_ff_w2):
    depth, d, _ = ada_w.shape
    n_a, n_b = dn_w_in.shape[0], cf_w_in.shape[0]
    heads = dn_a_log.shape[1]
    hw = heads * HEAD_DIM
    taps = cf_dw_w.shape[1]
    s = x.shape[1]
    alpha = (2.0 * depth) ** 0.25
    me = 4 * lax.axis_index("x") + 2 * lax.axis_index("y") + lax.axis_index("c")
    me_arr = jnp.reshape(me, (1,)).astype(jnp.int32)
    xs, tgt = x[0], loss_target[0]

    dn_in_cols = dn_w_in.shape[2]
    keys, shards = [], []
    for i in range(depth):
        j = i // 2
        mixer = [("dn_in", dn_w_in), ("dn_out", dn_w_out)] if i % 2 == 0 else [("cf_in", cf_w_in), ("cf_out", cf_w_out)]
        for nm, wt in mixer:
            keys.append((nm, j))
            shards.append(wt[j].astype(MXU_DTYPE))
        keys += [("ff1", i), ("ff2", i)]
        shards += [ff_w1[i].astype(MXU_DTYPE), ff_w2[i].astype(MXU_DTYPE)]

    small_local = [_rows(ln_g), _rows(ln_b), _rows(dn_conv_w), _rows(cf_dw_w), _rows(cf_dw_b), _rows(cf_ln_g),
                   _rows(cf_ln_b), _rows(c)]
    sizes = [a.shape[0] for a in small_local]
    packed = _pad_rows(jnp.concatenate(small_local, axis=0))
    (small_all,) = _exchange([packed], False, "comm_gather_params")
    offs = [0]
    for z in sizes:
        offs.append(offs[-1] + z)

    def small(i):
        return small_all[:, offs[i]:offs[i + 1], :]

    def unshard(piece, lead, groups):
        t = piece.reshape((N_DEV,) + lead + (groups * LANES,))
        t = jnp.moveaxis(t, 0, len(lead))
        return t.reshape(lead + (N_DEV * groups * LANES,))

    ln_g_f = unshard(small(0), (depth, 2), 1)
    ln_b_f = unshard(small(1), (depth, 2), 1)
    conv_w_f = unshard(small(2), (n_a, DN_CONV), 3 * heads // N_DEV)
    dw_w_f = unshard(small(3), (n_b, taps), 1)
    dw_b_f = unshard(small(4), (n_b,), 1)
    cf_ln_g_f = unshard(small(5), (n_b,), 1)
    cf_ln_b_f = unshard(small(6), (n_b,), 1)
    c_all = small(7).reshape(N_DEV, d)

    mod_part, cond_all = _ada_fwd(c_all, ada_w)
    (mod_all,) = _exchange([mod_part], False, "comm_gather_mod")
    mod_mine = lax.dynamic_index_in_dim(mod_all, me, axis=2, keepdims=False)
    mod_mine = jnp.moveaxis(mod_mine, 0, 1).reshape(depth, N_MOD * d)

    handles, token = _gather2_start(shards, [_landing(a, me) for a in shards], mod_all, "gather_weights_start")
    handles = dict(zip(keys, handles))
    groups = [keys[:1], keys[1:4]] + [keys[4 * i:4 * i + 4] for i in range(1, depth)]
    group_of = {k: n for n, grp in enumerate(groups) for k in grp}
    relayed, weights = {}, {}

    def relay(n, after):
        if n < len(groups) and groups[n][0] not in relayed:
            hs = _gather2_relay([handles[k] for k in groups[n]], after, "gather_relay_%d" % n)
            relayed.update(zip(groups[n], hs))

    relay(0, token)

    def gathered(key, after):
        if key not in weights:
            relay(group_of[key], after)
            if key[0] == "ff1":
                relay(key[1] + 2, after)
            weights[key] = _gather2_wait(relayed[key], after, "gather_wait_%s_%d" % key)
        return weights[key]

    def get_dn_in(j):
        def get(after):
            g = gathered(("dn_in", j), after)
            w = jnp.moveaxis(g, 0, 1).reshape(d, N_DEV * dn_in_cols)
            return jnp.pad(w, ((0, 0), (0, 4 * hw + LANES - N_DEV * dn_in_cols)))
        return get

    def get_rows(key):
        return lambda after: gathered(key, after).reshape((-1, d))

    def get_cols(key):
        return lambda after: gathered(key, after)

    def add_bias(a, b):
        return (a + b,), ()

    (mod,), _ = _rowmap(add_bias, [mod_mine, ada_b], [], [(N_MOD * d, F32)], [], "ada_bias", pin=token)
    mod_rows = mod.reshape(depth * N_MOD, 1, d)
    ln_g_rows = ln_g_f.reshape(depth * 2, 1, d)
    ln_b_rows = ln_b_f.reshape(depth * 2, 1, d)

    def mod_row(i, j):
        return (mod_rows, i * N_MOD + j)

    def ln_row(rows, i, j):
        return (rows, i * 2 + j)

    subs = []
    h_cur = _modulate_fwd(xs, mod_row(0, 1), mod_row(0, 0))
    x_cur = xs
    last = None
    for i in range(depth):
        j = i // 2
        if i % 2 == 0:
            y, res = _deltanet_fwd(h_cur, get_dn_in(j), conv_w_f[j], dn_a_log[j], dn_dt_bias[j], dn_norm_w[j],
                                   get_rows(("dn_out", j)))
        else:
            y, res = _conformer_fwd(h_cur, get_cols(("cf_in", j)), dw_w_f[j], dw_b_f[j][None, :], cf_ln_g_f[j][None, :],
                                    cf_ln_b_f[j][None, :], get_rows(("cf_out", j)))
        p1 = (mod_row(i, 2), ln_row(ln_g_rows, i, 0), ln_row(ln_b_rows, i, 0), mod_row(i, 4), mod_row(i, 3))
        x_mid, h_mid = _combine_fwd(alpha, x_cur, y, *p1)
        subs.append((x_cur, y, p1, res))
        m_out, res2 = _mlp_fwd(h_mid, get_cols(("ff1", i)), get_rows(("ff2", i)))
        if i + 1 < depth:
            p2 = (mod_row(i, 5), ln_row(ln_g_rows, i, 1), ln_row(ln_b_rows, i, 1), mod_row(i + 1, 1), mod_row(i + 1, 0))
            x_next, h_next = _combine_fwd(alpha, x_mid, m_out, *p2)
            subs.append((x_mid, m_out, p2, res2))
            x_cur, h_cur = x_next, h_next
        else:
            p2 = (mod_row(i, 5), ln_row(ln_g_rows, i, 1), ln_row(ln_b_rows, i, 1))
            last = (x_mid, m_out, p2, res2)

    x_in, y_in, p_last, res_last = last
    dx, dy, (loss_acc, g_gt, g_g, g_b) = _last_fwd_bwd(alpha, x_in, y_in, tgt, *p_last)
    loss = lax.psum(loss_acc[0, 0], ("x", "y", "c"))

    d_mod = [[None] * N_MOD for _ in range(depth)]
    d_ln_g = [[None, None] for _ in range(depth)]
    d_ln_b = [[None, None] for _ in range(depth)]
    d_mod[depth - 1][5], d_ln_g[depth - 1][1], d_ln_b[depth - 1][1] = g_gt, g_g, g_b
    gw = dict(dn=[None] * n_a, cf=[None] * n_b)

    sent = {}

    def send_grads(named, tag):
        parts = [p for _, p in named]
        hs, tok = _xfer_start(parts, [lax.empty(p.shape, p.dtype) for p in parts], True, parts[0], "scatter_start_" + tag)
        for (key, _), hnd in zip(named, hs):
            sent[key] = hnd
        return tok

    def by_rows(g):
        return g.reshape((N_DEV, g.shape[0] // N_DEV, g.shape[1]))

    def send_mlp(i, d_w1, d_w2):
        return send_grads([(("ff1", i), d_w1), (("ff2", i), by_rows(d_w2))], "ff_%d" % i)

    dh, d_w1, d_w2 = _mlp_bwd(res_last, dy)
    pin = send_mlp(depth - 1, d_w1, d_w2)
    for idx in range(len(subs) - 1, -1, -1):
        x_in, y_in, prm, res = subs[idx]
        i, second = idx // 2, idx % 2
        dx, dy, (g_gt, g_g, g_b, g_sc, g_sh) = _combine_bwd(alpha, x_in, y_in, dx, dh, *prm, pin=pin)
        d_mod[i][5 if second else 2], d_ln_g[i][second], d_ln_b[i][second] = g_gt, g_g, g_b
        nxt_i, nxt_base = (i + 1, 0) if second else (i, 3)
        d_mod[nxt_i][nxt_base + 1], d_mod[nxt_i][nxt_base] = g_sc, g_sh
        j = i // 2
        if second:
            dh, d_w1, d_w2 = _mlp_bwd(res, dy)
            pin = send_mlp(i, d_w1, d_w2)
        elif i % 2 == 0:
            def send_dn(d_w_in, d_w_out, j=j):
                d_in = d_w_in[:, :N_DEV * dn_in_cols].reshape(d, N_DEV, dn_in_cols)
                return send_grads([(("dn_in", j), jnp.moveaxis(d_in, 1, 0)), (("dn_out", j), by_rows(d_w_out))],
                                  "dn_%d" % j)

            dh, gw["dn"][j], pin = _deltanet_bwd(res, dy, conv_w_f[j], send_dn)
        else:
            dh, gw["cf"][j] = _conformer_bwd(res, dy, dw_w_f[j], cf_ln_g_f[j][None, :], cf_ln_b_f[j][None, :])
            pin = send_grads([(("cf_in", j), gw["cf"][j]["w_in"]), (("cf_out", j), by_rows(gw["cf"][j]["w_out"]))],
                             "cf_%d" % j)
    grad_x, g_sc, g_sh = _modulate_bwd(xs, dx, dh, mod_row(0, 1), mod_row(0, 0), pin=pin)
    d_mod[0][1], d_mod[0][0] = g_sc, g_sh
    d_mod_full = jnp.concatenate([jnp.concatenate(r, axis=1) for r in d_mod], axis=0)

    stacked = {"dn_w_in": ("dn_in", dn_w_in, m_dn_w_in, v_dn_w_in), "dn_w_out": ("dn_out", dn_w_out, m_dn_w_out, v_dn_w_out),
               "cf_w_in": ("cf_in", cf_w_in, m_cf_w_in, v_cf_w_in), "cf_w_out": ("cf_out", cf_w_out, m_cf_w_out, v_cf_w_out),
               "ff_w1": ("ff1", ff_w1, m_ff_w1, v_ff_w1), "ff_w2": ("ff2", ff_w2, m_ff_w2, v_ff_w2)}
    chains = {key: None for key in stacked}

    def update_layer(i, token):
        mixer = ["dn_w_in", "dn_w_out"] if i % 2 == 0 else ["cf_w_in", "cf_w_out"]
        for key, idx in [("ff_w1", i), ("ff_w2", i)] + [(k, i // 2) for k in mixer]:
            short, w, m, v = stacked[key]
            land = _xfer_wait(sent[(short, idx)], True, token, "scatter_wait_%s_%d" % (short, idx))
            chains[key], token = _adamw_layer(sent[(short, idx)][2], land, me_arr, w, m, v, idx, chains[key],
                                              "adamw_%s_%d" % (key, idx))
        return token

    def stack_rows(lst):
        return jnp.stack(lst, axis=0)

    gs_ln_g = jnp.stack([jnp.concatenate(r, axis=0) for r in d_ln_g], axis=0)
    gs_ln_b = jnp.stack([jnp.concatenate(r, axis=0) for r in d_ln_b], axis=0)
    gs_conv_w = stack_rows([gw["dn"][j]["conv_w"] for j in range(n_a)])
    gs_dw_w = stack_rows([gw["cf"][j]["dw_w"] for j in range(n_b)])
    gs_dw_b = stack_rows([gw["cf"][j]["dw_b"] for j in range(n_b)])
    gs_cf_ln_g = stack_rows([gw["cf"][j]["ln_g"] for j in range(n_b)])
    gs_cf_ln_b = stack_rows([gw["cf"][j]["ln_b"] for j in range(n_b)])
    gs_a_log = stack_rows([_pad_lanes(gw["dn"][j]["a_log"], 0)[0] for j in range(n_a)])
    gs_dt_bias = stack_rows([_pad_lanes(gw["dn"][j]["dt_bias"], 0)[0] for j in range(n_a)])
    gs_norm_w = stack_rows([gw["dn"][j]["norm_w"] for j in range(n_a)])
    small_grads = [gs_ln_g, gs_ln_b, gs_conv_w, gs_dw_w, gs_dw_b, gs_cf_ln_g, gs_cf_ln_b, gs_a_log, gs_dt_bias,
                   gs_norm_w, d_mod_full]
    sg_rows = [_rows(a) for a in small_grads]
    sg_sizes = [a.shape[0] for a in sg_rows]
    sg_packed = _pad_rows(jnp.concatenate(sg_rows, axis=0))
    (sg_handle,), sg_token = _xfer_start([sg_packed], [_landing(sg_packed, me)], False, grad_x, "gather_small_grads_start")
    for i in range(depth - 1, -1, -1):
        sg_token = update_layer(i, sg_token)
    sg_all = _xfer_wait(sg_handle, False, sg_token, "gather_small_grads_wait")
    sg_offs = [0]
    for z in sg_sizes:
        sg_offs.append(sg_offs[-1] + z)

    def sg(i, shape):
        return sg_all[:, sg_offs[i]:sg_offs[i + 1], :].reshape((N_DEV,) + shape)

    dmod_all = sg(10, (depth, N_MOD * d))
    nl = ada_w.shape[2]
    dmod_cols = lax.dynamic_slice_in_dim(dmod_all, me * nl, nl, axis=2)
    g_ada_w = _ada_bwd(cond_all, jnp.moveaxis(dmod_cols, 0, 1))

    outs = {}

    def run_adamw(key, parts, w, m, v):
        shp = w.shape
        as3 = lambda t: t.reshape((-1,) + shp[-2:]) if t.ndim >= 3 else t.reshape((1,) + shp)
        parts3 = parts.reshape((parts.shape[0],) + as3(w).shape)
        res = _adamw(parts3, as3(w), as3(m), as3(v), "adamw_" + key)
        outs[key] = tuple(r.reshape(shp) for r in res)

    run_adamw("ada_w", g_ada_w[None], ada_w, m_ada_w, v_ada_w)

    cgroups = 3 * heads // N_DEV
    n_sharded = 7

    def my_cols(first, last, groups):
        x = sg_all[:, sg_offs[first]:sg_offs[last], :].reshape(N_DEV, -1, N_DEV, groups, LANES)
        return lax.dynamic_index_in_dim(x, me, axis=2, keepdims=False).reshape(N_DEV, -1, LANES)

    small_parts = [my_cols(0, 2, 1), my_cols(2, 3, cgroups), my_cols(3, n_sharded, 1),
                   sg_all[:, sg_offs[n_sharded]:sg_offs[-1], :]]
    sp_offs = [0]
    for n, z in enumerate(sg_sizes):
        sp_offs.append(sp_offs[-1] + (z // N_DEV if n < n_sharded else z))
    parts_packed = jnp.zeros((N_DEV, sp_offs[-1] + (-sp_offs[-1]) % 8, LANES), F32)
    at = 0
    for part in small_parts:
        parts_packed = lax.dynamic_update_slice(parts_packed, part, (0, at, 0))
        at += part.shape[1]

    def pad_heads(t):
        return jnp.pad(t, ((0, 0), (0, LANES - heads)))

    def pack_state(ln_g_, ln_b_, conv_w_, dw_w_, dw_b_, cln_g_, cln_b_, a_log_, dt_b_, norm_w_, ada_b_):
        rows = [_rows(ln_g_), _rows(ln_b_), _rows(conv_w_), _rows(dw_w_), _rows(dw_b_), _rows(cln_g_), _rows(cln_b_),
                pad_heads(a_log_), pad_heads(dt_b_), norm_w_, _rows(ada_b_)]
        return _pad_rows(jnp.concatenate(rows, axis=0))

    w_s = pack_state(ln_g, ln_b, dn_conv_w, cf_dw_w, cf_dw_b, cf_ln_g, cf_ln_b, dn_a_log, dn_dt_bias, dn_norm_w, ada_b)
    m_s = pack_state(m_ln_g, m_ln_b, m_dn_conv_w, m_cf_dw_w, m_cf_dw_b, m_cf_ln_g, m_cf_ln_b, m_dn_a_log,
                     m_dn_dt_bias, m_dn_norm_w, m_ada_b)
    v_s = pack_state(v_ln_g, v_ln_b, v_dn_conv_w, v_cf_dw_w, v_cf_dw_b, v_cf_ln_g, v_cf_ln_b, v_dn_a_log,
                     v_dn_dt_bias, v_dn_norm_w, v_ada_b)
    res_s = _adamw(parts_packed[:, None], w_s[None], m_s[None], v_s[None], "adamw_small")
    small_keys = ["ln_g", "ln_b", "dn_conv_w", "cf_dw_w", "cf_dw_b", "cf_ln_g", "cf_ln_b", "dn_a_log", "dn_dt_bias",
                  "dn_norm_w", "ada_b"]
    small_shapes = [ln_g.shape, ln_b.shape, dn_conv_w.shape, cf_dw_w.shape, cf_dw_b.shape, cf_ln_g.shape,
                    cf_ln_b.shape, dn_a_log.shape, dn_dt_bias.shape, dn_norm_w.shape, ada_b.shape]
    for n, (key, shp) in enumerate(zip(small_keys, small_shapes)):
        vals = []
        for r in res_s:
            piece = r[0, sp_offs[n]:sp_offs[n + 1], :]
            if key in ("dn_a_log", "dn_dt_bias"):
                piece = piece[:, :heads]
            vals.append(piece.reshape(shp))
        outs[key] = tuple(vals)

    for key in stacked:
        outs[key] = tuple(chains[key])

    order = ["ada_w", "ada_b", "ln_g", "ln_b", "dn_w_in", "dn_conv_w", "dn_a_log", "dn_dt_bias", "dn_norm_w",
             "dn_w_out", "cf_w_in", "cf_dw_w", "cf_dw_b", "cf_ln_g", "cf_ln_b", "cf_w_out", "ff_w1", "ff_w2"]
    result = [loss, grad_x[None]]
    for part in range(4):
        result += [outs[k][part] for k in order]
    return tuple(result)
```

```python
import functools

import jax
import jax.numpy as jnp
from jax import lax
from jax.experimental import pallas as pl
from jax.experimental.pallas import tpu as pltpu

F32 = jnp.float32
MXU_DTYPE = jnp.bfloat16
N_DEV = 8
LANES = 128
HEAD_DIM = 128
CHUNK = 64
DN_CONV = 4
N_MOD = 6
LN_EPS = 1e-5
RMS_EPS = 1e-6
L2_EPS = 1e-6
ADAM_LR = 0.001
ADAM_B1 = 0.9
ADAM_B2 = 0.999
ADAM_EPS = 1e-08
ADAM_WD = 0.01
ADAM_STEP = 10

NN = ((1,), (0,))
NT = ((1,), (1,))
TN = ((0,), (0,))

ROW_TILE = 512
CONV_TILE = 256
SHORT_CONV_TILE = 1024


def _mdot(a, b, dims=NN):
    return lax.dot_general(a.astype(MXU_DTYPE), b.astype(MXU_DTYPE), (dims, ((), ())), preferred_element_type=F32)


def _split3(x):
    hi = x.astype(MXU_DTYPE)
    r1 = x - hi.astype(F32)
    mid = r1.astype(MXU_DTYPE)
    lo = (r1 - mid.astype(F32)).astype(MXU_DTYPE)
    return hi, mid, lo


def _dot01(a, b, dims=NN, mask_first=True):
    d = lambda p, q: lax.dot_general(p, q, (dims, ((), ())), preferred_element_type=F32)
    if mask_first:
        m = a.astype(MXU_DTYPE)
        return sum(d(m, p) for p in _split3(b))
    m = b.astype(MXU_DTYPE)
    return sum(d(p, m) for p in _split3(a))


def _cparams(n):
    return pltpu.CompilerParams(dimension_semantics=("arbitrary",) * n)


def _call(body, *, grid, ins, outs, name, scratch=()):
    res = pl.pallas_call(
        body,
        grid=grid,
        in_specs=[pl.BlockSpec(memory_space=pl.ANY) if b is None else pl.BlockSpec(b, m) for _, b, m in ins],
        out_specs=[pl.BlockSpec(b, m) for _, _, b, m in outs],
        out_shape=[jax.ShapeDtypeStruct(s, d) for s, d, _, _ in outs],
        scratch_shapes=list(scratch),
        name=name,
        compiler_params=_cparams(len(grid)),
    )(*[a for a, _, _ in ins])
    return res


def _tile(n, pref, unit=LANES):
    if n <= pref:
        return n
    t = (pref // unit) * unit
    while t > unit and n % t:
        t -= unit
    assert n % t == 0, (n, pref)
    return t


def _rowmap(fn, rows, consts, row_outs, acc_outs, name, pin=None):
    rows = [r if isinstance(r, tuple) else (r, r.shape[1], 0) for r in rows]
    s = rows[0][0].shape[0]
    tm = min(ROW_TILE, s)
    nr, nc, no, na = len(rows), len(consts), len(row_outs), len(acc_outs)
    npin = 0 if pin is None else 1

    def body(*refs):
        rin, cin = refs[:nr], refs[nr:nr + nc]
        refs = refs[:nr + nc] + refs[nr + nc + npin:]
        rout, aout = refs[nr + nc:nr + nc + no], refs[nr + nc + no:]
        ro, ao = fn(*[r[...] for r in rin], *[c[...] for c in cin])
        for ref, val in zip(rout, ro):
            ref[...] = val.astype(ref.dtype)
        if na:
            first = pl.program_id(0) == 0

            @pl.when(first)
            def _():
                for ref, val in zip(aout, ao):
                    ref[...] = val

            @pl.when(jnp.logical_not(first))
            def _():
                for ref, val in zip(aout, ao):
                    ref[...] += val

    ins = [(a, (tm, w), functools.partial(lambda i, cb: (i, cb), cb=cb)) for a, w, cb in rows]
    for c in consts:
        if isinstance(c, tuple):
            ins.append((c[0], (None, 1, c[0].shape[2]), functools.partial(lambda i, n: (n, 0, 0), n=c[1])))
        else:
            ins.append((c, c.shape, lambda i: (0, 0)))
    if pin is not None:
        ins.append((pin, None, None))
    outs = [((s, w), d, (tm, w), lambda i: (i, 0)) for w, d in row_outs]
    outs += [(shp, F32, shp, lambda i: (0, 0)) for shp in acc_outs]
    res = _call(body, grid=(s // tm,), ins=ins, outs=outs, name=name)
    return res[:no], res[no:]


def _ln(z, g, b):
    mu = jnp.mean(z, -1, keepdims=True)
    var = jnp.mean(jnp.square(z - mu), -1, keepdims=True)
    return (z - mu) * lax.rsqrt(var + LN_EPS) * g + b


def _combine(alpha, x, y, gt, g, b, sc, sh):
    xn = _ln(alpha * x + (1.0 + gt) * y, g, b)
    return xn, xn * (1.0 + sc) + sh


def _modulate_fwd(x, sc, sh):
    def fn(x, sc, sh):
        return ((x * (1.0 + sc) + sh),), ()

    (h,), _ = _rowmap(fn, [x], [sc, sh], [(x.shape[1], MXU_DTYPE)], [], "modulate_fwd")
    return h


def _modulate_bwd(x, dx, dh, sc, sh, pin=None):
    d = x.shape[1]

    def fn(x, dx, dh, sc, sh):
        _, vjp = jax.vjp(lambda x, sc, sh: x * (1.0 + sc) + sh, x, sc, sh)
        gx, gsc, gsh = vjp(dh)
        return (dx + gx,), (gsc, gsh)

    (gx,), (gsc, gsh) = _rowmap(fn, [x, dx, dh], [sc, sh], [(d, F32)], [(1, d), (1, d)], "modulate_bwd", pin=pin)
    return gx, gsc, gsh


def _combine_fwd(alpha, x, y, gt, g, b, sc, sh):
    d = x.shape[1]

    def fn(x, y, gt, g, b, sc, sh):
        return _combine(alpha, x, y, gt, g, b, sc, sh), ()

    (xn, h), _ = _rowmap(fn, [x, y], [gt, g, b, sc, sh], [(d, F32), (d, MXU_DTYPE)], [], "combine_fwd")
    return xn, h


def _combine_bwd(alpha, x, y, dxn, dh, gt, g, b, sc, sh, pin=None):
    d = x.shape[1]

    def fn(x, y, dxn, dh, gt, g, b, sc, sh):
        _, vjp = jax.vjp(functools.partial(_combine, alpha), x, y, gt, g, b, sc, sh)
        gx, gy, ggt, gg, gb, gsc, gsh = vjp((dxn, dh))
        return (gx, gy), (ggt, gg, gb, gsc, gsh)

    (gx, gy), accs = _rowmap(fn, [x, y, dxn, dh], [gt, g, b, sc, sh], [(d, F32), (d, MXU_DTYPE)],
                             [(1, d)] * 5, "combine_bwd", pin=pin)
    return gx, gy, accs


def _last_fwd_bwd(alpha, x, y, tgt, gt, g, b):
    d = x.shape[1]

    def fn(x, y, tgt, gt, g, b):
        xn, vjp = jax.vjp(lambda x, y, gt, g, b: _ln(alpha * x + (1.0 + gt) * y, g, b), x, y, gt, g, b)
        err = xn - tgt
        gx, gy, ggt, gg, gb = vjp(err * (1.0 / d))
        rows = jnp.sum(jnp.square(err), axis=-1, keepdims=True)
        loss = (0.5 / d) * jnp.sum(rows, axis=0, keepdims=True) * jnp.ones((1, LANES), F32)
        return (gx, gy), (loss, ggt, gg, gb)

    (gx, gy), accs = _rowmap(fn, [x, y, tgt], [gt, g, b], [(d, F32), (d, MXU_DTYPE)],
                             [(1, LANES), (1, d), (1, d), (1, d)], "last_fwd_bwd")
    return gx, gy, accs


MM_VMEM_BUDGET = 40 * 2 ** 20


def _fit(options, cost):
    for o in options:
        if 2 * cost(o) <= MM_VMEM_BUDGET:
            return o
    return options[-1]


def _row_tiles(m):
    return [t for t in (2048, 1024, 512, 256) if t <= m and m % t == 0] or [m]


def _mm_call(a, a_blk, a_map, b, b_blk, b_map, outs, dims, grid, name, epi=None, extra=None, split=None, blocks=None,
             pin=None):
    nk = grid[2]
    n_out = len(outs)
    n_in = 2 + (extra is not None) + (pin is not None)

    def body(*refs):
        a_ref, b_ref = refs[0], refs[1]
        rest = refs[n_in:]
        out_refs = rest[:n_out]

        def finish(val):
            if epi == "relu2":
                out_refs[0][...] = jnp.square(jnp.maximum(val, 0.0)).astype(out_refs[0].dtype)
            elif epi == "relu2_bwd":
                sq = refs[2][...].astype(F32)
                root = jnp.where(sq > 0.0, sq * lax.rsqrt(sq), 0.0)
                out_refs[0][...] = (val * 2.0 * root).astype(out_refs[0].dtype)
            elif split is not None:
                for g in range(split[0]):
                    out_refs[0][g] = val[:, g * split[1]:(g + 1) * split[1]].astype(out_refs[0].dtype)
            else:
                out_refs[0][...] = val.astype(out_refs[0].dtype)

        if blocks is None:
            p = lax.dot_general(a_ref[...], b_ref[...], (dims, ((), ())), preferred_element_type=F32)
        else:
            p = None
            for g in range(blocks[0]):
                part = lax.dot_general(a_ref[:, g * blocks[1]:(g + 1) * blocks[1]], b_ref[g], (dims, ((), ())),
                                       preferred_element_type=F32)
                p = part if p is None else p + part
        if nk == 1:
            finish(p)
        else:
            acc = rest[n_out]
            k = pl.program_id(2)

            @pl.when(k == 0)
            def _():
                acc[...] = p

            @pl.when(k > 0)
            def _():
                acc[...] += p

            @pl.when(k == nk - 1)
            def _():
                finish(acc[...])

    if nk > 1:
        out_blk = tuple(x for x in outs[0][2] if x is not None)
        if split is not None:
            out_blk = (out_blk[1], split[0] * split[1])
        scratch = [pltpu.VMEM(out_blk, F32)]
    else:
        scratch = []
    ins = [(a, a_blk, a_map), (b, b_blk, b_map)] + ([extra] if extra is not None else [])
    ins += [(pin, None, None)] if pin is not None else []
    return _call(body, grid=grid, ins=ins, outs=outs, name=name, scratch=scratch)


def _isz(dt):
    return jnp.dtype(dt).itemsize


def _mm_nn(a, b, out_dtype, name, relu2=False):
    m, kdim = a.shape
    if b.ndim == 2:
        n = b.shape[1]
        tn = _tile(n, 1536 if n > 2048 else 512)
        b_blk, b_map = (kdim, tn), lambda i, j, k: (0, j)
    else:
        g, _, ng = b.shape
        n = g * ng
        tn = _tile(ng, 512)
        b_blk = (None, kdim, tn)
        b_map = functools.partial(lambda i, j, k, npg: (j // npg, 0, j % npg), npg=ng // tn)
    tm = _fit(_row_tiles(m), lambda t: t * kdim * _isz(a.dtype) + kdim * tn * _isz(b.dtype) + t * tn * _isz(out_dtype))
    grid = (m // tm, n // tn, 1)
    outs = [((m, n), out_dtype, (tm, tn), lambda i, j, k: (i, j))]
    return _mm_call(a, (tm, kdim), lambda i, j, k: (i, 0), b, b_blk, b_map, outs, NN, grid, name,
                    epi="relu2" if relu2 else None)[0]


def _mm_nt(a, b, out_dtype, name, relu2_sq=None, pin=None):
    m, n = a.shape
    extra_bytes = _isz(relu2_sq.dtype) if relu2_sq is not None else 0
    if b.ndim == 2:
        kout = b.shape[0]
        to = _tile(kout, 512)
        b_blk, b_map, blocks = (to, n), lambda i, j, k: (j, 0), None
    else:
        g, kout, ng = b.shape
        to = _tile(kout, 512)
        b_blk, b_map, blocks = (g, to, ng), lambda i, j, k: (0, j, 0), (g, ng)
    tm = _fit(_row_tiles(m), lambda t: t * n * _isz(a.dtype) + to * n * _isz(b.dtype)
              + t * to * (_isz(out_dtype) + extra_bytes))
    grid = (m // tm, kout // to, 1)
    outs = [((m, kout), out_dtype, (tm, to), lambda i, j, k: (i, j))]
    extra = (relu2_sq, (tm, to), lambda i, j, k: (i, j)) if relu2_sq is not None else None
    return _mm_call(a, (tm, n), lambda i, j, k: (i, 0), b, b_blk, b_map, outs, NT, grid, name,
                    epi="relu2_bwd" if relu2_sq is not None else None, extra=extra, blocks=blocks, pin=pin)[0]


def _mm_tn(a, b, out_dtype, name, split_cols=False):
    m, kdim = a.shape
    n = b.shape[1]
    tk = _tile(kdim, 512)
    tn = _tile(n, 1536)
    if not split_cols:
        out, split = ((kdim, n), out_dtype, (tk, tn), lambda i, j, k: (i, j)), None
    else:
        ng = n // N_DEV
        if tn % ng:
            tn = _tile(ng, 512)
        if tn >= ng:
            gb = tn // ng
            out = ((N_DEV, kdim, ng), out_dtype, (gb, tk, ng), lambda i, j, k: (j, i, 0))
            split = (gb, ng)
        else:
            out = ((N_DEV, kdim, ng), out_dtype, (None, tk, tn),
                   functools.partial(lambda i, j, k, npg: (j // npg, i, j % npg), npg=ng // tn))
            split = None
    grid = (kdim // tk, n // tn, 1)
    return _mm_call(a, (m, tk), lambda i, j, k: (0, i), b, (m, tn), lambda i, j, k: (0, j), [out], TN, grid, name,
                    split=split)[0]


def _shifted(xa, off, rows):
    if off % 8 == 0:
        return xa[off:off + rows]
    return pltpu.roll(xa, xa.shape[0] - off, 0)[:rows]


def _conv_pad(taps):
    return -(-(taps - 1) // 8) * 8


def _conv_tile(xp_ref, w, i, rows, taps):
    pad = _conv_pad(taps)
    r0 = pl.multiple_of(i * rows, rows)
    xa = xp_ref[pl.ds(r0, rows + pad), :]
    views = [_shifted(xa, pad - (taps - 1) + j, rows) for j in range(taps)]
    acc = w[0:1, :] * views[0]
    for j in range(1, taps):
        acc = acc + w[j:j + 1, :] * views[j]
    return r0, acc, views


def _conv_back_tile(yp_ref, w, i, rows, taps):
    pad = _conv_pad(taps)
    r0 = pl.multiple_of(i * rows, rows)
    ya = yp_ref[pl.ds(r0, rows + pad), :]
    acc = w[taps - 1:taps, :] * ya[:rows]
    for j in range(taps - 1):
        acc = acc + w[j:j + 1, :] * _shifted(ya, taps - 1 - j, rows)
    return r0, acc


def _tap_sums(dy, views, taps):
    row = lax.broadcasted_iota(jnp.int32, (taps, LANES), 0)
    acc = jnp.zeros((taps, LANES), F32)
    for j in range(taps):
        acc = acc + jnp.where(row == j, jnp.sum(dy * views[j], axis=0, keepdims=True), 0.0)
    return acc


def _silu_l2(xc, l2):
    a = jax.nn.silu(xc)
    if l2:
        a = a * lax.rsqrt(jnp.sum(a * a, axis=-1, keepdims=True) + L2_EPS)
    return a


def _dn_conv_fwd(proj, conv_w, c0, nblk, l2, name):
    s = proj.shape[0]
    pad = _conv_pad(DN_CONV)
    rows = min(SHORT_CONV_TILE, s)

    def body(x_ref, w_ref, o_ref, xp):
        xp[0:pad, :] = jnp.zeros((pad, LANES), F32)
        xp[pad:, :] = x_ref[...]
        w = w_ref[...]

        def tile(i, c):
            r0, acc, _ = _conv_tile(xp, w, i, rows, DN_CONV)
            o_ref[pl.ds(r0, rows), :] = _silu_l2(acc, l2)
            return c

        lax.fori_loop(0, s // rows, tile, 0)

    return _call(body, grid=(nblk,),
                 ins=[(proj, (s, LANES), lambda c: (0, c0 + c)), (conv_w, (DN_CONV, LANES), lambda c: (0, c0 + c))],
                 outs=[((nblk, s, LANES), F32, (None, s, LANES), lambda c: (c, 0, 0))],
                 name=name, scratch=[pltpu.VMEM((s + pad, LANES), F32)])[0]


def _dn_conv_bwd(proj, conv_w, da, c0, nblk, l2, name):
    s = proj.shape[0]
    pad = _conv_pad(DN_CONV)
    rows = min(SHORT_CONV_TILE, s)

    def body(x_ref, w_ref, da_ref, dx_ref, dw_ref, xp, yp):
        xp[0:pad, :] = jnp.zeros((pad, LANES), F32)
        xp[pad:, :] = x_ref[...]
        yp[s:, :] = jnp.zeros((pad, LANES), F32)
        w = w_ref[...]

        def tile(i, dw):
            r0, acc, views = _conv_tile(xp, w, i, rows, DN_CONV)
            _, vjp = jax.vjp(functools.partial(_silu_l2, l2=l2), acc)
            (dxc,) = vjp(da_ref[pl.ds(r0, rows), :])
            yp[pl.ds(r0, rows), :] = dxc
            return dw + _tap_sums(dxc, views, DN_CONV)

        dw_ref[...] = lax.fori_loop(0, s // rows, tile, jnp.zeros((DN_CONV, LANES), F32))

        def tile2(i, c):
            r0, acc = _conv_back_tile(yp, w, i, rows, DN_CONV)
            dx_ref[pl.ds(r0, rows), :] = acc.astype(dx_ref.dtype)
            return c

        lax.fori_loop(0, s // rows, tile2, 0)

    return _call(body, grid=(nblk,),
                 ins=[(proj, (s, LANES), lambda c: (0, c0 + c)), (conv_w, (DN_CONV, LANES), lambda c: (0, c0 + c)),
                      (da, (None, s, LANES), lambda c: (c, 0, 0))],
                 outs=[((s, nblk * LANES), MXU_DTYPE, (s, LANES), lambda c: (0, c)),
                       ((DN_CONV, nblk * LANES), F32, (DN_CONV, LANES), lambda c: (0, c))],
                 name=name, scratch=[pltpu.VMEM((s + pad, LANES), F32), pltpu.VMEM((s + pad, LANES), F32)])


def _cf_conv_fwd(vg, dw_w, dw_b):
    s, c2 = vg.shape
    ch = c2 // 2
    nblk = ch // LANES
    taps = dw_w.shape[0]
    pad = _conv_pad(taps)
    rows = min(CONV_TILE, s)

    def body(v_ref, g_ref, w_ref, b_ref, o_ref, xp):
        xp[0:pad, :] = jnp.zeros((pad, LANES), F32)
        xp[pad:, :] = v_ref[...] * jax.nn.sigmoid(g_ref[...])
        w = w_ref[...]
        bias = b_ref[...]

        def tile(i, c):
            r0, acc, _ = _conv_tile(xp, w, i, rows, taps)
            o_ref[pl.ds(r0, rows), :] = acc + bias
            return c

        lax.fori_loop(0, s // rows, tile, 0)

    return _call(body, grid=(nblk,),
                 ins=[(vg, (s, LANES), lambda c: (0, c)), (vg, (s, LANES), lambda c: (0, nblk + c)),
                      (dw_w, (taps, LANES), lambda c: (0, c)), (dw_b, (1, LANES), lambda c: (0, c))],
                 outs=[((s, ch), F32, (s, LANES), lambda c: (0, c))],
                 name="cf_conv_fwd", scratch=[pltpu.VMEM((s + pad, LANES), F32)])[0]


def _cf_conv_bwd(vg, dw_w, du):
    s, c2 = vg.shape
    ch = c2 // 2
    nblk = ch // LANES
    taps = dw_w.shape[0]
    pad = _conv_pad(taps)
    rows = min(CONV_TILE, s)

    def body(v_ref, g_ref, w_ref, du_ref, dv_ref, dg_ref, dw_ref, db_ref, xp, yp):
        sig = jax.nn.sigmoid(g_ref[...])
        xp[0:pad, :] = jnp.zeros((pad, LANES), F32)
        xp[pad:, :] = v_ref[...] * sig
        yp[0:s, :] = du_ref[...]
        yp[s:, :] = jnp.zeros((pad, LANES), F32)
        w = w_ref[...]
        db_ref[...] = jnp.sum(du_ref[...], axis=0, keepdims=True)

        def tile(i, dw):
            r0, _, views = _conv_tile(xp, w, i, rows, taps)
            return dw + _tap_sums(du_ref[pl.ds(r0, rows), :], views, taps)

        dw_ref[...] = lax.fori_loop(0, s // rows, tile, jnp.zeros((taps, LANES), F32))

        def tile2(i, c):
            r0, du0 = _conv_back_tile(yp, w, i, rows, taps)
            val = v_ref[pl.ds(r0, rows), :]
            sg = jax.nn.sigmoid(g_ref[pl.ds(r0, rows), :])
            dv_ref[pl.ds(r0, rows), :] = (du0 * sg).astype(dv_ref.dtype)
            dg_ref[pl.ds(r0, rows), :] = (du0 * val * sg * (1.0 - sg)).astype(dg_ref.dtype)
            return c

        lax.fori_loop(0, s // rows, tile2, 0)

    return _call(body, grid=(nblk,),
                 ins=[(vg, (s, LANES), lambda c: (0, c)), (vg, (s, LANES), lambda c: (0, nblk + c)),
                      (dw_w, (taps, LANES), lambda c: (0, c)), (du, (s, LANES), lambda c: (0, c))],
                 outs=[((s, ch), MXU_DTYPE, (s, LANES), lambda c: (0, c)),
                       ((s, ch), MXU_DTYPE, (s, LANES), lambda c: (0, c)),
                       ((taps, ch), F32, (taps, LANES), lambda c: (0, c)),
                       ((1, ch), F32, (1, LANES), lambda c: (0, c))],
                 name="cf_conv_bwd", scratch=[pltpu.VMEM((s + pad, LANES), F32), pltpu.VMEM((s + pad, LANES), F32)])


def _masks():
    r = lax.broadcasted_iota(jnp.int32, (CHUNK, CHUNK), 0)
    c = lax.broadcasted_iota(jnp.int32, (CHUNK, CHUNK), 1)
    return r >= c, r > c, r <= c


def _chunk_decay(g):
    causal, _, upper = _masks()
    gb = jnp.broadcast_to(g, (CHUNK, CHUNK))
    gam_r = _dot01(jnp.where(causal, 1.0, 0.0), gb)
    gam_s = _dot01(jnp.ones((CHUNK, CHUNK), F32), jnp.where(upper, gb, 0.0))
    dm = jnp.where(causal, jnp.exp(jnp.where(causal, gam_r - gam_s, 0.0)), 0.0)
    return gam_r[:, 0:1], dm


def _chunk_scores(q, k, beta, dm):
    _, strict, _ = _masks()
    both = _mdot(jnp.concatenate([k * beta, q * (HEAD_DIM ** -0.5)], axis=0), k, NT)
    return jnp.where(strict, both[:CHUNK] * dm, 0.0), both[CHUNK:] * dm


def _lockstep(gens):
    results = [None] * len(gens)
    alive = list(range(len(gens)))
    while alive:
        for i in list(alive):
            try:
                next(gens[i])
            except StopIteration as stop:
                results[i] = stop.value
                alive.remove(i)
    return results


def _chunk_prep_bwd(q, k, v, beta, gam, t, du, dw, daqk, dqd, dkd, dgl):
    causal, strict, _ = _masks()
    r = lax.broadcasted_iota(jnp.int32, (CHUNK, CHUNK), 0)
    c = lax.broadcasted_iota(jnp.int32, (CHUNK, CHUNK), 1)
    scale = HEAD_DIM ** -0.5
    eg = jnp.exp(gam)
    gam_last = gam[CHUNK - 1:CHUNK, :]
    rr = jnp.exp(gam_last - gam)
    kb = k * beta
    qs = q * scale
    vb = v * beta
    kbe = kb * eg
    gam_b = jnp.broadcast_to(gam, (CHUNK, CHUNK))
    gam_s = _dot01(jnp.ones((CHUNK, CHUNK), F32), jnp.where(r == c, gam_b, 0.0))
    both = _mdot(jnp.concatenate([kb, qs], axis=0), k, NT)
    duw = jnp.concatenate([du, dw], axis=1)
    dt = _mdot(duw, jnp.concatenate([vb, kbe], axis=1), NT)
    dvk = _mdot(t, duw, TN)
    yield
    dm = jnp.where(causal, jnp.exp(jnp.where(causal, gam_b - gam_s, 0.0)), 0.0)
    a = jnp.where(strict, both[:CHUNK] * dm, 0.0)
    aqk = both[CHUNK:] * dm
    dvb, dkbe = dvk[:, :HEAD_DIM], dvk[:, HEAD_DIM:]
    x = _mdot(t, dt, TN)
    yield
    da = jnp.where(strict, -_mdot(x, t, NT), 0.0)
    yield
    dkk = da * dm
    dqk = daqk * dm
    ddiff = da * a + daqk * aqk
    dboth = jnp.concatenate([dkk, dqk], axis=0)
    dkq = _mdot(dboth, k)
    dk_mm = _mdot(dboth, jnp.concatenate([kb, qs], axis=0), TN)
    colsum = _dot01(ddiff, jnp.ones((CHUNK, LANES), F32), TN, mask_first=False)[:, 0:1]
    yield
    dkb = dkq[:CHUNK] + dkbe * eg
    dk = dk_mm + dkb * beta + dkd * rr
    dq = (dkq[CHUNK:] + dqd * eg) * scale
    dbeta = jnp.sum(dkb * k, axis=-1, keepdims=True) + jnp.sum(dvb * v, axis=-1, keepdims=True)
    dv = dvb * beta
    deg = jnp.sum(dkbe * kb, axis=-1, keepdims=True) + jnp.sum(dqd * qs, axis=-1, keepdims=True)
    drr = jnp.sum(dkd * k, axis=-1, keepdims=True)
    dgam = deg * eg - drr * rr + jnp.sum(ddiff, axis=-1, keepdims=True) - colsum
    dgam_last = jnp.sum(drr * rr, axis=0, keepdims=True) + dgl[0:1, :] * jnp.exp(gam_last)
    row = lax.broadcasted_iota(jnp.int32, (CHUNK, 1), 0)
    dgam = dgam + jnp.where(row == CHUNK - 1, dgam_last, 0.0)
    dg = _dot01(jnp.where(causal, 1.0, 0.0), jnp.broadcast_to(dgam, (CHUNK, LANES)), TN)[:, 0:1]
    return dq, dk, dv, dbeta, dg


def _prep_group(s):
    nch = s // CHUNK
    return next(c for c in (16, 8, 4, 2, 1) if nch % c == 0)


def _tri_solve_lanes(a_l):
    n = a_l.shape[1]
    group = 8

    def body(a_ref, t_ref):
        t_ref[...] = jnp.zeros_like(t_ref)
        col = lax.broadcasted_iota(jnp.int32, (CHUNK, n), 0)

        def row(r, carry):
            r0 = pl.multiple_of(r * CHUNK, CHUNK)

            def inner(sg, acc):
                a8 = a_ref[pl.ds(r0 + pl.multiple_of(sg * group, group), group), :]
                for j in range(group):
                    t0 = pl.multiple_of((sg * group + j) * CHUNK, CHUNK)
                    acc = acc + a8[j:j + 1, :] * t_ref[pl.ds(t0, CHUNK), :]
                return acc

            acc = lax.fori_loop(0, (r + group - 1) // group, inner, jnp.zeros((CHUNK, n), F32))
            t_ref[pl.ds(r0, CHUNK), :] = jnp.where(col == r, 1.0, 0.0) - acc
            return carry

        lax.fori_loop(0, CHUNK, row, 0)

    return pl.pallas_call(body, out_shape=jax.ShapeDtypeStruct(a_l.shape, F32), name="dn_tri_solve")(a_l)


def _head_cols(bg, hh, heads):
    lane = lax.broadcasted_iota(jnp.int32, bg.shape, 1)
    beta = jnp.sum(jnp.where(lane == hh, bg, 0.0), axis=-1, keepdims=True)
    g = jnp.sum(jnp.where(lane == heads + hh, bg, 0.0), axis=-1, keepdims=True)
    return beta, g


def _dn_prep(q, k, v, bg):
    h, s, _ = q.shape
    cb = _prep_group(s)
    rb = cb * CHUNK
    big = lambda x: (x, (None, rb, HEAD_DIM), lambda n, hh: (hh, n, 0))
    sq = lambda x: (x, (None, rb, CHUNK), lambda n, hh: (hh, n, 0))
    col = lambda x: (x, (None, rb, 1), lambda n, hh: (hh, n, 0))
    tok = (bg, (rb, LANES), lambda n, hh: (n, 0))
    o_big = ((h, s, HEAD_DIM), F32, (None, rb, HEAD_DIM), lambda n, hh: (hh, n, 0))
    o_sq = ((h, s, CHUNK), F32, (None, rb, CHUNK), lambda n, hh: (hh, n, 0))
    o_col = ((h, s, 1), F32, (None, rb, 1), lambda n, hh: (hh, n, 0))

    def scores(q_ref, k_ref, bg_ref, a_ref, aqk_ref, gam_ref):
        beta, g = _head_cols(bg_ref[...], pl.program_id(1), h)
        for i in range(cb):
            sl = slice(i * CHUNK, (i + 1) * CHUNK)
            gam, dm = _chunk_decay(g[sl])
            a_ref[sl, :], aqk_ref[sl, :] = _chunk_scores(q_ref[sl, :], k_ref[sl, :], beta[sl], dm)
            gam_ref[sl, :] = gam

    a, aqk, gam = _call(scores, grid=(s // rb, h), ins=[big(q), big(k), tok], outs=[o_sq, o_sq, o_col],
                        name="dn_scores")
    n_prob = h * (s // CHUNK)
    t_l = _tri_solve_lanes(jnp.transpose(a.reshape(n_prob, CHUNK * CHUNK)))
    t = jnp.transpose(t_l).reshape(h, s, CHUNK)

    def wy(k_ref, v_ref, bg_ref, gam_ref, t_ref, u_ref, w_ref):
        beta, _ = _head_cols(bg_ref[...], pl.program_id(1), h)
        for i in range(cb):
            sl = slice(i * CHUNK, (i + 1) * CHUNK)
            kb = k_ref[sl, :] * beta[sl]
            rhs = jnp.concatenate([v_ref[sl, :] * beta[sl], kb * jnp.exp(gam_ref[sl, :])], axis=1)
            uw = _mdot(t_ref[sl, :], rhs)
            u_ref[sl, :] = uw[:, :HEAD_DIM]
            w_ref[sl, :] = uw[:, HEAD_DIM:]

    u, w = _call(wy, grid=(s // rb, h), ins=[big(k), big(v), tok, col(gam), sq(t)], outs=[o_big, o_big],
                 name="dn_wy")
    return u, w, aqk, t, gam


def _dn_prep_bwd(q, k, v, bg, gam, t, du, dw, daqk, dqd, dkd, dgl):
    h, s, _ = q.shape
    cb = _prep_group(s)
    rb = cb * CHUNK

    def body(q_ref, k_ref, v_ref, bg_ref, g_ref, t_ref, du_ref, dw_ref, da_ref, dqd_ref, dkd_ref, dgl_ref,
             dq_ref, dk_ref, dv_ref, dbg_ref):
        hh = pl.program_id(1)
        beta, _ = _head_cols(bg_ref[...], hh, h)
        slices = [slice(i * CHUNK, (i + 1) * CHUNK) for i in range(cb)]
        results = _lockstep([_chunk_prep_bwd(
            q_ref[sl, :], k_ref[sl, :], v_ref[sl, :], beta[sl], g_ref[sl, :], t_ref[sl, :],
            du_ref[sl, :], dw_ref[sl, :], da_ref[sl, :], dqd_ref[sl, :], dkd_ref[sl, :], dgl_ref[sl, :])
            for sl in slices])

        @pl.when(hh == 0)
        def _():
            dbg_ref[...] = jnp.zeros_like(dbg_ref)

        lane = lax.broadcasted_iota(jnp.int32, (CHUNK, LANES), 1)
        for sl, (dq, dk, dv, dbeta, dg) in zip(slices, results):
            dq_ref[sl, :] = dq
            dk_ref[sl, :] = dk
            dv_ref[sl, :] = dv
            dbg_ref[sl, :] += jnp.where(lane == hh, dbeta, 0.0) + jnp.where(lane == h + hh, dg, 0.0)

    big = lambda x: (x, (None, rb, HEAD_DIM), lambda n, hh: (hh, n, 0))
    sq = lambda x: (x, (None, rb, CHUNK), lambda n, hh: (hh, n, 0))
    col = lambda x: (x, (None, rb, 1), lambda n, hh: (hh, n, 0))
    tok = (bg, (rb, LANES), lambda n, hh: (n, 0))
    o_big = ((h, s, HEAD_DIM), F32, (None, rb, HEAD_DIM), lambda n, hh: (hh, n, 0))
    return _call(body, grid=(s // rb, h),
                 ins=[big(q), big(k), big(v), tok, col(gam), sq(t), big(du), big(dw), sq(daqk), big(dqd), big(dkd),
                      col(dgl)],
                 outs=[o_big, o_big, o_big, ((s, LANES), F32, (rb, LANES), lambda n, hh: (n, 0))], name="dn_prep_bwd")


def _chunk_scaled(q, k, gam):
    gam_last = gam[CHUNK - 1:CHUNK, :]
    q_dec = q * (HEAD_DIM ** -0.5) * jnp.exp(gam)
    k_dec = k * jnp.exp(gam_last - gam)
    return q_dec, k_dec, jnp.exp(gam_last)


def _scan_group(s):
    return 2 if (s // CHUNK) % 2 == 0 else 1


def _dn_scan(q, k, u, w, aqk, gam):
    h, s, _ = q.shape
    nch = s // CHUNK
    sg = _scan_group(s)
    rb = sg * CHUNK

    def body(q_ref, k_ref, u_ref, w_ref, a_ref, gam_ref, o_ref, st_ref, state):
        @pl.when(pl.program_id(0) == 0)
        def _():
            state[...] = jnp.zeros_like(state)

        def head(hh, c):
            sl = slice(c * CHUNK, (c + 1) * CHUNK)
            s0 = state[hh]
            st_ref[c, hh] = s0
            q_dec, k_dec, gl = _chunk_scaled(q_ref[hh, sl, :], k_ref[hh, sl, :], gam_ref[hh, sl, :])
            both = _mdot(jnp.concatenate([w_ref[hh, sl, :], q_dec], axis=0), s0)
            yield
            v_new = u_ref[hh, sl, :] - both[:CHUNK]
            o_ref[sl, hh * HEAD_DIM:(hh + 1) * HEAD_DIM] = both[CHUNK:] + _mdot(a_ref[hh, sl, :], v_new)
            state[hh] = s0 * gl + _mdot(k_dec, v_new, TN)

        for c in range(sg):
            _lockstep([head(hh, c) for hh in range(h)])

    big = lambda x: (x, (h, rb, HEAD_DIM), lambda n: (0, n, 0))
    return _call(body, grid=(nch // sg,),
                 ins=[big(q), big(k), big(u), big(w), (aqk, (h, rb, CHUNK), lambda n: (0, n, 0)),
                      (gam, (h, rb, 1), lambda n: (0, n, 0))],
                 outs=[((s, h * HEAD_DIM), F32, (rb, h * HEAD_DIM), lambda n: (n, 0)),
                       ((nch, h, HEAD_DIM, HEAD_DIM), F32, (sg, h, HEAD_DIM, HEAD_DIM), lambda n: (n, 0, 0, 0))],
                 name="dn_scan", scratch=[pltpu.VMEM((h, HEAD_DIM, HEAD_DIM), F32)])


def _dn_scan_bwd(q, k, u, w, aqk, gam, states, do):
    h, s, _ = q.shape
    nch = s // CHUNK
    sg = _scan_group(s)
    rb = sg * CHUNK
    ngr = nch // sg

    def body(q_ref, k_ref, u_ref, w_ref, a_ref, gam_ref, st_ref, do_ref,
             du_ref, dw_ref, da_ref, dqd_ref, dkd_ref, dgl_ref, dstate):
        @pl.when(pl.program_id(0) == 0)
        def _():
            dstate[...] = jnp.zeros_like(dstate)

        def head(hh, c):
            sl = slice(c * CHUNK, (c + 1) * CHUNK)
            s0 = st_ref[c, hh]
            ds = dstate[hh]
            doh = do_ref[sl, hh * HEAD_DIM:(hh + 1) * HEAD_DIM]
            wv = w_ref[hh, sl, :]
            q_dec, k_dec, gl = _chunk_scaled(q_ref[hh, sl, :], k_ref[hh, sl, :], gam_ref[hh, sl, :])
            ws = _mdot(wv, s0)
            dv_new = _mdot(a_ref[hh, sl, :], doh, TN) + _mdot(k_dec, ds)
            dqd_ref[hh, sl, :] = _mdot(doh, s0, NT)
            qdo = _mdot(q_dec, doh, TN)
            tot = jnp.sum(jnp.sum(s0 * ds, axis=-1, keepdims=True), axis=0, keepdims=True)
            dgl_ref[hh, sl, :] = jnp.broadcast_to(tot, (CHUNK, 1))
            yield
            v_new = u_ref[hh, sl, :] - ws
            du_ref[hh, sl, :] = dv_new
            dw_ref[hh, sl, :] = -_mdot(dv_new, s0, NT)
            da_ref[hh, sl, :] = _mdot(doh, v_new, NT)
            dkd_ref[hh, sl, :] = _mdot(v_new, ds, NT)
            dstate[hh] = ds * gl + qdo - _mdot(wv, dv_new, TN)

        for c in range(sg - 1, -1, -1):
            _lockstep([head(hh, c) for hh in range(h)])

    rev = lambda n: (0, ngr - 1 - n, 0)
    big = lambda x: (x, (h, rb, HEAD_DIM), rev)
    o_big = ((h, s, HEAD_DIM), F32, (h, rb, HEAD_DIM), rev)
    return _call(body, grid=(ngr,),
                 ins=[big(q), big(k), big(u), big(w), (aqk, (h, rb, CHUNK), rev), (gam, (h, rb, 1), rev),
                      (states, (sg, h, HEAD_DIM, HEAD_DIM), lambda n: (ngr - 1 - n, 0, 0, 0)),
                      (do, (rb, h * HEAD_DIM), lambda n: (ngr - 1 - n, 0))],
                 outs=[o_big, o_big, ((h, s, CHUNK), F32, (h, rb, CHUNK), rev), o_big, o_big,
                       ((h, s, 1), F32, (h, rb, 1), rev)],
                 name="dn_scan_bwd", scratch=[pltpu.VMEM((h, HEAD_DIM, HEAD_DIM), F32)])


def _gates(x, a_log, dt_b, h):
    lane = lax.broadcasted_iota(jnp.int32, x.shape, 1)
    return jnp.where(lane < h, jax.nn.sigmoid(x), -jnp.exp(a_log) * jax.nn.softplus(x + dt_b))


def _head_out(oh, zh, nw):
    on = oh * lax.rsqrt(jnp.mean(oh * oh, axis=-1, keepdims=True) + RMS_EPS) * nw
    return on * jax.nn.silu(zh)


def _pad_lanes(x, lo):
    return jnp.zeros((1, LANES), F32).at[0, lo:lo + x.shape[0]].set(x)


def _deltanet_fwd(hin, get_w_in, conv_w, a_log, dt_bias, norm_w, get_w_out):
    h = a_log.shape[0]
    hw = h * HEAD_DIM
    w_in = get_w_in(hin)
    proj = _mm_nn(hin, w_in, F32, "dn_proj")
    q = _dn_conv_fwd(proj, conv_w, 0, h, True, "dn_conv_q")
    k = _dn_conv_fwd(proj, conv_w, h, h, True, "dn_conv_k")
    v = _dn_conv_fwd(proj, conv_w, 2 * h, h, False, "dn_conv_v")
    alp, dtp = _pad_lanes(a_log, h), _pad_lanes(dt_bias, h)

    def gates_fn(x, al, db):
        return (_gates(x, al, db, h),), ()

    (bg,), _ = _rowmap(gates_fn, [(proj, LANES, 4 * h)], [alp, dtp], [(LANES, F32)], [], "dn_gates")
    u, w, aqk, t, gam = _dn_prep(q, k, v, bg)
    o, states = _dn_scan(q, k, u, w, aqk, gam)
    nw = norm_w[None, :]

    def out_fn(o, z, nw):
        parts = [_head_out(o[:, i * HEAD_DIM:(i + 1) * HEAD_DIM], z[:, i * HEAD_DIM:(i + 1) * HEAD_DIM], nw)
                 for i in range(h)]
        return (jnp.concatenate(parts, axis=-1),), ()

    (og,), _ = _rowmap(out_fn, [o, (proj, hw, 3)], [nw], [(hw, MXU_DTYPE)], [], "dn_out")
    w_out = get_w_out(og)
    y = _mm_nn(og, w_out, F32, "dn_y")
    return y, (hin, proj, q, k, v, bg, u, w, aqk, t, gam, states, o, og, alp, dtp, nw, w_in, w_out)


def _deltanet_bwd(res, dy, conv_w, send):
    hin, proj, q, k, v, bg, u, w, aqk, t, gam, states, o, og, alp, dtp, nw, w_in, w_out = res
    h = q.shape[0]
    hw = h * HEAD_DIM
    s = hin.shape[0]
    d_w_out = _mm_tn(og, dy, MXU_DTYPE, "dn_dwout")
    dog = _mm_nt(dy, w_out, F32, "dn_dog")

    def out_bwd(o, z, dog, nw):
        dos, dzs = [], []
        dn = jnp.zeros((1, HEAD_DIM), F32)
        for i in range(h):
            sl = slice(i * HEAD_DIM, (i + 1) * HEAD_DIM)
            _, vjp = jax.vjp(_head_out, o[:, sl], z[:, sl], nw)
            a, b, c = vjp(dog[:, sl])
            dos.append(a)
            dzs.append(b)
            dn = dn + c
        return (jnp.concatenate(dos, axis=-1), jnp.concatenate(dzs, axis=-1)), (dn,)

    (do, dz), (d_norm_w,) = _rowmap(out_bwd, [o, (proj, hw, 3), dog], [nw], [(hw, F32), (hw, MXU_DTYPE)],
                                    [(1, HEAD_DIM)], "dn_out_bwd")
    du, dw, daqk, dqd, dkd, dgl = _dn_scan_bwd(q, k, u, w, aqk, gam, states, do)
    dq, dk, dv, dbg = _dn_prep_bwd(q, k, v, bg, gam, t, du, dw, daqk, dqd, dkd, dgl)
    dpq, dwq = _dn_conv_bwd(proj, conv_w, dq, 0, h, True, "dn_conv_q_bwd")
    dpk, dwk = _dn_conv_bwd(proj, conv_w, dk, h, h, True, "dn_conv_k_bwd")
    dpv, dwv = _dn_conv_bwd(proj, conv_w, dv, 2 * h, h, False, "dn_conv_v_bwd")

    def gates_bwd(x, dbg, al, db):
        _, vjp = jax.vjp(functools.partial(_gates, h=h), x, al, db)
        gx, gal, gdb = vjp(dbg)
        return (gx,), (gal, gdb)

    (dba,), (d_alp, d_dtp) = _rowmap(gates_bwd, [(proj, LANES, 4 * h), dbg], [alp, dtp], [(LANES, MXU_DTYPE)],
                                     [(1, LANES), (1, LANES)], "dn_gates_bwd")
    dproj = jnp.concatenate([dpq, dpk, dpv, dz, dba], axis=1)
    d_w_in = _mm_tn(hin, dproj, MXU_DTYPE, "dn_dwin")
    token = send(d_w_in, d_w_out)
    dh = _mm_nt(dproj, w_in, F32, "dn_dh", pin=token)
    d_conv_w = jnp.concatenate([dwq, dwk, dwv], axis=1)
    return dh, dict(conv_w=d_conv_w, a_log=d_alp[0, h:2 * h], dt_bias=d_dtp[0, h:2 * h], norm_w=d_norm_w[0]), token


def _ln_silu(u, g, b):
    return jax.nn.silu(_ln(u, g, b))


def _conformer_fwd(hin, get_w_in, dw_w, dw_b, ln_g, ln_b, get_w_out):
    w_in = get_w_in(hin)
    vg = _mm_nn(hin, w_in, F32, "cf_vg")
    u1 = _cf_conv_fwd(vg, dw_w, dw_b)
    ch = u1.shape[1]

    def fn(u, g, b):
        return (_ln_silu(u, g, b),), ()

    (u2,), _ = _rowmap(fn, [u1], [ln_g, ln_b], [(ch, MXU_DTYPE)], [], "cf_ln")
    w_out = get_w_out(u2)
    y = _mm_nn(u2, w_out, F32, "cf_y")
    return y, (hin, vg, u1, u2, w_in, w_out)


def _conformer_bwd(res, dy, dw_w, ln_g, ln_b):
    hin, vg, u1, u2, w_in, w_out = res
    ch = u1.shape[1]
    d_w_out = _mm_tn(u2, dy, MXU_DTYPE, "cf_dwout")
    du2 = _mm_nt(dy, w_out, F32, "cf_du2")

    def fn(u, du2, g, b):
        _, vjp = jax.vjp(_ln_silu, u, g, b)
        gu, gg, gb = vjp(du2)
        return (gu,), (gg, gb)

    (du1,), (d_ln_g, d_ln_b) = _rowmap(fn, [u1, du2], [ln_g, ln_b], [(ch, F32)], [(1, ch), (1, ch)], "cf_ln_bwd")
    dval, dgate, d_dw_w, d_dw_b = _cf_conv_bwd(vg, dw_w, du1)
    dvg = jnp.concatenate([dval, dgate], axis=1)
    d_w_in = _mm_tn(hin, dvg, MXU_DTYPE, "cf_dwin", split_cols=True)
    dh = _mm_nt(dvg, w_in, F32, "cf_dh")
    return dh, dict(w_in=d_w_in, w_out=d_w_out, dw_w=d_dw_w, dw_b=d_dw_b[0], ln_g=d_ln_g[0], ln_b=d_ln_b[0])


def _mlp_fwd(hin, get_w1, get_w2):
    w1 = get_w1(hin)
    r = _mm_nn(hin, w1, MXU_DTYPE, "ff_a", relu2=True)
    w2 = get_w2(r)
    m = _mm_nn(r, w2, F32, "ff_m")
    return m, (hin, r, w1, w2)


def _mlp_bwd(res, dm):
    hin, r, w1, w2 = res
    d_w2 = _mm_tn(r, dm, MXU_DTYPE, "ff_dw2")
    da = _mm_nt(dm, w2, MXU_DTYPE, "ff_da", relu2_sq=r)
    d_w1 = _mm_tn(hin, da, MXU_DTYPE, "ff_dw1", split_cols=True)
    dh = _mm_nt(da, w1, F32, "ff_dh")
    return dh, d_w1, d_w2


def _ada_fwd(c_all, ada_w):
    depth, d, nl = ada_w.shape
    tn = _tile(nl, 256)

    def body(c_ref, w_ref, o_ref, cond_ref):
        cond = jax.nn.silu(c_ref[...]).astype(MXU_DTYPE)
        cond_ref[...] = cond
        o_ref[...] = lax.dot_general(cond, w_ref[...].astype(MXU_DTYPE), (NN, ((), ())), preferred_element_type=F32)

    return _call(body, grid=(depth, nl // tn),
                 ins=[(c_all, c_all.shape, lambda l, j: (0, 0)), (ada_w, (None, d, tn), lambda l, j: (l, 0, j))],
                 outs=[((depth, N_DEV, nl), F32, (None, N_DEV, tn), lambda l, j: (l, 0, j)),
                       (c_all.shape, MXU_DTYPE, c_all.shape, lambda l, j: (0, 0))],
                 name="ada_fwd")


def _ada_bwd(cond_all, dmod_cols):
    depth, _, nl = dmod_cols.shape
    d = cond_all.shape[1]
    tn = _tile(nl, 256)

    def body(c_ref, g_ref, o_ref):
        o_ref[...] = lax.dot_general(c_ref[...], g_ref[...].astype(MXU_DTYPE), (TN, ((), ())),
                                     preferred_element_type=F32)

    return _call(body, grid=(depth, nl // tn),
                 ins=[(cond_all, cond_all.shape, lambda l, j: (0, 0)), (dmod_cols, (None, N_DEV, tn), lambda l, j: (l, 0, j))],
                 outs=[((depth, d, nl), F32, (None, d, tn), lambda l, j: (l, 0, j))], name="ada_bwd")[0]


def _peers():
    x, y, c = lax.axis_index("x"), lax.axis_index("y"), lax.axis_index("c")
    peers = []
    for k in range(1, N_DEV):
        px = 1 - x if k & 4 else x
        py = 1 - y if k & 2 else y
        pc = 1 - c if k & 1 else c
        peers.append(((px, py, pc), 4 * px + 2 * py + pc))
    return 4 * x + 2 * y + c, peers


_HBM = pl.BlockSpec(memory_space=pltpu.HBM)
_SEM = pl.BlockSpec(memory_space=pltpu.SEMAPHORE)
_ANY = pl.BlockSpec(memory_space=pl.ANY)
_EFFECT = pltpu.SideEffectType.DATAFLOW_SIDE_EFFECTING


def _xfer_start(srcs, lands, scatter, after, name):
    nt = len(srcs)

    def body(*refs):
        src, land = refs[:nt], refs[nt:2 * nt]
        sems = refs[2 * nt + 1:4 * nt + 1]
        token = refs[-1]
        me, peers = _peers()
        for t in range(nt):
            for k, (pid, plin) in enumerate(peers):
                pltpu.make_async_remote_copy(
                    src_ref=src[t].at[plin] if scatter else src[t], dst_ref=land[t].at[me],
                    send_sem=sems[2 * t].at[k], recv_sem=sems[2 * t + 1].at[k],
                    device_id=pid, device_id_type=pl.DeviceIdType.MESH).start()
        token[...] = jnp.zeros_like(token)

    out_shape = [pltpu.SemaphoreType.DMA((N_DEV - 1,)) for _ in range(2 * nt)]
    out_shape += [pltpu.HBM(a.shape, a.dtype) for a in lands]
    out_shape += [jax.ShapeDtypeStruct((8, LANES), F32)]
    srcs = [pltpu.with_memory_space_constraint(a, pltpu.HBM) for a in srcs]
    res = pl.pallas_call(
        body, name=name, out_shape=out_shape,
        in_specs=[_HBM] * (2 * nt) + [_ANY],
        out_specs=[_SEM] * (2 * nt) + [_HBM] * nt + [pl.BlockSpec(memory_space=pltpu.VMEM)],
        input_output_aliases={nt + i: 2 * nt + i for i in range(nt)},
        compiler_params=pltpu.CompilerParams(has_side_effects=_EFFECT),
    )(*srcs, *[pltpu.with_memory_space_constraint(a, pltpu.HBM) for a in lands], after)
    sems, thru = res[:2 * nt], res[2 * nt:3 * nt]
    return [(sems[2 * t], sems[2 * t + 1], srcs[t], thru[t]) for t in range(nt)], res[-1]


def _xfer_wait(handle, scatter, after, name):
    send, recv, src, land = handle

    def body(src_ref, land_ref, send_sem, recv_sem, after_ref, land_out):
        _, peers = _peers()
        for k, (pid, plin) in enumerate(peers):
            cp = pltpu.make_async_remote_copy(
                src_ref=src_ref.at[plin] if scatter else src_ref, dst_ref=land_ref.at[plin],
                send_sem=send_sem.at[k], recv_sem=recv_sem.at[k],
                device_id=pid, device_id_type=pl.DeviceIdType.MESH)
            cp.wait_send()
            cp.wait_recv()

    return pl.pallas_call(
        body, name=name, out_shape=pltpu.HBM(land.shape, land.dtype),
        in_specs=(_HBM, _HBM, _SEM, _SEM, _ANY), out_specs=_HBM, input_output_aliases={1: 0},
        compiler_params=pltpu.CompilerParams(has_side_effects=_EFFECT),
    )(src, land, send, recv, after)


def _landing(x, me):
    return lax.dynamic_update_slice(lax.empty((N_DEV,) + x.shape, x.dtype), x[None], (me,) + (0,) * x.ndim)


def _chip_peers():
    x, y, c = lax.axis_index("x"), lax.axis_index("y"), lax.axis_index("c")
    lin = lambda px, py, pc: 4 * px + 2 * py + pc
    sibling = ((x, y, 1 - c), lin(x, y, 1 - c))
    chips = [((1 - x, y, c), lin(1 - x, y, c)), ((x, 1 - y, c), lin(x, 1 - y, c)),
             ((1 - x, 1 - y, c), lin(1 - x, 1 - y, c))]
    return lin(x, y, c), sibling, chips


N_CHIPS_OTHER = 3


def _gather2_start(srcs, lands, after, name):
    nt = len(srcs)

    def body(*refs):
        src, land = refs[:nt], refs[nt:2 * nt]
        sems = refs[2 * nt + 1:5 * nt + 1]
        token = refs[-1]
        me, sibling, chips = _chip_peers()
        for t in range(nt):
            send, recv_ici, recv_sib = sems[3 * t], sems[3 * t + 1], sems[3 * t + 2]
            pltpu.make_async_remote_copy(src_ref=src[t], dst_ref=land[t].at[me], send_sem=send.at[0],
                                         recv_sem=recv_sib.at[0], device_id=sibling[0],
                                         device_id_type=pl.DeviceIdType.MESH).start()
            for j, (pid, _) in enumerate(chips):
                pltpu.make_async_remote_copy(src_ref=src[t], dst_ref=land[t].at[me], send_sem=send.at[1 + j],
                                             recv_sem=recv_ici.at[j], device_id=pid,
                                             device_id_type=pl.DeviceIdType.MESH).start()
        token[...] = jnp.zeros_like(token)

    out_shape = []
    for _ in range(nt):
        out_shape += [pltpu.SemaphoreType.DMA((1 + N_CHIPS_OTHER,)), pltpu.SemaphoreType.DMA((N_CHIPS_OTHER,)),
                      pltpu.SemaphoreType.DMA((1,))]
    out_shape += [pltpu.HBM(a.shape, a.dtype) for a in list(srcs) + list(lands)]
    out_shape += [jax.ShapeDtypeStruct((8, LANES), F32)]
    res = pl.pallas_call(
        body, name=name, out_shape=out_shape,
        in_specs=[_HBM] * (2 * nt) + [_ANY],
        out_specs=[_SEM] * (3 * nt) + [_HBM] * (2 * nt) + [pl.BlockSpec(memory_space=pltpu.VMEM)],
        input_output_aliases={i: 3 * nt + i for i in range(2 * nt)},
        compiler_params=pltpu.CompilerParams(has_side_effects=_EFFECT),
    )(*[pltpu.with_memory_space_constraint(a, pltpu.HBM) for a in list(srcs) + list(lands)], after)
    sems, thru = res[:3 * nt], res[3 * nt:5 * nt]
    return [(sems[3 * t], sems[3 * t + 1], sems[3 * t + 2], thru[t], thru[nt + t]) for t in range(nt)], res[-1]


def _gather2_relay(handles, after, name):
    nt = len(handles)

    def body(*refs):
        src, land = refs[:nt], refs[nt:2 * nt]
        send1, recv_ici = refs[2 * nt:3 * nt], refs[3 * nt:4 * nt]
        outs = refs[4 * nt + 1:]
        send2, recv2 = outs[:nt], outs[nt:2 * nt]
        token = refs[-1]
        token[...] = jnp.zeros_like(token)
        me, sibling, chips = _chip_peers()
        for t in range(nt):
            pltpu.make_async_remote_copy(src_ref=src[t], dst_ref=land[t].at[me], send_sem=send1[t].at[0],
                                         recv_sem=recv_ici[t].at[0], device_id=sibling[0],
                                         device_id_type=pl.DeviceIdType.MESH).wait_send()
            for j, (pid, plin) in enumerate(chips):
                arrived = pltpu.make_async_remote_copy(src_ref=src[t], dst_ref=land[t].at[plin], send_sem=send1[t].at[1 + j],
                                                       recv_sem=recv_ici[t].at[j], device_id=pid,
                                                       device_id_type=pl.DeviceIdType.MESH)
                arrived.wait_send()
                arrived.wait_recv()
                pltpu.make_async_remote_copy(src_ref=land[t].at[plin], dst_ref=land[t].at[plin], send_sem=send2[t].at[j],
                                             recv_sem=recv2[t].at[j], device_id=sibling[0],
                                             device_id_type=pl.DeviceIdType.MESH).start()

    srcs = [h[3] for h in handles]
    lands = [h[4] for h in handles]
    out_shape = [pltpu.SemaphoreType.DMA((N_CHIPS_OTHER,)) for _ in range(2 * nt)]
    out_shape += [pltpu.HBM(a.shape, a.dtype) for a in srcs + lands]
    out_shape += [jax.ShapeDtypeStruct((8, LANES), F32)]
    res = pl.pallas_call(
        body, name=name, out_shape=out_shape,
        in_specs=[_HBM] * (2 * nt) + [_SEM] * (2 * nt) + [_ANY],
        out_specs=[_SEM] * (2 * nt) + [_HBM] * (2 * nt) + [pl.BlockSpec(memory_space=pltpu.VMEM)],
        input_output_aliases={i: 2 * nt + i for i in range(2 * nt)},
        compiler_params=pltpu.CompilerParams(has_side_effects=_EFFECT),
    )(*srcs, *lands, *[h[0] for h in handles], *[h[1] for h in handles], after)
    return [(handles[t][2], res[t], res[nt + t], res[3 * nt + t]) for t in range(nt)], res[-1]


def _gather2_wait(handle, after, name):
    recv_sib, send2, recv2, land = handle

    def body(land_ref, recv_sib_sem, send2_sem, recv2_sem, after_ref, land_out):
        me, sibling, chips = _chip_peers()
        pltpu.make_async_remote_copy(src_ref=land_ref.at[me], dst_ref=land_ref.at[sibling[1]], send_sem=send2_sem.at[0],
                                     recv_sem=recv_sib_sem.at[0], device_id=sibling[0],
                                     device_id_type=pl.DeviceIdType.MESH).wait_recv()
        for j, (pid, plin) in enumerate(chips):
            relayed = pltpu.make_async_remote_copy(src_ref=land_ref.at[plin], dst_ref=land_ref.at[plin], send_sem=send2_sem.at[j],
                                                   recv_sem=recv2_sem.at[j], device_id=sibling[0],
                                                   device_id_type=pl.DeviceIdType.MESH)
            relayed.wait_send()
            relayed.wait_recv()

    return pl.pallas_call(
        body, name=name, out_shape=pltpu.HBM(land.shape, land.dtype),
        in_specs=(_HBM, _SEM, _SEM, _SEM, _ANY), out_specs=_HBM, input_output_aliases={0: 0},
        compiler_params=pltpu.CompilerParams(has_side_effects=_EFFECT),
    )(land, recv_sib, send2, recv2, after)


def _exchange(arrs, scatter, name):
    nt = len(arrs)
    out_shape = [jax.ShapeDtypeStruct(a.shape if scatter else (N_DEV,) + a.shape, a.dtype) for a in arrs]

    def body(*refs):
        ins, outs = refs[:nt], refs[nt:2 * nt]
        send, recv, loc = refs[2 * nt:]
        me, peers = _peers()
        copies = []
        for t in range(nt):
            own = pltpu.make_async_copy(ins[t].at[me] if scatter else ins[t], outs[t].at[me], loc.at[t])
            own.start()
            copies.append(own)
            for k, (pid, plin) in enumerate(peers):
                cp = pltpu.make_async_remote_copy(
                    src_ref=ins[t].at[plin] if scatter else ins[t], dst_ref=outs[t].at[me],
                    send_sem=send.at[t, k], recv_sem=recv.at[t, k],
                    device_id=pid, device_id_type=pl.DeviceIdType.MESH)
                cp.start()
                copies.append(cp)
        for cp in copies:
            cp.wait()

    any_spec = pl.BlockSpec(memory_space=pl.ANY)
    return pl.pallas_call(
        body, out_shape=out_shape, in_specs=[any_spec] * nt, out_specs=[any_spec] * nt,
        scratch_shapes=[pltpu.SemaphoreType.DMA((nt, N_DEV - 1)), pltpu.SemaphoreType.DMA((nt, N_DEV - 1)),
                        pltpu.SemaphoreType.DMA((nt,))],
        name=name)(*arrs)


def _adamw_body(n_parts, stacked=True):
    def body(p_ref, w_ref, m_ref, v_ref, *rest):
        g_out, d_out, m_out, v_out = rest[-4:]
        part = (lambda i: p_ref[i]) if stacked else (lambda i: p_ref[i][...])
        g = part(0).astype(F32)
        for i in range(1, n_parts):
            g = g + part(i).astype(F32)
        m2 = ADAM_B1 * m_ref[...] + (1.0 - ADAM_B1) * g
        v2 = ADAM_B2 * v_ref[...] + (1.0 - ADAM_B2) * jnp.square(g)
        m_hat = m2 / (1.0 - ADAM_B1 ** ADAM_STEP)
        v_hat = v2 / (1.0 - ADAM_B2 ** ADAM_STEP)
        g_out[...] = g
        d_out[...] = -ADAM_LR * (m_hat / (jnp.sqrt(v_hat) + ADAM_EPS) + ADAM_WD * w_ref[...])
        m_out[...] = m2
        v_out[...] = v2

    return body


def _adamw_layer(own, land, me, w, m, v, layer, prev, name):
    _, r, c = own.shape
    tr = _tile(r, 256, 8)
    blk = pl.BlockSpec((None, tr, c), lambda i, me_ref: (layer, i, 0))
    share = lambda k: pl.BlockSpec((None, tr, c), lambda i, me_ref: (me_ref[0] ^ k, i, 0))
    in_specs = [share(k) for k in range(N_DEV)] + [blk, blk, blk]
    args = [own] + [land] * (N_DEV - 1) + [w, m, v]
    aliases = {}
    if prev is not None:
        in_specs += [_ANY] * 4
        args += list(prev)
        aliases = {1 + N_DEV + 3 + i: i for i in range(4)}

    def body(me_ref, *refs):
        token_ref = refs[-1]
        refs = (refs[:N_DEV],) + refs[N_DEV:-1]
        _adamw_body(N_DEV, stacked=False)(*refs)
        token_ref[...] = jnp.zeros(token_ref.shape, F32)

    token_blk = pl.BlockSpec((8, LANES), lambda i, me_ref: (0, 0))
    res = pl.pallas_call(
        body,
        grid_spec=pltpu.PrefetchScalarGridSpec(num_scalar_prefetch=1, grid=(r // tr,), in_specs=in_specs,
                                               out_specs=[blk] * 4 + [token_blk]),
        out_shape=[jax.ShapeDtypeStruct(w.shape, F32)] * 4 + [jax.ShapeDtypeStruct((8, LANES), F32)],
        input_output_aliases=aliases, name=name, compiler_params=_cparams(1))(me, *args)
    return res[:4], res[4]


def _adamw(parts, w, m, v, name):
    p, nl, r, c = parts.shape
    tr = _tile(r, 256, 8)
    body = _adamw_body(p)

    blk = (None, tr, c)
    imap = lambda l, i: (l, i, 0)
    out = ((nl, r, c), F32, blk, imap)
    return _call(body, grid=(nl, r // tr),
                 ins=[(parts, (p, None, tr, c), lambda l, i: (0, l, i, 0)), (w, blk, imap), (m, blk, imap), (v, blk, imap)],
                 outs=[out] * 4, name=name)


def _rows(x):
    return x.reshape(-1, LANES)


def _pad_rows(x, mult=8):
    r = x.shape[0]
    extra = (-r) % mult
    return jnp.pad(x, ((0, extra), (0, 0))) if extra else x


def kernel(x, c, ada_w, ada_b, ln_g, ln_b, dn_w_in, dn_conv_w, dn_a_log, dn_dt_bias, dn_norm_w, dn_w_out, cf_w_in, cf_dw_w, cf_dw_b, cf_ln_g, cf_ln_b, cf_w_out, ff_w1, ff_w2, loss_target, m_ada_w, m_ada_b, m_ln_g, m_ln_b, m_dn_w_in, m_dn_conv_w, m_dn_a_log, m_dn_dt_bias, m_dn_norm_w, m_dn_w_out, m_cf_w_in, m_cf_dw_w, m_cf_dw_b, m_cf_ln_g, m_cf_ln_b, m_cf_w_out, m_ff_w1, m_ff_w2, v_ada_w, v_ada_b, v_ln_g, v_ln_b, v_dn_w_in, v_dn_conv_w, v_dn_a_log, v_dn_dt_bias, v_dn_norm_w, v_dn_w_out, v_cf_w_in, v_cf_dw_w, v_cf_dw_b, v_cf_ln_g, v_cf_ln_b, v_cf_w_out, v_ff_w1, v_ff_w2):
    depth, d, _ = ada_w.shape
    n_a, n_b = dn_w_in.shape[0], cf_w_in.shape[0]
    heads = dn_a_log.shape[1]
    hw = heads * HEAD_DIM
    taps = cf_dw_w.shape[1]
    s = x.shape[1]
    alpha = (2.0 * depth) ** 0.25
    me = 4 * lax.axis_index("x") + 2 * lax.axis_index("y") + lax.axis_index("c")
    me_arr = jnp.reshape(me, (1,)).astype(jnp.int32)
    xs, tgt = x[0], loss_target[0]

    dn_in_cols = dn_w_in.shape[2]
    keys, shards = [], []
    for i in range(depth):
        j = i // 2
        mixer = [("dn_in", dn_w_in), ("dn_out", dn_w_out)] if i % 2 == 0 else [("cf_in", cf_w_in), ("cf_out", cf_w_out)]
        for nm, wt in mixer:
            keys.append((nm, j))
            shards.append(wt[j].astype(MXU_DTYPE))
        keys += [("ff1", i), ("ff2", i)]
        shards += [ff_w1[i].astype(MXU_DTYPE), ff_w2[i].astype(MXU_DTYPE)]

    small_local = [_rows(ln_g), _rows(ln_b), _rows(dn_conv_w), _rows(cf_dw_w), _rows(cf_dw_b), _rows(cf_ln_g),
                   _rows(cf_ln_b), _rows(c)]
    sizes = [a.shape[0] for a in small_local]
    packed = _pad_rows(jnp.concatenate(small_local, axis=0))
    (small_all,) = _exchange([packed], False, "comm_gather_params")
    offs = [0]
    for z in sizes:
        offs.append(offs[-1] + z)

    def small(i):
        return small_all[:, offs[i]:offs[i + 1], :]

    def unshard(piece, lead, groups):
        t = piece.reshape((N_DEV,) + lead + (groups * LANES,))
        t = jnp.moveaxis(t, 0, len(lead))
        return t.reshape(lead + (N_DEV * groups * LANES,))

    ln_g_f = unshard(small(0), (depth, 2), 1)
    ln_b_f = unshard(small(1), (depth, 2), 1)
    conv_w_f = unshard(small(2), (n_a, DN_CONV), 3 * heads // N_DEV)
    dw_w_f = unshard(small(3), (n_b, taps), 1)
    dw_b_f = unshard(small(4), (n_b,), 1)
    cf_ln_g_f = unshard(small(5), (n_b,), 1)
    cf_ln_b_f = unshard(small(6), (n_b,), 1)
    c_all = small(7).reshape(N_DEV, d)

    mod_part, cond_all = _ada_fwd(c_all, ada_w)
    (mod_all,) = _exchange([mod_part], False, "comm_gather_mod")
    mod_mine = lax.dynamic_index_in_dim(mod_all, me, axis=2, keepdims=False)
    mod_mine = jnp.moveaxis(mod_mine, 0, 1).reshape(depth, N_MOD * d)

    lands = [_landing(a, me) for a in shards]
    first, token = _gather2_start(shards[:1], lands[:1], mod_all, "gather_first_weight_start")
    handles = {keys[0]: first[0]}
    groups = [keys[:1], keys[1:4]] + [keys[4 * i:4 * i + 4] for i in range(1, depth)]
    group_of = {k: n for n, grp in enumerate(groups) for k in grp}
    relayed, weights = {}, {}
    wait_after = {}

    def relay(n, after):
        if n < len(groups) and groups[n][0] not in relayed:
            hs, relay_token = _gather2_relay([handles[k] for k in groups[n]], after, "gather_relay_%d" % n)
            relayed.update(zip(groups[n], hs))
            return relay_token

    def gathered(key, after):
        if key not in weights:
            relay(group_of[key], after)
            if key[0] == "ff1":
                relay(key[1] + 2, after)
            weights[key] = _gather2_wait(relayed[key], wait_after.get(key, after), "gather_wait_%s_%d" % key)
        return weights[key]

    def get_dn_in(j):
        def get(after):
            g = gathered(("dn_in", j), after)
            w = jnp.moveaxis(g, 0, 1).reshape(d, N_DEV * dn_in_cols)
            return jnp.pad(w, ((0, 0), (0, 4 * hw + LANES - N_DEV * dn_in_cols)))
        return get

    def get_rows(key):
        return lambda after: gathered(key, after).reshape((-1, d))

    def get_cols(key):
        return lambda after: gathered(key, after)

    def add_bias(a, b):
        return (a + b,), ()

    (mod,), _ = _rowmap(add_bias, [mod_mine, ada_b], [], [(N_MOD * d, F32)], [], "ada_bias", pin=token)
    mod_rows = mod.reshape(depth * N_MOD, 1, d)
    ln_g_rows = ln_g_f.reshape(depth * 2, 1, d)
    ln_b_rows = ln_b_f.reshape(depth * 2, 1, d)

    def mod_row(i, j):
        return (mod_rows, i * N_MOD + j)

    def ln_row(rows, i, j):
        return (rows, i * 2 + j)

    subs = []
    h_cur = _modulate_fwd(xs, mod_row(0, 1), mod_row(0, 0))
    rest, wait_after[keys[0]] = _gather2_start(shards[1:], lands[1:], relay(0, h_cur), "gather_weights_start")
    handles.update(zip(keys[1:], rest))
    x_cur = xs
    last = None
    for i in range(depth):
        j = i // 2
        if i % 2 == 0:
            y, res = _deltanet_fwd(h_cur, get_dn_in(j), conv_w_f[j], dn_a_log[j], dn_dt_bias[j], dn_norm_w[j],
                                   get_rows(("dn_out", j)))
        else:
            y, res = _conformer_fwd(h_cur, get_cols(("cf_in", j)), dw_w_f[j], dw_b_f[j][None, :], cf_ln_g_f[j][None, :],
                                    cf_ln_b_f[j][None, :], get_rows(("cf_out", j)))
        p1 = (mod_row(i, 2), ln_row(ln_g_rows, i, 0), ln_row(ln_b_rows, i, 0), mod_row(i, 4), mod_row(i, 3))
        x_mid, h_mid = _combine_fwd(alpha, x_cur, y, *p1)
        subs.append((x_cur, y, p1, res))
        m_out, res2 = _mlp_fwd(h_mid, get_cols(("ff1", i)), get_rows(("ff2", i)))
        if i + 1 < depth:
            p2 = (mod_row(i, 5), ln_row(ln_g_rows, i, 1), ln_row(ln_b_rows, i, 1), mod_row(i + 1, 1), mod_row(i + 1, 0))
            x_next, h_next = _combine_fwd(alpha, x_mid, m_out, *p2)
            subs.append((x_mid, m_out, p2, res2))
            x_cur, h_cur = x_next, h_next
        else:
            p2 = (mod_row(i, 5), ln_row(ln_g_rows, i, 1), ln_row(ln_b_rows, i, 1))
            last = (x_mid, m_out, p2, res2)

    x_in, y_in, p_last, res_last = last
    dx, dy, (loss_acc, g_gt, g_g, g_b) = _last_fwd_bwd(alpha, x_in, y_in, tgt, *p_last)
    loss = lax.psum(loss_acc[0, 0], ("x", "y", "c"))

    d_mod = [[None] * N_MOD for _ in range(depth)]
    d_ln_g = [[None, None] for _ in range(depth)]
    d_ln_b = [[None, None] for _ in range(depth)]
    d_mod[depth - 1][5], d_ln_g[depth - 1][1], d_ln_b[depth - 1][1] = g_gt, g_g, g_b
    gw = dict(dn=[None] * n_a, cf=[None] * n_b)

    sent = {}

    def send_grads(named, tag):
        parts = [p for _, p in named]
        hs, tok = _xfer_start(parts, [lax.empty(p.shape, p.dtype) for p in parts], True, parts[0], "scatter_start_" + tag)
        for (key, _), hnd in zip(named, hs):
            sent[key] = hnd
        return tok

    def by_rows(g):
        return g.reshape((N_DEV, g.shape[0] // N_DEV, g.shape[1]))

    def send_mlp(i, d_w1, d_w2):
        return send_grads([(("ff1", i), d_w1), (("ff2", i), by_rows(d_w2))], "ff_%d" % i)

    dh, d_w1, d_w2 = _mlp_bwd(res_last, dy)
    pin = send_mlp(depth - 1, d_w1, d_w2)
    for idx in range(len(subs) - 1, -1, -1):
        x_in, y_in, prm, res = subs[idx]
        i, second = idx // 2, idx % 2
        dx, dy, (g_gt, g_g, g_b, g_sc, g_sh) = _combine_bwd(alpha, x_in, y_in, dx, dh, *prm, pin=pin)
        d_mod[i][5 if second else 2], d_ln_g[i][second], d_ln_b[i][second] = g_gt, g_g, g_b
        nxt_i, nxt_base = (i + 1, 0) if second else (i, 3)
        d_mod[nxt_i][nxt_base + 1], d_mod[nxt_i][nxt_base] = g_sc, g_sh
        j = i // 2
        if second:
            dh, d_w1, d_w2 = _mlp_bwd(res, dy)
            pin = send_mlp(i, d_w1, d_w2)
        elif i % 2 == 0:
            def send_dn(d_w_in, d_w_out, j=j):
                d_in = d_w_in[:, :N_DEV * dn_in_cols].reshape(d, N_DEV, dn_in_cols)
                return send_grads([(("dn_in", j), jnp.moveaxis(d_in, 1, 0)), (("dn_out", j), by_rows(d_w_out))],
                                  "dn_%d" % j)

            dh, gw["dn"][j], pin = _deltanet_bwd(res, dy, conv_w_f[j], send_dn)
        else:
            dh, gw["cf"][j] = _conformer_bwd(res, dy, dw_w_f[j], cf_ln_g_f[j][None, :], cf_ln_b_f[j][None, :])
            pin = send_grads([(("cf_in", j), gw["cf"][j]["w_in"]), (("cf_out", j), by_rows(gw["cf"][j]["w_out"]))],
                             "cf_%d" % j)
    grad_x, g_sc, g_sh = _modulate_bwd(xs, dx, dh, mod_row(0, 1), mod_row(0, 0), pin=pin)
    d_mod[0][1], d_mod[0][0] = g_sc, g_sh
    d_mod_full = jnp.concatenate([jnp.concatenate(r, axis=1) for r in d_mod], axis=0)

    stacked = {"dn_w_in": ("dn_in", dn_w_in, m_dn_w_in, v_dn_w_in), "dn_w_out": ("dn_out", dn_w_out, m_dn_w_out, v_dn_w_out),
               "cf_w_in": ("cf_in", cf_w_in, m_cf_w_in, v_cf_w_in), "cf_w_out": ("cf_out", cf_w_out, m_cf_w_out, v_cf_w_out),
               "ff_w1": ("ff1", ff_w1, m_ff_w1, v_ff_w1), "ff_w2": ("ff2", ff_w2, m_ff_w2, v_ff_w2)}
    chains = {key: None for key in stacked}

    def update_layer(i, token):
        mixer = ["dn_w_in", "dn_w_out"] if i % 2 == 0 else ["cf_w_in", "cf_w_out"]
        for key, idx in [("ff_w1", i), ("ff_w2", i)] + [(k, i // 2) for k in mixer]:
            short, w, m, v = stacked[key]
            land = _xfer_wait(sent[(short, idx)], True, token, "scatter_wait_%s_%d" % (short, idx))
            chains[key], token = _adamw_layer(sent[(short, idx)][2], land, me_arr, w, m, v, idx, chains[key],
                                              "adamw_%s_%d" % (key, idx))
        return token

    def stack_rows(lst):
        return jnp.stack(lst, axis=0)

    gs_ln_g = jnp.stack([jnp.concatenate(r, axis=0) for r in d_ln_g], axis=0)
    gs_ln_b = jnp.stack([jnp.concatenate(r, axis=0) for r in d_ln_b], axis=0)
    gs_conv_w = stack_rows([gw["dn"][j]["conv_w"] for j in range(n_a)])
    gs_dw_w = stack_rows([gw["cf"][j]["dw_w"] for j in range(n_b)])
    gs_dw_b = stack_rows([gw["cf"][j]["dw_b"] for j in range(n_b)])
    gs_cf_ln_g = stack_rows([gw["cf"][j]["ln_g"] for j in range(n_b)])
    gs_cf_ln_b = stack_rows([gw["cf"][j]["ln_b"] for j in range(n_b)])
    gs_a_log = stack_rows([_pad_lanes(gw["dn"][j]["a_log"], 0)[0] for j in range(n_a)])
    gs_dt_bias = stack_rows([_pad_lanes(gw["dn"][j]["dt_bias"], 0)[0] for j in range(n_a)])
    gs_norm_w = stack_rows([gw["dn"][j]["norm_w"] for j in range(n_a)])
    small_grads = [gs_ln_g, gs_ln_b, gs_conv_w, gs_dw_w, gs_dw_b, gs_cf_ln_g, gs_cf_ln_b, gs_a_log, gs_dt_bias,
                   gs_norm_w, d_mod_full]
    sg_rows = [_rows(a) for a in small_grads]
    sg_sizes = [a.shape[0] for a in sg_rows]
    sg_packed = _pad_rows(jnp.concatenate(sg_rows, axis=0))
    (sg_handle,), sg_token = _xfer_start([sg_packed], [_landing(sg_packed, me)], False, grad_x, "gather_small_grads_start")
    for i in range(depth - 1, -1, -1):
        sg_token = update_layer(i, sg_token)
    sg_all = _xfer_wait(sg_handle, False, sg_token, "gather_small_grads_wait")
    sg_offs = [0]
    for z in sg_sizes:
        sg_offs.append(sg_offs[-1] + z)

    def sg(i, shape):
        return sg_all[:, sg_offs[i]:sg_offs[i + 1], :].reshape((N_DEV,) + shape)

    dmod_all = sg(10, (depth, N_MOD * d))
    nl = ada_w.shape[2]
    dmod_cols = lax.dynamic_slice_in_dim(dmod_all, me * nl, nl, axis=2)
    g_ada_w = _ada_bwd(cond_all, jnp.moveaxis(dmod_cols, 0, 1))

    outs = {}

    def run_adamw(key, parts, w, m, v):
        shp = w.shape
        as3 = lambda t: t.reshape((-1,) + shp[-2:]) if t.ndim >= 3 else t.reshape((1,) + shp)
        parts3 = parts.reshape((parts.shape[0],) + as3(w).shape)
        res = _adamw(parts3, as3(w), as3(m), as3(v), "adamw_" + key)
        outs[key] = tuple(r.reshape(shp) for r in res)

    run_adamw("ada_w", g_ada_w[None], ada_w, m_ada_w, v_ada_w)

    cgroups = 3 * heads // N_DEV
    n_sharded = 7

    def my_cols(first, last, groups):
        x = sg_all[:, sg_offs[first]:sg_offs[last], :].reshape(N_DEV, -1, N_DEV, groups, LANES)
        return lax.dynamic_index_in_dim(x, me, axis=2, keepdims=False).reshape(N_DEV, -1, LANES)

    small_parts = [my_cols(0, 2, 1), my_cols(2, 3, cgroups), my_cols(3, n_sharded, 1),
                   sg_all[:, sg_offs[n_sharded]:sg_offs[-1], :]]
    sp_offs = [0]
    for n, z in enumerate(sg_sizes):
        sp_offs.append(sp_offs[-1] + (z // N_DEV if n < n_sharded else z))
    parts_packed = jnp.zeros((N_DEV, sp_offs[-1] + (-sp_offs[-1]) % 8, LANES), F32)
    at = 0
    for part in small_parts:
        parts_packed = lax.dynamic_update_slice(parts_packed, part, (0, at, 0))
        at += part.shape[1]

    def pad_heads(t):
        return jnp.pad(t, ((0, 0), (0, LANES - heads)))

    def pack_state(ln_g_, ln_b_, conv_w_, dw_w_, dw_b_, cln_g_, cln_b_, a_log_, dt_b_, norm_w_, ada_b_):
        rows = [_rows(ln_g_), _rows(ln_b_), _rows(conv_w_), _rows(dw_w_), _rows(dw_b_), _rows(cln_g_), _rows(cln_b_),
                pad_heads(a_log_), pad_heads(dt_b_), norm_w_, _rows(ada_b_)]
        return _pad_rows(jnp.concatenate(rows, axis=0))

    w_s = pack_state(ln_g, ln_b, dn_conv_w, cf_dw_w, cf_dw_b, cf_ln_g, cf_ln_b, dn_a_log, dn_dt_bias, dn_norm_w, ada_b)
    m_s = pack_state(m_ln_g, m_ln_b, m_dn_conv_w, m_cf_dw_w, m_cf_dw_b, m_cf_ln_g, m_cf_ln_b, m_dn_a_log,
                     m_dn_dt_bias, m_dn_norm_w, m_ada_b)
    v_s = pack_state(v_ln_g, v_ln_b, v_dn_conv_w, v_cf_dw_w, v_cf_dw_b, v_cf_ln_g, v_cf_ln_b, v_dn_a_log,
                     v_dn_dt_bias, v_dn_norm_w, v_ada_b)
    res_s = _adamw(parts_packed[:, None], w_s[None], m_s[None], v_s[None], "adamw_small")
    small_keys = ["ln_g", "ln_b", "dn_conv_w", "cf_dw_w", "cf_dw_b", "cf_ln_g", "cf_ln_b", "dn_a_log", "dn_dt_bias",
                  "dn_norm_w", "ada_b"]
    small_shapes = [ln_g.shape, ln_b.shape, dn_conv_w.shape, cf_dw_w.shape, cf_dw_b.shape, cf_ln_g.shape,
                    cf_ln_b.shape, dn_a_log.shape, dn_dt_bias.shape, dn_norm_w.shape, ada_b.shape]
    for n, (key, shp) in enumerate(zip(small_keys, small_shapes)):
        vals = []
        for r in res_s:
            piece = r[0, sp_offs[n]:sp_offs[n + 1], :]
            if key in ("dn_a_log", "dn_dt_bias"):
                piece = piece[:, :heads]
            vals.append(piece.reshape(shp))
        outs[key] = tuple(vals)

    for key in stacked:
        outs[key] = tuple(chains[key])

    order = ["ada_w", "ada_b", "ln_g", "ln_b", "dn_w_in", "dn_conv_w", "dn_a_log", "dn_dt_bias", "dn_norm_w",
             "dn_w_out", "cf_w_in", "cf_dw_w", "cf_dw_b", "cf_ln_g", "cf_ln_b", "cf_w_out", "ff_w1", "ff_w2"]
    result = [loss, grad_x[None]]
    for part in range(4):
        result += [outs[k][part] for k in order]
    return tuple(result)
```

```python
import functools

import jax
import jax.numpy as jnp
from jax import lax
from jax.experimental import pallas as pl
from jax.experimental.pallas import tpu as pltpu

F32 = jnp.float32
MXU_DTYPE = jnp.bfloat16
N_DEV = 8
LANES = 128
HEAD_DIM = 128
CHUNK = 64
DN_CONV = 4
N_MOD = 6
LN_EPS = 1e-5
RMS_EPS = 1e-6
L2_EPS = 1e-6
ADAM_LR = 0.001
ADAM_B1 = 0.9
ADAM_B2 = 0.999
ADAM_EPS = 1e-08
ADAM_WD = 0.01
ADAM_STEP = 10

NN = ((1,), (0,))
NT = ((1,), (1,))
TN = ((0,), (0,))

ROW_TILE = 512
CONV_TILE = 256
SHORT_CONV_TILE = 1024


def _mdot(a, b, dims=NN):
    return lax.dot_general(a.astype(MXU_DTYPE), b.astype(MXU_DTYPE), (dims, ((), ())), preferred_element_type=F32)


def _split3(x):
    hi = x.astype(MXU_DTYPE)
    r1 = x - hi.astype(F32)
    mid = r1.astype(MXU_DTYPE)
    lo = (r1 - mid.astype(F32)).astype(MXU_DTYPE)
    return hi, mid, lo


def _dot01(a, b, dims=NN, mask_first=True):
    d = lambda p, q: lax.dot_general(p, q, (dims, ((), ())), preferred_element_type=F32)
    if mask_first:
        m = a.astype(MXU_DTYPE)
        return sum(d(m, p) for p in _split3(b))
    m = b.astype(MXU_DTYPE)
    return sum(d(p, m) for p in _split3(a))


def _cparams(n):
    return pltpu.CompilerParams(dimension_semantics=("arbitrary",) * n)


def _call(body, *, grid, ins, outs, name, scratch=()):
    res = pl.pallas_call(
        body,
        grid=grid,
        in_specs=[pl.BlockSpec(memory_space=pl.ANY) if b is None else pl.BlockSpec(b, m) for _, b, m in ins],
        out_specs=[pl.BlockSpec(b, m) for _, _, b, m in outs],
        out_shape=[jax.ShapeDtypeStruct(s, d) for s, d, _, _ in outs],
        scratch_shapes=list(scratch),
        name=name,
        compiler_params=_cparams(len(grid)),
    )(*[a for a, _, _ in ins])
    return res


def _tile(n, pref, unit=LANES):
    if n <= pref:
        return n
    t = (pref // unit) * unit
    while t > unit and n % t:
        t -= unit
    assert n % t == 0, (n, pref)
    return t


def _rowmap(fn, rows, consts, row_outs, acc_outs, name, pin=None):
    rows = [r if isinstance(r, tuple) else (r, r.shape[1], 0) for r in rows]
    s = rows[0][0].shape[0]
    tm = min(ROW_TILE, s)
    nr, nc, no, na = len(rows), len(consts), len(row_outs), len(acc_outs)
    npin = 0 if pin is None else 1

    def body(*refs):
        rin, cin = refs[:nr], refs[nr:nr + nc]
        refs = refs[:nr + nc] + refs[nr + nc + npin:]
        rout, aout = refs[nr + nc:nr + nc + no], refs[nr + nc + no:]
        ro, ao = fn(*[r[...] for r in rin], *[c[...] for c in cin])
        for ref, val in zip(rout, ro):
            ref[...] = val.astype(ref.dtype)
        if na:
            first = pl.program_id(0) == 0

            @pl.when(first)
            def _():
                for ref, val in zip(aout, ao):
                    ref[...] = val

            @pl.when(jnp.logical_not(first))
            def _():
                for ref, val in zip(aout, ao):
                    ref[...] += val

    ins = [(a, (tm, w), functools.partial(lambda i, cb: (i, cb), cb=cb)) for a, w, cb in rows]
    for c in consts:
        if isinstance(c, tuple):
            ins.append((c[0], (None, 1, c[0].shape[2]), functools.partial(lambda i, n: (n, 0, 0), n=c[1])))
        else:
            ins.append((c, c.shape, lambda i: (0, 0)))
    if pin is not None:
        ins.append((pin, None, None))
    outs = [((s, w), d, (tm, w), lambda i: (i, 0)) for w, d in row_outs]
    outs += [(shp, F32, shp, lambda i: (0, 0)) for shp in acc_outs]
    res = _call(body, grid=(s // tm,), ins=ins, outs=outs, name=name)
    return res[:no], res[no:]


def _ln(z, g, b):
    mu = jnp.mean(z, -1, keepdims=True)
    var = jnp.mean(jnp.square(z - mu), -1, keepdims=True)
    return (z - mu) * lax.rsqrt(var + LN_EPS) * g + b


def _combine(alpha, x, y, gt, g, b, sc, sh):
    xn = _ln(alpha * x + (1.0 + gt) * y, g, b)
    return xn, xn * (1.0 + sc) + sh


def _modulate_fwd(x, sc, sh):
    def fn(x, sc, sh):
        return ((x * (1.0 + sc) + sh),), ()

    (h,), _ = _rowmap(fn, [x], [sc, sh], [(x.shape[1], MXU_DTYPE)], [], "modulate_fwd")
    return h


def _modulate_bwd(x, dx, dh, sc, sh, pin=None):
    d = x.shape[1]

    def fn(x, dx, dh, sc, sh):
        _, vjp = jax.vjp(lambda x, sc, sh: x * (1.0 + sc) + sh, x, sc, sh)
        gx, gsc, gsh = vjp(dh)
        return (dx + gx,), (gsc, gsh)

    (gx,), (gsc, gsh) = _rowmap(fn, [x, dx, dh], [sc, sh], [(d, F32)], [(1, d), (1, d)], "modulate_bwd", pin=pin)
    return gx, gsc, gsh


def _combine_fwd(alpha, x, y, gt, g, b, sc, sh):
    d = x.shape[1]

    def fn(x, y, gt, g, b, sc, sh):
        return _combine(alpha, x, y, gt, g, b, sc, sh), ()

    (xn, h), _ = _rowmap(fn, [x, y], [gt, g, b, sc, sh], [(d, F32), (d, MXU_DTYPE)], [], "combine_fwd")
    return xn, h


def _combine_bwd(alpha, x, y, dxn, dh, gt, g, b, sc, sh, pin=None):
    d = x.shape[1]

    def fn(x, y, dxn, dh, gt, g, b, sc, sh):
        _, vjp = jax.vjp(functools.partial(_combine, alpha), x, y, gt, g, b, sc, sh)
        gx, gy, ggt, gg, gb, gsc, gsh = vjp((dxn, dh))
        return (gx, gy), (ggt, gg, gb, gsc, gsh)

    (gx, gy), accs = _rowmap(fn, [x, y, dxn, dh], [gt, g, b, sc, sh], [(d, F32), (d, MXU_DTYPE)],
                             [(1, d)] * 5, "combine_bwd", pin=pin)
    return gx, gy, accs


def _last_fwd_bwd(alpha, x, y, tgt, gt, g, b):
    d = x.shape[1]

    def fn(x, y, tgt, gt, g, b):
        xn, vjp = jax.vjp(lambda x, y, gt, g, b: _ln(alpha * x + (1.0 + gt) * y, g, b), x, y, gt, g, b)
        err = xn - tgt
        gx, gy, ggt, gg, gb = vjp(err * (1.0 / d))
        rows = jnp.sum(jnp.square(err), axis=-1, keepdims=True)
        loss = (0.5 / d) * jnp.sum(rows, axis=0, keepdims=True) * jnp.ones((1, LANES), F32)
        return (gx, gy), (loss, ggt, gg, gb)

    (gx, gy), accs = _rowmap(fn, [x, y, tgt], [gt, g, b], [(d, F32), (d, MXU_DTYPE)],
                             [(1, LANES), (1, d), (1, d), (1, d)], "last_fwd_bwd")
    return gx, gy, accs


MM_VMEM_BUDGET = 40 * 2 ** 20


def _fit(options, cost):
    for o in options:
        if 2 * cost(o) <= MM_VMEM_BUDGET:
            return o
    return options[-1]


def _row_tiles(m):
    return [t for t in (2048, 1024, 512, 256) if t <= m and m % t == 0] or [m]


def _mm_call(a, a_blk, a_map, b, b_blk, b_map, outs, dims, grid, name, epi=None, extra=None, split=None, blocks=None,
             pin=None):
    nk = grid[2]
    n_out = len(outs)
    n_in = 2 + (extra is not None) + (pin is not None)

    def body(*refs):
        a_ref, b_ref = refs[0], refs[1]
        rest = refs[n_in:]
        out_refs = rest[:n_out]

        def finish(val):
            if epi == "relu2":
                out_refs[0][...] = jnp.square(jnp.maximum(val, 0.0)).astype(out_refs[0].dtype)
            elif epi == "relu2_bwd":
                sq = refs[2][...].astype(F32)
                root = jnp.where(sq > 0.0, sq * lax.rsqrt(sq), 0.0)
                out_refs[0][...] = (val * 2.0 * root).astype(out_refs[0].dtype)
            elif split is not None:
                for g in range(split[0]):
                    out_refs[0][g] = val[:, g * split[1]:(g + 1) * split[1]].astype(out_refs[0].dtype)
            else:
                out_refs[0][...] = val.astype(out_refs[0].dtype)

        if blocks is None:
            p = lax.dot_general(a_ref[...], b_ref[...], (dims, ((), ())), preferred_element_type=F32)
        else:
            p = None
            for g in range(blocks[0]):
                part = lax.dot_general(a_ref[:, g * blocks[1]:(g + 1) * blocks[1]], b_ref[g], (dims, ((), ())),
                                       preferred_element_type=F32)
                p = part if p is None else p + part
        if nk == 1:
            finish(p)
        else:
            acc = rest[n_out]
            k = pl.program_id(2)

            @pl.when(k == 0)
            def _():
                acc[...] = p

            @pl.when(k > 0)
            def _():
                acc[...] += p

            @pl.when(k == nk - 1)
            def _():
                finish(acc[...])

    if nk > 1:
        out_blk = tuple(x for x in outs[0][2] if x is not None)
        if split is not None:
            out_blk = (out_blk[1], split[0] * split[1])
        scratch = [pltpu.VMEM(out_blk, F32)]
    else:
        scratch = []
    ins = [(a, a_blk, a_map), (b, b_blk, b_map)] + ([extra] if extra is not None else [])
    ins += [(pin, None, None)] if pin is not None else []
    return _call(body, grid=grid, ins=ins, outs=outs, name=name, scratch=scratch)


def _isz(dt):
    return jnp.dtype(dt).itemsize


def _mm_nn(a, b, out_dtype, name, relu2=False):
    m, kdim = a.shape
    if b.ndim == 2:
        n = b.shape[1]
        tn = _tile(n, 1536 if n > 2048 else 512)
        b_blk, b_map = (kdim, tn), lambda i, j, k: (0, j)
    else:
        g, _, ng = b.shape
        n = g * ng
        tn = _tile(ng, 512)
        b_blk = (None, kdim, tn)
        b_map = functools.partial(lambda i, j, k, npg: (j // npg, 0, j % npg), npg=ng // tn)
    tm = _fit(_row_tiles(m), lambda t: t * kdim * _isz(a.dtype) + kdim * tn * _isz(b.dtype) + t * tn * _isz(out_dtype))
    grid = (m // tm, n // tn, 1)
    outs = [((m, n), out_dtype, (tm, tn), lambda i, j, k: (i, j))]
    return _mm_call(a, (tm, kdim), lambda i, j, k: (i, 0), b, b_blk, b_map, outs, NN, grid, name,
                    epi="relu2" if relu2 else None)[0]


def _mm_nt(a, b, out_dtype, name, relu2_sq=None, pin=None):
    m, n = a.shape
    extra_bytes = _isz(relu2_sq.dtype) if relu2_sq is not None else 0
    if b.ndim == 2:
        kout = b.shape[0]
        to = _tile(kout, 512)
        b_blk, b_map, blocks = (to, n), lambda i, j, k: (j, 0), None
    else:
        g, kout, ng = b.shape
        to = _tile(kout, 512)
        b_blk, b_map, blocks = (g, to, ng), lambda i, j, k: (0, j, 0), (g, ng)
    tm = _fit(_row_tiles(m), lambda t: t * n * _isz(a.dtype) + to * n * _isz(b.dtype)
              + t * to * (_isz(out_dtype) + extra_bytes))
    grid = (m // tm, kout // to, 1)
    outs = [((m, kout), out_dtype, (tm, to), lambda i, j, k: (i, j))]
    extra = (relu2_sq, (tm, to), lambda i, j, k: (i, j)) if relu2_sq is not None else None
    return _mm_call(a, (tm, n), lambda i, j, k: (i, 0), b, b_blk, b_map, outs, NT, grid, name,
                    epi="relu2_bwd" if relu2_sq is not None else None, extra=extra, blocks=blocks, pin=pin)[0]


def _mm_tn(a, b, out_dtype, name, split_cols=False):
    m, kdim = a.shape
    n = b.shape[1]
    tk = _tile(kdim, 512)
    tn = _tile(n, 1536)
    if not split_cols:
        out, split = ((kdim, n), out_dtype, (tk, tn), lambda i, j, k: (i, j)), None
    else:
        ng = n // N_DEV
        if tn % ng:
            tn = _tile(ng, 512)
        if tn >= ng:
            gb = tn // ng
            out = ((N_DEV, kdim, ng), out_dtype, (gb, tk, ng), lambda i, j, k: (j, i, 0))
            split = (gb, ng)
        else:
            out = ((N_DEV, kdim, ng), out_dtype, (None, tk, tn),
                   functools.partial(lambda i, j, k, npg: (j // npg, i, j % npg), npg=ng // tn))
            split = None
    grid = (kdim // tk, n // tn, 1)
    return _mm_call(a, (m, tk), lambda i, j, k: (0, i), b, (m, tn), lambda i, j, k: (0, j), [out], TN, grid, name,
                    split=split)[0]


def _shifted(xa, off, rows):
    if off % 8 == 0:
        return xa[off:off + rows]
    return pltpu.roll(xa, xa.shape[0] - off, 0)[:rows]


def _conv_pad(taps):
    return -(-(taps - 1) // 8) * 8


def _conv_tile(xp_ref, w, i, rows, taps):
    pad = _conv_pad(taps)
    r0 = pl.multiple_of(i * rows, rows)
    xa = xp_ref[pl.ds(r0, rows + pad), :]
    views = [_shifted(xa, pad - (taps - 1) + j, rows) for j in range(taps)]
    acc = w[0:1, :] * views[0]
    for j in range(1, taps):
        acc = acc + w[j:j + 1, :] * views[j]
    return r0, acc, views


def _conv_back_tile(yp_ref, w, i, rows, taps):
    pad = _conv_pad(taps)
    r0 = pl.multiple_of(i * rows, rows)
    ya = yp_ref[pl.ds(r0, rows + pad), :]
    acc = w[taps - 1:taps, :] * ya[:rows]
    for j in range(taps - 1):
        acc = acc + w[j:j + 1, :] * _shifted(ya, taps - 1 - j, rows)
    return r0, acc


def _tap_sums(dy, views, taps):
    row = lax.broadcasted_iota(jnp.int32, (taps, LANES), 0)
    acc = jnp.zeros((taps, LANES), F32)
    for j in range(taps):
        acc = acc + jnp.where(row == j, jnp.sum(dy * views[j], axis=0, keepdims=True), 0.0)
    return acc


def _silu_l2(xc, l2):
    a = jax.nn.silu(xc)
    if l2:
        a = a * lax.rsqrt(jnp.sum(a * a, axis=-1, keepdims=True) + L2_EPS)
    return a


def _dn_conv_fwd(proj, conv_w, c0, nblk, l2, name):
    s = proj.shape[0]
    pad = _conv_pad(DN_CONV)
    rows = min(SHORT_CONV_TILE, s)

    def body(x_ref, w_ref, o_ref, xp):
        xp[0:pad, :] = jnp.zeros((pad, LANES), F32)
        xp[pad:, :] = x_ref[...]
        w = w_ref[...]

        def tile(i, c):
            r0, acc, _ = _conv_tile(xp, w, i, rows, DN_CONV)
            o_ref[pl.ds(r0, rows), :] = _silu_l2(acc, l2)
            return c

        lax.fori_loop(0, s // rows, tile, 0)

    return _call(body, grid=(nblk,),
                 ins=[(proj, (s, LANES), lambda c: (0, c0 + c)), (conv_w, (DN_CONV, LANES), lambda c: (0, c0 + c))],
                 outs=[((nblk, s, LANES), F32, (None, s, LANES), lambda c: (c, 0, 0))],
                 name=name, scratch=[pltpu.VMEM((s + pad, LANES), F32)])[0]


def _dn_conv_bwd(proj, conv_w, da, c0, nblk, l2, name):
    s = proj.shape[0]
    pad = _conv_pad(DN_CONV)
    rows = min(SHORT_CONV_TILE, s)

    def body(x_ref, w_ref, da_ref, dx_ref, dw_ref, xp, yp):
        xp[0:pad, :] = jnp.zeros((pad, LANES), F32)
        xp[pad:, :] = x_ref[...]
        yp[s:, :] = jnp.zeros((pad, LANES), F32)
        w = w_ref[...]

        def tile(i, dw):
            r0, acc, views = _conv_tile(xp, w, i, rows, DN_CONV)
            _, vjp = jax.vjp(functools.partial(_silu_l2, l2=l2), acc)
            (dxc,) = vjp(da_ref[pl.ds(r0, rows), :])
            yp[pl.ds(r0, rows), :] = dxc
            return dw + _tap_sums(dxc, views, DN_CONV)

        dw_ref[...] = lax.fori_loop(0, s // rows, tile, jnp.zeros((DN_CONV, LANES), F32))

        def tile2(i, c):
            r0, acc = _conv_back_tile(yp, w, i, rows, DN_CONV)
            dx_ref[pl.ds(r0, rows), :] = acc.astype(dx_ref.dtype)
            return c

        lax.fori_loop(0, s // rows, tile2, 0)

    return _call(body, grid=(nblk,),
                 ins=[(proj, (s, LANES), lambda c: (0, c0 + c)), (conv_w, (DN_CONV, LANES), lambda c: (0, c0 + c)),
                      (da, (None, s, LANES), lambda c: (c, 0, 0))],
                 outs=[((s, nblk * LANES), MXU_DTYPE, (s, LANES), lambda c: (0, c)),
                       ((DN_CONV, nblk * LANES), F32, (DN_CONV, LANES), lambda c: (0, c))],
                 name=name, scratch=[pltpu.VMEM((s + pad, LANES), F32), pltpu.VMEM((s + pad, LANES), F32)])


def _cf_conv_fwd(vg, dw_w, dw_b):
    s, c2 = vg.shape
    ch = c2 // 2
    nblk = ch // LANES
    taps = dw_w.shape[0]
    pad = _conv_pad(taps)
    rows = min(CONV_TILE, s)

    def body(v_ref, g_ref, w_ref, b_ref, o_ref, xp):
        xp[0:pad, :] = jnp.zeros((pad, LANES), F32)
        xp[pad:, :] = v_ref[...] * jax.nn.sigmoid(g_ref[...])
        w = w_ref[...]
        bias = b_ref[...]

        def tile(i, c):
            r0, acc, _ = _conv_tile(xp, w, i, rows, taps)
            o_ref[pl.ds(r0, rows), :] = acc + bias
            return c

        lax.fori_loop(0, s // rows, tile, 0)

    return _call(body, grid=(nblk,),
                 ins=[(vg, (s, LANES), lambda c: (0, c)), (vg, (s, LANES), lambda c: (0, nblk + c)),
                      (dw_w, (taps, LANES), lambda c: (0, c)), (dw_b, (1, LANES), lambda c: (0, c))],
                 outs=[((s, ch), F32, (s, LANES), lambda c: (0, c))],
                 name="cf_conv_fwd", scratch=[pltpu.VMEM((s + pad, LANES), F32)])[0]


def _cf_conv_bwd(vg, dw_w, du):
    s, c2 = vg.shape
    ch = c2 // 2
    nblk = ch // LANES
    taps = dw_w.shape[0]
    pad = _conv_pad(taps)
    rows = min(CONV_TILE, s)

    def body(v_ref, g_ref, w_ref, du_ref, dv_ref, dg_ref, dw_ref, db_ref, xp, yp):
        sig = jax.nn.sigmoid(g_ref[...])
        xp[0:pad, :] = jnp.zeros((pad, LANES), F32)
        xp[pad:, :] = v_ref[...] * sig
        yp[0:s, :] = du_ref[...]
        yp[s:, :] = jnp.zeros((pad, LANES), F32)
        w = w_ref[...]
        db_ref[...] = jnp.sum(du_ref[...], axis=0, keepdims=True)

        def tile(i, dw):
            r0, _, views = _conv_tile(xp, w, i, rows, taps)
            return dw + _tap_sums(du_ref[pl.ds(r0, rows), :], views, taps)

        dw_ref[...] = lax.fori_loop(0, s // rows, tile, jnp.zeros((taps, LANES), F32))

        def tile2(i, c):
            r0, du0 = _conv_back_tile(yp, w, i, rows, taps)
            val = v_ref[pl.ds(r0, rows), :]
            sg = jax.nn.sigmoid(g_ref[pl.ds(r0, rows), :])
            dv_ref[pl.ds(r0, rows), :] = (du0 * sg).astype(dv_ref.dtype)
            dg_ref[pl.ds(r0, rows), :] = (du0 * val * sg * (1.0 - sg)).astype(dg_ref.dtype)
            return c

        lax.fori_loop(0, s // rows, tile2, 0)

    return _call(body, grid=(nblk,),
                 ins=[(vg, (s, LANES), lambda c: (0, c)), (vg, (s, LANES), lambda c: (0, nblk + c)),
                      (dw_w, (taps, LANES), lambda c: (0, c)), (du, (s, LANES), lambda c: (0, c))],
                 outs=[((s, ch), MXU_DTYPE, (s, LANES), lambda c: (0, c)),
                       ((s, ch), MXU_DTYPE, (s, LANES), lambda c: (0, c)),
                       ((taps, ch), F32, (taps, LANES), lambda c: (0, c)),
                       ((1, ch), F32, (1, LANES), lambda c: (0, c))],
                 name="cf_conv_bwd", scratch=[pltpu.VMEM((s + pad, LANES), F32), pltpu.VMEM((s + pad, LANES), F32)])


def _masks():
    r = lax.broadcasted_iota(jnp.int32, (CHUNK, CHUNK), 0)
    c = lax.broadcasted_iota(jnp.int32, (CHUNK, CHUNK), 1)
    return r >= c, r > c, r <= c


def _chunk_decay(g):
    causal, _, upper = _masks()
    gb = jnp.broadcast_to(g, (CHUNK, CHUNK))
    gam_r = _dot01(jnp.where(causal, 1.0, 0.0), gb)
    gam_s = _dot01(jnp.ones((CHUNK, CHUNK), F32), jnp.where(upper, gb, 0.0))
    dm = jnp.where(causal, jnp.exp(jnp.where(causal, gam_r - gam_s, 0.0)), 0.0)
    return gam_r[:, 0:1], dm


def _chunk_scores(q, k, beta, dm):
    _, strict, _ = _masks()
    both = _mdot(jnp.concatenate([k * beta, q * (HEAD_DIM ** -0.5)], axis=0), k, NT)
    return jnp.where(strict, both[:CHUNK] * dm, 0.0), both[CHUNK:] * dm


def _lockstep(gens):
    results = [None] * len(gens)
    alive = list(range(len(gens)))
    while alive:
        for i in list(alive):
            try:
                next(gens[i])
            except StopIteration as stop:
                results[i] = stop.value
                alive.remove(i)
    return results


def _chunk_prep_bwd(q, k, v, beta, gam, t, du, dw, daqk, dqd, dkd, dgl):
    causal, strict, _ = _masks()
    r = lax.broadcasted_iota(jnp.int32, (CHUNK, CHUNK), 0)
    c = lax.broadcasted_iota(jnp.int32, (CHUNK, CHUNK), 1)
    scale = HEAD_DIM ** -0.5
    eg = jnp.exp(gam)
    gam_last = gam[CHUNK - 1:CHUNK, :]
    rr = jnp.exp(gam_last - gam)
    kb = k * beta
    qs = q * scale
    vb = v * beta
    kbe = kb * eg
    gam_b = jnp.broadcast_to(gam, (CHUNK, CHUNK))
    gam_s = _dot01(jnp.ones((CHUNK, CHUNK), F32), jnp.where(r == c, gam_b, 0.0))
    both = _mdot(jnp.concatenate([kb, qs], axis=0), k, NT)
    duw = jnp.concatenate([du, dw], axis=1)
    dt = _mdot(duw, jnp.concatenate([vb, kbe], axis=1), NT)
    dvk = _mdot(t, duw, TN)
    yield
    dm = jnp.where(causal, jnp.exp(jnp.where(causal, gam_b - gam_s, 0.0)), 0.0)
    a = jnp.where(strict, both[:CHUNK] * dm, 0.0)
    aqk = both[CHUNK:] * dm
    dvb, dkbe = dvk[:, :HEAD_DIM], dvk[:, HEAD_DIM:]
    x = _mdot(t, dt, TN)
    yield
    da = jnp.where(strict, -_mdot(x, t, NT), 0.0)
    yield
    dkk = da * dm
    dqk = daqk * dm
    ddiff = da * a + daqk * aqk
    dboth = jnp.concatenate([dkk, dqk], axis=0)
    dkq = _mdot(dboth, k)
    dk_mm = _mdot(dboth, jnp.concatenate([kb, qs], axis=0), TN)
    colsum = _dot01(ddiff, jnp.ones((CHUNK, LANES), F32), TN, mask_first=False)[:, 0:1]
    yield
    dkb = dkq[:CHUNK] + dkbe * eg
    dk = dk_mm + dkb * beta + dkd * rr
    dq = (dkq[CHUNK:] + dqd * eg) * scale
    dbeta = jnp.sum(dkb * k, axis=-1, keepdims=True) + jnp.sum(dvb * v, axis=-1, keepdims=True)
    dv = dvb * beta
    deg = jnp.sum(dkbe * kb, axis=-1, keepdims=True) + jnp.sum(dqd * qs, axis=-1, keepdims=True)
    drr = jnp.sum(dkd * k, axis=-1, keepdims=True)
    dgam = deg * eg - drr * rr + jnp.sum(ddiff, axis=-1, keepdims=True) - colsum
    dgam_last = jnp.sum(drr * rr, axis=0, keepdims=True) + dgl[0:1, :] * jnp.exp(gam_last)
    row = lax.broadcasted_iota(jnp.int32, (CHUNK, 1), 0)
    dgam = dgam + jnp.where(row == CHUNK - 1, dgam_last, 0.0)
    dg = _dot01(jnp.where(causal, 1.0, 0.0), jnp.broadcast_to(dgam, (CHUNK, LANES)), TN)[:, 0:1]
    return dq, dk, dv, dbeta, dg


def _prep_group(s):
    nch = s // CHUNK
    return next(c for c in (16, 8, 4, 2, 1) if nch % c == 0)


def _tri_solve_lanes(a_l):
    n = a_l.shape[1]
    group = 8

    def body(a_ref, t_ref):
        t_ref[...] = jnp.zeros_like(t_ref)
        col = lax.broadcasted_iota(jnp.int32, (CHUNK, n), 0)

        def row(r, carry):
            r0 = pl.multiple_of(r * CHUNK, CHUNK)

            def inner(sg, acc):
                a8 = a_ref[pl.ds(r0 + pl.multiple_of(sg * group, group), group), :]
                for j in range(group):
                    t0 = pl.multiple_of((sg * group + j) * CHUNK, CHUNK)
                    acc = acc + a8[j:j + 1, :] * t_ref[pl.ds(t0, CHUNK), :]
                return acc

            acc = lax.fori_loop(0, (r + group - 1) // group, inner, jnp.zeros((CHUNK, n), F32))
            t_ref[pl.ds(r0, CHUNK), :] = jnp.where(col == r, 1.0, 0.0) - acc
            return carry

        lax.fori_loop(0, CHUNK, row, 0)

    return pl.pallas_call(body, out_shape=jax.ShapeDtypeStruct(a_l.shape, F32), name="dn_tri_solve")(a_l)


def _head_cols(bg, hh, heads):
    lane = lax.broadcasted_iota(jnp.int32, bg.shape, 1)
    beta = jnp.sum(jnp.where(lane == hh, bg, 0.0), axis=-1, keepdims=True)
    g = jnp.sum(jnp.where(lane == heads + hh, bg, 0.0), axis=-1, keepdims=True)
    return beta, g


def _dn_prep(q, k, v, bg):
    h, s, _ = q.shape
    cb = _prep_group(s)
    rb = cb * CHUNK
    big = lambda x: (x, (None, rb, HEAD_DIM), lambda n, hh: (hh, n, 0))
    sq = lambda x: (x, (None, rb, CHUNK), lambda n, hh: (hh, n, 0))
    col = lambda x: (x, (None, rb, 1), lambda n, hh: (hh, n, 0))
    tok = (bg, (rb, LANES), lambda n, hh: (n, 0))
    o_big = ((h, s, HEAD_DIM), F32, (None, rb, HEAD_DIM), lambda n, hh: (hh, n, 0))
    o_sq = ((h, s, CHUNK), F32, (None, rb, CHUNK), lambda n, hh: (hh, n, 0))
    o_col = ((h, s, 1), F32, (None, rb, 1), lambda n, hh: (hh, n, 0))

    def scores(q_ref, k_ref, bg_ref, a_ref, aqk_ref, gam_ref):
        beta, g = _head_cols(bg_ref[...], pl.program_id(1), h)
        for i in range(cb):
            sl = slice(i * CHUNK, (i + 1) * CHUNK)
            gam, dm = _chunk_decay(g[sl])
            a_ref[sl, :], aqk_ref[sl, :] = _chunk_scores(q_ref[sl, :], k_ref[sl, :], beta[sl], dm)
            gam_ref[sl, :] = gam

    a, aqk, gam = _call(scores, grid=(s // rb, h), ins=[big(q), big(k), tok], outs=[o_sq, o_sq, o_col],
                        name="dn_scores")
    n_prob = h * (s // CHUNK)
    t_l = _tri_solve_lanes(jnp.transpose(a.reshape(n_prob, CHUNK * CHUNK)))
    t = jnp.transpose(t_l).reshape(h, s, CHUNK)

    def wy(k_ref, v_ref, bg_ref, gam_ref, t_ref, u_ref, w_ref):
        beta, _ = _head_cols(bg_ref[...], pl.program_id(1), h)
        for i in range(cb):
            sl = slice(i * CHUNK, (i + 1) * CHUNK)
            kb = k_ref[sl, :] * beta[sl]
            rhs = jnp.concatenate([v_ref[sl, :] * beta[sl], kb * jnp.exp(gam_ref[sl, :])], axis=1)
            uw = _mdot(t_ref[sl, :], rhs)
            u_ref[sl, :] = uw[:, :HEAD_DIM]
            w_ref[sl, :] = uw[:, HEAD_DIM:]

    u, w = _call(wy, grid=(s // rb, h), ins=[big(k), big(v), tok, col(gam), sq(t)], outs=[o_big, o_big],
                 name="dn_wy")
    return u, w, aqk, t, gam


def _dn_prep_bwd(q, k, v, bg, gam, t, du, dw, daqk, dqd, dkd, dgl):
    h, s, _ = q.shape
    cb = _prep_group(s)
    rb = cb * CHUNK

    def body(q_ref, k_ref, v_ref, bg_ref, g_ref, t_ref, du_ref, dw_ref, da_ref, dqd_ref, dkd_ref, dgl_ref,
             dq_ref, dk_ref, dv_ref, dbg_ref):
        hh = pl.program_id(1)
        beta, _ = _head_cols(bg_ref[...], hh, h)
        slices = [slice(i * CHUNK, (i + 1) * CHUNK) for i in range(cb)]
        results = _lockstep([_chunk_prep_bwd(
            q_ref[sl, :], k_ref[sl, :], v_ref[sl, :], beta[sl], g_ref[sl, :], t_ref[sl, :],
            du_ref[sl, :], dw_ref[sl, :], da_ref[sl, :], dqd_ref[sl, :], dkd_ref[sl, :], dgl_ref[sl, :])
            for sl in slices])

        @pl.when(hh == 0)
        def _():
            dbg_ref[...] = jnp.zeros_like(dbg_ref)

        lane = lax.broadcasted_iota(jnp.int32, (CHUNK, LANES), 1)
        for sl, (dq, dk, dv, dbeta, dg) in zip(slices, results):
            dq_ref[sl, :] = dq
            dk_ref[sl, :] = dk
            dv_ref[sl, :] = dv
            dbg_ref[sl, :] += jnp.where(lane == hh, dbeta, 0.0) + jnp.where(lane == h + hh, dg, 0.0)

    big = lambda x: (x, (None, rb, HEAD_DIM), lambda n, hh: (hh, n, 0))
    sq = lambda x: (x, (None, rb, CHUNK), lambda n, hh: (hh, n, 0))
    col = lambda x: (x, (None, rb, 1), lambda n, hh: (hh, n, 0))
    tok = (bg, (rb, LANES), lambda n, hh: (n, 0))
    o_big = ((h, s, HEAD_DIM), F32, (None, rb, HEAD_DIM), lambda n, hh: (hh, n, 0))
    return _call(body, grid=(s // rb, h),
                 ins=[big(q), big(k), big(v), tok, col(gam), sq(t), big(du), big(dw), sq(daqk), big(dqd), big(dkd),
                      col(dgl)],
                 outs=[o_big, o_big, o_big, ((s, LANES), F32, (rb, LANES), lambda n, hh: (n, 0))], name="dn_prep_bwd")


def _chunk_scaled(q, k, gam):
    gam_last = gam[CHUNK - 1:CHUNK, :]
    q_dec = q * (HEAD_DIM ** -0.5) * jnp.exp(gam)
    k_dec = k * jnp.exp(gam_last - gam)
    return q_dec, k_dec, jnp.exp(gam_last)


def _scan_group(s):
    return 2 if (s // CHUNK) % 2 == 0 else 1


def _dn_scan(q, k, u, w, aqk, gam):
    h, s, _ = q.shape
    nch = s // CHUNK
    sg = _scan_group(s)
    rb = sg * CHUNK

    def body(q_ref, k_ref, u_ref, w_ref, a_ref, gam_ref, o_ref, st_ref, state):
        @pl.when(pl.program_id(0) == 0)
        def _():
            state[...] = jnp.zeros_like(state)

        def head(hh, c):
            sl = slice(c * CHUNK, (c + 1) * CHUNK)
            s0 = state[hh]
            st_ref[c, hh] = s0
            q_dec, k_dec, gl = _chunk_scaled(q_ref[hh, sl, :], k_ref[hh, sl, :], gam_ref[hh, sl, :])
            both = _mdot(jnp.concatenate([w_ref[hh, sl, :], q_dec], axis=0), s0)
            yield
            v_new = u_ref[hh, sl, :] - both[:CHUNK]
            o_ref[sl, hh * HEAD_DIM:(hh + 1) * HEAD_DIM] = both[CHUNK:] + _mdot(a_ref[hh, sl, :], v_new)
            state[hh] = s0 * gl + _mdot(k_dec, v_new, TN)

        for c in range(sg):
            _lockstep([head(hh, c) for hh in range(h)])

    big = lambda x: (x, (h, rb, HEAD_DIM), lambda n: (0, n, 0))
    return _call(body, grid=(nch // sg,),
                 ins=[big(q), big(k), big(u), big(w), (aqk, (h, rb, CHUNK), lambda n: (0, n, 0)),
                      (gam, (h, rb, 1), lambda n: (0, n, 0))],
                 outs=[((s, h * HEAD_DIM), F32, (rb, h * HEAD_DIM), lambda n: (n, 0)),
                       ((nch, h, HEAD_DIM, HEAD_DIM), F32, (sg, h, HEAD_DIM, HEAD_DIM), lambda n: (n, 0, 0, 0))],
                 name="dn_scan", scratch=[pltpu.VMEM((h, HEAD_DIM, HEAD_DIM), F32)])


def _dn_scan_bwd(q, k, u, w, aqk, gam, states, do):
    h, s, _ = q.shape
    nch = s // CHUNK
    sg = _scan_group(s)
    rb = sg * CHUNK
    ngr = nch // sg

    def body(q_ref, k_ref, u_ref, w_ref, a_ref, gam_ref, st_ref, do_ref,
             du_ref, dw_ref, da_ref, dqd_ref, dkd_ref, dgl_ref, dstate):
        @pl.when(pl.program_id(0) == 0)
        def _():
            dstate[...] = jnp.zeros_like(dstate)

        def head(hh, c):
            sl = slice(c * CHUNK, (c + 1) * CHUNK)
            s0 = st_ref[c, hh]
            ds = dstate[hh]
            doh = do_ref[sl, hh * HEAD_DIM:(hh + 1) * HEAD_DIM]
            wv = w_ref[hh, sl, :]
            q_dec, k_dec, gl = _chunk_scaled(q_ref[hh, sl, :], k_ref[hh, sl, :], gam_ref[hh, sl, :])
            ws = _mdot(wv, s0)
            dv_new = _mdot(a_ref[hh, sl, :], doh, TN) + _mdot(k_dec, ds)
            dqd_ref[hh, sl, :] = _mdot(doh, s0, NT)
            qdo = _mdot(q_dec, doh, TN)
            tot = jnp.sum(jnp.sum(s0 * ds, axis=-1, keepdims=True), axis=0, keepdims=True)
            dgl_ref[hh, sl, :] = jnp.broadcast_to(tot, (CHUNK, 1))
            yield
            v_new = u_ref[hh, sl, :] - ws
            du_ref[hh, sl, :] = dv_new
            dw_ref[hh, sl, :] = -_mdot(dv_new, s0, NT)
            da_ref[hh, sl, :] = _mdot(doh, v_new, NT)
            dkd_ref[hh, sl, :] = _mdot(v_new, ds, NT)
            dstate[hh] = ds * gl + qdo - _mdot(wv, dv_new, TN)

        for c in range(sg - 1, -1, -1):
            _lockstep([head(hh, c) for hh in range(h)])

    rev = lambda n: (0, ngr - 1 - n, 0)
    big = lambda x: (x, (h, rb, HEAD_DIM), rev)
    o_big = ((h, s, HEAD_DIM), F32, (h, rb, HEAD_DIM), rev)
    return _call(body, grid=(ngr,),
                 ins=[big(q), big(k), big(u), big(w), (aqk, (h, rb, CHUNK), rev), (gam, (h, rb, 1), rev),
                      (states, (sg, h, HEAD_DIM, HEAD_DIM), lambda n: (ngr - 1 - n, 0, 0, 0)),
                      (do, (rb, h * HEAD_DIM), lambda n: (ngr - 1 - n, 0))],
                 outs=[o_big, o_big, ((h, s, CHUNK), F32, (h, rb, CHUNK), rev), o_big, o_big,
                       ((h, s, 1), F32, (h, rb, 1), rev)],
                 name="dn_scan_bwd", scratch=[pltpu.VMEM((h, HEAD_DIM, HEAD_DIM), F32)])


def _gates(x, a_log, dt_b, h):
    lane = lax.broadcasted_iota(jnp.int32, x.shape, 1)
    return jnp.where(lane < h, jax.nn.sigmoid(x), -jnp.exp(a_log) * jax.nn.softplus(x + dt_b))


def _head_out(oh, zh, nw):
    on = oh * lax.rsqrt(jnp.mean(oh * oh, axis=-1, keepdims=True) + RMS_EPS) * nw
    return on * jax.nn.silu(zh)


def _pad_lanes(x, lo):
    return jnp.zeros((1, LANES), F32).at[0, lo:lo + x.shape[0]].set(x)


def _deltanet_fwd(hin, get_w_in, conv_w, a_log, dt_bias, norm_w, get_w_out):
    h = a_log.shape[0]
    hw = h * HEAD_DIM
    w_in = get_w_in(hin)
    proj = _mm_nn(hin, w_in, F32, "dn_proj")
    q = _dn_conv_fwd(proj, conv_w, 0, h, True, "dn_conv_q")
    k = _dn_conv_fwd(proj, conv_w, h, h, True, "dn_conv_k")
    v = _dn_conv_fwd(proj, conv_w, 2 * h, h, False, "dn_conv_v")
    alp, dtp = _pad_lanes(a_log, h), _pad_lanes(dt_bias, h)

    def gates_fn(x, al, db):
        return (_gates(x, al, db, h),), ()

    (bg,), _ = _rowmap(gates_fn, [(proj, LANES, 4 * h)], [alp, dtp], [(LANES, F32)], [], "dn_gates")
    u, w, aqk, t, gam = _dn_prep(q, k, v, bg)
    o, states = _dn_scan(q, k, u, w, aqk, gam)
    nw = norm_w[None, :]

    def out_fn(o, z, nw):
        parts = [_head_out(o[:, i * HEAD_DIM:(i + 1) * HEAD_DIM], z[:, i * HEAD_DIM:(i + 1) * HEAD_DIM], nw)
                 for i in range(h)]
        return (jnp.concatenate(parts, axis=-1),), ()

    (og,), _ = _rowmap(out_fn, [o, (proj, hw, 3)], [nw], [(hw, MXU_DTYPE)], [], "dn_out")
    w_out = get_w_out(og)
    y = _mm_nn(og, w_out, F32, "dn_y")
    return y, (hin, proj, q, k, v, bg, u, w, aqk, t, gam, states, o, og, alp, dtp, nw, w_in, w_out)


def _deltanet_bwd(res, dy, conv_w, send):
    hin, proj, q, k, v, bg, u, w, aqk, t, gam, states, o, og, alp, dtp, nw, w_in, w_out = res
    h = q.shape[0]
    hw = h * HEAD_DIM
    s = hin.shape[0]
    d_w_out = _mm_tn(og, dy, MXU_DTYPE, "dn_dwout")
    dog = _mm_nt(dy, w_out, F32, "dn_dog")

    def out_bwd(o, z, dog, nw):
        dos, dzs = [], []
        dn = jnp.zeros((1, HEAD_DIM), F32)
        for i in range(h):
            sl = slice(i * HEAD_DIM, (i + 1) * HEAD_DIM)
            _, vjp = jax.vjp(_head_out, o[:, sl], z[:, sl], nw)
            a, b, c = vjp(dog[:, sl])
            dos.append(a)
            dzs.append(b)
            dn = dn + c
        return (jnp.concatenate(dos, axis=-1), jnp.concatenate(dzs, axis=-1)), (dn,)

    (do, dz), (d_norm_w,) = _rowmap(out_bwd, [o, (proj, hw, 3), dog], [nw], [(hw, F32), (hw, MXU_DTYPE)],
                                    [(1, HEAD_DIM)], "dn_out_bwd")
    du, dw, daqk, dqd, dkd, dgl = _dn_scan_bwd(q, k, u, w, aqk, gam, states, do)
    dq, dk, dv, dbg = _dn_prep_bwd(q, k, v, bg, gam, t, du, dw, daqk, dqd, dkd, dgl)
    dpq, dwq = _dn_conv_bwd(proj, conv_w, dq, 0, h, True, "dn_conv_q_bwd")
    dpk, dwk = _dn_conv_bwd(proj, conv_w, dk, h, h, True, "dn_conv_k_bwd")
    dpv, dwv = _dn_conv_bwd(proj, conv_w, dv, 2 * h, h, False, "dn_conv_v_bwd")

    def gates_bwd(x, dbg, al, db):
        _, vjp = jax.vjp(functools.partial(_gates, h=h), x, al, db)
        gx, gal, gdb = vjp(dbg)
        return (gx,), (gal, gdb)

    (dba,), (d_alp, d_dtp) = _rowmap(gates_bwd, [(proj, LANES, 4 * h), dbg], [alp, dtp], [(LANES, MXU_DTYPE)],
                                     [(1, LANES), (1, LANES)], "dn_gates_bwd")
    dproj = jnp.concatenate([dpq, dpk, dpv, dz, dba], axis=1)
    d_w_in = _mm_tn(hin, dproj, MXU_DTYPE, "dn_dwin")
    token = send(d_w_in, d_w_out)
    dh = _mm_nt(dproj, w_in, F32, "dn_dh", pin=token)
    d_conv_w = jnp.concatenate([dwq, dwk, dwv], axis=1)
    return dh, dict(conv_w=d_conv_w, a_log=d_alp[0, h:2 * h], dt_bias=d_dtp[0, h:2 * h], norm_w=d_norm_w[0]), token


def _ln_silu(u, g, b):
    return jax.nn.silu(_ln(u, g, b))


def _conformer_fwd(hin, get_w_in, dw_w, dw_b, ln_g, ln_b, get_w_out):
    w_in = get_w_in(hin)
    vg = _mm_nn(hin, w_in, F32, "cf_vg")
    u1 = _cf_conv_fwd(vg, dw_w, dw_b)
    ch = u1.shape[1]

    def fn(u, g, b):
        return (_ln_silu(u, g, b),), ()

    (u2,), _ = _rowmap(fn, [u1], [ln_g, ln_b], [(ch, MXU_DTYPE)], [], "cf_ln")
    w_out = get_w_out(u2)
    y = _mm_nn(u2, w_out, F32, "cf_y")
    return y, (hin, vg, u1, u2, w_in, w_out)


def _conformer_bwd(res, dy, dw_w, ln_g, ln_b):
    hin, vg, u1, u2, w_in, w_out = res
    ch = u1.shape[1]
    d_w_out = _mm_tn(u2, dy, MXU_DTYPE, "cf_dwout")
    du2 = _mm_nt(dy, w_out, F32, "cf_du2")

    def fn(u, du2, g, b):
        _, vjp = jax.vjp(_ln_silu, u, g, b)
        gu, gg, gb = vjp(du2)
        return (gu,), (gg, gb)

    (du1,), (d_ln_g, d_ln_b) = _rowmap(fn, [u1, du2], [ln_g, ln_b], [(ch, F32)], [(1, ch), (1, ch)], "cf_ln_bwd")
    dval, dgate, d_dw_w, d_dw_b = _cf_conv_bwd(vg, dw_w, du1)
    dvg = jnp.concatenate([dval, dgate], axis=1)
    d_w_in = _mm_tn(hin, dvg, MXU_DTYPE, "cf_dwin", split_cols=True)
    dh = _mm_nt(dvg, w_in, F32, "cf_dh")
    return dh, dict(w_in=d_w_in, w_out=d_w_out, dw_w=d_dw_w, dw_b=d_dw_b[0], ln_g=d_ln_g[0], ln_b=d_ln_b[0])


def _mlp_fwd(hin, get_w1, get_w2):
    w1 = get_w1(hin)
    r = _mm_nn(hin, w1, MXU_DTYPE, "ff_a", relu2=True)
    w2 = get_w2(r)
    m = _mm_nn(r, w2, F32, "ff_m")
    return m, (hin, r, w1, w2)


def _mlp_bwd(res, dm):
    hin, r, w1, w2 = res
    d_w2 = _mm_tn(r, dm, MXU_DTYPE, "ff_dw2")
    da = _mm_nt(dm, w2, MXU_DTYPE, "ff_da", relu2_sq=r)
    d_w1 = _mm_tn(hin, da, MXU_DTYPE, "ff_dw1", split_cols=True)
    dh = _mm_nt(da, w1, F32, "ff_dh")
    return dh, d_w1, d_w2


def _ada_fwd(c_all, ada_w):
    depth, d, nl = ada_w.shape
    tn = _tile(nl, 256)

    def body(c_ref, w_ref, o_ref, cond_ref):
        cond = jax.nn.silu(c_ref[...]).astype(MXU_DTYPE)
        cond_ref[...] = cond
        o_ref[...] = lax.dot_general(cond, w_ref[...].astype(MXU_DTYPE), (NN, ((), ())), preferred_element_type=F32)

    return _call(body, grid=(depth, nl // tn),
                 ins=[(c_all, c_all.shape, lambda l, j: (0, 0)), (ada_w, (None, d, tn), lambda l, j: (l, 0, j))],
                 outs=[((depth, N_DEV, nl), F32, (None, N_DEV, tn), lambda l, j: (l, 0, j)),
                       (c_all.shape, MXU_DTYPE, c_all.shape, lambda l, j: (0, 0))],
                 name="ada_fwd")


def _ada_bwd(cond_all, dmod_cols):
    depth, _, nl = dmod_cols.shape
    d = cond_all.shape[1]
    tn = _tile(nl, 256)

    def body(c_ref, g_ref, o_ref):
        o_ref[...] = lax.dot_general(c_ref[...], g_ref[...].astype(MXU_DTYPE), (TN, ((), ())),
                                     preferred_element_type=F32)

    return _call(body, grid=(depth, nl // tn),
                 ins=[(cond_all, cond_all.shape, lambda l, j: (0, 0)), (dmod_cols, (None, N_DEV, tn), lambda l, j: (l, 0, j))],
                 outs=[((depth, d, nl), F32, (None, d, tn), lambda l, j: (l, 0, j))], name="ada_bwd")[0]


def _peers():
    x, y, c = lax.axis_index("x"), lax.axis_index("y"), lax.axis_index("c")
    peers = []
    for k in range(1, N_DEV):
        px = 1 - x if k & 4 else x
        py = 1 - y if k & 2 else y
        pc = 1 - c if k & 1 else c
        peers.append(((px, py, pc), 4 * px + 2 * py + pc))
    return 4 * x + 2 * y + c, peers


_HBM = pl.BlockSpec(memory_space=pltpu.HBM)
_SEM = pl.BlockSpec(memory_space=pltpu.SEMAPHORE)
_ANY = pl.BlockSpec(memory_space=pl.ANY)
_EFFECT = pltpu.SideEffectType.DATAFLOW_SIDE_EFFECTING


def _xfer_start(srcs, lands, scatter, after, name):
    nt = len(srcs)

    def body(*refs):
        src, land = refs[:nt], refs[nt:2 * nt]
        sems = refs[2 * nt + 1:4 * nt + 1]
        token = refs[-1]
        me, peers = _peers()
        for t in range(nt):
            for k, (pid, plin) in enumerate(peers):
                pltpu.make_async_remote_copy(
                    src_ref=src[t].at[plin] if scatter else src[t], dst_ref=land[t].at[me],
                    send_sem=sems[2 * t].at[k], recv_sem=sems[2 * t + 1].at[k],
                    device_id=pid, device_id_type=pl.DeviceIdType.MESH).start()
        token[...] = jnp.zeros_like(token)

    out_shape = [pltpu.SemaphoreType.DMA((N_DEV - 1,)) for _ in range(2 * nt)]
    out_shape += [pltpu.HBM(a.shape, a.dtype) for a in lands]
    out_shape += [jax.ShapeDtypeStruct((8, LANES), F32)]
    srcs = [pltpu.with_memory_space_constraint(a, pltpu.HBM) for a in srcs]
    res = pl.pallas_call(
        body, name=name, out_shape=out_shape,
        in_specs=[_HBM] * (2 * nt) + [_ANY],
        out_specs=[_SEM] * (2 * nt) + [_HBM] * nt + [pl.BlockSpec(memory_space=pltpu.VMEM)],
        input_output_aliases={nt + i: 2 * nt + i for i in range(nt)},
        compiler_params=pltpu.CompilerParams(has_side_effects=_EFFECT),
    )(*srcs, *[pltpu.with_memory_space_constraint(a, pltpu.HBM) for a in lands], after)
    sems, thru = res[:2 * nt], res[2 * nt:3 * nt]
    return [(sems[2 * t], sems[2 * t + 1], srcs[t], thru[t]) for t in range(nt)], res[-1]


def _xfer_wait(handle, scatter, after, name):
    send, recv, src, land = handle

    def body(src_ref, land_ref, send_sem, recv_sem, after_ref, land_out):
        _, peers = _peers()
        for k, (pid, plin) in enumerate(peers):
            cp = pltpu.make_async_remote_copy(
                src_ref=src_ref.at[plin] if scatter else src_ref, dst_ref=land_ref.at[plin],
                send_sem=send_sem.at[k], recv_sem=recv_sem.at[k],
                device_id=pid, device_id_type=pl.DeviceIdType.MESH)
            cp.wait_send()
            cp.wait_recv()

    return pl.pallas_call(
        body, name=name, out_shape=pltpu.HBM(land.shape, land.dtype),
        in_specs=(_HBM, _HBM, _SEM, _SEM, _ANY), out_specs=_HBM, input_output_aliases={1: 0},
        compiler_params=pltpu.CompilerParams(has_side_effects=_EFFECT),
    )(src, land, send, recv, after)


def _landing(x, me):
    return lax.dynamic_update_slice(lax.empty((N_DEV,) + x.shape, x.dtype), x[None], (me,) + (0,) * x.ndim)


def _chip_peers():
    x, y, c = lax.axis_index("x"), lax.axis_index("y"), lax.axis_index("c")
    lin = lambda px, py, pc: 4 * px + 2 * py + pc
    sibling = ((x, y, 1 - c), lin(x, y, 1 - c))
    chips = [((1 - x, y, c), lin(1 - x, y, c)), ((x, 1 - y, c), lin(x, 1 - y, c)),
             ((1 - x, 1 - y, c), lin(1 - x, 1 - y, c))]
    return lin(x, y, c), sibling, chips


N_CHIPS_OTHER = 3
N_SENDS = 1 + N_CHIPS_OTHER
FIRST_WEIGHT_PARTS = 8


def _rows_part(ref, part, parts):
    if parts == 1:
        return ref
    rows = ref.shape[0] // parts
    return ref.at[pl.ds(part * rows, rows)]


def _gather2_start(srcs, lands, after, name, parts=1):
    nt = len(srcs)

    def body(*refs):
        src, land = refs[:nt], refs[nt:2 * nt]
        sems = refs[2 * nt + 1:5 * nt + 1]
        token = refs[-1]
        me, sibling, chips = _chip_peers()
        for t in range(nt):
            send, recv_ici, recv_sib = sems[3 * t], sems[3 * t + 1], sems[3 * t + 2]
            for p in range(parts):
                piece, slot = _rows_part(src[t], p, parts), _rows_part(land[t].at[me], p, parts)
                pltpu.make_async_remote_copy(src_ref=piece, dst_ref=slot, send_sem=send.at[N_SENDS * p],
                                             recv_sem=recv_sib.at[p], device_id=sibling[0],
                                             device_id_type=pl.DeviceIdType.MESH).start()
                for j, (pid, _) in enumerate(chips):
                    pltpu.make_async_remote_copy(src_ref=piece, dst_ref=slot, send_sem=send.at[N_SENDS * p + 1 + j],
                                                 recv_sem=recv_ici.at[N_CHIPS_OTHER * p + j], device_id=pid,
                                                 device_id_type=pl.DeviceIdType.MESH).start()
        token[...] = jnp.zeros_like(token)

    out_shape = []
    for _ in range(nt):
        out_shape += [pltpu.SemaphoreType.DMA((N_SENDS * parts,)),
                      pltpu.SemaphoreType.DMA((N_CHIPS_OTHER * parts,)), pltpu.SemaphoreType.DMA((parts,))]
    out_shape += [pltpu.HBM(a.shape, a.dtype) for a in list(srcs) + list(lands)]
    out_shape += [jax.ShapeDtypeStruct((8, LANES), F32)]
    res = pl.pallas_call(
        body, name=name, out_shape=out_shape,
        in_specs=[_HBM] * (2 * nt) + [_ANY],
        out_specs=[_SEM] * (3 * nt) + [_HBM] * (2 * nt) + [pl.BlockSpec(memory_space=pltpu.VMEM)],
        input_output_aliases={i: 3 * nt + i for i in range(2 * nt)},
        compiler_params=pltpu.CompilerParams(has_side_effects=_EFFECT),
    )(*[pltpu.with_memory_space_constraint(a, pltpu.HBM) for a in list(srcs) + list(lands)], after)
    sems, thru = res[:3 * nt], res[3 * nt:5 * nt]
    return [(sems[3 * t], sems[3 * t + 1], sems[3 * t + 2], thru[t], thru[nt + t]) for t in range(nt)], res[-1]


def _gather2_relay(handles, after, name, parts=1):
    nt = len(handles)

    def body(*refs):
        src, land = refs[:nt], refs[nt:2 * nt]
        send1, recv_ici = refs[2 * nt:3 * nt], refs[3 * nt:4 * nt]
        outs = refs[4 * nt + 1:]
        send2, recv2 = outs[:nt], outs[nt:2 * nt]
        token = refs[-1]
        token[...] = jnp.zeros_like(token)
        me, sibling, chips = _chip_peers()
        for t in range(nt):
            for p in range(parts):
                piece = _rows_part(src[t], p, parts)
                pltpu.make_async_remote_copy(src_ref=piece, dst_ref=_rows_part(land[t].at[me], p, parts),
                                             send_sem=send1[t].at[N_SENDS * p], recv_sem=recv_ici[t].at[N_CHIPS_OTHER * p],
                                             device_id=sibling[0], device_id_type=pl.DeviceIdType.MESH).wait_send()
                for j, (pid, plin) in enumerate(chips):
                    slot = _rows_part(land[t].at[plin], p, parts)
                    arrived = pltpu.make_async_remote_copy(src_ref=piece, dst_ref=slot, send_sem=send1[t].at[N_SENDS * p + 1 + j],
                                                           recv_sem=recv_ici[t].at[N_CHIPS_OTHER * p + j], device_id=pid,
                                                           device_id_type=pl.DeviceIdType.MESH)
                    arrived.wait_send()
                    arrived.wait_recv()
                    pltpu.make_async_remote_copy(src_ref=slot, dst_ref=slot, send_sem=send2[t].at[N_CHIPS_OTHER * p + j],
                                                 recv_sem=recv2[t].at[N_CHIPS_OTHER * p + j], device_id=sibling[0],
                                                 device_id_type=pl.DeviceIdType.MESH).start()

    srcs = [h[3] for h in handles]
    lands = [h[4] for h in handles]
    out_shape = [pltpu.SemaphoreType.DMA((N_CHIPS_OTHER * parts,)) for _ in range(2 * nt)]
    out_shape += [pltpu.HBM(a.shape, a.dtype) for a in srcs + lands]
    out_shape += [jax.ShapeDtypeStruct((8, LANES), F32)]
    res = pl.pallas_call(
        body, name=name, out_shape=out_shape,
        in_specs=[_HBM] * (2 * nt) + [_SEM] * (2 * nt) + [_ANY],
        out_specs=[_SEM] * (2 * nt) + [_HBM] * (2 * nt) + [pl.BlockSpec(memory_space=pltpu.VMEM)],
        input_output_aliases={i: 2 * nt + i for i in range(2 * nt)},
        compiler_params=pltpu.CompilerParams(has_side_effects=_EFFECT),
    )(*srcs, *lands, *[h[0] for h in handles], *[h[1] for h in handles], after)
    return [(handles[t][2], res[t], res[nt + t], res[3 * nt + t]) for t in range(nt)], res[-1]


def _gather2_wait(handle, after, name, parts=1):
    recv_sib, send2, recv2, land = handle

    def body(land_ref, recv_sib_sem, send2_sem, recv2_sem, after_ref, land_out):
        me, sibling, chips = _chip_peers()
        for p in range(parts):
            pltpu.make_async_remote_copy(src_ref=_rows_part(land_ref.at[me], p, parts),
                                         dst_ref=_rows_part(land_ref.at[sibling[1]], p, parts), send_sem=send2_sem.at[0],
                                         recv_sem=recv_sib_sem.at[p], device_id=sibling[0],
                                         device_id_type=pl.DeviceIdType.MESH).wait_recv()
            for j, (pid, plin) in enumerate(chips):
                slot = _rows_part(land_ref.at[plin], p, parts)
                relayed = pltpu.make_async_remote_copy(src_ref=slot, dst_ref=slot, send_sem=send2_sem.at[N_CHIPS_OTHER * p + j],
                                                       recv_sem=recv2_sem.at[N_CHIPS_OTHER * p + j], device_id=sibling[0],
                                                       device_id_type=pl.DeviceIdType.MESH)
                relayed.wait_send()
                relayed.wait_recv()

    return pl.pallas_call(
        body, name=name, out_shape=pltpu.HBM(land.shape, land.dtype),
        in_specs=(_HBM, _SEM, _SEM, _SEM, _ANY), out_specs=_HBM, input_output_aliases={0: 0},
        compiler_params=pltpu.CompilerParams(has_side_effects=_EFFECT),
    )(land, recv_sib, send2, recv2, after)


def _exchange(arrs, scatter, name):
    nt = len(arrs)
    out_shape = [jax.ShapeDtypeStruct(a.shape if scatter else (N_DEV,) + a.shape, a.dtype) for a in arrs]

    def body(*refs):
        ins, outs = refs[:nt], refs[nt:2 * nt]
        send, recv, loc = refs[2 * nt:]
        me, peers = _peers()
        copies = []
        for t in range(nt):
            own = pltpu.make_async_copy(ins[t].at[me] if scatter else ins[t], outs[t].at[me], loc.at[t])
            own.start()
            copies.append(own)
            for k, (pid, plin) in enumerate(peers):
                cp = pltpu.make_async_remote_copy(
                    src_ref=ins[t].at[plin] if scatter else ins[t], dst_ref=outs[t].at[me],
                    send_sem=send.at[t, k], recv_sem=recv.at[t, k],
                    device_id=pid, device_id_type=pl.DeviceIdType.MESH)
                cp.start()
                copies.append(cp)
        for cp in copies:
            cp.wait()

    any_spec = pl.BlockSpec(memory_space=pl.ANY)
    return pl.pallas_call(
        body, out_shape=out_shape, in_specs=[any_spec] * nt, out_specs=[any_spec] * nt,
        scratch_shapes=[pltpu.SemaphoreType.DMA((nt, N_DEV - 1)), pltpu.SemaphoreType.DMA((nt, N_DEV - 1)),
                        pltpu.SemaphoreType.DMA((nt,))],
        name=name)(*arrs)


def _adamw_body(n_parts, stacked=True):
    def body(p_ref, w_ref, m_ref, v_ref, *rest):
        g_out, d_out, m_out, v_out = rest[-4:]
        part = (lambda i: p_ref[i]) if stacked else (lambda i: p_ref[i][...])
        g = part(0).astype(F32)
        for i in range(1, n_parts):
            g = g + part(i).astype(F32)
        m2 = ADAM_B1 * m_ref[...] + (1.0 - ADAM_B1) * g
        v2 = ADAM_B2 * v_ref[...] + (1.0 - ADAM_B2) * jnp.square(g)
        m_hat = m2 / (1.0 - ADAM_B1 ** ADAM_STEP)
        v_hat = v2 / (1.0 - ADAM_B2 ** ADAM_STEP)
        g_out[...] = g
        d_out[...] = -ADAM_LR * (m_hat / (jnp.sqrt(v_hat) + ADAM_EPS) + ADAM_WD * w_ref[...])
        m_out[...] = m2
        v_out[...] = v2

    return body


def _adamw_layer(own, land, me, w, m, v, layer, prev, name):
    _, r, c = own.shape
    tr = _tile(r, 256, 8)
    blk = pl.BlockSpec((None, tr, c), lambda i, me_ref: (layer, i, 0))
    share = lambda k: pl.BlockSpec((None, tr, c), lambda i, me_ref: (me_ref[0] ^ k, i, 0))
    in_specs = [share(k) for k in range(N_DEV)] + [blk, blk, blk]
    args = [own] + [land] * (N_DEV - 1) + [w, m, v]
    aliases = {}
    if prev is not None:
        in_specs += [_ANY] * 4
        args += list(prev)
        aliases = {1 + N_DEV + 3 + i: i for i in range(4)}

    def body(me_ref, *refs):
        token_ref = refs[-1]
        refs = (refs[:N_DEV],) + refs[N_DEV:-1]
        _adamw_body(N_DEV, stacked=False)(*refs)
        token_ref[...] = jnp.zeros(token_ref.shape, F32)

    token_blk = pl.BlockSpec((8, LANES), lambda i, me_ref: (0, 0))
    res = pl.pallas_call(
        body,
        grid_spec=pltpu.PrefetchScalarGridSpec(num_scalar_prefetch=1, grid=(r // tr,), in_specs=in_specs,
                                               out_specs=[blk] * 4 + [token_blk]),
        out_shape=[jax.ShapeDtypeStruct(w.shape, F32)] * 4 + [jax.ShapeDtypeStruct((8, LANES), F32)],
        input_output_aliases=aliases, name=name, compiler_params=_cparams(1))(me, *args)
    return res[:4], res[4]


def _adamw(parts, w, m, v, name):
    p, nl, r, c = parts.shape
    tr = _tile(r, 256, 8)
    body = _adamw_body(p)

    blk = (None, tr, c)
    imap = lambda l, i: (l, i, 0)
    out = ((nl, r, c), F32, blk, imap)
    return _call(body, grid=(nl, r // tr),
                 ins=[(parts, (p, None, tr, c), lambda l, i: (0, l, i, 0)), (w, blk, imap), (m, blk, imap), (v, blk, imap)],
                 outs=[out] * 4, name=name)


def _rows(x):
    return x.reshape(-1, LANES)


def _pad_rows(x, mult=8):
    r = x.shape[0]
    extra = (-r) % mult
    return jnp.pad(x, ((0, extra), (0, 0))) if extra else x


def kernel(x, c, ada_w, ada_b, ln_g, ln_b, dn_w_in, dn_conv_w, dn_a_log, dn_dt_bias, dn_norm_w, dn_w_out, cf_w_in, cf_dw_w, cf_dw_b, cf_ln_g, cf_ln_b, cf_w_out, ff_w1, ff_w2, loss_target, m_ada_w, m_ada_b, m_ln_g, m_ln_b, m_dn_w_in, m_dn_conv_w, m_dn_a_log, m_dn_dt_bias, m_dn_norm_w, m_dn_w_out, m_cf_w_in, m_cf_dw_w, m_cf_dw_b, m_cf_ln_g, m_cf_ln_b, m_cf_w_out, m_ff_w1, m_ff_w2, v_ada_w, v_ada_b, v_ln_g, v_ln_b, v_dn_w_in, v_dn_conv_w, v_dn_a_log, v_dn_dt_bias, v_dn_norm_w, v_dn_w_out, v_cf_w_in, v_cf_dw_w, v_cf_dw_b, v_cf_ln_g, v_cf_ln_b, v_cf_w_out, v_ff_w1, v_ff_w2):
    depth, d, _ = ada_w.shape
    n_a, n_b = dn_w_in.shape[0], cf_w_in.shape[0]
    heads = dn_a_log.shape[1]
    hw = heads * HEAD_DIM
    taps = cf_dw_w.shape[1]
    s = x.shape[1]
    alpha = (2.0 * depth) ** 0.25
    me = 4 * lax.axis_index("x") + 2 * lax.axis_index("y") + lax.axis_index("c")
    me_arr = jnp.reshape(me, (1,)).astype(jnp.int32)
    xs, tgt = x[0], loss_target[0]

    dn_in_cols = dn_w_in.shape[2]
    keys, shards = [], []
    for i in range(depth):
        j = i // 2
        mixer = [("dn_in", dn_w_in), ("dn_out", dn_w_out)] if i % 2 == 0 else [("cf_in", cf_w_in), ("cf_out", cf_w_out)]
        for nm, wt in mixer:
            keys.append((nm, j))
            shards.append(wt[j].astype(MXU_DTYPE))
        keys += [("ff1", i), ("ff2", i)]
        shards += [ff_w1[i].astype(MXU_DTYPE), ff_w2[i].astype(MXU_DTYPE)]

    small_local = [_rows(ln_g), _rows(ln_b), _rows(dn_conv_w), _rows(cf_dw_w), _rows(cf_dw_b), _rows(cf_ln_g),
                   _rows(cf_ln_b), _rows(c)]
    sizes = [a.shape[0] for a in small_local]
    packed = _pad_rows(jnp.concatenate(small_local, axis=0))
    (small_all,) = _exchange([packed], False, "comm_gather_params")
    offs = [0]
    for z in sizes:
        offs.append(offs[-1] + z)

    def small(i):
        return small_all[:, offs[i]:offs[i + 1], :]

    def unshard(piece, lead, groups):
        t = piece.reshape((N_DEV,) + lead + (groups * LANES,))
        t = jnp.moveaxis(t, 0, len(lead))
        return t.reshape(lead + (N_DEV * groups * LANES,))

    ln_g_f = unshard(small(0), (depth, 2), 1)
    ln_b_f = unshard(small(1), (depth, 2), 1)
    conv_w_f = unshard(small(2), (n_a, DN_CONV), 3 * heads // N_DEV)
    dw_w_f = unshard(small(3), (n_b, taps), 1)
    dw_b_f = unshard(small(4), (n_b,), 1)
    cf_ln_g_f = unshard(small(5), (n_b,), 1)
    cf_ln_b_f = unshard(small(6), (n_b,), 1)
    c_all = small(7).reshape(N_DEV, d)

    mod_part, cond_all = _ada_fwd(c_all, ada_w)
    (mod_all,) = _exchange([mod_part], False, "comm_gather_mod")
    mod_mine = lax.dynamic_index_in_dim(mod_all, me, axis=2, keepdims=False)
    mod_mine = jnp.moveaxis(mod_mine, 0, 1).reshape(depth, N_MOD * d)

    lands = [_landing(a, me) for a in shards]
    first, token = _gather2_start(shards[:1], lands[:1], mod_all, "gather_first_weight_start", FIRST_WEIGHT_PARTS)
    handles = {keys[0]: first[0]}
    groups = [keys[:1], keys[1:4]] + [keys[4 * i:4 * i + 4] for i in range(1, depth)]
    group_of = {k: n for n, grp in enumerate(groups) for k in grp}
    relayed, weights = {}, {}
    wait_after = {}

    def relay(n, after):
        if n < len(groups) and groups[n][0] not in relayed:
            hs, relay_token = _gather2_relay([handles[k] for k in groups[n]], after, "gather_relay_%d" % n,
                                             FIRST_WEIGHT_PARTS if n == 0 else 1)
            relayed.update(zip(groups[n], hs))
            return relay_token

    def gathered(key, after):
        if key not in weights:
            relay(group_of[key], after)
            if key[0] == "ff1":
                relay(key[1] + 2, after)
            weights[key] = _gather2_wait(relayed[key], wait_after.get(key, after), "gather_wait_%s_%d" % key,
                                         FIRST_WEIGHT_PARTS if key == keys[0] else 1)
        return weights[key]

    def get_dn_in(j):
        def get(after):
            g = gathered(("dn_in", j), after)
            w = jnp.moveaxis(g, 0, 1).reshape(d, N_DEV * dn_in_cols)
            return jnp.pad(w, ((0, 0), (0, 4 * hw + LANES - N_DEV * dn_in_cols)))
        return get

    def get_rows(key):
        return lambda after: gathered(key, after).reshape((-1, d))

    def get_cols(key):
        return lambda after: gathered(key, after)

    def add_bias(a, b):
        return (a + b,), ()

    (mod,), _ = _rowmap(add_bias, [mod_mine, ada_b], [], [(N_MOD * d, F32)], [], "ada_bias", pin=token)
    mod_rows = mod.reshape(depth * N_MOD, 1, d)
    ln_g_rows = ln_g_f.reshape(depth * 2, 1, d)
    ln_b_rows = ln_b_f.reshape(depth * 2, 1, d)

    def mod_row(i, j):
        return (mod_rows, i * N_MOD + j)

    def ln_row(rows, i, j):
        return (rows, i * 2 + j)

    subs = []
    h_cur = _modulate_fwd(xs, mod_row(0, 1), mod_row(0, 0))
    rest, wait_after[keys[0]] = _gather2_start(shards[1:], lands[1:], relay(0, h_cur), "gather_weights_start")
    handles.update(zip(keys[1:], rest))
    x_cur = xs
    last = None
    for i in range(depth):
        j = i // 2
        if i % 2 == 0:
            y, res = _deltanet_fwd(h_cur, get_dn_in(j), conv_w_f[j], dn_a_log[j], dn_dt_bias[j], dn_norm_w[j],
                                   get_rows(("dn_out", j)))
        else:
            y, res = _conformer_fwd(h_cur, get_cols(("cf_in", j)), dw_w_f[j], dw_b_f[j][None, :], cf_ln_g_f[j][None, :],
                                    cf_ln_b_f[j][None, :], get_rows(("cf_out", j)))
        p1 = (mod_row(i, 2), ln_row(ln_g_rows, i, 0), ln_row(ln_b_rows, i, 0), mod_row(i, 4), mod_row(i, 3))
        x_mid, h_mid = _combine_fwd(alpha, x_cur, y, *p1)
        subs.append((x_cur, y, p1, res))
        m_out, res2 = _mlp_fwd(h_mid, get_cols(("ff1", i)), get_rows(("ff2", i)))
        if i + 1 < depth:
            p2 = (mod_row(i, 5), ln_row(ln_g_rows, i, 1), ln_row(ln_b_rows, i, 1), mod_row(i + 1, 1), mod_row(i + 1, 0))
            x_next, h_next = _combine_fwd(alpha, x_mid, m_out, *p2)
            subs.append((x_mid, m_out, p2, res2))
            x_cur, h_cur = x_next, h_next
        else:
            p2 = (mod_row(i, 5), ln_row(ln_g_rows, i, 1), ln_row(ln_b_rows, i, 1))
            last = (x_mid, m_out, p2, res2)

    x_in, y_in, p_last, res_last = last
    dx, dy, (loss_acc, g_gt, g_g, g_b) = _last_fwd_bwd(alpha, x_in, y_in, tgt, *p_last)
    loss = lax.psum(loss_acc[0, 0], ("x", "y", "c"))

    d_mod = [[None] * N_MOD for _ in range(depth)]
    d_ln_g = [[None, None] for _ in range(depth)]
    d_ln_b = [[None, None] for _ in range(depth)]
    d_mod[depth - 1][5], d_ln_g[depth - 1][1], d_ln_b[depth - 1][1] = g_gt, g_g, g_b
    gw = dict(dn=[None] * n_a, cf=[None] * n_b)

    sent = {}

    def send_grads(named, tag):
        parts = [p for _, p in named]
        hs, tok = _xfer_start(parts, [lax.empty(p.shape, p.dtype) for p in parts], True, parts[0], "scatter_start_" + tag)
        for (key, _), hnd in zip(named, hs):
            sent[key] = hnd
        return tok

    def by_rows(g):
        return g.reshape((N_DEV, g.shape[0] // N_DEV, g.shape[1]))

    def send_mlp(i, d_w1, d_w2):
        return send_grads([(("ff1", i), d_w1), (("ff2", i), by_rows(d_w2))], "ff_%d" % i)

    dh, d_w1, d_w2 = _mlp_bwd(res_last, dy)
    pin = send_mlp(depth - 1, d_w1, d_w2)
    for idx in range(len(subs) - 1, -1, -1):
        x_in, y_in, prm, res = subs[idx]
        i, second = idx // 2, idx % 2
        dx, dy, (g_gt, g_g, g_b, g_sc, g_sh) = _combine_bwd(alpha, x_in, y_in, dx, dh, *prm, pin=pin)
        d_mod[i][5 if second else 2], d_ln_g[i][second], d_ln_b[i][second] = g_gt, g_g, g_b
        nxt_i, nxt_base = (i + 1, 0) if second else (i, 3)
        d_mod[nxt_i][nxt_base + 1], d_mod[nxt_i][nxt_base] = g_sc, g_sh
        j = i // 2
        if second:
            dh, d_w1, d_w2 = _mlp_bwd(res, dy)
            pin = send_mlp(i, d_w1, d_w2)
        elif i % 2 == 0:
            def send_dn(d_w_in, d_w_out, j=j):
                d_in = d_w_in[:, :N_DEV * dn_in_cols].reshape(d, N_DEV, dn_in_cols)
                return send_grads([(("dn_in", j), jnp.moveaxis(d_in, 1, 0)), (("dn_out", j), by_rows(d_w_out))],
                                  "dn_%d" % j)

            dh, gw["dn"][j], pin = _deltanet_bwd(res, dy, conv_w_f[j], send_dn)
        else:
            dh, gw["cf"][j] = _conformer_bwd(res, dy, dw_w_f[j], cf_ln_g_f[j][None, :], cf_ln_b_f[j][None, :])
            pin = send_grads([(("cf_in", j), gw["cf"][j]["w_in"]), (("cf_out", j), by_rows(gw["cf"][j]["w_out"]))],
                             "cf_%d" % j)
    grad_x, g_sc, g_sh = _modulate_bwd(xs, dx, dh, mod_row(0, 1), mod_row(0, 0), pin=pin)
    d_mod[0][1], d_mod[0][0] = g_sc, g_sh
    d_mod_full = jnp.concatenate([jnp.concatenate(r, axis=1) for r in d_mod], axis=0)

    stacked = {"dn_w_in": ("dn_in", dn_w_in, m_dn_w_in, v_dn_w_in), "dn_w_out": ("dn_out", dn_w_out, m_dn_w_out, v_dn_w_out),
               "cf_w_in": ("cf_in", cf_w_in, m_cf_w_in, v_cf_w_in), "cf_w_out": ("cf_out", cf_w_out, m_cf_w_out, v_cf_w_out),
               "ff_w1": ("ff1", ff_w1, m_ff_w1, v_ff_w1), "ff_w2": ("ff2", ff_w2, m_ff_w2, v_ff_w2)}
    chains = {key: None for key in stacked}

    def update_layer(i, token):
        mixer = ["dn_w_in", "dn_w_out"] if i % 2 == 0 else ["cf_w_in", "cf_w_out"]
        for key, idx in [("ff_w1", i), ("ff_w2", i)] + [(k, i // 2) for k in mixer]:
            short, w, m, v = stacked[key]
            land = _xfer_wait(sent[(short, idx)], True, token, "scatter_wait_%s_%d" % (short, idx))
            chains[key], token = _adamw_layer(sent[(short, idx)][2], land, me_arr, w, m, v, idx, chains[key],
                                              "adamw_%s_%d" % (key, idx))
        return token

    def stack_rows(lst):
        return jnp.stack(lst, axis=0)

    gs_ln_g = jnp.stack([jnp.concatenate(r, axis=0) for r in d_ln_g], axis=0)
    gs_ln_b = jnp.stack([jnp.concatenate(r, axis=0) for r in d_ln_b], axis=0)
    gs_conv_w = stack_rows([gw["dn"][j]["conv_w"] for j in range(n_a)])
    gs_dw_w = stack_rows([gw["cf"][j]["dw_w"] for j in range(n_b)])
    gs_dw_b = stack_rows([gw["cf"][j]["dw_b"] for j in range(n_b)])
    gs_cf_ln_g = stack_rows([gw["cf"][j]["ln_g"] for j in range(n_b)])
    gs_cf_ln_b = stack_rows([gw["cf"][j]["ln_b"] for j in range(n_b)])
    gs_a_log = stack_rows([_pad_lanes(gw["dn"][j]["a_log"], 0)[0] for j in range(n_a)])
    gs_dt_bias = stack_rows([_pad_lanes(gw["dn"][j]["dt_bias"], 0)[0] for j in range(n_a)])
    gs_norm_w = stack_rows([gw["dn"][j]["norm_w"] for j in range(n_a)])
    small_grads = [gs_ln_g, gs_ln_b, gs_conv_w, gs_dw_w, gs_dw_b, gs_cf_ln_g, gs_cf_ln_b, gs_a_log, gs_dt_bias,
                   gs_norm_w, d_mod_full]
    sg_rows = [_rows(a) for a in small_grads]
    sg_sizes = [a.shape[0] for a in sg_rows]
    sg_packed = _pad_rows(jnp.concatenate(sg_rows, axis=0))
    (sg_handle,), sg_token = _xfer_start([sg_packed], [_landing(sg_packed, me)], False, grad_x, "gather_small_grads_start")
    for i in range(depth - 1, -1, -1):
        sg_token = update_layer(i, sg_token)
    sg_all = _xfer_wait(sg_handle, False, sg_token, "gather_small_grads_wait")
    sg_offs = [0]
    for z in sg_sizes:
        sg_offs.append(sg_offs[-1] + z)

    def sg(i, shape):
        return sg_all[:, sg_offs[i]:sg_offs[i + 1], :].reshape((N_DEV,) + shape)

    dmod_all = sg(10, (depth, N_MOD * d))
    nl = ada_w.shape[2]
    dmod_cols = lax.dynamic_slice_in_dim(dmod_all, me * nl, nl, axis=2)
    g_ada_w = _ada_bwd(cond_all, jnp.moveaxis(dmod_cols, 0, 1))

    outs = {}

    def run_adamw(key, parts, w, m, v):
        shp = w.shape
        as3 = lambda t: t.reshape((-1,) + shp[-2:]) if t.ndim >= 3 else t.reshape((1,) + shp)
        parts3 = parts.reshape((parts.shape[0],) + as3(w).shape)
        res = _adamw(parts3, as3(w), as3(m), as3(v), "adamw_" + key)
        outs[key] = tuple(r.reshape(shp) for r in res)

    run_adamw("ada_w", g_ada_w[None], ada_w, m_ada_w, v_ada_w)

    cgroups = 3 * heads // N_DEV
    n_sharded = 7

    def my_cols(first, last, groups):
        x = sg_all[:, sg_offs[first]:sg_offs[last], :].reshape(N_DEV, -1, N_DEV, groups, LANES)
        return lax.dynamic_index_in_dim(x, me, axis=2, keepdims=False).reshape(N_DEV, -1, LANES)

    small_parts = [my_cols(0, 2, 1), my_cols(2, 3, cgroups), my_cols(3, n_sharded, 1),
                   sg_all[:, sg_offs[n_sharded]:sg_offs[-1], :]]
    sp_offs = [0]
    for n, z in enumerate(sg_sizes):
        sp_offs.append(sp_offs[-1] + (z // N_DEV if n < n_sharded else z))
    parts_packed = jnp.zeros((N_DEV, sp_offs[-1] + (-sp_offs[-1]) % 8, LANES), F32)
    at = 0
    for part in small_parts:
        parts_packed = lax.dynamic_update_slice(parts_packed, part, (0, at, 0))
        at += part.shape[1]

    def pad_heads(t):
        return jnp.pad(t, ((0, 0), (0, LANES - heads)))

    def pack_state(ln_g_, ln_b_, conv_w_, dw_w_, dw_b_, cln_g_, cln_b_, a_log_, dt_b_, norm_w_, ada_b_):
        rows = [_rows(ln_g_), _rows(ln_b_), _rows(conv_w_), _rows(dw_w_), _rows(dw_b_), _rows(cln_g_), _rows(cln_b_),
                pad_heads(a_log_), pad_heads(dt_b_), norm_w_, _rows(ada_b_)]
        return _pad_rows(jnp.concatenate(rows, axis=0))

    w_s = pack_state(ln_g, ln_b, dn_conv_w, cf_dw_w, cf_dw_b, cf_ln_g, cf_ln_b, dn_a_log, dn_dt_bias, dn_norm_w, ada_b)
    m_s = pack_state(m_ln_g, m_ln_b, m_dn_conv_w, m_cf_dw_w, m_cf_dw_b, m_cf_ln_g, m_cf_ln_b, m_dn_a_log,
                     m_dn_dt_bias, m_dn_norm_w, m_ada_b)
    v_s = pack_state(v_ln_g, v_ln_b, v_dn_conv_w, v_cf_dw_w, v_cf_dw_b, v_cf_ln_g, v_cf_ln_b, v_dn_a_log,
                     v_dn_dt_bias, v_dn_norm_w, v_ada_b)
    res_s = _adamw(parts_packed[:, None], w_s[None], m_s[None], v_s[None], "adamw_small")
    small_keys = ["ln_g", "ln_b", "dn_conv_w", "cf_dw_w", "cf_dw_b", "cf_ln_g", "cf_ln_b", "dn_a_log", "dn_dt_bias",
                  "dn_norm_w", "ada_b"]
    small_shapes = [ln_g.shape, ln_b.shape, dn_conv_w.shape, cf_dw_w.shape, cf_dw_b.shape, cf_ln_g.shape,
                    cf_ln_b.shape, dn_a_log.shape, dn_dt_bias.shape, dn_norm_w.shape, ada_b.shape]
    for n, (key, shp) in enumerate(zip(small_keys, small_shapes)):
        vals = []
        for r in res_s:
            piece = r[0, sp_offs[n]:sp_offs[n + 1], :]
            if key in ("dn_a_log", "dn_dt_bias"):
                piece = piece[:, :heads]
            vals.append(piece.reshape(shp))
        outs[key] = tuple(vals)

    for key in stacked:
        outs[key] = tuple(chains[key])

    order = ["ada_w", "ada_b", "ln_g", "ln_b", "dn_w_in", "dn_conv_w", "dn_a_log", "dn_dt_bias", "dn_norm_w",
             "dn_w_out", "cf_w_in", "cf_dw_w", "cf_dw_b", "cf_ln_g", "cf_ln_b", "cf_w_out", "ff_w1", "ff_w2"]
    result = [loss, grad_x[None]]
    for part in range(4):
        result += [outs[k][part] for k in order]
    return tuple(result)
```

```python
import functools

import jax
import jax.numpy as jnp
from jax import lax
from jax.experimental import pallas as pl
from jax.experimental.pallas import tpu as pltpu

F32 = jnp.float32
MXU_DTYPE = jnp.bfloat16
N_DEV = 8
LANES = 128
HEAD_DIM = 128
CHUNK = 64
DN_CONV = 4
N_MOD = 6
LN_EPS = 1e-5
RMS_EPS = 1e-6
L2_EPS = 1e-6
ADAM_LR = 0.001
ADAM_B1 = 0.9
ADAM_B2 = 0.999
ADAM_EPS = 1e-08
ADAM_WD = 0.01
ADAM_STEP = 10

NN = ((1,), (0,))
NT = ((1,), (1,))
TN = ((0,), (0,))

ROW_TILE = 512
CONV_TILE = 256
SHORT_CONV_TILE = 1024


def _mdot(a, b, dims=NN):
    return lax.dot_general(a.astype(MXU_DTYPE), b.astype(MXU_DTYPE), (dims, ((), ())), preferred_element_type=F32)


def _split3(x):
    hi = x.astype(MXU_DTYPE)
    r1 = x - hi.astype(F32)
    mid = r1.astype(MXU_DTYPE)
    lo = (r1 - mid.astype(F32)).astype(MXU_DTYPE)
    return hi, mid, lo


def _dot01(a, b, dims=NN, mask_first=True):
    d = lambda p, q: lax.dot_general(p, q, (dims, ((), ())), preferred_element_type=F32)
    if mask_first:
        m = a.astype(MXU_DTYPE)
        return sum(d(m, p) for p in _split3(b))
    m = b.astype(MXU_DTYPE)
    return sum(d(p, m) for p in _split3(a))


def _cparams(n):
    return pltpu.CompilerParams(dimension_semantics=("arbitrary",) * n)


def _call(body, *, grid, ins, outs, name, scratch=()):
    res = pl.pallas_call(
        body,
        grid=grid,
        in_specs=[pl.BlockSpec(memory_space=pl.ANY) if b is None else pl.BlockSpec(b, m) for _, b, m in ins],
        out_specs=[pl.BlockSpec(b, m) for _, _, b, m in outs],
        out_shape=[jax.ShapeDtypeStruct(s, d) for s, d, _, _ in outs],
        scratch_shapes=list(scratch),
        name=name,
        compiler_params=_cparams(len(grid)),
    )(*[a for a, _, _ in ins])
    return res


def _tile(n, pref, unit=LANES):
    if n <= pref:
        return n
    t = (pref // unit) * unit
    while t > unit and n % t:
        t -= unit
    assert n % t == 0, (n, pref)
    return t


def _rowmap(fn, rows, consts, row_outs, acc_outs, name, pin=None):
    rows = [r if isinstance(r, tuple) else (r, r.shape[1], 0) for r in rows]
    s = rows[0][0].shape[0]
    tm = min(ROW_TILE, s)
    nr, nc, no, na = len(rows), len(consts), len(row_outs), len(acc_outs)
    npin = 0 if pin is None else 1

    def body(*refs):
        rin, cin = refs[:nr], refs[nr:nr + nc]
        refs = refs[:nr + nc] + refs[nr + nc + npin:]
        rout, aout = refs[nr + nc:nr + nc + no], refs[nr + nc + no:]
        ro, ao = fn(*[r[...] for r in rin], *[c[...] for c in cin])
        for ref, val in zip(rout, ro):
            ref[...] = val.astype(ref.dtype)
        if na:
            first = pl.program_id(0) == 0

            @pl.when(first)
            def _():
                for ref, val in zip(aout, ao):
                    ref[...] = val

            @pl.when(jnp.logical_not(first))
            def _():
                for ref, val in zip(aout, ao):
                    ref[...] += val

    ins = [(a, (tm, w), functools.partial(lambda i, cb: (i, cb), cb=cb)) for a, w, cb in rows]
    for c in consts:
        if isinstance(c, tuple):
            ins.append((c[0], (None, 1, c[0].shape[2]), functools.partial(lambda i, n: (n, 0, 0), n=c[1])))
        else:
            ins.append((c, c.shape, lambda i: (0, 0)))
    if pin is not None:
        ins.append((pin, None, None))
    outs = [((s, w), d, (tm, w), lambda i: (i, 0)) for w, d in row_outs]
    outs += [(shp, F32, shp, lambda i: (0, 0)) for shp in acc_outs]
    res = _call(body, grid=(s // tm,), ins=ins, outs=outs, name=name)
    return res[:no], res[no:]


def _ln(z, g, b):
    mu = jnp.mean(z, -1, keepdims=True)
    var = jnp.mean(jnp.square(z - mu), -1, keepdims=True)
    return (z - mu) * lax.rsqrt(var + LN_EPS) * g + b


def _combine(alpha, x, y, gt, g, b, sc, sh):
    xn = _ln(alpha * x + (1.0 + gt) * y, g, b)
    return xn, xn * (1.0 + sc) + sh


def _modulate_fwd(x, sc, sh):
    def fn(x, sc, sh):
        return ((x * (1.0 + sc) + sh),), ()

    (h,), _ = _rowmap(fn, [x], [sc, sh], [(x.shape[1], MXU_DTYPE)], [], "modulate_fwd")
    return h


def _modulate_bwd(x, dx, dh, sc, sh, pin=None):
    d = x.shape[1]

    def fn(x, dx, dh, sc, sh):
        _, vjp = jax.vjp(lambda x, sc, sh: x * (1.0 + sc) + sh, x, sc, sh)
        gx, gsc, gsh = vjp(dh)
        return (dx + gx,), (gsc, gsh)

    (gx,), (gsc, gsh) = _rowmap(fn, [x, dx, dh], [sc, sh], [(d, F32)], [(1, d), (1, d)], "modulate_bwd", pin=pin)
    return gx, gsc, gsh


def _combine_fwd(alpha, x, y, gt, g, b, sc, sh):
    d = x.shape[1]

    def fn(x, y, gt, g, b, sc, sh):
        return _combine(alpha, x, y, gt, g, b, sc, sh), ()

    (xn, h), _ = _rowmap(fn, [x, y], [gt, g, b, sc, sh], [(d, F32), (d, MXU_DTYPE)], [], "combine_fwd")
    return xn, h


def _combine_bwd(alpha, x, y, dxn, dh, gt, g, b, sc, sh, pin=None):
    d = x.shape[1]

    def fn(x, y, dxn, dh, gt, g, b, sc, sh):
        _, vjp = jax.vjp(functools.partial(_combine, alpha), x, y, gt, g, b, sc, sh)
        gx, gy, ggt, gg, gb, gsc, gsh = vjp((dxn, dh))
        return (gx, gy), (ggt, gg, gb, gsc, gsh)

    (gx, gy), accs = _rowmap(fn, [x, y, dxn, dh], [gt, g, b, sc, sh], [(d, F32), (d, MXU_DTYPE)],
                             [(1, d)] * 5, "combine_bwd", pin=pin)
    return gx, gy, accs


def _last_fwd_bwd(alpha, x, y, tgt, gt, g, b):
    d = x.shape[1]

    def fn(x, y, tgt, gt, g, b):
        xn, vjp = jax.vjp(lambda x, y, gt, g, b: _ln(alpha * x + (1.0 + gt) * y, g, b), x, y, gt, g, b)
        err = xn - tgt
        gx, gy, ggt, gg, gb = vjp(err * (1.0 / d))
        rows = jnp.sum(jnp.square(err), axis=-1, keepdims=True)
        loss = (0.5 / d) * jnp.sum(rows, axis=0, keepdims=True) * jnp.ones((1, LANES), F32)
        return (gx, gy), (loss, ggt, gg, gb)

    (gx, gy), accs = _rowmap(fn, [x, y, tgt], [gt, g, b], [(d, F32), (d, MXU_DTYPE)],
                             [(1, LANES), (1, d), (1, d), (1, d)], "last_fwd_bwd")
    return gx, gy, accs


MM_VMEM_BUDGET = 40 * 2 ** 20


def _fit(options, cost):
    for o in options:
        if 2 * cost(o) <= MM_VMEM_BUDGET:
            return o
    return options[-1]


def _row_tiles(m):
    return [t for t in (2048, 1024, 512, 256) if t <= m and m % t == 0] or [m]


def _mm_call(a, a_blk, a_map, b, b_blk, b_map, outs, dims, grid, name, epi=None, extra=None, split=None, blocks=None,
             pin=None):
    nk = grid[2]
    n_out = len(outs)
    n_in = 2 + (extra is not None) + (pin is not None)

    def body(*refs):
        a_ref, b_ref = refs[0], refs[1]
        rest = refs[n_in:]
        out_refs = rest[:n_out]

        def finish(val):
            if epi == "relu2":
                out_refs[0][...] = jnp.square(jnp.maximum(val, 0.0)).astype(out_refs[0].dtype)
            elif epi == "relu2_bwd":
                sq = refs[2][...].astype(F32)
                root = jnp.where(sq > 0.0, sq * lax.rsqrt(sq), 0.0)
                out_refs[0][...] = (val * 2.0 * root).astype(out_refs[0].dtype)
            elif split is not None:
                for g in range(split[0]):
                    out_refs[0][g] = val[:, g * split[1]:(g + 1) * split[1]].astype(out_refs[0].dtype)
            else:
                out_refs[0][...] = val.astype(out_refs[0].dtype)

        if blocks is None:
            p = lax.dot_general(a_ref[...], b_ref[...], (dims, ((), ())), preferred_element_type=F32)
        else:
            p = None
            for g in range(blocks[0]):
                part = lax.dot_general(a_ref[:, g * blocks[1]:(g + 1) * blocks[1]], b_ref[g], (dims, ((), ())),
                                       preferred_element_type=F32)
                p = part if p is None else p + part
        if nk == 1:
            finish(p)
        else:
            acc = rest[n_out]
            k = pl.program_id(2)

            @pl.when(k == 0)
            def _():
                acc[...] = p

            @pl.when(k > 0)
            def _():
                acc[...] += p

            @pl.when(k == nk - 1)
            def _():
                finish(acc[...])

    if nk > 1:
        out_blk = tuple(x for x in outs[0][2] if x is not None)
        if split is not None:
            out_blk = (out_blk[1], split[0] * split[1])
        scratch = [pltpu.VMEM(out_blk, F32)]
    else:
        scratch = []
    ins = [(a, a_blk, a_map), (b, b_blk, b_map)] + ([extra] if extra is not None else [])
    ins += [(pin, None, None)] if pin is not None else []
    return _call(body, grid=grid, ins=ins, outs=outs, name=name, scratch=scratch)


def _isz(dt):
    return jnp.dtype(dt).itemsize


def _mm_nn(a, b, out_dtype, name, relu2=False):
    m, kdim = a.shape
    if b.ndim == 2:
        n = b.shape[1]
        tn = _tile(n, 1536 if n > 2048 else 512)
        b_blk, b_map = (kdim, tn), lambda i, j, k: (0, j)
    else:
        g, _, ng = b.shape
        n = g * ng
        tn = _tile(ng, 512)
        b_blk = (None, kdim, tn)
        b_map = functools.partial(lambda i, j, k, npg: (j // npg, 0, j % npg), npg=ng // tn)
    tm = _fit(_row_tiles(m), lambda t: t * kdim * _isz(a.dtype) + kdim * tn * _isz(b.dtype) + t * tn * _isz(out_dtype))
    grid = (m // tm, n // tn, 1)
    outs = [((m, n), out_dtype, (tm, tn), lambda i, j, k: (i, j))]
    return _mm_call(a, (tm, kdim), lambda i, j, k: (i, 0), b, b_blk, b_map, outs, NN, grid, name,
                    epi="relu2" if relu2 else None)[0]


def _mm_nt(a, b, out_dtype, name, relu2_sq=None, pin=None):
    m, n = a.shape
    extra_bytes = _isz(relu2_sq.dtype) if relu2_sq is not None else 0
    if b.ndim == 2:
        kout = b.shape[0]
        to = _tile(kout, 512)
        b_blk, b_map, blocks = (to, n), lambda i, j, k: (j, 0), None
    else:
        g, kout, ng = b.shape
        to = _tile(kout, 512)
        b_blk, b_map, blocks = (g, to, ng), lambda i, j, k: (0, j, 0), (g, ng)
    tm = _fit(_row_tiles(m), lambda t: t * n * _isz(a.dtype) + to * n * _isz(b.dtype)
              + t * to * (_isz(out_dtype) + extra_bytes))
    grid = (m // tm, kout // to, 1)
    outs = [((m, kout), out_dtype, (tm, to), lambda i, j, k: (i, j))]
    extra = (relu2_sq, (tm, to), lambda i, j, k: (i, j)) if relu2_sq is not None else None
    return _mm_call(a, (tm, n), lambda i, j, k: (i, 0), b, b_blk, b_map, outs, NT, grid, name,
                    epi="relu2_bwd" if relu2_sq is not None else None, extra=extra, blocks=blocks, pin=pin)[0]


def _mm_tn(a, b, out_dtype, name, split_cols=False):
    m, kdim = a.shape
    n = b.shape[1]
    tk = _tile(kdim, 512)
    tn = _tile(n, 1536)
    if not split_cols:
        out, split = ((kdim, n), out_dtype, (tk, tn), lambda i, j, k: (i, j)), None
    else:
        ng = n // N_DEV
        if tn % ng:
            tn = _tile(ng, 512)
        if tn >= ng:
            gb = tn // ng
            out = ((N_DEV, kdim, ng), out_dtype, (gb, tk, ng), lambda i, j, k: (j, i, 0))
            split = (gb, ng)
        else:
            out = ((N_DEV, kdim, ng), out_dtype, (None, tk, tn),
                   functools.partial(lambda i, j, k, npg: (j // npg, i, j % npg), npg=ng // tn))
            split = None
    grid = (kdim // tk, n // tn, 1)
    return _mm_call(a, (m, tk), lambda i, j, k: (0, i), b, (m, tn), lambda i, j, k: (0, j), [out], TN, grid, name,
                    split=split)[0]


def _shifted(xa, off, rows):
    if off % 8 == 0:
        return xa[off:off + rows]
    return pltpu.roll(xa, xa.shape[0] - off, 0)[:rows]


def _conv_pad(taps):
    return -(-(taps - 1) // 8) * 8


def _conv_tile(xp_ref, w, i, rows, taps):
    pad = _conv_pad(taps)
    r0 = pl.multiple_of(i * rows, rows)
    xa = xp_ref[pl.ds(r0, rows + pad), :]
    views = [_shifted(xa, pad - (taps - 1) + j, rows) for j in range(taps)]
    acc = w[0:1, :] * views[0]
    for j in range(1, taps):
        acc = acc + w[j:j + 1, :] * views[j]
    return r0, acc, views


def _conv_back_tile(yp_ref, w, i, rows, taps):
    pad = _conv_pad(taps)
    r0 = pl.multiple_of(i * rows, rows)
    ya = yp_ref[pl.ds(r0, rows + pad), :]
    acc = w[taps - 1:taps, :] * ya[:rows]
    for j in range(taps - 1):
        acc = acc + w[j:j + 1, :] * _shifted(ya, taps - 1 - j, rows)
    return r0, acc


def _tap_sums(dy, views, taps):
    row = lax.broadcasted_iota(jnp.int32, (taps, LANES), 0)
    acc = jnp.zeros((taps, LANES), F32)
    for j in range(taps):
        acc = acc + jnp.where(row == j, jnp.sum(dy * views[j], axis=0, keepdims=True), 0.0)
    return acc


def _silu_l2(xc, l2):
    a = jax.nn.silu(xc)
    if l2:
        a = a * lax.rsqrt(jnp.sum(a * a, axis=-1, keepdims=True) + L2_EPS)
    return a


def _dn_conv_fwd(proj, conv_w, c0, nblk, l2, name):
    s = proj.shape[0]
    pad = _conv_pad(DN_CONV)
    rows = min(SHORT_CONV_TILE, s)

    def body(x_ref, w_ref, o_ref, xp):
        xp[0:pad, :] = jnp.zeros((pad, LANES), F32)
        xp[pad:, :] = x_ref[...]
        w = w_ref[...]

        def tile(i, c):
            r0, acc, _ = _conv_tile(xp, w, i, rows, DN_CONV)
            o_ref[pl.ds(r0, rows), :] = _silu_l2(acc, l2)
            return c

        lax.fori_loop(0, s // rows, tile, 0)

    return _call(body, grid=(nblk,),
                 ins=[(proj, (s, LANES), lambda c: (0, c0 + c)), (conv_w, (DN_CONV, LANES), lambda c: (0, c0 + c))],
                 outs=[((nblk, s, LANES), F32, (None, s, LANES), lambda c: (c, 0, 0))],
                 name=name, scratch=[pltpu.VMEM((s + pad, LANES), F32)])[0]


def _dn_conv_bwd(proj, conv_w, da, c0, nblk, l2, name):
    s = proj.shape[0]
    pad = _conv_pad(DN_CONV)
    rows = min(SHORT_CONV_TILE, s)

    def body(x_ref, w_ref, da_ref, dx_ref, dw_ref, xp, yp):
        xp[0:pad, :] = jnp.zeros((pad, LANES), F32)
        xp[pad:, :] = x_ref[...]
        yp[s:, :] = jnp.zeros((pad, LANES), F32)
        w = w_ref[...]

        def tile(i, dw):
            r0, acc, views = _conv_tile(xp, w, i, rows, DN_CONV)
            _, vjp = jax.vjp(functools.partial(_silu_l2, l2=l2), acc)
            (dxc,) = vjp(da_ref[pl.ds(r0, rows), :])
            yp[pl.ds(r0, rows), :] = dxc
            return dw + _tap_sums(dxc, views, DN_CONV)

        dw_ref[...] = lax.fori_loop(0, s // rows, tile, jnp.zeros((DN_CONV, LANES), F32))

        def tile2(i, c):
            r0, acc = _conv_back_tile(yp, w, i, rows, DN_CONV)
            dx_ref[pl.ds(r0, rows), :] = acc.astype(dx_ref.dtype)
            return c

        lax.fori_loop(0, s // rows, tile2, 0)

    return _call(body, grid=(nblk,),
                 ins=[(proj, (s, LANES), lambda c: (0, c0 + c)), (conv_w, (DN_CONV, LANES), lambda c: (0, c0 + c)),
                      (da, (None, s, LANES), lambda c: (c, 0, 0))],
                 outs=[((s, nblk * LANES), MXU_DTYPE, (s, LANES), lambda c: (0, c)),
                       ((DN_CONV, nblk * LANES), F32, (DN_CONV, LANES), lambda c: (0, c))],
                 name=name, scratch=[pltpu.VMEM((s + pad, LANES), F32), pltpu.VMEM((s + pad, LANES), F32)])


def _cf_conv_fwd(vg, dw_w, dw_b):
    s, c2 = vg.shape
    ch = c2 // 2
    nblk = ch // LANES
    taps = dw_w.shape[0]
    pad = _conv_pad(taps)
    rows = min(CONV_TILE, s)

    def body(v_ref, g_ref, w_ref, b_ref, o_ref, xp):
        xp[0:pad, :] = jnp.zeros((pad, LANES), F32)
        xp[pad:, :] = v_ref[...] * jax.nn.sigmoid(g_ref[...])
        w = w_ref[...]
        bias = b_ref[...]

        def tile(i, c):
            r0, acc, _ = _conv_tile(xp, w, i, rows, taps)
            o_ref[pl.ds(r0, rows), :] = acc + bias
            return c

        lax.fori_loop(0, s // rows, tile, 0)

    return _call(body, grid=(nblk,),
                 ins=[(vg, (s, LANES), lambda c: (0, c)), (vg, (s, LANES), lambda c: (0, nblk + c)),
                      (dw_w, (taps, LANES), lambda c: (0, c)), (dw_b, (1, LANES), lambda c: (0, c))],
                 outs=[((s, ch), F32, (s, LANES), lambda c: (0, c))],
                 name="cf_conv_fwd", scratch=[pltpu.VMEM((s + pad, LANES), F32)])[0]


def _cf_conv_bwd(vg, dw_w, du):
    s, c2 = vg.shape
    ch = c2 // 2
    nblk = ch // LANES
    taps = dw_w.shape[0]
    pad = _conv_pad(taps)
    rows = min(CONV_TILE, s)

    def body(v_ref, g_ref, w_ref, du_ref, dv_ref, dg_ref, dw_ref, db_ref, xp, yp):
        sig = jax.nn.sigmoid(g_ref[...])
        xp[0:pad, :] = jnp.zeros((pad, LANES), F32)
        xp[pad:, :] = v_ref[...] * sig
        yp[0:s, :] = du_ref[...]
        yp[s:, :] = jnp.zeros((pad, LANES), F32)
        w = w_ref[...]
        db_ref[...] = jnp.sum(du_ref[...], axis=0, keepdims=True)

        def tile(i, dw):
            r0, _, views = _conv_tile(xp, w, i, rows, taps)
            return dw + _tap_sums(du_ref[pl.ds(r0, rows), :], views, taps)

        dw_ref[...] = lax.fori_loop(0, s // rows, tile, jnp.zeros((taps, LANES), F32))

        def tile2(i, c):
            r0, du0 = _conv_back_tile(yp, w, i, rows, taps)
            val = v_ref[pl.ds(r0, rows), :]
            sg = jax.nn.sigmoid(g_ref[pl.ds(r0, rows), :])
            dv_ref[pl.ds(r0, rows), :] = (du0 * sg).astype(dv_ref.dtype)
            dg_ref[pl.ds(r0, rows), :] = (du0 * val * sg * (1.0 - sg)).astype(dg_ref.dtype)
            return c

        lax.fori_loop(0, s // rows, tile2, 0)

    return _call(body, grid=(nblk,),
                 ins=[(vg, (s, LANES), lambda c: (0, c)), (vg, (s, LANES), lambda c: (0, nblk + c)),
                      (dw_w, (taps, LANES), lambda c: (0, c)), (du, (s, LANES), lambda c: (0, c))],
                 outs=[((s, ch), MXU_DTYPE, (s, LANES), lambda c: (0, c)),
                       ((s, ch), MXU_DTYPE, (s, LANES), lambda c: (0, c)),
                       ((taps, ch), F32, (taps, LANES), lambda c: (0, c)),
                       ((1, ch), F32, (1, LANES), lambda c: (0, c))],
                 name="cf_conv_bwd", scratch=[pltpu.VMEM((s + pad, LANES), F32), pltpu.VMEM((s + pad, LANES), F32)])


def _masks():
    r = lax.broadcasted_iota(jnp.int32, (CHUNK, CHUNK), 0)
    c = lax.broadcasted_iota(jnp.int32, (CHUNK, CHUNK), 1)
    return r >= c, r > c, r <= c


def _chunk_decay(g):
    causal, _, upper = _masks()
    gb = jnp.broadcast_to(g, (CHUNK, CHUNK))
    gam_r = _dot01(jnp.where(causal, 1.0, 0.0), gb)
    gam_s = _dot01(jnp.ones((CHUNK, CHUNK), F32), jnp.where(upper, gb, 0.0))
    dm = jnp.where(causal, jnp.exp(jnp.where(causal, gam_r - gam_s, 0.0)), 0.0)
    return gam_r[:, 0:1], dm


def _chunk_scores(q, k, beta, dm):
    _, strict, _ = _masks()
    both = _mdot(jnp.concatenate([k * beta, q * (HEAD_DIM ** -0.5)], axis=0), k, NT)
    return jnp.where(strict, both[:CHUNK] * dm, 0.0), both[CHUNK:] * dm


def _lockstep(gens):
    results = [None] * len(gens)
    alive = list(range(len(gens)))
    while alive:
        for i in list(alive):
            try:
                next(gens[i])
            except StopIteration as stop:
                results[i] = stop.value
                alive.remove(i)
    return results


def _chunk_prep_bwd(q, k, v, beta, gam, t, du, dw, daqk, dqd, dkd, dgl):
    causal, strict, _ = _masks()
    r = lax.broadcasted_iota(jnp.int32, (CHUNK, CHUNK), 0)
    c = lax.broadcasted_iota(jnp.int32, (CHUNK, CHUNK), 1)
    scale = HEAD_DIM ** -0.5
    eg = jnp.exp(gam)
    gam_last = gam[CHUNK - 1:CHUNK, :]
    rr = jnp.exp(gam_last - gam)
    kb = k * beta
    qs = q * scale
    vb = v * beta
    kbe = kb * eg
    gam_b = jnp.broadcast_to(gam, (CHUNK, CHUNK))
    gam_s = _dot01(jnp.ones((CHUNK, CHUNK), F32), jnp.where(r == c, gam_b, 0.0))
    both = _mdot(jnp.concatenate([kb, qs], axis=0), k, NT)
    duw = jnp.concatenate([du, dw], axis=1)
    dt = _mdot(duw, jnp.concatenate([vb, kbe], axis=1), NT)
    dvk = _mdot(t, duw, TN)
    yield
    dm = jnp.where(causal, jnp.exp(jnp.where(causal, gam_b - gam_s, 0.0)), 0.0)
    a = jnp.where(strict, both[:CHUNK] * dm, 0.0)
    aqk = both[CHUNK:] * dm
    dvb, dkbe = dvk[:, :HEAD_DIM], dvk[:, HEAD_DIM:]
    x = _mdot(t, dt, TN)
    yield
    da = jnp.where(strict, -_mdot(x, t, NT), 0.0)
    yield
    dkk = da * dm
    dqk = daqk * dm
    ddiff = da * a + daqk * aqk
    dboth = jnp.concatenate([dkk, dqk], axis=0)
    dkq = _mdot(dboth, k)
    dk_mm = _mdot(dboth, jnp.concatenate([kb, qs], axis=0), TN)
    colsum = _dot01(ddiff, jnp.ones((CHUNK, LANES), F32), TN, mask_first=False)[:, 0:1]
    yield
    dkb = dkq[:CHUNK] + dkbe * eg
    dk = dk_mm + dkb * beta + dkd * rr
    dq = (dkq[CHUNK:] + dqd * eg) * scale
    dbeta = jnp.sum(dkb * k, axis=-1, keepdims=True) + jnp.sum(dvb * v, axis=-1, keepdims=True)
    dv = dvb * beta
    deg = jnp.sum(dkbe * kb, axis=-1, keepdims=True) + jnp.sum(dqd * qs, axis=-1, keepdims=True)
    drr = jnp.sum(dkd * k, axis=-1, keepdims=True)
    dgam = deg * eg - drr * rr + jnp.sum(ddiff, axis=-1, keepdims=True) - colsum
    dgam_last = jnp.sum(drr * rr, axis=0, keepdims=True) + dgl[0:1, :] * jnp.exp(gam_last)
    row = lax.broadcasted_iota(jnp.int32, (CHUNK, 1), 0)
    dgam = dgam + jnp.where(row == CHUNK - 1, dgam_last, 0.0)
    dg = _dot01(jnp.where(causal, 1.0, 0.0), jnp.broadcast_to(dgam, (CHUNK, LANES)), TN)[:, 0:1]
    return dq, dk, dv, dbeta, dg


def _prep_group(s):
    nch = s // CHUNK
    return next(c for c in (16, 8, 4, 2, 1) if nch % c == 0)


def _tri_solve_lanes(a_l):
    n = a_l.shape[1]
    group = 8

    def body(a_ref, t_ref):
        t_ref[...] = jnp.zeros_like(t_ref)
        col = lax.broadcasted_iota(jnp.int32, (CHUNK, n), 0)

        def row(r, carry):
            r0 = pl.multiple_of(r * CHUNK, CHUNK)

            def inner(sg, acc):
                a8 = a_ref[pl.ds(r0 + pl.multiple_of(sg * group, group), group), :]
                for j in range(group):
                    t0 = pl.multiple_of((sg * group + j) * CHUNK, CHUNK)
                    acc = acc + a8[j:j + 1, :] * t_ref[pl.ds(t0, CHUNK), :]
                return acc

            acc = lax.fori_loop(0, (r + group - 1) // group, inner, jnp.zeros((CHUNK, n), F32))
            t_ref[pl.ds(r0, CHUNK), :] = jnp.where(col == r, 1.0, 0.0) - acc
            return carry

        lax.fori_loop(0, CHUNK, row, 0)

    return pl.pallas_call(body, out_shape=jax.ShapeDtypeStruct(a_l.shape, F32), name="dn_tri_solve")(a_l)


def _head_cols(bg, hh, heads):
    lane = lax.broadcasted_iota(jnp.int32, bg.shape, 1)
    beta = jnp.sum(jnp.where(lane == hh, bg, 0.0), axis=-1, keepdims=True)
    g = jnp.sum(jnp.where(lane == heads + hh, bg, 0.0), axis=-1, keepdims=True)
    return beta, g


def _dn_prep(q, k, v, bg):
    h, s, _ = q.shape
    cb = _prep_group(s)
    rb = cb * CHUNK
    big = lambda x: (x, (None, rb, HEAD_DIM), lambda n, hh: (hh, n, 0))
    sq = lambda x: (x, (None, rb, CHUNK), lambda n, hh: (hh, n, 0))
    col = lambda x: (x, (None, rb, 1), lambda n, hh: (hh, n, 0))
    tok = (bg, (rb, LANES), lambda n, hh: (n, 0))
    o_big = ((h, s, HEAD_DIM), F32, (None, rb, HEAD_DIM), lambda n, hh: (hh, n, 0))
    o_sq = ((h, s, CHUNK), F32, (None, rb, CHUNK), lambda n, hh: (hh, n, 0))
    o_col = ((h, s, 1), F32, (None, rb, 1), lambda n, hh: (hh, n, 0))

    def scores(q_ref, k_ref, bg_ref, a_ref, aqk_ref, gam_ref):
        beta, g = _head_cols(bg_ref[...], pl.program_id(1), h)
        for i in range(cb):
            sl = slice(i * CHUNK, (i + 1) * CHUNK)
            gam, dm = _chunk_decay(g[sl])
            a_ref[sl, :], aqk_ref[sl, :] = _chunk_scores(q_ref[sl, :], k_ref[sl, :], beta[sl], dm)
            gam_ref[sl, :] = gam

    a, aqk, gam = _call(scores, grid=(s // rb, h), ins=[big(q), big(k), tok], outs=[o_sq, o_sq, o_col],
                        name="dn_scores")
    n_prob = h * (s // CHUNK)
    t_l = _tri_solve_lanes(jnp.transpose(a.reshape(n_prob, CHUNK * CHUNK)))
    t = jnp.transpose(t_l).reshape(h, s, CHUNK)

    def wy(k_ref, v_ref, bg_ref, gam_ref, t_ref, u_ref, w_ref):
        beta, _ = _head_cols(bg_ref[...], pl.program_id(1), h)
        for i in range(cb):
            sl = slice(i * CHUNK, (i + 1) * CHUNK)
            kb = k_ref[sl, :] * beta[sl]
            rhs = jnp.concatenate([v_ref[sl, :] * beta[sl], kb * jnp.exp(gam_ref[sl, :])], axis=1)
            uw = _mdot(t_ref[sl, :], rhs)
            u_ref[sl, :] = uw[:, :HEAD_DIM]
            w_ref[sl, :] = uw[:, HEAD_DIM:]

    u, w = _call(wy, grid=(s // rb, h), ins=[big(k), big(v), tok, col(gam), sq(t)], outs=[o_big, o_big],
                 name="dn_wy")
    return u, w, aqk, t, gam


def _dn_prep_bwd(q, k, v, bg, gam, t, du, dw, daqk, dqd, dkd, dgl):
    h, s, _ = q.shape
    cb = _prep_group(s)
    rb = cb * CHUNK

    def body(q_ref, k_ref, v_ref, bg_ref, g_ref, t_ref, du_ref, dw_ref, da_ref, dqd_ref, dkd_ref, dgl_ref,
             dq_ref, dk_ref, dv_ref, dbg_ref):
        hh = pl.program_id(1)
        beta, _ = _head_cols(bg_ref[...], hh, h)
        slices = [slice(i * CHUNK, (i + 1) * CHUNK) for i in range(cb)]
        results = _lockstep([_chunk_prep_bwd(
            q_ref[sl, :], k_ref[sl, :], v_ref[sl, :], beta[sl], g_ref[sl, :], t_ref[sl, :],
            du_ref[sl, :], dw_ref[sl, :], da_ref[sl, :], dqd_ref[sl, :], dkd_ref[sl, :], dgl_ref[sl, :])
            for sl in slices])

        @pl.when(hh == 0)
        def _():
            dbg_ref[...] = jnp.zeros_like(dbg_ref)

        lane = lax.broadcasted_iota(jnp.int32, (CHUNK, LANES), 1)
        for sl, (dq, dk, dv, dbeta, dg) in zip(slices, results):
            dq_ref[sl, :] = dq
            dk_ref[sl, :] = dk
            dv_ref[sl, :] = dv
            dbg_ref[sl, :] += jnp.where(lane == hh, dbeta, 0.0) + jnp.where(lane == h + hh, dg, 0.0)

    big = lambda x: (x, (None, rb, HEAD_DIM), lambda n, hh: (hh, n, 0))
    sq = lambda x: (x, (None, rb, CHUNK), lambda n, hh: (hh, n, 0))
    col = lambda x: (x, (None, rb, 1), lambda n, hh: (hh, n, 0))
    tok = (bg, (rb, LANES), lambda n, hh: (n, 0))
    o_big = ((h, s, HEAD_DIM), F32, (None, rb, HEAD_DIM), lambda n, hh: (hh, n, 0))
    return _call(body, grid=(s // rb, h),
                 ins=[big(q), big(k), big(v), tok, col(gam), sq(t), big(du), big(dw), sq(daqk), big(dqd), big(dkd),
                      col(dgl)],
                 outs=[o_big, o_big, o_big, ((s, LANES), F32, (rb, LANES), lambda n, hh: (n, 0))], name="dn_prep_bwd")


def _chunk_scaled(q, k, gam):
    gam_last = gam[CHUNK - 1:CHUNK, :]
    q_dec = q * (HEAD_DIM ** -0.5) * jnp.exp(gam)
    k_dec = k * jnp.exp(gam_last - gam)
    return q_dec, k_dec, jnp.exp(gam_last)


def _scan_group(s):
    return 2 if (s // CHUNK) % 2 == 0 else 1


def _dn_scan(q, k, u, w, aqk, gam):
    h, s, _ = q.shape
    nch = s // CHUNK
    sg = _scan_group(s)
    rb = sg * CHUNK

    def body(q_ref, k_ref, u_ref, w_ref, a_ref, gam_ref, o_ref, st_ref, state):
        @pl.when(pl.program_id(0) == 0)
        def _():
            state[...] = jnp.zeros_like(state)

        def head(hh, c):
            sl = slice(c * CHUNK, (c + 1) * CHUNK)
            s0 = state[hh]
            st_ref[c, hh] = s0
            q_dec, k_dec, gl = _chunk_scaled(q_ref[hh, sl, :], k_ref[hh, sl, :], gam_ref[hh, sl, :])
            both = _mdot(jnp.concatenate([w_ref[hh, sl, :], q_dec], axis=0), s0)
            yield
            v_new = u_ref[hh, sl, :] - both[:CHUNK]
            o_ref[sl, hh * HEAD_DIM:(hh + 1) * HEAD_DIM] = both[CHUNK:] + _mdot(a_ref[hh, sl, :], v_new)
            state[hh] = s0 * gl + _mdot(k_dec, v_new, TN)

        for c in range(sg):
            _lockstep([head(hh, c) for hh in range(h)])

    big = lambda x: (x, (h, rb, HEAD_DIM), lambda n: (0, n, 0))
    return _call(body, grid=(nch // sg,),
                 ins=[big(q), big(k), big(u), big(w), (aqk, (h, rb, CHUNK), lambda n: (0, n, 0)),
                      (gam, (h, rb, 1), lambda n: (0, n, 0))],
                 outs=[((s, h * HEAD_DIM), F32, (rb, h * HEAD_DIM), lambda n: (n, 0)),
                       ((nch, h, HEAD_DIM, HEAD_DIM), F32, (sg, h, HEAD_DIM, HEAD_DIM), lambda n: (n, 0, 0, 0))],
                 name="dn_scan", scratch=[pltpu.VMEM((h, HEAD_DIM, HEAD_DIM), F32)])


def _dn_scan_bwd(q, k, u, w, aqk, gam, states, do):
    h, s, _ = q.shape
    nch = s // CHUNK
    sg = _scan_group(s)
    rb = sg * CHUNK
    ngr = nch // sg

    def body(q_ref, k_ref, u_ref, w_ref, a_ref, gam_ref, st_ref, do_ref,
             du_ref, dw_ref, da_ref, dqd_ref, dkd_ref, dgl_ref, dstate):
        @pl.when(pl.program_id(0) == 0)
        def _():
            dstate[...] = jnp.zeros_like(dstate)

        def head(hh, c):
            sl = slice(c * CHUNK, (c + 1) * CHUNK)
            s0 = st_ref[c, hh]
            ds = dstate[hh]
            doh = do_ref[sl, hh * HEAD_DIM:(hh + 1) * HEAD_DIM]
            wv = w_ref[hh, sl, :]
            q_dec, k_dec, gl = _chunk_scaled(q_ref[hh, sl, :], k_ref[hh, sl, :], gam_ref[hh, sl, :])
            ws = _mdot(wv, s0)
            dv_new = _mdot(a_ref[hh, sl, :], doh, TN) + _mdot(k_dec, ds)
            dqd_ref[hh, sl, :] = _mdot(doh, s0, NT)
            qdo = _mdot(q_dec, doh, TN)
            tot = jnp.sum(jnp.sum(s0 * ds, axis=-1, keepdims=True), axis=0, keepdims=True)
            dgl_ref[hh, sl, :] = jnp.broadcast_to(tot, (CHUNK, 1))
            yield
            v_new = u_ref[hh, sl, :] - ws
            du_ref[hh, sl, :] = dv_new
            dw_ref[hh, sl, :] = -_mdot(dv_new, s0, NT)
            da_ref[hh, sl, :] = _mdot(doh, v_new, NT)
            dkd_ref[hh, sl, :] = _mdot(v_new, ds, NT)
            dstate[hh] = ds * gl + qdo - _mdot(wv, dv_new, TN)

        for c in range(sg - 1, -1, -1):
            _lockstep([head(hh, c) for hh in range(h)])

    rev = lambda n: (0, ngr - 1 - n, 0)
    big = lambda x: (x, (h, rb, HEAD_DIM), rev)
    o_big = ((h, s, HEAD_DIM), F32, (h, rb, HEAD_DIM), rev)
    return _call(body, grid=(ngr,),
                 ins=[big(q), big(k), big(u), big(w), (aqk, (h, rb, CHUNK), rev), (gam, (h, rb, 1), rev),
                      (states, (sg, h, HEAD_DIM, HEAD_DIM), lambda n: (ngr - 1 - n, 0, 0, 0)),
                      (do, (rb, h * HEAD_DIM), lambda n: (ngr - 1 - n, 0))],
                 outs=[o_big, o_big, ((h, s, CHUNK), F32, (h, rb, CHUNK), rev), o_big, o_big,
                       ((h, s, 1), F32, (h, rb, 1), rev)],
                 name="dn_scan_bwd", scratch=[pltpu.VMEM((h, HEAD_DIM, HEAD_DIM), F32)])


def _gates(x, a_log, dt_b, h):
    lane = lax.broadcasted_iota(jnp.int32, x.shape, 1)
    return jnp.where(lane < h, jax.nn.sigmoid(x), -jnp.exp(a_log) * jax.nn.softplus(x + dt_b))


def _head_out(oh, zh, nw):
    on = oh * lax.rsqrt(jnp.mean(oh * oh, axis=-1, keepdims=True) + RMS_EPS) * nw
    return on * jax.nn.silu(zh)


def _pad_lanes(x, lo):
    return jnp.zeros((1, LANES), F32).at[0, lo:lo + x.shape[0]].set(x)


def _deltanet_fwd(hin, get_w_in, conv_w, a_log, dt_bias, norm_w, get_w_out):
    h = a_log.shape[0]
    hw = h * HEAD_DIM
    w_in = get_w_in(hin)
    proj = _mm_nn(hin, w_in, F32, "dn_proj")
    q = _dn_conv_fwd(proj, conv_w, 0, h, True, "dn_conv_q")
    k = _dn_conv_fwd(proj, conv_w, h, h, True, "dn_conv_k")
    v = _dn_conv_fwd(proj, conv_w, 2 * h, h, False, "dn_conv_v")
    alp, dtp = _pad_lanes(a_log, h), _pad_lanes(dt_bias, h)

    def gates_fn(x, al, db):
        return (_gates(x, al, db, h),), ()

    (bg,), _ = _rowmap(gates_fn, [(proj, LANES, 4 * h)], [alp, dtp], [(LANES, F32)], [], "dn_gates")
    u, w, aqk, t, gam = _dn_prep(q, k, v, bg)
    o, states = _dn_scan(q, k, u, w, aqk, gam)
    nw = norm_w[None, :]

    def out_fn(o, z, nw):
        parts = [_head_out(o[:, i * HEAD_DIM:(i + 1) * HEAD_DIM], z[:, i * HEAD_DIM:(i + 1) * HEAD_DIM], nw)
                 for i in range(h)]
        return (jnp.concatenate(parts, axis=-1),), ()

    (og,), _ = _rowmap(out_fn, [o, (proj, hw, 3)], [nw], [(hw, MXU_DTYPE)], [], "dn_out")
    w_out = get_w_out(og)
    y = _mm_nn(og, w_out, F32, "dn_y")
    return y, (hin, proj, q, k, v, bg, u, w, aqk, t, gam, states, o, og, alp, dtp, nw, w_in, w_out)


def _deltanet_bwd(res, dy, conv_w, send):
    hin, proj, q, k, v, bg, u, w, aqk, t, gam, states, o, og, alp, dtp, nw, w_in, w_out = res
    h = q.shape[0]
    hw = h * HEAD_DIM
    s = hin.shape[0]
    d_w_out = _mm_tn(og, dy, MXU_DTYPE, "dn_dwout")
    dog = _mm_nt(dy, w_out, F32, "dn_dog")

    def out_bwd(o, z, dog, nw):
        dos, dzs = [], []
        dn = jnp.zeros((1, HEAD_DIM), F32)
        for i in range(h):
            sl = slice(i * HEAD_DIM, (i + 1) * HEAD_DIM)
            _, vjp = jax.vjp(_head_out, o[:, sl], z[:, sl], nw)
            a, b, c = vjp(dog[:, sl])
            dos.append(a)
            dzs.append(b)
            dn = dn + c
        return (jnp.concatenate(dos, axis=-1), jnp.concatenate(dzs, axis=-1)), (dn,)

    (do, dz), (d_norm_w,) = _rowmap(out_bwd, [o, (proj, hw, 3), dog], [nw], [(hw, F32), (hw, MXU_DTYPE)],
                                    [(1, HEAD_DIM)], "dn_out_bwd")
    du, dw, daqk, dqd, dkd, dgl = _dn_scan_bwd(q, k, u, w, aqk, gam, states, do)
    dq, dk, dv, dbg = _dn_prep_bwd(q, k, v, bg, gam, t, du, dw, daqk, dqd, dkd, dgl)
    dpq, dwq = _dn_conv_bwd(proj, conv_w, dq, 0, h, True, "dn_conv_q_bwd")
    dpk, dwk = _dn_conv_bwd(proj, conv_w, dk, h, h, True, "dn_conv_k_bwd")
    dpv, dwv = _dn_conv_bwd(proj, conv_w, dv, 2 * h, h, False, "dn_conv_v_bwd")

    def gates_bwd(x, dbg, al, db):
        _, vjp = jax.vjp(functools.partial(_gates, h=h), x, al, db)
        gx, gal, gdb = vjp(dbg)
        return (gx,), (gal, gdb)

    (dba,), (d_alp, d_dtp) = _rowmap(gates_bwd, [(proj, LANES, 4 * h), dbg], [alp, dtp], [(LANES, MXU_DTYPE)],
                                     [(1, LANES), (1, LANES)], "dn_gates_bwd")
    dproj = jnp.concatenate([dpq, dpk, dpv, dz, dba], axis=1)
    d_w_in = _mm_tn(hin, dproj, MXU_DTYPE, "dn_dwin")
    token = send(d_w_in, d_w_out)
    dh = _mm_nt(dproj, w_in, F32, "dn_dh", pin=token)
    d_conv_w = jnp.concatenate([dwq, dwk, dwv], axis=1)
    return dh, dict(conv_w=d_conv_w, a_log=d_alp[0, h:2 * h], dt_bias=d_dtp[0, h:2 * h], norm_w=d_norm_w[0]), token


def _ln_silu(u, g, b):
    return jax.nn.silu(_ln(u, g, b))


def _conformer_fwd(hin, get_w_in, dw_w, dw_b, ln_g, ln_b, get_w_out):
    w_in = get_w_in(hin)
    vg = _mm_nn(hin, w_in, F32, "cf_vg")
    u1 = _cf_conv_fwd(vg, dw_w, dw_b)
    ch = u1.shape[1]

    def fn(u, g, b):
        return (_ln_silu(u, g, b),), ()

    (u2,), _ = _rowmap(fn, [u1], [ln_g, ln_b], [(ch, MXU_DTYPE)], [], "cf_ln")
    w_out = get_w_out(u2)
    y = _mm_nn(u2, w_out, F32, "cf_y")
    return y, (hin, vg, u1, u2, w_in, w_out)


def _conformer_bwd(res, dy, dw_w, ln_g, ln_b):
    hin, vg, u1, u2, w_in, w_out = res
    ch = u1.shape[1]
    d_w_out = _mm_tn(u2, dy, MXU_DTYPE, "cf_dwout")
    du2 = _mm_nt(dy, w_out, F32, "cf_du2")

    def fn(u, du2, g, b):
        _, vjp = jax.vjp(_ln_silu, u, g, b)
        gu, gg, gb = vjp(du2)
        return (gu,), (gg, gb)

    (du1,), (d_ln_g, d_ln_b) = _rowmap(fn, [u1, du2], [ln_g, ln_b], [(ch, F32)], [(1, ch), (1, ch)], "cf_ln_bwd")
    dval, dgate, d_dw_w, d_dw_b = _cf_conv_bwd(vg, dw_w, du1)
    dvg = jnp.concatenate([dval, dgate], axis=1)
    d_w_in = _mm_tn(hin, dvg, MXU_DTYPE, "cf_dwin", split_cols=True)
    dh = _mm_nt(dvg, w_in, F32, "cf_dh")
    return dh, dict(w_in=d_w_in, w_out=d_w_out, dw_w=d_dw_w, dw_b=d_dw_b[0], ln_g=d_ln_g[0], ln_b=d_ln_b[0])


def _mlp_fwd(hin, get_w1, get_w2):
    w1 = get_w1(hin)
    r = _mm_nn(hin, w1, MXU_DTYPE, "ff_a", relu2=True)
    w2 = get_w2(r)
    m = _mm_nn(r, w2, F32, "ff_m")
    return m, (hin, r, w1, w2)


def _mlp_bwd(res, dm):
    hin, r, w1, w2 = res
    d_w2 = _mm_tn(r, dm, MXU_DTYPE, "ff_dw2")
    da = _mm_nt(dm, w2, MXU_DTYPE, "ff_da", relu2_sq=r)
    d_w1 = _mm_tn(hin, da, MXU_DTYPE, "ff_dw1", split_cols=True)
    dh = _mm_nt(da, w1, F32, "ff_dh")
    return dh, d_w1, d_w2


def _ada_fwd(c_all, ada_w):
    depth, d, nl = ada_w.shape
    tn = _tile(nl, 256)

    def body(c_ref, w_ref, o_ref, cond_ref):
        cond = jax.nn.silu(c_ref[...]).astype(MXU_DTYPE)
        cond_ref[...] = cond
        o_ref[...] = lax.dot_general(cond, w_ref[...].astype(MXU_DTYPE), (NN, ((), ())), preferred_element_type=F32)

    return _call(body, grid=(depth, nl // tn),
                 ins=[(c_all, c_all.shape, lambda l, j: (0, 0)), (ada_w, (None, d, tn), lambda l, j: (l, 0, j))],
                 outs=[((depth, N_DEV, nl), F32, (None, N_DEV, tn), lambda l, j: (l, 0, j)),
                       (c_all.shape, MXU_DTYPE, c_all.shape, lambda l, j: (0, 0))],
                 name="ada_fwd")


def _ada_bwd(cond_all, dmod_cols):
    depth, _, nl = dmod_cols.shape
    d = cond_all.shape[1]
    tn = _tile(nl, 256)

    def body(c_ref, g_ref, o_ref):
        o_ref[...] = lax.dot_general(c_ref[...], g_ref[...].astype(MXU_DTYPE), (TN, ((), ())),
                                     preferred_element_type=F32)

    return _call(body, grid=(depth, nl // tn),
                 ins=[(cond_all, cond_all.shape, lambda l, j: (0, 0)), (dmod_cols, (None, N_DEV, tn), lambda l, j: (l, 0, j))],
                 outs=[((depth, d, nl), F32, (None, d, tn), lambda l, j: (l, 0, j))], name="ada_bwd")[0]


def _peers():
    x, y, c = lax.axis_index("x"), lax.axis_index("y"), lax.axis_index("c")
    peers = []
    for k in range(1, N_DEV):
        px = 1 - x if k & 4 else x
        py = 1 - y if k & 2 else y
        pc = 1 - c if k & 1 else c
        peers.append(((px, py, pc), 4 * px + 2 * py + pc))
    return 4 * x + 2 * y + c, peers


_HBM = pl.BlockSpec(memory_space=pltpu.HBM)
_SEM = pl.BlockSpec(memory_space=pltpu.SEMAPHORE)
_ANY = pl.BlockSpec(memory_space=pl.ANY)
_EFFECT = pltpu.SideEffectType.DATAFLOW_SIDE_EFFECTING


def _xfer_start(srcs, lands, scatter, after, name):
    nt = len(srcs)

    def body(*refs):
        src, land = refs[:nt], refs[nt:2 * nt]
        sems = refs[2 * nt + 1:4 * nt + 1]
        token = refs[-1]
        me, peers = _peers()
        for t in range(nt):
            for k, (pid, plin) in enumerate(peers):
                pltpu.make_async_remote_copy(
                    src_ref=src[t].at[plin] if scatter else src[t], dst_ref=land[t].at[me],
                    send_sem=sems[2 * t].at[k], recv_sem=sems[2 * t + 1].at[k],
                    device_id=pid, device_id_type=pl.DeviceIdType.MESH).start()
        token[...] = jnp.zeros_like(token)

    out_shape = [pltpu.SemaphoreType.DMA((N_DEV - 1,)) for _ in range(2 * nt)]
    out_shape += [pltpu.HBM(a.shape, a.dtype) for a in lands]
    out_shape += [jax.ShapeDtypeStruct((8, LANES), F32)]
    srcs = [pltpu.with_memory_space_constraint(a, pltpu.HBM) for a in srcs]
    res = pl.pallas_call(
        body, name=name, out_shape=out_shape,
        in_specs=[_HBM] * (2 * nt) + [_ANY],
        out_specs=[_SEM] * (2 * nt) + [_HBM] * nt + [pl.BlockSpec(memory_space=pltpu.VMEM)],
        input_output_aliases={nt + i: 2 * nt + i for i in range(nt)},
        compiler_params=pltpu.CompilerParams(has_side_effects=_EFFECT),
    )(*srcs, *[pltpu.with_memory_space_constraint(a, pltpu.HBM) for a in lands], after)
    sems, thru = res[:2 * nt], res[2 * nt:3 * nt]
    return [(sems[2 * t], sems[2 * t + 1], srcs[t], thru[t]) for t in range(nt)], res[-1]


def _xfer_wait(handle, scatter, after, name):
    send, recv, src, land = handle

    def body(src_ref, land_ref, send_sem, recv_sem, after_ref, land_out):
        _, peers = _peers()
        for k, (pid, plin) in enumerate(peers):
            cp = pltpu.make_async_remote_copy(
                src_ref=src_ref.at[plin] if scatter else src_ref, dst_ref=land_ref.at[plin],
                send_sem=send_sem.at[k], recv_sem=recv_sem.at[k],
                device_id=pid, device_id_type=pl.DeviceIdType.MESH)
            cp.wait_send()
            cp.wait_recv()

    return pl.pallas_call(
        body, name=name, out_shape=pltpu.HBM(land.shape, land.dtype),
        in_specs=(_HBM, _HBM, _SEM, _SEM, _ANY), out_specs=_HBM, input_output_aliases={1: 0},
        compiler_params=pltpu.CompilerParams(has_side_effects=_EFFECT),
    )(src, land, send, recv, after)


def _landing(x, me):
    return lax.dynamic_update_slice(lax.empty((N_DEV,) + x.shape, x.dtype), x[None], (me,) + (0,) * x.ndim)


def _chip_peers():
    x, y, c = lax.axis_index("x"), lax.axis_index("y"), lax.axis_index("c")
    lin = lambda px, py, pc: 4 * px + 2 * py + pc
    sibling = ((x, y, 1 - c), lin(x, y, 1 - c))
    chips = [((1 - x, y, c), lin(1 - x, y, c)), ((x, 1 - y, c), lin(x, 1 - y, c)),
             ((1 - x, 1 - y, c), lin(1 - x, 1 - y, c))]
    return lin(x, y, c), sibling, chips


N_CHIPS_OTHER = 3


def _gather2_start(srcs, lands, after, name):
    nt = len(srcs)

    def body(*refs):
        src, land = refs[:nt], refs[nt:2 * nt]
        sems = refs[2 * nt + 1:5 * nt + 1]
        token = refs[-1]
        me, sibling, chips = _chip_peers()
        for t in range(nt):
            send, recv_ici, recv_sib = sems[3 * t], sems[3 * t + 1], sems[3 * t + 2]
            pltpu.make_async_remote_copy(src_ref=src[t], dst_ref=land[t].at[me], send_sem=send.at[0],
                                         recv_sem=recv_sib.at[0], device_id=sibling[0],
                                         device_id_type=pl.DeviceIdType.MESH).start()
            for j, (pid, _) in enumerate(chips):
                pltpu.make_async_remote_copy(src_ref=src[t], dst_ref=land[t].at[me], send_sem=send.at[1 + j],
                                             recv_sem=recv_ici.at[j], device_id=pid,
                                             device_id_type=pl.DeviceIdType.MESH).start()
        token[...] = jnp.zeros_like(token)

    out_shape = []
    for _ in range(nt):
        out_shape += [pltpu.SemaphoreType.DMA((1 + N_CHIPS_OTHER,)), pltpu.SemaphoreType.DMA((N_CHIPS_OTHER,)),
                      pltpu.SemaphoreType.DMA((1,))]
    out_shape += [pltpu.HBM(a.shape, a.dtype) for a in list(srcs) + list(lands)]
    out_shape += [jax.ShapeDtypeStruct((8, LANES), F32)]
    res = pl.pallas_call(
        body, name=name, out_shape=out_shape,
        in_specs=[_HBM] * (2 * nt) + [_ANY],
        out_specs=[_SEM] * (3 * nt) + [_HBM] * (2 * nt) + [pl.BlockSpec(memory_space=pltpu.VMEM)],
        input_output_aliases={i: 3 * nt + i for i in range(2 * nt)},
        compiler_params=pltpu.CompilerParams(has_side_effects=_EFFECT),
    )(*[pltpu.with_memory_space_constraint(a, pltpu.HBM) for a in list(srcs) + list(lands)], after)
    sems, thru = res[:3 * nt], res[3 * nt:5 * nt]
    return [(sems[3 * t], sems[3 * t + 1], sems[3 * t + 2], thru[t], thru[nt + t]) for t in range(nt)], res[-1]


def _gather2_relay(handles, after, name):
    nt = len(handles)
    after = list(after) if isinstance(after, (list, tuple)) else [after]

    def body(*refs):
        src, land = refs[:nt], refs[nt:2 * nt]
        send1, recv_ici = refs[2 * nt:3 * nt], refs[3 * nt:4 * nt]
        outs = refs[4 * nt + len(after):]
        send2, recv2 = outs[:nt], outs[nt:2 * nt]
        token = refs[-1]
        token[...] = jnp.zeros_like(token)
        me, sibling, chips = _chip_peers()
        for t in range(nt):
            pltpu.make_async_remote_copy(src_ref=src[t], dst_ref=land[t].at[me], send_sem=send1[t].at[0],
                                         recv_sem=recv_ici[t].at[0], device_id=sibling[0],
                                         device_id_type=pl.DeviceIdType.MESH).wait_send()
            for j, (pid, plin) in enumerate(chips):
                arrived = pltpu.make_async_remote_copy(src_ref=src[t], dst_ref=land[t].at[plin], send_sem=send1[t].at[1 + j],
                                                       recv_sem=recv_ici[t].at[j], device_id=pid,
                                                       device_id_type=pl.DeviceIdType.MESH)
                arrived.wait_send()
                arrived.wait_recv()
                pltpu.make_async_remote_copy(src_ref=land[t].at[plin], dst_ref=land[t].at[plin], send_sem=send2[t].at[j],
                                             recv_sem=recv2[t].at[j], device_id=sibling[0],
                                             device_id_type=pl.DeviceIdType.MESH).start()

    srcs = [h[3] for h in handles]
    lands = [h[4] for h in handles]
    out_shape = [pltpu.SemaphoreType.DMA((N_CHIPS_OTHER,)) for _ in range(2 * nt)]
    out_shape += [pltpu.HBM(a.shape, a.dtype) for a in srcs + lands]
    out_shape += [jax.ShapeDtypeStruct((8, LANES), F32)]
    res = pl.pallas_call(
        body, name=name, out_shape=out_shape,
        in_specs=[_HBM] * (2 * nt) + [_SEM] * (2 * nt) + [_ANY] * len(after),
        out_specs=[_SEM] * (2 * nt) + [_HBM] * (2 * nt) + [pl.BlockSpec(memory_space=pltpu.VMEM)],
        input_output_aliases={i: 2 * nt + i for i in range(2 * nt)},
        compiler_params=pltpu.CompilerParams(has_side_effects=_EFFECT),
    )(*srcs, *lands, *[h[0] for h in handles], *[h[1] for h in handles], *after)
    return [(handles[t][2], res[t], res[nt + t], res[3 * nt + t]) for t in range(nt)], res[-1]


def _gather2_wait(handle, after, name):
    recv_sib, send2, recv2, land = handle

    def body(land_ref, recv_sib_sem, send2_sem, recv2_sem, after_ref, land_out):
        me, sibling, chips = _chip_peers()
        pltpu.make_async_remote_copy(src_ref=land_ref.at[me], dst_ref=land_ref.at[sibling[1]], send_sem=send2_sem.at[0],
                                     recv_sem=recv_sib_sem.at[0], device_id=sibling[0],
                                     device_id_type=pl.DeviceIdType.MESH).wait_recv()
        for j, (pid, plin) in enumerate(chips):
            relayed = pltpu.make_async_remote_copy(src_ref=land_ref.at[plin], dst_ref=land_ref.at[plin], send_sem=send2_sem.at[j],
                                                   recv_sem=recv2_sem.at[j], device_id=sibling[0],
                                                   device_id_type=pl.DeviceIdType.MESH)
            relayed.wait_send()
            relayed.wait_recv()

    return pl.pallas_call(
        body, name=name, out_shape=pltpu.HBM(land.shape, land.dtype),
        in_specs=(_HBM, _SEM, _SEM, _SEM, _ANY), out_specs=_HBM, input_output_aliases={0: 0},
        compiler_params=pltpu.CompilerParams(has_side_effects=_EFFECT),
    )(land, recv_sib, send2, recv2, after)


def _exchange(arrs, scatter, name):
    nt = len(arrs)
    out_shape = [jax.ShapeDtypeStruct(a.shape if scatter else (N_DEV,) + a.shape, a.dtype) for a in arrs]

    def body(*refs):
        ins, outs = refs[:nt], refs[nt:2 * nt]
        send, recv, loc = refs[2 * nt:]
        me, peers = _peers()
        copies = []
        for t in range(nt):
            own = pltpu.make_async_copy(ins[t].at[me] if scatter else ins[t], outs[t].at[me], loc.at[t])
            own.start()
            copies.append(own)
            for k, (pid, plin) in enumerate(peers):
                cp = pltpu.make_async_remote_copy(
                    src_ref=ins[t].at[plin] if scatter else ins[t], dst_ref=outs[t].at[me],
                    send_sem=send.at[t, k], recv_sem=recv.at[t, k],
                    device_id=pid, device_id_type=pl.DeviceIdType.MESH)
                cp.start()
                copies.append(cp)
        for cp in copies:
            cp.wait()

    any_spec = pl.BlockSpec(memory_space=pl.ANY)
    return pl.pallas_call(
        body, out_shape=out_shape, in_specs=[any_spec] * nt, out_specs=[any_spec] * nt,
        scratch_shapes=[pltpu.SemaphoreType.DMA((nt, N_DEV - 1)), pltpu.SemaphoreType.DMA((nt, N_DEV - 1)),
                        pltpu.SemaphoreType.DMA((nt,))],
        name=name)(*arrs)


def _adamw_body(n_parts, stacked=True):
    def body(p_ref, w_ref, m_ref, v_ref, *rest):
        g_out, d_out, m_out, v_out = rest[-4:]
        part = (lambda i: p_ref[i]) if stacked else (lambda i: p_ref[i][...])
        g = part(0).astype(F32)
        for i in range(1, n_parts):
            g = g + part(i).astype(F32)
        m2 = ADAM_B1 * m_ref[...] + (1.0 - ADAM_B1) * g
        v2 = ADAM_B2 * v_ref[...] + (1.0 - ADAM_B2) * jnp.square(g)
        m_hat = m2 / (1.0 - ADAM_B1 ** ADAM_STEP)
        v_hat = v2 / (1.0 - ADAM_B2 ** ADAM_STEP)
        g_out[...] = g
        d_out[...] = -ADAM_LR * (m_hat / (jnp.sqrt(v_hat) + ADAM_EPS) + ADAM_WD * w_ref[...])
        m_out[...] = m2
        v_out[...] = v2

    return body


def _adamw_layer(own, land, me, w, m, v, layer, prev, name):
    _, r, c = own.shape
    tr = _tile(r, 256, 8)
    blk = pl.BlockSpec((None, tr, c), lambda i, me_ref: (layer, i, 0))
    share = lambda k: pl.BlockSpec((None, tr, c), lambda i, me_ref: (me_ref[0] ^ k, i, 0))
    in_specs = [share(k) for k in range(N_DEV)] + [blk, blk, blk]
    args = [own] + [land] * (N_DEV - 1) + [w, m, v]
    aliases = {}
    if prev is not None:
        in_specs += [_ANY] * 4
        args += list(prev)
        aliases = {1 + N_DEV + 3 + i: i for i in range(4)}

    def body(me_ref, *refs):
        token_ref = refs[-1]
        refs = (refs[:N_DEV],) + refs[N_DEV:-1]
        _adamw_body(N_DEV, stacked=False)(*refs)
        token_ref[...] = jnp.zeros(token_ref.shape, F32)

    token_blk = pl.BlockSpec((8, LANES), lambda i, me_ref: (0, 0))
    res = pl.pallas_call(
        body,
        grid_spec=pltpu.PrefetchScalarGridSpec(num_scalar_prefetch=1, grid=(r // tr,), in_specs=in_specs,
                                               out_specs=[blk] * 4 + [token_blk]),
        out_shape=[jax.ShapeDtypeStruct(w.shape, F32)] * 4 + [jax.ShapeDtypeStruct((8, LANES), F32)],
        input_output_aliases=aliases, name=name, compiler_params=_cparams(1))(me, *args)
    return res[:4], res[4]


def _adamw(parts, w, m, v, name):
    p, nl, r, c = parts.shape
    tr = _tile(r, 256, 8)
    body = _adamw_body(p)

    blk = (None, tr, c)
    imap = lambda l, i: (l, i, 0)
    out = ((nl, r, c), F32, blk, imap)
    return _call(body, grid=(nl, r // tr),
                 ins=[(parts, (p, None, tr, c), lambda l, i: (0, l, i, 0)), (w, blk, imap), (m, blk, imap), (v, blk, imap)],
                 outs=[out] * 4, name=name)


def _rows(x):
    return x.reshape(-1, LANES)


def _pad_rows(x, mult=8):
    r = x.shape[0]
    extra = (-r) % mult
    return jnp.pad(x, ((0, extra), (0, 0))) if extra else x


def kernel(x, c, ada_w, ada_b, ln_g, ln_b, dn_w_in, dn_conv_w, dn_a_log, dn_dt_bias, dn_norm_w, dn_w_out, cf_w_in, cf_dw_w, cf_dw_b, cf_ln_g, cf_ln_b, cf_w_out, ff_w1, ff_w2, loss_target, m_ada_w, m_ada_b, m_ln_g, m_ln_b, m_dn_w_in, m_dn_conv_w, m_dn_a_log, m_dn_dt_bias, m_dn_norm_w, m_dn_w_out, m_cf_w_in, m_cf_dw_w, m_cf_dw_b, m_cf_ln_g, m_cf_ln_b, m_cf_w_out, m_ff_w1, m_ff_w2, v_ada_w, v_ada_b, v_ln_g, v_ln_b, v_dn_w_in, v_dn_conv_w, v_dn_a_log, v_dn_dt_bias, v_dn_norm_w, v_dn_w_out, v_cf_w_in, v_cf_dw_w, v_cf_dw_b, v_cf_ln_g, v_cf_ln_b, v_cf_w_out, v_ff_w1, v_ff_w2):
    depth, d, _ = ada_w.shape
    n_a, n_b = dn_w_in.shape[0], cf_w_in.shape[0]
    heads = dn_a_log.shape[1]
    hw = heads * HEAD_DIM
    taps = cf_dw_w.shape[1]
    s = x.shape[1]
    alpha = (2.0 * depth) ** 0.25
    me = 4 * lax.axis_index("x") + 2 * lax.axis_index("y") + lax.axis_index("c")
    me_arr = jnp.reshape(me, (1,)).astype(jnp.int32)
    xs, tgt = x[0], loss_target[0]

    dn_in_cols = dn_w_in.shape[2]
    keys, shards = [], []
    for i in range(depth):
        j = i // 2
        mixer = [("dn_in", dn_w_in), ("dn_out", dn_w_out)] if i % 2 == 0 else [("cf_in", cf_w_in), ("cf_out", cf_w_out)]
        for nm, wt in mixer:
            keys.append((nm, j))
            shards.append(wt[j].astype(MXU_DTYPE))
        keys += [("ff1", i), ("ff2", i)]
        shards += [ff_w1[i].astype(MXU_DTYPE), ff_w2[i].astype(MXU_DTYPE)]

    small_local = [_rows(ln_g), _rows(ln_b), _rows(dn_conv_w), _rows(cf_dw_w), _rows(cf_dw_b), _rows(cf_ln_g),
                   _rows(cf_ln_b), _rows(c)]
    sizes = [a.shape[0] for a in small_local]
    packed = _pad_rows(jnp.concatenate(small_local, axis=0))
    (small_all,) = _exchange([packed], False, "comm_gather_params")
    offs = [0]
    for z in sizes:
        offs.append(offs[-1] + z)

    def small(i):
        return small_all[:, offs[i]:offs[i + 1], :]

    def unshard(piece, lead, groups):
        t = piece.reshape((N_DEV,) + lead + (groups * LANES,))
        t = jnp.moveaxis(t, 0, len(lead))
        return t.reshape(lead + (N_DEV * groups * LANES,))

    ln_g_f = unshard(small(0), (depth, 2), 1)
    ln_b_f = unshard(small(1), (depth, 2), 1)
    conv_w_f = unshard(small(2), (n_a, DN_CONV), 3 * heads // N_DEV)
    dw_w_f = unshard(small(3), (n_b, taps), 1)
    dw_b_f = unshard(small(4), (n_b,), 1)
    cf_ln_g_f = unshard(small(5), (n_b,), 1)
    cf_ln_b_f = unshard(small(6), (n_b,), 1)
    c_all = small(7).reshape(N_DEV, d)

    mod_part, cond_all = _ada_fwd(c_all, ada_w)
    (mod_all,) = _exchange([mod_part], False, "comm_gather_mod")
    mod_mine = lax.dynamic_index_in_dim(mod_all, me, axis=2, keepdims=False)
    mod_mine = jnp.moveaxis(mod_mine, 0, 1).reshape(depth, N_MOD * d)

    lands = [_landing(a, me) for a in shards]
    first, token = _gather2_start(shards[:1], lands[:1], mod_all, "gather_first_weight_start")
    handles = {keys[0]: first[0]}
    groups = [keys[:1], keys[1:4]] + [keys[4 * i:4 * i + 4] for i in range(1, depth)]
    group_of = {k: n for n, grp in enumerate(groups) for k in grp}
    relayed, weights = {}, {}
    wait_after = {}

    def relay(n, after):
        if n < len(groups) and groups[n][0] not in relayed:
            hs, relay_token = _gather2_relay([handles[k] for k in groups[n]], after, "gather_relay_%d" % n)
            relayed.update(zip(groups[n], hs))
            return relay_token

    def gathered(key, after):
        if key not in weights:
            relay(group_of[key], after)
            if key[0] == "ff1":
                relay(key[1] + 2, after)
            weights[key] = _gather2_wait(relayed[key], wait_after.get(key, after), "gather_wait_%s_%d" % key)
        return weights[key]

    def get_dn_in(j):
        def get(after):
            g = gathered(("dn_in", j), after)
            w = jnp.moveaxis(g, 0, 1).reshape(d, N_DEV * dn_in_cols)
            return jnp.pad(w, ((0, 0), (0, 4 * hw + LANES - N_DEV * dn_in_cols)))
        return get

    def get_rows(key):
        return lambda after: gathered(key, after).reshape((-1, d))

    def get_cols(key):
        return lambda after: gathered(key, after)

    def add_bias(a, b):
        return (a + b,), ()

    (mod,), _ = _rowmap(add_bias, [mod_mine, ada_b], [], [(N_MOD * d, F32)], [], "ada_bias", pin=token)
    mod_rows = mod.reshape(depth * N_MOD, 1, d)
    ln_g_rows = ln_g_f.reshape(depth * 2, 1, d)
    ln_b_rows = ln_b_f.reshape(depth * 2, 1, d)

    def mod_row(i, j):
        return (mod_rows, i * N_MOD + j)

    def ln_row(rows, i, j):
        return (rows, i * 2 + j)

    subs = []
    h_cur = _modulate_fwd(xs, mod_row(0, 1), mod_row(0, 0))
    relay_token = relay(0, [h_cur, m_dn_w_in, v_dn_w_in])
    rest, wait_after[keys[0]] = _gather2_start(shards[1:], lands[1:], relay_token, "gather_weights_start")
    handles.update(zip(keys[1:], rest))
    x_cur = xs
    last = None
    for i in range(depth):
        j = i // 2
        if i % 2 == 0:
            y, res = _deltanet_fwd(h_cur, get_dn_in(j), conv_w_f[j], dn_a_log[j], dn_dt_bias[j], dn_norm_w[j],
                                   get_rows(("dn_out", j)))
        else:
            y, res = _conformer_fwd(h_cur, get_cols(("cf_in", j)), dw_w_f[j], dw_b_f[j][None, :], cf_ln_g_f[j][None, :],
                                    cf_ln_b_f[j][None, :], get_rows(("cf_out", j)))
        p1 = (mod_row(i, 2), ln_row(ln_g_rows, i, 0), ln_row(ln_b_rows, i, 0), mod_row(i, 4), mod_row(i, 3))
        x_mid, h_mid = _combine_fwd(alpha, x_cur, y, *p1)
        subs.append((x_cur, y, p1, res))
        m_out, res2 = _mlp_fwd(h_mid, get_cols(("ff1", i)), get_rows(("ff2", i)))
        if i + 1 < depth:
            p2 = (mod_row(i, 5), ln_row(ln_g_rows, i, 1), ln_row(ln_b_rows, i, 1), mod_row(i + 1, 1), mod_row(i + 1, 0))
            x_next, h_next = _combine_fwd(alpha, x_mid, m_out, *p2)
            subs.append((x_mid, m_out, p2, res2))
            x_cur, h_cur = x_next, h_next
        else:
            p2 = (mod_row(i, 5), ln_row(ln_g_rows, i, 1), ln_row(ln_b_rows, i, 1))
            last = (x_mid, m_out, p2, res2)

    x_in, y_in, p_last, res_last = last
    dx, dy, (loss_acc, g_gt, g_g, g_b) = _last_fwd_bwd(alpha, x_in, y_in, tgt, *p_last)
    loss = lax.psum(loss_acc[0, 0], ("x", "y", "c"))

    d_mod = [[None] * N_MOD for _ in range(depth)]
    d_ln_g = [[None, None] for _ in range(depth)]
    d_ln_b = [[None, None] for _ in range(depth)]
    d_mod[depth - 1][5], d_ln_g[depth - 1][1], d_ln_b[depth - 1][1] = g_gt, g_g, g_b
    gw = dict(dn=[None] * n_a, cf=[None] * n_b)

    sent = {}

    def send_grads(named, tag):
        parts = [p for _, p in named]
        hs, tok = _xfer_start(parts, [lax.empty(p.shape, p.dtype) for p in parts], True, parts[0], "scatter_start_" + tag)
        for (key, _), hnd in zip(named, hs):
            sent[key] = hnd
        return tok

    def by_rows(g):
        return g.reshape((N_DEV, g.shape[0] // N_DEV, g.shape[1]))

    def send_mlp(i, d_w1, d_w2):
        return send_grads([(("ff1", i), d_w1), (("ff2", i), by_rows(d_w2))], "ff_%d" % i)

    dh, d_w1, d_w2 = _mlp_bwd(res_last, dy)
    pin = send_mlp(depth - 1, d_w1, d_w2)
    for idx in range(len(subs) - 1, -1, -1):
        x_in, y_in, prm, res = subs[idx]
        i, second = idx // 2, idx % 2
        dx, dy, (g_gt, g_g, g_b, g_sc, g_sh) = _combine_bwd(alpha, x_in, y_in, dx, dh, *prm, pin=pin)
        d_mod[i][5 if second else 2], d_ln_g[i][second], d_ln_b[i][second] = g_gt, g_g, g_b
        nxt_i, nxt_base = (i + 1, 0) if second else (i, 3)
        d_mod[nxt_i][nxt_base + 1], d_mod[nxt_i][nxt_base] = g_sc, g_sh
        j = i // 2
        if second:
            dh, d_w1, d_w2 = _mlp_bwd(res, dy)
            pin = send_mlp(i, d_w1, d_w2)
        elif i % 2 == 0:
            def send_dn(d_w_in, d_w_out, j=j):
                d_in = d_w_in[:, :N_DEV * dn_in_cols].reshape(d, N_DEV, dn_in_cols)
                return send_grads([(("dn_in", j), jnp.moveaxis(d_in, 1, 0)), (("dn_out", j), by_rows(d_w_out))],
                                  "dn_%d" % j)

            dh, gw["dn"][j], pin = _deltanet_bwd(res, dy, conv_w_f[j], send_dn)
        else:
            dh, gw["cf"][j] = _conformer_bwd(res, dy, dw_w_f[j], cf_ln_g_f[j][None, :], cf_ln_b_f[j][None, :])
            pin = send_grads([(("cf_in", j), gw["cf"][j]["w_in"]), (("cf_out", j), by_rows(gw["cf"][j]["w_out"]))],
                             "cf_%d" % j)
    grad_x, g_sc, g_sh = _modulate_bwd(xs, dx, dh, mod_row(0, 1), mod_row(0, 0), pin=pin)
    d_mod[0][1], d_mod[0][0] = g_sc, g_sh
    d_mod_full = jnp.concatenate([jnp.concatenate(r, axis=1) for r in d_mod], axis=0)

    stacked = {"dn_w_in": ("dn_in", dn_w_in, m_dn_w_in, v_dn_w_in), "dn_w_out": ("dn_out", dn_w_out, m_dn_w_out, v_dn_w_out),
               "cf_w_in": ("cf_in", cf_w_in, m_cf_w_in, v_cf_w_in), "cf_w_out": ("cf_out", cf_w_out, m_cf_w_out, v_cf_w_out),
               "ff_w1": ("ff1", ff_w1, m_ff_w1, v_ff_w1), "ff_w2": ("ff2", ff_w2, m_ff_w2, v_ff_w2)}
    chains = {key: None for key in stacked}

    def update_layer(i, token):
        mixer = ["dn_w_in", "dn_w_out"] if i % 2 == 0 else ["cf_w_in", "cf_w_out"]
        for key, idx in [("ff_w1", i), ("ff_w2", i)] + [(k, i // 2) for k in mixer]:
            short, w, m, v = stacked[key]
            land = _xfer_wait(sent[(short, idx)], True, token, "scatter_wait_%s_%d" % (short, idx))
            chains[key], token = _adamw_layer(sent[(short, idx)][2], land, me_arr, w, m, v, idx, chains[key],
                                              "adamw_%s_%d" % (key, idx))
        return token

    def stack_rows(lst):
        return jnp.stack(lst, axis=0)

    gs_ln_g = jnp.stack([jnp.concatenate(r, axis=0) for r in d_ln_g], axis=0)
    gs_ln_b = jnp.stack([jnp.concatenate(r, axis=0) for r in d_ln_b], axis=0)
    gs_conv_w = stack_rows([gw["dn"][j]["conv_w"] for j in range(n_a)])
    gs_dw_w = stack_rows([gw["cf"][j]["dw_w"] for j in range(n_b)])
    gs_dw_b = stack_rows([gw["cf"][j]["dw_b"] for j in range(n_b)])
    gs_cf_ln_g = stack_rows([gw["cf"][j]["ln_g"] for j in range(n_b)])
    gs_cf_ln_b = stack_rows([gw["cf"][j]["ln_b"] for j in range(n_b)])
    gs_a_log = stack_rows([_pad_lanes(gw["dn"][j]["a_log"], 0)[0] for j in range(n_a)])
    gs_dt_bias = stack_rows([_pad_lanes(gw["dn"][j]["dt_bias"], 0)[0] for j in range(n_a)])
    gs_norm_w = stack_rows([gw["dn"][j]["norm_w"] for j in range(n_a)])
    small_grads = [gs_ln_g, gs_ln_b, gs_conv_w, gs_dw_w, gs_dw_b, gs_cf_ln_g, gs_cf_ln_b, gs_a_log, gs_dt_bias,
                   gs_norm_w, d_mod_full]
    sg_rows = [_rows(a) for a in small_grads]
    sg_sizes = [a.shape[0] for a in sg_rows]
    sg_packed = _pad_rows(jnp.concatenate(sg_rows, axis=0))
    (sg_handle,), sg_token = _xfer_start([sg_packed], [_landing(sg_packed, me)], False, grad_x, "gather_small_grads_start")
    for i in range(depth - 1, -1, -1):
        sg_token = update_layer(i, sg_token)
    sg_all = _xfer_wait(sg_handle, False, sg_token, "gather_small_grads_wait")
    sg_offs = [0]
    for z in sg_sizes:
        sg_offs.append(sg_offs[-1] + z)

    def sg(i, shape):
        return sg_all[:, sg_offs[i]:sg_offs[i + 1], :].reshape((N_DEV,) + shape)

    dmod_all = sg(10, (depth, N_MOD * d))
    nl = ada_w.shape[2]
    dmod_cols = lax.dynamic_slice_in_dim(dmod_all, me * nl, nl, axis=2)
    g_ada_w = _ada_bwd(cond_all, jnp.moveaxis(dmod_cols, 0, 1))

    outs = {}

    def run_adamw(key, parts, w, m, v):
        shp = w.shape
        as3 = lambda t: t.reshape((-1,) + shp[-2:]) if t.ndim >= 3 else t.reshape((1,) + shp)
        parts3 = parts.reshape((parts.shape[0],) + as3(w).shape)
        res = _adamw(parts3, as3(w), as3(m), as3(v), "adamw_" + key)
        outs[key] = tuple(r.reshape(shp) for r in res)

    run_adamw("ada_w", g_ada_w[None], ada_w, m_ada_w, v_ada_w)

    cgroups = 3 * heads // N_DEV
    n_sharded = 7

    def my_cols(first, last, groups):
        x = sg_all[:, sg_offs[first]:sg_offs[last], :].reshape(N_DEV, -1, N_DEV, groups, LANES)
        return lax.dynamic_index_in_dim(x, me, axis=2, keepdims=False).reshape(N_DEV, -1, LANES)

    small_parts = [my_cols(0, 2, 1), my_cols(2, 3, cgroups), my_cols(3, n_sharded, 1),
                   sg_all[:, sg_offs[n_sharded]:sg_offs[-1], :]]
    sp_offs = [0]
    for n, z in enumerate(sg_sizes):
        sp_offs.append(sp_offs[-1] + (z // N_DEV if n < n_sharded else z))
    parts_packed = jnp.zeros((N_DEV, sp_offs[-1] + (-sp_offs[-1]) % 8, LANES), F32)
    at = 0
    for part in small_parts:
        parts_packed = lax.dynamic_update_slice(parts_packed, part, (0, at, 0))
        at += part.shape[1]

    def pad_heads(t):
        return jnp.pad(t, ((0, 0), (0, LANES - heads)))

    def pack_state(ln_g_, ln_b_, conv_w_, dw_w_, dw_b_, cln_g_, cln_b_, a_log_, dt_b_, norm_w_, ada_b_):
        rows = [_rows(ln_g_), _rows(ln_b_), _rows(conv_w_), _rows(dw_w_), _rows(dw_b_), _rows(cln_g_), _rows(cln_b_),
                pad_heads(a_log_), pad_heads(dt_b_), norm_w_, _rows(ada_b_)]
        return _pad_rows(jnp.concatenate(rows, axis=0))

    w_s = pack_state(ln_g, ln_b, dn_conv_w, cf_dw_w, cf_dw_b, cf_ln_g, cf_ln_b, dn_a_log, dn_dt_bias, dn_norm_w, ada_b)
    m_s = pack_state(m_ln_g, m_ln_b, m_dn_conv_w, m_cf_dw_w, m_cf_dw_b, m_cf_ln_g, m_cf_ln_b, m_dn_a_log,
                     m_dn_dt_bias, m_dn_norm_w, m_ada_b)
    v_s = pack_state(v_ln_g, v_ln_b, v_dn_conv_w, v_cf_dw_w, v_cf_dw_b, v_cf_ln_g, v_cf_ln_b, v_dn_a_log,
                     v_dn_dt_bias, v_dn_norm_w, v_ada_b)
    res_s = _adamw(parts_packed[:, None], w_s[None], m_s[None], v_s[None], "adamw_small")
    small_keys = ["ln_g", "ln_b", "dn_conv_w", "cf_dw_w", "cf_dw_b", "cf_ln_g", "cf_ln_b", "dn_a_log", "dn_dt_bias",
                  "dn_norm_w", "ada_b"]
    small_shapes = [ln_g.shape, ln_b.shape, dn_conv_w.shape, cf_dw_w.shape, cf_dw_b.shape, cf_ln_g.shape,
                    cf_ln_b.shape, dn_a_log.shape, dn_dt_bias.shape, dn_norm_w.shape, ada_b.shape]
    for n, (key, shp) in enumerate(zip(small_keys, small_shapes)):
        vals = []
        for r in res_s:
            piece = r[0, sp_offs[n]:sp_offs[n + 1], :]
            if key in ("dn_a_log", "dn_dt_bias"):
                piece = piece[:, :heads]
            vals.append(piece.reshape(shp))
        outs[key] = tuple(vals)

    for key in stacked:
        outs[key] = tuple(chains[key])

    order = ["ada_w", "ada_b", "ln_g", "ln_b", "dn_w_in", "dn_conv_w", "dn_a_log", "dn_dt_bias", "dn_norm_w",
             "dn_w_out", "cf_w_in", "cf_dw_w", "cf_dw_b", "cf_ln_g", "cf_ln_b", "cf_w_out", "ff_w1", "ff_w2"]
    result = [loss, grad_x[None]]
    for part in range(4):
        result += [outs[k][part] for k in order]
    return tuple(result)
```

```python
import functools

import jax
import jax.numpy as jnp
import numpy as np
from jax import lax
from jax.experimental import pallas as pl
from jax.experimental.pallas import tpu as pltpu

F32 = jnp.float32
MXU_DTYPE = jnp.bfloat16
N_DEV = 8
LANES = 128
HEAD_DIM = 128
CHUNK = 64
DN_CONV = 4
N_MOD = 6
LN_EPS = 1e-5
RMS_EPS = 1e-6
L2_EPS = 1e-6
ADAM_LR = 0.001
ADAM_B1 = 0.9
ADAM_B2 = 0.999
ADAM_EPS = 1e-08
ADAM_WD = 0.01
ADAM_STEP = 10

NN = ((1,), (0,))
NT = ((1,), (1,))
TN = ((0,), (0,))

ROW_TILE = 512
CONV_TILE = 256
SHORT_CONV_TILE = 1024


def _mdot(a, b, dims=NN):
    return lax.dot_general(a.astype(MXU_DTYPE), b.astype(MXU_DTYPE), (dims, ((), ())), preferred_element_type=F32)


def _split3(x):
    hi = x.astype(MXU_DTYPE)
    r1 = x - hi.astype(F32)
    mid = r1.astype(MXU_DTYPE)
    lo = (r1 - mid.astype(F32)).astype(MXU_DTYPE)
    return hi, mid, lo


def _dot01(a, b, dims=NN, mask_first=True):
    d = lambda p, q: lax.dot_general(p, q, (dims, ((), ())), preferred_element_type=F32)
    if mask_first:
        m = a.astype(MXU_DTYPE)
        return sum(d(m, p) for p in _split3(b))
    m = b.astype(MXU_DTYPE)
    return sum(d(p, m) for p in _split3(a))


def _cparams(n):
    return pltpu.CompilerParams(dimension_semantics=("arbitrary",) * n)


def _call(body, *, grid, ins, outs, name, scratch=()):
    res = pl.pallas_call(
        body,
        grid=grid,
        in_specs=[pl.BlockSpec(memory_space=pl.ANY) if b is None else pl.BlockSpec(b, m) for _, b, m in ins],
        out_specs=[pl.BlockSpec(b, m) for _, _, b, m in outs],
        out_shape=[jax.ShapeDtypeStruct(s, d) for s, d, _, _ in outs],
        scratch_shapes=list(scratch),
        name=name,
        compiler_params=_cparams(len(grid)),
    )(*[a for a, _, _ in ins])
    return res


def _tile(n, pref, unit=LANES):
    if n <= pref:
        return n
    t = (pref // unit) * unit
    while t > unit and n % t:
        t -= unit
    assert n % t == 0, (n, pref)
    return t


def _rowmap(fn, rows, consts, row_outs, acc_outs, name, pin=None):
    rows = [r if isinstance(r, tuple) else (r, r.shape[1], 0) for r in rows]
    s = rows[0][0].shape[0]
    tm = min(ROW_TILE, s)
    nr, nc, no, na = len(rows), len(consts), len(row_outs), len(acc_outs)
    npin = 0 if pin is None else 1

    def body(*refs):
        rin, cin = refs[:nr], refs[nr:nr + nc]
        refs = refs[:nr + nc] + refs[nr + nc + npin:]
        rout, aout = refs[nr + nc:nr + nc + no], refs[nr + nc + no:]
        ro, ao = fn(*[r[...] for r in rin], *[c[...] for c in cin])
        for ref, val in zip(rout, ro):
            ref[...] = val.astype(ref.dtype)
        if na:
            first = pl.program_id(0) == 0

            @pl.when(first)
            def _():
                for ref, val in zip(aout, ao):
                    ref[...] = val

            @pl.when(jnp.logical_not(first))
            def _():
                for ref, val in zip(aout, ao):
                    ref[...] += val

    ins = [(a, (tm, w), functools.partial(lambda i, cb: (i, cb), cb=cb)) for a, w, cb in rows]
    for c in consts:
        if isinstance(c, tuple):
            ins.append((c[0], (None, 1, c[0].shape[2]), functools.partial(lambda i, n: (n, 0, 0), n=c[1])))
        else:
            ins.append((c, c.shape, lambda i: (0, 0)))
    if pin is not None:
        ins.append((pin, None, None))
    outs = [((s, w), d, (tm, w), lambda i: (i, 0)) for w, d in row_outs]
    outs += [(shp, F32, shp, lambda i: (0, 0)) for shp in acc_outs]
    res = _call(body, grid=(s // tm,), ins=ins, outs=outs, name=name)
    return res[:no], res[no:]


def _ln(z, g, b):
    mu = jnp.mean(z, -1, keepdims=True)
    var = jnp.mean(jnp.square(z - mu), -1, keepdims=True)
    return (z - mu) * lax.rsqrt(var + LN_EPS) * g + b


def _combine(alpha, x, y, gt, g, b, sc, sh):
    xn = _ln(alpha * x + (1.0 + gt) * y, g, b)
    return xn, xn * (1.0 + sc) + sh


def _modulate_fwd(x, sc, sh):
    def fn(x, sc, sh):
        return ((x * (1.0 + sc) + sh),), ()

    (h,), _ = _rowmap(fn, [x], [sc, sh], [(x.shape[1], MXU_DTYPE)], [], "modulate_fwd")
    return h


def _modulate_bwd(x, dx, dh, sc, sh, pin=None):
    d = x.shape[1]

    def fn(x, dx, dh, sc, sh):
        _, vjp = jax.vjp(lambda x, sc, sh: x * (1.0 + sc) + sh, x, sc, sh)
        gx, gsc, gsh = vjp(dh)
        return (dx + gx,), (gsc, gsh)

    (gx,), (gsc, gsh) = _rowmap(fn, [x, dx, dh], [sc, sh], [(d, F32)], [(1, d), (1, d)], "modulate_bwd", pin=pin)
    return gx, gsc, gsh


def _combine_fwd(alpha, x, y, gt, g, b, sc, sh):
    d = x.shape[1]

    def fn(x, y, gt, g, b, sc, sh):
        return _combine(alpha, x, y, gt, g, b, sc, sh), ()

    (xn, h), _ = _rowmap(fn, [x, y], [gt, g, b, sc, sh], [(d, F32), (d, MXU_DTYPE)], [], "combine_fwd")
    return xn, h


def _combine_bwd(alpha, x, y, dxn, dh, gt, g, b, sc, sh, pin=None):
    d = x.shape[1]

    def fn(x, y, dxn, dh, gt, g, b, sc, sh):
        _, vjp = jax.vjp(functools.partial(_combine, alpha), x, y, gt, g, b, sc, sh)
        gx, gy, ggt, gg, gb, gsc, gsh = vjp((dxn, dh))
        return (gx, gy), (ggt, gg, gb, gsc, gsh)

    (gx, gy), accs = _rowmap(fn, [x, y, dxn, dh], [gt, g, b, sc, sh], [(d, F32), (d, MXU_DTYPE)],
                             [(1, d)] * 5, "combine_bwd", pin=pin)
    return gx, gy, accs


def _last_fwd_bwd(alpha, x, y, tgt, gt, g, b):
    d = x.shape[1]

    def fn(x, y, tgt, gt, g, b):
        xn, vjp = jax.vjp(lambda x, y, gt, g, b: _ln(alpha * x + (1.0 + gt) * y, g, b), x, y, gt, g, b)
        err = xn - tgt
        gx, gy, ggt, gg, gb = vjp(err * (1.0 / d))
        rows = jnp.sum(jnp.square(err), axis=-1, keepdims=True)
        loss = (0.5 / d) * jnp.sum(rows, axis=0, keepdims=True) * jnp.ones((1, LANES), F32)
        return (gx, gy), (loss, ggt, gg, gb)

    (gx, gy), accs = _rowmap(fn, [x, y, tgt], [gt, g, b], [(d, F32), (d, MXU_DTYPE)],
                             [(1, LANES), (1, d), (1, d), (1, d)], "last_fwd_bwd")
    return gx, gy, accs


MM_VMEM_BUDGET = 40 * 2 ** 20


def _fit(options, cost):
    for o in options:
        if 2 * cost(o) <= MM_VMEM_BUDGET:
            return o
    return options[-1]


def _row_tiles(m):
    return [t for t in (2048, 1024, 512, 256) if t <= m and m % t == 0] or [m]


def _mm_call(a, a_blk, a_map, b, b_blk, b_map, outs, dims, grid, name, epi=None, extra=None, split=None, blocks=None,
             pin=None):
    nk = grid[2]
    n_out = len(outs)
    n_in = 2 + (extra is not None) + (pin is not None)

    def body(*refs):
        a_ref, b_ref = refs[0], refs[1]
        rest = refs[n_in:]
        out_refs = rest[:n_out]

        def finish(val):
            if epi == "relu2":
                out_refs[0][...] = jnp.square(jnp.maximum(val, 0.0)).astype(out_refs[0].dtype)
            elif epi == "relu2_bwd":
                sq = refs[2][...].astype(F32)
                root = jnp.where(sq > 0.0, sq * lax.rsqrt(sq), 0.0)
                out_refs[0][...] = (val * 2.0 * root).astype(out_refs[0].dtype)
            elif split is not None:
                for g in range(split[0]):
                    out_refs[0][g] = val[:, g * split[1]:(g + 1) * split[1]].astype(out_refs[0].dtype)
            else:
                out_refs[0][...] = val.astype(out_refs[0].dtype)

        if blocks is None:
            p = lax.dot_general(a_ref[...], b_ref[...], (dims, ((), ())), preferred_element_type=F32)
        else:
            p = None
            for g in range(blocks[0]):
                part = lax.dot_general(a_ref[:, g * blocks[1]:(g + 1) * blocks[1]], b_ref[g], (dims, ((), ())),
                                       preferred_element_type=F32)
                p = part if p is None else p + part
        if nk == 1:
            finish(p)
        else:
            acc = rest[n_out]
            k = pl.program_id(2)

            @pl.when(k == 0)
            def _():
                acc[...] = p

            @pl.when(k > 0)
            def _():
                acc[...] += p

            @pl.when(k == nk - 1)
            def _():
                finish(acc[...])

    if nk > 1:
        out_blk = tuple(x for x in outs[0][2] if x is not None)
        if split is not None:
            out_blk = (out_blk[1], split[0] * split[1])
        scratch = [pltpu.VMEM(out_blk, F32)]
    else:
        scratch = []
    ins = [(a, a_blk, a_map), (b, b_blk, b_map)] + ([extra] if extra is not None else [])
    ins += [(pin, None, None)] if pin is not None else []
    return _call(body, grid=grid, ins=ins, outs=outs, name=name, scratch=scratch)


def _isz(dt):
    return jnp.dtype(dt).itemsize


def _mm_nn(a, b, out_dtype, name, relu2=False):
    m, kdim = a.shape
    if b.ndim == 2:
        n = b.shape[1]
        tn = _tile(n, 1536 if n > 2048 else 512)
        b_blk, b_map = (kdim, tn), lambda i, j, k: (0, j)
    else:
        g, _, ng = b.shape
        n = g * ng
        tn = _tile(ng, 512)
        b_blk = (None, kdim, tn)
        b_map = functools.partial(lambda i, j, k, npg: (j // npg, 0, j % npg), npg=ng // tn)
    tm = _fit(_row_tiles(m), lambda t: t * kdim * _isz(a.dtype) + kdim * tn * _isz(b.dtype) + t * tn * _isz(out_dtype))
    grid = (m // tm, n // tn, 1)
    outs = [((m, n), out_dtype, (tm, tn), lambda i, j, k: (i, j))]
    return _mm_call(a, (tm, kdim), lambda i, j, k: (i, 0), b, b_blk, b_map, outs, NN, grid, name,
                    epi="relu2" if relu2 else None)[0]


def _mm_nt(a, b, out_dtype, name, relu2_sq=None, pin=None):
    m, n = a.shape
    extra_bytes = _isz(relu2_sq.dtype) if relu2_sq is not None else 0
    if b.ndim == 2:
        kout = b.shape[0]
        to = _tile(kout, 512)
        b_blk, b_map, blocks = (to, n), lambda i, j, k: (j, 0), None
    else:
        g, kout, ng = b.shape
        to = _tile(kout, 512)
        b_blk, b_map, blocks = (g, to, ng), lambda i, j, k: (0, j, 0), (g, ng)
    tm = _fit(_row_tiles(m), lambda t: t * n * _isz(a.dtype) + to * n * _isz(b.dtype)
              + t * to * (_isz(out_dtype) + extra_bytes))
    grid = (m // tm, kout // to, 1)
    outs = [((m, kout), out_dtype, (tm, to), lambda i, j, k: (i, j))]
    extra = (relu2_sq, (tm, to), lambda i, j, k: (i, j)) if relu2_sq is not None else None
    return _mm_call(a, (tm, n), lambda i, j, k: (i, 0), b, b_blk, b_map, outs, NT, grid, name,
                    epi="relu2_bwd" if relu2_sq is not None else None, extra=extra, blocks=blocks, pin=pin)[0]


def _mm_tn(a, b, out_dtype, name, split_cols=False):
    m, kdim = a.shape
    n = b.shape[1]
    tk = _tile(kdim, 512)
    tn = _tile(n, 1536)
    if not split_cols:
        out, split = ((kdim, n), out_dtype, (tk, tn), lambda i, j, k: (i, j)), None
    else:
        ng = n // N_DEV
        if tn % ng:
            tn = _tile(ng, 512)
        if tn >= ng:
            gb = tn // ng
            out = ((N_DEV, kdim, ng), out_dtype, (gb, tk, ng), lambda i, j, k: (j, i, 0))
            split = (gb, ng)
        else:
            out = ((N_DEV, kdim, ng), out_dtype, (None, tk, tn),
                   functools.partial(lambda i, j, k, npg: (j // npg, i, j % npg), npg=ng // tn))
            split = None
    grid = (kdim // tk, n // tn, 1)
    return _mm_call(a, (m, tk), lambda i, j, k: (0, i), b, (m, tn), lambda i, j, k: (0, j), [out], TN, grid, name,
                    split=split)[0]


def _shifted(xa, off, rows):
    if off % 8 == 0:
        return xa[off:off + rows]
    return pltpu.roll(xa, xa.shape[0] - off, 0)[:rows]


def _conv_pad(taps):
    return -(-(taps - 1) // 8) * 8


def _conv_tile(xp_ref, w, i, rows, taps):
    pad = _conv_pad(taps)
    r0 = pl.multiple_of(i * rows, rows)
    xa = xp_ref[pl.ds(r0, rows + pad), :]
    views = [_shifted(xa, pad - (taps - 1) + j, rows) for j in range(taps)]
    acc = w[0:1, :] * views[0]
    for j in range(1, taps):
        acc = acc + w[j:j + 1, :] * views[j]
    return r0, acc, views


def _conv_back_tile(yp_ref, w, i, rows, taps):
    pad = _conv_pad(taps)
    r0 = pl.multiple_of(i * rows, rows)
    ya = yp_ref[pl.ds(r0, rows + pad), :]
    acc = w[taps - 1:taps, :] * ya[:rows]
    for j in range(taps - 1):
        acc = acc + w[j:j + 1, :] * _shifted(ya, taps - 1 - j, rows)
    return r0, acc


def _tap_sums(dy, views, taps):
    row = lax.broadcasted_iota(jnp.int32, (taps, LANES), 0)
    acc = jnp.zeros((taps, LANES), F32)
    for j in range(taps):
        acc = acc + jnp.where(row == j, jnp.sum(dy * views[j], axis=0, keepdims=True), 0.0)
    return acc


def _silu_l2(xc, l2):
    a = jax.nn.silu(xc)
    if l2:
        a = a * lax.rsqrt(jnp.sum(a * a, axis=-1, keepdims=True) + L2_EPS)
    return a


def _dn_conv_fwd(proj, conv_w, c0, nblk, l2, name):
    s = proj.shape[0]
    pad = _conv_pad(DN_CONV)
    rows = min(SHORT_CONV_TILE, s)

    def body(x_ref, w_ref, o_ref, xp):
        xp[0:pad, :] = jnp.zeros((pad, LANES), F32)
        xp[pad:, :] = x_ref[...]
        w = w_ref[...]

        def tile(i, c):
            r0, acc, _ = _conv_tile(xp, w, i, rows, DN_CONV)
            o_ref[pl.ds(r0, rows), :] = _silu_l2(acc, l2)
            return c

        lax.fori_loop(0, s // rows, tile, 0)

    return _call(body, grid=(nblk,),
                 ins=[(proj, (s, LANES), lambda c: (0, c0 + c)), (conv_w, (DN_CONV, LANES), lambda c: (0, c0 + c))],
                 outs=[((nblk, s, LANES), F32, (None, s, LANES), lambda c: (c, 0, 0))],
                 name=name, scratch=[pltpu.VMEM((s + pad, LANES), F32)])[0]


def _dn_conv_bwd(proj, conv_w, da, c0, nblk, l2, name):
    s = proj.shape[0]
    pad = _conv_pad(DN_CONV)
    rows = min(SHORT_CONV_TILE, s)

    def body(x_ref, w_ref, da_ref, dx_ref, dw_ref, xp, yp):
        xp[0:pad, :] = jnp.zeros((pad, LANES), F32)
        xp[pad:, :] = x_ref[...]
        yp[s:, :] = jnp.zeros((pad, LANES), F32)
        w = w_ref[...]

        def tile(i, dw):
            r0, acc, views = _conv_tile(xp, w, i, rows, DN_CONV)
            _, vjp = jax.vjp(functools.partial(_silu_l2, l2=l2), acc)
            (dxc,) = vjp(da_ref[pl.ds(r0, rows), :])
            yp[pl.ds(r0, rows), :] = dxc
            return dw + _tap_sums(dxc, views, DN_CONV)

        dw_ref[...] = lax.fori_loop(0, s // rows, tile, jnp.zeros((DN_CONV, LANES), F32))

        def tile2(i, c):
            r0, acc = _conv_back_tile(yp, w, i, rows, DN_CONV)
            dx_ref[pl.ds(r0, rows), :] = acc.astype(dx_ref.dtype)
            return c

        lax.fori_loop(0, s // rows, tile2, 0)

    return _call(body, grid=(nblk,),
                 ins=[(proj, (s, LANES), lambda c: (0, c0 + c)), (conv_w, (DN_CONV, LANES), lambda c: (0, c0 + c)),
                      (da, (None, s, LANES), lambda c: (c, 0, 0))],
                 outs=[((s, nblk * LANES), MXU_DTYPE, (s, LANES), lambda c: (0, c)),
                       ((DN_CONV, nblk * LANES), F32, (DN_CONV, LANES), lambda c: (0, c))],
                 name=name, scratch=[pltpu.VMEM((s + pad, LANES), F32), pltpu.VMEM((s + pad, LANES), F32)])


def _cf_conv_fwd(vg, dw_w, dw_b):
    s, c2 = vg.shape
    ch = c2 // 2
    nblk = ch // LANES
    taps = dw_w.shape[0]
    pad = _conv_pad(taps)
    rows = min(CONV_TILE, s)

    def body(v_ref, g_ref, w_ref, b_ref, o_ref, xp):
        xp[0:pad, :] = jnp.zeros((pad, LANES), F32)
        xp[pad:, :] = v_ref[...] * jax.nn.sigmoid(g_ref[...])
        w = w_ref[...]
        bias = b_ref[...]

        def tile(i, c):
            r0, acc, _ = _conv_tile(xp, w, i, rows, taps)
            o_ref[pl.ds(r0, rows), :] = acc + bias
            return c

        lax.fori_loop(0, s // rows, tile, 0)

    return _call(body, grid=(nblk,),
                 ins=[(vg, (s, LANES), lambda c: (0, c)), (vg, (s, LANES), lambda c: (0, nblk + c)),
                      (dw_w, (taps, LANES), lambda c: (0, c)), (dw_b, (1, LANES), lambda c: (0, c))],
                 outs=[((s, ch), F32, (s, LANES), lambda c: (0, c))],
                 name="cf_conv_fwd", scratch=[pltpu.VMEM((s + pad, LANES), F32)])[0]


def _cf_conv_bwd(vg, dw_w, du):
    s, c2 = vg.shape
    ch = c2 // 2
    nblk = ch // LANES
    taps = dw_w.shape[0]
    pad = _conv_pad(taps)
    rows = min(CONV_TILE, s)

    def body(v_ref, g_ref, w_ref, du_ref, dv_ref, dg_ref, dw_ref, db_ref, xp, yp):
        sig = jax.nn.sigmoid(g_ref[...])
        xp[0:pad, :] = jnp.zeros((pad, LANES), F32)
        xp[pad:, :] = v_ref[...] * sig
        yp[0:s, :] = du_ref[...]
        yp[s:, :] = jnp.zeros((pad, LANES), F32)
        w = w_ref[...]
        db_ref[...] = jnp.sum(du_ref[...], axis=0, keepdims=True)

        def tile(i, dw):
            r0, _, views = _conv_tile(xp, w, i, rows, taps)
            return dw + _tap_sums(du_ref[pl.ds(r0, rows), :], views, taps)

        dw_ref[...] = lax.fori_loop(0, s // rows, tile, jnp.zeros((taps, LANES), F32))

        def tile2(i, c):
            r0, du0 = _conv_back_tile(yp, w, i, rows, taps)
            val = v_ref[pl.ds(r0, rows), :]
            sg = jax.nn.sigmoid(g_ref[pl.ds(r0, rows), :])
            dv_ref[pl.ds(r0, rows), :] = (du0 * sg).astype(dv_ref.dtype)
            dg_ref[pl.ds(r0, rows), :] = (du0 * val * sg * (1.0 - sg)).astype(dg_ref.dtype)
            return c

        lax.fori_loop(0, s // rows, tile2, 0)

    return _call(body, grid=(nblk,),
                 ins=[(vg, (s, LANES), lambda c: (0, c)), (vg, (s, LANES), lambda c: (0, nblk + c)),
                      (dw_w, (taps, LANES), lambda c: (0, c)), (du, (s, LANES), lambda c: (0, c))],
                 outs=[((s, ch), MXU_DTYPE, (s, LANES), lambda c: (0, c)),
                       ((s, ch), MXU_DTYPE, (s, LANES), lambda c: (0, c)),
                       ((taps, ch), F32, (taps, LANES), lambda c: (0, c)),
                       ((1, ch), F32, (1, LANES), lambda c: (0, c))],
                 name="cf_conv_bwd", scratch=[pltpu.VMEM((s + pad, LANES), F32), pltpu.VMEM((s + pad, LANES), F32)])


def _masks():
    r = lax.broadcasted_iota(jnp.int32, (CHUNK, CHUNK), 0)
    c = lax.broadcasted_iota(jnp.int32, (CHUNK, CHUNK), 1)
    return r >= c, r > c, r <= c


def _chunk_decay(g):
    causal, _, upper = _masks()
    gb = jnp.broadcast_to(g, (CHUNK, CHUNK))
    gam_r = _dot01(jnp.where(causal, 1.0, 0.0), gb)
    gam_s = _dot01(jnp.ones((CHUNK, CHUNK), F32), jnp.where(upper, gb, 0.0))
    dm = jnp.where(causal, jnp.exp(jnp.where(causal, gam_r - gam_s, 0.0)), 0.0)
    return gam_r[:, 0:1], dm


def _chunk_scores(q, k, beta, dm):
    _, strict, _ = _masks()
    both = _mdot(jnp.concatenate([k * beta, q * (HEAD_DIM ** -0.5)], axis=0), k, NT)
    return jnp.where(strict, both[:CHUNK] * dm, 0.0), both[CHUNK:] * dm


def _lockstep(gens):
    results = [None] * len(gens)
    alive = list(range(len(gens)))
    while alive:
        for i in list(alive):
            try:
                next(gens[i])
            except StopIteration as stop:
                results[i] = stop.value
                alive.remove(i)
    return results


def _chunk_prep_bwd(q, k, v, beta, gam, t, du, dw, daqk, dqd, dkd, dgl):
    causal, strict, _ = _masks()
    r = lax.broadcasted_iota(jnp.int32, (CHUNK, CHUNK), 0)
    c = lax.broadcasted_iota(jnp.int32, (CHUNK, CHUNK), 1)
    scale = HEAD_DIM ** -0.5
    eg = jnp.exp(gam)
    gam_last = gam[CHUNK - 1:CHUNK, :]
    rr = jnp.exp(gam_last - gam)
    kb = k * beta
    qs = q * scale
    vb = v * beta
    kbe = kb * eg
    gam_b = jnp.broadcast_to(gam, (CHUNK, CHUNK))
    gam_s = _dot01(jnp.ones((CHUNK, CHUNK), F32), jnp.where(r == c, gam_b, 0.0))
    both = _mdot(jnp.concatenate([kb, qs], axis=0), k, NT)
    duw = jnp.concatenate([du, dw], axis=1)
    dt = _mdot(duw, jnp.concatenate([vb, kbe], axis=1), NT)
    dvk = _mdot(t, duw, TN)
    yield
    dm = jnp.where(causal, jnp.exp(jnp.where(causal, gam_b - gam_s, 0.0)), 0.0)
    a = jnp.where(strict, both[:CHUNK] * dm, 0.0)
    aqk = both[CHUNK:] * dm
    dvb, dkbe = dvk[:, :HEAD_DIM], dvk[:, HEAD_DIM:]
    x = _mdot(t, dt, TN)
    yield
    da = jnp.where(strict, -_mdot(x, t, NT), 0.0)
    yield
    dkk = da * dm
    dqk = daqk * dm
    ddiff = da * a + daqk * aqk
    dboth = jnp.concatenate([dkk, dqk], axis=0)
    dkq = _mdot(dboth, k)
    dk_mm = _mdot(dboth, jnp.concatenate([kb, qs], axis=0), TN)
    colsum = _dot01(ddiff, jnp.ones((CHUNK, LANES), F32), TN, mask_first=False)[:, 0:1]
    yield
    dkb = dkq[:CHUNK] + dkbe * eg
    dk = dk_mm + dkb * beta + dkd * rr
    dq = (dkq[CHUNK:] + dqd * eg) * scale
    dbeta = jnp.sum(dkb * k, axis=-1, keepdims=True) + jnp.sum(dvb * v, axis=-1, keepdims=True)
    dv = dvb * beta
    deg = jnp.sum(dkbe * kb, axis=-1, keepdims=True) + jnp.sum(dqd * qs, axis=-1, keepdims=True)
    drr = jnp.sum(dkd * k, axis=-1, keepdims=True)
    dgam = deg * eg - drr * rr + jnp.sum(ddiff, axis=-1, keepdims=True) - colsum
    dgam_last = jnp.sum(drr * rr, axis=0, keepdims=True) + dgl[0:1, :] * jnp.exp(gam_last)
    row = lax.broadcasted_iota(jnp.int32, (CHUNK, 1), 0)
    dgam = dgam + jnp.where(row == CHUNK - 1, dgam_last, 0.0)
    dg = _dot01(jnp.where(causal, 1.0, 0.0), jnp.broadcast_to(dgam, (CHUNK, LANES)), TN)[:, 0:1]
    return dq, dk, dv, dbeta, dg


def _prep_group(s):
    nch = s // CHUNK
    return next(c for c in (16, 8, 4, 2, 1) if nch % c == 0)


def _tri_solve_lanes(a_l):
    n = a_l.shape[1]
    group = 8

    def body(a_ref, t_ref):
        t_ref[...] = jnp.zeros_like(t_ref)
        col = lax.broadcasted_iota(jnp.int32, (CHUNK, n), 0)

        def row(r, carry):
            r0 = pl.multiple_of(r * CHUNK, CHUNK)

            def inner(sg, acc):
                a8 = a_ref[pl.ds(r0 + pl.multiple_of(sg * group, group), group), :]
                for j in range(group):
                    t0 = pl.multiple_of((sg * group + j) * CHUNK, CHUNK)
                    acc = acc + a8[j:j + 1, :] * t_ref[pl.ds(t0, CHUNK), :]
                return acc

            acc = lax.fori_loop(0, (r + group - 1) // group, inner, jnp.zeros((CHUNK, n), F32))
            t_ref[pl.ds(r0, CHUNK), :] = jnp.where(col == r, 1.0, 0.0) - acc
            return carry

        lax.fori_loop(0, CHUNK, row, 0)

    return pl.pallas_call(body, out_shape=jax.ShapeDtypeStruct(a_l.shape, F32), name="dn_tri_solve")(a_l)


def _head_cols(bg, hh, heads):
    lane = lax.broadcasted_iota(jnp.int32, bg.shape, 1)
    beta = jnp.sum(jnp.where(lane == hh, bg, 0.0), axis=-1, keepdims=True)
    g = jnp.sum(jnp.where(lane == heads + hh, bg, 0.0), axis=-1, keepdims=True)
    return beta, g


def _dn_prep(q, k, v, bg):
    h, s, _ = q.shape
    cb = _prep_group(s)
    rb = cb * CHUNK
    big = lambda x: (x, (None, rb, HEAD_DIM), lambda n, hh: (hh, n, 0))
    sq = lambda x: (x, (None, rb, CHUNK), lambda n, hh: (hh, n, 0))
    col = lambda x: (x, (None, rb, 1), lambda n, hh: (hh, n, 0))
    tok = (bg, (rb, LANES), lambda n, hh: (n, 0))
    o_big = ((h, s, HEAD_DIM), F32, (None, rb, HEAD_DIM), lambda n, hh: (hh, n, 0))
    o_sq = ((h, s, CHUNK), F32, (None, rb, CHUNK), lambda n, hh: (hh, n, 0))
    o_col = ((h, s, 1), F32, (None, rb, 1), lambda n, hh: (hh, n, 0))

    def scores(q_ref, k_ref, bg_ref, a_ref, aqk_ref, gam_ref):
        beta, g = _head_cols(bg_ref[...], pl.program_id(1), h)
        for i in range(cb):
            sl = slice(i * CHUNK, (i + 1) * CHUNK)
            gam, dm = _chunk_decay(g[sl])
            a_ref[sl, :], aqk_ref[sl, :] = _chunk_scores(q_ref[sl, :], k_ref[sl, :], beta[sl], dm)
            gam_ref[sl, :] = gam

    a, aqk, gam = _call(scores, grid=(s // rb, h), ins=[big(q), big(k), tok], outs=[o_sq, o_sq, o_col],
                        name="dn_scores")
    n_prob = h * (s // CHUNK)
    t_l = _tri_solve_lanes(jnp.transpose(a.reshape(n_prob, CHUNK * CHUNK)))
    t = jnp.transpose(t_l).reshape(h, s, CHUNK)

    def wy(k_ref, v_ref, bg_ref, gam_ref, t_ref, u_ref, w_ref):
        beta, _ = _head_cols(bg_ref[...], pl.program_id(1), h)
        for i in range(cb):
            sl = slice(i * CHUNK, (i + 1) * CHUNK)
            kb = k_ref[sl, :] * beta[sl]
            rhs = jnp.concatenate([v_ref[sl, :] * beta[sl], kb * jnp.exp(gam_ref[sl, :])], axis=1)
            uw = _mdot(t_ref[sl, :], rhs)
            u_ref[sl, :] = uw[:, :HEAD_DIM]
            w_ref[sl, :] = uw[:, HEAD_DIM:]

    u, w = _call(wy, grid=(s // rb, h), ins=[big(k), big(v), tok, col(gam), sq(t)], outs=[o_big, o_big],
                 name="dn_wy")
    return u, w, aqk, t, gam


def _dn_prep_bwd(q, k, v, bg, gam, t, du, dw, daqk, dqd, dkd, dgl):
    h, s, _ = q.shape
    cb = _prep_group(s)
    rb = cb * CHUNK

    def body(q_ref, k_ref, v_ref, bg_ref, g_ref, t_ref, du_ref, dw_ref, da_ref, dqd_ref, dkd_ref, dgl_ref,
             dq_ref, dk_ref, dv_ref, dbg_ref):
        hh = pl.program_id(1)
        beta, _ = _head_cols(bg_ref[...], hh, h)
        slices = [slice(i * CHUNK, (i + 1) * CHUNK) for i in range(cb)]
        results = _lockstep([_chunk_prep_bwd(
            q_ref[sl, :], k_ref[sl, :], v_ref[sl, :], beta[sl], g_ref[sl, :], t_ref[sl, :],
            du_ref[sl, :], dw_ref[sl, :], da_ref[sl, :], dqd_ref[sl, :], dkd_ref[sl, :], dgl_ref[sl, :])
            for sl in slices])

        @pl.when(hh == 0)
        def _():
            dbg_ref[...] = jnp.zeros_like(dbg_ref)

        lane = lax.broadcasted_iota(jnp.int32, (CHUNK, LANES), 1)
        for sl, (dq, dk, dv, dbeta, dg) in zip(slices, results):
            dq_ref[sl, :] = dq
            dk_ref[sl, :] = dk
            dv_ref[sl, :] = dv
            dbg_ref[sl, :] += jnp.where(lane == hh, dbeta, 0.0) + jnp.where(lane == h + hh, dg, 0.0)

    big = lambda x: (x, (None, rb, HEAD_DIM), lambda n, hh: (hh, n, 0))
    sq = lambda x: (x, (None, rb, CHUNK), lambda n, hh: (hh, n, 0))
    col = lambda x: (x, (None, rb, 1), lambda n, hh: (hh, n, 0))
    tok = (bg, (rb, LANES), lambda n, hh: (n, 0))
    o_big = ((h, s, HEAD_DIM), F32, (None, rb, HEAD_DIM), lambda n, hh: (hh, n, 0))
    return _call(body, grid=(s // rb, h),
                 ins=[big(q), big(k), big(v), tok, col(gam), sq(t), big(du), big(dw), sq(daqk), big(dqd), big(dkd),
                      col(dgl)],
                 outs=[o_big, o_big, o_big, ((s, LANES), F32, (rb, LANES), lambda n, hh: (n, 0))], name="dn_prep_bwd")


def _chunk_scaled(q, k, gam):
    gam_last = gam[CHUNK - 1:CHUNK, :]
    q_dec = q * (HEAD_DIM ** -0.5) * jnp.exp(gam)
    k_dec = k * jnp.exp(gam_last - gam)
    return q_dec, k_dec, jnp.exp(gam_last)


def _scan_group(s):
    return 2 if (s // CHUNK) % 2 == 0 else 1


def _dn_scan(q, k, u, w, aqk, gam):
    h, s, _ = q.shape
    nch = s // CHUNK
    sg = _scan_group(s)
    rb = sg * CHUNK

    def body(q_ref, k_ref, u_ref, w_ref, a_ref, gam_ref, o_ref, st_ref, state):
        @pl.when(pl.program_id(0) == 0)
        def _():
            state[...] = jnp.zeros_like(state)

        def head(hh, c):
            sl = slice(c * CHUNK, (c + 1) * CHUNK)
            s0 = state[hh]
            st_ref[c, hh] = s0
            q_dec, k_dec, gl = _chunk_scaled(q_ref[hh, sl, :], k_ref[hh, sl, :], gam_ref[hh, sl, :])
            both = _mdot(jnp.concatenate([w_ref[hh, sl, :], q_dec], axis=0), s0)
            yield
            v_new = u_ref[hh, sl, :] - both[:CHUNK]
            o_ref[sl, hh * HEAD_DIM:(hh + 1) * HEAD_DIM] = both[CHUNK:] + _mdot(a_ref[hh, sl, :], v_new)
            state[hh] = s0 * gl + _mdot(k_dec, v_new, TN)

        for c in range(sg):
            _lockstep([head(hh, c) for hh in range(h)])

    big = lambda x: (x, (h, rb, HEAD_DIM), lambda n: (0, n, 0))
    return _call(body, grid=(nch // sg,),
                 ins=[big(q), big(k), big(u), big(w), (aqk, (h, rb, CHUNK), lambda n: (0, n, 0)),
                      (gam, (h, rb, 1), lambda n: (0, n, 0))],
                 outs=[((s, h * HEAD_DIM), F32, (rb, h * HEAD_DIM), lambda n: (n, 0)),
                       ((nch, h, HEAD_DIM, HEAD_DIM), F32, (sg, h, HEAD_DIM, HEAD_DIM), lambda n: (n, 0, 0, 0))],
                 name="dn_scan", scratch=[pltpu.VMEM((h, HEAD_DIM, HEAD_DIM), F32)])


def _dn_scan_bwd(q, k, u, w, aqk, gam, states, do):
    h, s, _ = q.shape
    nch = s // CHUNK
    sg = _scan_group(s)
    rb = sg * CHUNK
    ngr = nch // sg

    def body(q_ref, k_ref, u_ref, w_ref, a_ref, gam_ref, st_ref, do_ref,
             du_ref, dw_ref, da_ref, dqd_ref, dkd_ref, dgl_ref, dstate):
        @pl.when(pl.program_id(0) == 0)
        def _():
            dstate[...] = jnp.zeros_like(dstate)

        def head(hh, c):
            sl = slice(c * CHUNK, (c + 1) * CHUNK)
            s0 = st_ref[c, hh]
            ds = dstate[hh]
            doh = do_ref[sl, hh * HEAD_DIM:(hh + 1) * HEAD_DIM]
            wv = w_ref[hh, sl, :]
            q_dec, k_dec, gl = _chunk_scaled(q_ref[hh, sl, :], k_ref[hh, sl, :], gam_ref[hh, sl, :])
            ws = _mdot(wv, s0)
            dv_new = _mdot(a_ref[hh, sl, :], doh, TN) + _mdot(k_dec, ds)
            dqd_ref[hh, sl, :] = _mdot(doh, s0, NT)
            qdo = _mdot(q_dec, doh, TN)
            tot = jnp.sum(jnp.sum(s0 * ds, axis=-1, keepdims=True), axis=0, keepdims=True)
            dgl_ref[hh, sl, :] = jnp.broadcast_to(tot, (CHUNK, 1))
            yield
            v_new = u_ref[hh, sl, :] - ws
            du_ref[hh, sl, :] = dv_new
            dw_ref[hh, sl, :] = -_mdot(dv_new, s0, NT)
            da_ref[hh, sl, :] = _mdot(doh, v_new, NT)
            dkd_ref[hh, sl, :] = _mdot(v_new, ds, NT)
            dstate[hh] = ds * gl + qdo - _mdot(wv, dv_new, TN)

        for c in range(sg - 1, -1, -1):
            _lockstep([head(hh, c) for hh in range(h)])

    rev = lambda n: (0, ngr - 1 - n, 0)
    big = lambda x: (x, (h, rb, HEAD_DIM), rev)
    o_big = ((h, s, HEAD_DIM), F32, (h, rb, HEAD_DIM), rev)
    return _call(body, grid=(ngr,),
                 ins=[big(q), big(k), big(u), big(w), (aqk, (h, rb, CHUNK), rev), (gam, (h, rb, 1), rev),
                      (states, (sg, h, HEAD_DIM, HEAD_DIM), lambda n: (ngr - 1 - n, 0, 0, 0)),
                      (do, (rb, h * HEAD_DIM), lambda n: (ngr - 1 - n, 0))],
                 outs=[o_big, o_big, ((h, s, CHUNK), F32, (h, rb, CHUNK), rev), o_big, o_big,
                       ((h, s, 1), F32, (h, rb, 1), rev)],
                 name="dn_scan_bwd", scratch=[pltpu.VMEM((h, HEAD_DIM, HEAD_DIM), F32)])


def _gates(x, a_log, dt_b, h):
    lane = lax.broadcasted_iota(jnp.int32, x.shape, 1)
    return jnp.where(lane < h, jax.nn.sigmoid(x), -jnp.exp(a_log) * jax.nn.softplus(x + dt_b))


def _head_out(oh, zh, nw):
    on = oh * lax.rsqrt(jnp.mean(oh * oh, axis=-1, keepdims=True) + RMS_EPS) * nw
    return on * jax.nn.silu(zh)


def _pad_lanes(x, lo):
    return jnp.zeros((1, LANES), F32).at[0, lo:lo + x.shape[0]].set(x)


def _deltanet_fwd(hin, get_w_in, conv_w, a_log, dt_bias, norm_w, get_w_out):
    h = a_log.shape[0]
    hw = h * HEAD_DIM
    w_in = get_w_in(hin)
    proj = _mm_nn(hin, w_in, F32, "dn_proj")
    q = _dn_conv_fwd(proj, conv_w, 0, h, True, "dn_conv_q")
    k = _dn_conv_fwd(proj, conv_w, h, h, True, "dn_conv_k")
    v = _dn_conv_fwd(proj, conv_w, 2 * h, h, False, "dn_conv_v")
    alp, dtp = _pad_lanes(a_log, h), _pad_lanes(dt_bias, h)

    def gates_fn(x, al, db):
        return (_gates(x, al, db, h),), ()

    (bg,), _ = _rowmap(gates_fn, [(proj, LANES, 4 * h)], [alp, dtp], [(LANES, F32)], [], "dn_gates")
    u, w, aqk, t, gam = _dn_prep(q, k, v, bg)
    o, states = _dn_scan(q, k, u, w, aqk, gam)
    nw = norm_w[None, :]

    def out_fn(o, z, nw):
        parts = [_head_out(o[:, i * HEAD_DIM:(i + 1) * HEAD_DIM], z[:, i * HEAD_DIM:(i + 1) * HEAD_DIM], nw)
                 for i in range(h)]
        return (jnp.concatenate(parts, axis=-1),), ()

    (og,), _ = _rowmap(out_fn, [o, (proj, hw, 3)], [nw], [(hw, MXU_DTYPE)], [], "dn_out")
    w_out = get_w_out(og)
    y = _mm_nn(og, w_out, F32, "dn_y")
    return y, (hin, proj, q, k, v, bg, u, w, aqk, t, gam, states, o, og, alp, dtp, nw, w_in, w_out)


def _deltanet_bwd(res, dy, conv_w, send):
    hin, proj, q, k, v, bg, u, w, aqk, t, gam, states, o, og, alp, dtp, nw, w_in, w_out = res
    h = q.shape[0]
    hw = h * HEAD_DIM
    s = hin.shape[0]
    d_w_out = _mm_tn(og, dy, MXU_DTYPE, "dn_dwout")
    dog = _mm_nt(dy, w_out, F32, "dn_dog")

    def out_bwd(o, z, dog, nw):
        dos, dzs = [], []
        dn = jnp.zeros((1, HEAD_DIM), F32)
        for i in range(h):
            sl = slice(i * HEAD_DIM, (i + 1) * HEAD_DIM)
            _, vjp = jax.vjp(_head_out, o[:, sl], z[:, sl], nw)
            a, b, c = vjp(dog[:, sl])
            dos.append(a)
            dzs.append(b)
            dn = dn + c
        return (jnp.concatenate(dos, axis=-1), jnp.concatenate(dzs, axis=-1)), (dn,)

    (do, dz), (d_norm_w,) = _rowmap(out_bwd, [o, (proj, hw, 3), dog], [nw], [(hw, F32), (hw, MXU_DTYPE)],
                                    [(1, HEAD_DIM)], "dn_out_bwd")
    du, dw, daqk, dqd, dkd, dgl = _dn_scan_bwd(q, k, u, w, aqk, gam, states, do)
    dq, dk, dv, dbg = _dn_prep_bwd(q, k, v, bg, gam, t, du, dw, daqk, dqd, dkd, dgl)
    dpq, dwq = _dn_conv_bwd(proj, conv_w, dq, 0, h, True, "dn_conv_q_bwd")
    dpk, dwk = _dn_conv_bwd(proj, conv_w, dk, h, h, True, "dn_conv_k_bwd")
    dpv, dwv = _dn_conv_bwd(proj, conv_w, dv, 2 * h, h, False, "dn_conv_v_bwd")

    def gates_bwd(x, dbg, al, db):
        _, vjp = jax.vjp(functools.partial(_gates, h=h), x, al, db)
        gx, gal, gdb = vjp(dbg)
        return (gx,), (gal, gdb)

    (dba,), (d_alp, d_dtp) = _rowmap(gates_bwd, [(proj, LANES, 4 * h), dbg], [alp, dtp], [(LANES, MXU_DTYPE)],
                                     [(1, LANES), (1, LANES)], "dn_gates_bwd")
    dproj = jnp.concatenate([dpq, dpk, dpv, dz, dba], axis=1)
    d_w_in = _mm_tn(hin, dproj, MXU_DTYPE, "dn_dwin")
    token = send(d_w_in, d_w_out)
    dh = _mm_nt(dproj, w_in, F32, "dn_dh", pin=token)
    d_conv_w = jnp.concatenate([dwq, dwk, dwv], axis=1)
    return dh, dict(conv_w=d_conv_w, a_log=d_alp[0, h:2 * h], dt_bias=d_dtp[0, h:2 * h], norm_w=d_norm_w[0]), token


def _ln_silu(u, g, b):
    return jax.nn.silu(_ln(u, g, b))


def _conformer_fwd(hin, get_w_in, dw_w, dw_b, ln_g, ln_b, get_w_out):
    w_in = get_w_in(hin)
    vg = _mm_nn(hin, w_in, F32, "cf_vg")
    u1 = _cf_conv_fwd(vg, dw_w, dw_b)
    ch = u1.shape[1]

    def fn(u, g, b):
        return (_ln_silu(u, g, b),), ()

    (u2,), _ = _rowmap(fn, [u1], [ln_g, ln_b], [(ch, MXU_DTYPE)], [], "cf_ln")
    w_out = get_w_out(u2)
    y = _mm_nn(u2, w_out, F32, "cf_y")
    return y, (hin, vg, u1, u2, w_in, w_out)


def _conformer_bwd(res, dy, dw_w, ln_g, ln_b):
    hin, vg, u1, u2, w_in, w_out = res
    ch = u1.shape[1]
    d_w_out = _mm_tn(u2, dy, MXU_DTYPE, "cf_dwout")
    du2 = _mm_nt(dy, w_out, F32, "cf_du2")

    def fn(u, du2, g, b):
        _, vjp = jax.vjp(_ln_silu, u, g, b)
        gu, gg, gb = vjp(du2)
        return (gu,), (gg, gb)

    (du1,), (d_ln_g, d_ln_b) = _rowmap(fn, [u1, du2], [ln_g, ln_b], [(ch, F32)], [(1, ch), (1, ch)], "cf_ln_bwd")
    dval, dgate, d_dw_w, d_dw_b = _cf_conv_bwd(vg, dw_w, du1)
    dvg = jnp.concatenate([dval, dgate], axis=1)
    d_w_in = _mm_tn(hin, dvg, MXU_DTYPE, "cf_dwin", split_cols=True)
    dh = _mm_nt(dvg, w_in, F32, "cf_dh")
    return dh, dict(w_in=d_w_in, w_out=d_w_out, dw_w=d_dw_w, dw_b=d_dw_b[0], ln_g=d_ln_g[0], ln_b=d_ln_b[0])


def _mlp_fwd(hin, get_w1, get_w2):
    w1 = get_w1(hin)
    r = _mm_nn(hin, w1, MXU_DTYPE, "ff_a", relu2=True)
    w2 = get_w2(r)
    m = _mm_nn(r, w2, F32, "ff_m")
    return m, (hin, r, w1, w2)


def _mlp_bwd(res, dm):
    hin, r, w1, w2 = res
    d_w2 = _mm_tn(r, dm, MXU_DTYPE, "ff_dw2")
    da = _mm_nt(dm, w2, MXU_DTYPE, "ff_da", relu2_sq=r)
    d_w1 = _mm_tn(hin, da, MXU_DTYPE, "ff_dw1", split_cols=True)
    dh = _mm_nt(da, w1, F32, "ff_dh")
    return dh, d_w1, d_w2


def _ada_fwd(c_all, ada_w):
    depth, d, nl = ada_w.shape
    tn = _tile(nl, 256)

    def body(c_ref, w_ref, o_ref, cond_ref):
        cond = jax.nn.silu(c_ref[...]).astype(MXU_DTYPE)
        cond_ref[...] = cond
        o_ref[...] = lax.dot_general(cond, w_ref[...].astype(MXU_DTYPE), (NN, ((), ())), preferred_element_type=F32)

    return _call(body, grid=(depth, nl // tn),
                 ins=[(c_all, c_all.shape, lambda l, j: (0, 0)), (ada_w, (None, d, tn), lambda l, j: (l, 0, j))],
                 outs=[((depth, N_DEV, nl), F32, (None, N_DEV, tn), lambda l, j: (l, 0, j)),
                       (c_all.shape, MXU_DTYPE, c_all.shape, lambda l, j: (0, 0))],
                 name="ada_fwd")


def _ada_bwd(cond_all, dmod_cols):
    depth, _, nl = dmod_cols.shape
    d = cond_all.shape[1]
    tn = _tile(nl, 256)

    def body(c_ref, g_ref, o_ref):
        o_ref[...] = lax.dot_general(c_ref[...], g_ref[...].astype(MXU_DTYPE), (TN, ((), ())),
                                     preferred_element_type=F32)

    return _call(body, grid=(depth, nl // tn),
                 ins=[(cond_all, cond_all.shape, lambda l, j: (0, 0)), (dmod_cols, (None, N_DEV, tn), lambda l, j: (l, 0, j))],
                 outs=[((depth, d, nl), F32, (None, d, tn), lambda l, j: (l, 0, j))], name="ada_bwd")[0]


def _peers():
    x, y, c = lax.axis_index("x"), lax.axis_index("y"), lax.axis_index("c")
    peers = []
    for k in range(1, N_DEV):
        px = 1 - x if k & 4 else x
        py = 1 - y if k & 2 else y
        pc = 1 - c if k & 1 else c
        peers.append(((px, py, pc), 4 * px + 2 * py + pc))
    return 4 * x + 2 * y + c, peers


_HBM = pl.BlockSpec(memory_space=pltpu.HBM)
_SEM = pl.BlockSpec(memory_space=pltpu.SEMAPHORE)
_ANY = pl.BlockSpec(memory_space=pl.ANY)
_EFFECT = pltpu.SideEffectType.DATAFLOW_SIDE_EFFECTING


def _xfer_start(srcs, lands, scatter, after, name):
    nt = len(srcs)

    def body(*refs):
        src, land = refs[:nt], refs[nt:2 * nt]
        sems = refs[2 * nt + 1:4 * nt + 1]
        token = refs[-1]
        me, peers = _peers()
        for t in range(nt):
            for k, (pid, plin) in enumerate(peers):
                pltpu.make_async_remote_copy(
                    src_ref=src[t].at[plin] if scatter else src[t], dst_ref=land[t].at[me],
                    send_sem=sems[2 * t].at[k], recv_sem=sems[2 * t + 1].at[k],
                    device_id=pid, device_id_type=pl.DeviceIdType.MESH).start()
        token[...] = jnp.zeros_like(token)

    out_shape = [pltpu.SemaphoreType.DMA((N_DEV - 1,)) for _ in range(2 * nt)]
    out_shape += [pltpu.HBM(a.shape, a.dtype) for a in lands]
    out_shape += [jax.ShapeDtypeStruct((8, LANES), F32)]
    srcs = [pltpu.with_memory_space_constraint(a, pltpu.HBM) for a in srcs]
    res = pl.pallas_call(
        body, name=name, out_shape=out_shape,
        in_specs=[_HBM] * (2 * nt) + [_ANY],
        out_specs=[_SEM] * (2 * nt) + [_HBM] * nt + [pl.BlockSpec(memory_space=pltpu.VMEM)],
        input_output_aliases={nt + i: 2 * nt + i for i in range(nt)},
        compiler_params=pltpu.CompilerParams(has_side_effects=_EFFECT),
    )(*srcs, *[pltpu.with_memory_space_constraint(a, pltpu.HBM) for a in lands], after)
    sems, thru = res[:2 * nt], res[2 * nt:3 * nt]
    return [(sems[2 * t], sems[2 * t + 1], srcs[t], thru[t]) for t in range(nt)], res[-1]


def _xfer_wait(handle, scatter, after, name):
    send, recv, src, land = handle

    def body(src_ref, land_ref, send_sem, recv_sem, after_ref, land_out):
        _, peers = _peers()
        for k, (pid, plin) in enumerate(peers):
            cp = pltpu.make_async_remote_copy(
                src_ref=src_ref.at[plin] if scatter else src_ref, dst_ref=land_ref.at[plin],
                send_sem=send_sem.at[k], recv_sem=recv_sem.at[k],
                device_id=pid, device_id_type=pl.DeviceIdType.MESH)
            cp.wait_send()
            cp.wait_recv()

    return pl.pallas_call(
        body, name=name, out_shape=pltpu.HBM(land.shape, land.dtype),
        in_specs=(_HBM, _HBM, _SEM, _SEM, _ANY), out_specs=_HBM, input_output_aliases={1: 0},
        compiler_params=pltpu.CompilerParams(has_side_effects=_EFFECT),
    )(src, land, send, recv, after)


def _landing(x, me):
    return lax.dynamic_update_slice(lax.empty((N_DEV,) + x.shape, x.dtype), x[None], (me,) + (0,) * x.ndim)


def _chip_peers():
    x, y, c = lax.axis_index("x"), lax.axis_index("y"), lax.axis_index("c")
    lin = lambda px, py, pc: 4 * px + 2 * py + pc
    sibling = ((x, y, 1 - c), lin(x, y, 1 - c))
    chips = [((1 - x, y, c), lin(1 - x, y, c)), ((x, 1 - y, c), lin(x, 1 - y, c)),
             ((1 - x, 1 - y, c), lin(1 - x, 1 - y, c))]
    return lin(x, y, c), sibling, chips


N_CHIPS_OTHER = 3


def _gather2_start(srcs, lands, after, name):
    nt = len(srcs)

    def body(*refs):
        src, land = refs[:nt], refs[nt:2 * nt]
        sems = refs[2 * nt + 1:5 * nt + 1]
        token = refs[-1]
        me, sibling, chips = _chip_peers()
        for t in range(nt):
            send, recv_ici, recv_sib = sems[3 * t], sems[3 * t + 1], sems[3 * t + 2]
            pltpu.make_async_remote_copy(src_ref=src[t], dst_ref=land[t].at[me], send_sem=send.at[0],
                                         recv_sem=recv_sib.at[0], device_id=sibling[0],
                                         device_id_type=pl.DeviceIdType.MESH).start()
            for j, (pid, _) in enumerate(chips):
                pltpu.make_async_remote_copy(src_ref=src[t], dst_ref=land[t].at[me], send_sem=send.at[1 + j],
                                             recv_sem=recv_ici.at[j], device_id=pid,
                                             device_id_type=pl.DeviceIdType.MESH).start()
        token[...] = jnp.zeros_like(token)

    out_shape = []
    for _ in range(nt):
        out_shape += [pltpu.SemaphoreType.DMA((1 + N_CHIPS_OTHER,)), pltpu.SemaphoreType.DMA((N_CHIPS_OTHER,)),
                      pltpu.SemaphoreType.DMA((1,))]
    out_shape += [pltpu.HBM(a.shape, a.dtype) for a in list(srcs) + list(lands)]
    out_shape += [jax.ShapeDtypeStruct((8, LANES), F32)]
    res = pl.pallas_call(
        body, name=name, out_shape=out_shape,
        in_specs=[_HBM] * (2 * nt) + [_ANY],
        out_specs=[_SEM] * (3 * nt) + [_HBM] * (2 * nt) + [pl.BlockSpec(memory_space=pltpu.VMEM)],
        input_output_aliases={i: 3 * nt + i for i in range(2 * nt)},
        compiler_params=pltpu.CompilerParams(has_side_effects=_EFFECT),
    )(*[pltpu.with_memory_space_constraint(a, pltpu.HBM) for a in list(srcs) + list(lands)], after)
    sems, thru = res[:3 * nt], res[3 * nt:5 * nt]
    return [(sems[3 * t], sems[3 * t + 1], sems[3 * t + 2], thru[t], thru[nt + t]) for t in range(nt)], res[-1]


def _gather2_relay(handles, after, name):
    nt = len(handles)
    after = list(after) if isinstance(after, (list, tuple)) else [after]

    def body(*refs):
        src, land = refs[:nt], refs[nt:2 * nt]
        send1, recv_ici = refs[2 * nt:3 * nt], refs[3 * nt:4 * nt]
        outs = refs[4 * nt + len(after):]
        send2, recv2 = outs[:nt], outs[nt:2 * nt]
        token = refs[-1]
        token[...] = jnp.zeros_like(token)
        me, sibling, chips = _chip_peers()
        for t in range(nt):
            pltpu.make_async_remote_copy(src_ref=src[t], dst_ref=land[t].at[me], send_sem=send1[t].at[0],
                                         recv_sem=recv_ici[t].at[0], device_id=sibling[0],
                                         device_id_type=pl.DeviceIdType.MESH).wait_send()
            for j, (pid, plin) in enumerate(chips):
                arrived = pltpu.make_async_remote_copy(src_ref=src[t], dst_ref=land[t].at[plin], send_sem=send1[t].at[1 + j],
                                                       recv_sem=recv_ici[t].at[j], device_id=pid,
                                                       device_id_type=pl.DeviceIdType.MESH)
                arrived.wait_send()
                arrived.wait_recv()
                pltpu.make_async_remote_copy(src_ref=land[t].at[plin], dst_ref=land[t].at[plin], send_sem=send2[t].at[j],
                                             recv_sem=recv2[t].at[j], device_id=sibling[0],
                                             device_id_type=pl.DeviceIdType.MESH).start()

    srcs = [h[3] for h in handles]
    lands = [h[4] for h in handles]
    out_shape = [pltpu.SemaphoreType.DMA((N_CHIPS_OTHER,)) for _ in range(2 * nt)]
    out_shape += [pltpu.HBM(a.shape, a.dtype) for a in srcs + lands]
    out_shape += [jax.ShapeDtypeStruct((8, LANES), F32)]
    res = pl.pallas_call(
        body, name=name, out_shape=out_shape,
        in_specs=[_HBM] * (2 * nt) + [_SEM] * (2 * nt) + [_ANY] * len(after),
        out_specs=[_SEM] * (2 * nt) + [_HBM] * (2 * nt) + [pl.BlockSpec(memory_space=pltpu.VMEM)],
        input_output_aliases={i: 2 * nt + i for i in range(2 * nt)},
        compiler_params=pltpu.CompilerParams(has_side_effects=_EFFECT),
    )(*srcs, *lands, *[h[0] for h in handles], *[h[1] for h in handles], *after)
    return [(handles[t][2], res[t], res[nt + t], res[3 * nt + t]) for t in range(nt)], res[-1]


def _gather2_wait(handle, after, name):
    recv_sib, send2, recv2, land = handle

    def body(land_ref, recv_sib_sem, send2_sem, recv2_sem, after_ref, land_out):
        me, sibling, chips = _chip_peers()
        pltpu.make_async_remote_copy(src_ref=land_ref.at[me], dst_ref=land_ref.at[sibling[1]], send_sem=send2_sem.at[0],
                                     recv_sem=recv_sib_sem.at[0], device_id=sibling[0],
                                     device_id_type=pl.DeviceIdType.MESH).wait_recv()
        for j, (pid, plin) in enumerate(chips):
            relayed = pltpu.make_async_remote_copy(src_ref=land_ref.at[plin], dst_ref=land_ref.at[plin], send_sem=send2_sem.at[j],
                                                   recv_sem=recv2_sem.at[j], device_id=sibling[0],
                                                   device_id_type=pl.DeviceIdType.MESH)
            relayed.wait_send()
            relayed.wait_recv()

    return pl.pallas_call(
        body, name=name, out_shape=pltpu.HBM(land.shape, land.dtype),
        in_specs=(_HBM, _SEM, _SEM, _SEM, _ANY), out_specs=_HBM, input_output_aliases={0: 0},
        compiler_params=pltpu.CompilerParams(has_side_effects=_EFFECT),
    )(land, recv_sib, send2, recv2, after)


def _exchange(arrs, scatter, name):
    nt = len(arrs)
    out_shape = [jax.ShapeDtypeStruct(a.shape if scatter else (N_DEV,) + a.shape, a.dtype) for a in arrs]

    def body(*refs):
        ins, outs = refs[:nt], refs[nt:2 * nt]
        send, recv, loc = refs[2 * nt:]
        me, peers = _peers()
        copies = []
        for t in range(nt):
            own = pltpu.make_async_copy(ins[t].at[me] if scatter else ins[t], outs[t].at[me], loc.at[t])
            own.start()
            copies.append(own)
            for k, (pid, plin) in enumerate(peers):
                cp = pltpu.make_async_remote_copy(
                    src_ref=ins[t].at[plin] if scatter else ins[t], dst_ref=outs[t].at[me],
                    send_sem=send.at[t, k], recv_sem=recv.at[t, k],
                    device_id=pid, device_id_type=pl.DeviceIdType.MESH)
                cp.start()
                copies.append(cp)
        for cp in copies:
            cp.wait()

    any_spec = pl.BlockSpec(memory_space=pl.ANY)
    return pl.pallas_call(
        body, out_shape=out_shape, in_specs=[any_spec] * nt, out_specs=[any_spec] * nt,
        scratch_shapes=[pltpu.SemaphoreType.DMA((nt, N_DEV - 1)), pltpu.SemaphoreType.DMA((nt, N_DEV - 1)),
                        pltpu.SemaphoreType.DMA((nt,))],
        name=name)(*arrs)


def _adamw_body(n_parts, stacked=True):
    def body(p_ref, w_ref, m_ref, v_ref, *rest):
        g_out, d_out, m_out, v_out = rest[-4:]
        part = (lambda i: p_ref[i]) if stacked else (lambda i: p_ref[i][...])
        g = part(0).astype(F32)
        for i in range(1, n_parts):
            g = g + part(i).astype(F32)
        m2 = ADAM_B1 * m_ref[...] + (1.0 - ADAM_B1) * g
        v2 = ADAM_B2 * v_ref[...] + (1.0 - ADAM_B2) * jnp.square(g)
        m_hat = m2 / (1.0 - ADAM_B1 ** ADAM_STEP)
        v_hat = v2 / (1.0 - ADAM_B2 ** ADAM_STEP)
        g_out[...] = g
        d_out[...] = -ADAM_LR * (m_hat / (jnp.sqrt(v_hat) + ADAM_EPS) + ADAM_WD * w_ref[...])
        m_out[...] = m2
        v_out[...] = v2

    return body


def _adamw_layer(own, land, me, w, m, v, layer, prev, name):
    _, r, c = own.shape
    tr = _tile(r, 256, 8)
    blk = pl.BlockSpec((None, tr, c), lambda i, me_ref: (layer, i, 0))
    share = lambda k: pl.BlockSpec((None, tr, c), lambda i, me_ref: (me_ref[0] ^ k, i, 0))
    in_specs = [share(k) for k in range(N_DEV)] + [blk, blk, blk]
    args = [own] + [land] * (N_DEV - 1) + [w, m, v]
    aliases = {}
    if prev is not None:
        in_specs += [_ANY] * 4
        args += list(prev)
        aliases = {1 + N_DEV + 3 + i: i for i in range(4)}

    def body(me_ref, *refs):
        token_ref = refs[-1]
        refs = (refs[:N_DEV],) + refs[N_DEV:-1]
        _adamw_body(N_DEV, stacked=False)(*refs)
        token_ref[...] = jnp.zeros(token_ref.shape, F32)

    token_blk = pl.BlockSpec((8, LANES), lambda i, me_ref: (0, 0))
    res = pl.pallas_call(
        body,
        grid_spec=pltpu.PrefetchScalarGridSpec(num_scalar_prefetch=1, grid=(r // tr,), in_specs=in_specs,
                                               out_specs=[blk] * 4 + [token_blk]),
        out_shape=[jax.ShapeDtypeStruct(w.shape, F32)] * 4 + [jax.ShapeDtypeStruct((8, LANES), F32)],
        input_output_aliases=aliases, name=name, compiler_params=_cparams(1))(me, *args)
    return res[:4], res[4]


def _adamw(parts, w, m, v, name):
    p, nl, r, c = parts.shape
    tr = _tile(r, 256, 8)
    body = _adamw_body(p)

    blk = (None, tr, c)
    imap = lambda l, i: (l, i, 0)
    out = ((nl, r, c), F32, blk, imap)
    return _call(body, grid=(nl, r // tr),
                 ins=[(parts, (p, None, tr, c), lambda l, i: (0, l, i, 0)), (w, blk, imap), (m, blk, imap), (v, blk, imap)],
                 outs=[out] * 4, name=name)


def _adamw_small(parts, w, m, v, shapes, offs, name):
    r = w.shape[0]
    n_results = 4

    def body(p_ref, w_ref, m_ref, v_ref, *rest):
        outs, packed = rest[:n_results * len(shapes)], rest[n_results * len(shapes):]
        _adamw_body(N_DEV)(p_ref, w_ref, m_ref, v_ref, *packed)
        for n, shp in enumerate(shapes):
            groups, width = max(shp[-1] // LANES, 1), min(shp[-1], LANES)
            for row in range(offs[n + 1] - offs[n]):
                lead = np.unravel_index(row // groups, shp[:-1])
                at = tuple(int(i) for i in lead[:-1]) + (pl.ds(int(lead[-1]), 1), pl.ds((row % groups) * LANES, width))
                for k in range(n_results):
                    outs[n_results * n + k][at] = packed[k][pl.ds(offs[n] + row, 1), pl.ds(0, width)]

    whole = lambda shape: (shape, functools.partial(lambda i, nd: (0,) * nd, nd=len(shape)))
    return _call(body, grid=(1,),
                 ins=[(a,) + whole(a.shape) for a in (parts, w, m, v)],
                 outs=[(shp, F32) + whole(shp) for shp in shapes for _ in range(n_results)],
                 name=name, scratch=[pltpu.VMEM((r, LANES), F32)] * n_results)


def _rows(x):
    return x.reshape(-1, LANES)


def _pad_rows(x, mult=8):
    r = x.shape[0]
    extra = (-r) % mult
    return jnp.pad(x, ((0, extra), (0, 0))) if extra else x


def kernel(x, c, ada_w, ada_b, ln_g, ln_b, dn_w_in, dn_conv_w, dn_a_log, dn_dt_bias, dn_norm_w, dn_w_out, cf_w_in, cf_dw_w, cf_dw_b, cf_ln_g, cf_ln_b, cf_w_out, ff_w1, ff_w2, loss_target, m_ada_w, m_ada_b, m_ln_g, m_ln_b, m_dn_w_in, m_dn_conv_w, m_dn_a_log, m_dn_dt_bias, m_dn_norm_w, m_dn_w_out, m_cf_w_in, m_cf_dw_w, m_cf_dw_b, m_cf_ln_g, m_cf_ln_b, m_cf_w_out, m_ff_w1, m_ff_w2, v_ada_w, v_ada_b, v_ln_g, v_ln_b, v_dn_w_in, v_dn_conv_w, v_dn_a_log, v_dn_dt_bias, v_dn_norm_w, v_dn_w_out, v_cf_w_in, v_cf_dw_w, v_cf_dw_b, v_cf_ln_g, v_cf_ln_b, v_cf_w_out, v_ff_w1, v_ff_w2):
    depth, d, _ = ada_w.shape
    n_a, n_b = dn_w_in.shape[0], cf_w_in.shape[0]
    heads = dn_a_log.shape[1]
    hw = heads * HEAD_DIM
    taps = cf_dw_w.shape[1]
    s = x.shape[1]
    alpha = (2.0 * depth) ** 0.25
    me = 4 * lax.axis_index("x") + 2 * lax.axis_index("y") + lax.axis_index("c")
    me_arr = jnp.reshape(me, (1,)).astype(jnp.int32)
    xs, tgt = x[0], loss_target[0]

    dn_in_cols = dn_w_in.shape[2]
    keys, shards = [], []
    for i in range(depth):
        j = i // 2
        mixer = [("dn_in", dn_w_in), ("dn_out", dn_w_out)] if i % 2 == 0 else [("cf_in", cf_w_in), ("cf_out", cf_w_out)]
        for nm, wt in mixer:
            keys.append((nm, j))
            shards.append(wt[j].astype(MXU_DTYPE))
        keys += [("ff1", i), ("ff2", i)]
        shards += [ff_w1[i].astype(MXU_DTYPE), ff_w2[i].astype(MXU_DTYPE)]

    small_local = [_rows(ln_g), _rows(ln_b), _rows(dn_conv_w), _rows(cf_dw_w), _rows(cf_dw_b), _rows(cf_ln_g),
                   _rows(cf_ln_b), _rows(c)]
    sizes = [a.shape[0] for a in small_local]
    packed = _pad_rows(jnp.concatenate(small_local, axis=0))
    (small_all,) = _exchange([packed], False, "comm_gather_params")
    offs = [0]
    for z in sizes:
        offs.append(offs[-1] + z)

    def small(i):
        return small_all[:, offs[i]:offs[i + 1], :]

    def unshard(piece, lead, groups):
        t = piece.reshape((N_DEV,) + lead + (groups * LANES,))
        t = jnp.moveaxis(t, 0, len(lead))
        return t.reshape(lead + (N_DEV * groups * LANES,))

    ln_g_f = unshard(small(0), (depth, 2), 1)
    ln_b_f = unshard(small(1), (depth, 2), 1)
    conv_w_f = unshard(small(2), (n_a, DN_CONV), 3 * heads // N_DEV)
    dw_w_f = unshard(small(3), (n_b, taps), 1)
    dw_b_f = unshard(small(4), (n_b,), 1)
    cf_ln_g_f = unshard(small(5), (n_b,), 1)
    cf_ln_b_f = unshard(small(6), (n_b,), 1)
    c_all = small(7).reshape(N_DEV, d)

    mod_part, cond_all = _ada_fwd(c_all, ada_w)
    (mod_all,) = _exchange([mod_part], False, "comm_gather_mod")
    mod_mine = lax.dynamic_index_in_dim(mod_all, me, axis=2, keepdims=False)
    mod_mine = jnp.moveaxis(mod_mine, 0, 1).reshape(depth, N_MOD * d)

    lands = [_landing(a, me) for a in shards]
    first, token = _gather2_start(shards[:1], lands[:1], mod_all, "gather_first_weight_start")
    handles = {keys[0]: first[0]}
    groups = [keys[:1], keys[1:4]] + [keys[4 * i:4 * i + 4] for i in range(1, depth)]
    group_of = {k: n for n, grp in enumerate(groups) for k in grp}
    relayed, weights = {}, {}
    wait_after = {}

    def relay(n, after):
        if n < len(groups) and groups[n][0] not in relayed:
            hs, relay_token = _gather2_relay([handles[k] for k in groups[n]], after, "gather_relay_%d" % n)
            relayed.update(zip(groups[n], hs))
            return relay_token

    def gathered(key, after):
        if key not in weights:
            relay(group_of[key], after)
            if key[0] == "ff1":
                relay(key[1] + 2, after)
            weights[key] = _gather2_wait(relayed[key], wait_after.get(key, after), "gather_wait_%s_%d" % key)
        return weights[key]

    def get_dn_in(j):
        def get(after):
            g = gathered(("dn_in", j), after)
            w = jnp.moveaxis(g, 0, 1).reshape(d, N_DEV * dn_in_cols)
            return jnp.pad(w, ((0, 0), (0, 4 * hw + LANES - N_DEV * dn_in_cols)))
        return get

    def get_rows(key):
        return lambda after: gathered(key, after).reshape((-1, d))

    def get_cols(key):
        return lambda after: gathered(key, after)

    def add_bias(a, b):
        return (a + b,), ()

    (mod,), _ = _rowmap(add_bias, [mod_mine, ada_b], [], [(N_MOD * d, F32)], [], "ada_bias", pin=token)
    mod_rows = mod.reshape(depth * N_MOD, 1, d)
    ln_g_rows = ln_g_f.reshape(depth * 2, 1, d)
    ln_b_rows = ln_b_f.reshape(depth * 2, 1, d)

    def mod_row(i, j):
        return (mod_rows, i * N_MOD + j)

    def ln_row(rows, i, j):
        return (rows, i * 2 + j)

    subs = []
    h_cur = _modulate_fwd(xs, mod_row(0, 1), mod_row(0, 0))
    relay_token = relay(0, [h_cur, m_dn_w_in, v_dn_w_in])
    rest, wait_after[keys[0]] = _gather2_start(shards[1:], lands[1:], relay_token, "gather_weights_start")
    handles.update(zip(keys[1:], rest))
    x_cur = xs
    last = None
    for i in range(depth):
        j = i // 2
        if i % 2 == 0:
            y, res = _deltanet_fwd(h_cur, get_dn_in(j), conv_w_f[j], dn_a_log[j], dn_dt_bias[j], dn_norm_w[j],
                                   get_rows(("dn_out", j)))
        else:
            y, res = _conformer_fwd(h_cur, get_cols(("cf_in", j)), dw_w_f[j], dw_b_f[j][None, :], cf_ln_g_f[j][None, :],
                                    cf_ln_b_f[j][None, :], get_rows(("cf_out", j)))
        p1 = (mod_row(i, 2), ln_row(ln_g_rows, i, 0), ln_row(ln_b_rows, i, 0), mod_row(i, 4), mod_row(i, 3))
        x_mid, h_mid = _combine_fwd(alpha, x_cur, y, *p1)
        subs.append((x_cur, y, p1, res))
        m_out, res2 = _mlp_fwd(h_mid, get_cols(("ff1", i)), get_rows(("ff2", i)))
        if i + 1 < depth:
            p2 = (mod_row(i, 5), ln_row(ln_g_rows, i, 1), ln_row(ln_b_rows, i, 1), mod_row(i + 1, 1), mod_row(i + 1, 0))
            x_next, h_next = _combine_fwd(alpha, x_mid, m_out, *p2)
            subs.append((x_mid, m_out, p2, res2))
            x_cur, h_cur = x_next, h_next
        else:
            p2 = (mod_row(i, 5), ln_row(ln_g_rows, i, 1), ln_row(ln_b_rows, i, 1))
            last = (x_mid, m_out, p2, res2)

    x_in, y_in, p_last, res_last = last
    dx, dy, (loss_acc, g_gt, g_g, g_b) = _last_fwd_bwd(alpha, x_in, y_in, tgt, *p_last)
    loss = lax.psum(loss_acc[0, 0], ("x", "y", "c"))

    d_mod = [[None] * N_MOD for _ in range(depth)]
    d_ln_g = [[None, None] for _ in range(depth)]
    d_ln_b = [[None, None] for _ in range(depth)]
    d_mod[depth - 1][5], d_ln_g[depth - 1][1], d_ln_b[depth - 1][1] = g_gt, g_g, g_b
    gw = dict(dn=[None] * n_a, cf=[None] * n_b)

    sent = {}

    def send_grads(named, tag):
        parts = [p for _, p in named]
        hs, tok = _xfer_start(parts, [lax.empty(p.shape, p.dtype) for p in parts], True, parts[0], "scatter_start_" + tag)
        for (key, _), hnd in zip(named, hs):
            sent[key] = hnd
        return tok

    def by_rows(g):
        return g.reshape((N_DEV, g.shape[0] // N_DEV, g.shape[1]))

    def send_mlp(i, d_w1, d_w2):
        return send_grads([(("ff1", i), d_w1), (("ff2", i), by_rows(d_w2))], "ff_%d" % i)

    dh, d_w1, d_w2 = _mlp_bwd(res_last, dy)
    pin = send_mlp(depth - 1, d_w1, d_w2)
    for idx in range(len(subs) - 1, -1, -1):
        x_in, y_in, prm, res = subs[idx]
        i, second = idx // 2, idx % 2
        dx, dy, (g_gt, g_g, g_b, g_sc, g_sh) = _combine_bwd(alpha, x_in, y_in, dx, dh, *prm, pin=pin)
        d_mod[i][5 if second else 2], d_ln_g[i][second], d_ln_b[i][second] = g_gt, g_g, g_b
        nxt_i, nxt_base = (i + 1, 0) if second else (i, 3)
        d_mod[nxt_i][nxt_base + 1], d_mod[nxt_i][nxt_base] = g_sc, g_sh
        j = i // 2
        if second:
            dh, d_w1, d_w2 = _mlp_bwd(res, dy)
            pin = send_mlp(i, d_w1, d_w2)
        elif i % 2 == 0:
            def send_dn(d_w_in, d_w_out, j=j):
                d_in = d_w_in[:, :N_DEV * dn_in_cols].reshape(d, N_DEV, dn_in_cols)
                return send_grads([(("dn_in", j), jnp.moveaxis(d_in, 1, 0)), (("dn_out", j), by_rows(d_w_out))],
                                  "dn_%d" % j)

            dh, gw["dn"][j], pin = _deltanet_bwd(res, dy, conv_w_f[j], send_dn)
        else:
            dh, gw["cf"][j] = _conformer_bwd(res, dy, dw_w_f[j], cf_ln_g_f[j][None, :], cf_ln_b_f[j][None, :])
            pin = send_grads([(("cf_in", j), gw["cf"][j]["w_in"]), (("cf_out", j), by_rows(gw["cf"][j]["w_out"]))],
                             "cf_%d" % j)
    grad_x, g_sc, g_sh = _modulate_bwd(xs, dx, dh, mod_row(0, 1), mod_row(0, 0), pin=pin)
    d_mod[0][1], d_mod[0][0] = g_sc, g_sh
    d_mod_full = jnp.concatenate([jnp.concatenate(r, axis=1) for r in d_mod], axis=0)

    stacked = {"dn_w_in": ("dn_in", dn_w_in, m_dn_w_in, v_dn_w_in), "dn_w_out": ("dn_out", dn_w_out, m_dn_w_out, v_dn_w_out),
               "cf_w_in": ("cf_in", cf_w_in, m_cf_w_in, v_cf_w_in), "cf_w_out": ("cf_out", cf_w_out, m_cf_w_out, v_cf_w_out),
               "ff_w1": ("ff1", ff_w1, m_ff_w1, v_ff_w1), "ff_w2": ("ff2", ff_w2, m_ff_w2, v_ff_w2)}
    chains = {key: None for key in stacked}

    def update_layer(i, token):
        mixer = ["dn_w_in", "dn_w_out"] if i % 2 == 0 else ["cf_w_in", "cf_w_out"]
        for key, idx in [("ff_w1", i), ("ff_w2", i)] + [(k, i // 2) for k in mixer]:
            short, w, m, v = stacked[key]
            land = _xfer_wait(sent[(short, idx)], True, token, "scatter_wait_%s_%d" % (short, idx))
            chains[key], token = _adamw_layer(sent[(short, idx)][2], land, me_arr, w, m, v, idx, chains[key],
                                              "adamw_%s_%d" % (key, idx))
        return token

    def stack_rows(lst):
        return jnp.stack(lst, axis=0)

    gs_ln_g = jnp.stack([jnp.concatenate(r, axis=0) for r in d_ln_g], axis=0)
    gs_ln_b = jnp.stack([jnp.concatenate(r, axis=0) for r in d_ln_b], axis=0)
    gs_conv_w = stack_rows([gw["dn"][j]["conv_w"] for j in range(n_a)])
    gs_dw_w = stack_rows([gw["cf"][j]["dw_w"] for j in range(n_b)])
    gs_dw_b = stack_rows([gw["cf"][j]["dw_b"] for j in range(n_b)])
    gs_cf_ln_g = stack_rows([gw["cf"][j]["ln_g"] for j in range(n_b)])
    gs_cf_ln_b = stack_rows([gw["cf"][j]["ln_b"] for j in range(n_b)])
    gs_a_log = stack_rows([_pad_lanes(gw["dn"][j]["a_log"], 0)[0] for j in range(n_a)])
    gs_dt_bias = stack_rows([_pad_lanes(gw["dn"][j]["dt_bias"], 0)[0] for j in range(n_a)])
    gs_norm_w = stack_rows([gw["dn"][j]["norm_w"] for j in range(n_a)])
    small_grads = [gs_ln_g, gs_ln_b, gs_conv_w, gs_dw_w, gs_dw_b, gs_cf_ln_g, gs_cf_ln_b, gs_a_log, gs_dt_bias,
                   gs_norm_w, d_mod_full]
    sg_rows = [_rows(a) for a in small_grads]
    sg_sizes = [a.shape[0] for a in sg_rows]
    sg_packed = _pad_rows(jnp.concatenate(sg_rows, axis=0))
    (sg_handle,), sg_token = _xfer_start([sg_packed], [_landing(sg_packed, me)], False, grad_x, "gather_small_grads_start")
    for i in range(depth - 1, -1, -1):
        sg_token = update_layer(i, sg_token)
    sg_all = _xfer_wait(sg_handle, False, sg_token, "gather_small_grads_wait")
    sg_offs = [0]
    for z in sg_sizes:
        sg_offs.append(sg_offs[-1] + z)

    def sg(i, shape):
        return sg_all[:, sg_offs[i]:sg_offs[i + 1], :].reshape((N_DEV,) + shape)

    dmod_all = sg(10, (depth, N_MOD * d))
    nl = ada_w.shape[2]
    dmod_cols = lax.dynamic_slice_in_dim(dmod_all, me * nl, nl, axis=2)
    g_ada_w = _ada_bwd(cond_all, jnp.moveaxis(dmod_cols, 0, 1))

    outs = {}

    def run_adamw(key, parts, w, m, v):
        shp = w.shape
        as3 = lambda t: t.reshape((-1,) + shp[-2:]) if t.ndim >= 3 else t.reshape((1,) + shp)
        parts3 = parts.reshape((parts.shape[0],) + as3(w).shape)
        res = _adamw(parts3, as3(w), as3(m), as3(v), "adamw_" + key)
        outs[key] = tuple(r.reshape(shp) for r in res)

    run_adamw("ada_w", g_ada_w[None], ada_w, m_ada_w, v_ada_w)

    cgroups = 3 * heads // N_DEV
    n_sharded = 7

    def my_cols(first, last, groups):
        x = sg_all[:, sg_offs[first]:sg_offs[last], :].reshape(N_DEV, -1, N_DEV, groups, LANES)
        return lax.dynamic_index_in_dim(x, me, axis=2, keepdims=False).reshape(N_DEV, -1, LANES)

    small_parts = [my_cols(0, 2, 1), my_cols(2, 3, cgroups), my_cols(3, n_sharded, 1),
                   sg_all[:, sg_offs[n_sharded]:sg_offs[-1], :]]
    sp_offs = [0]
    for n, z in enumerate(sg_sizes):
        sp_offs.append(sp_offs[-1] + (z // N_DEV if n < n_sharded else z))
    parts_packed = jnp.zeros((N_DEV, sp_offs[-1] + (-sp_offs[-1]) % 8, LANES), F32)
    at = 0
    for part in small_parts:
        parts_packed = lax.dynamic_update_slice(parts_packed, part, (0, at, 0))
        at += part.shape[1]

    def pad_heads(t):
        return jnp.pad(t, ((0, 0), (0, LANES - heads)))

    def pack_state(ln_g_, ln_b_, conv_w_, dw_w_, dw_b_, cln_g_, cln_b_, a_log_, dt_b_, norm_w_, ada_b_):
        rows = [_rows(ln_g_), _rows(ln_b_), _rows(conv_w_), _rows(dw_w_), _rows(dw_b_), _rows(cln_g_), _rows(cln_b_),
                pad_heads(a_log_), pad_heads(dt_b_), norm_w_, _rows(ada_b_)]
        return _pad_rows(jnp.concatenate(rows, axis=0))

    w_s = pack_state(ln_g, ln_b, dn_conv_w, cf_dw_w, cf_dw_b, cf_ln_g, cf_ln_b, dn_a_log, dn_dt_bias, dn_norm_w, ada_b)
    m_s = pack_state(m_ln_g, m_ln_b, m_dn_conv_w, m_cf_dw_w, m_cf_dw_b, m_cf_ln_g, m_cf_ln_b, m_dn_a_log,
                     m_dn_dt_bias, m_dn_norm_w, m_ada_b)
    v_s = pack_state(v_ln_g, v_ln_b, v_dn_conv_w, v_cf_dw_w, v_cf_dw_b, v_cf_ln_g, v_cf_ln_b, v_dn_a_log,
                     v_dn_dt_bias, v_dn_norm_w, v_ada_b)
    small_keys = ["ln_g", "ln_b", "dn_conv_w", "cf_dw_w", "cf_dw_b", "cf_ln_g", "cf_ln_b", "dn_a_log", "dn_dt_bias",
                  "dn_norm_w", "ada_b"]
    small_shapes = [ln_g.shape, ln_b.shape, dn_conv_w.shape, cf_dw_w.shape, cf_dw_b.shape, cf_ln_g.shape,
                    cf_ln_b.shape, dn_a_log.shape, dn_dt_bias.shape, dn_norm_w.shape, ada_b.shape]
    res_s = _adamw_small(parts_packed, w_s, m_s, v_s, small_shapes, sp_offs, "adamw_small")
    for n, key in enumerate(small_keys):
        outs[key] = tuple(res_s[4 * n:4 * n + 4])

    for key in stacked:
        outs[key] = tuple(chains[key])

    order = ["ada_w", "ada_b", "ln_g", "ln_b", "dn_w_in", "dn_conv_w", "dn_a_log", "dn_dt_bias", "dn_norm_w",
             "dn_w_out", "cf_w_in", "cf_dw_w", "cf_dw_b", "cf_ln_g", "cf_ln_b", "cf_w_out", "ff_w1", "ff_w2"]
    result = [loss, grad_x[None]]
    for part in range(4):
        result += [outs[k][part] for k in order]
    return tuple(result)
```

```python
import functools

import jax
import jax.numpy as jnp
import numpy as np
from jax import lax
from jax.experimental import pallas as pl
from jax.experimental.pallas import tpu as pltpu

F32 = jnp.float32
MXU_DTYPE = jnp.bfloat16
N_DEV = 8
LANES = 128
HEAD_DIM = 128
CHUNK = 64
DN_CONV = 4
N_MOD = 6
LN_EPS = 1e-5
RMS_EPS = 1e-6
L2_EPS = 1e-6
ADAM_LR = 0.001
ADAM_B1 = 0.9
ADAM_B2 = 0.999
ADAM_EPS = 1e-08
ADAM_WD = 0.01
ADAM_STEP = 10

NN = ((1,), (0,))
NT = ((1,), (1,))
TN = ((0,), (0,))

ROW_TILE = 512
CONV_TILE = 256
SHORT_CONV_TILE = 1024


def _mdot(a, b, dims=NN):
    return lax.dot_general(a.astype(MXU_DTYPE), b.astype(MXU_DTYPE), (dims, ((), ())), preferred_element_type=F32)


def _split3(x):
    hi = x.astype(MXU_DTYPE)
    r1 = x - hi.astype(F32)
    mid = r1.astype(MXU_DTYPE)
    lo = (r1 - mid.astype(F32)).astype(MXU_DTYPE)
    return hi, mid, lo


def _dot01(a, b, dims=NN, mask_first=True):
    d = lambda p, q: lax.dot_general(p, q, (dims, ((), ())), preferred_element_type=F32)
    if mask_first:
        m = a.astype(MXU_DTYPE)
        return sum(d(m, p) for p in _split3(b))
    m = b.astype(MXU_DTYPE)
    return sum(d(p, m) for p in _split3(a))


def _cparams(n):
    return pltpu.CompilerParams(dimension_semantics=("arbitrary",) * n)


def _call(body, *, grid, ins, outs, name, scratch=()):
    res = pl.pallas_call(
        body,
        grid=grid,
        in_specs=[pl.BlockSpec(memory_space=pl.ANY) if b is None else pl.BlockSpec(b, m) for _, b, m in ins],
        out_specs=[pl.BlockSpec(b, m) for _, _, b, m in outs],
        out_shape=[jax.ShapeDtypeStruct(s, d) for s, d, _, _ in outs],
        scratch_shapes=list(scratch),
        name=name,
        compiler_params=_cparams(len(grid)),
    )(*[a for a, _, _ in ins])
    return res


def _tile(n, pref, unit=LANES):
    if n <= pref:
        return n
    t = (pref // unit) * unit
    while t > unit and n % t:
        t -= unit
    assert n % t == 0, (n, pref)
    return t


def _rowmap(fn, rows, consts, row_outs, acc_outs, name, pin=None):
    rows = [r if isinstance(r, tuple) else (r, r.shape[1], 0) for r in rows]
    s = rows[0][0].shape[0]
    tm = min(ROW_TILE, s)
    nr, nc, no, na = len(rows), len(consts), len(row_outs), len(acc_outs)
    npin = 0 if pin is None else 1

    def body(*refs):
        rin, cin = refs[:nr], refs[nr:nr + nc]
        refs = refs[:nr + nc] + refs[nr + nc + npin:]
        rout, aout = refs[nr + nc:nr + nc + no], refs[nr + nc + no:]
        ro, ao = fn(*[r[...] for r in rin], *[c[...] for c in cin])
        for ref, val in zip(rout, ro):
            ref[...] = val.astype(ref.dtype)
        if na:
            first = pl.program_id(0) == 0

            @pl.when(first)
            def _():
                for ref, val in zip(aout, ao):
                    ref[...] = val

            @pl.when(jnp.logical_not(first))
            def _():
                for ref, val in zip(aout, ao):
                    ref[...] += val

    ins = [(a, (tm, w), functools.partial(lambda i, cb: (i, cb), cb=cb)) for a, w, cb in rows]
    for c in consts:
        if isinstance(c, tuple):
            ins.append((c[0], (None, 1, c[0].shape[2]), functools.partial(lambda i, n: (n, 0, 0), n=c[1])))
        else:
            ins.append((c, c.shape, lambda i: (0, 0)))
    if pin is not None:
        ins.append((pin, None, None))
    outs = [((s, w), d, (tm, w), lambda i: (i, 0)) for w, d in row_outs]
    outs += [(shp, F32, shp, lambda i: (0, 0)) for shp in acc_outs]
    res = _call(body, grid=(s // tm,), ins=ins, outs=outs, name=name)
    return res[:no], res[no:]


def _ln(z, g, b):
    mu = jnp.mean(z, -1, keepdims=True)
    var = jnp.mean(jnp.square(z - mu), -1, keepdims=True)
    return (z - mu) * lax.rsqrt(var + LN_EPS) * g + b


def _combine(alpha, x, y, gt, g, b, sc, sh):
    xn = _ln(alpha * x + (1.0 + gt) * y, g, b)
    return xn, xn * (1.0 + sc) + sh


def _modulate_fwd(x, sc, sh):
    def fn(x, sc, sh):
        return ((x * (1.0 + sc) + sh),), ()

    (h,), _ = _rowmap(fn, [x], [sc, sh], [(x.shape[1], MXU_DTYPE)], [], "modulate_fwd")
    return h


def _modulate_bwd(x, dx, dh, sc, sh, pin=None):
    d = x.shape[1]

    def fn(x, dx, dh, sc, sh):
        _, vjp = jax.vjp(lambda x, sc, sh: x * (1.0 + sc) + sh, x, sc, sh)
        gx, gsc, gsh = vjp(dh)
        return (dx + gx,), (gsc, gsh)

    (gx,), (gsc, gsh) = _rowmap(fn, [x, dx, dh], [sc, sh], [(d, F32)], [(1, d), (1, d)], "modulate_bwd", pin=pin)
    return gx, gsc, gsh


def _combine_fwd(alpha, x, y, gt, g, b, sc, sh):
    d = x.shape[1]

    def fn(x, y, gt, g, b, sc, sh):
        return _combine(alpha, x, y, gt, g, b, sc, sh), ()

    (xn, h), _ = _rowmap(fn, [x, y], [gt, g, b, sc, sh], [(d, F32), (d, MXU_DTYPE)], [], "combine_fwd")
    return xn, h


def _combine_bwd(alpha, x, y, dxn, dh, gt, g, b, sc, sh, pin=None):
    d = x.shape[1]

    def fn(x, y, dxn, dh, gt, g, b, sc, sh):
        _, vjp = jax.vjp(functools.partial(_combine, alpha), x, y, gt, g, b, sc, sh)
        gx, gy, ggt, gg, gb, gsc, gsh = vjp((dxn, dh))
        return (gx, gy), (ggt, gg, gb, gsc, gsh)

    (gx, gy), accs = _rowmap(fn, [x, y, dxn, dh], [gt, g, b, sc, sh], [(d, F32), (d, MXU_DTYPE)],
                             [(1, d)] * 5, "combine_bwd", pin=pin)
    return gx, gy, accs


def _last_fwd_bwd(alpha, x, y, tgt, gt, g, b):
    d = x.shape[1]

    def fn(x, y, tgt, gt, g, b):
        xn, vjp = jax.vjp(lambda x, y, gt, g, b: _ln(alpha * x + (1.0 + gt) * y, g, b), x, y, gt, g, b)
        err = xn - tgt
        gx, gy, ggt, gg, gb = vjp(err * (1.0 / d))
        rows = jnp.sum(jnp.square(err), axis=-1, keepdims=True)
        loss = (0.5 / d) * jnp.sum(rows, axis=0, keepdims=True) * jnp.ones((1, LANES), F32)
        return (gx, gy), (loss, ggt, gg, gb)

    (gx, gy), accs = _rowmap(fn, [x, y, tgt], [gt, g, b], [(d, F32), (d, MXU_DTYPE)],
                             [(1, LANES), (1, d), (1, d), (1, d)], "last_fwd_bwd")
    return gx, gy, accs


MM_VMEM_BUDGET = 40 * 2 ** 20


def _fit(options, cost):
    for o in options:
        if 2 * cost(o) <= MM_VMEM_BUDGET:
            return o
    return options[-1]


def _row_tiles(m):
    return [t for t in (2048, 1024, 512, 256) if t <= m and m % t == 0] or [m]


def _mm_call(a, a_blk, a_map, b, b_blk, b_map, outs, dims, grid, name, epi=None, extra=None, split=None, blocks=None,
             pin=None):
    nk = grid[2]
    n_out = len(outs)
    n_in = 2 + (extra is not None) + (pin is not None)

    def body(*refs):
        a_ref, b_ref = refs[0], refs[1]
        rest = refs[n_in:]
        out_refs = rest[:n_out]

        def finish(val):
            if epi == "relu2":
                out_refs[0][...] = jnp.square(jnp.maximum(val, 0.0)).astype(out_refs[0].dtype)
            elif epi == "relu2_bwd":
                sq = refs[2][...].astype(F32)
                root = jnp.where(sq > 0.0, sq * lax.rsqrt(sq), 0.0)
                out_refs[0][...] = (val * 2.0 * root).astype(out_refs[0].dtype)
            elif split is not None:
                for g in range(split[0]):
                    out_refs[0][g] = val[:, g * split[1]:(g + 1) * split[1]].astype(out_refs[0].dtype)
            else:
                out_refs[0][...] = val.astype(out_refs[0].dtype)

        if blocks is None:
            p = lax.dot_general(a_ref[...], b_ref[...], (dims, ((), ())), preferred_element_type=F32)
        else:
            p = None
            for g in range(blocks[0]):
                part = lax.dot_general(a_ref[:, g * blocks[1]:(g + 1) * blocks[1]], b_ref[g], (dims, ((), ())),
                                       preferred_element_type=F32)
                p = part if p is None else p + part
        if nk == 1:
            finish(p)
        else:
            acc = rest[n_out]
            k = pl.program_id(2)

            @pl.when(k == 0)
            def _():
                acc[...] = p

            @pl.when(k > 0)
            def _():
                acc[...] += p

            @pl.when(k == nk - 1)
            def _():
                finish(acc[...])

    if nk > 1:
        out_blk = tuple(x for x in outs[0][2] if x is not None)
        if split is not None:
            out_blk = (out_blk[1], split[0] * split[1])
        scratch = [pltpu.VMEM(out_blk, F32)]
    else:
        scratch = []
    ins = [(a, a_blk, a_map), (b, b_blk, b_map)] + ([extra] if extra is not None else [])
    ins += [(pin, None, None)] if pin is not None else []
    return _call(body, grid=grid, ins=ins, outs=outs, name=name, scratch=scratch)


def _isz(dt):
    return jnp.dtype(dt).itemsize


def _mm_nn(a, b, out_dtype, name, relu2=False):
    m, kdim = a.shape
    if b.ndim == 2:
        n = b.shape[1]
        tn = _tile(n, 1536 if n > 2048 else 512)
        b_blk, b_map = (kdim, tn), lambda i, j, k: (0, j)
    else:
        g, _, ng = b.shape
        n = g * ng
        tn = _tile(ng, 512)
        b_blk = (None, kdim, tn)
        b_map = functools.partial(lambda i, j, k, npg: (j // npg, 0, j % npg), npg=ng // tn)
    tm = _fit(_row_tiles(m), lambda t: t * kdim * _isz(a.dtype) + kdim * tn * _isz(b.dtype) + t * tn * _isz(out_dtype))
    grid = (m // tm, n // tn, 1)
    outs = [((m, n), out_dtype, (tm, tn), lambda i, j, k: (i, j))]
    return _mm_call(a, (tm, kdim), lambda i, j, k: (i, 0), b, b_blk, b_map, outs, NN, grid, name,
                    epi="relu2" if relu2 else None)[0]


def _mm_nt(a, b, out_dtype, name, relu2_sq=None, pin=None):
    m, n = a.shape
    extra_bytes = _isz(relu2_sq.dtype) if relu2_sq is not None else 0
    if b.ndim == 2:
        kout = b.shape[0]
        to = _tile(kout, 512)
        b_blk, b_map, blocks = (to, n), lambda i, j, k: (j, 0), None
    else:
        g, kout, ng = b.shape
        to = _tile(kout, 512)
        b_blk, b_map, blocks = (g, to, ng), lambda i, j, k: (0, j, 0), (g, ng)
    tm = _fit(_row_tiles(m), lambda t: t * n * _isz(a.dtype) + to * n * _isz(b.dtype)
              + t * to * (_isz(out_dtype) + extra_bytes))
    grid = (m // tm, kout // to, 1)
    outs = [((m, kout), out_dtype, (tm, to), lambda i, j, k: (i, j))]
    extra = (relu2_sq, (tm, to), lambda i, j, k: (i, j)) if relu2_sq is not None else None
    return _mm_call(a, (tm, n), lambda i, j, k: (i, 0), b, b_blk, b_map, outs, NT, grid, name,
                    epi="relu2_bwd" if relu2_sq is not None else None, extra=extra, blocks=blocks, pin=pin)[0]


def _mm_tn(a, b, out_dtype, name, split_cols=False):
    m, kdim = a.shape
    n = b.shape[1]
    tk = _tile(kdim, 512)
    tn = _tile(n, 1536)
    if not split_cols:
        out, split = ((kdim, n), out_dtype, (tk, tn), lambda i, j, k: (i, j)), None
    else:
        ng = n // N_DEV
        if tn % ng:
            tn = _tile(ng, 512)
        if tn >= ng:
            gb = tn // ng
            out = ((N_DEV, kdim, ng), out_dtype, (gb, tk, ng), lambda i, j, k: (j, i, 0))
            split = (gb, ng)
        else:
            out = ((N_DEV, kdim, ng), out_dtype, (None, tk, tn),
                   functools.partial(lambda i, j, k, npg: (j // npg, i, j % npg), npg=ng // tn))
            split = None
    grid = (kdim // tk, n // tn, 1)
    return _mm_call(a, (m, tk), lambda i, j, k: (0, i), b, (m, tn), lambda i, j, k: (0, j), [out], TN, grid, name,
                    split=split)[0]


def _shifted(xa, off, rows):
    if off % 8 == 0:
        return xa[off:off + rows]
    return pltpu.roll(xa, xa.shape[0] - off, 0)[:rows]


def _conv_pad(taps):
    return -(-(taps - 1) // 8) * 8


def _conv_tile(xp_ref, w, i, rows, taps):
    pad = _conv_pad(taps)
    r0 = pl.multiple_of(i * rows, rows)
    xa = xp_ref[pl.ds(r0, rows + pad), :]
    views = [_shifted(xa, pad - (taps - 1) + j, rows) for j in range(taps)]
    acc = w[0:1, :] * views[0]
    for j in range(1, taps):
        acc = acc + w[j:j + 1, :] * views[j]
    return r0, acc, views


def _conv_back_tile(yp_ref, w, i, rows, taps):
    pad = _conv_pad(taps)
    r0 = pl.multiple_of(i * rows, rows)
    ya = yp_ref[pl.ds(r0, rows + pad), :]
    acc = w[taps - 1:taps, :] * ya[:rows]
    for j in range(taps - 1):
        acc = acc + w[j:j + 1, :] * _shifted(ya, taps - 1 - j, rows)
    return r0, acc


def _tap_sums(dy, views, taps):
    row = lax.broadcasted_iota(jnp.int32, (taps, LANES), 0)
    acc = jnp.zeros((taps, LANES), F32)
    for j in range(taps):
        acc = acc + jnp.where(row == j, jnp.sum(dy * views[j], axis=0, keepdims=True), 0.0)
    return acc


def _silu_l2(xc, l2):
    a = jax.nn.silu(xc)
    if l2:
        a = a * lax.rsqrt(jnp.sum(a * a, axis=-1, keepdims=True) + L2_EPS)
    return a


def _dn_conv_fwd(proj, conv_w, c0, nblk, l2, name):
    s = proj.shape[0]
    pad = _conv_pad(DN_CONV)
    rows = min(SHORT_CONV_TILE, s)

    def body(x_ref, w_ref, o_ref, xp):
        xp[0:pad, :] = jnp.zeros((pad, LANES), F32)
        xp[pad:, :] = x_ref[...]
        w = w_ref[...]

        def tile(i, c):
            r0, acc, _ = _conv_tile(xp, w, i, rows, DN_CONV)
            o_ref[pl.ds(r0, rows), :] = _silu_l2(acc, l2)
            return c

        lax.fori_loop(0, s // rows, tile, 0)

    return _call(body, grid=(nblk,),
                 ins=[(proj, (s, LANES), lambda c: (0, c0 + c)), (conv_w, (DN_CONV, LANES), lambda c: (0, c0 + c))],
                 outs=[((nblk, s, LANES), F32, (None, s, LANES), lambda c: (c, 0, 0))],
                 name=name, scratch=[pltpu.VMEM((s + pad, LANES), F32)])[0]


def _dn_conv_bwd(proj, conv_w, da, c0, nblk, l2, name):
    s = proj.shape[0]
    pad = _conv_pad(DN_CONV)
    rows = min(SHORT_CONV_TILE, s)

    def body(x_ref, w_ref, da_ref, dx_ref, dw_ref, xp, yp):
        xp[0:pad, :] = jnp.zeros((pad, LANES), F32)
        xp[pad:, :] = x_ref[...]
        yp[s:, :] = jnp.zeros((pad, LANES), F32)
        w = w_ref[...]

        def tile(i, dw):
            r0, acc, views = _conv_tile(xp, w, i, rows, DN_CONV)
            _, vjp = jax.vjp(functools.partial(_silu_l2, l2=l2), acc)
            (dxc,) = vjp(da_ref[pl.ds(r0, rows), :])
            yp[pl.ds(r0, rows), :] = dxc
            return dw + _tap_sums(dxc, views, DN_CONV)

        dw_ref[...] = lax.fori_loop(0, s // rows, tile, jnp.zeros((DN_CONV, LANES), F32))

        def tile2(i, c):
            r0, acc = _conv_back_tile(yp, w, i, rows, DN_CONV)
            dx_ref[pl.ds(r0, rows), :] = acc.astype(dx_ref.dtype)
            return c

        lax.fori_loop(0, s // rows, tile2, 0)

    return _call(body, grid=(nblk,),
                 ins=[(proj, (s, LANES), lambda c: (0, c0 + c)), (conv_w, (DN_CONV, LANES), lambda c: (0, c0 + c)),
                      (da, (None, s, LANES), lambda c: (c, 0, 0))],
                 outs=[((s, nblk * LANES), MXU_DTYPE, (s, LANES), lambda c: (0, c)),
                       ((DN_CONV, nblk * LANES), F32, (DN_CONV, LANES), lambda c: (0, c))],
                 name=name, scratch=[pltpu.VMEM((s + pad, LANES), F32), pltpu.VMEM((s + pad, LANES), F32)])


def _cf_conv_fwd(vg, dw_w, dw_b):
    s, c2 = vg.shape
    ch = c2 // 2
    nblk = ch // LANES
    taps = dw_w.shape[0]
    pad = _conv_pad(taps)
    rows = min(CONV_TILE, s)

    def body(v_ref, g_ref, w_ref, b_ref, o_ref, xp):
        xp[0:pad, :] = jnp.zeros((pad, LANES), F32)
        xp[pad:, :] = v_ref[...] * jax.nn.sigmoid(g_ref[...])
        w = w_ref[...]
        bias = b_ref[...]

        def tile(i, c):
            r0, acc, _ = _conv_tile(xp, w, i, rows, taps)
            o_ref[pl.ds(r0, rows), :] = acc + bias
            return c

        lax.fori_loop(0, s // rows, tile, 0)

    return _call(body, grid=(nblk,),
                 ins=[(vg, (s, LANES), lambda c: (0, c)), (vg, (s, LANES), lambda c: (0, nblk + c)),
                      (dw_w, (taps, LANES), lambda c: (0, c)), (dw_b, (1, LANES), lambda c: (0, c))],
                 outs=[((s, ch), F32, (s, LANES), lambda c: (0, c))],
                 name="cf_conv_fwd", scratch=[pltpu.VMEM((s + pad, LANES), F32)])[0]


def _cf_conv_bwd(vg, dw_w, du):
    s, c2 = vg.shape
    ch = c2 // 2
    nblk = ch // LANES
    taps = dw_w.shape[0]
    pad = _conv_pad(taps)
    rows = min(CONV_TILE, s)

    def body(v_ref, g_ref, w_ref, du_ref, dv_ref, dg_ref, dw_ref, db_ref, xp, yp):
        sig = jax.nn.sigmoid(g_ref[...])
        xp[0:pad, :] = jnp.zeros((pad, LANES), F32)
        xp[pad:, :] = v_ref[...] * sig
        yp[0:s, :] = du_ref[...]
        yp[s:, :] = jnp.zeros((pad, LANES), F32)
        w = w_ref[...]
        db_ref[...] = jnp.sum(du_ref[...], axis=0, keepdims=True)

        def tile(i, dw):
            r0, _, views = _conv_tile(xp, w, i, rows, taps)
            return dw + _tap_sums(du_ref[pl.ds(r0, rows), :], views, taps)

        dw_ref[...] = lax.fori_loop(0, s // rows, tile, jnp.zeros((taps, LANES), F32))

        def tile2(i, c):
            r0, du0 = _conv_back_tile(yp, w, i, rows, taps)
            val = v_ref[pl.ds(r0, rows), :]
            sg = jax.nn.sigmoid(g_ref[pl.ds(r0, rows), :])
            dv_ref[pl.ds(r0, rows), :] = (du0 * sg).astype(dv_ref.dtype)
            dg_ref[pl.ds(r0, rows), :] = (du0 * val * sg * (1.0 - sg)).astype(dg_ref.dtype)
            return c

        lax.fori_loop(0, s // rows, tile2, 0)

    return _call(body, grid=(nblk,),
                 ins=[(vg, (s, LANES), lambda c: (0, c)), (vg, (s, LANES), lambda c: (0, nblk + c)),
                      (dw_w, (taps, LANES), lambda c: (0, c)), (du, (s, LANES), lambda c: (0, c))],
                 outs=[((s, ch), MXU_DTYPE, (s, LANES), lambda c: (0, c)),
                       ((s, ch), MXU_DTYPE, (s, LANES), lambda c: (0, c)),
                       ((taps, ch), F32, (taps, LANES), lambda c: (0, c)),
                       ((1, ch), F32, (1, LANES), lambda c: (0, c))],
                 name="cf_conv_bwd", scratch=[pltpu.VMEM((s + pad, LANES), F32), pltpu.VMEM((s + pad, LANES), F32)])


def _masks():
    r = lax.broadcasted_iota(jnp.int32, (CHUNK, CHUNK), 0)
    c = lax.broadcasted_iota(jnp.int32, (CHUNK, CHUNK), 1)
    return r >= c, r > c, r <= c


def _chunk_decay(g):
    causal, _, upper = _masks()
    gb = jnp.broadcast_to(g, (CHUNK, CHUNK))
    gam_r = _dot01(jnp.where(causal, 1.0, 0.0), gb)
    gam_s = _dot01(jnp.ones((CHUNK, CHUNK), F32), jnp.where(upper, gb, 0.0))
    dm = jnp.where(causal, jnp.exp(jnp.where(causal, gam_r - gam_s, 0.0)), 0.0)
    return gam_r[:, 0:1], dm


def _chunk_scores(q, k, beta, dm):
    _, strict, _ = _masks()
    both = _mdot(jnp.concatenate([k * beta, q * (HEAD_DIM ** -0.5)], axis=0), k, NT)
    return jnp.where(strict, both[:CHUNK] * dm, 0.0), both[CHUNK:] * dm


def _lockstep(gens):
    results = [None] * len(gens)
    alive = list(range(len(gens)))
    while alive:
        for i in list(alive):
            try:
                next(gens[i])
            except StopIteration as stop:
                results[i] = stop.value
                alive.remove(i)
    return results


def _chunk_prep_bwd(q, k, v, beta, gam, t, du, dw, daqk, dqd, dkd, dgl):
    causal, strict, _ = _masks()
    r = lax.broadcasted_iota(jnp.int32, (CHUNK, CHUNK), 0)
    c = lax.broadcasted_iota(jnp.int32, (CHUNK, CHUNK), 1)
    scale = HEAD_DIM ** -0.5
    eg = jnp.exp(gam)
    gam_last = gam[CHUNK - 1:CHUNK, :]
    rr = jnp.exp(gam_last - gam)
    kb = k * beta
    qs = q * scale
    vb = v * beta
    kbe = kb * eg
    gam_b = jnp.broadcast_to(gam, (CHUNK, CHUNK))
    gam_s = _dot01(jnp.ones((CHUNK, CHUNK), F32), jnp.where(r == c, gam_b, 0.0))
    both = _mdot(jnp.concatenate([kb, qs], axis=0), k, NT)
    duw = jnp.concatenate([du, dw], axis=1)
    dt = _mdot(duw, jnp.concatenate([vb, kbe], axis=1), NT)
    dvk = _mdot(t, duw, TN)
    yield
    dm = jnp.where(causal, jnp.exp(jnp.where(causal, gam_b - gam_s, 0.0)), 0.0)
    a = jnp.where(strict, both[:CHUNK] * dm, 0.0)
    aqk = both[CHUNK:] * dm
    dvb, dkbe = dvk[:, :HEAD_DIM], dvk[:, HEAD_DIM:]
    x = _mdot(t, dt, TN)
    yield
    da = jnp.where(strict, -_mdot(x, t, NT), 0.0)
    yield
    dkk = da * dm
    dqk = daqk * dm
    ddiff = da * a + daqk * aqk
    dboth = jnp.concatenate([dkk, dqk], axis=0)
    dkq = _mdot(dboth, k)
    dk_mm = _mdot(dboth, jnp.concatenate([kb, qs], axis=0), TN)
    colsum = _dot01(ddiff, jnp.ones((CHUNK, LANES), F32), TN, mask_first=False)[:, 0:1]
    yield
    dkb = dkq[:CHUNK] + dkbe * eg
    dk = dk_mm + dkb * beta + dkd * rr
    dq = (dkq[CHUNK:] + dqd * eg) * scale
    dbeta = jnp.sum(dkb * k, axis=-1, keepdims=True) + jnp.sum(dvb * v, axis=-1, keepdims=True)
    dv = dvb * beta
    deg = jnp.sum(dkbe * kb, axis=-1, keepdims=True) + jnp.sum(dqd * qs, axis=-1, keepdims=True)
    drr = jnp.sum(dkd * k, axis=-1, keepdims=True)
    dgam = deg * eg - drr * rr + jnp.sum(ddiff, axis=-1, keepdims=True) - colsum
    dgam_last = jnp.sum(drr * rr, axis=0, keepdims=True) + dgl[0:1, :] * jnp.exp(gam_last)
    row = lax.broadcasted_iota(jnp.int32, (CHUNK, 1), 0)
    dgam = dgam + jnp.where(row == CHUNK - 1, dgam_last, 0.0)
    dg = _dot01(jnp.where(causal, 1.0, 0.0), jnp.broadcast_to(dgam, (CHUNK, LANES)), TN)[:, 0:1]
    return dq, dk, dv, dbeta, dg


def _prep_group(s):
    nch = s // CHUNK
    return next(c for c in (16, 8, 4, 2, 1) if nch % c == 0)


def _tri_solve_lanes(a_l):
    n = a_l.shape[1]
    group = 8

    def body(a_ref, t_ref):
        t_ref[...] = jnp.zeros_like(t_ref)
        col = lax.broadcasted_iota(jnp.int32, (CHUNK, n), 0)

        def row(r, carry):
            r0 = pl.multiple_of(r * CHUNK, CHUNK)

            def inner(sg, acc):
                a8 = a_ref[pl.ds(r0 + pl.multiple_of(sg * group, group), group), :]
                for j in range(group):
                    t0 = pl.multiple_of((sg * group + j) * CHUNK, CHUNK)
                    acc = acc + a8[j:j + 1, :] * t_ref[pl.ds(t0, CHUNK), :]
                return acc

            acc = lax.fori_loop(0, (r + group - 1) // group, inner, jnp.zeros((CHUNK, n), F32))
            t_ref[pl.ds(r0, CHUNK), :] = jnp.where(col == r, 1.0, 0.0) - acc
            return carry

        lax.fori_loop(0, CHUNK, row, 0)

    return pl.pallas_call(body, out_shape=jax.ShapeDtypeStruct(a_l.shape, F32), name="dn_tri_solve")(a_l)


def _head_cols(bg, hh, heads):
    lane = lax.broadcasted_iota(jnp.int32, bg.shape, 1)
    beta = jnp.sum(jnp.where(lane == hh, bg, 0.0), axis=-1, keepdims=True)
    g = jnp.sum(jnp.where(lane == heads + hh, bg, 0.0), axis=-1, keepdims=True)
    return beta, g


def _dn_prep(q, k, v, bg):
    h, s, _ = q.shape
    cb = _prep_group(s)
    rb = cb * CHUNK
    big = lambda x: (x, (None, rb, HEAD_DIM), lambda n, hh: (hh, n, 0))
    sq = lambda x: (x, (None, rb, CHUNK), lambda n, hh: (hh, n, 0))
    col = lambda x: (x, (None, rb, 1), lambda n, hh: (hh, n, 0))
    tok = (bg, (rb, LANES), lambda n, hh: (n, 0))
    o_big = ((h, s, HEAD_DIM), F32, (None, rb, HEAD_DIM), lambda n, hh: (hh, n, 0))
    o_sq = ((h, s, CHUNK), F32, (None, rb, CHUNK), lambda n, hh: (hh, n, 0))
    o_col = ((h, s, 1), F32, (None, rb, 1), lambda n, hh: (hh, n, 0))

    def scores(q_ref, k_ref, bg_ref, a_ref, aqk_ref, gam_ref):
        beta, g = _head_cols(bg_ref[...], pl.program_id(1), h)
        for i in range(cb):
            sl = slice(i * CHUNK, (i + 1) * CHUNK)
            gam, dm = _chunk_decay(g[sl])
            a_ref[sl, :], aqk_ref[sl, :] = _chunk_scores(q_ref[sl, :], k_ref[sl, :], beta[sl], dm)
            gam_ref[sl, :] = gam

    a, aqk, gam = _call(scores, grid=(s // rb, h), ins=[big(q), big(k), tok], outs=[o_sq, o_sq, o_col],
                        name="dn_scores")
    n_prob = h * (s // CHUNK)
    t_l = _tri_solve_lanes(jnp.transpose(a.reshape(n_prob, CHUNK * CHUNK)))
    t = jnp.transpose(t_l).reshape(h, s, CHUNK)

    def wy(k_ref, v_ref, bg_ref, gam_ref, t_ref, u_ref, w_ref):
        beta, _ = _head_cols(bg_ref[...], pl.program_id(1), h)
        for i in range(cb):
            sl = slice(i * CHUNK, (i + 1) * CHUNK)
            kb = k_ref[sl, :] * beta[sl]
            rhs = jnp.concatenate([v_ref[sl, :] * beta[sl], kb * jnp.exp(gam_ref[sl, :])], axis=1)
            uw = _mdot(t_ref[sl, :], rhs)
            u_ref[sl, :] = uw[:, :HEAD_DIM]
            w_ref[sl, :] = uw[:, HEAD_DIM:]

    u, w = _call(wy, grid=(s // rb, h), ins=[big(k), big(v), tok, col(gam), sq(t)], outs=[o_big, o_big],
                 name="dn_wy")
    return u, w, aqk, t, gam


def _dn_prep_bwd(q, k, v, bg, gam, t, du, dw, daqk, dqd, dkd, dgl):
    h, s, _ = q.shape
    cb = _prep_group(s)
    rb = cb * CHUNK

    def body(q_ref, k_ref, v_ref, bg_ref, g_ref, t_ref, du_ref, dw_ref, da_ref, dqd_ref, dkd_ref, dgl_ref,
             dq_ref, dk_ref, dv_ref, dbg_ref):
        hh = pl.program_id(1)
        beta, _ = _head_cols(bg_ref[...], hh, h)
        slices = [slice(i * CHUNK, (i + 1) * CHUNK) for i in range(cb)]
        results = _lockstep([_chunk_prep_bwd(
            q_ref[sl, :], k_ref[sl, :], v_ref[sl, :], beta[sl], g_ref[sl, :], t_ref[sl, :],
            du_ref[sl, :], dw_ref[sl, :], da_ref[sl, :], dqd_ref[sl, :], dkd_ref[sl, :], dgl_ref[sl, :])
            for sl in slices])

        @pl.when(hh == 0)
        def _():
            dbg_ref[...] = jnp.zeros_like(dbg_ref)

        lane = lax.broadcasted_iota(jnp.int32, (CHUNK, LANES), 1)
        for sl, (dq, dk, dv, dbeta, dg) in zip(slices, results):
            dq_ref[sl, :] = dq
            dk_ref[sl, :] = dk
            dv_ref[sl, :] = dv
            dbg_ref[sl, :] += jnp.where(lane == hh, dbeta, 0.0) + jnp.where(lane == h + hh, dg, 0.0)

    big = lambda x: (x, (None, rb, HEAD_DIM), lambda n, hh: (hh, n, 0))
    sq = lambda x: (x, (None, rb, CHUNK), lambda n, hh: (hh, n, 0))
    col = lambda x: (x, (None, rb, 1), lambda n, hh: (hh, n, 0))
    tok = (bg, (rb, LANES), lambda n, hh: (n, 0))
    o_big = ((h, s, HEAD_DIM), F32, (None, rb, HEAD_DIM), lambda n, hh: (hh, n, 0))
    return _call(body, grid=(s // rb, h),
                 ins=[big(q), big(k), big(v), tok, col(gam), sq(t), big(du), big(dw), sq(daqk), big(dqd), big(dkd),
                      col(dgl)],
                 outs=[o_big, o_big, o_big, ((s, LANES), F32, (rb, LANES), lambda n, hh: (n, 0))], name="dn_prep_bwd")


def _chunk_scaled(q, k, gam):
    gam_last = gam[CHUNK - 1:CHUNK, :]
    q_dec = q * (HEAD_DIM ** -0.5) * jnp.exp(gam)
    k_dec = k * jnp.exp(gam_last - gam)
    return q_dec, k_dec, jnp.exp(gam_last)


def _scan_group(s):
    return 2 if (s // CHUNK) % 2 == 0 else 1


def _dn_scan(q, k, u, w, aqk, gam):
    h, s, _ = q.shape
    nch = s // CHUNK
    sg = _scan_group(s)
    rb = sg * CHUNK

    def body(q_ref, k_ref, u_ref, w_ref, a_ref, gam_ref, o_ref, st_ref, state):
        @pl.when(pl.program_id(0) == 0)
        def _():
            state[...] = jnp.zeros_like(state)

        def head(hh, c):
            sl = slice(c * CHUNK, (c + 1) * CHUNK)
            s0 = state[hh]
            st_ref[c, hh] = s0
            q_dec, k_dec, gl = _chunk_scaled(q_ref[hh, sl, :], k_ref[hh, sl, :], gam_ref[hh, sl, :])
            both = _mdot(jnp.concatenate([w_ref[hh, sl, :], q_dec], axis=0), s0)
            yield
            v_new = u_ref[hh, sl, :] - both[:CHUNK]
            o_ref[sl, hh * HEAD_DIM:(hh + 1) * HEAD_DIM] = both[CHUNK:] + _mdot(a_ref[hh, sl, :], v_new)
            state[hh] = s0 * gl + _mdot(k_dec, v_new, TN)

        for c in range(sg):
            _lockstep([head(hh, c) for hh in range(h)])

    big = lambda x: (x, (h, rb, HEAD_DIM), lambda n: (0, n, 0))
    return _call(body, grid=(nch // sg,),
                 ins=[big(q), big(k), big(u), big(w), (aqk, (h, rb, CHUNK), lambda n: (0, n, 0)),
                      (gam, (h, rb, 1), lambda n: (0, n, 0))],
                 outs=[((s, h * HEAD_DIM), F32, (rb, h * HEAD_DIM), lambda n: (n, 0)),
                       ((nch, h, HEAD_DIM, HEAD_DIM), F32, (sg, h, HEAD_DIM, HEAD_DIM), lambda n: (n, 0, 0, 0))],
                 name="dn_scan", scratch=[pltpu.VMEM((h, HEAD_DIM, HEAD_DIM), F32)])


def _dn_scan_bwd(q, k, u, w, aqk, gam, states, do):
    h, s, _ = q.shape
    nch = s // CHUNK
    sg = _scan_group(s)
    rb = sg * CHUNK
    ngr = nch // sg

    def body(q_ref, k_ref, u_ref, w_ref, a_ref, gam_ref, st_ref, do_ref,
             du_ref, dw_ref, da_ref, dqd_ref, dkd_ref, dgl_ref, dstate):
        @pl.when(pl.program_id(0) == 0)
        def _():
            dstate[...] = jnp.zeros_like(dstate)

        def head(hh, c):
            sl = slice(c * CHUNK, (c + 1) * CHUNK)
            s0 = st_ref[c, hh]
            ds = dstate[hh]
            doh = do_ref[sl, hh * HEAD_DIM:(hh + 1) * HEAD_DIM]
            wv = w_ref[hh, sl, :]
            q_dec, k_dec, gl = _chunk_scaled(q_ref[hh, sl, :], k_ref[hh, sl, :], gam_ref[hh, sl, :])
            ws = _mdot(wv, s0)
            dv_new = _mdot(a_ref[hh, sl, :], doh, TN) + _mdot(k_dec, ds)
            dqd_ref[hh, sl, :] = _mdot(doh, s0, NT)
            qdo = _mdot(q_dec, doh, TN)
            tot = jnp.sum(jnp.sum(s0 * ds, axis=-1, keepdims=True), axis=0, keepdims=True)
            dgl_ref[hh, sl, :] = jnp.broadcast_to(tot, (CHUNK, 1))
            yield
            v_new = u_ref[hh, sl, :] - ws
            du_ref[hh, sl, :] = dv_new
            dw_ref[hh, sl, :] = -_mdot(dv_new, s0, NT)
            da_ref[hh, sl, :] = _mdot(doh, v_new, NT)
            dkd_ref[hh, sl, :] = _mdot(v_new, ds, NT)
            dstate[hh] = ds * gl + qdo - _mdot(wv, dv_new, TN)

        for c in range(sg - 1, -1, -1):
            _lockstep([head(hh, c) for hh in range(h)])

    rev = lambda n: (0, ngr - 1 - n, 0)
    big = lambda x: (x, (h, rb, HEAD_DIM), rev)
    o_big = ((h, s, HEAD_DIM), F32, (h, rb, HEAD_DIM), rev)
    return _call(body, grid=(ngr,),
                 ins=[big(q), big(k), big(u), big(w), (aqk, (h, rb, CHUNK), rev), (gam, (h, rb, 1), rev),
                      (states, (sg, h, HEAD_DIM, HEAD_DIM), lambda n: (ngr - 1 - n, 0, 0, 0)),
                      (do, (rb, h * HEAD_DIM), lambda n: (ngr - 1 - n, 0))],
                 outs=[o_big, o_big, ((h, s, CHUNK), F32, (h, rb, CHUNK), rev), o_big, o_big,
                       ((h, s, 1), F32, (h, rb, 1), rev)],
                 name="dn_scan_bwd", scratch=[pltpu.VMEM((h, HEAD_DIM, HEAD_DIM), F32)])


def _gates(x, a_log, dt_b, h):
    lane = lax.broadcasted_iota(jnp.int32, x.shape, 1)
    return jnp.where(lane < h, jax.nn.sigmoid(x), -jnp.exp(a_log) * jax.nn.softplus(x + dt_b))


def _head_out(oh, zh, nw):
    on = oh * lax.rsqrt(jnp.mean(oh * oh, axis=-1, keepdims=True) + RMS_EPS) * nw
    return on * jax.nn.silu(zh)


def _pad_lanes(x, lo):
    return jnp.zeros((1, LANES), F32).at[0, lo:lo + x.shape[0]].set(x)


def _deltanet_fwd(hin, get_w_in, conv_w, a_log, dt_bias, norm_w, get_w_out):
    h = a_log.shape[0]
    hw = h * HEAD_DIM
    w_in = get_w_in(hin)
    proj = _mm_nn(hin, w_in, F32, "dn_proj")
    q = _dn_conv_fwd(proj, conv_w, 0, h, True, "dn_conv_q")
    k = _dn_conv_fwd(proj, conv_w, h, h, True, "dn_conv_k")
    v = _dn_conv_fwd(proj, conv_w, 2 * h, h, False, "dn_conv_v")
    alp, dtp = _pad_lanes(a_log, h), _pad_lanes(dt_bias, h)

    def gates_fn(x, al, db):
        return (_gates(x, al, db, h),), ()

    (bg,), _ = _rowmap(gates_fn, [(proj, LANES, 4 * h)], [alp, dtp], [(LANES, F32)], [], "dn_gates")
    u, w, aqk, t, gam = _dn_prep(q, k, v, bg)
    o, states = _dn_scan(q, k, u, w, aqk, gam)
    nw = norm_w[None, :]

    def out_fn(o, z, nw):
        parts = [_head_out(o[:, i * HEAD_DIM:(i + 1) * HEAD_DIM], z[:, i * HEAD_DIM:(i + 1) * HEAD_DIM], nw)
                 for i in range(h)]
        return (jnp.concatenate(parts, axis=-1),), ()

    (og,), _ = _rowmap(out_fn, [o, (proj, hw, 3)], [nw], [(hw, MXU_DTYPE)], [], "dn_out")
    w_out = get_w_out(og)
    y = _mm_nn(og, w_out, F32, "dn_y")
    return y, (hin, proj, q, k, v, bg, u, w, aqk, t, gam, states, o, og, alp, dtp, nw, w_in, w_out)


def _deltanet_bwd(res, dy, conv_w, send):
    hin, proj, q, k, v, bg, u, w, aqk, t, gam, states, o, og, alp, dtp, nw, w_in, w_out = res
    h = q.shape[0]
    hw = h * HEAD_DIM
    s = hin.shape[0]
    d_w_out = _mm_tn(og, dy, MXU_DTYPE, "dn_dwout")
    dog = _mm_nt(dy, w_out, F32, "dn_dog")

    def out_bwd(o, z, dog, nw):
        dos, dzs = [], []
        dn = jnp.zeros((1, HEAD_DIM), F32)
        for i in range(h):
            sl = slice(i * HEAD_DIM, (i + 1) * HEAD_DIM)
            _, vjp = jax.vjp(_head_out, o[:, sl], z[:, sl], nw)
            a, b, c = vjp(dog[:, sl])
            dos.append(a)
            dzs.append(b)
            dn = dn + c
        return (jnp.concatenate(dos, axis=-1), jnp.concatenate(dzs, axis=-1)), (dn,)

    (do, dz), (d_norm_w,) = _rowmap(out_bwd, [o, (proj, hw, 3), dog], [nw], [(hw, F32), (hw, MXU_DTYPE)],
                                    [(1, HEAD_DIM)], "dn_out_bwd")
    du, dw, daqk, dqd, dkd, dgl = _dn_scan_bwd(q, k, u, w, aqk, gam, states, do)
    dq, dk, dv, dbg = _dn_prep_bwd(q, k, v, bg, gam, t, du, dw, daqk, dqd, dkd, dgl)
    dpq, dwq = _dn_conv_bwd(proj, conv_w, dq, 0, h, True, "dn_conv_q_bwd")
    dpk, dwk = _dn_conv_bwd(proj, conv_w, dk, h, h, True, "dn_conv_k_bwd")
    dpv, dwv = _dn_conv_bwd(proj, conv_w, dv, 2 * h, h, False, "dn_conv_v_bwd")

    def gates_bwd(x, dbg, al, db):
        _, vjp = jax.vjp(functools.partial(_gates, h=h), x, al, db)
        gx, gal, gdb = vjp(dbg)
        return (gx,), (gal, gdb)

    (dba,), (d_alp, d_dtp) = _rowmap(gates_bwd, [(proj, LANES, 4 * h), dbg], [alp, dtp], [(LANES, MXU_DTYPE)],
                                     [(1, LANES), (1, LANES)], "dn_gates_bwd")
    dproj = jnp.concatenate([dpq, dpk, dpv, dz, dba], axis=1)
    d_w_in = _mm_tn(hin, dproj, MXU_DTYPE, "dn_dwin")
    token = send(d_w_in, d_w_out)
    dh = _mm_nt(dproj, w_in, F32, "dn_dh", pin=token)
    d_conv_w = jnp.concatenate([dwq, dwk, dwv], axis=1)
    return dh, dict(conv_w=d_conv_w, a_log=d_alp[0, h:2 * h], dt_bias=d_dtp[0, h:2 * h], norm_w=d_norm_w[0]), token


def _ln_silu(u, g, b):
    return jax.nn.silu(_ln(u, g, b))


def _conformer_fwd(hin, get_w_in, dw_w, dw_b, ln_g, ln_b, get_w_out):
    w_in = get_w_in(hin)
    vg = _mm_nn(hin, w_in, F32, "cf_vg")
    u1 = _cf_conv_fwd(vg, dw_w, dw_b)
    ch = u1.shape[1]

    def fn(u, g, b):
        return (_ln_silu(u, g, b),), ()

    (u2,), _ = _rowmap(fn, [u1], [ln_g, ln_b], [(ch, MXU_DTYPE)], [], "cf_ln")
    w_out = get_w_out(u2)
    y = _mm_nn(u2, w_out, F32, "cf_y")
    return y, (hin, vg, u1, u2, w_in, w_out)


def _conformer_bwd(res, dy, dw_w, ln_g, ln_b):
    hin, vg, u1, u2, w_in, w_out = res
    ch = u1.shape[1]
    d_w_out = _mm_tn(u2, dy, MXU_DTYPE, "cf_dwout")
    du2 = _mm_nt(dy, w_out, F32, "cf_du2")

    def fn(u, du2, g, b):
        _, vjp = jax.vjp(_ln_silu, u, g, b)
        gu, gg, gb = vjp(du2)
        return (gu,), (gg, gb)

    (du1,), (d_ln_g, d_ln_b) = _rowmap(fn, [u1, du2], [ln_g, ln_b], [(ch, F32)], [(1, ch), (1, ch)], "cf_ln_bwd")
    dval, dgate, d_dw_w, d_dw_b = _cf_conv_bwd(vg, dw_w, du1)
    dvg = jnp.concatenate([dval, dgate], axis=1)
    d_w_in = _mm_tn(hin, dvg, MXU_DTYPE, "cf_dwin", split_cols=True)
    dh = _mm_nt(dvg, w_in, F32, "cf_dh")
    return dh, dict(w_in=d_w_in, w_out=d_w_out, dw_w=d_dw_w, dw_b=d_dw_b[0], ln_g=d_ln_g[0], ln_b=d_ln_b[0])


def _mlp_fwd(hin, get_w1, get_w2):
    w1 = get_w1(hin)
    r = _mm_nn(hin, w1, MXU_DTYPE, "ff_a", relu2=True)
    w2 = get_w2(r)
    m = _mm_nn(r, w2, F32, "ff_m")
    return m, (hin, r, w1, w2)


def _mlp_bwd(res, dm):
    hin, r, w1, w2 = res
    d_w2 = _mm_tn(r, dm, MXU_DTYPE, "ff_dw2")
    da = _mm_nt(dm, w2, MXU_DTYPE, "ff_da", relu2_sq=r)
    d_w1 = _mm_tn(hin, da, MXU_DTYPE, "ff_dw1", split_cols=True)
    dh = _mm_nt(da, w1, F32, "ff_dh")
    return dh, d_w1, d_w2


def _ada_fwd(c_all, ada_w):
    depth, d, nl = ada_w.shape
    tn = _tile(nl, 256)

    def body(c_ref, w_ref, o_ref, cond_ref):
        cond = jax.nn.silu(c_ref[...]).astype(MXU_DTYPE)
        cond_ref[...] = cond
        o_ref[...] = lax.dot_general(cond, w_ref[...].astype(MXU_DTYPE), (NN, ((), ())), preferred_element_type=F32)

    return _call(body, grid=(depth, nl // tn),
                 ins=[(c_all, c_all.shape, lambda l, j: (0, 0)), (ada_w, (None, d, tn), lambda l, j: (l, 0, j))],
                 outs=[((depth, N_DEV, nl), F32, (None, N_DEV, tn), lambda l, j: (l, 0, j)),
                       (c_all.shape, MXU_DTYPE, c_all.shape, lambda l, j: (0, 0))],
                 name="ada_fwd")


def _ada_bwd(cond_all, dmod_cols):
    depth, _, nl = dmod_cols.shape
    d = cond_all.shape[1]
    tn = _tile(nl, 256)

    def body(c_ref, g_ref, o_ref):
        o_ref[...] = lax.dot_general(c_ref[...], g_ref[...].astype(MXU_DTYPE), (TN, ((), ())),
                                     preferred_element_type=F32)

    return _call(body, grid=(depth, nl // tn),
                 ins=[(cond_all, cond_all.shape, lambda l, j: (0, 0)), (dmod_cols, (None, N_DEV, tn), lambda l, j: (l, 0, j))],
                 outs=[((depth, d, nl), F32, (None, d, tn), lambda l, j: (l, 0, j))], name="ada_bwd")[0]


def _peers():
    x, y, c = lax.axis_index("x"), lax.axis_index("y"), lax.axis_index("c")
    peers = []
    for k in range(1, N_DEV):
        px = 1 - x if k & 4 else x
        py = 1 - y if k & 2 else y
        pc = 1 - c if k & 1 else c
        peers.append(((px, py, pc), 4 * px + 2 * py + pc))
    return 4 * x + 2 * y + c, peers


_HBM = pl.BlockSpec(memory_space=pltpu.HBM)
_SEM = pl.BlockSpec(memory_space=pltpu.SEMAPHORE)
_ANY = pl.BlockSpec(memory_space=pl.ANY)
_EFFECT = pltpu.SideEffectType.DATAFLOW_SIDE_EFFECTING


def _xfer_start(srcs, lands, scatter, after, name):
    nt = len(srcs)

    def body(*refs):
        src, land = refs[:nt], refs[nt:2 * nt]
        sems = refs[2 * nt + 1:4 * nt + 1]
        token = refs[-1]
        me, peers = _peers()
        for t in range(nt):
            for k, (pid, plin) in enumerate(peers):
                pltpu.make_async_remote_copy(
                    src_ref=src[t].at[plin] if scatter else src[t], dst_ref=land[t].at[me],
                    send_sem=sems[2 * t].at[k], recv_sem=sems[2 * t + 1].at[k],
                    device_id=pid, device_id_type=pl.DeviceIdType.MESH).start()
        token[...] = jnp.zeros_like(token)

    out_shape = [pltpu.SemaphoreType.DMA((N_DEV - 1,)) for _ in range(2 * nt)]
    out_shape += [pltpu.HBM(a.shape, a.dtype) for a in lands]
    out_shape += [jax.ShapeDtypeStruct((8, LANES), F32)]
    srcs = [pltpu.with_memory_space_constraint(a, pltpu.HBM) for a in srcs]
    res = pl.pallas_call(
        body, name=name, out_shape=out_shape,
        in_specs=[_HBM] * (2 * nt) + [_ANY],
        out_specs=[_SEM] * (2 * nt) + [_HBM] * nt + [pl.BlockSpec(memory_space=pltpu.VMEM)],
        input_output_aliases={nt + i: 2 * nt + i for i in range(nt)},
        compiler_params=pltpu.CompilerParams(has_side_effects=_EFFECT),
    )(*srcs, *[pltpu.with_memory_space_constraint(a, pltpu.HBM) for a in lands], after)
    sems, thru = res[:2 * nt], res[2 * nt:3 * nt]
    return [(sems[2 * t], sems[2 * t + 1], srcs[t], thru[t]) for t in range(nt)], res[-1]


def _xfer_wait(handle, scatter, after, name):
    send, recv, src, land = handle

    def body(src_ref, land_ref, send_sem, recv_sem, after_ref, land_out):
        _, peers = _peers()
        for k, (pid, plin) in enumerate(peers):
            cp = pltpu.make_async_remote_copy(
                src_ref=src_ref.at[plin] if scatter else src_ref, dst_ref=land_ref.at[plin],
                send_sem=send_sem.at[k], recv_sem=recv_sem.at[k],
                device_id=pid, device_id_type=pl.DeviceIdType.MESH)
            cp.wait_send()
            cp.wait_recv()

    return pl.pallas_call(
        body, name=name, out_shape=pltpu.HBM(land.shape, land.dtype),
        in_specs=(_HBM, _HBM, _SEM, _SEM, _ANY), out_specs=_HBM, input_output_aliases={1: 0},
        compiler_params=pltpu.CompilerParams(has_side_effects=_EFFECT),
    )(src, land, send, recv, after)


def _landing(x, me):
    return lax.dynamic_update_slice(lax.empty((N_DEV,) + x.shape, x.dtype), x[None], (me,) + (0,) * x.ndim)


def _chip_peers():
    x, y, c = lax.axis_index("x"), lax.axis_index("y"), lax.axis_index("c")
    lin = lambda px, py, pc: 4 * px + 2 * py + pc
    sibling = ((x, y, 1 - c), lin(x, y, 1 - c))
    chips = [((1 - x, y, c), lin(1 - x, y, c)), ((x, 1 - y, c), lin(x, 1 - y, c)),
             ((1 - x, 1 - y, c), lin(1 - x, 1 - y, c))]
    return lin(x, y, c), sibling, chips


N_CHIPS_OTHER = 3


def _gather2_start(srcs, lands, after, name):
    nt = len(srcs)

    def body(*refs):
        src, land = refs[:nt], refs[nt:2 * nt]
        sems = refs[2 * nt + 1:5 * nt + 1]
        token = refs[-1]
        me, sibling, chips = _chip_peers()
        for t in range(nt):
            send, recv_ici, recv_sib = sems[3 * t], sems[3 * t + 1], sems[3 * t + 2]
            pltpu.make_async_remote_copy(src_ref=src[t], dst_ref=land[t].at[me], send_sem=send.at[0],
                                         recv_sem=recv_sib.at[0], device_id=sibling[0],
                                         device_id_type=pl.DeviceIdType.MESH).start()
            for j, (pid, _) in enumerate(chips):
                pltpu.make_async_remote_copy(src_ref=src[t], dst_ref=land[t].at[me], send_sem=send.at[1 + j],
                                             recv_sem=recv_ici.at[j], device_id=pid,
                                             device_id_type=pl.DeviceIdType.MESH).start()
        token[...] = jnp.zeros_like(token)

    out_shape = []
    for _ in range(nt):
        out_shape += [pltpu.SemaphoreType.DMA((1 + N_CHIPS_OTHER,)), pltpu.SemaphoreType.DMA((N_CHIPS_OTHER,)),
                      pltpu.SemaphoreType.DMA((1,))]
    out_shape += [pltpu.HBM(a.shape, a.dtype) for a in list(srcs) + list(lands)]
    out_shape += [jax.ShapeDtypeStruct((8, LANES), F32)]
    res = pl.pallas_call(
        body, name=name, out_shape=out_shape,
        in_specs=[_HBM] * (2 * nt) + [_ANY],
        out_specs=[_SEM] * (3 * nt) + [_HBM] * (2 * nt) + [pl.BlockSpec(memory_space=pltpu.VMEM)],
        input_output_aliases={i: 3 * nt + i for i in range(2 * nt)},
        compiler_params=pltpu.CompilerParams(has_side_effects=_EFFECT),
    )(*[pltpu.with_memory_space_constraint(a, pltpu.HBM) for a in list(srcs) + list(lands)], after)
    sems, thru = res[:3 * nt], res[3 * nt:5 * nt]
    return [(sems[3 * t], sems[3 * t + 1], sems[3 * t + 2], thru[t], thru[nt + t]) for t in range(nt)], res[-1]


def _gather2_relay(handles, after, name):
    nt = len(handles)
    after = list(after) if isinstance(after, (list, tuple)) else [after]

    def body(*refs):
        src, land = refs[:nt], refs[nt:2 * nt]
        send1, recv_ici = refs[2 * nt:3 * nt], refs[3 * nt:4 * nt]
        outs = refs[4 * nt + len(after):]
        send2, recv2 = outs[:nt], outs[nt:2 * nt]
        token = refs[-1]
        token[...] = jnp.zeros_like(token)
        me, sibling, chips = _chip_peers()
        for t in range(nt):
            pltpu.make_async_remote_copy(src_ref=src[t], dst_ref=land[t].at[me], send_sem=send1[t].at[0],
                                         recv_sem=recv_ici[t].at[0], device_id=sibling[0],
                                         device_id_type=pl.DeviceIdType.MESH).wait_send()
            for j, (pid, plin) in enumerate(chips):
                arrived = pltpu.make_async_remote_copy(src_ref=src[t], dst_ref=land[t].at[plin], send_sem=send1[t].at[1 + j],
                                                       recv_sem=recv_ici[t].at[j], device_id=pid,
                                                       device_id_type=pl.DeviceIdType.MESH)
                arrived.wait_send()
                arrived.wait_recv()
                pltpu.make_async_remote_copy(src_ref=land[t].at[plin], dst_ref=land[t].at[plin], send_sem=send2[t].at[j],
                                             recv_sem=recv2[t].at[j], device_id=sibling[0],
                                             device_id_type=pl.DeviceIdType.MESH).start()

    srcs = [h[3] for h in handles]
    lands = [h[4] for h in handles]
    out_shape = [pltpu.SemaphoreType.DMA((N_CHIPS_OTHER,)) for _ in range(2 * nt)]
    out_shape += [pltpu.HBM(a.shape, a.dtype) for a in srcs + lands]
    out_shape += [jax.ShapeDtypeStruct((8, LANES), F32)]
    res = pl.pallas_call(
        body, name=name, out_shape=out_shape,
        in_specs=[_HBM] * (2 * nt) + [_SEM] * (2 * nt) + [_ANY] * len(after),
        out_specs=[_SEM] * (2 * nt) + [_HBM] * (2 * nt) + [pl.BlockSpec(memory_space=pltpu.VMEM)],
        input_output_aliases={i: 2 * nt + i for i in range(2 * nt)},
        compiler_params=pltpu.CompilerParams(has_side_effects=_EFFECT),
    )(*srcs, *lands, *[h[0] for h in handles], *[h[1] for h in handles], *after)
    return [(handles[t][2], res[t], res[nt + t], res[3 * nt + t]) for t in range(nt)], res[-1]


def _gather2_wait(handle, after, name):
    recv_sib, send2, recv2, land = handle

    def body(land_ref, recv_sib_sem, send2_sem, recv2_sem, after_ref, land_out):
        me, sibling, chips = _chip_peers()
        pltpu.make_async_remote_copy(src_ref=land_ref.at[me], dst_ref=land_ref.at[sibling[1]], send_sem=send2_sem.at[0],
                                     recv_sem=recv_sib_sem.at[0], device_id=sibling[0],
                                     device_id_type=pl.DeviceIdType.MESH).wait_recv()
        for j, (pid, plin) in enumerate(chips):
            relayed = pltpu.make_async_remote_copy(src_ref=land_ref.at[plin], dst_ref=land_ref.at[plin], send_sem=send2_sem.at[j],
                                                   recv_sem=recv2_sem.at[j], device_id=sibling[0],
                                                   device_id_type=pl.DeviceIdType.MESH)
            relayed.wait_send()
            relayed.wait_recv()

    return pl.pallas_call(
        body, name=name, out_shape=pltpu.HBM(land.shape, land.dtype),
        in_specs=(_HBM, _SEM, _SEM, _SEM, _ANY), out_specs=_HBM, input_output_aliases={0: 0},
        compiler_params=pltpu.CompilerParams(has_side_effects=_EFFECT),
    )(land, recv_sib, send2, recv2, after)


def _exchange(arrs, scatter, name):
    nt = len(arrs)
    out_shape = [jax.ShapeDtypeStruct(a.shape if scatter else (N_DEV,) + a.shape, a.dtype) for a in arrs]

    def body(*refs):
        ins, outs = refs[:nt], refs[nt:2 * nt]
        send, recv, loc = refs[2 * nt:]
        me, peers = _peers()
        copies = []
        for t in range(nt):
            own = pltpu.make_async_copy(ins[t].at[me] if scatter else ins[t], outs[t].at[me], loc.at[t])
            own.start()
            copies.append(own)
            for k, (pid, plin) in enumerate(peers):
                cp = pltpu.make_async_remote_copy(
                    src_ref=ins[t].at[plin] if scatter else ins[t], dst_ref=outs[t].at[me],
                    send_sem=send.at[t, k], recv_sem=recv.at[t, k],
                    device_id=pid, device_id_type=pl.DeviceIdType.MESH)
                cp.start()
                copies.append(cp)
        for cp in copies:
            cp.wait()

    any_spec = pl.BlockSpec(memory_space=pl.ANY)
    return pl.pallas_call(
        body, out_shape=out_shape, in_specs=[any_spec] * nt, out_specs=[any_spec] * nt,
        scratch_shapes=[pltpu.SemaphoreType.DMA((nt, N_DEV - 1)), pltpu.SemaphoreType.DMA((nt, N_DEV - 1)),
                        pltpu.SemaphoreType.DMA((nt,))],
        name=name)(*arrs)


def _adamw_body(n_parts, stacked=True):
    def body(p_ref, w_ref, m_ref, v_ref, *rest):
        g_out, d_out, m_out, v_out = rest[-4:]
        part = (lambda i: p_ref[i]) if stacked else (lambda i: p_ref[i][...])
        g = part(0).astype(F32)
        for i in range(1, n_parts):
            g = g + part(i).astype(F32)
        m2 = ADAM_B1 * m_ref[...] + (1.0 - ADAM_B1) * g
        v2 = ADAM_B2 * v_ref[...] + (1.0 - ADAM_B2) * jnp.square(g)
        m_hat = m2 / (1.0 - ADAM_B1 ** ADAM_STEP)
        v_hat = v2 / (1.0 - ADAM_B2 ** ADAM_STEP)
        g_out[...] = g
        d_out[...] = -ADAM_LR * (m_hat / (jnp.sqrt(v_hat) + ADAM_EPS) + ADAM_WD * w_ref[...])
        m_out[...] = m2
        v_out[...] = v2

    return body


def _adamw_layer(own, land, me, w, m, v, layer, prev, name):
    _, r, c = own.shape
    tr = _tile(r, 256, 8)
    blk = pl.BlockSpec((None, tr, c), lambda i, me_ref: (layer, i, 0))
    share = lambda k: pl.BlockSpec((None, tr, c), lambda i, me_ref: (me_ref[0] ^ k, i, 0))
    in_specs = [share(k) for k in range(N_DEV)] + [blk, blk, blk]
    args = [own] + [land] * (N_DEV - 1) + [w, m, v]
    aliases = {}
    if prev is not None:
        in_specs += [_ANY] * 4
        args += list(prev)
        aliases = {1 + N_DEV + 3 + i: i for i in range(4)}

    def body(me_ref, *refs):
        token_ref = refs[-1]
        refs = (refs[:N_DEV],) + refs[N_DEV:-1]
        _adamw_body(N_DEV, stacked=False)(*refs)
        token_ref[...] = jnp.zeros(token_ref.shape, F32)

    token_blk = pl.BlockSpec((8, LANES), lambda i, me_ref: (0, 0))
    res = pl.pallas_call(
        body,
        grid_spec=pltpu.PrefetchScalarGridSpec(num_scalar_prefetch=1, grid=(r // tr,), in_specs=in_specs,
                                               out_specs=[blk] * 4 + [token_blk]),
        out_shape=[jax.ShapeDtypeStruct(w.shape, F32)] * 4 + [jax.ShapeDtypeStruct((8, LANES), F32)],
        input_output_aliases=aliases, name=name, compiler_params=_cparams(1))(me, *args)
    return res[:4], res[4]


def _adamw(parts, w, m, v, name):
    p, nl, r, c = parts.shape
    tr = _tile(r, 256, 8)
    body = _adamw_body(p)

    blk = (None, tr, c)
    imap = lambda l, i: (l, i, 0)
    out = ((nl, r, c), F32, blk, imap)
    return _call(body, grid=(nl, r // tr),
                 ins=[(parts, (p, None, tr, c), lambda l, i: (0, l, i, 0)), (w, blk, imap), (m, blk, imap), (v, blk, imap)],
                 outs=[out] * 4, name=name)


def _adamw_small(gathered, me, w, m, v, shapes, offs, src_offs, groups, name):
    r = w.shape[0]
    n_results = 4

    def body(me_ref, g_ref, w_ref, m_ref, v_ref, *rest):
        outs, (p_ref, *packed) = rest[:n_results * len(shapes)], rest[n_results * len(shapes):]
        if r > offs[-1]:
            p_ref[:, pl.ds(offs[-1], r - offs[-1]), :] = jnp.zeros((N_DEV, r - offs[-1], LANES), F32)
        for n, grp in enumerate(groups):
            rows = offs[n + 1] - offs[n]
            if grp is None:
                p_ref[:, pl.ds(offs[n], rows), :] = g_ref[:, pl.ds(src_offs[n], rows), :]
            else:
                for lead in range(rows // grp):
                    src = src_offs[n] + (lead * N_DEV + me_ref[0]) * grp
                    p_ref[:, pl.ds(offs[n] + lead * grp, grp), :] = g_ref[:, pl.ds(src, grp), :]
        _adamw_body(N_DEV)(p_ref, w_ref, m_ref, v_ref, *packed)
        for n, shp in enumerate(shapes):
            per_lead, width = max(shp[-1] // LANES, 1), min(shp[-1], LANES)
            for row in range(offs[n + 1] - offs[n]):
                lead = np.unravel_index(row // per_lead, shp[:-1])
                at = tuple(int(i) for i in lead[:-1]) + (pl.ds(int(lead[-1]), 1), pl.ds((row % per_lead) * LANES, width))
                for k in range(n_results):
                    outs[n_results * n + k][at] = packed[k][pl.ds(offs[n] + row, 1), pl.ds(0, width)]

    whole = lambda shape: pl.BlockSpec(shape, functools.partial(lambda i, me_ref, nd: (0,) * nd, nd=len(shape)))
    return pl.pallas_call(
        body,
        grid_spec=pltpu.PrefetchScalarGridSpec(
            num_scalar_prefetch=1, grid=(1,), in_specs=[whole(a.shape) for a in (gathered, w, m, v)],
            out_specs=[whole(shp) for shp in shapes for _ in range(n_results)],
            scratch_shapes=[pltpu.VMEM((N_DEV, r, LANES), F32)] + [pltpu.VMEM((r, LANES), F32)] * n_results),
        out_shape=[jax.ShapeDtypeStruct(shp, F32) for shp in shapes for _ in range(n_results)],
        name=name, compiler_params=_cparams(1))(me, gathered, w, m, v)


def _rows(x):
    return x.reshape(-1, LANES)


def _pad_rows(x, mult=8):
    r = x.shape[0]
    extra = (-r) % mult
    return jnp.pad(x, ((0, extra), (0, 0))) if extra else x


def kernel(x, c, ada_w, ada_b, ln_g, ln_b, dn_w_in, dn_conv_w, dn_a_log, dn_dt_bias, dn_norm_w, dn_w_out, cf_w_in, cf_dw_w, cf_dw_b, cf_ln_g, cf_ln_b, cf_w_out, ff_w1, ff_w2, loss_target, m_ada_w, m_ada_b, m_ln_g, m_ln_b, m_dn_w_in, m_dn_conv_w, m_dn_a_log, m_dn_dt_bias, m_dn_norm_w, m_dn_w_out, m_cf_w_in, m_cf_dw_w, m_cf_dw_b, m_cf_ln_g, m_cf_ln_b, m_cf_w_out, m_ff_w1, m_ff_w2, v_ada_w, v_ada_b, v_ln_g, v_ln_b, v_dn_w_in, v_dn_conv_w, v_dn_a_log, v_dn_dt_bias, v_dn_norm_w, v_dn_w_out, v_cf_w_in, v_cf_dw_w, v_cf_dw_b, v_cf_ln_g, v_cf_ln_b, v_cf_w_out, v_ff_w1, v_ff_w2):
    depth, d, _ = ada_w.shape
    n_a, n_b = dn_w_in.shape[0], cf_w_in.shape[0]
    heads = dn_a_log.shape[1]
    hw = heads * HEAD_DIM
    taps = cf_dw_w.shape[1]
    s = x.shape[1]
    alpha = (2.0 * depth) ** 0.25
    me = 4 * lax.axis_index("x") + 2 * lax.axis_index("y") + lax.axis_index("c")
    me_arr = jnp.reshape(me, (1,)).astype(jnp.int32)
    xs, tgt = x[0], loss_target[0]

    dn_in_cols = dn_w_in.shape[2]
    keys, shards = [], []
    for i in range(depth):
        j = i // 2
        mixer = [("dn_in", dn_w_in), ("dn_out", dn_w_out)] if i % 2 == 0 else [("cf_in", cf_w_in), ("cf_out", cf_w_out)]
        for nm, wt in mixer:
            keys.append((nm, j))
            shards.append(wt[j].astype(MXU_DTYPE))
        keys += [("ff1", i), ("ff2", i)]
        shards += [ff_w1[i].astype(MXU_DTYPE), ff_w2[i].astype(MXU_DTYPE)]

    small_local = [_rows(ln_g), _rows(ln_b), _rows(dn_conv_w), _rows(cf_dw_w), _rows(cf_dw_b), _rows(cf_ln_g),
                   _rows(cf_ln_b), _rows(c)]
    sizes = [a.shape[0] for a in small_local]
    packed = _pad_rows(jnp.concatenate(small_local, axis=0))
    (small_all,) = _exchange([packed], False, "comm_gather_params")
    offs = [0]
    for z in sizes:
        offs.append(offs[-1] + z)

    def small(i):
        return small_all[:, offs[i]:offs[i + 1], :]

    def unshard(piece, lead, groups):
        t = piece.reshape((N_DEV,) + lead + (groups * LANES,))
        t = jnp.moveaxis(t, 0, len(lead))
        return t.reshape(lead + (N_DEV * groups * LANES,))

    ln_g_f = unshard(small(0), (depth, 2), 1)
    ln_b_f = unshard(small(1), (depth, 2), 1)
    conv_w_f = unshard(small(2), (n_a, DN_CONV), 3 * heads // N_DEV)
    dw_w_f = unshard(small(3), (n_b, taps), 1)
    dw_b_f = unshard(small(4), (n_b,), 1)
    cf_ln_g_f = unshard(small(5), (n_b,), 1)
    cf_ln_b_f = unshard(small(6), (n_b,), 1)
    c_all = small(7).reshape(N_DEV, d)

    mod_part, cond_all = _ada_fwd(c_all, ada_w)
    (mod_all,) = _exchange([mod_part], False, "comm_gather_mod")
    mod_mine = lax.dynamic_index_in_dim(mod_all, me, axis=2, keepdims=False)
    mod_mine = jnp.moveaxis(mod_mine, 0, 1).reshape(depth, N_MOD * d)

    lands = [_landing(a, me) for a in shards]
    first, token = _gather2_start(shards[:1], lands[:1], mod_all, "gather_first_weight_start")
    handles = {keys[0]: first[0]}
    groups = [keys[:1], keys[1:4]] + [keys[4 * i:4 * i + 4] for i in range(1, depth)]
    group_of = {k: n for n, grp in enumerate(groups) for k in grp}
    relayed, weights = {}, {}
    wait_after = {}

    def relay(n, after):
        if n < len(groups) and groups[n][0] not in relayed:
            hs, relay_token = _gather2_relay([handles[k] for k in groups[n]], after, "gather_relay_%d" % n)
            relayed.update(zip(groups[n], hs))
            return relay_token

    def gathered(key, after):
        if key not in weights:
            relay(group_of[key], after)
            if key[0] == "ff1":
                relay(key[1] + 2, after)
            weights[key] = _gather2_wait(relayed[key], wait_after.get(key, after), "gather_wait_%s_%d" % key)
        return weights[key]

    def get_dn_in(j):
        def get(after):
            g = gathered(("dn_in", j), after)
            w = jnp.moveaxis(g, 0, 1).reshape(d, N_DEV * dn_in_cols)
            return jnp.pad(w, ((0, 0), (0, 4 * hw + LANES - N_DEV * dn_in_cols)))
        return get

    def get_rows(key):
        return lambda after: gathered(key, after).reshape((-1, d))

    def get_cols(key):
        return lambda after: gathered(key, after)

    def add_bias(a, b):
        return (a + b,), ()

    (mod,), _ = _rowmap(add_bias, [mod_mine, ada_b], [], [(N_MOD * d, F32)], [], "ada_bias", pin=token)
    mod_rows = mod.reshape(depth * N_MOD, 1, d)
    ln_g_rows = ln_g_f.reshape(depth * 2, 1, d)
    ln_b_rows = ln_b_f.reshape(depth * 2, 1, d)

    def mod_row(i, j):
        return (mod_rows, i * N_MOD + j)

    def ln_row(rows, i, j):
        return (rows, i * 2 + j)

    subs = []
    h_cur = _modulate_fwd(xs, mod_row(0, 1), mod_row(0, 0))
    relay_token = relay(0, [h_cur, m_dn_w_in, v_dn_w_in])
    rest, wait_after[keys[0]] = _gather2_start(shards[1:], lands[1:], relay_token, "gather_weights_start")
    handles.update(zip(keys[1:], rest))
    x_cur = xs
    last = None
    for i in range(depth):
        j = i // 2
        if i % 2 == 0:
            y, res = _deltanet_fwd(h_cur, get_dn_in(j), conv_w_f[j], dn_a_log[j], dn_dt_bias[j], dn_norm_w[j],
                                   get_rows(("dn_out", j)))
        else:
            y, res = _conformer_fwd(h_cur, get_cols(("cf_in", j)), dw_w_f[j], dw_b_f[j][None, :], cf_ln_g_f[j][None, :],
                                    cf_ln_b_f[j][None, :], get_rows(("cf_out", j)))
        p1 = (mod_row(i, 2), ln_row(ln_g_rows, i, 0), ln_row(ln_b_rows, i, 0), mod_row(i, 4), mod_row(i, 3))
        x_mid, h_mid = _combine_fwd(alpha, x_cur, y, *p1)
        subs.append((x_cur, y, p1, res))
        m_out, res2 = _mlp_fwd(h_mid, get_cols(("ff1", i)), get_rows(("ff2", i)))
        if i + 1 < depth:
            p2 = (mod_row(i, 5), ln_row(ln_g_rows, i, 1), ln_row(ln_b_rows, i, 1), mod_row(i + 1, 1), mod_row(i + 1, 0))
            x_next, h_next = _combine_fwd(alpha, x_mid, m_out, *p2)
            subs.append((x_mid, m_out, p2, res2))
            x_cur, h_cur = x_next, h_next
        else:
            p2 = (mod_row(i, 5), ln_row(ln_g_rows, i, 1), ln_row(ln_b_rows, i, 1))
            last = (x_mid, m_out, p2, res2)

    x_in, y_in, p_last, res_last = last
    dx, dy, (loss_acc, g_gt, g_g, g_b) = _last_fwd_bwd(alpha, x_in, y_in, tgt, *p_last)
    loss = lax.psum(loss_acc[0, 0], ("x", "y", "c"))

    d_mod = [[None] * N_MOD for _ in range(depth)]
    d_ln_g = [[None, None] for _ in range(depth)]
    d_ln_b = [[None, None] for _ in range(depth)]
    d_mod[depth - 1][5], d_ln_g[depth - 1][1], d_ln_b[depth - 1][1] = g_gt, g_g, g_b
    gw = dict(dn=[None] * n_a, cf=[None] * n_b)

    sent = {}

    def send_grads(named, tag):
        parts = [p for _, p in named]
        hs, tok = _xfer_start(parts, [lax.empty(p.shape, p.dtype) for p in parts], True, parts[0], "scatter_start_" + tag)
        for (key, _), hnd in zip(named, hs):
            sent[key] = hnd
        return tok

    def by_rows(g):
        return g.reshape((N_DEV, g.shape[0] // N_DEV, g.shape[1]))

    def send_mlp(i, d_w1, d_w2):
        return send_grads([(("ff1", i), d_w1), (("ff2", i), by_rows(d_w2))], "ff_%d" % i)

    dh, d_w1, d_w2 = _mlp_bwd(res_last, dy)
    pin = send_mlp(depth - 1, d_w1, d_w2)
    for idx in range(len(subs) - 1, -1, -1):
        x_in, y_in, prm, res = subs[idx]
        i, second = idx // 2, idx % 2
        dx, dy, (g_gt, g_g, g_b, g_sc, g_sh) = _combine_bwd(alpha, x_in, y_in, dx, dh, *prm, pin=pin)
        d_mod[i][5 if second else 2], d_ln_g[i][second], d_ln_b[i][second] = g_gt, g_g, g_b
        nxt_i, nxt_base = (i + 1, 0) if second else (i, 3)
        d_mod[nxt_i][nxt_base + 1], d_mod[nxt_i][nxt_base] = g_sc, g_sh
        j = i // 2
        if second:
            dh, d_w1, d_w2 = _mlp_bwd(res, dy)
            pin = send_mlp(i, d_w1, d_w2)
        elif i % 2 == 0:
            def send_dn(d_w_in, d_w_out, j=j):
                d_in = d_w_in[:, :N_DEV * dn_in_cols].reshape(d, N_DEV, dn_in_cols)
                return send_grads([(("dn_in", j), jnp.moveaxis(d_in, 1, 0)), (("dn_out", j), by_rows(d_w_out))],
                                  "dn_%d" % j)

            dh, gw["dn"][j], pin = _deltanet_bwd(res, dy, conv_w_f[j], send_dn)
        else:
            dh, gw["cf"][j] = _conformer_bwd(res, dy, dw_w_f[j], cf_ln_g_f[j][None, :], cf_ln_b_f[j][None, :])
            pin = send_grads([(("cf_in", j), gw["cf"][j]["w_in"]), (("cf_out", j), by_rows(gw["cf"][j]["w_out"]))],
                             "cf_%d" % j)
    grad_x, g_sc, g_sh = _modulate_bwd(xs, dx, dh, mod_row(0, 1), mod_row(0, 0), pin=pin)
    d_mod[0][1], d_mod[0][0] = g_sc, g_sh
    d_mod_full = jnp.concatenate([jnp.concatenate(r, axis=1) for r in d_mod], axis=0)

    stacked = {"dn_w_in": ("dn_in", dn_w_in, m_dn_w_in, v_dn_w_in), "dn_w_out": ("dn_out", dn_w_out, m_dn_w_out, v_dn_w_out),
               "cf_w_in": ("cf_in", cf_w_in, m_cf_w_in, v_cf_w_in), "cf_w_out": ("cf_out", cf_w_out, m_cf_w_out, v_cf_w_out),
               "ff_w1": ("ff1", ff_w1, m_ff_w1, v_ff_w1), "ff_w2": ("ff2", ff_w2, m_ff_w2, v_ff_w2)}
    chains = {key: None for key in stacked}

    def update_layer(i, token):
        mixer = ["dn_w_in", "dn_w_out"] if i % 2 == 0 else ["cf_w_in", "cf_w_out"]
        for key, idx in [("ff_w1", i), ("ff_w2", i)] + [(k, i // 2) for k in mixer]:
            short, w, m, v = stacked[key]
            land = _xfer_wait(sent[(short, idx)], True, token, "scatter_wait_%s_%d" % (short, idx))
            chains[key], token = _adamw_layer(sent[(short, idx)][2], land, me_arr, w, m, v, idx, chains[key],
                                              "adamw_%s_%d" % (key, idx))
        return token

    def stack_rows(lst):
        return jnp.stack(lst, axis=0)

    gs_ln_g = jnp.stack([jnp.concatenate(r, axis=0) for r in d_ln_g], axis=0)
    gs_ln_b = jnp.stack([jnp.concatenate(r, axis=0) for r in d_ln_b], axis=0)
    gs_conv_w = stack_rows([gw["dn"][j]["conv_w"] for j in range(n_a)])
    gs_dw_w = stack_rows([gw["cf"][j]["dw_w"] for j in range(n_b)])
    gs_dw_b = stack_rows([gw["cf"][j]["dw_b"] for j in range(n_b)])
    gs_cf_ln_g = stack_rows([gw["cf"][j]["ln_g"] for j in range(n_b)])
    gs_cf_ln_b = stack_rows([gw["cf"][j]["ln_b"] for j in range(n_b)])
    gs_a_log = stack_rows([_pad_lanes(gw["dn"][j]["a_log"], 0)[0] for j in range(n_a)])
    gs_dt_bias = stack_rows([_pad_lanes(gw["dn"][j]["dt_bias"], 0)[0] for j in range(n_a)])
    gs_norm_w = stack_rows([gw["dn"][j]["norm_w"] for j in range(n_a)])
    small_grads = [gs_ln_g, gs_ln_b, gs_conv_w, gs_dw_w, gs_dw_b, gs_cf_ln_g, gs_cf_ln_b, gs_a_log, gs_dt_bias,
                   gs_norm_w, d_mod_full]
    sg_rows = [_rows(a) for a in small_grads]
    sg_sizes = [a.shape[0] for a in sg_rows]
    sg_packed = _pad_rows(jnp.concatenate(sg_rows, axis=0))
    (sg_handle,), sg_token = _xfer_start([sg_packed], [_landing(sg_packed, me)], False, grad_x, "gather_small_grads_start")
    for i in range(depth - 1, -1, -1):
        sg_token = update_layer(i, sg_token)
    sg_all = _xfer_wait(sg_handle, False, sg_token, "gather_small_grads_wait")
    sg_offs = [0]
    for z in sg_sizes:
        sg_offs.append(sg_offs[-1] + z)

    def sg(i, shape):
        return sg_all[:, sg_offs[i]:sg_offs[i + 1], :].reshape((N_DEV,) + shape)

    dmod_all = sg(10, (depth, N_MOD * d))
    nl = ada_w.shape[2]
    dmod_cols = lax.dynamic_slice_in_dim(dmod_all, me * nl, nl, axis=2)
    g_ada_w = _ada_bwd(cond_all, jnp.moveaxis(dmod_cols, 0, 1))

    outs = {}

    def run_adamw(key, parts, w, m, v):
        shp = w.shape
        as3 = lambda t: t.reshape((-1,) + shp[-2:]) if t.ndim >= 3 else t.reshape((1,) + shp)
        parts3 = parts.reshape((parts.shape[0],) + as3(w).shape)
        res = _adamw(parts3, as3(w), as3(m), as3(v), "adamw_" + key)
        outs[key] = tuple(r.reshape(shp) for r in res)

    run_adamw("ada_w", g_ada_w[None], ada_w, m_ada_w, v_ada_w)

    cgroups = 3 * heads // N_DEV
    n_sharded = 7

    sg_groups = [1, 1, cgroups, 1, 1, 1, 1] + [None] * (len(sg_sizes) - n_sharded)
    sp_offs = [0]
    for n, z in enumerate(sg_sizes):
        sp_offs.append(sp_offs[-1] + (z // N_DEV if n < n_sharded else z))

    def pad_heads(t):
        return jnp.pad(t, ((0, 0), (0, LANES - heads)))

    def pack_state(ln_g_, ln_b_, conv_w_, dw_w_, dw_b_, cln_g_, cln_b_, a_log_, dt_b_, norm_w_, ada_b_):
        rows = [_rows(ln_g_), _rows(ln_b_), _rows(conv_w_), _rows(dw_w_), _rows(dw_b_), _rows(cln_g_), _rows(cln_b_),
                pad_heads(a_log_), pad_heads(dt_b_), norm_w_, _rows(ada_b_)]
        return _pad_rows(jnp.concatenate(rows, axis=0))

    w_s = pack_state(ln_g, ln_b, dn_conv_w, cf_dw_w, cf_dw_b, cf_ln_g, cf_ln_b, dn_a_log, dn_dt_bias, dn_norm_w, ada_b)
    m_s = pack_state(m_ln_g, m_ln_b, m_dn_conv_w, m_cf_dw_w, m_cf_dw_b, m_cf_ln_g, m_cf_ln_b, m_dn_a_log,
                     m_dn_dt_bias, m_dn_norm_w, m_ada_b)
    v_s = pack_state(v_ln_g, v_ln_b, v_dn_conv_w, v_cf_dw_w, v_cf_dw_b, v_cf_ln_g, v_cf_ln_b, v_dn_a_log,
                     v_dn_dt_bias, v_dn_norm_w, v_ada_b)
    small_keys = ["ln_g", "ln_b", "dn_conv_w", "cf_dw_w", "cf_dw_b", "cf_ln_g", "cf_ln_b", "dn_a_log", "dn_dt_bias",
                  "dn_norm_w", "ada_b"]
    small_shapes = [ln_g.shape, ln_b.shape, dn_conv_w.shape, cf_dw_w.shape, cf_dw_b.shape, cf_ln_g.shape,
                    cf_ln_b.shape, dn_a_log.shape, dn_dt_bias.shape, dn_norm_w.shape, ada_b.shape]
    res_s = _adamw_small(sg_all, me_arr, w_s, m_s, v_s, small_shapes, sp_offs, sg_offs, sg_groups, "adamw_small")
    for n, key in enumerate(small_keys):
        outs[key] = tuple(res_s[4 * n:4 * n + 4])

    for key in stacked:
        outs[key] = tuple(chains[key])

    order = ["ada_w", "ada_b", "ln_g", "ln_b", "dn_w_in", "dn_conv_w", "dn_a_log", "dn_dt_bias", "dn_norm_w",
             "dn_w_out", "cf_w_in", "cf_dw_w", "cf_dw_b", "cf_ln_g", "cf_ln_b", "cf_w_out", "ff_w1", "ff_w2"]
    result = [loss, grad_x[None]]
    for part in range(4):
        result += [outs[k][part] for k in order]
    return tuple(result)
```
